```python
import math
import jax
import jax.numpy as jnp
from jax import lax
import numpy as np

D_MODEL = 1024
BATCH = 8
SEQ = 2048
DEPTH = 2
DEC_BATCH = 128
DEC_SEQ = 1
PAST_LEN = 8192
PAGE_SIZE = 128

N_A_LAYERS = DEPTH // 2
N_B_LAYERS = DEPTH - N_A_LAYERS
N_DENSE = (DEPTH + 1) // 2
N_MOE = DEPTH // 2
SSM_GROUP = 16
SSM_GROUPS = D_MODEL // SSM_GROUP
SSM_STATE = 64
N_HEADS = 16
N_KV_HEADS = 4
HEAD_DIM = 64
Q_PER_KV = N_HEADS // N_KV_HEADS
KV_WIDTH = N_KV_HEADS * HEAD_DIM
WINDOW = 128
NUM_BUCKETS = 32
MAX_DISTANCE = 128
ATTN_SCALE = 1.0 / math.sqrt(HEAD_DIM)
NEG_INF = -1e30
D_FF = 2816
N_EXPERTS = 8
TOP_K = 2
D_FF_EXPERT = 3584
EPS = 1e-6

kernel_name = 'yoco_s5_swa_sink_step'


def rmsnorm(x, g):
    xf = x.astype(jnp.float32)
    y = xf * lax.rsqrt(jnp.mean(xf * xf, axis=-1, keepdims=True) + EPS)
    return (y * g.astype(jnp.float32)).astype(x.dtype)


def swiglu(h, w_gu, w_down):
    a, b = jnp.split(h @ w_gu, 2, axis=-1)
    return (jax.nn.silu(a) * b) @ w_down


def moe_swiglu(h, w_router, w_gu, w_down):
    logits = jnp.einsum('btd,de->bte', h, w_router, preferred_element_type=jnp.float32)
    top_v, top_i = lax.top_k(logits, TOP_K)
    gates = jax.nn.softmax(top_v, axis=-1)
    comb = jnp.einsum('btk,btke->bte', gates, jax.nn.one_hot(top_i, N_EXPERTS, dtype=jnp.float32))
    out = jnp.zeros(h.shape, jnp.float32)
    for e in range(N_EXPERTS):
        out = out + comb[..., e:e + 1] * swiglu(h, w_gu[e], w_down[e]).astype(jnp.float32)
    return out.astype(h.dtype)


def _ssm_combine(left, right):
    a_l, b_l = left
    a_r, b_r = right
    return a_r * a_l, a_r * b_l + b_r


def s5_mixer(u, h0, a_re, a_im, log_dt, b_re, b_im, c_re, c_im, d_skip, w_glu):
    bsz, t_len, _ = u.shape
    lam = lax.complex(a_re.astype(jnp.float32), a_im.astype(jnp.float32))
    dt = jnp.exp(log_dt.astype(jnp.float32))[:, None]
    lam_bar = jnp.exp(lam * dt)
    b_mat = lax.complex(b_re.astype(jnp.float32), b_im.astype(jnp.float32))
    b_bar = ((lam_bar - 1.0) / lam)[..., None] * b_mat
    uf = u.astype(jnp.float32).reshape(bsz, t_len, SSM_GROUPS, SSM_GROUP)
    bu = lax.complex(jnp.einsum('btgh,gph->btgp', uf, jnp.real(b_bar)),
                     jnp.einsum('btgh,gph->btgp', uf, jnp.imag(b_bar)))
    if h0 is not None:
        bu = bu.at[:, 0].add(lam_bar * h0)
    a = jnp.broadcast_to(lam_bar, (1, t_len, SSM_GROUPS, SSM_STATE))
    _, h = lax.associative_scan(_ssm_combine, (a, bu), axis=1)
    y = (jnp.einsum('btgp,ghp->btgh', jnp.real(h), c_re.astype(jnp.float32))
         - jnp.einsum('btgp,ghp->btgh', jnp.imag(h), c_im.astype(jnp.float32)))
    y = y.reshape(bsz, t_len, D_MODEL) + d_skip.astype(jnp.float32) * u.astype(jnp.float32)
    z = jax.nn.gelu(y).astype(u.dtype)
    gl = z @ w_glu
    out = gl[..., :D_MODEL] * jax.nn.sigmoid(gl[..., D_MODEL:])
    return out, h[:, -1]


def t5_bucket(dist):
    max_exact = NUM_BUCKETS // 2
    d = jnp.maximum(dist, 0)
    large = max_exact + (jnp.log(jnp.maximum(d, 1).astype(jnp.float32) / max_exact)
                         / math.log(MAX_DISTANCE / max_exact) * (NUM_BUCKETS - max_exact)).astype(jnp.int32)
    large = jnp.minimum(large, NUM_BUCKETS - 1)
    return jnp.where(d < max_exact, d, large)


def sink_attention(q, k, v, dist, valid, sinks, rel_bias):
    n_q, n_k = dist.shape
    bias = rel_bias.astype(jnp.float32)[t5_bucket(dist)]
    bias = jnp.transpose(bias, (2, 0, 1)).reshape(N_KV_HEADS, Q_PER_KV, n_q, n_k)
    s = jnp.einsum('nqgrd,nkgd->ngrqk', q, k, preferred_element_type=jnp.float32) * ATTN_SCALE + bias
    s = jnp.where(valid, s, NEG_INF)
    sink = sinks.astype(jnp.float32).reshape(N_KV_HEADS, Q_PER_KV)[None, :, :, None, None]
    m = jnp.maximum(jnp.max(s, axis=-1, keepdims=True), sink)
    p = jnp.exp(s - m)
    probs = p / (jnp.sum(p, axis=-1, keepdims=True) + jnp.exp(sink - m))
    return jnp.einsum('ngrqk,nkgd->nqgrd', probs.astype(v.dtype), v)


def window_attn_prompt(q, k, v, sinks, rel_bias):
    bsz, t_len = q.shape[:2]
    nb = t_len // WINDOW
    qb = q.reshape(bsz * nb, WINDOW, N_KV_HEADS, Q_PER_KV, HEAD_DIM)

    def band(z):
        zb = z.reshape(bsz, nb, WINDOW, N_KV_HEADS, HEAD_DIM)
        zprev = jnp.concatenate([jnp.zeros_like(zb[:, :1]), zb[:, :-1]], axis=1)
        return jnp.concatenate([zprev, zb], axis=2).reshape(bsz * nb, 2 * WINDOW, N_KV_HEADS, HEAD_DIM)

    qi = jnp.arange(WINDOW)[:, None]
    kj = jnp.arange(2 * WINDOW)[None, :]
    dist = qi + WINDOW - kj
    in_band = (dist >= 0) & (dist < WINDOW)
    key_ok = (jnp.arange(nb)[:, None] > 0) | (jnp.arange(2 * WINDOW)[None, :] >= WINDOW)
    valid = in_band[None] & key_ok[:, None, :]
    valid = jnp.broadcast_to(valid[None], (bsz, nb, WINDOW, 2 * WINDOW)).reshape(bsz * nb, 1, 1, WINDOW, 2 * WINDOW)
    o = sink_attention(qb, band(k), band(v), dist, valid, sinks, rel_bias)
    return o.reshape(bsz, t_len, N_HEADS * HEAD_DIM)


def window_attn_sample(q, k_all, v_all, sinks, rel_bias):
    bsz, s_len = q.shape[:2]
    qi = jnp.arange(s_len)[:, None]
    kj = jnp.arange(WINDOW + s_len)[None, :]
    dist = qi + WINDOW - kj
    valid = ((dist >= 0) & (dist < WINDOW))[None, None, None]
    o = sink_attention(q, k_all, v_all, dist, valid, sinks, rel_bias)
    return o.reshape(bsz, s_len, N_HEADS * HEAD_DIM)


def run_trunk(x, ssm_h0, k_win, v_win, g_mix, g_ffn, g_kv, g_final,
              ssm_a_re, ssm_a_im, ssm_log_dt, ssm_b_re, ssm_b_im, ssm_c_re, ssm_c_im, ssm_d, w_glu,
              w_kv, w_q, w_o, attn_sinks, rel_bias,
              w_ffn_gate_up, w_ffn_down, w_router, w_exp_gate_up, w_exp_down):
    bsz, t_len, _ = x.shape
    ssm_states = []
    k_all = v_all = new_k_win = new_v_win = None
    for l in range(DEPTH):
        if l < N_A_LAYERS:
            h0 = None if ssm_h0 is None else ssm_h0[l]
            y, h_last = s5_mixer(rmsnorm(x, g_mix[l]), h0, ssm_a_re[l], ssm_a_im[l], ssm_log_dt[l],
                                 ssm_b_re[l], ssm_b_im[l], ssm_c_re[l], ssm_c_im[l], ssm_d[l], w_glu[l])
            ssm_states.append(h_last)
            x = x + y
        else:
            if k_all is None:
                kv = rmsnorm(x, g_kv) @ w_kv
                k_new = kv[..., :KV_WIDTH].reshape(bsz, t_len, N_KV_HEADS, HEAD_DIM)
                v_new = kv[..., KV_WIDTH:].reshape(bsz, t_len, N_KV_HEADS, HEAD_DIM)
                if k_win is None:
                    k_all, v_all = k_new, v_new
                else:
                    k_all = jnp.concatenate([k_win, k_new], axis=1)
                    v_all = jnp.concatenate([v_win, v_new], axis=1)
                new_k_win = k_all[:, -WINDOW:]
                new_v_win = v_all[:, -WINDOW:]
            j = l - N_A_LAYERS
            q = (rmsnorm(x, g_mix[l]) @ w_q[j]).reshape(bsz, t_len, N_KV_HEADS, Q_PER_KV, HEAD_DIM)
            if k_win is None:
                o = window_attn_prompt(q, k_all, v_all, attn_sinks[j], rel_bias)
            else:
                o = window_attn_sample(q, k_all, v_all, attn_sinks[j], rel_bias)
            x = x + o @ w_o[j]
        h = rmsnorm(x, g_ffn[l])
        if l % 2 == 0:
            x = x + swiglu(h, w_ffn_gate_up[l // 2], w_ffn_down[l // 2])
        else:
            x = x + moe_swiglu(h, w_router[l // 2], w_exp_gate_up[l // 2], w_exp_down[l // 2])
    return rmsnorm(x, g_final), jnp.stack(ssm_states, axis=0), new_k_win, new_v_win


def setup_inputs(seed: int = 0) -> dict:
    key = jax.random.key(seed)
    ks = iter(jax.random.split(key, 40))

    def nrm(shape, scale):
        return jax.random.normal(next(ks), shape, jnp.float32) * scale

    d = D_MODEL
    a_im_base = jnp.pi * jnp.arange(SSM_STATE, dtype=jnp.float32)[None, None, :]
    return {
        'x_prompt': nrm((BATCH, SEQ, d), 1.0),
        'x_sample': nrm((DEC_BATCH, DEC_SEQ, d), 1.0),
        'state_ssm_re': nrm((N_A_LAYERS, DEC_BATCH, SSM_GROUPS, SSM_STATE), 0.1),
        'state_ssm_im': nrm((N_A_LAYERS, DEC_BATCH, SSM_GROUPS, SSM_STATE), 0.1),
        'cache_k_win': nrm((DEC_BATCH, WINDOW, N_KV_HEADS, HEAD_DIM), 1.0),
        'cache_v_win': nrm((DEC_BATCH, WINDOW, N_KV_HEADS, HEAD_DIM), 1.0),
        'g_mix': 1.0 + nrm((DEPTH, d), 0.05),
        'g_ffn': 1.0 + nrm((DEPTH, d), 0.05),
        'g_kv': 1.0 + nrm((d,), 0.05),
        'g_final': 1.0 + nrm((d,), 0.05),
        'ssm_a_re': -0.5 + nrm((N_A_LAYERS, SSM_GROUPS, SSM_STATE), 0.01),
        'ssm_a_im': a_im_base + nrm((N_A_LAYERS, SSM_GROUPS, SSM_STATE), 0.01),
        'ssm_log_dt': jax.random.uniform(next(ks), (N_A_LAYERS, SSM_GROUPS), jnp.float32,
                                         minval=math.log(1e-3), maxval=math.log(1e-1)),
        'ssm_b_re': nrm((N_A_LAYERS, SSM_GROUPS, SSM_STATE, SSM_GROUP), (2 * SSM_GROUP) ** -0.5),
        'ssm_b_im': nrm((N_A_LAYERS, SSM_GROUPS, SSM_STATE, SSM_GROUP), (2 * SSM_GROUP) ** -0.5),
        'ssm_c_re': nrm((N_A_LAYERS, SSM_GROUPS, SSM_GROUP, SSM_STATE), 0.3),
        'ssm_c_im': nrm((N_A_LAYERS, SSM_GROUPS, SSM_GROUP, SSM_STATE), 0.3),
        'ssm_d': 1.0 + nrm((N_A_LAYERS, d), 0.1),
        'w_glu': nrm((N_A_LAYERS, d, 2 * d), d ** -0.5),
        'w_kv': nrm((d, 2 * KV_WIDTH), d ** -0.5),
        'w_q': nrm((N_B_LAYERS, d, N_HEADS * HEAD_DIM), d ** -0.5),
        'w_o': nrm((N_B_LAYERS, N_HEADS * HEAD_DIM, d), (N_HEADS * HEAD_DIM) ** -0.5),
        'attn_sinks': nrm((N_B_LAYERS, N_HEADS), 1.0),
        'rel_bias': nrm((NUM_BUCKETS, N_HEADS), 0.5),
        'w_ffn_gate_up': nrm((N_DENSE, d, 2 * D_FF), d ** -0.5),
        'w_ffn_down': nrm((N_DENSE, D_FF, d), D_FF ** -0.5),
        'w_router': nrm((N_MOE, d, N_EXPERTS), d ** -0.5),
        'w_exp_gate_up': nrm((N_MOE, N_EXPERTS, d, 2 * D_FF_EXPERT), d ** -0.5),
        'w_exp_down': nrm((N_MOE, N_EXPERTS, D_FF_EXPERT, d), D_FF_EXPERT ** -0.5),
    }


def reference(x_prompt, x_sample, state_ssm_re, state_ssm_im, cache_k_win, cache_v_win,
              g_mix, g_ffn, g_kv, g_final,
              ssm_a_re, ssm_a_im, ssm_log_dt, ssm_b_re, ssm_b_im, ssm_c_re, ssm_c_im, ssm_d, w_glu,
              w_kv, w_q, w_o, attn_sinks, rel_bias,
              w_ffn_gate_up, w_ffn_down, w_router, w_exp_gate_up, w_exp_down):
    weights = (g_mix, g_ffn, g_kv, g_final,
               ssm_a_re, ssm_a_im, ssm_log_dt, ssm_b_re, ssm_b_im, ssm_c_re, ssm_c_im, ssm_d, w_glu,
               w_kv, w_q, w_o, attn_sinks, rel_bias,
               w_ffn_gate_up, w_ffn_down, w_router, w_exp_gate_up, w_exp_down)
    y_prompt, h_p, k_win_p, v_win_p = run_trunk(x_prompt, None, None, None, *weights)
    h0 = lax.complex(state_ssm_re.astype(jnp.float32), state_ssm_im.astype(jnp.float32))
    y_sample, h_s, k_win_s, v_win_s = run_trunk(x_sample, h0, cache_k_win, cache_v_win, *weights)
    return (y_prompt, y_sample, jnp.real(h_p), jnp.imag(h_p), k_win_p, v_win_p,
            jnp.real(h_s), jnp.imag(h_s), k_win_s, v_win_s)
```

```python
import functools
import math

import numpy as np
import jax
import jax.numpy as jnp
from jax import lax
from jax.experimental import pallas as pl
from jax.experimental.pallas import tpu as pltpu

F32 = jnp.float32
BF16 = jnp.bfloat16

EPS = 1e-6
NEG_INF = -1e30
TOP_K = 2
MAX_DISTANCE = 128
MXU_DIM = 256
LANES = 128
VMEM_LIMIT_BYTES = 56 * 1024 * 1024


def _dot(a, b):
    return jnp.dot(a, b, preferred_element_type=F32)


def _rms(x, g):
    return x * lax.rsqrt(jnp.mean(x * x, axis=-1, keepdims=True) + EPS) * g


def _const_spec(shape):
    nd = len(shape)
    return pl.BlockSpec(shape, lambda *_: (0,) * nd)


def _params(*sem):
    return pltpu.CompilerParams(dimension_semantics=sem, vmem_limit_bytes=VMEM_LIMIT_BYTES)


def _zoh_kernel(a_re_ref, a_im_ref, log_dt_ref, b_re_ref, b_im_ref,
                lam_re_ref, lam_im_ref, bb_re_ref, bb_im_ref):
    a_re = a_re_ref[...]
    a_im = a_im_ref[...]
    dt = jnp.exp(log_dt_ref[...])
    mag = jnp.exp(a_re * dt)
    lr = mag * jnp.cos(a_im * dt)
    li = mag * jnp.sin(a_im * dt)
    lam_re_ref[...] = lr
    lam_im_ref[...] = li
    nr = lr - 1.0
    den = a_re * a_re + a_im * a_im
    qr = (nr * a_re + li * a_im) / den
    qi = (li * a_re - nr * a_im) / den
    b_re = b_re_ref[...]
    b_im = b_im_ref[...]
    bb_re_ref[...] = qr * b_re - qi * b_im
    bb_im_ref[...] = qr * b_im + qi * b_re


def _zoh(a_re, a_im, log_dt, b_re, b_im):
    g, p = a_re.shape
    h = b_re.shape[-1]
    gp = g * p
    row = jax.ShapeDtypeStruct((1, gp), F32)
    mat = jax.ShapeDtypeStruct((h, gp), F32)
    return pl.pallas_call(
        _zoh_kernel,
        out_shape=(row, row, mat, mat),
        name="s5_zoh",
    )(a_re.reshape(1, gp), a_im.reshape(1, gp), jnp.repeat(log_dt, p).reshape(1, gp),
      b_re.transpose(2, 0, 1).reshape(h, gp), b_im.transpose(2, 0, 1).reshape(h, gp))


def _blockdiag_in(bb, g, p, gpb):
    h = bb.shape[0]
    nkb = g // gpb
    bb4 = bb.reshape(h, nkb, gpb, p).transpose(1, 0, 2, 3)
    eye = jnp.eye(gpb, dtype=bb.dtype)
    w = bb4[:, None, :, :, :] * eye[None, :, None, :, None]
    return w.reshape(nkb, gpb * h, gpb * p).astype(BF16)


def _blockdiag_out(c, gpb):
    g, h, p = c.shape
    nkb = g // gpb
    c4 = c.reshape(nkb, gpb, h, p).transpose(0, 1, 3, 2)
    eye = jnp.eye(gpb, dtype=c.dtype)
    w = c4[:, :, :, None, :] * eye[None, :, None, :, None]
    return w.reshape(nkb, gpb * p, gpb * h).astype(BF16)


def _ssm_in_proj(ub, wb_re_ref, wb_im_ref, bu_re, bu_im):
    nkb, ublk, sblk = wb_re_ref.shape
    for kb in range(nkb):
        ukb = ub[:, kb * ublk:(kb + 1) * ublk]
        bu_re[:, kb * sblk:(kb + 1) * sblk] = _dot(ukb, wb_re_ref[kb])
        bu_im[:, kb * sblk:(kb + 1) * sblk] = _dot(ukb, wb_im_ref[kb])


def _ssm_tail(x, u, h_re, h_im, wc_re_ref, wc_imn_ref, d_ref, wglu_ref):
    nkb, sblk, _ = wc_re_ref.shape
    ys = []
    for kb in range(nkb):
        hr = h_re[:, kb * sblk:(kb + 1) * sblk].astype(BF16)
        hi = h_im[:, kb * sblk:(kb + 1) * sblk].astype(BF16)
        ys.append(_dot(hr, wc_re_ref[kb]) + _dot(hi, wc_imn_ref[kb]))
    y = jnp.concatenate(ys, axis=1) + d_ref[...] * u
    z = jax.nn.gelu(y).astype(BF16)
    gl = _dot(z, wglu_ref[...])
    d = x.shape[1]
    return x + gl[:, :d] * jax.nn.sigmoid(gl[:, d:])


def _ssm_prompt_kernel(x_ref, g_ref, lam_re_ref, lam_im_ref, wb_re_ref, wb_im_ref,
                       wc_re_ref, wc_imn_ref, d_ref, wglu_ref,
                       out_ref, st_re_ref, st_im_ref, bu_re, bu_im, *, lc, bsz, lane_chunk):
    @pl.when(pl.program_id(0) == 0)
    def _():
        st_re_ref[...] = jnp.zeros_like(st_re_ref)
        st_im_ref[...] = jnp.zeros_like(st_im_ref)

    x = x_ref[...]
    u = _rms(x, g_ref[...])
    _ssm_in_proj(u.astype(BF16), wb_re_ref, wb_im_ref, bu_re, bu_im)

    gp = bu_re.shape[1]
    for c0 in range(0, gp, lane_chunk):
        sl = slice(c0, c0 + lane_chunk)
        lre = jnp.broadcast_to(lam_re_ref[:, sl], (bsz, lane_chunk))
        lim = jnp.broadcast_to(lam_im_ref[:, sl], (bsz, lane_chunk))

        def step(t, carry):
            hr, hi = carry
            rows = pl.ds(pl.multiple_of(t * bsz, bsz), bsz)
            nr = lre * hr - lim * hi + bu_re[rows, sl]
            ni = lre * hi + lim * hr + bu_im[rows, sl]
            bu_re[rows, sl] = nr
            bu_im[rows, sl] = ni
            return nr, ni

        hr, hi = lax.fori_loop(0, lc, step, (st_re_ref[:, sl], st_im_ref[:, sl]), unroll=8)
        st_re_ref[:, sl] = hr
        st_im_ref[:, sl] = hi

    out_ref[...] = _ssm_tail(x, u, bu_re, bu_im, wc_re_ref, wc_imn_ref, d_ref, wglu_ref)


def _ssm_sample_kernel(x_ref, g_ref, lam_re_ref, lam_im_ref, h0_re_ref, h0_im_ref, wb_re_ref, wb_im_ref,
                       wc_re_ref, wc_imn_ref, d_ref, wglu_ref,
                       out_ref, h_re_ref, h_im_ref):
    x = x_ref[...]
    u = _rms(x, g_ref[...])
    _ssm_in_proj(u.astype(BF16), wb_re_ref, wb_im_ref, h_re_ref, h_im_ref)
    lre = lam_re_ref[...]
    lim = lam_im_ref[...]
    h0r = h0_re_ref[...]
    h0i = h0_im_ref[...]
    h_re_ref[...] = lre * h0r - lim * h0i + h_re_ref[...]
    h_im_ref[...] = lre * h0i + lim * h0r + h_im_ref[...]
    out_ref[...] = _ssm_tail(x, u, h_re_ref, h_im_ref, wc_re_ref, wc_imn_ref, d_ref, wglu_ref)


def _ssm_prompt(x_tb, bsz, g, lam_re, lam_im, wb_re, wb_im, wc_re, wc_imn, d_skip, wglu, *, lc):
    rows, d = x_tb.shape
    gp = lam_re.shape[1]
    r = lc * bsz
    kern = functools.partial(_ssm_prompt_kernel, lc=lc, bsz=bsz, lane_chunk=8 * LANES)
    return pl.pallas_call(
        kern,
        grid=(rows // r,),
        in_specs=[pl.BlockSpec((r, d), lambda c: (c, 0)),
                  _const_spec(g.shape), _const_spec(lam_re.shape), _const_spec(lam_im.shape),
                  _const_spec(wb_re.shape), _const_spec(wb_im.shape),
                  _const_spec(wc_re.shape), _const_spec(wc_imn.shape),
                  _const_spec(d_skip.shape), _const_spec(wglu.shape)],
        out_specs=[pl.BlockSpec((r, d), lambda c: (c, 0)),
                   _const_spec((bsz, gp)), _const_spec((bsz, gp))],
        out_shape=[jax.ShapeDtypeStruct((rows, d), F32),
                   jax.ShapeDtypeStruct((bsz, gp), F32), jax.ShapeDtypeStruct((bsz, gp), F32)],
        scratch_shapes=[pltpu.VMEM((r, gp), F32), pltpu.VMEM((r, gp), F32)],
        compiler_params=_params("arbitrary"),
        name="s5_prompt",
    )(x_tb, g, lam_re, lam_im, wb_re, wb_im, wc_re, wc_imn, d_skip, wglu)


def _ssm_sample(x, g, lam_re, lam_im, h0_re, h0_im, wb_re, wb_im, wc_re, wc_imn, d_skip, wglu):
    n, d = x.shape
    gp = lam_re.shape[1]
    return pl.pallas_call(
        _ssm_sample_kernel,
        out_shape=[jax.ShapeDtypeStruct((n, d), F32),
                   jax.ShapeDtypeStruct((n, gp), F32), jax.ShapeDtypeStruct((n, gp), F32)],
        compiler_params=_params(),
        name="s5_sample",
    )(x, g, lam_re, lam_im, h0_re, h0_im, wb_re, wb_im, wc_re, wc_imn, d_skip, wglu)


def _ffn_chunks(d_ff):
    step = 3 * MXU_DIM
    return [(c, min(c + step, d_ff)) for c in range(0, d_ff, step)]


def _ffn_kernel(x_ref, gffn_ref, wgu_ref, wd_ref, gkv_ref, wkv_ref, gq_ref, wq_ref,
                x2_ref, kv_ref, q_ref):
    x = x_ref[...]
    hb = _rms(x, gffn_ref[...]).astype(BF16)
    d_ff = wd_ref.shape[0]
    acc = None
    for c0, c1 in _ffn_chunks(d_ff):
        a = _dot(hb, wgu_ref[:, c0:c1])
        b = _dot(hb, wgu_ref[:, d_ff + c0:d_ff + c1])
        part = _dot((jax.nn.silu(a) * b).astype(BF16), wd_ref[c0:c1, :])
        acc = part if acc is None else acc + part
    x2 = x + acc
    x2_ref[...] = x2
    kv_ref[...] = _dot(_rms(x2, gkv_ref[...]).astype(BF16), wkv_ref[...])
    q_ref[...] = _dot(_rms(x2, gq_ref[...]).astype(BF16), wq_ref[...]).astype(BF16)


def _ffn(x, gffn, wgu, wd, gkv, wkv, gq, wq, *, tm):
    rows, d = x.shape
    kvw = wkv.shape[1]
    nq = wq.shape[1]
    return pl.pallas_call(
        _ffn_kernel,
        grid=(rows // tm,),
        in_specs=[pl.BlockSpec((tm, d), lambda i: (i, 0)),
                  _const_spec(gffn.shape), _const_spec(wgu.shape), _const_spec(wd.shape),
                  _const_spec(gkv.shape), _const_spec(wkv.shape),
                  _const_spec(gq.shape), _const_spec(wq.shape)],
        out_specs=[pl.BlockSpec((tm, d), lambda i: (i, 0)),
                   pl.BlockSpec((tm, kvw), lambda i: (i, 0)),
                   pl.BlockSpec((tm, nq), lambda i: (i, 0))],
        out_shape=[jax.ShapeDtypeStruct((rows, d), F32),
                   jax.ShapeDtypeStruct((rows, kvw), F32),
                   jax.ShapeDtypeStruct((rows, nq), BF16)],
        compiler_params=_params("arbitrary"),
        name="ffn_kv_q",
    )(x, gffn, wgu, wd, gkv, wkv, gq, wq)


def _t5_bucket(dist, num_buckets):
    max_exact = num_buckets // 2
    d = jnp.maximum(dist, 0)
    large = max_exact + (jnp.log(jnp.maximum(d, 1).astype(F32) / max_exact)
                         / math.log(MAX_DISTANCE / max_exact) * (num_buckets - max_exact)).astype(jnp.int32)
    large = jnp.minimum(large, num_buckets - 1)
    return jnp.where(d < max_exact, d, large)


def _bias_kernel(bm_ref, bs_ref, rb_ref, bias_ref, bias_s_ref, *, kvh, rep):
    nb, nh = rb_ref.shape
    bm = bm_ref[...]
    bs = bs_ref[...]

    def lookup(buckets, h):
        def body(k, acc):
            return jnp.where(buckets == k, rb_ref[k, h], acc)
        return lax.fori_loop(0, nb, body, jnp.zeros(buckets.shape, F32))

    for h in range(nh):
        bias_ref[h] = lookup(bm, h)
    for g in range(kvh):
        for r in range(rep):
            bias_s_ref[g, r:r + 1, :] = lookup(bs, g * rep + r)


def _bias_tables(rel_bias, window, kvh, rep):
    nb, nh = rel_bias.shape
    qi = jnp.arange(window)[:, None]
    kj = jnp.arange(window)[None, :]
    bm = _t5_bucket((qi - kj) % window, nb).astype(jnp.int32)
    bs = _t5_bucket(window - 1 - kj, nb).astype(jnp.int32)
    kern = functools.partial(_bias_kernel, kvh=kvh, rep=rep)
    return pl.pallas_call(
        kern,
        in_specs=[pl.BlockSpec(memory_space=pltpu.VMEM), pl.BlockSpec(memory_space=pltpu.VMEM),
                  pl.BlockSpec(memory_space=pltpu.SMEM)],
        out_shape=[jax.ShapeDtypeStruct((nh, window, window), F32),
                   jax.ShapeDtypeStruct((kvh, rep, window), F32)],
        name="t5_bias",
    )(bm, bs, rel_bias)


def _attn_prompt_kernel(q_ref, kvc_ref, kvp_ref, bias_ref, sink_ref, o_ref, *, kvh, rep, hd, scale):
    w = q_ref.shape[0]
    kvw = kvh * hd
    first = pl.program_id(1) == 0
    q = q_ref[...]
    kvc = kvc_ref[...]
    kvp = kvp_ref[...]
    lane = lax.broadcasted_iota(jnp.int32, (1, LANES), 1)
    qi = lax.broadcasted_iota(jnp.int32, (w, w), 0)
    kj = lax.broadcasted_iota(jnp.int32, (w, w), 1)
    upper = jnp.concatenate([kj > qi] * rep, axis=0)
    masked = jnp.logical_and(upper, first)
    heads_per_blk = LANES // hd
    o_blks = []
    for p in range(kvw // LANES):
        cs = slice(p * LANES, (p + 1) * LANES)
        kc, kp = kvc[:, cs], kvp[:, cs]
        vs = slice(kvw + p * LANES, kvw + (p + 1) * LANES)
        vb = jnp.concatenate([kvp[:, vs], kvc[:, vs]], axis=0).astype(BF16)
        qg = jnp.concatenate([q[:, r * kvw + p * LANES:r * kvw + (p + 1) * LANES] for r in range(rep)], axis=0)
        o_blk = None
        for half in range(heads_per_blk):
            g = p * heads_per_blk + half
            lmask = jnp.logical_and(lane >= half * hd, lane < (half + 1) * hd)
            kb = jnp.concatenate([jnp.where(lmask, kp, 0.0), jnp.where(lmask, kc, 0.0)], axis=0).astype(BF16)
            s = lax.dot_general(qg, kb, (((1,), (1,)), ((), ())), preferred_element_type=F32)
            bias = bias_ref[g * rep:(g + 1) * rep].reshape(rep * w, w)
            sc = jnp.where(upper, s[:, :w], s[:, w:]) * scale + bias
            sc = jnp.where(masked, NEG_INF, sc)
            sink = jnp.concatenate([jnp.full((w, 1), sink_ref[g * rep + r], F32) for r in range(rep)], axis=0)
            m = jnp.maximum(jnp.max(sc, axis=-1, keepdims=True), sink)
            pe = jnp.exp(sc - m)
            denom = jnp.sum(pe, axis=-1, keepdims=True) + jnp.exp(sink - m)
            pcat = jnp.concatenate([jnp.where(upper, pe, 0.0), jnp.where(upper, 0.0, pe)], axis=1).astype(BF16)
            og = _dot(pcat, vb) / denom
            o_blk = og if o_blk is None else jnp.where(lmask, og, o_blk)
        o_blks.append(o_blk)
    cols = []
    for r in range(rep):
        for p in range(kvw // LANES):
            cols.append(o_blks[p][r * w:(r + 1) * w])
    o_ref[...] = jnp.concatenate(cols, axis=1).astype(BF16)


def _attn_prompt(q2, kv2, bias, sinks, *, bsz, nblk, window, kvh, rep, hd):
    nq = kvh * rep * hd
    kvw2 = 2 * kvh * hd
    rows = q2.shape[0]
    kern = functools.partial(_attn_prompt_kernel, kvh=kvh, rep=rep, hd=hd, scale=1.0 / math.sqrt(hd))
    return pl.pallas_call(
        kern,
        grid=(bsz, nblk),
        in_specs=[pl.BlockSpec((window, nq), lambda b, i: (i, b)),
                  pl.BlockSpec((window, kvw2), lambda b, i: (i, b)),
                  pl.BlockSpec((window, kvw2), lambda b, i: (jnp.maximum(i - 1, 0), b)),
                  _const_spec(bias.shape),
                  pl.BlockSpec(memory_space=pltpu.SMEM)],
        out_specs=pl.BlockSpec((window, nq), lambda b, i: (i, b)),
        out_shape=jax.ShapeDtypeStruct((rows, bsz * nq), BF16),
        compiler_params=_params("arbitrary", "arbitrary"),
        name="swa_prompt",
    )(q2, kv2, kv2, bias, sinks)


def _attn_sample_kernel(q_ref, kv_ref, ck_ref, cv_ref, bias_ref, sink_ref, o_ref, nk_ref, nv_ref,
                        *, kvh, rep, hd, scale):
    nb, w, kvw = ck_ref.shape
    kv = kv_ref[...]
    lane = lax.broadcasted_iota(jnp.int32, (nb, rep, kvw), 2)

    def shifted(c_ref, n_ref, new):
        flat = c_ref[...].reshape(nb * w, kvw)
        n_ref[...] = pltpu.roll(flat, nb * w - 1, axis=0).reshape(nb, w, kvw)
        n_ref[:, w - 1:w, :] = new
        return n_ref[...].astype(BF16)

    nkb = shifted(ck_ref, nk_ref, kv[:, :, :kvw])
    nvb = shifted(cv_ref, nv_ref, kv[:, :, kvw:])
    q = q_ref[...].astype(F32)
    o = jnp.zeros((nb, rep, kvw), F32)
    for g in range(kvh):
        lmask = jnp.logical_and(lane >= g * hd, lane < (g + 1) * hd)
        qg = jnp.where(lmask, q, 0.0).astype(BF16)
        s = jnp.einsum("nrc,njc->nrj", qg, nkb, preferred_element_type=F32)
        sc = s * scale + bias_ref[g][None]
        sink = sink_ref[g][None]
        m = jnp.maximum(jnp.max(sc, axis=-1, keepdims=True), sink)
        pe = jnp.exp(sc - m)
        probs = pe / (jnp.sum(pe, axis=-1, keepdims=True) + jnp.exp(sink - m))
        og = jnp.einsum("nrj,njc->nrc", probs.astype(BF16), nvb, preferred_element_type=F32)
        o = jnp.where(lmask, og, o)
    o_ref[...] = o.astype(BF16)


def _attn_sample(q3, kv3, ck, cv, bias_s, sink_s, *, nb, kvh, rep, hd):
    n, w, kvw = ck.shape
    kern = functools.partial(_attn_sample_kernel, kvh=kvh, rep=rep, hd=hd, scale=1.0 / math.sqrt(hd))
    cache_spec = pl.BlockSpec((nb, w, kvw), lambda i: (i, 0, 0))
    return pl.pallas_call(
        kern,
        grid=(n // nb,),
        in_specs=[pl.BlockSpec((nb, rep, kvw), lambda i: (i, 0, 0)),
                  pl.BlockSpec((nb, 1, 2 * kvw), lambda i: (i, 0, 0)),
                  cache_spec, cache_spec,
                  _const_spec(bias_s.shape), _const_spec(sink_s.shape)],
        out_specs=[pl.BlockSpec((nb, rep, kvw), lambda i: (i, 0, 0)), cache_spec, cache_spec],
        out_shape=[jax.ShapeDtypeStruct((n, rep, kvw), BF16),
                   jax.ShapeDtypeStruct((n, w, kvw), F32), jax.ShapeDtypeStruct((n, w, kvw), F32)],
        compiler_params=_params("arbitrary"),
        name="swa_sample",
    )(q3, kv3, ck, cv, bias_s, sink_s)


def _route(h, wr_ref):
    logits = jnp.dot(h, wr_ref[...], preferred_element_type=F32, precision=lax.Precision.HIGHEST)
    ne = logits.shape[1]
    col = lax.broadcasted_iota(jnp.int32, logits.shape, 1)
    m1 = jnp.max(logits, axis=-1, keepdims=True)
    i1 = jnp.min(jnp.where(logits == m1, col, ne), axis=-1, keepdims=True)
    rest = jnp.where(col == i1, -jnp.inf, logits)
    m2 = jnp.max(rest, axis=-1, keepdims=True)
    i2 = jnp.min(jnp.where(rest == m2, col, ne), axis=-1, keepdims=True)
    e2 = jnp.exp(m2 - m1)
    g1 = 1.0 / (1.0 + e2)
    g2 = e2 / (1.0 + e2)
    return jnp.where(col == i1, g1, 0.0) + jnp.where(col == i2, g2, 0.0)


def _moe_dense_kernel(x2_ref, o_ref, wo_ref, gffn_ref, wr_ref, wg_ref, wu_ref, wd_ref, gfin_ref,
                      y_ref, x3_s, hb_s, comb_s, acc_s):
    e = pl.program_id(1)
    f = pl.program_id(2)
    last = jnp.logical_and(e == pl.num_programs(1) - 1, f == pl.num_programs(2) - 1)

    @pl.when(jnp.logical_and(e == 0, f == 0))
    def _():
        x3 = x2_ref[...] + _dot(o_ref[...], wo_ref[...])
        x3_s[...] = x3
        h = _rms(x3, gffn_ref[...])
        hb_s[...] = h.astype(BF16)
        comb_s[...] = _route(h, wr_ref)
        acc_s[...] = jnp.zeros_like(acc_s)

    hb = hb_s[...]
    a = _dot(hb, wg_ref[...])
    b = _dot(hb, wu_ref[...])
    part = _dot((jax.nn.silu(a) * b).astype(BF16), wd_ref[...])
    comb = comb_s[...]
    col = lax.broadcasted_iota(jnp.int32, comb.shape, 1)
    coef = jnp.sum(jnp.where(col == e, comb, 0.0), axis=-1, keepdims=True)
    acc_s[...] += coef * part

    @pl.when(last)
    def _():
        y_ref[...] = _rms(x3_s[...] + acc_s[...], gfin_ref[...])


def _moe_dense(x2, o, wo, gffn, wr, wgu, wd, gfin, *, tm, fc):
    rows, d = x2.shape
    ne, _, dff2 = wgu.shape
    dff = dff2 // 2
    nf = dff // fc
    return pl.pallas_call(
        _moe_dense_kernel,
        grid=(rows // tm, ne, nf),
        in_specs=[pl.BlockSpec((tm, d), lambda i, e, f: (i, 0)),
                  pl.BlockSpec((tm, o.shape[1]), lambda i, e, f: (i, 0)),
                  _const_spec(wo.shape), _const_spec(gffn.shape), _const_spec(wr.shape),
                  pl.BlockSpec((None, d, fc), lambda i, e, f: (e, 0, f)),
                  pl.BlockSpec((None, d, fc), lambda i, e, f: (e, 0, nf + f)),
                  pl.BlockSpec((None, fc, d), lambda i, e, f: (e, f, 0)),
                  _const_spec(gfin.shape)],
        out_specs=pl.BlockSpec((tm, d), lambda i, e, f: (i, 0)),
        out_shape=jax.ShapeDtypeStruct((rows, d), F32),
        scratch_shapes=[pltpu.VMEM((tm, d), F32), pltpu.VMEM((tm, d), BF16),
                        pltpu.VMEM((tm, ne), F32), pltpu.VMEM((tm, d), F32)],
        compiler_params=_params("arbitrary", "arbitrary", "arbitrary"),
        name="moe_dense",
    )(x2, o, wo, gffn, wr, wgu, wgu, wd, gfin)


def _largest_divisor(n, cap, mult):
    best = None
    for t in range(mult, cap + 1, mult):
        if n % t == 0:
            best = t
    assert best is not None, (n, cap, mult)
    return best


def kernel(x_prompt, x_sample, state_ssm_re, state_ssm_im, cache_k_win, cache_v_win, g_mix, g_ffn, g_kv, g_final, ssm_a_re, ssm_a_im, ssm_log_dt, ssm_b_re, ssm_b_im, ssm_c_re, ssm_c_im, ssm_d, w_glu, w_kv, w_q, w_o, attn_sinks, rel_bias, w_ffn_gate_up, w_ffn_down, w_router, w_exp_gate_up, w_exp_down):
    bsz, seq, d = x_prompt.shape
    ns, dec_seq, _ = x_sample.shape
    assert dec_seq == 1 and g_mix.shape[0] == 2 and ssm_a_re.shape[0] == 1 and w_q.shape[0] == 1
    _, g, p = ssm_a_re.shape
    hch = d // g
    gp = g * p
    window, kvh, hd = cache_k_win.shape[1:]
    kvw = kvh * hd
    nh = attn_sinks.shape[1]
    rep = nh // kvh
    nq = nh * hd
    assert bsz == 8 and ns % bsz == 0 and seq % window == 0 and LANES % hd == 0

    gpb = MXU_DIM // hch
    lam_re, lam_im, bb_re, bb_im = _zoh(ssm_a_re[0], ssm_a_im[0], ssm_log_dt[0], ssm_b_re[0], ssm_b_im[0])
    wb_re = _blockdiag_in(bb_re, g, p, gpb)
    wb_im = _blockdiag_in(bb_im, g, p, gpb)
    wc_re = _blockdiag_out(ssm_c_re[0], gpb)
    wc_imn = _blockdiag_out(-ssm_c_im[0], gpb)
    wglu = w_glu[0].astype(BF16)
    d_skip = ssm_d[0].reshape(1, d)
    wgu = w_ffn_gate_up[0].astype(BF16)
    wd = w_ffn_down[0].astype(BF16)
    wkv = w_kv.astype(BF16)
    wq = w_q[0].reshape(d, kvh, rep, hd).transpose(0, 2, 1, 3).reshape(d, nq).astype(BF16)
    wo = w_o[0].reshape(kvh, rep, hd, d).transpose(1, 0, 2, 3).reshape(nq, d).astype(BF16)
    wegu = w_exp_gate_up[0].astype(BF16)
    wed = w_exp_down[0].astype(BF16)
    bias, bias_s = _bias_tables(rel_bias, window, kvh, rep)
    sinks = attn_sinks[0]
    sink_s = sinks.reshape(kvh, rep, 1)

    x_tb = x_prompt.transpose(1, 0, 2).reshape(seq * bsz, d)
    x1_p, st_re, st_im = _ssm_prompt(x_tb, bsz, g_mix[0:1], lam_re, lam_im, wb_re, wb_im, wc_re, wc_imn,
                                     d_skip, wglu, lc=64)
    x1_s, hs_re, hs_im = _ssm_sample(x_sample.reshape(ns, d), g_mix[0:1], lam_re, lam_im,
                                     state_ssm_re[0].reshape(ns, gp), state_ssm_im[0].reshape(ns, gp),
                                     wb_re, wb_im, wc_re, wc_imn, d_skip, wglu)
    x1 = jnp.concatenate([x1_p, x1_s], axis=0)
    rows = x1.shape[0]

    x2, kv, q = _ffn(x1, g_ffn[0:1], wgu, wd, g_kv.reshape(1, d), wkv, g_mix[1:2], wq,
                     tm=_largest_divisor(rows, 768, 16))

    rows_b = rows // bsz
    o_p = _attn_prompt(q.reshape(rows_b, bsz * nq), kv.reshape(rows_b, bsz * 2 * kvw), bias, sinks,
                       bsz=bsz, nblk=seq // window, window=window, kvh=kvh, rep=rep, hd=hd)
    o_s, nk_s, nv_s = _attn_sample(q[seq * bsz:].reshape(ns, rep, kvw), kv[seq * bsz:].reshape(ns, 1, 2 * kvw),
                                   cache_k_win.reshape(ns, window, kvw), cache_v_win.reshape(ns, window, kvw),
                                   bias_s, sink_s, nb=16, kvh=kvh, rep=rep, hd=hd)
    o = jnp.concatenate([o_p.reshape(rows, nq)[:seq * bsz], o_s.reshape(ns, nq)], axis=0)

    y = _moe_dense(x2, o, wo, g_ffn[1:2], w_router[0], wegu, wed, g_final.reshape(1, d),
                   tm=_largest_divisor(rows, 768, 16), fc=512)

    y_prompt = y[:seq * bsz].reshape(seq, bsz, d).transpose(1, 0, 2)
    y_sample = y[seq * bsz:].reshape(ns, 1, d)
    kv_tail = kv[(seq - window) * bsz:seq * bsz].reshape(window, bsz, 2, kvh, hd).transpose(2, 1, 0, 3, 4)
    return (y_prompt, y_sample,
            st_re.reshape(1, bsz, g, p), st_im.reshape(1, bsz, g, p), kv_tail[0], kv_tail[1],
            hs_re.reshape(1, ns, g, p), hs_im.reshape(1, ns, g, p),
            nk_s.reshape(ns, window, kvh, hd), nv_s.reshape(ns, window, kvh, hd))
```

```python
import functools
import math

import numpy as np
import jax
import jax.numpy as jnp
from jax import lax
from jax.experimental import pallas as pl
from jax.experimental.pallas import tpu as pltpu

F32 = jnp.float32
BF16 = jnp.bfloat16

EPS = 1e-6
NEG_INF = -1e30
TOP_K = 2
MAX_DISTANCE = 128
MXU_DIM = 256
LANES = 128
VMEM_LIMIT_BYTES = 56 * 1024 * 1024


def _dot(a, b):
    return jnp.dot(a, b, preferred_element_type=F32)


def _rms(x, g):
    return x * lax.rsqrt(jnp.mean(x * x, axis=-1, keepdims=True) + EPS) * g


def _const_spec(shape):
    nd = len(shape)
    return pl.BlockSpec(shape, lambda *_: (0,) * nd)


def _params(*sem):
    return pltpu.CompilerParams(dimension_semantics=sem, vmem_limit_bytes=VMEM_LIMIT_BYTES)


def _zoh_kernel(a_re_ref, a_im_ref, log_dt_ref, b_re_ref, b_im_ref,
                lam_re_ref, lam_im_ref, bb_re_ref, bb_im_ref):
    a_re = a_re_ref[...]
    a_im = a_im_ref[...]
    dt = jnp.exp(log_dt_ref[...])
    mag = jnp.exp(a_re * dt)
    lr = mag * jnp.cos(a_im * dt)
    li = mag * jnp.sin(a_im * dt)
    lam_re_ref[...] = lr
    lam_im_ref[...] = li
    nr = lr - 1.0
    den = a_re * a_re + a_im * a_im
    qr = (nr * a_re + li * a_im) / den
    qi = (li * a_re - nr * a_im) / den
    b_re = b_re_ref[...]
    b_im = b_im_ref[...]
    bb_re_ref[...] = qr * b_re - qi * b_im
    bb_im_ref[...] = qr * b_im + qi * b_re


def _zoh(a_re, a_im, log_dt, b_re, b_im):
    g, p = a_re.shape
    h = b_re.shape[-1]
    gp = g * p
    row = jax.ShapeDtypeStruct((1, gp), F32)
    mat = jax.ShapeDtypeStruct((h, gp), F32)
    return pl.pallas_call(
        _zoh_kernel,
        out_shape=(row, row, mat, mat),
        name="s5_zoh",
    )(a_re.reshape(1, gp), a_im.reshape(1, gp), jnp.repeat(log_dt, p).reshape(1, gp),
      b_re.transpose(2, 0, 1).reshape(h, gp), b_im.transpose(2, 0, 1).reshape(h, gp))


def _blockdiag_in(bb, g, p, gpb):
    h = bb.shape[0]
    nkb = g // gpb
    bb4 = bb.reshape(h, nkb, gpb, p).transpose(1, 0, 2, 3)
    eye = jnp.eye(gpb, dtype=bb.dtype)
    w = bb4[:, None, :, :, :] * eye[None, :, None, :, None]
    return w.reshape(nkb, gpb * h, gpb * p).astype(BF16)


def _blockdiag_out(c, gpb):
    g, h, p = c.shape
    nkb = g // gpb
    c4 = c.reshape(nkb, gpb, h, p).transpose(0, 1, 3, 2)
    eye = jnp.eye(gpb, dtype=c.dtype)
    w = c4[:, :, :, None, :] * eye[None, :, None, :, None]
    return w.reshape(nkb, gpb * p, gpb * h).astype(BF16)


def _ssm_in_proj(ub, wb_re_ref, wb_im_ref, bu_re, bu_im):
    nkb, ublk, sblk = wb_re_ref.shape
    for kb in range(nkb):
        ukb = ub[:, kb * ublk:(kb + 1) * ublk]
        bu_re[:, kb * sblk:(kb + 1) * sblk] = _dot(ukb, wb_re_ref[kb])
        bu_im[:, kb * sblk:(kb + 1) * sblk] = _dot(ukb, wb_im_ref[kb])


def _ssm_tail(x, u, h_re, h_im, wc_re_ref, wc_imn_ref, d_ref, wglu_ref):
    nkb, sblk, _ = wc_re_ref.shape
    ys = []
    for kb in range(nkb):
        hr = h_re[:, kb * sblk:(kb + 1) * sblk].astype(BF16)
        hi = h_im[:, kb * sblk:(kb + 1) * sblk].astype(BF16)
        ys.append(_dot(hr, wc_re_ref[kb]) + _dot(hi, wc_imn_ref[kb]))
    y = jnp.concatenate(ys, axis=1) + d_ref[...] * u
    z = jax.nn.gelu(y).astype(BF16)
    gl = _dot(z, wglu_ref[...])
    d = x.shape[1]
    return x + gl[:, :d] * jax.nn.sigmoid(gl[:, d:])


def _ssm_prompt_kernel(x_ref, g_ref, lam_re_ref, lam_im_ref, wb_re_ref, wb_im_ref,
                       wc_re_ref, wc_imn_ref, d_ref, wglu_ref,
                       out_ref, st_re_ref, st_im_ref, bu_re, bu_im, *, lc, bsz, lane_chunk):
    @pl.when(pl.program_id(0) == 0)
    def _():
        st_re_ref[...] = jnp.zeros_like(st_re_ref)
        st_im_ref[...] = jnp.zeros_like(st_im_ref)

    x = x_ref[...]
    u = _rms(x, g_ref[...])
    _ssm_in_proj(u.astype(BF16), wb_re_ref, wb_im_ref, bu_re, bu_im)

    gp = bu_re.shape[1]
    for c0 in range(0, gp, lane_chunk):
        sl = slice(c0, c0 + lane_chunk)
        lre = jnp.broadcast_to(lam_re_ref[:, sl], (bsz, lane_chunk))
        lim = jnp.broadcast_to(lam_im_ref[:, sl], (bsz, lane_chunk))

        def step(t, carry):
            hr, hi = carry
            rows = pl.ds(pl.multiple_of(t * bsz, bsz), bsz)
            nr = lre * hr - lim * hi + bu_re[rows, sl]
            ni = lre * hi + lim * hr + bu_im[rows, sl]
            bu_re[rows, sl] = nr
            bu_im[rows, sl] = ni
            return nr, ni

        hr, hi = lax.fori_loop(0, lc, step, (st_re_ref[:, sl], st_im_ref[:, sl]), unroll=8)
        st_re_ref[:, sl] = hr
        st_im_ref[:, sl] = hi

    out_ref[...] = _ssm_tail(x, u, bu_re, bu_im, wc_re_ref, wc_imn_ref, d_ref, wglu_ref)


def _ssm_sample_kernel(x_ref, g_ref, lam_re_ref, lam_im_ref, h0_re_ref, h0_im_ref, wb_re_ref, wb_im_ref,
                       wc_re_ref, wc_imn_ref, d_ref, wglu_ref,
                       out_ref, h_re_ref, h_im_ref):
    x = x_ref[...]
    u = _rms(x, g_ref[...])
    _ssm_in_proj(u.astype(BF16), wb_re_ref, wb_im_ref, h_re_ref, h_im_ref)
    lre = lam_re_ref[...]
    lim = lam_im_ref[...]
    h0r = h0_re_ref[...]
    h0i = h0_im_ref[...]
    h_re_ref[...] = lre * h0r - lim * h0i + h_re_ref[...]
    h_im_ref[...] = lre * h0i + lim * h0r + h_im_ref[...]
    out_ref[...] = _ssm_tail(x, u, h_re_ref, h_im_ref, wc_re_ref, wc_imn_ref, d_ref, wglu_ref)


def _ssm_prompt(x_tb, bsz, g, lam_re, lam_im, wb_re, wb_im, wc_re, wc_imn, d_skip, wglu, *, lc):
    rows, d = x_tb.shape
    gp = lam_re.shape[1]
    r = lc * bsz
    kern = functools.partial(_ssm_prompt_kernel, lc=lc, bsz=bsz, lane_chunk=8 * LANES)
    return pl.pallas_call(
        kern,
        grid=(rows // r,),
        in_specs=[pl.BlockSpec((r, d), lambda c: (c, 0)),
                  _const_spec(g.shape), _const_spec(lam_re.shape), _const_spec(lam_im.shape),
                  _const_spec(wb_re.shape), _const_spec(wb_im.shape),
                  _const_spec(wc_re.shape), _const_spec(wc_imn.shape),
                  _const_spec(d_skip.shape), _const_spec(wglu.shape)],
        out_specs=[pl.BlockSpec((r, d), lambda c: (c, 0)),
                   _const_spec((bsz, gp)), _const_spec((bsz, gp))],
        out_shape=[jax.ShapeDtypeStruct((rows, d), F32),
                   jax.ShapeDtypeStruct((bsz, gp), F32), jax.ShapeDtypeStruct((bsz, gp), F32)],
        scratch_shapes=[pltpu.VMEM((r, gp), F32), pltpu.VMEM((r, gp), F32)],
        compiler_params=_params("arbitrary"),
        name="s5_prompt",
    )(x_tb, g, lam_re, lam_im, wb_re, wb_im, wc_re, wc_imn, d_skip, wglu)


def _ssm_sample(x, g, lam_re, lam_im, h0_re, h0_im, wb_re, wb_im, wc_re, wc_imn, d_skip, wglu):
    n, d = x.shape
    gp = lam_re.shape[1]
    return pl.pallas_call(
        _ssm_sample_kernel,
        out_shape=[jax.ShapeDtypeStruct((n, d), F32),
                   jax.ShapeDtypeStruct((n, gp), F32), jax.ShapeDtypeStruct((n, gp), F32)],
        compiler_params=_params(),
        name="s5_sample",
    )(x, g, lam_re, lam_im, h0_re, h0_im, wb_re, wb_im, wc_re, wc_imn, d_skip, wglu)


def _ffn_chunks(d_ff):
    step = 3 * MXU_DIM
    return [(c, min(c + step, d_ff)) for c in range(0, d_ff, step)]


def _ffn_kernel(x_ref, gffn_ref, wgu_ref, wd_ref, gkv_ref, wkv_ref, gq_ref, wq_ref,
                x2_ref, kv_ref, q_ref):
    x = x_ref[...]
    hb = _rms(x, gffn_ref[...]).astype(BF16)
    d_ff = wd_ref.shape[0]
    acc = None
    for c0, c1 in _ffn_chunks(d_ff):
        a = _dot(hb, wgu_ref[:, c0:c1])
        b = _dot(hb, wgu_ref[:, d_ff + c0:d_ff + c1])
        part = _dot((jax.nn.silu(a) * b).astype(BF16), wd_ref[c0:c1, :])
        acc = part if acc is None else acc + part
    x2 = x + acc
    x2_ref[...] = x2
    kv_ref[...] = _dot(_rms(x2, gkv_ref[...]).astype(BF16), wkv_ref[...])
    q_ref[...] = _dot(_rms(x2, gq_ref[...]).astype(BF16), wq_ref[...]).astype(BF16)


def _ffn(x, gffn, wgu, wd, gkv, wkv, gq, wq, *, tm):
    rows, d = x.shape
    kvw = wkv.shape[1]
    nq = wq.shape[1]
    return pl.pallas_call(
        _ffn_kernel,
        grid=(rows // tm,),
        in_specs=[pl.BlockSpec((tm, d), lambda i: (i, 0)),
                  _const_spec(gffn.shape), _const_spec(wgu.shape), _const_spec(wd.shape),
                  _const_spec(gkv.shape), _const_spec(wkv.shape),
                  _const_spec(gq.shape), _const_spec(wq.shape)],
        out_specs=[pl.BlockSpec((tm, d), lambda i: (i, 0)),
                   pl.BlockSpec((tm, kvw), lambda i: (i, 0)),
                   pl.BlockSpec((tm, nq), lambda i: (i, 0))],
        out_shape=[jax.ShapeDtypeStruct((rows, d), F32),
                   jax.ShapeDtypeStruct((rows, kvw), F32),
                   jax.ShapeDtypeStruct((rows, nq), BF16)],
        compiler_params=_params("arbitrary"),
        name="ffn_kv_q",
    )(x, gffn, wgu, wd, gkv, wkv, gq, wq)


def _t5_bucket(dist, num_buckets):
    max_exact = num_buckets // 2
    d = jnp.maximum(dist, 0)
    large = max_exact + (jnp.log(jnp.maximum(d, 1).astype(F32) / max_exact)
                         / math.log(MAX_DISTANCE / max_exact) * (num_buckets - max_exact)).astype(jnp.int32)
    large = jnp.minimum(large, num_buckets - 1)
    return jnp.where(d < max_exact, d, large)


def _bias_kernel(bm_ref, bs_ref, rb_ref, bias_ref, bias_s_ref, *, kvh, rep):
    nb, nh = rb_ref.shape
    bm = bm_ref[...]
    bs = bs_ref[...]

    def lookup(buckets, h):
        def body(k, acc):
            return jnp.where(buckets == k, rb_ref[k, h], acc)
        return lax.fori_loop(0, nb, body, jnp.zeros(buckets.shape, F32))

    for h in range(nh):
        bias_ref[h] = lookup(bm, h)
    for g in range(kvh):
        for r in range(rep):
            bias_s_ref[g, r:r + 1, :] = lookup(bs, g * rep + r)


def _bias_tables(rel_bias, window, kvh, rep):
    nb, nh = rel_bias.shape
    qi = jnp.arange(window)[:, None]
    kj = jnp.arange(window)[None, :]
    bm = _t5_bucket((qi - kj) % window, nb).astype(jnp.int32)
    bs = _t5_bucket(window - 1 - kj, nb).astype(jnp.int32)
    kern = functools.partial(_bias_kernel, kvh=kvh, rep=rep)
    return pl.pallas_call(
        kern,
        in_specs=[pl.BlockSpec(memory_space=pltpu.VMEM), pl.BlockSpec(memory_space=pltpu.VMEM),
                  pl.BlockSpec(memory_space=pltpu.SMEM)],
        out_shape=[jax.ShapeDtypeStruct((nh, window, window), F32),
                   jax.ShapeDtypeStruct((kvh, rep, window), F32)],
        name="t5_bias",
    )(bm, bs, rel_bias)


def _attn_prompt_kernel(q_ref, kvc_ref, kvp_ref, bias_ref, sink_ref, o_ref, *, kvh, rep, hd, scale):
    w = q_ref.shape[0]
    kvw = kvh * hd
    first = pl.program_id(1) == 0
    q = q_ref[...]
    kvc = kvc_ref[...]
    kvp = kvp_ref[...]
    lane = lax.broadcasted_iota(jnp.int32, (1, LANES), 1)
    qi = lax.broadcasted_iota(jnp.int32, (w, w), 0)
    kj = lax.broadcasted_iota(jnp.int32, (w, w), 1)
    upper = jnp.concatenate([kj > qi] * rep, axis=0)
    masked = jnp.logical_and(upper, first)
    heads_per_blk = LANES // hd
    o_blks = []
    for p in range(kvw // LANES):
        cs = slice(p * LANES, (p + 1) * LANES)
        kc, kp = kvc[:, cs], kvp[:, cs]
        vs = slice(kvw + p * LANES, kvw + (p + 1) * LANES)
        vb = jnp.concatenate([kvp[:, vs], kvc[:, vs]], axis=0).astype(BF16)
        qg = jnp.concatenate([q[:, r * kvw + p * LANES:r * kvw + (p + 1) * LANES] for r in range(rep)], axis=0)
        o_blk = None
        for half in range(heads_per_blk):
            g = p * heads_per_blk + half
            lmask = jnp.logical_and(lane >= half * hd, lane < (half + 1) * hd)
            kb = jnp.concatenate([jnp.where(lmask, kp, 0.0), jnp.where(lmask, kc, 0.0)], axis=0).astype(BF16)
            s = lax.dot_general(qg, kb, (((1,), (1,)), ((), ())), preferred_element_type=F32)
            bias = bias_ref[g * rep:(g + 1) * rep].reshape(rep * w, w)
            sc = jnp.where(upper, s[:, :w], s[:, w:]) * scale + bias
            sc = jnp.where(masked, NEG_INF, sc)
            sink = jnp.concatenate([jnp.full((w, 1), sink_ref[g * rep + r], F32) for r in range(rep)], axis=0)
            m = jnp.maximum(jnp.max(sc, axis=-1, keepdims=True), sink)
            pe = jnp.exp(sc - m)
            denom = jnp.sum(pe, axis=-1, keepdims=True) + jnp.exp(sink - m)
            pcat = jnp.concatenate([jnp.where(upper, pe, 0.0), jnp.where(upper, 0.0, pe)], axis=1).astype(BF16)
            og = _dot(pcat, vb) / denom
            o_blk = og if o_blk is None else jnp.where(lmask, og, o_blk)
        o_blks.append(o_blk)
    cols = []
    for r in range(rep):
        for p in range(kvw // LANES):
            cols.append(o_blks[p][r * w:(r + 1) * w])
    o_ref[...] = jnp.concatenate(cols, axis=1).astype(BF16)


def _attn_prompt(q2, kv2, bias, sinks, *, bsz, nblk, window, kvh, rep, hd):
    nq = kvh * rep * hd
    kvw2 = 2 * kvh * hd
    kern = functools.partial(_attn_prompt_kernel, kvh=kvh, rep=rep, hd=hd, scale=1.0 / math.sqrt(hd))
    return pl.pallas_call(
        kern,
        grid=(bsz, nblk),
        in_specs=[pl.BlockSpec((window, nq), lambda b, i: (i, b)),
                  pl.BlockSpec((window, kvw2), lambda b, i: (i, b)),
                  pl.BlockSpec((window, kvw2), lambda b, i: (jnp.maximum(i - 1, 0), b)),
                  _const_spec(bias.shape),
                  pl.BlockSpec(memory_space=pltpu.SMEM)],
        out_specs=pl.BlockSpec((window, nq), lambda b, i: (i, b)),
        out_shape=jax.ShapeDtypeStruct((nblk * window, bsz * nq), BF16),
        compiler_params=_params("arbitrary", "arbitrary"),
        name="swa_prompt",
    )(q2, kv2, kv2, bias, sinks)


def _attn_sample_kernel(q_ref, kv_ref, ck_ref, cv_ref, bias_ref, sink_ref, o_ref, nk_ref, nv_ref,
                        *, kvh, rep, hd, scale):
    nb, w, kvw = ck_ref.shape
    kv = kv_ref[...]
    lane = lax.broadcasted_iota(jnp.int32, (nb, rep, kvw), 2)

    def shifted(c_ref, n_ref, new):
        flat = c_ref[...].reshape(nb * w, kvw)
        n_ref[...] = pltpu.roll(flat, nb * w - 1, axis=0).reshape(nb, w, kvw)
        n_ref[:, w - 1:w, :] = new
        return n_ref[...].astype(BF16)

    nkb = shifted(ck_ref, nk_ref, kv[:, :, :kvw])
    nvb = shifted(cv_ref, nv_ref, kv[:, :, kvw:])
    q = q_ref[...].astype(F32)
    o = jnp.zeros((nb, rep, kvw), F32)
    for g in range(kvh):
        lmask = jnp.logical_and(lane >= g * hd, lane < (g + 1) * hd)
        qg = jnp.where(lmask, q, 0.0).astype(BF16)
        s = jnp.einsum("nrc,njc->nrj", qg, nkb, preferred_element_type=F32)
        sc = s * scale + bias_ref[g][None]
        sink = sink_ref[g][None]
        m = jnp.maximum(jnp.max(sc, axis=-1, keepdims=True), sink)
        pe = jnp.exp(sc - m)
        probs = pe / (jnp.sum(pe, axis=-1, keepdims=True) + jnp.exp(sink - m))
        og = jnp.einsum("nrj,njc->nrc", probs.astype(BF16), nvb, preferred_element_type=F32)
        o = jnp.where(lmask, og, o)
    o_ref[...] = o.astype(BF16)


def _attn_sample(q3, kv3, ck, cv, bias_s, sink_s, *, nb, kvh, rep, hd):
    n, w, kvw = ck.shape
    kern = functools.partial(_attn_sample_kernel, kvh=kvh, rep=rep, hd=hd, scale=1.0 / math.sqrt(hd))
    cache_spec = pl.BlockSpec((nb, w, kvw), lambda i: (i, 0, 0))
    return pl.pallas_call(
        kern,
        grid=(n // nb,),
        in_specs=[pl.BlockSpec((nb, rep, kvw), lambda i: (i, 0, 0)),
                  pl.BlockSpec((nb, 1, 2 * kvw), lambda i: (i, 0, 0)),
                  cache_spec, cache_spec,
                  _const_spec(bias_s.shape), _const_spec(sink_s.shape)],
        out_specs=[pl.BlockSpec((nb, rep, kvw), lambda i: (i, 0, 0)), cache_spec, cache_spec],
        out_shape=[jax.ShapeDtypeStruct((n, rep, kvw), BF16),
                   jax.ShapeDtypeStruct((n, w, kvw), F32), jax.ShapeDtypeStruct((n, w, kvw), F32)],
        compiler_params=_params("arbitrary"),
        name="swa_sample",
    )(q3, kv3, ck, cv, bias_s, sink_s)


CHUNK = 16
SUB = 256
ROW_BLOCK = 512


def _chunk_copy(src, src_row, dst, dst_row, sem):
    return pltpu.make_async_copy(src.at[pl.ds(pl.multiple_of(src_row, CHUNK), CHUNK), :],
                                 dst.at[pl.ds(pl.multiple_of(dst_row, CHUNK), CHUNK), :], sem)


def _moe_route_kernel(x2_ref, o_ref, wo_ref, gffn_ref, wr_ref, lst_ref, ust_ref,
                      x3_ref, cm_ref, tmeta_ref, xs_hbm,
                      comp_s, zero_s, base_v, pend_sm, sem, *, ne, cap):
    i = pl.program_id(0)
    nt = pl.num_programs(0)
    tm = x2_ref.shape[0]
    w = comp_s.shape[1]
    slot = lax.rem(i, 2)

    @pl.when(i == 0)
    def _():
        base_v[...] = jnp.zeros_like(base_v)
        zero_s[...] = jnp.zeros_like(zero_s)

    def drain(n):
        def body(c, carry):
            _chunk_copy(zero_s, 0, xs_hbm, 0, sem).wait()
            return carry
        lax.fori_loop(0, n, body, 0)

    @pl.when(i > 0)
    def _():
        drain(pend_sm[0])

    x3 = x2_ref[...] + _dot(o_ref[...], wo_ref[...])
    x3_ref[...] = x3
    hb = _rms(x3, gffn_ref[...]).astype(BF16)

    lane = lax.broadcasted_iota(jnp.int32, (tm, LANES), 1)
    logits = jnp.where(lane < ne, _dot(hb, wr_ref[...]), -jnp.inf)
    m1 = jnp.max(logits, axis=-1, keepdims=True)
    i1 = jnp.min(jnp.where(logits == m1, lane, LANES), axis=-1, keepdims=True)
    rest = jnp.where(lane == i1, -jnp.inf, logits)
    m2 = jnp.max(rest, axis=-1, keepdims=True)
    i2 = jnp.min(jnp.where(rest == m2, lane, LANES), axis=-1, keepdims=True)
    e2 = jnp.exp(m2 - m1)
    g1 = 1.0 / (1.0 + e2)
    g2 = e2 / (1.0 + e2)

    sel = jnp.where(jnp.logical_or(lane == i1, lane == i2), 1.0, 0.0)
    rank = _dot(lst_ref[...], sel.astype(BF16))
    cnt = jnp.sum(sel, axis=0, keepdims=True)
    cpad = jnp.floor((cnt + (CHUNK - 1)) / CHUNK) * CHUNK
    loff = _dot(jnp.broadcast_to(cpad, (8, LANES)).astype(BF16), ust_ref[...])[0:1]
    dest = loff + rank
    ld1 = jnp.sum(jnp.where(lane == i1, dest, 0.0), axis=-1, keepdims=True)
    ld2 = jnp.sum(jnp.where(lane == i2, dest, 0.0), axis=-1, keepdims=True)
    cm = jnp.where(lane == 0, ld1, jnp.where(lane == 1, ld2, jnp.where(lane == 2, g1, jnp.where(lane == 3, g2, 0.0))))
    cm_ref[...] = cm

    rm = cm.T
    rowi = lax.broadcasted_iota(jnp.int32, (w, tm), 0).astype(F32)
    place = jnp.where(jnp.logical_or(rowi == rm[0:1], rowi == rm[1:2]), 1.0, 0.0).astype(BF16)
    comp_s[slot] = _dot(place, hb).astype(BF16)

    base = base_v[...]
    srow = lax.broadcasted_iota(jnp.int32, (8, LANES), 0)
    tmeta_ref[...] = jnp.where(srow == 0, cpad, jnp.where(srow == 1, base, jnp.where(srow == 2, base + cpad, 0.0))
                               ).astype(jnp.int32)
    base_v[...] = base + cpad

    cpad_i = cpad.astype(jnp.int32)
    loff_i = loff.astype(jnp.int32)
    base_i = base.astype(jnp.int32)
    total = 0
    tails = []
    for e in range(ne):
        n_e = cpad_i[0, e]
        src0 = loff_i[0, e]
        dst0 = base_i[0, e] + e * cap

        def start(c, carry, src0=src0, dst0=dst0):
            _chunk_copy(comp_s.at[slot], src0 + c * CHUNK, xs_hbm, dst0 + c * CHUNK, sem).start()
            return carry

        nchunks = lax.shift_right_logical(n_e, int(math.log2(CHUNK)))
        lax.fori_loop(0, nchunks, start, 0)
        total = total + nchunks
        tails.append(dst0 + n_e)
    pend_sm[0] = total

    @pl.when(i == nt - 1)
    def _():
        nz_total = 0
        for e in range(ne):
            end = tails[e]
            nz = lax.shift_right_logical(lax.rem(SUB - lax.rem(end, SUB), SUB), int(math.log2(CHUNK)))

            def zstart(c, carry, end=end):
                _chunk_copy(zero_s, 0, xs_hbm, end + c * CHUNK, sem).start()
                return carry

            lax.fori_loop(0, nz, zstart, 0)
            nz_total = nz_total + nz
        drain(total + nz_total)


def _moe_route(x2, o, wo, gffn, wr_pad, *, tm, ne, cap):
    rows, d = x2.shape
    nt = rows // tm
    w = TOP_K * tm + LANES
    lst = jnp.asarray(np.tril(np.ones((tm, tm), np.float32), -1), BF16)
    ust = jnp.asarray(np.triu(np.ones((LANES, LANES), np.float32), 1), BF16)
    kern = functools.partial(_moe_route_kernel, ne=ne, cap=cap)
    return pl.pallas_call(
        kern,
        grid=(nt,),
        in_specs=[pl.BlockSpec((tm, d), lambda i: (i, 0)),
                  pl.BlockSpec((tm, o.shape[1]), lambda i: (i, 0)),
                  _const_spec(wo.shape), _const_spec(gffn.shape), _const_spec(wr_pad.shape),
                  _const_spec(lst.shape), _const_spec(ust.shape)],
        out_specs=[pl.BlockSpec((tm, d), lambda i: (i, 0)),
                   pl.BlockSpec((tm, LANES), lambda i: (i, 0)),
                   pl.BlockSpec((8, LANES), lambda i: (i, 0)),
                   pl.BlockSpec(memory_space=pl.ANY)],
        out_shape=[jax.ShapeDtypeStruct((rows, d), F32),
                   jax.ShapeDtypeStruct((rows, LANES), F32),
                   jax.ShapeDtypeStruct((nt * 8, LANES), jnp.int32),
                   jax.ShapeDtypeStruct((ne * cap, d), BF16)],
        scratch_shapes=[pltpu.VMEM((2, w, d), BF16), pltpu.VMEM((CHUNK, d), BF16),
                        pltpu.VMEM((1, LANES), F32), pltpu.SMEM((1,), jnp.int32),
                        pltpu.SemaphoreType.DMA(())],
        compiler_params=_params("arbitrary"),
        name="moe_route",
    )(x2, o, wo, gffn, wr_pad, lst, ust)


def _expert_steps(nrows, nf, max_blocks):
    ne = nrows.shape[0]
    nblk = (nrows + ROW_BLOCK - 1) // ROW_BLOCK
    cum = jnp.cumsum(nblk) * nf
    total = cum[-1]
    s = jnp.minimum(jnp.arange(max_blocks * nf, dtype=jnp.int32), total - 1)
    e = jnp.minimum(jnp.searchsorted(cum, s, side="right"), ne - 1).astype(jnp.int32)
    nb_e = nblk[e]
    within = s - (cum[e] - nb_e * nf)
    f = within // nb_e
    r = within - f * nb_e
    nsub = jnp.clip((nrows[e] - r * ROW_BLOCK + SUB - 1) // SUB, 0, ROW_BLOCK // SUB)
    first = (r == 0).astype(jnp.int32)
    return (e, f.astype(jnp.int32), r.astype(jnp.int32), first, nsub.astype(jnp.int32),
            jnp.reshape(total, (1,)).astype(jnp.int32))


def _moe_up_kernel(e_ref, f_ref, r_ref, first_ref, nsub_ref, n_ref, xs_ref, wg_ref, wu_ref, act_ref, wgb_s, wub_s):
    s = pl.program_id(0)

    @pl.when(s < n_ref[0])
    def _():
        @pl.when(first_ref[s] == 1)
        def _():
            wgb_s[...] = wg_ref[...].astype(BF16)
            wub_s[...] = wu_ref[...].astype(BF16)

        def body(j, carry):
            rows = pl.ds(pl.multiple_of(j * SUB, SUB), SUB)
            xsb = xs_ref[rows, :]
            a = _dot(xsb, wgb_s[...])
            b = _dot(xsb, wub_s[...])
            act_ref[rows, :] = (jax.nn.silu(a) * b).astype(BF16)
            return carry

        lax.fori_loop(0, nsub_ref[s], body, 0)


def _moe_up(steps, xs, wgu, *, fc, cap, max_blocks):
    ne, d, dff2 = wgu.shape
    dff = dff2 // 2
    nf = dff // fc
    cb = cap // ROW_BLOCK
    grid_spec = pltpu.PrefetchScalarGridSpec(
        num_scalar_prefetch=6,
        grid=(max_blocks * nf,),
        in_specs=[pl.BlockSpec((ROW_BLOCK, d), lambda s, e, f, r, *_: (e[s] * cb + r[s], 0)),
                  pl.BlockSpec((None, d, fc), lambda s, e, f, r, *_: (e[s], 0, f[s])),
                  pl.BlockSpec((None, d, fc), lambda s, e, f, r, *_: (e[s], 0, nf + f[s]))],
        out_specs=pl.BlockSpec((ROW_BLOCK, fc), lambda s, e, f, r, *_: (e[s] * cb + r[s], f[s])),
        scratch_shapes=[pltpu.VMEM((d, fc), BF16), pltpu.VMEM((d, fc), BF16)])
    return pl.pallas_call(
        _moe_up_kernel,
        grid_spec=grid_spec,
        out_shape=jax.ShapeDtypeStruct((ne * cap, dff), BF16),
        compiler_params=_params("arbitrary"),
        name="moe_up",
    )(*steps, xs, wgu, wgu)


def _moe_down_kernel(e_ref, f_ref, r_ref, first_ref, nsub_ref, n_ref, act_ref, wd_ref, ys_ref, wdb_s):
    s = pl.program_id(0)

    @pl.when(s < n_ref[0])
    def _():
        @pl.when(first_ref[s] == 1)
        def _():
            wdb_s[...] = wd_ref[...].astype(BF16)

        def body(j, carry):
            rows = pl.ds(pl.multiple_of(j * SUB, SUB), SUB)
            ys_ref[rows, :] = _dot(act_ref[rows, :], wdb_s[...]).astype(BF16)
            return carry

        lax.fori_loop(0, nsub_ref[s], body, 0)


def _moe_down(steps, act, wd, *, cap, max_blocks):
    ne, dff, d = wd.shape
    cb = cap // ROW_BLOCK
    grid_spec = pltpu.PrefetchScalarGridSpec(
        num_scalar_prefetch=6,
        grid=(max_blocks,),
        in_specs=[pl.BlockSpec((ROW_BLOCK, dff), lambda s, e, f, r, *_: (e[s] * cb + r[s], 0)),
                  pl.BlockSpec((None, dff, d), lambda s, e, f, r, *_: (e[s], 0, 0))],
        out_specs=pl.BlockSpec((ROW_BLOCK, d), lambda s, e, f, r, *_: (e[s] * cb + r[s], 0)),
        scratch_shapes=[pltpu.VMEM((dff, d), BF16)])
    return pl.pallas_call(
        _moe_down_kernel,
        grid_spec=grid_spec,
        out_shape=jax.ShapeDtypeStruct((ne * cap, d), BF16),
        compiler_params=_params("arbitrary"),
        name="moe_down",
    )(*steps, act, wd)


def _moe_combine_kernel(cpad_ref, seg_ref, x3_ref, cm_ref, gfin_ref, ys_hbm, y_ref, yloc_s, sem, *, ne):
    i = pl.program_id(0)
    nt = pl.num_programs(0)
    tm = x3_ref.shape[0]
    w = yloc_s.shape[1]
    slot = lax.rem(i, 2)
    shift = int(math.log2(CHUNK))

    def issue(t, sl):
        off = 0
        for e in range(ne):
            n_e = cpad_ref[t * ne + e]
            src0 = seg_ref[t * ne + e]

            def start(c, carry, src0=src0, off=off):
                _chunk_copy(ys_hbm, src0 + c * CHUNK, yloc_s.at[sl], off + c * CHUNK, sem.at[sl]).start()
                return carry

            lax.fori_loop(0, lax.shift_right_logical(n_e, shift), start, 0)
            off = off + n_e

    @pl.when(i == 0)
    def _():
        yloc_s[...] = jnp.zeros_like(yloc_s)
        issue(0, 0)

    @pl.when(i + 1 < nt)
    def _():
        issue(i + 1, 1 - slot)

    total = 0
    for e in range(ne):
        total = total + lax.shift_right_logical(cpad_ref[i * ne + e], shift)

    def wait(c, carry):
        _chunk_copy(ys_hbm, 0, yloc_s.at[slot], 0, sem.at[slot]).wait()
        return carry

    lax.fori_loop(0, total, wait, 0)

    cm = cm_ref[...]
    col = lax.broadcasted_iota(jnp.int32, (tm, w), 1).astype(F32)
    yl = yloc_s[slot]
    y1 = _dot(jnp.where(col == cm[:, 0:1], 1.0, 0.0).astype(BF16), yl)
    y2 = _dot(jnp.where(col == cm[:, 1:2], 1.0, 0.0).astype(BF16), yl)
    y = x3_ref[...] + cm[:, 2:3] * y1 + cm[:, 3:4] * y2
    y_ref[...] = _rms(y, gfin_ref[...])


def _moe_combine(cpad, seg, x3, cm, gfin, ys, *, tm, ne):
    rows, d = x3.shape
    w = TOP_K * tm + LANES
    kern = functools.partial(_moe_combine_kernel, ne=ne)
    grid_spec = pltpu.PrefetchScalarGridSpec(
        num_scalar_prefetch=2,
        grid=(rows // tm,),
        in_specs=[pl.BlockSpec((tm, d), lambda i, *_: (i, 0)),
                  pl.BlockSpec((tm, LANES), lambda i, *_: (i, 0)),
                  pl.BlockSpec(gfin.shape, lambda i, *_: (0, 0)),
                  pl.BlockSpec(memory_space=pl.ANY)],
        out_specs=pl.BlockSpec((tm, d), lambda i, *_: (i, 0)),
        scratch_shapes=[pltpu.VMEM((2, w, d), BF16), pltpu.SemaphoreType.DMA((2,))])
    return pl.pallas_call(
        kern,
        grid_spec=grid_spec,
        out_shape=jax.ShapeDtypeStruct((rows, d), F32),
        compiler_params=_params("arbitrary"),
        name="moe_combine",
    )(cpad, seg, x3, cm, gfin, ys)


def _moe(x2, o, wo, gffn, wr, wgu, wd, gfin, *, tm, fc):
    rows, d = x2.shape
    ne = wr.shape[1]
    nt = rows // tm
    nf = wgu.shape[2] // 2 // fc
    cap = -(-(rows + nt * (CHUNK - 1)) // ROW_BLOCK) * ROW_BLOCK
    max_blocks = (TOP_K * rows + nt * ne * (CHUNK - 1)) // ROW_BLOCK + ne
    wr_pad = jnp.pad(wr, ((0, 0), (0, LANES - ne))).astype(BF16)
    x3, cm, tmeta, xs = _moe_route(x2, o, wo, gffn, wr_pad, tm=tm, ne=ne, cap=cap)
    tmeta = tmeta.reshape(nt, 8, LANES)[:, :, :ne]
    cpad = tmeta[:, 0, :].reshape(nt * ne)
    seg = (tmeta[:, 1, :] + jnp.arange(ne, dtype=jnp.int32)[None, :] * cap).reshape(nt * ne)
    nrows = tmeta[nt - 1, 2, :]
    act = _moe_up(_expert_steps(nrows, nf, max_blocks), xs, wgu, fc=fc, cap=cap, max_blocks=max_blocks)
    ys = _moe_down(_expert_steps(nrows, 1, max_blocks), act, wd, cap=cap, max_blocks=max_blocks)
    return _moe_combine(cpad, seg, x3, cm, gfin, ys, tm=tm, ne=ne)


def _largest_divisor(n, cap, mult):
    best = None
    for t in range(mult, cap + 1, mult):
        if n % t == 0:
            best = t
    assert best is not None, (n, cap, mult)
    return best


def kernel(x_prompt, x_sample, state_ssm_re, state_ssm_im, cache_k_win, cache_v_win, g_mix, g_ffn, g_kv, g_final, ssm_a_re, ssm_a_im, ssm_log_dt, ssm_b_re, ssm_b_im, ssm_c_re, ssm_c_im, ssm_d, w_glu, w_kv, w_q, w_o, attn_sinks, rel_bias, w_ffn_gate_up, w_ffn_down, w_router, w_exp_gate_up, w_exp_down):
    bsz, seq, d = x_prompt.shape
    ns, dec_seq, _ = x_sample.shape
    assert dec_seq == 1 and g_mix.shape[0] == 2 and ssm_a_re.shape[0] == 1 and w_q.shape[0] == 1
    _, g, p = ssm_a_re.shape
    hch = d // g
    gp = g * p
    window, kvh, hd = cache_k_win.shape[1:]
    kvw = kvh * hd
    nh = attn_sinks.shape[1]
    rep = nh // kvh
    nq = nh * hd
    assert bsz == 8 and ns % bsz == 0 and seq % window == 0 and LANES % hd == 0

    gpb = MXU_DIM // hch
    lam_re, lam_im, bb_re, bb_im = _zoh(ssm_a_re[0], ssm_a_im[0], ssm_log_dt[0], ssm_b_re[0], ssm_b_im[0])
    wb_re = _blockdiag_in(bb_re, g, p, gpb)
    wb_im = _blockdiag_in(bb_im, g, p, gpb)
    wc_re = _blockdiag_out(ssm_c_re[0], gpb)
    wc_imn = _blockdiag_out(-ssm_c_im[0], gpb)
    wglu = w_glu[0].astype(BF16)
    d_skip = ssm_d[0].reshape(1, d)
    wgu = w_ffn_gate_up[0].astype(BF16)
    wd = w_ffn_down[0].astype(BF16)
    wkv = w_kv.astype(BF16)
    wq = w_q[0].reshape(d, kvh, rep, hd).transpose(0, 2, 1, 3).reshape(d, nq).astype(BF16)
    wo = w_o[0].reshape(kvh, rep, hd, d).transpose(1, 0, 2, 3).reshape(nq, d).astype(BF16)
    bias, bias_s = _bias_tables(rel_bias, window, kvh, rep)
    sinks = attn_sinks[0]
    sink_s = sinks.reshape(kvh, rep, 1)

    x_tb = x_prompt.transpose(1, 0, 2).reshape(seq * bsz, d)
    x1_p, st_re, st_im = _ssm_prompt(x_tb, bsz, g_mix[0:1], lam_re, lam_im, wb_re, wb_im, wc_re, wc_imn,
                                     d_skip, wglu, lc=64)
    x1_s, hs_re, hs_im = _ssm_sample(x_sample.reshape(ns, d), g_mix[0:1], lam_re, lam_im,
                                     state_ssm_re[0].reshape(ns, gp), state_ssm_im[0].reshape(ns, gp),
                                     wb_re, wb_im, wc_re, wc_imn, d_skip, wglu)
    x1 = jnp.concatenate([x1_p, x1_s], axis=0)
    rows = x1.shape[0]

    x2, kv, q = _ffn(x1, g_ffn[0:1], wgu, wd, g_kv.reshape(1, d), wkv, g_mix[1:2], wq,
                     tm=_largest_divisor(rows, 768, 16))

    rows_b = rows // bsz
    o_p = _attn_prompt(q.reshape(rows_b, bsz * nq), kv.reshape(rows_b, bsz * 2 * kvw), bias, sinks,
                       bsz=bsz, nblk=seq // window, window=window, kvh=kvh, rep=rep, hd=hd)
    o_s, nk_s, nv_s = _attn_sample(q[seq * bsz:].reshape(ns, rep, kvw), kv[seq * bsz:].reshape(ns, 1, 2 * kvw),
                                   cache_k_win.reshape(ns, window, kvw), cache_v_win.reshape(ns, window, kvw),
                                   bias_s, sink_s, nb=16, kvh=kvh, rep=rep, hd=hd)
    o = jnp.concatenate([o_p.reshape(seq * bsz, nq), o_s.reshape(ns, nq)], axis=0)

    y = _moe(x2, o, wo, g_ffn[1:2], w_router[0], w_exp_gate_up[0], w_exp_down[0], g_final.reshape(1, d),
             tm=_largest_divisor(rows, 3 * LANES, LANES), fc=w_exp_down.shape[2] // 2)

    y_prompt = y[:seq * bsz].reshape(seq, bsz, d).transpose(1, 0, 2)
    y_sample = y[seq * bsz:].reshape(ns, 1, d)
    kv_tail = kv[(seq - window) * bsz:seq * bsz].reshape(window, bsz, 2, kvh, hd).transpose(2, 1, 0, 3, 4)
    return (y_prompt, y_sample,
            st_re.reshape(1, bsz, g, p), st_im.reshape(1, bsz, g, p), kv_tail[0], kv_tail[1],
            hs_re.reshape(1, ns, g, p), hs_im.reshape(1, ns, g, p),
            nk_s.reshape(ns, window, kvh, hd), nv_s.reshape(ns, window, kvh, hd))
```

```python
import functools
import math

import numpy as np
import jax
import jax.numpy as jnp
from jax import lax
from jax.experimental import pallas as pl
from jax.experimental.pallas import tpu as pltpu

F32 = jnp.float32
BF16 = jnp.bfloat16

EPS = 1e-6
NEG_INF = -1e30
TOP_K = 2
MAX_DISTANCE = 128
MXU_DIM = 256
LANES = 128
VMEM_LIMIT_BYTES = 56 * 1024 * 1024


def _dot(a, b):
    return jnp.dot(a, b, preferred_element_type=F32)


def _rms(x, g):
    return x * lax.rsqrt(jnp.mean(x * x, axis=-1, keepdims=True) + EPS) * g


def _const_spec(shape):
    nd = len(shape)
    return pl.BlockSpec(shape, lambda *_: (0,) * nd)


def _params(*sem):
    return pltpu.CompilerParams(dimension_semantics=sem, vmem_limit_bytes=VMEM_LIMIT_BYTES)


def _zoh_kernel(a_re_ref, a_im_ref, log_dt_ref, b_re_ref, b_im_ref,
                lam_re_ref, lam_im_ref, bb_re_ref, bb_im_ref):
    a_re = a_re_ref[...]
    a_im = a_im_ref[...]
    dt = jnp.exp(log_dt_ref[...])
    mag = jnp.exp(a_re * dt)
    lr = mag * jnp.cos(a_im * dt)
    li = mag * jnp.sin(a_im * dt)
    lam_re_ref[...] = lr
    lam_im_ref[...] = li
    nr = lr - 1.0
    den = a_re * a_re + a_im * a_im
    qr = (nr * a_re + li * a_im) / den
    qi = (li * a_re - nr * a_im) / den
    b_re = b_re_ref[...]
    b_im = b_im_ref[...]
    bb_re_ref[...] = qr * b_re - qi * b_im
    bb_im_ref[...] = qr * b_im + qi * b_re


def _zoh(a_re, a_im, log_dt, b_re, b_im):
    g, p = a_re.shape
    h = b_re.shape[-1]
    gp = g * p
    row = jax.ShapeDtypeStruct((1, gp), F32)
    mat = jax.ShapeDtypeStruct((h, gp), F32)
    return pl.pallas_call(
        _zoh_kernel,
        out_shape=(row, row, mat, mat),
        name="s5_zoh",
    )(a_re.reshape(1, gp), a_im.reshape(1, gp), jnp.repeat(log_dt, p).reshape(1, gp),
      b_re.transpose(2, 0, 1).reshape(h, gp), b_im.transpose(2, 0, 1).reshape(h, gp))


def _blockdiag_in(bb, g, p, gpb):
    h = bb.shape[0]
    nkb = g // gpb
    bb4 = bb.reshape(h, nkb, gpb, p).transpose(1, 0, 2, 3)
    eye = jnp.eye(gpb, dtype=bb.dtype)
    w = bb4[:, None, :, :, :] * eye[None, :, None, :, None]
    return w.reshape(nkb, gpb * h, gpb * p).astype(BF16)


def _blockdiag_out(c, gpb):
    g, h, p = c.shape
    nkb = g // gpb
    c4 = c.reshape(nkb, gpb, h, p).transpose(0, 1, 3, 2)
    eye = jnp.eye(gpb, dtype=c.dtype)
    w = c4[:, :, :, None, :] * eye[None, :, None, :, None]
    return w.reshape(nkb, gpb * p, gpb * h).astype(BF16)


def _ssm_in_proj(ub, wb_re_ref, wb_im_ref, bu_re, bu_im):
    nkb, ublk, sblk = wb_re_ref.shape
    for kb in range(nkb):
        ukb = ub[:, kb * ublk:(kb + 1) * ublk]
        bu_re[:, kb * sblk:(kb + 1) * sblk] = _dot(ukb, wb_re_ref[kb])
        bu_im[:, kb * sblk:(kb + 1) * sblk] = _dot(ukb, wb_im_ref[kb])


def _ssm_out_proj(h_re, h_im, wc_re_ref, wc_imn_ref):
    nkb, sblk, _ = wc_re_ref.shape
    ys = []
    for kb in range(nkb):
        hr = h_re[:, kb * sblk:(kb + 1) * sblk].astype(BF16)
        hi = h_im[:, kb * sblk:(kb + 1) * sblk].astype(BF16)
        ys.append(_dot(hr, wc_re_ref[kb]) + _dot(hi, wc_imn_ref[kb]))
    return jnp.concatenate(ys, axis=1)


def _ssm_glu(x, u, y, d_ref, wglu_ref):
    z = jax.nn.gelu(y + d_ref[...] * u).astype(BF16)
    gl = _dot(z, wglu_ref[...])
    d = x.shape[1]
    return x + gl[:, :d] * jax.nn.sigmoid(gl[:, d:])


def _split3(v):
    hi = v.astype(BF16)
    r1 = v - hi.astype(F32)
    mid = r1.astype(BF16)
    lo = (r1 - mid.astype(F32)).astype(BF16)
    return hi, mid, lo


def _ssm_prompt_kernel(x_ref, g_ref, lam_re_ref, lam_im_ref, wb_re_ref, wb_im_ref,
                       wc_re_ref, wc_imn_ref, d_ref, wglu_ref, perm_ref, permt_ref,
                       out_ref, st_re_ref, st_im_ref, bu_re, bu_im, *, lc, bsz, lane_chunk):
    @pl.when(pl.program_id(0) == 0)
    def _():
        st_re_ref[...] = jnp.zeros_like(st_re_ref)
        st_im_ref[...] = jnp.zeros_like(st_im_ref)

    d = x_ref.shape[2]
    x = x_ref[...].reshape(bsz * lc, d)
    u = _rms(x, g_ref[...])
    ub_tb = _dot(perm_ref[...], u.astype(BF16)).astype(BF16)
    _ssm_in_proj(ub_tb, wb_re_ref, wb_im_ref, bu_re, bu_im)

    gp = bu_re.shape[1]
    for c0 in range(0, gp, lane_chunk):
        sl = slice(c0, c0 + lane_chunk)
        lre = jnp.broadcast_to(lam_re_ref[:, sl], (bsz, lane_chunk))
        lim = jnp.broadcast_to(lam_im_ref[:, sl], (bsz, lane_chunk))

        def step(t, carry):
            hr, hi = carry
            rows = pl.ds(pl.multiple_of(t * bsz, bsz), bsz)
            nr = lre * hr - lim * hi + bu_re[rows, sl]
            ni = lre * hi + lim * hr + bu_im[rows, sl]
            bu_re[rows, sl] = nr
            bu_im[rows, sl] = ni
            return nr, ni

        hr, hi = lax.fori_loop(0, lc, step, (st_re_ref[:, sl], st_im_ref[:, sl]), unroll=8)
        st_re_ref[:, sl] = hr
        st_im_ref[:, sl] = hi

    permt = permt_ref[...]
    hi, mid, lo = _split3(_ssm_out_proj(bu_re, bu_im, wc_re_ref, wc_imn_ref))
    y = (_dot(permt, hi) + _dot(permt, mid)) + _dot(permt, lo)
    out_ref[...] = _ssm_glu(x, u, y, d_ref, wglu_ref).reshape(bsz, lc, d)


def _ssm_sample_kernel(x_ref, g_ref, lam_re_ref, lam_im_ref, h0_re_ref, h0_im_ref, wb_re_ref, wb_im_ref,
                       wc_re_ref, wc_imn_ref, d_ref, wglu_ref,
                       out_ref, h_re_ref, h_im_ref):
    x = x_ref[...]
    u = _rms(x, g_ref[...])
    _ssm_in_proj(u.astype(BF16), wb_re_ref, wb_im_ref, h_re_ref, h_im_ref)
    lre = lam_re_ref[...]
    lim = lam_im_ref[...]
    h0r = h0_re_ref[...]
    h0i = h0_im_ref[...]
    h_re_ref[...] = lre * h0r - lim * h0i + h_re_ref[...]
    h_im_ref[...] = lre * h0i + lim * h0r + h_im_ref[...]
    y = _ssm_out_proj(h_re_ref, h_im_ref, wc_re_ref, wc_imn_ref)
    n = x.shape[0]
    out_ref[0:n, :] = _ssm_glu(x, u, y, d_ref, wglu_ref)
    out_ref[n:, :] = jnp.zeros((out_ref.shape[0] - n, out_ref.shape[1]), F32)


def _ssm_prompt(x, g, lam_re, lam_im, wb_re, wb_im, wc_re, wc_imn, d_skip, wglu, *, lc):
    bsz, seq, d = x.shape
    gp = lam_re.shape[1]
    r = lc * bsz
    perm = np.zeros((r, r), np.float32)
    for b in range(bsz):
        for t in range(lc):
            perm[t * bsz + b, b * lc + t] = 1.0
    permt = jnp.asarray(perm.T, BF16)
    perm = jnp.asarray(perm, BF16)
    kern = functools.partial(_ssm_prompt_kernel, lc=lc, bsz=bsz, lane_chunk=8 * LANES)
    return pl.pallas_call(
        kern,
        grid=(seq // lc,),
        in_specs=[pl.BlockSpec((bsz, lc, d), lambda c: (0, c, 0)),
                  _const_spec(g.shape), _const_spec(lam_re.shape), _const_spec(lam_im.shape),
                  _const_spec(wb_re.shape), _const_spec(wb_im.shape),
                  _const_spec(wc_re.shape), _const_spec(wc_imn.shape),
                  _const_spec(d_skip.shape), _const_spec(wglu.shape),
                  _const_spec(perm.shape), _const_spec(permt.shape)],
        out_specs=[pl.BlockSpec((bsz, lc, d), lambda c: (0, c, 0)),
                   _const_spec((bsz, gp)), _const_spec((bsz, gp))],
        out_shape=[jax.ShapeDtypeStruct((bsz, seq, d), F32),
                   jax.ShapeDtypeStruct((bsz, gp), F32), jax.ShapeDtypeStruct((bsz, gp), F32)],
        scratch_shapes=[pltpu.VMEM((r, gp), F32), pltpu.VMEM((r, gp), F32)],
        compiler_params=_params("arbitrary"),
        name="s5_prompt",
    )(x, g, lam_re, lam_im, wb_re, wb_im, wc_re, wc_imn, d_skip, wglu, perm, permt)


def _ssm_sample(x, g, lam_re, lam_im, h0_re, h0_im, wb_re, wb_im, wc_re, wc_imn, d_skip, wglu, *, pad_rows):
    n, d = x.shape
    gp = lam_re.shape[1]
    return pl.pallas_call(
        _ssm_sample_kernel,
        out_shape=[jax.ShapeDtypeStruct((pad_rows, d), F32),
                   jax.ShapeDtypeStruct((n, gp), F32), jax.ShapeDtypeStruct((n, gp), F32)],
        compiler_params=_params(),
        name="s5_sample",
    )(x, g, lam_re, lam_im, h0_re, h0_im, wb_re, wb_im, wc_re, wc_imn, d_skip, wglu)


def _ffn_chunks(d_ff):
    step = 3 * MXU_DIM
    return [(c, min(c + step, d_ff)) for c in range(0, d_ff, step)]


def _ffn_kernel(xp_ref, xs_ref, gffn_ref, wgu_ref, wd_ref, gkv_ref, wkv_ref, gq_ref, wq_ref,
                x2_ref, kv_ref, q_ref):
    x = jnp.where(pl.program_id(0) < pl.num_programs(0) - 1, xp_ref[...], xs_ref[...])
    hb = _rms(x, gffn_ref[...]).astype(BF16)
    d_ff = wd_ref.shape[0]
    acc = None
    for c0, c1 in _ffn_chunks(d_ff):
        a = _dot(hb, wgu_ref[:, c0:c1])
        b = _dot(hb, wgu_ref[:, d_ff + c0:d_ff + c1])
        part = _dot((jax.nn.silu(a) * b).astype(BF16), wd_ref[c0:c1, :])
        acc = part if acc is None else acc + part
    x2 = x + acc
    x2_ref[...] = x2
    kv_ref[...] = _dot(_rms(x2, gkv_ref[...]).astype(BF16), wkv_ref[...])
    q_ref[...] = _dot(_rms(x2, gq_ref[...]).astype(BF16), wq_ref[...]).astype(BF16)


def _ffn(xp, xs, gffn, wgu, wd, gkv, wkv, gq, wq):
    tm, d = xs.shape
    npt = xp.shape[0] // tm
    rows = xp.shape[0] + tm
    kvw = wkv.shape[1]
    nq = wq.shape[1]
    return pl.pallas_call(
        _ffn_kernel,
        grid=(npt + 1,),
        in_specs=[pl.BlockSpec((tm, d), lambda i: (jnp.minimum(i, npt - 1), 0)),
                  _const_spec(xs.shape),
                  _const_spec(gffn.shape), _const_spec(wgu.shape), _const_spec(wd.shape),
                  _const_spec(gkv.shape), _const_spec(wkv.shape),
                  _const_spec(gq.shape), _const_spec(wq.shape)],
        out_specs=[pl.BlockSpec((tm, d), lambda i: (i, 0)),
                   pl.BlockSpec((tm, kvw), lambda i: (i, 0)),
                   pl.BlockSpec((tm, nq), lambda i: (i, 0))],
        out_shape=[jax.ShapeDtypeStruct((rows, d), F32),
                   jax.ShapeDtypeStruct((rows, kvw), F32),
                   jax.ShapeDtypeStruct((rows, nq), BF16)],
        compiler_params=_params("arbitrary"),
        name="ffn_kv_q",
    )(xp, xs, gffn, wgu, wd, gkv, wkv, gq, wq)


def _t5_bucket(dist, num_buckets):
    max_exact = num_buckets // 2
    d = jnp.maximum(dist, 0)
    large = max_exact + (jnp.log(jnp.maximum(d, 1).astype(F32) / max_exact)
                         / math.log(MAX_DISTANCE / max_exact) * (num_buckets - max_exact)).astype(jnp.int32)
    large = jnp.minimum(large, num_buckets - 1)
    return jnp.where(d < max_exact, d, large)


def _bias_kernel(bm_ref, bs_ref, rb_ref, bias_ref, bias_s_ref, *, kvh, rep):
    nb, nh = rb_ref.shape
    bm = bm_ref[...]
    bs = bs_ref[...]

    def lookup(buckets, h):
        def body(k, acc):
            return jnp.where(buckets == k, rb_ref[k, h], acc)
        return lax.fori_loop(0, nb, body, jnp.zeros(buckets.shape, F32))

    for h in range(nh):
        bias_ref[h] = lookup(bm, h)
    for g in range(kvh):
        for r in range(rep):
            bias_s_ref[g, r:r + 1, :] = lookup(bs, g * rep + r)


def _bias_tables(rel_bias, window, kvh, rep):
    nb, nh = rel_bias.shape
    qi = jnp.arange(window)[:, None]
    kj = jnp.arange(window)[None, :]
    bm = _t5_bucket((qi - kj) % window, nb).astype(jnp.int32)
    bs = _t5_bucket(window - 1 - kj, nb).astype(jnp.int32)
    kern = functools.partial(_bias_kernel, kvh=kvh, rep=rep)
    return pl.pallas_call(
        kern,
        in_specs=[pl.BlockSpec(memory_space=pltpu.VMEM), pl.BlockSpec(memory_space=pltpu.VMEM),
                  pl.BlockSpec(memory_space=pltpu.SMEM)],
        out_shape=[jax.ShapeDtypeStruct((nh, window, window), F32),
                   jax.ShapeDtypeStruct((kvh, rep, window), F32)],
        name="t5_bias",
    )(bm, bs, rel_bias)


def _attn_prompt_kernel(q_ref, kvc_ref, kvp_ref, bias_ref, sink_ref, o_ref, *, kvh, rep, hd, scale):
    w = q_ref.shape[0]
    kvw = kvh * hd
    first = pl.program_id(1) == 0
    q = q_ref[...]
    kvc = kvc_ref[...]
    kvp = kvp_ref[...]
    lane = lax.broadcasted_iota(jnp.int32, (1, LANES), 1)
    qi = lax.broadcasted_iota(jnp.int32, (w, w), 0)
    kj = lax.broadcasted_iota(jnp.int32, (w, w), 1)
    upper = jnp.concatenate([kj > qi] * rep, axis=0)
    masked = jnp.logical_and(upper, first)
    heads_per_blk = LANES // hd
    o_blks = []
    for p in range(kvw // LANES):
        cs = slice(p * LANES, (p + 1) * LANES)
        kc, kp = kvc[:, cs], kvp[:, cs]
        vs = slice(kvw + p * LANES, kvw + (p + 1) * LANES)
        vb = jnp.concatenate([kvp[:, vs], kvc[:, vs]], axis=0).astype(BF16)
        qg = jnp.concatenate([q[:, r * kvw + p * LANES:r * kvw + (p + 1) * LANES] for r in range(rep)], axis=0)
        o_blk = None
        for half in range(heads_per_blk):
            g = p * heads_per_blk + half
            lmask = jnp.logical_and(lane >= half * hd, lane < (half + 1) * hd)
            kb = jnp.concatenate([jnp.where(lmask, kp, 0.0), jnp.where(lmask, kc, 0.0)], axis=0).astype(BF16)
            s = lax.dot_general(qg, kb, (((1,), (1,)), ((), ())), preferred_element_type=F32)
            bias = bias_ref[g * rep:(g + 1) * rep].reshape(rep * w, w)
            sc = jnp.where(upper, s[:, :w], s[:, w:]) * scale + bias
            sc = jnp.where(masked, NEG_INF, sc)
            sink = jnp.concatenate([jnp.full((w, 1), sink_ref[g * rep + r], F32) for r in range(rep)], axis=0)
            m = jnp.maximum(jnp.max(sc, axis=-1, keepdims=True), sink)
            pe = jnp.exp(sc - m)
            denom = jnp.sum(pe, axis=-1, keepdims=True) + jnp.exp(sink - m)
            pcat = jnp.concatenate([jnp.where(upper, pe, 0.0), jnp.where(upper, 0.0, pe)], axis=1).astype(BF16)
            og = _dot(pcat, vb) / denom
            o_blk = og if o_blk is None else jnp.where(lmask, og, o_blk)
        o_blks.append(o_blk)
    cols = []
    for r in range(rep):
        for p in range(kvw // LANES):
            cols.append(o_blks[p][r * w:(r + 1) * w])
    o_ref[...] = jnp.concatenate(cols, axis=1).astype(BF16)


def _attn_prompt(q, kv, bias, sinks, *, bsz, nblk, window, kvh, rep, hd):
    rows, nq = q.shape
    kvw2 = kv.shape[1]
    kern = functools.partial(_attn_prompt_kernel, kvh=kvh, rep=rep, hd=hd, scale=1.0 / math.sqrt(hd))
    return pl.pallas_call(
        kern,
        grid=(bsz, nblk),
        in_specs=[pl.BlockSpec((window, nq), lambda b, i: (b * nblk + i, 0)),
                  pl.BlockSpec((window, kvw2), lambda b, i: (b * nblk + i, 0)),
                  pl.BlockSpec((window, kvw2), lambda b, i: (b * nblk + jnp.maximum(i - 1, 0), 0)),
                  _const_spec(bias.shape),
                  pl.BlockSpec(memory_space=pltpu.SMEM)],
        out_specs=pl.BlockSpec((window, nq), lambda b, i: (b * nblk + i, 0)),
        out_shape=jax.ShapeDtypeStruct((rows, nq), BF16),
        compiler_params=_params("arbitrary", "arbitrary"),
        name="swa_prompt",
    )(q, kv, kv, bias, sinks)


def _attn_sample_kernel(q_ref, kv_ref, ck_ref, cv_ref, bias_ref, sink_ref, o_ref, nk_ref, nv_ref,
                        *, kvh, rep, hd, scale):
    nb, w, kvw = ck_ref.shape
    kv = kv_ref[...]
    lane = lax.broadcasted_iota(jnp.int32, (nb, rep, kvw), 2)

    def shifted(c_ref, n_ref, new):
        flat = c_ref[...].reshape(nb * w, kvw)
        n_ref[...] = pltpu.roll(flat, nb * w - 1, axis=0).reshape(nb, w, kvw)
        n_ref[:, w - 1:w, :] = new
        return n_ref[...].astype(BF16)

    nkb = shifted(ck_ref, nk_ref, kv[:, :, :kvw])
    nvb = shifted(cv_ref, nv_ref, kv[:, :, kvw:])
    q = q_ref[...].astype(F32)
    o = jnp.zeros((nb, rep, kvw), F32)
    for g in range(kvh):
        lmask = jnp.logical_and(lane >= g * hd, lane < (g + 1) * hd)
        qg = jnp.where(lmask, q, 0.0).astype(BF16)
        s = jnp.einsum("nrc,njc->nrj", qg, nkb, preferred_element_type=F32)
        sc = s * scale + bias_ref[g][None]
        sink = sink_ref[g][None]
        m = jnp.maximum(jnp.max(sc, axis=-1, keepdims=True), sink)
        pe = jnp.exp(sc - m)
        probs = pe / (jnp.sum(pe, axis=-1, keepdims=True) + jnp.exp(sink - m))
        og = jnp.einsum("nrj,njc->nrc", probs.astype(BF16), nvb, preferred_element_type=F32)
        o = jnp.where(lmask, og, o)
    o_ref[...] = o.astype(BF16)


def _attn_sample(q3, kv3, ck, cv, bias_s, sink_s, *, nb, kvh, rep, hd):
    n, w, kvw = ck.shape
    kern = functools.partial(_attn_sample_kernel, kvh=kvh, rep=rep, hd=hd, scale=1.0 / math.sqrt(hd))
    cache_spec = pl.BlockSpec((nb, w, kvw), lambda i: (i, 0, 0))
    return pl.pallas_call(
        kern,
        grid=(n // nb,),
        in_specs=[pl.BlockSpec((nb, rep, kvw), lambda i: (i, 0, 0)),
                  pl.BlockSpec((nb, 1, 2 * kvw), lambda i: (i, 0, 0)),
                  cache_spec, cache_spec,
                  _const_spec(bias_s.shape), _const_spec(sink_s.shape)],
        out_specs=[pl.BlockSpec((nb, rep, kvw), lambda i: (i, 0, 0)), cache_spec, cache_spec],
        out_shape=[jax.ShapeDtypeStruct((n, rep, kvw), BF16),
                   jax.ShapeDtypeStruct((n, w, kvw), F32), jax.ShapeDtypeStruct((n, w, kvw), F32)],
        compiler_params=_params("arbitrary"),
        name="swa_sample",
    )(q3, kv3, ck, cv, bias_s, sink_s)


CHUNK = 16
SUB = 256
ROW_BLOCK = 512
TOKEN_TILE = 512


def _chunk_copy(src, src_row, dst, dst_row, sem):
    return pltpu.make_async_copy(src.at[pl.ds(pl.multiple_of(src_row, CHUNK), CHUNK), :],
                                 dst.at[pl.ds(pl.multiple_of(dst_row, CHUNK), CHUNK), :], sem)


def _moe_route_kernel(x2_ref, o_ref, wo_ref, gffn_ref, wr_ref, lst_ref, ust_ref,
                      x3_ref, cm_ref, tmeta_ref, xs_hbm,
                      comp_s, zero_s, base_v, pend_sm, sem, *, ne, cap):
    i = pl.program_id(0)
    nt = pl.num_programs(0)
    tm = x2_ref.shape[0]
    w = comp_s.shape[1]
    slot = lax.rem(i, 2)

    @pl.when(i == 0)
    def _():
        base_v[...] = jnp.zeros_like(base_v)
        zero_s[...] = jnp.zeros_like(zero_s)

    def drain(n):
        def body(c, carry):
            _chunk_copy(zero_s, 0, xs_hbm, 0, sem).wait()
            return carry
        lax.fori_loop(0, n, body, 0)

    @pl.when(i > 0)
    def _():
        drain(pend_sm[0])

    x3 = x2_ref[...] + _dot(o_ref[...], wo_ref[...])
    x3_ref[...] = x3
    hb = _rms(x3, gffn_ref[...]).astype(BF16)

    lane = lax.broadcasted_iota(jnp.int32, (tm, LANES), 1)
    logits = jnp.where(lane < ne, _dot(hb, wr_ref[...]), -jnp.inf)
    m1 = jnp.max(logits, axis=-1, keepdims=True)
    i1 = jnp.min(jnp.where(logits == m1, lane, LANES), axis=-1, keepdims=True)
    rest = jnp.where(lane == i1, -jnp.inf, logits)
    m2 = jnp.max(rest, axis=-1, keepdims=True)
    i2 = jnp.min(jnp.where(rest == m2, lane, LANES), axis=-1, keepdims=True)
    e2 = jnp.exp(m2 - m1)
    g1 = 1.0 / (1.0 + e2)
    g2 = e2 / (1.0 + e2)

    sel = jnp.where(jnp.logical_or(lane == i1, lane == i2), 1.0, 0.0)
    rank = _dot(lst_ref[...], sel.astype(BF16))
    cnt = jnp.sum(sel, axis=0, keepdims=True)
    cpad = jnp.floor((cnt + (CHUNK - 1)) / CHUNK) * CHUNK
    loff = _dot(jnp.broadcast_to(cpad, (8, LANES)).astype(BF16), ust_ref[...])[0:1]
    dest = loff + rank
    ld1 = jnp.sum(jnp.where(lane == i1, dest, 0.0), axis=-1, keepdims=True)
    ld2 = jnp.sum(jnp.where(lane == i2, dest, 0.0), axis=-1, keepdims=True)
    cm = jnp.where(lane == 0, ld1, jnp.where(lane == 1, ld2, jnp.where(lane == 2, g1, jnp.where(lane == 3, g2, 0.0))))
    cm_ref[...] = cm

    rm = cm.T
    rowi = lax.broadcasted_iota(jnp.int32, (w, tm), 0).astype(F32)
    place = jnp.where(jnp.logical_or(rowi == rm[0:1], rowi == rm[1:2]), 1.0, 0.0).astype(BF16)
    comp_s[slot] = _dot(place, hb).astype(BF16)

    base = base_v[...]
    srow = lax.broadcasted_iota(jnp.int32, (8, LANES), 0)
    tmeta_ref[...] = jnp.where(srow == 0, cpad, jnp.where(srow == 1, base, jnp.where(srow == 2, base + cpad, 0.0))
                               ).astype(jnp.int32)
    base_v[...] = base + cpad

    cpad_i = cpad.astype(jnp.int32)
    loff_i = loff.astype(jnp.int32)
    base_i = base.astype(jnp.int32)
    total = 0
    tails = []
    for e in range(ne):
        n_e = cpad_i[0, e]
        src0 = loff_i[0, e]
        dst0 = base_i[0, e] + e * cap

        def start(c, carry, src0=src0, dst0=dst0):
            _chunk_copy(comp_s.at[slot], src0 + c * CHUNK, xs_hbm, dst0 + c * CHUNK, sem).start()
            return carry

        nchunks = lax.shift_right_logical(n_e, int(math.log2(CHUNK)))
        lax.fori_loop(0, nchunks, start, 0)
        total = total + nchunks
        tails.append(dst0 + n_e)
    pend_sm[0] = total

    @pl.when(i == nt - 1)
    def _():
        nz_total = 0
        for e in range(ne):
            end = tails[e]
            nz = lax.shift_right_logical(lax.rem(SUB - lax.rem(end, SUB), SUB), int(math.log2(CHUNK)))

            def zstart(c, carry, end=end):
                _chunk_copy(zero_s, 0, xs_hbm, end + c * CHUNK, sem).start()
                return carry

            lax.fori_loop(0, nz, zstart, 0)
            nz_total = nz_total + nz
        drain(total + nz_total)


def _moe_route(x2, o, wo, gffn, wr_pad, *, tm, ne, cap):
    rows, d = x2.shape
    nt = rows // tm
    w = TOP_K * tm + LANES
    lst = jnp.asarray(np.tril(np.ones((tm, tm), np.float32), -1), BF16)
    ust = jnp.asarray(np.triu(np.ones((LANES, LANES), np.float32), 1), BF16)
    kern = functools.partial(_moe_route_kernel, ne=ne, cap=cap)
    return pl.pallas_call(
        kern,
        grid=(nt,),
        in_specs=[pl.BlockSpec((tm, d), lambda i: (i, 0)),
                  pl.BlockSpec((tm, o.shape[1]), lambda i: (i, 0)),
                  _const_spec(wo.shape), _const_spec(gffn.shape), _const_spec(wr_pad.shape),
                  _const_spec(lst.shape), _const_spec(ust.shape)],
        out_specs=[pl.BlockSpec((tm, d), lambda i: (i, 0)),
                   pl.BlockSpec((tm, LANES), lambda i: (i, 0)),
                   pl.BlockSpec((8, LANES), lambda i: (i, 0)),
                   pl.BlockSpec(memory_space=pl.ANY)],
        out_shape=[jax.ShapeDtypeStruct((rows, d), F32),
                   jax.ShapeDtypeStruct((rows, LANES), F32),
                   jax.ShapeDtypeStruct((nt * 8, LANES), jnp.int32),
                   jax.ShapeDtypeStruct((ne * cap, d), BF16)],
        scratch_shapes=[pltpu.VMEM((2, w, d), BF16), pltpu.VMEM((CHUNK, d), BF16),
                        pltpu.VMEM((1, LANES), F32), pltpu.SMEM((1,), jnp.int32),
                        pltpu.SemaphoreType.DMA(())],
        compiler_params=_params("arbitrary"),
        name="moe_route",
    )(x2, o, wo, gffn, wr_pad, lst, ust)


def _expert_steps(nrows, nf, max_blocks):
    ne = nrows.shape[0]
    nblk = (nrows + ROW_BLOCK - 1) // ROW_BLOCK
    cum = jnp.cumsum(nblk) * nf
    total = cum[-1]
    s = jnp.minimum(jnp.arange(max_blocks * nf, dtype=jnp.int32), total - 1)
    e = jnp.minimum(jnp.sum((s[:, None] >= cum[None, :]).astype(jnp.int32), axis=1), ne - 1)
    nb_e = nblk[e]
    within = s - (cum[e] - nb_e * nf)
    f = within // nb_e
    r = within - f * nb_e
    nsub = jnp.clip((nrows[e] - r * ROW_BLOCK + SUB - 1) // SUB, 0, ROW_BLOCK // SUB)
    first = (r == 0).astype(jnp.int32)
    return (e, f.astype(jnp.int32), r.astype(jnp.int32), first, nsub.astype(jnp.int32),
            jnp.reshape(total, (1,)).astype(jnp.int32))


def _moe_up_kernel(e_ref, f_ref, r_ref, first_ref, nsub_ref, n_ref, xs_ref, wg_ref, wu_ref, act_ref, wgb_s, wub_s):
    s = pl.program_id(0)

    @pl.when(s < n_ref[0])
    def _():
        @pl.when(first_ref[s] == 1)
        def _():
            wgb_s[...] = wg_ref[...].astype(BF16)
            wub_s[...] = wu_ref[...].astype(BF16)

        def body(j, carry):
            rows = pl.ds(pl.multiple_of(j * SUB, SUB), SUB)
            xsb = xs_ref[rows, :]
            a = _dot(xsb, wgb_s[...])
            b = _dot(xsb, wub_s[...])
            act_ref[rows, :] = (jax.nn.silu(a) * b).astype(BF16)
            return carry

        lax.fori_loop(0, nsub_ref[s], body, 0)


def _moe_up(steps, xs, wgu, *, fc, cap, max_blocks):
    ne, d, dff2 = wgu.shape
    dff = dff2 // 2
    nf = dff // fc
    cb = cap // ROW_BLOCK
    grid_spec = pltpu.PrefetchScalarGridSpec(
        num_scalar_prefetch=6,
        grid=(max_blocks * nf,),
        in_specs=[pl.BlockSpec((ROW_BLOCK, d), lambda s, e, f, r, *_: (e[s] * cb + r[s], 0)),
                  pl.BlockSpec((None, d, fc), lambda s, e, f, r, *_: (e[s], 0, f[s])),
                  pl.BlockSpec((None, d, fc), lambda s, e, f, r, *_: (e[s], 0, nf + f[s]))],
        out_specs=pl.BlockSpec((ROW_BLOCK, fc), lambda s, e, f, r, *_: (e[s] * cb + r[s], f[s])),
        scratch_shapes=[pltpu.VMEM((d, fc), BF16), pltpu.VMEM((d, fc), BF16)])
    return pl.pallas_call(
        _moe_up_kernel,
        grid_spec=grid_spec,
        out_shape=jax.ShapeDtypeStruct((ne * cap, dff), BF16),
        compiler_params=_params("arbitrary"),
        name="moe_up",
    )(*steps, xs, wgu, wgu)


def _moe_down_kernel(e_ref, f_ref, r_ref, first_ref, nsub_ref, n_ref, act_ref, wd_ref, ys_ref, wdb_s):
    s = pl.program_id(0)

    @pl.when(s < n_ref[0])
    def _():
        @pl.when(first_ref[s] == 1)
        def _():
            wdb_s[...] = wd_ref[...].astype(BF16)

        def body(j, carry):
            rows = pl.ds(pl.multiple_of(j * SUB, SUB), SUB)
            ys_ref[rows, :] = _dot(act_ref[rows, :], wdb_s[...]).astype(BF16)
            return carry

        lax.fori_loop(0, nsub_ref[s], body, 0)


def _moe_down(steps, act, wd, *, cap, max_blocks):
    ne, dff, d = wd.shape
    cb = cap // ROW_BLOCK
    grid_spec = pltpu.PrefetchScalarGridSpec(
        num_scalar_prefetch=6,
        grid=(max_blocks,),
        in_specs=[pl.BlockSpec((ROW_BLOCK, dff), lambda s, e, f, r, *_: (e[s] * cb + r[s], 0)),
                  pl.BlockSpec((None, dff, d), lambda s, e, f, r, *_: (e[s], 0, 0))],
        out_specs=pl.BlockSpec((ROW_BLOCK, d), lambda s, e, f, r, *_: (e[s] * cb + r[s], 0)),
        scratch_shapes=[pltpu.VMEM((dff, d), BF16)])
    return pl.pallas_call(
        _moe_down_kernel,
        grid_spec=grid_spec,
        out_shape=jax.ShapeDtypeStruct((ne * cap, d), BF16),
        compiler_params=_params("arbitrary"),
        name="moe_down",
    )(*steps, act, wd)


def _moe_combine_kernel(cpad_ref, seg_ref, x3_ref, cm_ref, gfin_ref, ys_hbm, yp_ref, ys_ref, yloc_s, sem, *, ne):
    i = pl.program_id(0)
    nt = pl.num_programs(0)
    tm = x3_ref.shape[0]
    w = yloc_s.shape[1]
    slot = lax.rem(i, 2)
    shift = int(math.log2(CHUNK))

    def issue(t, sl):
        off = 0
        for e in range(ne):
            n_e = cpad_ref[t * ne + e]
            src0 = seg_ref[t * ne + e]

            def start(c, carry, src0=src0, off=off):
                _chunk_copy(ys_hbm, src0 + c * CHUNK, yloc_s.at[sl], off + c * CHUNK, sem.at[sl]).start()
                return carry

            lax.fori_loop(0, lax.shift_right_logical(n_e, shift), start, 0)
            off = off + n_e

    @pl.when(i == 0)
    def _():
        yloc_s[...] = jnp.zeros_like(yloc_s)
        issue(0, 0)

    @pl.when(i + 1 < nt)
    def _():
        issue(i + 1, 1 - slot)

    total = 0
    for e in range(ne):
        total = total + lax.shift_right_logical(cpad_ref[i * ne + e], shift)

    def wait(c, carry):
        _chunk_copy(ys_hbm, 0, yloc_s.at[slot], 0, sem.at[slot]).wait()
        return carry

    lax.fori_loop(0, total, wait, 0)

    cm = cm_ref[...]
    col = lax.broadcasted_iota(jnp.int32, (tm, w), 1).astype(F32)
    yl = yloc_s[slot]
    y1 = _dot(jnp.where(col == cm[:, 0:1], 1.0, 0.0).astype(BF16), yl)
    y2 = _dot(jnp.where(col == cm[:, 1:2], 1.0, 0.0).astype(BF16), yl)
    y = _rms(x3_ref[...] + cm[:, 2:3] * y1 + cm[:, 3:4] * y2, gfin_ref[...])

    @pl.when(i < nt - 1)
    def _():
        yp_ref[...] = y

    @pl.when(i == nt - 1)
    def _():
        ys_ref[...] = y


def _moe_combine(cpad, seg, x3, cm, gfin, ys, *, tm, ne):
    rows, d = x3.shape
    npt = rows // tm - 1
    w = TOP_K * tm + LANES
    kern = functools.partial(_moe_combine_kernel, ne=ne)
    grid_spec = pltpu.PrefetchScalarGridSpec(
        num_scalar_prefetch=2,
        grid=(npt + 1,),
        in_specs=[pl.BlockSpec((tm, d), lambda i, *_: (i, 0)),
                  pl.BlockSpec((tm, LANES), lambda i, *_: (i, 0)),
                  pl.BlockSpec(gfin.shape, lambda i, *_: (0, 0)),
                  pl.BlockSpec(memory_space=pl.ANY)],
        out_specs=[pl.BlockSpec((tm, d), lambda i, *_: (jnp.minimum(i, npt - 1), 0)),
                   pl.BlockSpec((tm, d), lambda i, *_: (0, 0))],
        scratch_shapes=[pltpu.VMEM((2, w, d), BF16), pltpu.SemaphoreType.DMA((2,))])
    return pl.pallas_call(
        kern,
        grid_spec=grid_spec,
        out_shape=[jax.ShapeDtypeStruct((npt * tm, d), F32), jax.ShapeDtypeStruct((tm, d), F32)],
        compiler_params=_params("arbitrary"),
        name="moe_combine",
    )(cpad, seg, x3, cm, gfin, ys)


def _moe(x2, o, wo, gffn, wr, wgu, wd, gfin, *, tm, fc):
    rows, d = x2.shape
    ne = wr.shape[1]
    nt = rows // tm
    nf = wgu.shape[2] // 2 // fc
    cap = -(-(rows + nt * (CHUNK - 1)) // ROW_BLOCK) * ROW_BLOCK
    max_blocks = (TOP_K * rows + nt * ne * (CHUNK - 1)) // ROW_BLOCK + ne
    wr_pad = jnp.pad(wr, ((0, 0), (0, LANES - ne))).astype(BF16)
    x3, cm, tmeta, xs = _moe_route(x2, o, wo, gffn, wr_pad, tm=tm, ne=ne, cap=cap)
    tmeta = tmeta.reshape(nt, 8, LANES)[:, :, :ne]
    cpad = tmeta[:, 0, :].reshape(nt * ne)
    seg = (tmeta[:, 1, :] + jnp.arange(ne, dtype=jnp.int32)[None, :] * cap).reshape(nt * ne)
    nrows = tmeta[nt - 1, 2, :]
    act = _moe_up(_expert_steps(nrows, nf, max_blocks), xs, wgu, fc=fc, cap=cap, max_blocks=max_blocks)
    ys = _moe_down(_expert_steps(nrows, 1, max_blocks), act, wd, cap=cap, max_blocks=max_blocks)
    return _moe_combine(cpad, seg, x3, cm, gfin, ys, tm=tm, ne=ne)


def kernel(x_prompt, x_sample, state_ssm_re, state_ssm_im, cache_k_win, cache_v_win, g_mix, g_ffn, g_kv, g_final, ssm_a_re, ssm_a_im, ssm_log_dt, ssm_b_re, ssm_b_im, ssm_c_re, ssm_c_im, ssm_d, w_glu, w_kv, w_q, w_o, attn_sinks, rel_bias, w_ffn_gate_up, w_ffn_down, w_router, w_exp_gate_up, w_exp_down):
    bsz, seq, d = x_prompt.shape
    ns, dec_seq, _ = x_sample.shape
    assert dec_seq == 1 and g_mix.shape[0] == 2 and ssm_a_re.shape[0] == 1 and w_q.shape[0] == 1
    _, g, p = ssm_a_re.shape
    hch = d // g
    gp = g * p
    window, kvh, hd = cache_k_win.shape[1:]
    kvw = kvh * hd
    nh = attn_sinks.shape[1]
    rep = nh // kvh
    nq = nh * hd
    assert bsz == 8 and ns % bsz == 0 and seq % window == 0 and LANES % hd == 0

    gpb = MXU_DIM // hch
    lam_re, lam_im, bb_re, bb_im = _zoh(ssm_a_re[0], ssm_a_im[0], ssm_log_dt[0], ssm_b_re[0], ssm_b_im[0])
    wb_re = _blockdiag_in(bb_re, g, p, gpb)
    wb_im = _blockdiag_in(bb_im, g, p, gpb)
    wc_re = _blockdiag_out(ssm_c_re[0], gpb)
    wc_imn = _blockdiag_out(-ssm_c_im[0], gpb)
    wglu = w_glu[0].astype(BF16)
    d_skip = ssm_d[0].reshape(1, d)
    wgu = w_ffn_gate_up[0].astype(BF16)
    wd = w_ffn_down[0].astype(BF16)
    wkv = w_kv.astype(BF16)
    wq = w_q[0].reshape(d, kvh, rep, hd).transpose(0, 2, 1, 3).reshape(d, nq).astype(BF16)
    wo = w_o[0].reshape(kvh, rep, hd, d).transpose(1, 0, 2, 3).reshape(nq, d).astype(BF16)
    bias, bias_s = _bias_tables(rel_bias, window, kvh, rep)
    sinks = attn_sinks[0]
    sink_s = sinks.reshape(kvh, rep, 1)

    tm = TOKEN_TILE
    npr = seq * bsz
    assert npr % tm == 0 and ns <= tm

    x1_p, st_re, st_im = _ssm_prompt(x_prompt, g_mix[0:1], lam_re, lam_im, wb_re, wb_im, wc_re, wc_imn,
                                     d_skip, wglu, lc=32)
    x1_s, hs_re, hs_im = _ssm_sample(x_sample.reshape(ns, d), g_mix[0:1], lam_re, lam_im,
                                     state_ssm_re[0].reshape(ns, gp), state_ssm_im[0].reshape(ns, gp),
                                     wb_re, wb_im, wc_re, wc_imn, d_skip, wglu, pad_rows=tm)

    x2, kv, q = _ffn(x1_p.reshape(npr, d), x1_s, g_ffn[0:1], wgu, wd, g_kv.reshape(1, d), wkv, g_mix[1:2], wq)

    o = _attn_prompt(q, kv, bias, sinks, bsz=bsz, nblk=seq // window, window=window, kvh=kvh, rep=rep, hd=hd)
    o_s, nk_s, nv_s = _attn_sample(q[npr:npr + ns].reshape(ns, rep, kvw), kv[npr:npr + ns].reshape(ns, 1, 2 * kvw),
                                   cache_k_win.reshape(ns, window, kvw), cache_v_win.reshape(ns, window, kvw),
                                   bias_s, sink_s, nb=16, kvh=kvh, rep=rep, hd=hd)
    o = lax.dynamic_update_slice(o, jnp.pad(o_s.reshape(ns, nq), ((0, tm - ns), (0, 0))), (npr, 0))

    y_p, y_s = _moe(x2, o, wo, g_ffn[1:2], w_router[0], w_exp_gate_up[0], w_exp_down[0], g_final.reshape(1, d),
                    tm=tm, fc=w_exp_down.shape[2] // 2)

    y_prompt = y_p.reshape(bsz, seq, d)
    y_sample = y_s[:ns].reshape(ns, 1, d)
    kv_tail = jnp.stack([kv[(b + 1) * seq - window:(b + 1) * seq] for b in range(bsz)])
    kv_tail = kv_tail.reshape(bsz, window, 2, kvh, hd).transpose(2, 0, 1, 3, 4)
    return (y_prompt, y_sample,
            st_re.reshape(1, bsz, g, p), st_im.reshape(1, bsz, g, p), kv_tail[0], kv_tail[1],
            hs_re.reshape(1, ns, g, p), hs_im.reshape(1, ns, g, p),
            nk_s.reshape(ns, window, kvh, hd), nv_s.reshape(ns, window, kvh, hd))
```

```python
import functools
import math

import numpy as np
import jax
import jax.numpy as jnp
from jax import lax
from jax.experimental import pallas as pl
from jax.experimental.pallas import tpu as pltpu

F32 = jnp.float32
BF16 = jnp.bfloat16

EPS = 1e-6
NEG_INF = -1e30
TOP_K = 2
MAX_DISTANCE = 128
MXU_DIM = 256
LANES = 128
VMEM_LIMIT_BYTES = 56 * 1024 * 1024


def _dot(a, b):
    return jnp.dot(a, b, preferred_element_type=F32)


def _rms(x, g):
    return x * lax.rsqrt(jnp.mean(x * x, axis=-1, keepdims=True) + EPS) * g


def _const_spec(shape):
    nd = len(shape)
    return pl.BlockSpec(shape, lambda *_: (0,) * nd)


def _params(*sem):
    return pltpu.CompilerParams(dimension_semantics=sem, vmem_limit_bytes=VMEM_LIMIT_BYTES)


def _zoh_kernel(a_re_ref, a_im_ref, log_dt_ref, b_re_ref, b_im_ref,
                lam_re_ref, lam_im_ref, bb_re_ref, bb_im_ref):
    a_re = a_re_ref[...]
    a_im = a_im_ref[...]
    dt = jnp.exp(log_dt_ref[...])
    mag = jnp.exp(a_re * dt)
    lr = mag * jnp.cos(a_im * dt)
    li = mag * jnp.sin(a_im * dt)
    lam_re_ref[...] = lr
    lam_im_ref[...] = li
    nr = lr - 1.0
    den = a_re * a_re + a_im * a_im
    qr = (nr * a_re + li * a_im) / den
    qi = (li * a_re - nr * a_im) / den
    b_re = b_re_ref[...]
    b_im = b_im_ref[...]
    bb_re_ref[...] = qr * b_re - qi * b_im
    bb_im_ref[...] = qr * b_im + qi * b_re


def _zoh(a_re, a_im, log_dt, b_re, b_im):
    g, p = a_re.shape
    h = b_re.shape[-1]
    gp = g * p
    row = jax.ShapeDtypeStruct((1, gp), F32)
    mat = jax.ShapeDtypeStruct((h, gp), F32)
    return pl.pallas_call(
        _zoh_kernel,
        out_shape=(row, row, mat, mat),
        name="s5_zoh",
    )(a_re.reshape(1, gp), a_im.reshape(1, gp), jnp.repeat(log_dt, p).reshape(1, gp),
      b_re.transpose(2, 0, 1).reshape(h, gp), b_im.transpose(2, 0, 1).reshape(h, gp))


def _blockdiag_in(bb, g, p, gpb):
    h = bb.shape[0]
    nkb = g // gpb
    bb4 = bb.reshape(h, nkb, gpb, p).transpose(1, 0, 2, 3)
    eye = jnp.eye(gpb, dtype=bb.dtype)
    w = bb4[:, None, :, :, :] * eye[None, :, None, :, None]
    return w.reshape(nkb, gpb * h, gpb * p).astype(BF16)


def _blockdiag_out(c, gpb):
    g, h, p = c.shape
    nkb = g // gpb
    c4 = c.reshape(nkb, gpb, h, p).transpose(0, 1, 3, 2)
    eye = jnp.eye(gpb, dtype=c.dtype)
    w = c4[:, :, :, None, :] * eye[None, :, None, :, None]
    return w.reshape(nkb, gpb * p, gpb * h).astype(BF16)


def _ssm_in_proj(ub, wb_re_ref, wb_im_ref, bu_re, bu_im):
    nkb, ublk, sblk = wb_re_ref.shape
    for kb in range(nkb):
        ukb = ub[:, kb * ublk:(kb + 1) * ublk]
        bu_re[:, kb * sblk:(kb + 1) * sblk] = _dot(ukb, wb_re_ref[kb])
        bu_im[:, kb * sblk:(kb + 1) * sblk] = _dot(ukb, wb_im_ref[kb])


def _ssm_out_proj(h_re, h_im, wc_re_ref, wc_imn_ref):
    nkb, sblk, _ = wc_re_ref.shape
    ys = []
    for kb in range(nkb):
        hr = h_re[:, kb * sblk:(kb + 1) * sblk].astype(BF16)
        hi = h_im[:, kb * sblk:(kb + 1) * sblk].astype(BF16)
        ys.append(_dot(hr, wc_re_ref[kb]) + _dot(hi, wc_imn_ref[kb]))
    return jnp.concatenate(ys, axis=1)


def _ssm_glu(x, u, y, d_ref, wglu_ref):
    z = jax.nn.gelu(y + d_ref[...] * u).astype(BF16)
    gl = _dot(z, wglu_ref[...])
    d = x.shape[1]
    return x + gl[:, :d] * jax.nn.sigmoid(gl[:, d:])


def _split3(v):
    hi = v.astype(BF16)
    r1 = v - hi.astype(F32)
    mid = r1.astype(BF16)
    lo = (r1 - mid.astype(F32)).astype(BF16)
    return hi, mid, lo


def _ssm_prompt_kernel(x_ref, g_ref, lam_re_ref, lam_im_ref, wb_re_ref, wb_im_ref,
                       wc_re_ref, wc_imn_ref, d_ref, wglu_ref, perm_ref, permt_ref,
                       out_ref, st_re_ref, st_im_ref, bu_re, bu_im, *, lc, bsz, lane_chunk):
    @pl.when(pl.program_id(0) == 0)
    def _():
        st_re_ref[...] = jnp.zeros_like(st_re_ref)
        st_im_ref[...] = jnp.zeros_like(st_im_ref)

    d = x_ref.shape[2]
    x = x_ref[...].reshape(bsz * lc, d)
    u = _rms(x, g_ref[...])
    ub_tb = _dot(perm_ref[...], u.astype(BF16)).astype(BF16)
    _ssm_in_proj(ub_tb, wb_re_ref, wb_im_ref, bu_re, bu_im)

    gp = bu_re.shape[1]
    for c0 in range(0, gp, lane_chunk):
        sl = slice(c0, c0 + lane_chunk)
        lre = jnp.broadcast_to(lam_re_ref[:, sl], (bsz, lane_chunk))
        lim = jnp.broadcast_to(lam_im_ref[:, sl], (bsz, lane_chunk))

        def step(t, carry):
            hr, hi = carry
            rows = pl.ds(pl.multiple_of(t * bsz, bsz), bsz)
            nr = lre * hr - lim * hi + bu_re[rows, sl]
            ni = lre * hi + lim * hr + bu_im[rows, sl]
            bu_re[rows, sl] = nr
            bu_im[rows, sl] = ni
            return nr, ni

        hr, hi = lax.fori_loop(0, lc, step, (st_re_ref[:, sl], st_im_ref[:, sl]), unroll=8)
        st_re_ref[:, sl] = hr
        st_im_ref[:, sl] = hi

    permt = permt_ref[...]
    hi, mid, lo = _split3(_ssm_out_proj(bu_re, bu_im, wc_re_ref, wc_imn_ref))
    y = (_dot(permt, hi) + _dot(permt, mid)) + _dot(permt, lo)
    out_ref[...] = _ssm_glu(x, u, y, d_ref, wglu_ref).reshape(bsz, lc, d)


def _ssm_sample_kernel(x_ref, g_ref, lam_re_ref, lam_im_ref, h0_re_ref, h0_im_ref, wb_re_ref, wb_im_ref,
                       wc_re_ref, wc_imn_ref, d_ref, wglu_ref,
                       out_ref, h_re_ref, h_im_ref):
    x = x_ref[...]
    u = _rms(x, g_ref[...])
    _ssm_in_proj(u.astype(BF16), wb_re_ref, wb_im_ref, h_re_ref, h_im_ref)
    lre = lam_re_ref[...]
    lim = lam_im_ref[...]
    h0r = h0_re_ref[...]
    h0i = h0_im_ref[...]
    h_re_ref[...] = lre * h0r - lim * h0i + h_re_ref[...]
    h_im_ref[...] = lre * h0i + lim * h0r + h_im_ref[...]
    y = _ssm_out_proj(h_re_ref, h_im_ref, wc_re_ref, wc_imn_ref)
    n = x.shape[0]
    out_ref[0:n, :] = _ssm_glu(x, u, y, d_ref, wglu_ref)
    out_ref[n:, :] = jnp.zeros((out_ref.shape[0] - n, out_ref.shape[1]), F32)


def _ssm_prompt(x, g, lam_re, lam_im, wb_re, wb_im, wc_re, wc_imn, d_skip, wglu, *, lc):
    bsz, seq, d = x.shape
    gp = lam_re.shape[1]
    r = lc * bsz
    perm = np.zeros((r, r), np.float32)
    for b in range(bsz):
        for t in range(lc):
            perm[t * bsz + b, b * lc + t] = 1.0
    permt = jnp.asarray(perm.T, BF16)
    perm = jnp.asarray(perm, BF16)
    kern = functools.partial(_ssm_prompt_kernel, lc=lc, bsz=bsz, lane_chunk=8 * LANES)
    return pl.pallas_call(
        kern,
        grid=(seq // lc,),
        in_specs=[pl.BlockSpec((bsz, lc, d), lambda c: (0, c, 0)),
                  _const_spec(g.shape), _const_spec(lam_re.shape), _const_spec(lam_im.shape),
                  _const_spec(wb_re.shape), _const_spec(wb_im.shape),
                  _const_spec(wc_re.shape), _const_spec(wc_imn.shape),
                  _const_spec(d_skip.shape), _const_spec(wglu.shape),
                  _const_spec(perm.shape), _const_spec(permt.shape)],
        out_specs=[pl.BlockSpec((bsz, lc, d), lambda c: (0, c, 0)),
                   _const_spec((bsz, gp)), _const_spec((bsz, gp))],
        out_shape=[jax.ShapeDtypeStruct((bsz, seq, d), F32),
                   jax.ShapeDtypeStruct((bsz, gp), F32), jax.ShapeDtypeStruct((bsz, gp), F32)],
        scratch_shapes=[pltpu.VMEM((r, gp), F32), pltpu.VMEM((r, gp), F32)],
        compiler_params=_params("arbitrary"),
        name="s5_prompt",
    )(x, g, lam_re, lam_im, wb_re, wb_im, wc_re, wc_imn, d_skip, wglu, perm, permt)


def _ssm_sample(x, g, lam_re, lam_im, h0_re, h0_im, wb_re, wb_im, wc_re, wc_imn, d_skip, wglu, *, pad_rows):
    n, d = x.shape
    gp = lam_re.shape[1]
    return pl.pallas_call(
        _ssm_sample_kernel,
        out_shape=[jax.ShapeDtypeStruct((pad_rows, d), F32),
                   jax.ShapeDtypeStruct((n, gp), F32), jax.ShapeDtypeStruct((n, gp), F32)],
        compiler_params=_params(),
        name="s5_sample",
    )(x, g, lam_re, lam_im, h0_re, h0_im, wb_re, wb_im, wc_re, wc_imn, d_skip, wglu)


def _ffn_chunks(d_ff):
    step = 3 * MXU_DIM
    return [(c, min(c + step, d_ff)) for c in range(0, d_ff, step)]


def _ffn_kernel(xp_ref, xs_ref, gffn_ref, wgu_ref, wd_ref, gkv_ref, wkv_ref, gq_ref, wq_ref,
                x2_ref, kv_ref, q_ref):
    x = jnp.where(pl.program_id(0) < pl.num_programs(0) - 1, xp_ref[...], xs_ref[...])
    hb = _rms(x, gffn_ref[...]).astype(BF16)
    d_ff = wd_ref.shape[0]
    acc = None
    for c0, c1 in _ffn_chunks(d_ff):
        a = _dot(hb, wgu_ref[:, c0:c1])
        b = _dot(hb, wgu_ref[:, d_ff + c0:d_ff + c1])
        part = _dot((jax.nn.silu(a) * b).astype(BF16), wd_ref[c0:c1, :])
        acc = part if acc is None else acc + part
    x2 = x + acc
    x2_ref[...] = x2
    kv_ref[...] = _dot(_rms(x2, gkv_ref[...]).astype(BF16), wkv_ref[...])
    q_ref[...] = _dot(_rms(x2, gq_ref[...]).astype(BF16), wq_ref[...]).astype(BF16)


def _ffn(xp, xs, gffn, wgu, wd, gkv, wkv, gq, wq):
    tm, d = xs.shape
    npt = xp.shape[0] // tm
    rows = xp.shape[0] + tm
    kvw = wkv.shape[1]
    nq = wq.shape[1]
    return pl.pallas_call(
        _ffn_kernel,
        grid=(npt + 1,),
        in_specs=[pl.BlockSpec((tm, d), lambda i: (jnp.minimum(i, npt - 1), 0)),
                  _const_spec(xs.shape),
                  _const_spec(gffn.shape), _const_spec(wgu.shape), _const_spec(wd.shape),
                  _const_spec(gkv.shape), _const_spec(wkv.shape),
                  _const_spec(gq.shape), _const_spec(wq.shape)],
        out_specs=[pl.BlockSpec((tm, d), lambda i: (i, 0)),
                   pl.BlockSpec((tm, kvw), lambda i: (i, 0)),
                   pl.BlockSpec((tm, nq), lambda i: (i, 0))],
        out_shape=[jax.ShapeDtypeStruct((rows, d), F32),
                   jax.ShapeDtypeStruct((rows, kvw), F32),
                   jax.ShapeDtypeStruct((rows, nq), BF16)],
        compiler_params=_params("arbitrary"),
        name="ffn_kv_q",
    )(xp, xs, gffn, wgu, wd, gkv, wkv, gq, wq)


def _t5_bucket(dist, num_buckets):
    max_exact = num_buckets // 2
    d = jnp.maximum(dist, 0)
    large = max_exact + (jnp.log(jnp.maximum(d, 1).astype(F32) / max_exact)
                         / math.log(MAX_DISTANCE / max_exact) * (num_buckets - max_exact)).astype(jnp.int32)
    large = jnp.minimum(large, num_buckets - 1)
    return jnp.where(d < max_exact, d, large)


def _bias_kernel(bm_ref, bs_ref, rb_ref, bias_ref, bias_s_ref, *, kvh, rep):
    nb, nh = rb_ref.shape
    bm = bm_ref[...]
    bs = bs_ref[...]

    def lookup(buckets, h):
        def body(k, acc):
            return jnp.where(buckets == k, rb_ref[k, h], acc)
        return lax.fori_loop(0, nb, body, jnp.zeros(buckets.shape, F32))

    w = bm.shape[0]
    for g in range(kvh):
        for r in range(rep):
            bias_ref[g, :, r * w:(r + 1) * w] = lookup(bm, g * rep + r)
            bias_s_ref[g, r:r + 1, :] = lookup(bs, g * rep + r)


def _bias_tables(rel_bias, window, kvh, rep):
    nb, nh = rel_bias.shape
    key = jnp.arange(window)[:, None]
    qry = jnp.arange(window)[None, :]
    bm = _t5_bucket((qry - key) % window, nb).astype(jnp.int32)
    bs = _t5_bucket(window - 1 - qry, nb).astype(jnp.int32)
    kern = functools.partial(_bias_kernel, kvh=kvh, rep=rep)
    return pl.pallas_call(
        kern,
        in_specs=[pl.BlockSpec(memory_space=pltpu.VMEM), pl.BlockSpec(memory_space=pltpu.VMEM),
                  pl.BlockSpec(memory_space=pltpu.SMEM)],
        out_shape=[jax.ShapeDtypeStruct((kvh, window, rep * window), F32),
                   jax.ShapeDtypeStruct((kvh, rep, window), F32)],
        name="t5_bias",
    )(bm, bs, rel_bias)


def _attn_prompt_kernel(q_ref, kvc_ref, kvp_ref, bias_ref, sink_ref, ot_ref, *, kvh, rep, hd, scale):
    w = q_ref.shape[0]
    kvw = kvh * hd
    first = pl.program_id(1) == 0
    q = q_ref[...]
    kvc = kvc_ref[...]
    kvp = kvp_ref[...]
    lane = lax.broadcasted_iota(jnp.int32, (1, LANES), 1)
    key = lax.broadcasted_iota(jnp.int32, (w, rep * w), 0)
    qry = lax.broadcasted_iota(jnp.int32, (w, rep * w), 1) % w
    upper = key > qry
    masked = jnp.logical_and(upper, first)
    heads_per_blk = LANES // hd
    for p in range(kvw // LANES):
        cs = slice(p * LANES, (p + 1) * LANES)
        kc, kp = kvc[:, cs], kvp[:, cs]
        vs = slice(kvw + p * LANES, kvw + (p + 1) * LANES)
        vb = jnp.concatenate([kvp[:, vs], kvc[:, vs]], axis=0).astype(BF16)
        qg = jnp.concatenate([q[:, r * kvw + p * LANES:r * kvw + (p + 1) * LANES] for r in range(rep)], axis=0)
        halves = []
        for half in range(heads_per_blk):
            g = p * heads_per_blk + half
            lmask = jnp.logical_and(lane >= half * hd, lane < (half + 1) * hd)
            kb = jnp.concatenate([jnp.where(lmask, kp, 0.0), jnp.where(lmask, kc, 0.0)], axis=0).astype(BF16)
            s = lax.dot_general(kb, qg, (((1,), (1,)), ((), ())), preferred_element_type=F32)
            sc = jnp.where(upper, s[:w], s[w:]) * scale + bias_ref[g]
            sc = jnp.where(masked, NEG_INF, sc)
            sink = jnp.concatenate([jnp.full((1, w), sink_ref[g * rep + r], F32) for r in range(rep)], axis=1)
            m = jnp.maximum(jnp.max(sc, axis=0, keepdims=True), sink)
            pe = jnp.exp(sc - m)
            denom = jnp.sum(pe, axis=0, keepdims=True) + jnp.exp(sink - m)
            pcat = jnp.concatenate([jnp.where(upper, pe, 0.0), jnp.where(upper, 0.0, pe)], axis=0).astype(BF16)
            og = lax.dot_general(vb, pcat, (((0,), (0,)), ((), ())), preferred_element_type=F32)
            halves.append((og / denom)[half * hd:(half + 1) * hd])
        o_blk = jnp.concatenate(halves, axis=0).astype(BF16)
        for r in range(rep):
            ot_ref[r * kvw + p * LANES:r * kvw + (p + 1) * LANES, :] = o_blk[:, r * w:(r + 1) * w]


def _attn_prompt(q, kv, bias, sinks, *, bsz, nblk, window, kvh, rep, hd):
    rows, nq = q.shape
    kvw2 = kv.shape[1]
    kern = functools.partial(_attn_prompt_kernel, kvh=kvh, rep=rep, hd=hd, scale=1.0 / math.sqrt(hd))
    return pl.pallas_call(
        kern,
        grid=(bsz, nblk),
        in_specs=[pl.BlockSpec((window, nq), lambda b, i: (b * nblk + i, 0)),
                  pl.BlockSpec((window, kvw2), lambda b, i: (b * nblk + i, 0)),
                  pl.BlockSpec((window, kvw2), lambda b, i: (b * nblk + jnp.maximum(i - 1, 0), 0)),
                  _const_spec(bias.shape),
                  pl.BlockSpec(memory_space=pltpu.SMEM)],
        out_specs=pl.BlockSpec((nq, window), lambda b, i: (0, b * nblk + i)),
        out_shape=jax.ShapeDtypeStruct((nq, rows), BF16),
        compiler_params=_params("arbitrary", "arbitrary"),
        name="swa_prompt",
    )(q, kv, kv, bias, sinks)


def _attn_sample_kernel(q_ref, kv_ref, ck_ref, cv_ref, bias_ref, sink_ref, o_ref, nk_ref, nv_ref,
                        *, kvh, rep, hd, scale):
    nb, w, kvw = ck_ref.shape
    kv = kv_ref[...]
    lane = lax.broadcasted_iota(jnp.int32, (nb, rep, kvw), 2)

    def shifted(c_ref, n_ref, new):
        flat = c_ref[...].reshape(nb * w, kvw)
        n_ref[...] = pltpu.roll(flat, nb * w - 1, axis=0).reshape(nb, w, kvw)
        n_ref[:, w - 1:w, :] = new
        return n_ref[...].astype(BF16)

    nkb = shifted(ck_ref, nk_ref, kv[:, :, :kvw])
    nvb = shifted(cv_ref, nv_ref, kv[:, :, kvw:])
    q = q_ref[...].astype(F32)
    o = jnp.zeros((nb, rep, kvw), F32)
    for g in range(kvh):
        lmask = jnp.logical_and(lane >= g * hd, lane < (g + 1) * hd)
        qg = jnp.where(lmask, q, 0.0).astype(BF16)
        s = jnp.einsum("nrc,njc->nrj", qg, nkb, preferred_element_type=F32)
        sc = s * scale + bias_ref[g][None]
        sink = sink_ref[g][None]
        m = jnp.maximum(jnp.max(sc, axis=-1, keepdims=True), sink)
        pe = jnp.exp(sc - m)
        probs = pe / (jnp.sum(pe, axis=-1, keepdims=True) + jnp.exp(sink - m))
        og = jnp.einsum("nrj,njc->nrc", probs.astype(BF16), nvb, preferred_element_type=F32)
        o = jnp.where(lmask, og, o)
    o_ref[...] = o.astype(BF16)


def _attn_sample(q3, kv3, ck, cv, bias_s, sink_s, *, nb, kvh, rep, hd):
    n, w, kvw = ck.shape
    kern = functools.partial(_attn_sample_kernel, kvh=kvh, rep=rep, hd=hd, scale=1.0 / math.sqrt(hd))
    cache_spec = pl.BlockSpec((nb, w, kvw), lambda i: (i, 0, 0))
    return pl.pallas_call(
        kern,
        grid=(n // nb,),
        in_specs=[pl.BlockSpec((nb, rep, kvw), lambda i: (i, 0, 0)),
                  pl.BlockSpec((nb, 1, 2 * kvw), lambda i: (i, 0, 0)),
                  cache_spec, cache_spec,
                  _const_spec(bias_s.shape), _const_spec(sink_s.shape)],
        out_specs=[pl.BlockSpec((nb, rep, kvw), lambda i: (i, 0, 0)), cache_spec, cache_spec],
        out_shape=[jax.ShapeDtypeStruct((n, rep, kvw), BF16),
                   jax.ShapeDtypeStruct((n, w, kvw), F32), jax.ShapeDtypeStruct((n, w, kvw), F32)],
        compiler_params=_params("arbitrary"),
        name="swa_sample",
    )(q3, kv3, ck, cv, bias_s, sink_s)


CHUNK = 16
SUB = 256
ROW_BLOCK = 512
TOKEN_TILE = 512


def _chunk_copy(src, src_row, dst, dst_row, sem):
    return pltpu.make_async_copy(src.at[pl.ds(pl.multiple_of(src_row, CHUNK), CHUNK), :],
                                 dst.at[pl.ds(pl.multiple_of(dst_row, CHUNK), CHUNK), :], sem)


def _moe_route_kernel(x2_ref, ot_ref, wo_ref, gffn_ref, wr_ref, lst_ref, ust_ref,
                      x3_ref, cm_ref, tmeta_ref, xs_hbm,
                      comp_s, zero_s, base_v, pend_sm, sem, *, ne, cap):
    i = pl.program_id(0)
    nt = pl.num_programs(0)
    tm = x2_ref.shape[0]
    w = comp_s.shape[1]
    slot = lax.rem(i, 2)

    @pl.when(i == 0)
    def _():
        base_v[...] = jnp.zeros_like(base_v)
        zero_s[...] = jnp.zeros_like(zero_s)

    def drain(n):
        def body(c, carry):
            _chunk_copy(zero_s, 0, xs_hbm, 0, sem).wait()
            return carry
        lax.fori_loop(0, n, body, 0)

    @pl.when(i > 0)
    def _():
        drain(pend_sm[0])

    x3 = x2_ref[...] + lax.dot_general(ot_ref[...], wo_ref[...], (((0,), (0,)), ((), ())),
                                       preferred_element_type=F32)
    x3_ref[...] = x3
    hb = _rms(x3, gffn_ref[...]).astype(BF16)

    lane = lax.broadcasted_iota(jnp.int32, (tm, LANES), 1)
    logits = jnp.where(lane < ne, _dot(hb, wr_ref[...]), -jnp.inf)
    m1 = jnp.max(logits, axis=-1, keepdims=True)
    i1 = jnp.min(jnp.where(logits == m1, lane, LANES), axis=-1, keepdims=True)
    rest = jnp.where(lane == i1, -jnp.inf, logits)
    m2 = jnp.max(rest, axis=-1, keepdims=True)
    i2 = jnp.min(jnp.where(rest == m2, lane, LANES), axis=-1, keepdims=True)
    e2 = jnp.exp(m2 - m1)
    g1 = 1.0 / (1.0 + e2)
    g2 = e2 / (1.0 + e2)

    sel = jnp.where(jnp.logical_or(lane == i1, lane == i2), 1.0, 0.0)
    rank = _dot(lst_ref[...], sel.astype(BF16))
    cnt = jnp.sum(sel, axis=0, keepdims=True)
    cpad = jnp.floor((cnt + (CHUNK - 1)) / CHUNK) * CHUNK
    loff = _dot(jnp.broadcast_to(cpad, (8, LANES)).astype(BF16), ust_ref[...])[0:1]
    dest = loff + rank
    ld1 = jnp.sum(jnp.where(lane == i1, dest, 0.0), axis=-1, keepdims=True)
    ld2 = jnp.sum(jnp.where(lane == i2, dest, 0.0), axis=-1, keepdims=True)
    cm = jnp.where(lane == 0, ld1, jnp.where(lane == 1, ld2, jnp.where(lane == 2, g1, jnp.where(lane == 3, g2, 0.0))))
    cm_ref[...] = cm

    rm = cm.T
    rowi = lax.broadcasted_iota(jnp.int32, (w, tm), 0).astype(F32)
    place = jnp.where(jnp.logical_or(rowi == rm[0:1], rowi == rm[1:2]), 1.0, 0.0).astype(BF16)
    comp_s[slot] = _dot(place, hb).astype(BF16)

    base = base_v[...]
    srow = lax.broadcasted_iota(jnp.int32, (8, LANES), 0)
    tmeta_ref[...] = jnp.where(srow == 0, cpad, jnp.where(srow == 1, base, jnp.where(srow == 2, base + cpad, 0.0))
                               ).astype(jnp.int32)
    base_v[...] = base + cpad

    cpad_i = cpad.astype(jnp.int32)
    loff_i = loff.astype(jnp.int32)
    base_i = base.astype(jnp.int32)
    total = 0
    tails = []
    for e in range(ne):
        n_e = cpad_i[0, e]
        src0 = loff_i[0, e]
        dst0 = base_i[0, e] + e * cap

        def start(c, carry, src0=src0, dst0=dst0):
            _chunk_copy(comp_s.at[slot], src0 + c * CHUNK, xs_hbm, dst0 + c * CHUNK, sem).start()
            return carry

        nchunks = lax.shift_right_logical(n_e, int(math.log2(CHUNK)))
        lax.fori_loop(0, nchunks, start, 0)
        total = total + nchunks
        tails.append(dst0 + n_e)
    pend_sm[0] = total

    @pl.when(i == nt - 1)
    def _():
        nz_total = 0
        for e in range(ne):
            end = tails[e]
            nz = lax.shift_right_logical(lax.rem(SUB - lax.rem(end, SUB), SUB), int(math.log2(CHUNK)))

            def zstart(c, carry, end=end):
                _chunk_copy(zero_s, 0, xs_hbm, end + c * CHUNK, sem).start()
                return carry

            lax.fori_loop(0, nz, zstart, 0)
            nz_total = nz_total + nz
        drain(total + nz_total)


def _moe_route(x2, ot, wo, gffn, wr_pad, *, tm, ne, cap):
    rows, d = x2.shape
    nt = rows // tm
    w = TOP_K * tm + LANES
    lst = jnp.asarray(np.tril(np.ones((tm, tm), np.float32), -1), BF16)
    ust = jnp.asarray(np.triu(np.ones((LANES, LANES), np.float32), 1), BF16)
    kern = functools.partial(_moe_route_kernel, ne=ne, cap=cap)
    return pl.pallas_call(
        kern,
        grid=(nt,),
        in_specs=[pl.BlockSpec((tm, d), lambda i: (i, 0)),
                  pl.BlockSpec((ot.shape[0], tm), lambda i: (0, i)),
                  _const_spec(wo.shape), _const_spec(gffn.shape), _const_spec(wr_pad.shape),
                  _const_spec(lst.shape), _const_spec(ust.shape)],
        out_specs=[pl.BlockSpec((tm, d), lambda i: (i, 0)),
                   pl.BlockSpec((tm, LANES), lambda i: (i, 0)),
                   pl.BlockSpec((8, LANES), lambda i: (i, 0)),
                   pl.BlockSpec(memory_space=pl.ANY)],
        out_shape=[jax.ShapeDtypeStruct((rows, d), F32),
                   jax.ShapeDtypeStruct((rows, LANES), F32),
                   jax.ShapeDtypeStruct((nt * 8, LANES), jnp.int32),
                   jax.ShapeDtypeStruct((ne * cap, d), BF16)],
        scratch_shapes=[pltpu.VMEM((2, w, d), BF16), pltpu.VMEM((CHUNK, d), BF16),
                        pltpu.VMEM((1, LANES), F32), pltpu.SMEM((1,), jnp.int32),
                        pltpu.SemaphoreType.DMA(())],
        compiler_params=_params("arbitrary"),
        name="moe_route",
    )(x2, ot, wo, gffn, wr_pad, lst, ust)


def _expert_steps(nrows, nf, max_blocks):
    ne = nrows.shape[0]
    nblk = (nrows + ROW_BLOCK - 1) // ROW_BLOCK
    cum = jnp.cumsum(nblk) * nf
    total = cum[-1]
    s = jnp.minimum(jnp.arange(max_blocks * nf, dtype=jnp.int32), total - 1)
    e = jnp.minimum(jnp.sum((s[:, None] >= cum[None, :]).astype(jnp.int32), axis=1), ne - 1)
    nb_e = nblk[e]
    within = s - (cum[e] - nb_e * nf)
    f = within // nb_e
    r = within - f * nb_e
    nsub = jnp.clip((nrows[e] - r * ROW_BLOCK + SUB - 1) // SUB, 0, ROW_BLOCK // SUB)
    first = (r == 0).astype(jnp.int32)
    return (e, f.astype(jnp.int32), r.astype(jnp.int32), first, nsub.astype(jnp.int32),
            jnp.reshape(total, (1,)).astype(jnp.int32))


def _moe_up_kernel(e_ref, f_ref, r_ref, first_ref, nsub_ref, n_ref, xs_ref, wg_ref, wu_ref, act_ref, wgb_s, wub_s):
    s = pl.program_id(0)

    @pl.when(s < n_ref[0])
    def _():
        @pl.when(first_ref[s] == 1)
        def _():
            wgb_s[...] = wg_ref[...].astype(BF16)
            wub_s[...] = wu_ref[...].astype(BF16)

        def body(j, carry):
            rows = pl.ds(pl.multiple_of(j * SUB, SUB), SUB)
            xsb = xs_ref[rows, :]
            a = _dot(xsb, wgb_s[...])
            b = _dot(xsb, wub_s[...])
            act_ref[rows, :] = (jax.nn.silu(a) * b).astype(BF16)
            return carry

        lax.fori_loop(0, nsub_ref[s], body, 0)


def _moe_up(steps, xs, wgu, *, fc, cap, max_blocks):
    ne, d, dff2 = wgu.shape
    dff = dff2 // 2
    nf = dff // fc
    cb = cap // ROW_BLOCK
    grid_spec = pltpu.PrefetchScalarGridSpec(
        num_scalar_prefetch=6,
        grid=(max_blocks * nf,),
        in_specs=[pl.BlockSpec((ROW_BLOCK, d), lambda s, e, f, r, *_: (e[s] * cb + r[s], 0)),
                  pl.BlockSpec((None, d, fc), lambda s, e, f, r, *_: (e[s], 0, f[s])),
                  pl.BlockSpec((None, d, fc), lambda s, e, f, r, *_: (e[s], 0, nf + f[s]))],
        out_specs=pl.BlockSpec((ROW_BLOCK, fc), lambda s, e, f, r, *_: (e[s] * cb + r[s], f[s])),
        scratch_shapes=[pltpu.VMEM((d, fc), BF16), pltpu.VMEM((d, fc), BF16)])
    return pl.pallas_call(
        _moe_up_kernel,
        grid_spec=grid_spec,
        out_shape=jax.ShapeDtypeStruct((ne * cap, dff), BF16),
        compiler_params=_params("arbitrary"),
        name="moe_up",
    )(*steps, xs, wgu, wgu)


def _moe_down_kernel(e_ref, f_ref, r_ref, first_ref, nsub_ref, n_ref, act_ref, wd_ref, ys_ref, wdb_s):
    s = pl.program_id(0)

    @pl.when(s < n_ref[0])
    def _():
        @pl.when(first_ref[s] == 1)
        def _():
            wdb_s[...] = wd_ref[...].astype(BF16)

        def body(j, carry):
            rows = pl.ds(pl.multiple_of(j * SUB, SUB), SUB)
            ys_ref[rows, :] = _dot(act_ref[rows, :], wdb_s[...]).astype(BF16)
            return carry

        lax.fori_loop(0, nsub_ref[s], body, 0)


def _moe_down(steps, act, wd, *, cap, max_blocks):
    ne, dff, d = wd.shape
    cb = cap // ROW_BLOCK
    grid_spec = pltpu.PrefetchScalarGridSpec(
        num_scalar_prefetch=6,
        grid=(max_blocks,),
        in_specs=[pl.BlockSpec((ROW_BLOCK, dff), lambda s, e, f, r, *_: (e[s] * cb + r[s], 0)),
                  pl.BlockSpec((None, dff, d), lambda s, e, f, r, *_: (e[s], 0, 0))],
        out_specs=pl.BlockSpec((ROW_BLOCK, d), lambda s, e, f, r, *_: (e[s] * cb + r[s], 0)),
        scratch_shapes=[pltpu.VMEM((dff, d), BF16)])
    return pl.pallas_call(
        _moe_down_kernel,
        grid_spec=grid_spec,
        out_shape=jax.ShapeDtypeStruct((ne * cap, d), BF16),
        compiler_params=_params("arbitrary"),
        name="moe_down",
    )(*steps, act, wd)


def _moe_combine_kernel(cpad_ref, seg_ref, x3_ref, cm_ref, gfin_ref, ys_hbm, yp_ref, ys_ref, yloc_s, sem, *, ne):
    i = pl.program_id(0)
    nt = pl.num_programs(0)
    tm = x3_ref.shape[0]
    w = yloc_s.shape[1]
    slot = lax.rem(i, 2)
    shift = int(math.log2(CHUNK))

    def issue(t, sl):
        off = 0
        for e in range(ne):
            n_e = cpad_ref[t * ne + e]
            src0 = seg_ref[t * ne + e]

            def start(c, carry, src0=src0, off=off):
                _chunk_copy(ys_hbm, src0 + c * CHUNK, yloc_s.at[sl], off + c * CHUNK, sem.at[sl]).start()
                return carry

            lax.fori_loop(0, lax.shift_right_logical(n_e, shift), start, 0)
            off = off + n_e

    @pl.when(i == 0)
    def _():
        yloc_s[...] = jnp.zeros_like(yloc_s)
        issue(0, 0)

    @pl.when(i + 1 < nt)
    def _():
        issue(i + 1, 1 - slot)

    total = 0
    for e in range(ne):
        total = total + lax.shift_right_logical(cpad_ref[i * ne + e], shift)

    def wait(c, carry):
        _chunk_copy(ys_hbm, 0, yloc_s.at[slot], 0, sem.at[slot]).wait()
        return carry

    lax.fori_loop(0, total, wait, 0)

    cm = cm_ref[...]
    col = lax.broadcasted_iota(jnp.int32, (tm, w), 1).astype(F32)
    yl = yloc_s[slot]
    y1 = _dot(jnp.where(col == cm[:, 0:1], 1.0, 0.0).astype(BF16), yl)
    y2 = _dot(jnp.where(col == cm[:, 1:2], 1.0, 0.0).astype(BF16), yl)
    y = _rms(x3_ref[...] + cm[:, 2:3] * y1 + cm[:, 3:4] * y2, gfin_ref[...])

    @pl.when(i < nt - 1)
    def _():
        yp_ref[...] = y

    @pl.when(i == nt - 1)
    def _():
        ys_ref[...] = y


def _moe_combine(cpad, seg, x3, cm, gfin, ys, *, tm, ne):
    rows, d = x3.shape
    npt = rows // tm - 1
    w = TOP_K * tm + LANES
    kern = functools.partial(_moe_combine_kernel, ne=ne)
    grid_spec = pltpu.PrefetchScalarGridSpec(
        num_scalar_prefetch=2,
        grid=(npt + 1,),
        in_specs=[pl.BlockSpec((tm, d), lambda i, *_: (i, 0)),
                  pl.BlockSpec((tm, LANES), lambda i, *_: (i, 0)),
                  pl.BlockSpec(gfin.shape, lambda i, *_: (0, 0)),
                  pl.BlockSpec(memory_space=pl.ANY)],
        out_specs=[pl.BlockSpec((tm, d), lambda i, *_: (jnp.minimum(i, npt - 1), 0)),
                   pl.BlockSpec((tm, d), lambda i, *_: (0, 0))],
        scratch_shapes=[pltpu.VMEM((2, w, d), BF16), pltpu.SemaphoreType.DMA((2,))])
    return pl.pallas_call(
        kern,
        grid_spec=grid_spec,
        out_shape=[jax.ShapeDtypeStruct((npt * tm, d), F32), jax.ShapeDtypeStruct((tm, d), F32)],
        compiler_params=_params("arbitrary"),
        name="moe_combine",
    )(cpad, seg, x3, cm, gfin, ys)


def _moe(x2, ot, wo, gffn, wr, wgu, wd, gfin, *, tm, fc):
    rows, d = x2.shape
    ne = wr.shape[1]
    nt = rows // tm
    nf = wgu.shape[2] // 2 // fc
    cap = -(-(rows + nt * (CHUNK - 1)) // ROW_BLOCK) * ROW_BLOCK
    max_blocks = (TOP_K * rows + nt * ne * (CHUNK - 1)) // ROW_BLOCK + ne
    wr_pad = jnp.pad(wr, ((0, 0), (0, LANES - ne))).astype(BF16)
    x3, cm, tmeta, xs = _moe_route(x2, ot, wo, gffn, wr_pad, tm=tm, ne=ne, cap=cap)
    tmeta = tmeta.reshape(nt, 8, LANES)[:, :, :ne]
    cpad = tmeta[:, 0, :].reshape(nt * ne)
    seg = (tmeta[:, 1, :] + jnp.arange(ne, dtype=jnp.int32)[None, :] * cap).reshape(nt * ne)
    nrows = tmeta[nt - 1, 2, :]
    act = _moe_up(_expert_steps(nrows, nf, max_blocks), xs, wgu, fc=fc, cap=cap, max_blocks=max_blocks)
    ys = _moe_down(_expert_steps(nrows, 1, max_blocks), act, wd, cap=cap, max_blocks=max_blocks)
    return _moe_combine(cpad, seg, x3, cm, gfin, ys, tm=tm, ne=ne)


def kernel(x_prompt, x_sample, state_ssm_re, state_ssm_im, cache_k_win, cache_v_win, g_mix, g_ffn, g_kv, g_final, ssm_a_re, ssm_a_im, ssm_log_dt, ssm_b_re, ssm_b_im, ssm_c_re, ssm_c_im, ssm_d, w_glu, w_kv, w_q, w_o, attn_sinks, rel_bias, w_ffn_gate_up, w_ffn_down, w_router, w_exp_gate_up, w_exp_down):
    bsz, seq, d = x_prompt.shape
    ns, dec_seq, _ = x_sample.shape
    assert dec_seq == 1 and g_mix.shape[0] == 2 and ssm_a_re.shape[0] == 1 and w_q.shape[0] == 1
    _, g, p = ssm_a_re.shape
    hch = d // g
    gp = g * p
    window, kvh, hd = cache_k_win.shape[1:]
    kvw = kvh * hd
    nh = attn_sinks.shape[1]
    rep = nh // kvh
    nq = nh * hd
    assert bsz == 8 and ns % bsz == 0 and seq % window == 0 and LANES % hd == 0

    gpb = MXU_DIM // hch
    lam_re, lam_im, bb_re, bb_im = _zoh(ssm_a_re[0], ssm_a_im[0], ssm_log_dt[0], ssm_b_re[0], ssm_b_im[0])
    wb_re = _blockdiag_in(bb_re, g, p, gpb)
    wb_im = _blockdiag_in(bb_im, g, p, gpb)
    wc_re = _blockdiag_out(ssm_c_re[0], gpb)
    wc_imn = _blockdiag_out(-ssm_c_im[0], gpb)
    wglu = w_glu[0].astype(BF16)
    d_skip = ssm_d[0].reshape(1, d)
    wgu = w_ffn_gate_up[0].astype(BF16)
    wd = w_ffn_down[0].astype(BF16)
    wkv = w_kv.astype(BF16)
    wq = w_q[0].reshape(d, kvh, rep, hd).transpose(0, 2, 1, 3).reshape(d, nq).astype(BF16)
    wo = w_o[0].reshape(kvh, rep, hd, d).transpose(1, 0, 2, 3).reshape(nq, d).astype(BF16)
    bias, bias_s = _bias_tables(rel_bias, window, kvh, rep)
    sinks = attn_sinks[0]
    sink_s = sinks.reshape(kvh, rep, 1)

    tm = TOKEN_TILE
    npr = seq * bsz
    assert npr % tm == 0 and ns <= tm

    x1_p, st_re, st_im = _ssm_prompt(x_prompt, g_mix[0:1], lam_re, lam_im, wb_re, wb_im, wc_re, wc_imn,
                                     d_skip, wglu, lc=32)
    x1_s, hs_re, hs_im = _ssm_sample(x_sample.reshape(ns, d), g_mix[0:1], lam_re, lam_im,
                                     state_ssm_re[0].reshape(ns, gp), state_ssm_im[0].reshape(ns, gp),
                                     wb_re, wb_im, wc_re, wc_imn, d_skip, wglu, pad_rows=tm)

    x2, kv, q = _ffn(x1_p.reshape(npr, d), x1_s, g_ffn[0:1], wgu, wd, g_kv.reshape(1, d), wkv, g_mix[1:2], wq)

    ot = _attn_prompt(q, kv, bias, sinks, bsz=bsz, nblk=seq // window, window=window, kvh=kvh, rep=rep, hd=hd)
    o_s, nk_s, nv_s = _attn_sample(q[npr:npr + ns].reshape(ns, rep, kvw), kv[npr:npr + ns].reshape(ns, 1, 2 * kvw),
                                   cache_k_win.reshape(ns, window, kvw), cache_v_win.reshape(ns, window, kvw),
                                   bias_s, sink_s, nb=16, kvh=kvh, rep=rep, hd=hd)
    ot = lax.dynamic_update_slice(ot, jnp.pad(o_s.reshape(ns, nq).T, ((0, 0), (0, tm - ns))), (0, npr))

    y_p, y_s = _moe(x2, ot, wo, g_ffn[1:2], w_router[0], w_exp_gate_up[0], w_exp_down[0], g_final.reshape(1, d),
                    tm=tm, fc=w_exp_down.shape[2] // 2)

    y_prompt = y_p.reshape(bsz, seq, d)
    y_sample = y_s[:ns].reshape(ns, 1, d)
    kv_tail = jnp.stack([kv[(b + 1) * seq - window:(b + 1) * seq] for b in range(bsz)])
    kv_tail = kv_tail.reshape(bsz, window, 2, kvh, hd).transpose(2, 0, 1, 3, 4)
    return (y_prompt, y_sample,
            st_re.reshape(1, bsz, g, p), st_im.reshape(1, bsz, g, p), kv_tail[0], kv_tail[1],
            hs_re.reshape(1, ns, g, p), hs_im.reshape(1, ns, g, p),
            nk_s.reshape(ns, window, kvh, hd), nv_s.reshape(ns, window, kvh, hd))
```

```python
import functools
import math

import numpy as np
import jax
import jax.numpy as jnp
from jax import lax
from jax.experimental import pallas as pl
from jax.experimental.pallas import tpu as pltpu

F32 = jnp.float32
BF16 = jnp.bfloat16

EPS = 1e-6
NEG_INF = -1e30
TOP_K = 2
MAX_DISTANCE = 128
MXU_DIM = 256
LANES = 128
VMEM_LIMIT_BYTES = 56 * 1024 * 1024


def _dot(a, b):
    return jnp.dot(a, b, preferred_element_type=F32)


def _rms(x, g):
    return x * lax.rsqrt(jnp.mean(x * x, axis=-1, keepdims=True) + EPS) * g


def _const_spec(shape):
    nd = len(shape)
    return pl.BlockSpec(shape, lambda *_: (0,) * nd)


def _params(*sem):
    return pltpu.CompilerParams(dimension_semantics=sem, vmem_limit_bytes=VMEM_LIMIT_BYTES)


def _zoh_kernel(a_re_ref, a_im_ref, log_dt_ref, b_re_ref, b_im_ref,
                lam_re_ref, lam_im_ref, bb_re_ref, bb_im_ref):
    a_re = a_re_ref[...]
    a_im = a_im_ref[...]
    dt = jnp.exp(log_dt_ref[...])
    mag = jnp.exp(a_re * dt)
    lr = mag * jnp.cos(a_im * dt)
    li = mag * jnp.sin(a_im * dt)
    lam_re_ref[...] = lr
    lam_im_ref[...] = li
    nr = lr - 1.0
    den = a_re * a_re + a_im * a_im
    qr = (nr * a_re + li * a_im) / den
    qi = (li * a_re - nr * a_im) / den
    b_re = b_re_ref[...]
    b_im = b_im_ref[...]
    bb_re_ref[...] = qr * b_re - qi * b_im
    bb_im_ref[...] = qr * b_im + qi * b_re


def _zoh(a_re, a_im, log_dt, b_re, b_im):
    g, p = a_re.shape
    h = b_re.shape[-1]
    gp = g * p
    row = jax.ShapeDtypeStruct((1, gp), F32)
    mat = jax.ShapeDtypeStruct((h, gp), F32)
    return pl.pallas_call(
        _zoh_kernel,
        out_shape=(row, row, mat, mat),
        name="s5_zoh",
    )(a_re.reshape(1, gp), a_im.reshape(1, gp), jnp.repeat(log_dt, p).reshape(1, gp),
      b_re.transpose(2, 0, 1).reshape(h, gp), b_im.transpose(2, 0, 1).reshape(h, gp))


def _blockdiag_in(bb, g, p, gpb):
    h = bb.shape[0]
    nkb = g // gpb
    bb4 = bb.reshape(h, nkb, gpb, p).transpose(1, 0, 2, 3)
    eye = jnp.eye(gpb, dtype=bb.dtype)
    w = bb4[:, None, :, :, :] * eye[None, :, None, :, None]
    return w.reshape(nkb, gpb * h, gpb * p).astype(BF16)


def _blockdiag_out(c, gpb):
    g, h, p = c.shape
    nkb = g // gpb
    c4 = c.reshape(nkb, gpb, h, p).transpose(0, 1, 3, 2)
    eye = jnp.eye(gpb, dtype=c.dtype)
    w = c4[:, :, :, None, :] * eye[None, :, None, :, None]
    return w.reshape(nkb, gpb * p, gpb * h).astype(BF16)


def _ssm_in_proj(ub, wb_re_ref, wb_im_ref, bu_re, bu_im):
    nkb, ublk, sblk = wb_re_ref.shape
    for kb in range(nkb):
        ukb = ub[:, kb * ublk:(kb + 1) * ublk]
        bu_re[:, kb * sblk:(kb + 1) * sblk] = _dot(ukb, wb_re_ref[kb])
        bu_im[:, kb * sblk:(kb + 1) * sblk] = _dot(ukb, wb_im_ref[kb])


def _ssm_out_proj(h_re, h_im, wc_re_ref, wc_imn_ref):
    nkb, sblk, _ = wc_re_ref.shape
    ys = []
    for kb in range(nkb):
        hr = h_re[:, kb * sblk:(kb + 1) * sblk].astype(BF16)
        hi = h_im[:, kb * sblk:(kb + 1) * sblk].astype(BF16)
        ys.append(_dot(hr, wc_re_ref[kb]) + _dot(hi, wc_imn_ref[kb]))
    return jnp.concatenate(ys, axis=1)


def _ssm_glu(x, u, y, d_ref, wglu_ref):
    z = jax.nn.gelu(y + d_ref[...] * u).astype(BF16)
    gl = _dot(z, wglu_ref[...])
    d = x.shape[1]
    return x + gl[:, :d] * jax.nn.sigmoid(gl[:, d:])


def _split3(v):
    hi = v.astype(BF16)
    r1 = v - hi.astype(F32)
    mid = r1.astype(BF16)
    lo = (r1 - mid.astype(F32)).astype(BF16)
    return hi, mid, lo


def _ssm_prompt_kernel(x_ref, g_ref, lam_re_ref, lam_im_ref, wb_re_ref, wb_im_ref,
                       wc_re_ref, wc_imn_ref, d_ref, wglu_ref, perm_ref, permt_ref,
                       out_ref, st_re_ref, st_im_ref, bu_re, bu_im, *, lc, bsz, lane_chunk):
    @pl.when(pl.program_id(0) == 0)
    def _():
        st_re_ref[...] = jnp.zeros_like(st_re_ref)
        st_im_ref[...] = jnp.zeros_like(st_im_ref)

    d = x_ref.shape[2]
    x = x_ref[...].reshape(bsz * lc, d)
    u = _rms(x, g_ref[...])
    ub_tb = _dot(perm_ref[...], u.astype(BF16)).astype(BF16)
    _ssm_in_proj(ub_tb, wb_re_ref, wb_im_ref, bu_re, bu_im)

    gp = bu_re.shape[1]
    for c0 in range(0, gp, lane_chunk):
        sl = slice(c0, c0 + lane_chunk)
        lre = jnp.broadcast_to(lam_re_ref[:, sl], (bsz, lane_chunk))
        lim = jnp.broadcast_to(lam_im_ref[:, sl], (bsz, lane_chunk))
        hr = st_re_ref[:, sl]
        hi = st_im_ref[:, sl]
        for t in range(lc):
            rows = slice(t * bsz, (t + 1) * bsz)
            hr, hi = (lre * hr - lim * hi + bu_re[rows, sl], lre * hi + lim * hr + bu_im[rows, sl])
            bu_re[rows, sl] = hr
            bu_im[rows, sl] = hi
        st_re_ref[:, sl] = hr
        st_im_ref[:, sl] = hi

    permt = permt_ref[...]
    hi, mid, lo = _split3(_ssm_out_proj(bu_re, bu_im, wc_re_ref, wc_imn_ref))
    y = (_dot(permt, hi) + _dot(permt, mid)) + _dot(permt, lo)
    out_ref[...] = _ssm_glu(x, u, y, d_ref, wglu_ref).reshape(bsz, lc, d)


def _ssm_sample_kernel(x_ref, g_ref, lam_re_ref, lam_im_ref, h0_re_ref, h0_im_ref, wb_re_ref, wb_im_ref,
                       wc_re_ref, wc_imn_ref, d_ref, wglu_ref,
                       out_ref, h_re_ref, h_im_ref):
    x = x_ref[...]
    u = _rms(x, g_ref[...])
    _ssm_in_proj(u.astype(BF16), wb_re_ref, wb_im_ref, h_re_ref, h_im_ref)
    lre = lam_re_ref[...]
    lim = lam_im_ref[...]
    h0r = h0_re_ref[...]
    h0i = h0_im_ref[...]
    h_re_ref[...] = lre * h0r - lim * h0i + h_re_ref[...]
    h_im_ref[...] = lre * h0i + lim * h0r + h_im_ref[...]
    y = _ssm_out_proj(h_re_ref, h_im_ref, wc_re_ref, wc_imn_ref)
    n = x.shape[0]
    out_ref[0:n, :] = _ssm_glu(x, u, y, d_ref, wglu_ref)
    out_ref[n:, :] = jnp.zeros((out_ref.shape[0] - n, out_ref.shape[1]), F32)


def _ssm_prompt(x, g, lam_re, lam_im, wb_re, wb_im, wc_re, wc_imn, d_skip, wglu, *, lc):
    bsz, seq, d = x.shape
    gp = lam_re.shape[1]
    r = lc * bsz
    perm = np.zeros((r, r), np.float32)
    for b in range(bsz):
        for t in range(lc):
            perm[t * bsz + b, b * lc + t] = 1.0
    permt = jnp.asarray(perm.T, BF16)
    perm = jnp.asarray(perm, BF16)
    kern = functools.partial(_ssm_prompt_kernel, lc=lc, bsz=bsz, lane_chunk=8 * LANES)
    return pl.pallas_call(
        kern,
        grid=(seq // lc,),
        in_specs=[pl.BlockSpec((bsz, lc, d), lambda c: (0, c, 0)),
                  _const_spec(g.shape), _const_spec(lam_re.shape), _const_spec(lam_im.shape),
                  _const_spec(wb_re.shape), _const_spec(wb_im.shape),
                  _const_spec(wc_re.shape), _const_spec(wc_imn.shape),
                  _const_spec(d_skip.shape), _const_spec(wglu.shape),
                  _const_spec(perm.shape), _const_spec(permt.shape)],
        out_specs=[pl.BlockSpec((bsz, lc, d), lambda c: (0, c, 0)),
                   _const_spec((bsz, gp)), _const_spec((bsz, gp))],
        out_shape=[jax.ShapeDtypeStruct((bsz, seq, d), F32),
                   jax.ShapeDtypeStruct((bsz, gp), F32), jax.ShapeDtypeStruct((bsz, gp), F32)],
        scratch_shapes=[pltpu.VMEM((r, gp), F32), pltpu.VMEM((r, gp), F32)],
        compiler_params=_params("arbitrary"),
        name="s5_prompt",
    )(x, g, lam_re, lam_im, wb_re, wb_im, wc_re, wc_imn, d_skip, wglu, perm, permt)


def _ssm_sample(x, g, lam_re, lam_im, h0_re, h0_im, wb_re, wb_im, wc_re, wc_imn, d_skip, wglu, *, pad_rows):
    n, d = x.shape
    gp = lam_re.shape[1]
    return pl.pallas_call(
        _ssm_sample_kernel,
        out_shape=[jax.ShapeDtypeStruct((pad_rows, d), F32),
                   jax.ShapeDtypeStruct((n, gp), F32), jax.ShapeDtypeStruct((n, gp), F32)],
        compiler_params=_params(),
        name="s5_sample",
    )(x, g, lam_re, lam_im, h0_re, h0_im, wb_re, wb_im, wc_re, wc_imn, d_skip, wglu)


def _ffn_chunks(d_ff):
    step = 3 * MXU_DIM
    return [(c, min(c + step, d_ff)) for c in range(0, d_ff, step)]


def _ffn_kernel(xp_ref, xs_ref, gffn_ref, wgu_ref, wd_ref, gkv_ref, wkv_ref, gq_ref, wq_ref,
                x2_ref, kv_ref, q_ref):
    x = jnp.where(pl.program_id(0) < pl.num_programs(0) - 1, xp_ref[...], xs_ref[...])
    hb = _rms(x, gffn_ref[...]).astype(BF16)
    d_ff = wd_ref.shape[0]
    acc = None
    for c0, c1 in _ffn_chunks(d_ff):
        a = _dot(hb, wgu_ref[:, c0:c1])
        b = _dot(hb, wgu_ref[:, d_ff + c0:d_ff + c1])
        part = _dot((jax.nn.silu(a) * b).astype(BF16), wd_ref[c0:c1, :])
        acc = part if acc is None else acc + part
    x2 = x + acc
    x2_ref[...] = x2
    kv_ref[...] = _dot(_rms(x2, gkv_ref[...]).astype(BF16), wkv_ref[...])
    q_ref[...] = _dot(_rms(x2, gq_ref[...]).astype(BF16), wq_ref[...]).astype(BF16)


def _ffn(xp, xs, gffn, wgu, wd, gkv, wkv, gq, wq):
    tm, d = xs.shape
    npt = xp.shape[0] // tm
    rows = xp.shape[0] + tm
    kvw = wkv.shape[1]
    nq = wq.shape[1]
    return pl.pallas_call(
        _ffn_kernel,
        grid=(npt + 1,),
        in_specs=[pl.BlockSpec((tm, d), lambda i: (jnp.minimum(i, npt - 1), 0)),
                  _const_spec(xs.shape),
                  _const_spec(gffn.shape), _const_spec(wgu.shape), _const_spec(wd.shape),
                  _const_spec(gkv.shape), _const_spec(wkv.shape),
                  _const_spec(gq.shape), _const_spec(wq.shape)],
        out_specs=[pl.BlockSpec((tm, d), lambda i: (i, 0)),
                   pl.BlockSpec((tm, kvw), lambda i: (i, 0)),
                   pl.BlockSpec((tm, nq), lambda i: (i, 0))],
        out_shape=[jax.ShapeDtypeStruct((rows, d), F32),
                   jax.ShapeDtypeStruct((rows, kvw), F32),
                   jax.ShapeDtypeStruct((rows, nq), BF16)],
        compiler_params=_params("arbitrary"),
        name="ffn_kv_q",
    )(xp, xs, gffn, wgu, wd, gkv, wkv, gq, wq)


def _t5_bucket(dist, num_buckets):
    max_exact = num_buckets // 2
    d = jnp.maximum(dist, 0)
    large = max_exact + (jnp.log(jnp.maximum(d, 1).astype(F32) / max_exact)
                         / math.log(MAX_DISTANCE / max_exact) * (num_buckets - max_exact)).astype(jnp.int32)
    large = jnp.minimum(large, num_buckets - 1)
    return jnp.where(d < max_exact, d, large)


def _bias_kernel(bm_ref, bs_ref, rb_ref, bias_ref, bias_s_ref, *, kvh, rep):
    nb, nh = rb_ref.shape
    bm = bm_ref[...]
    bs = bs_ref[...]

    def lookup(buckets, h):
        def body(k, acc):
            return jnp.where(buckets == k, rb_ref[k, h], acc)
        return lax.fori_loop(0, nb, body, jnp.zeros(buckets.shape, F32))

    w = bm.shape[0]
    for g in range(kvh):
        for r in range(rep):
            bias_ref[g, :, r * w:(r + 1) * w] = lookup(bm, g * rep + r)
            bias_s_ref[g, r:r + 1, :] = lookup(bs, g * rep + r)


def _bias_tables(rel_bias, window, kvh, rep):
    nb, nh = rel_bias.shape
    key = jnp.arange(window)[:, None]
    qry = jnp.arange(window)[None, :]
    bm = _t5_bucket((qry - key) % window, nb).astype(jnp.int32)
    bs = _t5_bucket(window - 1 - qry, nb).astype(jnp.int32)
    kern = functools.partial(_bias_kernel, kvh=kvh, rep=rep)
    return pl.pallas_call(
        kern,
        in_specs=[pl.BlockSpec(memory_space=pltpu.VMEM), pl.BlockSpec(memory_space=pltpu.VMEM),
                  pl.BlockSpec(memory_space=pltpu.SMEM)],
        out_shape=[jax.ShapeDtypeStruct((kvh, window, rep * window), F32),
                   jax.ShapeDtypeStruct((kvh, rep, window), F32)],
        name="t5_bias",
    )(bm, bs, rel_bias)


def _attn_prompt_kernel(q_ref, kvc_ref, kvp_ref, bias_ref, sink_ref, ot_ref, *, kvh, rep, hd, scale):
    w = q_ref.shape[0]
    kvw = kvh * hd
    first = pl.program_id(1) == 0
    q = q_ref[...]
    kvc = kvc_ref[...]
    kvp = kvp_ref[...]
    lane = lax.broadcasted_iota(jnp.int32, (1, LANES), 1)
    key = lax.broadcasted_iota(jnp.int32, (w, rep * w), 0)
    qry = lax.broadcasted_iota(jnp.int32, (w, rep * w), 1) % w
    upper = key > qry
    masked = jnp.logical_and(upper, first)
    heads_per_blk = LANES // hd
    for p in range(kvw // LANES):
        cs = slice(p * LANES, (p + 1) * LANES)
        kc, kp = kvc[:, cs], kvp[:, cs]
        vs = slice(kvw + p * LANES, kvw + (p + 1) * LANES)
        vb = jnp.concatenate([kvp[:, vs], kvc[:, vs]], axis=0).astype(BF16)
        qg = jnp.concatenate([q[:, r * kvw + p * LANES:r * kvw + (p + 1) * LANES] for r in range(rep)], axis=0)
        halves = []
        for half in range(heads_per_blk):
            g = p * heads_per_blk + half
            lmask = jnp.logical_and(lane >= half * hd, lane < (half + 1) * hd)
            kb = jnp.concatenate([jnp.where(lmask, kp, 0.0), jnp.where(lmask, kc, 0.0)], axis=0).astype(BF16)
            s = lax.dot_general(kb, qg, (((1,), (1,)), ((), ())), preferred_element_type=F32)
            sc = jnp.where(upper, s[:w], s[w:]) * scale + bias_ref[g]
            sc = jnp.where(masked, NEG_INF, sc)
            sink = jnp.concatenate([jnp.full((1, w), sink_ref[g * rep + r], F32) for r in range(rep)], axis=1)
            m = jnp.maximum(jnp.max(sc, axis=0, keepdims=True), sink)
            pe = jnp.exp(sc - m)
            denom = jnp.sum(pe, axis=0, keepdims=True) + jnp.exp(sink - m)
            pcat = jnp.concatenate([jnp.where(upper, pe, 0.0), jnp.where(upper, 0.0, pe)], axis=0).astype(BF16)
            og = lax.dot_general(vb, pcat, (((0,), (0,)), ((), ())), preferred_element_type=F32)
            halves.append((og / denom)[half * hd:(half + 1) * hd])
        o_blk = jnp.concatenate(halves, axis=0).astype(BF16)
        for r in range(rep):
            ot_ref[r * kvw + p * LANES:r * kvw + (p + 1) * LANES, :] = o_blk[:, r * w:(r + 1) * w]


def _attn_prompt(q, kv, bias, sinks, *, bsz, nblk, window, kvh, rep, hd):
    rows, nq = q.shape
    kvw2 = kv.shape[1]
    kern = functools.partial(_attn_prompt_kernel, kvh=kvh, rep=rep, hd=hd, scale=1.0 / math.sqrt(hd))
    return pl.pallas_call(
        kern,
        grid=(bsz, nblk),
        in_specs=[pl.BlockSpec((window, nq), lambda b, i: (b * nblk + i, 0)),
                  pl.BlockSpec((window, kvw2), lambda b, i: (b * nblk + i, 0)),
                  pl.BlockSpec((window, kvw2), lambda b, i: (b * nblk + jnp.maximum(i - 1, 0), 0)),
                  _const_spec(bias.shape),
                  pl.BlockSpec(memory_space=pltpu.SMEM)],
        out_specs=pl.BlockSpec((nq, window), lambda b, i: (0, b * nblk + i)),
        out_shape=jax.ShapeDtypeStruct((nq, rows), BF16),
        compiler_params=_params("arbitrary", "arbitrary"),
        name="swa_prompt",
    )(q, kv, kv, bias, sinks)


def _attn_sample_kernel(q_ref, kv_ref, ck_ref, cv_ref, bias_ref, sink_ref, o_ref, nk_ref, nv_ref,
                        *, kvh, rep, hd, scale):
    nb, w, kvw = ck_ref.shape
    kv = kv_ref[...]
    lane = lax.broadcasted_iota(jnp.int32, (nb, rep, kvw), 2)

    def shifted(c_ref, n_ref, new):
        flat = c_ref[...].reshape(nb * w, kvw)
        n_ref[...] = pltpu.roll(flat, nb * w - 1, axis=0).reshape(nb, w, kvw)
        n_ref[:, w - 1:w, :] = new
        return n_ref[...].astype(BF16)

    nkb = shifted(ck_ref, nk_ref, kv[:, :, :kvw])
    nvb = shifted(cv_ref, nv_ref, kv[:, :, kvw:])
    q = q_ref[...].astype(F32)
    o = jnp.zeros((nb, rep, kvw), F32)
    for g in range(kvh):
        lmask = jnp.logical_and(lane >= g * hd, lane < (g + 1) * hd)
        qg = jnp.where(lmask, q, 0.0).astype(BF16)
        s = jnp.einsum("nrc,njc->nrj", qg, nkb, preferred_element_type=F32)
        sc = s * scale + bias_ref[g][None]
        sink = sink_ref[g][None]
        m = jnp.maximum(jnp.max(sc, axis=-1, keepdims=True), sink)
        pe = jnp.exp(sc - m)
        probs = pe / (jnp.sum(pe, axis=-1, keepdims=True) + jnp.exp(sink - m))
        og = jnp.einsum("nrj,njc->nrc", probs.astype(BF16), nvb, preferred_element_type=F32)
        o = jnp.where(lmask, og, o)
    o_ref[...] = o.astype(BF16)


def _attn_sample(q3, kv3, ck, cv, bias_s, sink_s, *, nb, kvh, rep, hd):
    n, w, kvw = ck.shape
    kern = functools.partial(_attn_sample_kernel, kvh=kvh, rep=rep, hd=hd, scale=1.0 / math.sqrt(hd))
    cache_spec = pl.BlockSpec((nb, w, kvw), lambda i: (i, 0, 0))
    return pl.pallas_call(
        kern,
        grid=(n // nb,),
        in_specs=[pl.BlockSpec((nb, rep, kvw), lambda i: (i, 0, 0)),
                  pl.BlockSpec((nb, 1, 2 * kvw), lambda i: (i, 0, 0)),
                  cache_spec, cache_spec,
                  _const_spec(bias_s.shape), _const_spec(sink_s.shape)],
        out_specs=[pl.BlockSpec((nb, rep, kvw), lambda i: (i, 0, 0)), cache_spec, cache_spec],
        out_shape=[jax.ShapeDtypeStruct((n, rep, kvw), BF16),
                   jax.ShapeDtypeStruct((n, w, kvw), F32), jax.ShapeDtypeStruct((n, w, kvw), F32)],
        compiler_params=_params("arbitrary"),
        name="swa_sample",
    )(q3, kv3, ck, cv, bias_s, sink_s)


CHUNK = 16
SUB = 256
UP_ROW_BLOCK = 1024
DOWN_ROW_BLOCK = 512
TOKEN_TILE = 512


def _chunk_copy(src, src_row, dst, dst_row, sem):
    return pltpu.make_async_copy(src.at[pl.ds(pl.multiple_of(src_row, CHUNK), CHUNK), :],
                                 dst.at[pl.ds(pl.multiple_of(dst_row, CHUNK), CHUNK), :], sem)


def _moe_route_kernel(x2_ref, ot_ref, wo_ref, gffn_ref, wr_ref, lst_ref, ust_ref,
                      x3_ref, cm_ref, tmeta_ref, xs_hbm,
                      comp_s, zero_s, base_v, pend_sm, sem, *, ne, cap, n_valid):
    i = pl.program_id(0)
    nt = pl.num_programs(0)
    tm = x2_ref.shape[0]
    w = comp_s.shape[1]
    slot = lax.rem(i, 2)

    @pl.when(i == 0)
    def _():
        base_v[...] = jnp.zeros_like(base_v)
        zero_s[...] = jnp.zeros_like(zero_s)

    def drain(n):
        def body(c, carry):
            _chunk_copy(zero_s, 0, xs_hbm, 0, sem).wait()
            return carry
        lax.fori_loop(0, n, body, 0)

    @pl.when(i > 0)
    def _():
        drain(pend_sm[0])

    x3 = x2_ref[...] + lax.dot_general(ot_ref[...], wo_ref[...], (((0,), (0,)), ((), ())),
                                       preferred_element_type=F32)
    x3_ref[...] = x3
    hb = _rms(x3, gffn_ref[...]).astype(BF16)

    lane = lax.broadcasted_iota(jnp.int32, (tm, LANES), 1)
    logits = jnp.where(lane < ne, _dot(hb, wr_ref[...]), -jnp.inf)
    m1 = jnp.max(logits, axis=-1, keepdims=True)
    i1 = jnp.min(jnp.where(logits == m1, lane, LANES), axis=-1, keepdims=True)
    rest = jnp.where(lane == i1, -jnp.inf, logits)
    m2 = jnp.max(rest, axis=-1, keepdims=True)
    i2 = jnp.min(jnp.where(rest == m2, lane, LANES), axis=-1, keepdims=True)
    e2 = jnp.exp(m2 - m1)
    g1 = 1.0 / (1.0 + e2)
    g2 = e2 / (1.0 + e2)

    live = i * tm + lax.broadcasted_iota(jnp.int32, (tm, 1), 0) < n_valid
    sel = jnp.where(jnp.logical_and(live, jnp.logical_or(lane == i1, lane == i2)), 1.0, 0.0)
    rank = _dot(lst_ref[...], sel.astype(BF16))
    cnt = jnp.sum(sel, axis=0, keepdims=True)
    cpad = jnp.floor((cnt + (CHUNK - 1)) / CHUNK) * CHUNK
    loff = _dot(jnp.broadcast_to(cpad, (8, LANES)).astype(BF16), ust_ref[...])[0:1]
    dest = loff + rank
    ld1 = jnp.where(live, jnp.sum(jnp.where(lane == i1, dest, 0.0), axis=-1, keepdims=True), -1.0)
    ld2 = jnp.where(live, jnp.sum(jnp.where(lane == i2, dest, 0.0), axis=-1, keepdims=True), -1.0)
    cm = jnp.where(lane == 0, ld1, jnp.where(lane == 1, ld2, jnp.where(lane == 2, g1, jnp.where(lane == 3, g2, 0.0))))
    cm_ref[...] = cm

    rm = cm.T
    rowi = lax.broadcasted_iota(jnp.int32, (w, tm), 0).astype(F32)
    place = (jnp.where(rowi == rm[0:1], 1.0, 0.0) + jnp.where(rowi == rm[1:2], 1.0, 0.0)).astype(BF16)
    comp_s[slot] = _dot(place, hb).astype(BF16)

    base = base_v[...]
    srow = lax.broadcasted_iota(jnp.int32, (8, LANES), 0)
    tmeta_ref[...] = jnp.where(srow == 0, cpad, jnp.where(srow == 1, base, jnp.where(srow == 2, base + cpad, 0.0))
                               ).astype(jnp.int32)
    base_v[...] = base + cpad

    cpad_i = cpad.astype(jnp.int32)
    loff_i = loff.astype(jnp.int32)
    base_i = base.astype(jnp.int32)
    total = 0
    tails = []
    for e in range(ne):
        n_e = cpad_i[0, e]
        src0 = loff_i[0, e]
        dst0 = base_i[0, e] + e * cap

        def start(c, carry, src0=src0, dst0=dst0):
            _chunk_copy(comp_s.at[slot], src0 + c * CHUNK, xs_hbm, dst0 + c * CHUNK, sem).start()
            return carry

        nchunks = lax.shift_right_logical(n_e, int(math.log2(CHUNK)))
        lax.fori_loop(0, nchunks, start, 0)
        total = total + nchunks
        tails.append(dst0 + n_e)
    pend_sm[0] = total

    @pl.when(i == nt - 1)
    def _():
        nz_total = 0
        for e in range(ne):
            end = tails[e]
            nz = lax.shift_right_logical(lax.rem(SUB - lax.rem(end, SUB), SUB), int(math.log2(CHUNK)))

            def zstart(c, carry, end=end):
                _chunk_copy(zero_s, 0, xs_hbm, end + c * CHUNK, sem).start()
                return carry

            lax.fori_loop(0, nz, zstart, 0)
            nz_total = nz_total + nz
        drain(total + nz_total)


def _moe_route(x2, ot, wo, gffn, wr_pad, *, tm, ne, cap, n_valid):
    rows, d = x2.shape
    nt = rows // tm
    w = TOP_K * tm + LANES
    lst = jnp.asarray(np.tril(np.ones((tm, tm), np.float32), -1), BF16)
    ust = jnp.asarray(np.triu(np.ones((LANES, LANES), np.float32), 1), BF16)
    kern = functools.partial(_moe_route_kernel, ne=ne, cap=cap, n_valid=n_valid)
    return pl.pallas_call(
        kern,
        grid=(nt,),
        in_specs=[pl.BlockSpec((tm, d), lambda i: (i, 0)),
                  pl.BlockSpec((ot.shape[0], tm), lambda i: (0, i)),
                  _const_spec(wo.shape), _const_spec(gffn.shape), _const_spec(wr_pad.shape),
                  _const_spec(lst.shape), _const_spec(ust.shape)],
        out_specs=[pl.BlockSpec((tm, d), lambda i: (i, 0)),
                   pl.BlockSpec((tm, LANES), lambda i: (i, 0)),
                   pl.BlockSpec((8, LANES), lambda i: (i, 0)),
                   pl.BlockSpec(memory_space=pl.ANY)],
        out_shape=[jax.ShapeDtypeStruct((rows, d), F32),
                   jax.ShapeDtypeStruct((rows, LANES), F32),
                   jax.ShapeDtypeStruct((nt * 8, LANES), jnp.int32),
                   jax.ShapeDtypeStruct((ne * cap, d), BF16)],
        scratch_shapes=[pltpu.VMEM((2, w, d), BF16), pltpu.VMEM((CHUNK, d), BF16),
                        pltpu.VMEM((1, LANES), F32), pltpu.SMEM((1,), jnp.int32),
                        pltpu.SemaphoreType.DMA(())],
        compiler_params=_params("arbitrary"),
        name="moe_route",
    )(x2, ot, wo, gffn, wr_pad, lst, ust)


def _max_blocks(rows, nt, ne, row_block):
    return (TOP_K * rows + nt * ne * (CHUNK - 1)) // row_block + ne


def _expert_steps(nrows, nf, max_blocks, row_block):
    ne = nrows.shape[0]
    nblk = (nrows + row_block - 1) // row_block
    cum = jnp.cumsum(nblk) * nf
    total = cum[-1]
    s = jnp.minimum(jnp.arange(max_blocks * nf, dtype=jnp.int32), total - 1)
    e = jnp.minimum(jnp.sum((s[:, None] >= cum[None, :]).astype(jnp.int32), axis=1), ne - 1)
    nb_e = nblk[e]
    within = s - (cum[e] - nb_e * nf)
    f = within // nb_e
    r = within - f * nb_e
    nsub = jnp.clip((nrows[e] - r * row_block + SUB - 1) // SUB, 0, row_block // SUB)
    first = (r == 0).astype(jnp.int32)
    return (e, f.astype(jnp.int32), r.astype(jnp.int32), first, nsub.astype(jnp.int32),
            jnp.reshape(total, (1,)).astype(jnp.int32))


def _moe_up_kernel(e_ref, f_ref, r_ref, first_ref, nsub_ref, n_ref, xs_ref, wg_ref, wu_ref, act_ref, wgb_s, wub_s):
    s = pl.program_id(0)

    @pl.when(s < n_ref[0])
    def _():
        @pl.when(first_ref[s] == 1)
        def _():
            wgb_s[...] = wg_ref[...].astype(BF16)
            wub_s[...] = wu_ref[...].astype(BF16)

        def body(j, carry):
            rows = pl.ds(pl.multiple_of(j * SUB, SUB), SUB)
            xsb = xs_ref[rows, :]
            a = _dot(xsb, wgb_s[...])
            b = _dot(xsb, wub_s[...])
            act_ref[rows, :] = (jax.nn.silu(a) * b).astype(BF16)
            return carry

        lax.fori_loop(0, nsub_ref[s], body, 0)


def _moe_up(steps, xs, wgu, *, fc, cap, row_block):
    ne, d, dff2 = wgu.shape
    dff = dff2 // 2
    nf = dff // fc
    cb = cap // row_block
    grid_spec = pltpu.PrefetchScalarGridSpec(
        num_scalar_prefetch=6,
        grid=(steps[0].shape[0],),
        in_specs=[pl.BlockSpec((row_block, d), lambda s, e, f, r, *_: (e[s] * cb + r[s], 0)),
                  pl.BlockSpec((None, d, fc), lambda s, e, f, r, *_: (e[s], 0, f[s])),
                  pl.BlockSpec((None, d, fc), lambda s, e, f, r, *_: (e[s], 0, nf + f[s]))],
        out_specs=pl.BlockSpec((row_block, fc), lambda s, e, f, r, *_: (e[s] * cb + r[s], f[s])),
        scratch_shapes=[pltpu.VMEM((d, fc), BF16), pltpu.VMEM((d, fc), BF16)])
    return pl.pallas_call(
        _moe_up_kernel,
        grid_spec=grid_spec,
        out_shape=jax.ShapeDtypeStruct((ne * cap, dff), BF16),
        compiler_params=_params("arbitrary"),
        name="moe_up",
    )(*steps, xs, wgu, wgu)


def _moe_down_kernel(e_ref, f_ref, r_ref, first_ref, nsub_ref, n_ref, act_ref, wd_ref, ys_ref, wdb_s):
    s = pl.program_id(0)

    @pl.when(s < n_ref[0])
    def _():
        @pl.when(first_ref[s] == 1)
        def _():
            wdb_s[...] = wd_ref[...].astype(BF16)

        def body(j, carry):
            rows = pl.ds(pl.multiple_of(j * SUB, SUB), SUB)
            ys_ref[rows, :] = _dot(act_ref[rows, :], wdb_s[...]).astype(BF16)
            return carry

        lax.fori_loop(0, nsub_ref[s], body, 0)


def _moe_down(steps, act, wd, *, cap, row_block):
    ne, dff, d = wd.shape
    cb = cap // row_block
    grid_spec = pltpu.PrefetchScalarGridSpec(
        num_scalar_prefetch=6,
        grid=(steps[0].shape[0],),
        in_specs=[pl.BlockSpec((row_block, dff), lambda s, e, f, r, *_: (e[s] * cb + r[s], 0)),
                  pl.BlockSpec((None, dff, d), lambda s, e, f, r, *_: (e[s], 0, 0))],
        out_specs=pl.BlockSpec((row_block, d), lambda s, e, f, r, *_: (e[s] * cb + r[s], 0)),
        scratch_shapes=[pltpu.VMEM((dff, d), BF16)])
    return pl.pallas_call(
        _moe_down_kernel,
        grid_spec=grid_spec,
        out_shape=jax.ShapeDtypeStruct((ne * cap, d), BF16),
        compiler_params=_params("arbitrary"),
        name="moe_down",
    )(*steps, act, wd)


def _moe_combine_kernel(cpad_ref, seg_ref, x3_ref, cm_ref, gfin_ref, ys_hbm, yp_ref, ys_ref, yloc_s, sem, *, ne):
    i = pl.program_id(0)
    nt = pl.num_programs(0)
    tm = x3_ref.shape[0]
    w = yloc_s.shape[1]
    slot = lax.rem(i, 2)
    shift = int(math.log2(CHUNK))

    def issue(t, sl):
        off = 0
        for e in range(ne):
            n_e = cpad_ref[t * ne + e]
            src0 = seg_ref[t * ne + e]

            def start(c, carry, src0=src0, off=off):
                _chunk_copy(ys_hbm, src0 + c * CHUNK, yloc_s.at[sl], off + c * CHUNK, sem.at[sl]).start()
                return carry

            lax.fori_loop(0, lax.shift_right_logical(n_e, shift), start, 0)
            off = off + n_e

    @pl.when(i == 0)
    def _():
        yloc_s[...] = jnp.zeros_like(yloc_s)
        issue(0, 0)

    @pl.when(i + 1 < nt)
    def _():
        issue(i + 1, 1 - slot)

    total = 0
    for e in range(ne):
        total = total + lax.shift_right_logical(cpad_ref[i * ne + e], shift)

    def wait(c, carry):
        _chunk_copy(ys_hbm, 0, yloc_s.at[slot], 0, sem.at[slot]).wait()
        return carry

    lax.fori_loop(0, total, wait, 0)

    cm = cm_ref[...]
    col = lax.broadcasted_iota(jnp.int32, (tm, w), 1).astype(F32)
    yl = yloc_s[slot]
    y1 = _dot(jnp.where(col == cm[:, 0:1], 1.0, 0.0).astype(BF16), yl)
    y2 = _dot(jnp.where(col == cm[:, 1:2], 1.0, 0.0).astype(BF16), yl)
    y = _rms(x3_ref[...] + cm[:, 2:3] * y1 + cm[:, 3:4] * y2, gfin_ref[...])

    @pl.when(i < nt - 1)
    def _():
        yp_ref[...] = y

    @pl.when(i == nt - 1)
    def _():
        ys_ref[...] = y


def _moe_combine(cpad, seg, x3, cm, gfin, ys, *, tm, ne):
    rows, d = x3.shape
    npt = rows // tm - 1
    w = TOP_K * tm + LANES
    kern = functools.partial(_moe_combine_kernel, ne=ne)
    grid_spec = pltpu.PrefetchScalarGridSpec(
        num_scalar_prefetch=2,
        grid=(npt + 1,),
        in_specs=[pl.BlockSpec((tm, d), lambda i, *_: (i, 0)),
                  pl.BlockSpec((tm, LANES), lambda i, *_: (i, 0)),
                  pl.BlockSpec(gfin.shape, lambda i, *_: (0, 0)),
                  pl.BlockSpec(memory_space=pl.ANY)],
        out_specs=[pl.BlockSpec((tm, d), lambda i, *_: (jnp.minimum(i, npt - 1), 0)),
                   pl.BlockSpec((tm, d), lambda i, *_: (0, 0))],
        scratch_shapes=[pltpu.VMEM((2, w, d), BF16), pltpu.SemaphoreType.DMA((2,))])
    return pl.pallas_call(
        kern,
        grid_spec=grid_spec,
        out_shape=[jax.ShapeDtypeStruct((npt * tm, d), F32), jax.ShapeDtypeStruct((tm, d), F32)],
        compiler_params=_params("arbitrary"),
        name="moe_combine",
    )(cpad, seg, x3, cm, gfin, ys)


def _moe(x2, ot, wo, gffn, wr, wgu, wd, gfin, *, tm, fc, n_valid):
    rows, d = x2.shape
    ne = wr.shape[1]
    nt = rows // tm
    nf = wgu.shape[2] // 2 // fc
    cap = -(-(rows + nt * (CHUNK - 1)) // UP_ROW_BLOCK) * UP_ROW_BLOCK
    wr_pad = jnp.pad(wr, ((0, 0), (0, LANES - ne))).astype(BF16)
    x3, cm, tmeta, xs = _moe_route(x2, ot, wo, gffn, wr_pad, tm=tm, ne=ne, cap=cap, n_valid=n_valid)
    tmeta = tmeta.reshape(nt, 8, LANES)[:, :, :ne]
    cpad = tmeta[:, 0, :].reshape(nt * ne)
    seg = (tmeta[:, 1, :] + jnp.arange(ne, dtype=jnp.int32)[None, :] * cap).reshape(nt * ne)
    nrows = tmeta[nt - 1, 2, :]
    up_steps = _expert_steps(nrows, nf, _max_blocks(rows, nt, ne, UP_ROW_BLOCK), UP_ROW_BLOCK)
    act = _moe_up(up_steps, xs, wgu, fc=fc, cap=cap, row_block=UP_ROW_BLOCK)
    down_steps = _expert_steps(nrows, 1, _max_blocks(rows, nt, ne, DOWN_ROW_BLOCK), DOWN_ROW_BLOCK)
    ys = _moe_down(down_steps, act, wd, cap=cap, row_block=DOWN_ROW_BLOCK)
    return _moe_combine(cpad, seg, x3, cm, gfin, ys, tm=tm, ne=ne)


def kernel(x_prompt, x_sample, state_ssm_re, state_ssm_im, cache_k_win, cache_v_win, g_mix, g_ffn, g_kv, g_final, ssm_a_re, ssm_a_im, ssm_log_dt, ssm_b_re, ssm_b_im, ssm_c_re, ssm_c_im, ssm_d, w_glu, w_kv, w_q, w_o, attn_sinks, rel_bias, w_ffn_gate_up, w_ffn_down, w_router, w_exp_gate_up, w_exp_down):
    bsz, seq, d = x_prompt.shape
    ns, dec_seq, _ = x_sample.shape
    assert dec_seq == 1 and g_mix.shape[0] == 2 and ssm_a_re.shape[0] == 1 and w_q.shape[0] == 1
    _, g, p = ssm_a_re.shape
    hch = d // g
    gp = g * p
    window, kvh, hd = cache_k_win.shape[1:]
    kvw = kvh * hd
    nh = attn_sinks.shape[1]
    rep = nh // kvh
    nq = nh * hd
    assert bsz == 8 and ns % bsz == 0 and seq % window == 0 and LANES % hd == 0

    gpb = MXU_DIM // hch
    lam_re, lam_im, bb_re, bb_im = _zoh(ssm_a_re[0], ssm_a_im[0], ssm_log_dt[0], ssm_b_re[0], ssm_b_im[0])
    wb_re = _blockdiag_in(bb_re, g, p, gpb)
    wb_im = _blockdiag_in(bb_im, g, p, gpb)
    wc_re = _blockdiag_out(ssm_c_re[0], gpb)
    wc_imn = _blockdiag_out(-ssm_c_im[0], gpb)
    wglu = w_glu[0].astype(BF16)
    d_skip = ssm_d[0].reshape(1, d)
    wgu = w_ffn_gate_up[0].astype(BF16)
    wd = w_ffn_down[0].astype(BF16)
    wkv = w_kv.astype(BF16)
    wq = w_q[0].reshape(d, kvh, rep, hd).transpose(0, 2, 1, 3).reshape(d, nq).astype(BF16)
    wo = w_o[0].reshape(kvh, rep, hd, d).transpose(1, 0, 2, 3).reshape(nq, d).astype(BF16)
    bias, bias_s = _bias_tables(rel_bias, window, kvh, rep)
    sinks = attn_sinks[0]
    sink_s = sinks.reshape(kvh, rep, 1)

    tm = TOKEN_TILE
    npr = seq * bsz
    assert npr % tm == 0 and ns <= tm

    x1_p, st_re, st_im = _ssm_prompt(x_prompt, g_mix[0:1], lam_re, lam_im, wb_re, wb_im, wc_re, wc_imn,
                                     d_skip, wglu, lc=32)
    x1_s, hs_re, hs_im = _ssm_sample(x_sample.reshape(ns, d), g_mix[0:1], lam_re, lam_im,
                                     state_ssm_re[0].reshape(ns, gp), state_ssm_im[0].reshape(ns, gp),
                                     wb_re, wb_im, wc_re, wc_imn, d_skip, wglu, pad_rows=tm)

    x2, kv, q = _ffn(x1_p.reshape(npr, d), x1_s, g_ffn[0:1], wgu, wd, g_kv.reshape(1, d), wkv, g_mix[1:2], wq)

    ot = _attn_prompt(q, kv, bias, sinks, bsz=bsz, nblk=seq // window, window=window, kvh=kvh, rep=rep, hd=hd)
    o_s, nk_s, nv_s = _attn_sample(q[npr:npr + ns].reshape(ns, rep, kvw), kv[npr:npr + ns].reshape(ns, 1, 2 * kvw),
                                   cache_k_win.reshape(ns, window, kvw), cache_v_win.reshape(ns, window, kvw),
                                   bias_s, sink_s, nb=16, kvh=kvh, rep=rep, hd=hd)
    ot = lax.dynamic_update_slice(ot, jnp.pad(o_s.reshape(ns, nq).T, ((0, 0), (0, tm - ns))), (0, npr))

    y_p, y_s = _moe(x2, ot, wo, g_ffn[1:2], w_router[0], w_exp_gate_up[0], w_exp_down[0], g_final.reshape(1, d),
                    tm=tm, fc=w_exp_down.shape[2] // 2, n_valid=npr + ns)

    y_prompt = y_p.reshape(bsz, seq, d)
    y_sample = y_s[:ns].reshape(ns, 1, d)
    kv_tail = jnp.stack([kv[(b + 1) * seq - window:(b + 1) * seq] for b in range(bsz)])
    kv_tail = kv_tail.reshape(bsz, window, 2, kvh, hd).transpose(2, 0, 1, 3, 4)
    return (y_prompt, y_sample,
            st_re.reshape(1, bsz, g, p), st_im.reshape(1, bsz, g, p), kv_tail[0], kv_tail[1],
            hs_re.reshape(1, ns, g, p), hs_im.reshape(1, ns, g, p),
            nk_s.reshape(ns, window, kvh, hd), nv_s.reshape(ns, window, kvh, hd))
```

```python
import functools
import math

import numpy as np
import jax
import jax.numpy as jnp
from jax import lax
from jax.experimental import pallas as pl
from jax.experimental.pallas import tpu as pltpu

F32 = jnp.float32
BF16 = jnp.bfloat16

EPS = 1e-6
NEG_INF = -1e30
TOP_K = 2
MAX_DISTANCE = 128
MXU_DIM = 256
LANES = 128
VMEM_LIMIT_BYTES = 56 * 1024 * 1024


def _dot(a, b):
    return jnp.dot(a, b, preferred_element_type=F32)


def _rms(x, g):
    return x * lax.rsqrt(jnp.mean(x * x, axis=-1, keepdims=True) + EPS) * g


def _const_spec(shape):
    nd = len(shape)
    return pl.BlockSpec(shape, lambda *_: (0,) * nd)


def _params(*sem):
    return pltpu.CompilerParams(dimension_semantics=sem, vmem_limit_bytes=VMEM_LIMIT_BYTES)


def _zoh_kernel(a_re_ref, a_im_ref, log_dt_ref, b_re_ref, b_im_ref, ct_re_ref, ct_im_ref,
                lam_re_ref, lam_im_ref, wb_re_ref, wb_im_ref, wc_re_ref, wc_imn_ref, *, hch, p):
    a_re = a_re_ref[...]
    a_im = a_im_ref[...]
    dt = jnp.exp(log_dt_ref[...])
    mag = jnp.exp(a_re * dt)
    lr = mag * jnp.cos(a_im * dt)
    li = mag * jnp.sin(a_im * dt)
    lam_re_ref[...] = lr
    lam_im_ref[...] = li
    nr = lr - 1.0
    den = a_re * a_re + a_im * a_im
    qr = (nr * a_re + li * a_im) / den
    qi = (li * a_re - nr * a_im) / den
    b_re = b_re_ref[...]
    b_im = b_im_ref[...]
    bb_re = qr * b_re - qi * b_im
    bb_im = qr * b_im + qi * b_re

    nkb, ublk, sblk = wb_re_ref.shape
    gpb = ublk // hch
    sh_h, sh_p = int(math.log2(hch)), int(math.log2(p))
    row = lax.broadcasted_iota(jnp.int32, (ublk, sblk), 0)
    col = lax.broadcasted_iota(jnp.int32, (ublk, sblk), 1)
    diag_in = lax.shift_right_logical(row, sh_h) == lax.shift_right_logical(col, sh_p)
    row = lax.broadcasted_iota(jnp.int32, (sblk, ublk), 0)
    col = lax.broadcasted_iota(jnp.int32, (sblk, ublk), 1)
    diag_out = lax.shift_right_logical(row, sh_p) == lax.shift_right_logical(col, sh_h)
    spread = jnp.where(lax.broadcasted_iota(jnp.int32, (hch, ublk), 0)
                       == jnp.bitwise_and(lax.broadcasted_iota(jnp.int32, (hch, ublk), 1), hch - 1),
                       1.0, 0.0).astype(BF16)
    for kb in range(nkb):
        cs = slice(kb * sblk, (kb + 1) * sblk)
        wb_re_ref[kb] = jnp.where(diag_in, jnp.concatenate([bb_re[:, cs]] * gpb, axis=0), 0.0).astype(BF16)
        wb_im_ref[kb] = jnp.where(diag_in, jnp.concatenate([bb_im[:, cs]] * gpb, axis=0), 0.0).astype(BF16)
        c_re = _dot(ct_re_ref[cs, :].astype(BF16), spread)
        c_im = _dot(ct_im_ref[cs, :].astype(BF16), spread)
        wc_re_ref[kb] = jnp.where(diag_out, c_re, 0.0).astype(BF16)
        wc_imn_ref[kb] = jnp.where(diag_out, -c_im, 0.0).astype(BF16)


def _zoh(a_re, a_im, log_dt, b_re, b_im, c_re, c_im):
    g, p = a_re.shape
    h = b_re.shape[-1]
    gp = g * p
    gpb = MXU_DIM // h
    nkb = g // gpb
    assert h & (h - 1) == 0 and p & (p - 1) == 0 and g % gpb == 0
    row = jax.ShapeDtypeStruct((1, gp), F32)
    wb = jax.ShapeDtypeStruct((nkb, gpb * h, gpb * p), BF16)
    wc = jax.ShapeDtypeStruct((nkb, gpb * p, gpb * h), BF16)
    return pl.pallas_call(
        functools.partial(_zoh_kernel, hch=h, p=p),
        out_shape=(row, row, wb, wb, wc, wc),
        name="s5_zoh",
    )(a_re.reshape(1, gp), a_im.reshape(1, gp), jnp.repeat(log_dt, p).reshape(1, gp),
      b_re.transpose(2, 0, 1).reshape(h, gp), b_im.transpose(2, 0, 1).reshape(h, gp),
      c_re.transpose(0, 2, 1).reshape(gp, h), c_im.transpose(0, 2, 1).reshape(gp, h))


def _ssm_in_proj(ub, wb_re_ref, wb_im_ref, bu_re, bu_im):
    nkb, ublk, sblk = wb_re_ref.shape
    for kb in range(nkb):
        ukb = ub[:, kb * ublk:(kb + 1) * ublk]
        bu_re[:, kb * sblk:(kb + 1) * sblk] = _dot(ukb, wb_re_ref[kb])
        bu_im[:, kb * sblk:(kb + 1) * sblk] = _dot(ukb, wb_im_ref[kb])


def _ssm_out_proj(h_re, h_im, wc_re_ref, wc_imn_ref):
    nkb, sblk, _ = wc_re_ref.shape
    ys = []
    for kb in range(nkb):
        hr = h_re[:, kb * sblk:(kb + 1) * sblk].astype(BF16)
        hi = h_im[:, kb * sblk:(kb + 1) * sblk].astype(BF16)
        ys.append(_dot(hr, wc_re_ref[kb]) + _dot(hi, wc_imn_ref[kb]))
    return jnp.concatenate(ys, axis=1)


def _ssm_glu(x, u, y, d_ref, wglu_ref):
    z = jax.nn.gelu(y + d_ref[...] * u).astype(BF16)
    gl = _dot(z, wglu_ref[...])
    d = x.shape[1]
    return x + gl[:, :d] * jax.nn.sigmoid(gl[:, d:])


def _split3(v):
    hi = v.astype(BF16)
    r1 = v - hi.astype(F32)
    mid = r1.astype(BF16)
    lo = (r1 - mid.astype(F32)).astype(BF16)
    return hi, mid, lo


def _ssm_prompt_kernel(x_ref, g_ref, lam_re_ref, lam_im_ref, wb_re_ref, wb_im_ref,
                       wc_re_ref, wc_imn_ref, d_ref, wglu_ref, perm_ref, permt_ref,
                       out_ref, st_re_ref, st_im_ref, bu_re, bu_im, *, lc, bsz, lane_chunk):
    @pl.when(pl.program_id(0) == 0)
    def _():
        st_re_ref[...] = jnp.zeros_like(st_re_ref)
        st_im_ref[...] = jnp.zeros_like(st_im_ref)

    d = x_ref.shape[2]
    x = x_ref[...].reshape(bsz * lc, d)
    u = _rms(x, g_ref[...])
    ub_tb = _dot(perm_ref[...], u.astype(BF16)).astype(BF16)
    _ssm_in_proj(ub_tb, wb_re_ref, wb_im_ref, bu_re, bu_im)

    gp = bu_re.shape[1]
    for c0 in range(0, gp, lane_chunk):
        sl = slice(c0, c0 + lane_chunk)
        lre = jnp.broadcast_to(lam_re_ref[:, sl], (bsz, lane_chunk))
        lim = jnp.broadcast_to(lam_im_ref[:, sl], (bsz, lane_chunk))
        hr = st_re_ref[:, sl]
        hi = st_im_ref[:, sl]
        for t in range(lc):
            rows = slice(t * bsz, (t + 1) * bsz)
            hr, hi = (lre * hr - lim * hi + bu_re[rows, sl], lre * hi + lim * hr + bu_im[rows, sl])
            bu_re[rows, sl] = hr
            bu_im[rows, sl] = hi
        st_re_ref[:, sl] = hr
        st_im_ref[:, sl] = hi

    permt = permt_ref[...]
    hi, mid, lo = _split3(_ssm_out_proj(bu_re, bu_im, wc_re_ref, wc_imn_ref))
    y = (_dot(permt, hi) + _dot(permt, mid)) + _dot(permt, lo)
    out_ref[...] = _ssm_glu(x, u, y, d_ref, wglu_ref).reshape(bsz, lc, d)


def _ssm_sample_kernel(x_ref, g_ref, lam_re_ref, lam_im_ref, h0_re_ref, h0_im_ref, wb_re_ref, wb_im_ref,
                       wc_re_ref, wc_imn_ref, d_ref, wglu_ref,
                       out_ref, h_re_ref, h_im_ref):
    x = x_ref[...]
    u = _rms(x, g_ref[...])
    _ssm_in_proj(u.astype(BF16), wb_re_ref, wb_im_ref, h_re_ref, h_im_ref)
    lre = lam_re_ref[...]
    lim = lam_im_ref[...]
    h0r = h0_re_ref[...]
    h0i = h0_im_ref[...]
    h_re_ref[...] = lre * h0r - lim * h0i + h_re_ref[...]
    h_im_ref[...] = lre * h0i + lim * h0r + h_im_ref[...]
    y = _ssm_out_proj(h_re_ref, h_im_ref, wc_re_ref, wc_imn_ref)
    n = x.shape[0]
    out_ref[0:n, :] = _ssm_glu(x, u, y, d_ref, wglu_ref)
    out_ref[n:, :] = jnp.zeros((out_ref.shape[0] - n, out_ref.shape[1]), F32)


def _ssm_prompt(x, g, lam_re, lam_im, wb_re, wb_im, wc_re, wc_imn, d_skip, wglu, *, lc):
    bsz, seq, d = x.shape
    gp = lam_re.shape[1]
    r = lc * bsz
    perm = np.zeros((r, r), np.float32)
    for b in range(bsz):
        for t in range(lc):
            perm[t * bsz + b, b * lc + t] = 1.0
    permt = jnp.asarray(perm.T, BF16)
    perm = jnp.asarray(perm, BF16)
    kern = functools.partial(_ssm_prompt_kernel, lc=lc, bsz=bsz, lane_chunk=8 * LANES)
    return pl.pallas_call(
        kern,
        grid=(seq // lc,),
        in_specs=[pl.BlockSpec((bsz, lc, d), lambda c: (0, c, 0)),
                  _const_spec(g.shape), _const_spec(lam_re.shape), _const_spec(lam_im.shape),
                  _const_spec(wb_re.shape), _const_spec(wb_im.shape),
                  _const_spec(wc_re.shape), _const_spec(wc_imn.shape),
                  _const_spec(d_skip.shape), _const_spec(wglu.shape),
                  _const_spec(perm.shape), _const_spec(permt.shape)],
        out_specs=[pl.BlockSpec((bsz, lc, d), lambda c: (0, c, 0)),
                   _const_spec((bsz, gp)), _const_spec((bsz, gp))],
        out_shape=[jax.ShapeDtypeStruct((bsz, seq, d), F32),
                   jax.ShapeDtypeStruct((bsz, gp), F32), jax.ShapeDtypeStruct((bsz, gp), F32)],
        scratch_shapes=[pltpu.VMEM((r, gp), F32), pltpu.VMEM((r, gp), F32)],
        compiler_params=_params("arbitrary"),
        name="s5_prompt",
    )(x, g, lam_re, lam_im, wb_re, wb_im, wc_re, wc_imn, d_skip, wglu, perm, permt)


def _ssm_sample(x, g, lam_re, lam_im, h0_re, h0_im, wb_re, wb_im, wc_re, wc_imn, d_skip, wglu, *, pad_rows):
    n, d = x.shape
    gp = lam_re.shape[1]
    return pl.pallas_call(
        _ssm_sample_kernel,
        out_shape=[jax.ShapeDtypeStruct((pad_rows, d), F32),
                   jax.ShapeDtypeStruct((n, gp), F32), jax.ShapeDtypeStruct((n, gp), F32)],
        compiler_params=_params(),
        name="s5_sample",
    )(x, g, lam_re, lam_im, h0_re, h0_im, wb_re, wb_im, wc_re, wc_imn, d_skip, wglu)


def _ffn_chunks(d_ff):
    step = 3 * MXU_DIM
    return [(c, min(c + step, d_ff)) for c in range(0, d_ff, step)]


def _ffn_kernel(xp_ref, xs_ref, gffn_ref, wgu_ref, wd_ref, gkv_ref, wkv_ref, gq_ref, wq_ref,
                x2_ref, kv_ref, q_ref):
    x = jnp.where(pl.program_id(0) < pl.num_programs(0) - 1, xp_ref[...], xs_ref[...])
    hb = _rms(x, gffn_ref[...]).astype(BF16)
    d_ff = wd_ref.shape[0]
    acc = None
    for c0, c1 in _ffn_chunks(d_ff):
        a = _dot(hb, wgu_ref[:, c0:c1])
        b = _dot(hb, wgu_ref[:, d_ff + c0:d_ff + c1])
        part = _dot((jax.nn.silu(a) * b).astype(BF16), wd_ref[c0:c1, :])
        acc = part if acc is None else acc + part
    x2 = x + acc
    x2_ref[...] = x2
    kv_ref[...] = _dot(_rms(x2, gkv_ref[...]).astype(BF16), wkv_ref[...])
    q_ref[...] = _dot(_rms(x2, gq_ref[...]).astype(BF16), wq_ref[...]).astype(BF16)


def _ffn(xp, xs, gffn, wgu, wd, gkv, wkv, gq, wq):
    tm, d = xs.shape
    npt = xp.shape[0] // tm
    rows = xp.shape[0] + tm
    kvw = wkv.shape[1]
    nq = wq.shape[1]
    return pl.pallas_call(
        _ffn_kernel,
        grid=(npt + 1,),
        in_specs=[pl.BlockSpec((tm, d), lambda i: (jnp.minimum(i, npt - 1), 0)),
                  _const_spec(xs.shape),
                  _const_spec(gffn.shape), _const_spec(wgu.shape), _const_spec(wd.shape),
                  _const_spec(gkv.shape), _const_spec(wkv.shape),
                  _const_spec(gq.shape), _const_spec(wq.shape)],
        out_specs=[pl.BlockSpec((tm, d), lambda i: (i, 0)),
                   pl.BlockSpec((tm, kvw), lambda i: (i, 0)),
                   pl.BlockSpec((tm, nq), lambda i: (i, 0))],
        out_shape=[jax.ShapeDtypeStruct((rows, d), F32),
                   jax.ShapeDtypeStruct((rows, kvw), F32),
                   jax.ShapeDtypeStruct((rows, nq), BF16)],
        compiler_params=_params("arbitrary"),
        name="ffn_kv_q",
    )(xp, xs, gffn, wgu, wd, gkv, wkv, gq, wq)


def _t5_bucket(dist, num_buckets):
    max_exact = num_buckets // 2
    d = jnp.maximum(dist, 0)
    large = max_exact + (jnp.log(jnp.maximum(d, 1).astype(F32) / max_exact)
                         / math.log(MAX_DISTANCE / max_exact) * (num_buckets - max_exact)).astype(jnp.int32)
    large = jnp.minimum(large, num_buckets - 1)
    return jnp.where(d < max_exact, d, large)


def _bias_kernel(bm_ref, bs_ref, rb_ref, bias_ref, bias_s_ref, *, kvh, rep):
    nb, nh = rb_ref.shape
    bm = bm_ref[...]
    bs = bs_ref[...]

    def lookup(buckets, h):
        def body(k, acc):
            return jnp.where(buckets == k, rb_ref[k, h], acc)
        return lax.fori_loop(0, nb, body, jnp.zeros(buckets.shape, F32))

    w = bm.shape[0]
    for g in range(kvh):
        for r in range(rep):
            bias_ref[g, :, r * w:(r + 1) * w] = lookup(bm, g * rep + r) * math.log2(math.e)
            bias_s_ref[g, r:r + 1, :] = lookup(bs, g * rep + r)


def _bias_tables(rel_bias, window, kvh, rep):
    nb, nh = rel_bias.shape
    key = jnp.arange(window)[:, None]
    qry = jnp.arange(window)[None, :]
    bm = _t5_bucket((qry - key) % window, nb).astype(jnp.int32)
    bs = _t5_bucket(window - 1 - qry, nb).astype(jnp.int32)
    kern = functools.partial(_bias_kernel, kvh=kvh, rep=rep)
    return pl.pallas_call(
        kern,
        in_specs=[pl.BlockSpec(memory_space=pltpu.VMEM), pl.BlockSpec(memory_space=pltpu.VMEM),
                  pl.BlockSpec(memory_space=pltpu.SMEM)],
        out_shape=[jax.ShapeDtypeStruct((kvh, window, rep * window), F32),
                   jax.ShapeDtypeStruct((kvh, rep, window), F32)],
        name="t5_bias",
    )(bm, bs, rel_bias)


def _attn_prompt_kernel(q_ref, kvc_ref, kvp_ref, bias_ref, sink_ref, ot_ref, *, kvh, rep, hd, scale):
    w = q_ref.shape[0]
    kvw = kvh * hd
    first = pl.program_id(1) == 0
    q = q_ref[...]
    kvc = kvc_ref[...]
    kvp = kvp_ref[...]
    lane = lax.broadcasted_iota(jnp.int32, (1, LANES), 1)
    key = lax.broadcasted_iota(jnp.int32, (w, rep * w), 0)
    qry = lax.broadcasted_iota(jnp.int32, (w, rep * w), 1) % w
    upper = key > qry
    mask_add = jnp.where(jnp.logical_and(upper, first), NEG_INF, 0.0)
    log2e = math.log2(math.e)
    heads_per_blk = LANES // hd
    for p in range(kvw // LANES):
        cs = slice(p * LANES, (p + 1) * LANES)
        kc, kp = kvc[:, cs], kvp[:, cs]
        vs = slice(kvw + p * LANES, kvw + (p + 1) * LANES)
        vb = jnp.concatenate([kvp[:, vs], kvc[:, vs]], axis=0).astype(BF16)
        qg = jnp.concatenate([q[:, r * kvw + p * LANES:r * kvw + (p + 1) * LANES] for r in range(rep)], axis=0)
        halves = []
        for half in range(heads_per_blk):
            g = p * heads_per_blk + half
            lmask = jnp.logical_and(lane >= half * hd, lane < (half + 1) * hd)
            kb = jnp.concatenate([jnp.where(lmask, kp, 0.0), jnp.where(lmask, kc, 0.0)], axis=0).astype(BF16)
            s = lax.dot_general(kb, qg, (((1,), (1,)), ((), ())), preferred_element_type=F32)
            sc = jnp.where(upper, s[:w], s[w:]) * (scale * log2e) + (bias_ref[g] + mask_add)
            sink = jnp.concatenate([jnp.full((1, w), sink_ref[g * rep + r] * log2e, F32) for r in range(rep)],
                                   axis=1)
            m = jnp.maximum(jnp.max(sc, axis=0, keepdims=True), sink)
            pe = jnp.exp2(sc - m)
            denom = jnp.sum(pe, axis=0, keepdims=True) + jnp.exp2(sink - m)
            pcat = jnp.concatenate([jnp.where(upper, pe, 0.0), jnp.where(upper, 0.0, pe)], axis=0).astype(BF16)
            og = lax.dot_general(vb, pcat, (((0,), (0,)), ((), ())), preferred_element_type=F32)
            halves.append((og / denom)[half * hd:(half + 1) * hd])
        o_blk = jnp.concatenate(halves, axis=0).astype(BF16)
        for r in range(rep):
            ot_ref[r * kvw + p * LANES:r * kvw + (p + 1) * LANES, :] = o_blk[:, r * w:(r + 1) * w]


def _attn_prompt(q, kv, bias, sinks, *, bsz, nblk, window, kvh, rep, hd):
    rows, nq = q.shape
    kvw2 = kv.shape[1]
    kern = functools.partial(_attn_prompt_kernel, kvh=kvh, rep=rep, hd=hd, scale=1.0 / math.sqrt(hd))
    return pl.pallas_call(
        kern,
        grid=(bsz, nblk),
        in_specs=[pl.BlockSpec((window, nq), lambda b, i: (b * nblk + i, 0)),
                  pl.BlockSpec((window, kvw2), lambda b, i: (b * nblk + i, 0)),
                  pl.BlockSpec((window, kvw2), lambda b, i: (b * nblk + jnp.maximum(i - 1, 0), 0)),
                  _const_spec(bias.shape),
                  pl.BlockSpec(memory_space=pltpu.SMEM)],
        out_specs=pl.BlockSpec((nq, window), lambda b, i: (0, b * nblk + i)),
        out_shape=jax.ShapeDtypeStruct((nq, rows), BF16),
        compiler_params=_params("arbitrary", "arbitrary"),
        name="swa_prompt",
    )(q, kv, kv, bias, sinks)


def _attn_sample_kernel(q_ref, kv_ref, ck_ref, cv_ref, bias_ref, sink_ref, o_ref, nk_ref, nv_ref,
                        *, kvh, rep, hd, scale):
    nb, w, kvw = ck_ref.shape
    kv = kv_ref[...]
    lane = lax.broadcasted_iota(jnp.int32, (nb, rep, kvw), 2)

    def shifted(c_ref, n_ref, new):
        flat = c_ref[...].reshape(nb * w, kvw)
        n_ref[...] = pltpu.roll(flat, nb * w - 1, axis=0).reshape(nb, w, kvw)
        n_ref[:, w - 1:w, :] = new
        return n_ref[...].astype(BF16)

    nkb = shifted(ck_ref, nk_ref, kv[:, :, :kvw])
    nvb = shifted(cv_ref, nv_ref, kv[:, :, kvw:])
    q = q_ref[...].astype(F32)
    o = jnp.zeros((nb, rep, kvw), F32)
    for g in range(kvh):
        lmask = jnp.logical_and(lane >= g * hd, lane < (g + 1) * hd)
        qg = jnp.where(lmask, q, 0.0).astype(BF16)
        s = jnp.einsum("nrc,njc->nrj", qg, nkb, preferred_element_type=F32)
        sc = s * scale + bias_ref[g][None]
        sink = sink_ref[g][None]
        m = jnp.maximum(jnp.max(sc, axis=-1, keepdims=True), sink)
        pe = jnp.exp(sc - m)
        probs = pe / (jnp.sum(pe, axis=-1, keepdims=True) + jnp.exp(sink - m))
        og = jnp.einsum("nrj,njc->nrc", probs.astype(BF16), nvb, preferred_element_type=F32)
        o = jnp.where(lmask, og, o)
    o_ref[...] = o.astype(BF16)


def _attn_sample(q3, kv3, ck, cv, bias_s, sink_s, *, nb, kvh, rep, hd):
    n, w, kvw = ck.shape
    kern = functools.partial(_attn_sample_kernel, kvh=kvh, rep=rep, hd=hd, scale=1.0 / math.sqrt(hd))
    cache_spec = pl.BlockSpec((nb, w, kvw), lambda i: (i, 0, 0))
    return pl.pallas_call(
        kern,
        grid=(n // nb,),
        in_specs=[pl.BlockSpec((nb, rep, kvw), lambda i: (i, 0, 0)),
                  pl.BlockSpec((nb, 1, 2 * kvw), lambda i: (i, 0, 0)),
                  cache_spec, cache_spec,
                  _const_spec(bias_s.shape), _const_spec(sink_s.shape)],
        out_specs=[pl.BlockSpec((nb, rep, kvw), lambda i: (i, 0, 0)), cache_spec, cache_spec],
        out_shape=[jax.ShapeDtypeStruct((n, rep, kvw), BF16),
                   jax.ShapeDtypeStruct((n, w, kvw), F32), jax.ShapeDtypeStruct((n, w, kvw), F32)],
        compiler_params=_params("arbitrary"),
        name="swa_sample",
    )(q3, kv3, ck, cv, bias_s, sink_s)


CHUNK = 16
SUB = 256
UP_ROW_BLOCK = 1024
DOWN_ROW_BLOCK = 512
TOKEN_TILE = 512


def _chunk_copy(src, src_row, dst, dst_row, sem):
    return pltpu.make_async_copy(src.at[pl.ds(pl.multiple_of(src_row, CHUNK), CHUNK), :],
                                 dst.at[pl.ds(pl.multiple_of(dst_row, CHUNK), CHUNK), :], sem)


def _moe_route_kernel(x2_ref, ot_ref, wo_ref, gffn_ref, wr_ref, lst_ref, ust_ref,
                      x3_ref, cm_ref, tmeta_ref, xs_hbm,
                      comp_s, zero_s, base_v, pend_sm, sem, *, ne, cap, n_valid):
    i = pl.program_id(0)
    nt = pl.num_programs(0)
    tm = x2_ref.shape[0]
    w = comp_s.shape[1]
    slot = lax.rem(i, 2)

    @pl.when(i == 0)
    def _():
        base_v[...] = jnp.zeros_like(base_v)
        zero_s[...] = jnp.zeros_like(zero_s)

    def drain(n):
        def body(c, carry):
            _chunk_copy(zero_s, 0, xs_hbm, 0, sem).wait()
            return carry
        lax.fori_loop(0, n, body, 0)

    @pl.when(i > 0)
    def _():
        drain(pend_sm[0])

    x3 = x2_ref[...] + lax.dot_general(ot_ref[...], wo_ref[...], (((0,), (0,)), ((), ())),
                                       preferred_element_type=F32)
    x3_ref[...] = x3
    hb = _rms(x3, gffn_ref[...]).astype(BF16)

    lane = lax.broadcasted_iota(jnp.int32, (tm, LANES), 1)
    logits = jnp.where(lane < ne, _dot(hb, wr_ref[...]), -jnp.inf)
    m1 = jnp.max(logits, axis=-1, keepdims=True)
    i1 = jnp.min(jnp.where(logits == m1, lane, LANES), axis=-1, keepdims=True)
    rest = jnp.where(lane == i1, -jnp.inf, logits)
    m2 = jnp.max(rest, axis=-1, keepdims=True)
    i2 = jnp.min(jnp.where(rest == m2, lane, LANES), axis=-1, keepdims=True)
    e2 = jnp.exp(m2 - m1)
    g1 = 1.0 / (1.0 + e2)
    g2 = e2 / (1.0 + e2)

    live = i * tm + lax.broadcasted_iota(jnp.int32, (tm, 1), 0) < n_valid
    sel = jnp.where(jnp.logical_and(live, jnp.logical_or(lane == i1, lane == i2)), 1.0, 0.0)
    rank = _dot(lst_ref[...], sel.astype(BF16))
    cnt = jnp.sum(sel, axis=0, keepdims=True)
    cpad = jnp.floor((cnt + (CHUNK - 1)) / CHUNK) * CHUNK
    loff = _dot(jnp.broadcast_to(cpad, (8, LANES)).astype(BF16), ust_ref[...])[0:1]
    dest = loff + rank
    ld1 = jnp.where(live, jnp.sum(jnp.where(lane == i1, dest, 0.0), axis=-1, keepdims=True), -1.0)
    ld2 = jnp.where(live, jnp.sum(jnp.where(lane == i2, dest, 0.0), axis=-1, keepdims=True), -1.0)
    cm = jnp.where(lane == 0, ld1, jnp.where(lane == 1, ld2, jnp.where(lane == 2, g1, jnp.where(lane == 3, g2, 0.0))))
    cm_ref[...] = cm

    rm = cm.T
    rowi = lax.broadcasted_iota(jnp.int32, (w, tm), 0).astype(F32)
    place = (jnp.where(rowi == rm[0:1], 1.0, 0.0) + jnp.where(rowi == rm[1:2], 1.0, 0.0)).astype(BF16)
    comp_s[slot] = _dot(place, hb).astype(BF16)

    base = base_v[...]
    srow = lax.broadcasted_iota(jnp.int32, (8, LANES), 0)
    tmeta_ref[...] = jnp.where(srow == 0, cpad, jnp.where(srow == 1, base, jnp.where(srow == 2, base + cpad, 0.0))
                               ).astype(jnp.int32)
    base_v[...] = base + cpad

    cpad_i = cpad.astype(jnp.int32)
    loff_i = loff.astype(jnp.int32)
    base_i = base.astype(jnp.int32)
    total = 0
    tails = []
    for e in range(ne):
        n_e = cpad_i[0, e]
        src0 = loff_i[0, e]
        dst0 = base_i[0, e] + e * cap

        def start(c, carry, src0=src0, dst0=dst0):
            _chunk_copy(comp_s.at[slot], src0 + c * CHUNK, xs_hbm, dst0 + c * CHUNK, sem).start()
            return carry

        nchunks = lax.shift_right_logical(n_e, int(math.log2(CHUNK)))
        lax.fori_loop(0, nchunks, start, 0)
        total = total + nchunks
        tails.append(dst0 + n_e)
    pend_sm[0] = total

    @pl.when(i == nt - 1)
    def _():
        nz_total = 0
        for e in range(ne):
            end = tails[e]
            nz = lax.shift_right_logical(lax.rem(SUB - lax.rem(end, SUB), SUB), int(math.log2(CHUNK)))

            def zstart(c, carry, end=end):
                _chunk_copy(zero_s, 0, xs_hbm, end + c * CHUNK, sem).start()
                return carry

            lax.fori_loop(0, nz, zstart, 0)
            nz_total = nz_total + nz
        drain(total + nz_total)


def _moe_route(x2, ot, wo, gffn, wr_pad, *, tm, ne, cap, n_valid):
    rows, d = x2.shape
    nt = rows // tm
    w = TOP_K * tm + LANES
    lst = jnp.asarray(np.tril(np.ones((tm, tm), np.float32), -1), BF16)
    ust = jnp.asarray(np.triu(np.ones((LANES, LANES), np.float32), 1), BF16)
    kern = functools.partial(_moe_route_kernel, ne=ne, cap=cap, n_valid=n_valid)
    return pl.pallas_call(
        kern,
        grid=(nt,),
        in_specs=[pl.BlockSpec((tm, d), lambda i: (i, 0)),
                  pl.BlockSpec((ot.shape[0], tm), lambda i: (0, i)),
                  _const_spec(wo.shape), _const_spec(gffn.shape), _const_spec(wr_pad.shape),
                  _const_spec(lst.shape), _const_spec(ust.shape)],
        out_specs=[pl.BlockSpec((tm, d), lambda i: (i, 0)),
                   pl.BlockSpec((tm, LANES), lambda i: (i, 0)),
                   pl.BlockSpec((8, LANES), lambda i: (i, 0)),
                   pl.BlockSpec(memory_space=pl.ANY)],
        out_shape=[jax.ShapeDtypeStruct((rows, d), F32),
                   jax.ShapeDtypeStruct((rows, LANES), F32),
                   jax.ShapeDtypeStruct((nt * 8, LANES), jnp.int32),
                   jax.ShapeDtypeStruct((ne * cap, d), BF16)],
        scratch_shapes=[pltpu.VMEM((2, w, d), BF16), pltpu.VMEM((CHUNK, d), BF16),
                        pltpu.VMEM((1, LANES), F32), pltpu.SMEM((1,), jnp.int32),
                        pltpu.SemaphoreType.DMA(())],
        compiler_params=_params("arbitrary"),
        name="moe_route",
    )(x2, ot, wo, gffn, wr_pad, lst, ust)


def _max_blocks(rows, nt, ne, row_block):
    return (TOP_K * rows + nt * ne * (CHUNK - 1)) // row_block + ne


def _expert_steps(nrows, nf, max_blocks, row_block):
    ne = nrows.shape[0]
    nblk = (nrows + row_block - 1) // row_block
    cum = jnp.cumsum(nblk) * nf
    total = cum[-1]
    s = jnp.minimum(jnp.arange(max_blocks * nf, dtype=jnp.int32), total - 1)
    e = jnp.minimum(jnp.sum((s[:, None] >= cum[None, :]).astype(jnp.int32), axis=1), ne - 1)
    nb_e = nblk[e]
    within = s - (cum[e] - nb_e * nf)
    f = within // nb_e
    pos = within - f * nb_e
    r = (pos + nb_e - 1) % nb_e
    nsub = jnp.clip((nrows[e] - r * row_block + SUB - 1) // SUB, 0, row_block // SUB)
    first = (pos == 0).astype(jnp.int32)
    return (e, f.astype(jnp.int32), r.astype(jnp.int32), first, nsub.astype(jnp.int32),
            jnp.reshape(total, (1,)).astype(jnp.int32))


def _moe_up_kernel(e_ref, f_ref, r_ref, first_ref, nsub_ref, n_ref, xs_ref, wg_ref, wu_ref, act_ref, wgb_s, wub_s):
    s = pl.program_id(0)

    @pl.when(s < n_ref[0])
    def _():
        @pl.when(first_ref[s] == 1)
        def _():
            wgb_s[...] = wg_ref[...].astype(BF16)
            wub_s[...] = wu_ref[...].astype(BF16)

        def body(j, carry):
            rows = pl.ds(pl.multiple_of(j * SUB, SUB), SUB)
            xsb = xs_ref[rows, :]
            a = _dot(xsb, wgb_s[...])
            b = _dot(xsb, wub_s[...])
            act_ref[rows, :] = (jax.nn.silu(a) * b).astype(BF16)
            return carry

        lax.fori_loop(0, nsub_ref[s], body, 0)


def _moe_up(steps, xs, wgu, *, fc, cap, row_block):
    ne, d, dff2 = wgu.shape
    dff = dff2 // 2
    nf = dff // fc
    cb = cap // row_block
    grid_spec = pltpu.PrefetchScalarGridSpec(
        num_scalar_prefetch=6,
        grid=(steps[0].shape[0],),
        in_specs=[pl.BlockSpec((row_block, d), lambda s, e, f, r, *_: (e[s] * cb + r[s], 0)),
                  pl.BlockSpec((None, d, fc), lambda s, e, f, r, *_: (e[s], 0, f[s])),
                  pl.BlockSpec((None, d, fc), lambda s, e, f, r, *_: (e[s], 0, nf + f[s]))],
        out_specs=pl.BlockSpec((row_block, fc), lambda s, e, f, r, *_: (e[s] * cb + r[s], f[s])),
        scratch_shapes=[pltpu.VMEM((d, fc), BF16), pltpu.VMEM((d, fc), BF16)])
    return pl.pallas_call(
        _moe_up_kernel,
        grid_spec=grid_spec,
        out_shape=jax.ShapeDtypeStruct((ne * cap, dff), BF16),
        compiler_params=_params("arbitrary"),
        name="moe_up",
    )(*steps, xs, wgu, wgu)


def _moe_down_kernel(e_ref, f_ref, r_ref, first_ref, nsub_ref, n_ref, act_ref, wd_ref, ys_ref, wdb_s):
    s = pl.program_id(0)

    @pl.when(s < n_ref[0])
    def _():
        @pl.when(first_ref[s] == 1)
        def _():
            wdb_s[...] = wd_ref[...].astype(BF16)

        def body(j, carry):
            rows = pl.ds(pl.multiple_of(j * SUB, SUB), SUB)
            ys_ref[rows, :] = _dot(act_ref[rows, :], wdb_s[...]).astype(BF16)
            return carry

        lax.fori_loop(0, nsub_ref[s], body, 0)


def _moe_down(steps, act, wd, *, cap, row_block):
    ne, dff, d = wd.shape
    cb = cap // row_block
    grid_spec = pltpu.PrefetchScalarGridSpec(
        num_scalar_prefetch=6,
        grid=(steps[0].shape[0],),
        in_specs=[pl.BlockSpec((row_block, dff), lambda s, e, f, r, *_: (e[s] * cb + r[s], 0)),
                  pl.BlockSpec((None, dff, d), lambda s, e, f, r, *_: (e[s], 0, 0))],
        out_specs=pl.BlockSpec((row_block, d), lambda s, e, f, r, *_: (e[s] * cb + r[s], 0)),
        scratch_shapes=[pltpu.VMEM((dff, d), BF16)])
    return pl.pallas_call(
        _moe_down_kernel,
        grid_spec=grid_spec,
        out_shape=jax.ShapeDtypeStruct((ne * cap, d), BF16),
        compiler_params=_params("arbitrary"),
        name="moe_down",
    )(*steps, act, wd)


def _moe_combine_kernel(cpad_ref, seg_ref, x3_ref, cm_ref, gfin_ref, ys_hbm, yp_ref, ys_ref, yloc_s, sem, *, ne):
    i = pl.program_id(0)
    nt = pl.num_programs(0)
    tm = x3_ref.shape[0]
    w = yloc_s.shape[1]
    slot = lax.rem(i, 2)
    shift = int(math.log2(CHUNK))

    def issue(t, sl):
        off = 0
        for e in range(ne):
            n_e = cpad_ref[t * ne + e]
            src0 = seg_ref[t * ne + e]

            def start(c, carry, src0=src0, off=off):
                _chunk_copy(ys_hbm, src0 + c * CHUNK, yloc_s.at[sl], off + c * CHUNK, sem.at[sl]).start()
                return carry

            lax.fori_loop(0, lax.shift_right_logical(n_e, shift), start, 0)
            off = off + n_e

    @pl.when(i == 0)
    def _():
        yloc_s[...] = jnp.zeros_like(yloc_s)
        issue(0, 0)

    @pl.when(i + 1 < nt)
    def _():
        issue(i + 1, 1 - slot)

    total = 0
    for e in range(ne):
        total = total + lax.shift_right_logical(cpad_ref[i * ne + e], shift)

    def wait(c, carry):
        _chunk_copy(ys_hbm, 0, yloc_s.at[slot], 0, sem.at[slot]).wait()
        return carry

    lax.fori_loop(0, total, wait, 0)

    cm = cm_ref[...]
    col = lax.broadcasted_iota(jnp.int32, (tm, w), 1).astype(F32)
    yl = yloc_s[slot]
    y1 = _dot(jnp.where(col == cm[:, 0:1], 1.0, 0.0).astype(BF16), yl)
    y2 = _dot(jnp.where(col == cm[:, 1:2], 1.0, 0.0).astype(BF16), yl)
    y = _rms(x3_ref[...] + cm[:, 2:3] * y1 + cm[:, 3:4] * y2, gfin_ref[...])

    @pl.when(i < nt - 1)
    def _():
        yp_ref[...] = y

    @pl.when(i == nt - 1)
    def _():
        ys_ref[...] = y


def _moe_combine(cpad, seg, x3, cm, gfin, ys, *, tm, ne):
    rows, d = x3.shape
    npt = rows // tm - 1
    w = TOP_K * tm + LANES
    kern = functools.partial(_moe_combine_kernel, ne=ne)
    grid_spec = pltpu.PrefetchScalarGridSpec(
        num_scalar_prefetch=2,
        grid=(npt + 1,),
        in_specs=[pl.BlockSpec((tm, d), lambda i, *_: (i, 0)),
                  pl.BlockSpec((tm, LANES), lambda i, *_: (i, 0)),
                  pl.BlockSpec(gfin.shape, lambda i, *_: (0, 0)),
                  pl.BlockSpec(memory_space=pl.ANY)],
        out_specs=[pl.BlockSpec((tm, d), lambda i, *_: (jnp.minimum(i, npt - 1), 0)),
                   pl.BlockSpec((tm, d), lambda i, *_: (0, 0))],
        scratch_shapes=[pltpu.VMEM((2, w, d), BF16), pltpu.SemaphoreType.DMA((2,))])
    return pl.pallas_call(
        kern,
        grid_spec=grid_spec,
        out_shape=[jax.ShapeDtypeStruct((npt * tm, d), F32), jax.ShapeDtypeStruct((tm, d), F32)],
        compiler_params=_params("arbitrary"),
        name="moe_combine",
    )(cpad, seg, x3, cm, gfin, ys)


def _moe(x2, ot, wo, gffn, wr, wgu, wd, gfin, *, tm, fc, n_valid):
    rows, d = x2.shape
    ne = wr.shape[1]
    nt = rows // tm
    nf = wgu.shape[2] // 2 // fc
    cap = -(-(rows + nt * (CHUNK - 1)) // UP_ROW_BLOCK) * UP_ROW_BLOCK
    wr_pad = jnp.pad(wr, ((0, 0), (0, LANES - ne))).astype(BF16)
    x3, cm, tmeta, xs = _moe_route(x2, ot, wo, gffn, wr_pad, tm=tm, ne=ne, cap=cap, n_valid=n_valid)
    tmeta = tmeta.reshape(nt, 8, LANES)[:, :, :ne]
    cpad = tmeta[:, 0, :].reshape(nt * ne)
    seg = (tmeta[:, 1, :] + jnp.arange(ne, dtype=jnp.int32)[None, :] * cap).reshape(nt * ne)
    nrows = tmeta[nt - 1, 2, :]
    up_steps = _expert_steps(nrows, nf, _max_blocks(rows, nt, ne, UP_ROW_BLOCK), UP_ROW_BLOCK)
    act = _moe_up(up_steps, xs, wgu, fc=fc, cap=cap, row_block=UP_ROW_BLOCK)
    down_steps = _expert_steps(nrows, 1, _max_blocks(rows, nt, ne, DOWN_ROW_BLOCK), DOWN_ROW_BLOCK)
    ys = _moe_down(down_steps, act, wd, cap=cap, row_block=DOWN_ROW_BLOCK)
    return _moe_combine(cpad, seg, x3, cm, gfin, ys, tm=tm, ne=ne)


def kernel(x_prompt, x_sample, state_ssm_re, state_ssm_im, cache_k_win, cache_v_win, g_mix, g_ffn, g_kv, g_final, ssm_a_re, ssm_a_im, ssm_log_dt, ssm_b_re, ssm_b_im, ssm_c_re, ssm_c_im, ssm_d, w_glu, w_kv, w_q, w_o, attn_sinks, rel_bias, w_ffn_gate_up, w_ffn_down, w_router, w_exp_gate_up, w_exp_down):
    bsz, seq, d = x_prompt.shape
    ns, dec_seq, _ = x_sample.shape
    assert dec_seq == 1 and g_mix.shape[0] == 2 and ssm_a_re.shape[0] == 1 and w_q.shape[0] == 1
    _, g, p = ssm_a_re.shape
    gp = g * p
    window, kvh, hd = cache_k_win.shape[1:]
    kvw = kvh * hd
    nh = attn_sinks.shape[1]
    rep = nh // kvh
    nq = nh * hd
    assert bsz == 8 and ns % bsz == 0 and seq % window == 0 and LANES % hd == 0

    lam_re, lam_im, wb_re, wb_im, wc_re, wc_imn = _zoh(ssm_a_re[0], ssm_a_im[0], ssm_log_dt[0],
                                                        ssm_b_re[0], ssm_b_im[0], ssm_c_re[0], ssm_c_im[0])
    wglu = w_glu[0].astype(BF16)
    d_skip = ssm_d[0].reshape(1, d)
    wgu = w_ffn_gate_up[0].astype(BF16)
    wd = w_ffn_down[0].astype(BF16)
    wkv = w_kv.astype(BF16)
    wq = w_q[0].reshape(d, kvh, rep, hd).transpose(0, 2, 1, 3).reshape(d, nq).astype(BF16)
    wo = w_o[0].reshape(kvh, rep, hd, d).transpose(1, 0, 2, 3).reshape(nq, d).astype(BF16)
    bias, bias_s = _bias_tables(rel_bias, window, kvh, rep)
    sinks = attn_sinks[0]
    sink_s = sinks.reshape(kvh, rep, 1)

    tm = TOKEN_TILE
    npr = seq * bsz
    assert npr % tm == 0 and ns <= tm

    x1_p, st_re, st_im = _ssm_prompt(x_prompt, g_mix[0:1], lam_re, lam_im, wb_re, wb_im, wc_re, wc_imn,
                                     d_skip, wglu, lc=32)
    x1_s, hs_re, hs_im = _ssm_sample(x_sample.reshape(ns, d), g_mix[0:1], lam_re, lam_im,
                                     state_ssm_re[0].reshape(ns, gp), state_ssm_im[0].reshape(ns, gp),
                                     wb_re, wb_im, wc_re, wc_imn, d_skip, wglu, pad_rows=tm)

    x2, kv, q = _ffn(x1_p.reshape(npr, d), x1_s, g_ffn[0:1], wgu, wd, g_kv.reshape(1, d), wkv, g_mix[1:2], wq)

    ot = _attn_prompt(q, kv, bias, sinks, bsz=bsz, nblk=seq // window, window=window, kvh=kvh, rep=rep, hd=hd)
    o_s, nk_s, nv_s = _attn_sample(q[npr:npr + ns].reshape(ns, rep, kvw), kv[npr:npr + ns].reshape(ns, 1, 2 * kvw),
                                   cache_k_win.reshape(ns, window, kvw), cache_v_win.reshape(ns, window, kvw),
                                   bias_s, sink_s, nb=16, kvh=kvh, rep=rep, hd=hd)
    ot = lax.dynamic_update_slice(ot, jnp.pad(o_s.reshape(ns, nq).T, ((0, 0), (0, tm - ns))), (0, npr))

    y_p, y_s = _moe(x2, ot, wo, g_ffn[1:2], w_router[0], w_exp_gate_up[0], w_exp_down[0], g_final.reshape(1, d),
                    tm=tm, fc=w_exp_down.shape[2] // 2, n_valid=npr + ns)

    y_prompt = y_p.reshape(bsz, seq, d)
    y_sample = y_s[:ns].reshape(ns, 1, d)
    kv_tail = jnp.stack([kv[(b + 1) * seq - window:(b + 1) * seq] for b in range(bsz)])
    kv_tail = kv_tail.reshape(bsz, window, 2, kvh, hd).transpose(2, 0, 1, 3, 4)
    return (y_prompt, y_sample,
            st_re.reshape(1, bsz, g, p), st_im.reshape(1, bsz, g, p), kv_tail[0], kv_tail[1],
            hs_re.reshape(1, ns, g, p), hs_im.reshape(1, ns, g, p),
            nk_s.reshape(ns, window, kvh, hd), nv_s.reshape(ns, window, kvh, hd))
```

```python
import functools
import math

import numpy as np
import jax
import jax.numpy as jnp
from jax import lax
from jax.experimental import pallas as pl
from jax.experimental.pallas import tpu as pltpu

F32 = jnp.float32
BF16 = jnp.bfloat16

EPS = 1e-6
NEG_INF = -1e30
TOP_K = 2
MAX_DISTANCE = 128
MXU_DIM = 256
LANES = 128
VMEM_LIMIT_BYTES = 56 * 1024 * 1024


def _dot(a, b):
    return jnp.dot(a, b, preferred_element_type=F32)


def _rms(x, g):
    return x * lax.rsqrt(jnp.mean(x * x, axis=-1, keepdims=True) + EPS) * g


def _const_spec(shape):
    nd = len(shape)
    return pl.BlockSpec(shape, lambda *_: (0,) * nd)


def _params(*sem):
    return pltpu.CompilerParams(dimension_semantics=sem, vmem_limit_bytes=VMEM_LIMIT_BYTES)


def _zoh_kernel(a_re_ref, a_im_ref, log_dt_ref, b_re_ref, b_im_ref, ct_re_ref, ct_im_ref,
                lam_re_ref, lam_im_ref, wb_re_ref, wb_im_ref, wc_re_ref, wc_imn_ref, *, hch, p):
    a_re = a_re_ref[...]
    a_im = a_im_ref[...]
    dt = jnp.exp(log_dt_ref[...])
    mag = jnp.exp(a_re * dt)
    lr = mag * jnp.cos(a_im * dt)
    li = mag * jnp.sin(a_im * dt)
    lam_re_ref[...] = lr
    lam_im_ref[...] = li
    nr = lr - 1.0
    den = a_re * a_re + a_im * a_im
    qr = (nr * a_re + li * a_im) / den
    qi = (li * a_re - nr * a_im) / den
    b_re = b_re_ref[...]
    b_im = b_im_ref[...]
    bb_re = qr * b_re - qi * b_im
    bb_im = qr * b_im + qi * b_re

    nkb, ublk, sblk = wb_re_ref.shape
    gpb = ublk // hch
    sh_h, sh_p = int(math.log2(hch)), int(math.log2(p))
    row = lax.broadcasted_iota(jnp.int32, (ublk, sblk), 0)
    col = lax.broadcasted_iota(jnp.int32, (ublk, sblk), 1)
    diag_in = lax.shift_right_logical(row, sh_h) == lax.shift_right_logical(col, sh_p)
    row = lax.broadcasted_iota(jnp.int32, (sblk, ublk), 0)
    col = lax.broadcasted_iota(jnp.int32, (sblk, ublk), 1)
    diag_out = lax.shift_right_logical(row, sh_p) == lax.shift_right_logical(col, sh_h)
    spread = jnp.where(lax.broadcasted_iota(jnp.int32, (hch, ublk), 0)
                       == jnp.bitwise_and(lax.broadcasted_iota(jnp.int32, (hch, ublk), 1), hch - 1),
                       1.0, 0.0).astype(BF16)
    for kb in range(nkb):
        cs = slice(kb * sblk, (kb + 1) * sblk)
        wb_re_ref[kb] = jnp.where(diag_in, jnp.concatenate([bb_re[:, cs]] * gpb, axis=0), 0.0).astype(BF16)
        wb_im_ref[kb] = jnp.where(diag_in, jnp.concatenate([bb_im[:, cs]] * gpb, axis=0), 0.0).astype(BF16)
        c_re = _dot(ct_re_ref[cs, :].astype(BF16), spread)
        c_im = _dot(ct_im_ref[cs, :].astype(BF16), spread)
        wc_re_ref[kb] = jnp.where(diag_out, c_re, 0.0).astype(BF16)
        wc_imn_ref[kb] = jnp.where(diag_out, -c_im, 0.0).astype(BF16)


def _zoh(a_re, a_im, log_dt, b_re, b_im, c_re, c_im):
    g, p = a_re.shape
    h = b_re.shape[-1]
    gp = g * p
    gpb = MXU_DIM // h
    nkb = g // gpb
    assert h & (h - 1) == 0 and p & (p - 1) == 0 and g % gpb == 0
    row = jax.ShapeDtypeStruct((1, gp), F32)
    wb = jax.ShapeDtypeStruct((nkb, gpb * h, gpb * p), BF16)
    wc = jax.ShapeDtypeStruct((nkb, gpb * p, gpb * h), BF16)
    return pl.pallas_call(
        functools.partial(_zoh_kernel, hch=h, p=p),
        out_shape=(row, row, wb, wb, wc, wc),
        name="s5_zoh",
    )(a_re.reshape(1, gp), a_im.reshape(1, gp), jnp.repeat(log_dt, p).reshape(1, gp),
      b_re.transpose(2, 0, 1).reshape(h, gp), b_im.transpose(2, 0, 1).reshape(h, gp),
      c_re.transpose(0, 2, 1).reshape(gp, h), c_im.transpose(0, 2, 1).reshape(gp, h))


def _ssm_in_proj(ub, wb_re_ref, wb_im_ref, bu_re, bu_im):
    nkb, ublk, sblk = wb_re_ref.shape
    for kb in range(nkb):
        ukb = ub[:, kb * ublk:(kb + 1) * ublk]
        bu_re[:, kb * sblk:(kb + 1) * sblk] = _dot(ukb, wb_re_ref[kb])
        bu_im[:, kb * sblk:(kb + 1) * sblk] = _dot(ukb, wb_im_ref[kb])


def _ssm_out_proj(h_re, h_im, wc_re_ref, wc_imn_ref):
    nkb, sblk, _ = wc_re_ref.shape
    ys = []
    for kb in range(nkb):
        hr = h_re[:, kb * sblk:(kb + 1) * sblk].astype(BF16)
        hi = h_im[:, kb * sblk:(kb + 1) * sblk].astype(BF16)
        ys.append(_dot(hr, wc_re_ref[kb]) + _dot(hi, wc_imn_ref[kb]))
    return jnp.concatenate(ys, axis=1)


def _ssm_glu(x, u, y, d_ref, wglu_ref):
    z = jax.nn.gelu(y + d_ref[...] * u).astype(BF16)
    gl = _dot(z, wglu_ref[...])
    d = x.shape[1]
    return x + gl[:, :d] * jax.nn.sigmoid(gl[:, d:])


def _split3(v):
    hi = v.astype(BF16)
    r1 = v - hi.astype(F32)
    mid = r1.astype(BF16)
    lo = (r1 - mid.astype(F32)).astype(BF16)
    return hi, mid, lo


def _ssm_prompt_kernel(x_ref, g_ref, lam_re_ref, lam_im_ref, wb_re_ref, wb_im_ref,
                       wc_re_ref, wc_imn_ref, d_ref, wglu_ref, perm_ref, permt_ref,
                       out_ref, st_re_ref, st_im_ref, bu_re, bu_im, *, lc, bsz, lane_chunk):
    @pl.when(pl.program_id(0) == 0)
    def _():
        st_re_ref[...] = jnp.zeros_like(st_re_ref)
        st_im_ref[...] = jnp.zeros_like(st_im_ref)

    d = x_ref.shape[2]
    x = x_ref[...].reshape(bsz * lc, d)
    u = _rms(x, g_ref[...])
    ub_tb = _dot(perm_ref[...], u.astype(BF16)).astype(BF16)
    _ssm_in_proj(ub_tb, wb_re_ref, wb_im_ref, bu_re, bu_im)

    gp = bu_re.shape[1]
    for c0 in range(0, gp, lane_chunk):
        sl = slice(c0, c0 + lane_chunk)
        lre = jnp.broadcast_to(lam_re_ref[:, sl], (bsz, lane_chunk))
        lim = jnp.broadcast_to(lam_im_ref[:, sl], (bsz, lane_chunk))
        hr = st_re_ref[:, sl]
        hi = st_im_ref[:, sl]
        for t in range(lc):
            rows = slice(t * bsz, (t + 1) * bsz)
            hr, hi = (lre * hr - lim * hi + bu_re[rows, sl], lre * hi + lim * hr + bu_im[rows, sl])
            bu_re[rows, sl] = hr
            bu_im[rows, sl] = hi
        st_re_ref[:, sl] = hr
        st_im_ref[:, sl] = hi

    permt = permt_ref[...]
    hi, mid, lo = _split3(_ssm_out_proj(bu_re, bu_im, wc_re_ref, wc_imn_ref))
    y = (_dot(permt, hi) + _dot(permt, mid)) + _dot(permt, lo)
    out_ref[...] = _ssm_glu(x, u, y, d_ref, wglu_ref).reshape(bsz, lc, d)


def _ssm_sample_kernel(x_ref, g_ref, lam_re_ref, lam_im_ref, h0_re_ref, h0_im_ref, wb_re_ref, wb_im_ref,
                       wc_re_ref, wc_imn_ref, d_ref, wglu_ref,
                       out_ref, h_re_ref, h_im_ref):
    x = x_ref[...]
    u = _rms(x, g_ref[...])
    _ssm_in_proj(u.astype(BF16), wb_re_ref, wb_im_ref, h_re_ref, h_im_ref)
    lre = lam_re_ref[...]
    lim = lam_im_ref[...]
    h0r = h0_re_ref[...]
    h0i = h0_im_ref[...]
    h_re_ref[...] = lre * h0r - lim * h0i + h_re_ref[...]
    h_im_ref[...] = lre * h0i + lim * h0r + h_im_ref[...]
    y = _ssm_out_proj(h_re_ref, h_im_ref, wc_re_ref, wc_imn_ref)
    n = x.shape[0]
    out_ref[0:n, :] = _ssm_glu(x, u, y, d_ref, wglu_ref)
    out_ref[n:, :] = jnp.zeros((out_ref.shape[0] - n, out_ref.shape[1]), F32)


def _ssm_prompt(x, g, lam_re, lam_im, wb_re, wb_im, wc_re, wc_imn, d_skip, wglu, *, lc):
    bsz, seq, d = x.shape
    gp = lam_re.shape[1]
    r = lc * bsz
    perm = np.zeros((r, r), np.float32)
    for b in range(bsz):
        for t in range(lc):
            perm[t * bsz + b, b * lc + t] = 1.0
    permt = jnp.asarray(perm.T, BF16)
    perm = jnp.asarray(perm, BF16)
    kern = functools.partial(_ssm_prompt_kernel, lc=lc, bsz=bsz, lane_chunk=8 * LANES)
    return pl.pallas_call(
        kern,
        grid=(seq // lc,),
        in_specs=[pl.BlockSpec((bsz, lc, d), lambda c: (0, c, 0)),
                  _const_spec(g.shape), _const_spec(lam_re.shape), _const_spec(lam_im.shape),
                  _const_spec(wb_re.shape), _const_spec(wb_im.shape),
                  _const_spec(wc_re.shape), _const_spec(wc_imn.shape),
                  _const_spec(d_skip.shape), _const_spec(wglu.shape),
                  _const_spec(perm.shape), _const_spec(permt.shape)],
        out_specs=[pl.BlockSpec((bsz, lc, d), lambda c: (0, c, 0)),
                   _const_spec((bsz, gp)), _const_spec((bsz, gp))],
        out_shape=[jax.ShapeDtypeStruct((bsz, seq, d), F32),
                   jax.ShapeDtypeStruct((bsz, gp), F32), jax.ShapeDtypeStruct((bsz, gp), F32)],
        scratch_shapes=[pltpu.VMEM((r, gp), F32), pltpu.VMEM((r, gp), F32)],
        compiler_params=_params("arbitrary"),
        name="s5_prompt",
    )(x, g, lam_re, lam_im, wb_re, wb_im, wc_re, wc_imn, d_skip, wglu, perm, permt)


def _ssm_sample(x, g, lam_re, lam_im, h0_re, h0_im, wb_re, wb_im, wc_re, wc_imn, d_skip, wglu, *, pad_rows):
    n, d = x.shape
    gp = lam_re.shape[1]
    return pl.pallas_call(
        _ssm_sample_kernel,
        out_shape=[jax.ShapeDtypeStruct((pad_rows, d), F32),
                   jax.ShapeDtypeStruct((n, gp), F32), jax.ShapeDtypeStruct((n, gp), F32)],
        compiler_params=_params(),
        name="s5_sample",
    )(x, g, lam_re, lam_im, h0_re, h0_im, wb_re, wb_im, wc_re, wc_imn, d_skip, wglu)


def _ffn_chunks(d_ff):
    step = 3 * MXU_DIM
    return [(c, min(c + step, d_ff)) for c in range(0, d_ff, step)]


def _ffn_kernel(xp_ref, xs_ref, gffn_ref, wgu_ref, wd_ref, gkv_ref, wkv_ref, gq_ref, wq_ref,
                x2_ref, kv_ref, q_ref):
    x = jnp.where(pl.program_id(0) < pl.num_programs(0) - 1, xp_ref[...], xs_ref[...])
    hb = _rms(x, gffn_ref[...]).astype(BF16)
    d_ff = wd_ref.shape[0]
    acc = None
    for c0, c1 in _ffn_chunks(d_ff):
        a = _dot(hb, wgu_ref[:, c0:c1])
        b = _dot(hb, wgu_ref[:, d_ff + c0:d_ff + c1])
        part = _dot((jax.nn.silu(a) * b).astype(BF16), wd_ref[c0:c1, :])
        acc = part if acc is None else acc + part
    x2 = x + acc
    x2_ref[...] = x2
    kv_ref[...] = _dot(_rms(x2, gkv_ref[...]).astype(BF16), wkv_ref[...])
    q_ref[...] = _dot(_rms(x2, gq_ref[...]).astype(BF16), wq_ref[...]).astype(BF16)


def _ffn(xp, xs, gffn, wgu, wd, gkv, wkv, gq, wq):
    tm, d = xs.shape
    npt = xp.shape[0] // tm
    rows = xp.shape[0] + tm
    kvw = wkv.shape[1]
    nq = wq.shape[1]
    return pl.pallas_call(
        _ffn_kernel,
        grid=(npt + 1,),
        in_specs=[pl.BlockSpec((tm, d), lambda i: (jnp.minimum(i, npt - 1), 0)),
                  _const_spec(xs.shape),
                  _const_spec(gffn.shape), _const_spec(wgu.shape), _const_spec(wd.shape),
                  _const_spec(gkv.shape), _const_spec(wkv.shape),
                  _const_spec(gq.shape), _const_spec(wq.shape)],
        out_specs=[pl.BlockSpec((tm, d), lambda i: (i, 0)),
                   pl.BlockSpec((tm, kvw), lambda i: (i, 0)),
                   pl.BlockSpec((tm, nq), lambda i: (i, 0))],
        out_shape=[jax.ShapeDtypeStruct((rows, d), F32),
                   jax.ShapeDtypeStruct((rows, kvw), F32),
                   jax.ShapeDtypeStruct((rows, nq), BF16)],
        compiler_params=_params("arbitrary"),
        name="ffn_kv_q",
    )(xp, xs, gffn, wgu, wd, gkv, wkv, gq, wq)


def _t5_bucket(dist, num_buckets):
    max_exact = num_buckets // 2
    d = jnp.maximum(dist, 0)
    large = max_exact + (jnp.log(jnp.maximum(d, 1).astype(F32) / max_exact)
                         / math.log(MAX_DISTANCE / max_exact) * (num_buckets - max_exact)).astype(jnp.int32)
    large = jnp.minimum(large, num_buckets - 1)
    return jnp.where(d < max_exact, d, large)


def _bias_kernel(bm_ref, bs_ref, rb_ref, bias_ref, bias_s_ref, *, kvh, rep):
    nb, nh = rb_ref.shape
    bm = bm_ref[...]
    bs = bs_ref[...]

    def lookup(buckets, h):
        def body(k, acc):
            return jnp.where(buckets == k, rb_ref[k, h], acc)
        return lax.fori_loop(0, nb, body, jnp.zeros(buckets.shape, F32))

    w = bm.shape[0]
    for g in range(kvh):
        for r in range(rep):
            bias_ref[g, :, r * w:(r + 1) * w] = lookup(bm, g * rep + r) * math.log2(math.e)
            bias_s_ref[g, r:r + 1, :] = lookup(bs, g * rep + r)


def _bias_tables(rel_bias, window, kvh, rep):
    nb, nh = rel_bias.shape
    key = jnp.arange(window)[:, None]
    qry = jnp.arange(window)[None, :]
    bm = _t5_bucket((qry - key) % window, nb).astype(jnp.int32)
    bs = _t5_bucket(window - 1 - qry, nb).astype(jnp.int32)
    kern = functools.partial(_bias_kernel, kvh=kvh, rep=rep)
    return pl.pallas_call(
        kern,
        in_specs=[pl.BlockSpec(memory_space=pltpu.VMEM), pl.BlockSpec(memory_space=pltpu.VMEM),
                  pl.BlockSpec(memory_space=pltpu.SMEM)],
        out_shape=[jax.ShapeDtypeStruct((kvh, window, rep * window), F32),
                   jax.ShapeDtypeStruct((kvh, rep, window), F32)],
        name="t5_bias",
    )(bm, bs, rel_bias)


def _attn_prompt_kernel(q_ref, kvc_ref, kvp_ref, bias_ref, sink_ref, ot_ref, *, kvh, rep, hd, scale):
    w = q_ref.shape[0]
    kvw = kvh * hd
    first = pl.program_id(1) == 0
    q = q_ref[...]
    kvc = kvc_ref[...]
    kvp = kvp_ref[...]
    lane = lax.broadcasted_iota(jnp.int32, (1, LANES), 1)
    key = lax.broadcasted_iota(jnp.int32, (w, rep * w), 0)
    qry = lax.broadcasted_iota(jnp.int32, (w, rep * w), 1) % w
    upper = key > qry
    mask_add = jnp.where(jnp.logical_and(upper, first), NEG_INF, 0.0)
    log2e = math.log2(math.e)
    heads_per_blk = LANES // hd
    for p in range(kvw // LANES):
        cs = slice(p * LANES, (p + 1) * LANES)
        kc, kp = kvc[:, cs], kvp[:, cs]
        vs = slice(kvw + p * LANES, kvw + (p + 1) * LANES)
        vb = jnp.concatenate([kvp[:, vs], kvc[:, vs]], axis=0).astype(BF16)
        qg = jnp.concatenate([q[:, r * kvw + p * LANES:r * kvw + (p + 1) * LANES] for r in range(rep)], axis=0)
        halves = []
        for half in range(heads_per_blk):
            g = p * heads_per_blk + half
            lmask = jnp.logical_and(lane >= half * hd, lane < (half + 1) * hd)
            kb = jnp.concatenate([jnp.where(lmask, kp, 0.0), jnp.where(lmask, kc, 0.0)], axis=0).astype(BF16)
            s = lax.dot_general(kb, qg, (((1,), (1,)), ((), ())), preferred_element_type=F32)
            sc = jnp.where(upper, s[:w], s[w:]) * (scale * log2e) + (bias_ref[g] + mask_add)
            sink = jnp.concatenate([jnp.full((1, w), sink_ref[g * rep + r] * log2e, F32) for r in range(rep)],
                                   axis=1)
            m = jnp.maximum(jnp.max(sc, axis=0, keepdims=True), sink)
            pe = jnp.exp2(sc - m)
            denom = jnp.sum(pe, axis=0, keepdims=True) + jnp.exp2(sink - m)
            pcat = jnp.concatenate([jnp.where(upper, pe, 0.0), jnp.where(upper, 0.0, pe)], axis=0).astype(BF16)
            og = lax.dot_general(vb, pcat, (((0,), (0,)), ((), ())), preferred_element_type=F32)
            halves.append((og / denom)[half * hd:(half + 1) * hd])
        o_blk = jnp.concatenate(halves, axis=0).astype(BF16)
        for r in range(rep):
            ot_ref[r * kvw + p * LANES:r * kvw + (p + 1) * LANES, :] = o_blk[:, r * w:(r + 1) * w]


def _attn_prompt(q, kv, bias, sinks, *, bsz, nblk, window, kvh, rep, hd):
    rows, nq = q.shape
    kvw2 = kv.shape[1]
    kern = functools.partial(_attn_prompt_kernel, kvh=kvh, rep=rep, hd=hd, scale=1.0 / math.sqrt(hd))
    return pl.pallas_call(
        kern,
        grid=(bsz, nblk),
        in_specs=[pl.BlockSpec((window, nq), lambda b, i: (b * nblk + i, 0)),
                  pl.BlockSpec((window, kvw2), lambda b, i: (b * nblk + i, 0)),
                  pl.BlockSpec((window, kvw2), lambda b, i: (b * nblk + jnp.maximum(i - 1, 0), 0)),
                  _const_spec(bias.shape),
                  pl.BlockSpec(memory_space=pltpu.SMEM)],
        out_specs=pl.BlockSpec((nq, window), lambda b, i: (0, b * nblk + i)),
        out_shape=jax.ShapeDtypeStruct((nq, rows), BF16),
        compiler_params=_params("arbitrary", "arbitrary"),
        name="swa_prompt",
    )(q, kv, kv, bias, sinks)


def _attn_sample_kernel(q_ref, kv_ref, ck_ref, cv_ref, bias_ref, sink_ref, o_ref, nk_ref, nv_ref,
                        *, kvh, rep, hd, scale):
    nb, w, kvw = ck_ref.shape
    kv = kv_ref[...]
    lane = lax.broadcasted_iota(jnp.int32, (nb, rep, kvw), 2)

    def shifted(c_ref, n_ref, new):
        flat = c_ref[...].reshape(nb * w, kvw)
        n_ref[...] = pltpu.roll(flat, nb * w - 1, axis=0).reshape(nb, w, kvw)
        n_ref[:, w - 1:w, :] = new
        return n_ref[...].astype(BF16)

    nkb = shifted(ck_ref, nk_ref, kv[:, :, :kvw])
    nvb = shifted(cv_ref, nv_ref, kv[:, :, kvw:])
    q = q_ref[...].astype(F32)
    o = jnp.zeros((nb, rep, kvw), F32)
    for g in range(kvh):
        lmask = jnp.logical_and(lane >= g * hd, lane < (g + 1) * hd)
        qg = jnp.where(lmask, q, 0.0).astype(BF16)
        s = jnp.einsum("nrc,njc->nrj", qg, nkb, preferred_element_type=F32)
        sc = s * scale + bias_ref[g][None]
        sink = sink_ref[g][None]
        m = jnp.maximum(jnp.max(sc, axis=-1, keepdims=True), sink)
        pe = jnp.exp(sc - m)
        probs = pe / (jnp.sum(pe, axis=-1, keepdims=True) + jnp.exp(sink - m))
        og = jnp.einsum("nrj,njc->nrc", probs.astype(BF16), nvb, preferred_element_type=F32)
        o = jnp.where(lmask, og, o)
    o_ref[...] = o.astype(BF16)


def _attn_sample(q3, kv3, ck, cv, bias_s, sink_s, *, nb, kvh, rep, hd):
    n, w, kvw = ck.shape
    kern = functools.partial(_attn_sample_kernel, kvh=kvh, rep=rep, hd=hd, scale=1.0 / math.sqrt(hd))
    cache_spec = pl.BlockSpec((nb, w, kvw), lambda i: (i, 0, 0))
    return pl.pallas_call(
        kern,
        grid=(n // nb,),
        in_specs=[pl.BlockSpec((nb, rep, kvw), lambda i: (i, 0, 0)),
                  pl.BlockSpec((nb, 1, 2 * kvw), lambda i: (i, 0, 0)),
                  cache_spec, cache_spec,
                  _const_spec(bias_s.shape), _const_spec(sink_s.shape)],
        out_specs=[pl.BlockSpec((nb, rep, kvw), lambda i: (i, 0, 0)), cache_spec, cache_spec],
        out_shape=[jax.ShapeDtypeStruct((n, rep, kvw), BF16),
                   jax.ShapeDtypeStruct((n, w, kvw), F32), jax.ShapeDtypeStruct((n, w, kvw), F32)],
        compiler_params=_params("arbitrary"),
        name="swa_sample",
    )(q3, kv3, ck, cv, bias_s, sink_s)


CHUNK = 16
SUB = 256
UP_ROW_BLOCK = 1024
DOWN_ROW_BLOCK = 512
TOKEN_TILE = 512
MOE_TILE = 256


def _chunk_copy(src, src_row, dst, dst_row, sem):
    return pltpu.make_async_copy(src.at[pl.ds(pl.multiple_of(src_row, CHUNK), CHUNK), :],
                                 dst.at[pl.ds(pl.multiple_of(dst_row, CHUNK), CHUNK), :], sem)


def _moe_route_kernel(x2_ref, ot_ref, wo_ref, gffn_ref, wr_ref, lst_ref, ust_ref,
                      x3_ref, cm_ref, tmeta_ref, xs_hbm,
                      comp_s, carry_s, zero_s, base_v, pend_sm, sem, *, ne, cap, n_valid):
    i = pl.program_id(0)
    nt = pl.num_programs(0)
    tm = x2_ref.shape[0]
    w = comp_s.shape[1]
    slot = lax.rem(i, 2)

    @pl.when(i == 0)
    def _():
        base_v[...] = jnp.zeros_like(base_v)
        zero_s[...] = jnp.zeros_like(zero_s)
        carry_s[...] = jnp.zeros_like(carry_s)

    def drain(n):
        def body(c, carry):
            _chunk_copy(zero_s, 0, xs_hbm, 0, sem).wait()
            return carry
        lax.fori_loop(0, n, body, 0)

    @pl.when(i > 0)
    def _():
        drain(pend_sm[0])

    x3 = x2_ref[...] + lax.dot_general(ot_ref[...], wo_ref[...], (((0,), (0,)), ((), ())),
                                       preferred_element_type=F32)
    x3_ref[...] = x3
    hb = _rms(x3, gffn_ref[...]).astype(BF16)

    lane = lax.broadcasted_iota(jnp.int32, (tm, LANES), 1)
    logits = jnp.where(lane < ne, _dot(hb, wr_ref[...]), -jnp.inf)
    m1 = jnp.max(logits, axis=-1, keepdims=True)
    i1 = jnp.min(jnp.where(logits == m1, lane, LANES), axis=-1, keepdims=True)
    rest = jnp.where(lane == i1, -jnp.inf, logits)
    m2 = jnp.max(rest, axis=-1, keepdims=True)
    i2 = jnp.min(jnp.where(rest == m2, lane, LANES), axis=-1, keepdims=True)
    e2 = jnp.exp(m2 - m1)
    g1 = 1.0 / (1.0 + e2)
    g2 = e2 / (1.0 + e2)

    live = i * tm + lax.broadcasted_iota(jnp.int32, (tm, 1), 0) < n_valid
    sel = jnp.where(jnp.logical_and(live, jnp.logical_or(lane == i1, lane == i2)), 1.0, 0.0)
    rank = _dot(lst_ref[...], sel.astype(BF16))
    cnt = jnp.sum(sel, axis=0, keepdims=True)
    fill = base_v[...]
    rem = fill - jnp.floor(fill / CHUNK) * CHUNK
    cpad = jnp.floor((rem + cnt + (CHUNK - 1)) / CHUNK) * CHUNK
    loff = _dot(jnp.broadcast_to(cpad, (8, LANES)).astype(BF16), ust_ref[...])[0:1]
    dest = loff + rem + rank
    ld1 = jnp.where(live, jnp.sum(jnp.where(lane == i1, dest, 0.0), axis=-1, keepdims=True), -1.0)
    ld2 = jnp.where(live, jnp.sum(jnp.where(lane == i2, dest, 0.0), axis=-1, keepdims=True), -1.0)
    cm = jnp.where(lane == 0, ld1, jnp.where(lane == 1, ld2, jnp.where(lane == 2, g1, jnp.where(lane == 3, g2, 0.0))))
    cm_ref[...] = cm

    rm = cm.T
    rowi = lax.broadcasted_iota(jnp.int32, (w, tm), 0).astype(F32)
    place = (jnp.where(rowi == rm[0:1], 1.0, 0.0) + jnp.where(rowi == rm[1:2], 1.0, 0.0)).astype(BF16)
    comp_s[slot] = _dot(place, hb).astype(BF16)

    base = fill - rem
    srow = lax.broadcasted_iota(jnp.int32, (8, LANES), 0)
    tmeta_ref[...] = jnp.where(srow == 0, cpad, jnp.where(srow == 1, base, jnp.where(srow == 2, fill + cnt, 0.0))
                               ).astype(jnp.int32)
    base_v[...] = fill + cnt

    cpad_i = cpad.astype(jnp.int32)
    loff_i = loff.astype(jnp.int32)
    base_i = base.astype(jnp.int32)
    tail_i = (rem + cnt - jnp.floor((rem + cnt) / CHUNK) * CHUNK).astype(jnp.int32)
    total = 0
    tails = []
    for e in range(ne):
        n_e = cpad_i[0, e]
        src0 = loff_i[0, e]
        dst0 = base_i[0, e] + e * cap

        @pl.when(n_e > 0)
        def _(e=e, n_e=n_e, src0=src0, partial=tail_i[0, e] > 0):
            head = pl.ds(pl.multiple_of(src0, CHUNK), CHUNK)
            comp_s[slot, head, :] = comp_s[slot, head, :] + carry_s[e]
            last = comp_s[slot, pl.ds(pl.multiple_of(src0 + n_e - CHUNK, CHUNK), CHUNK), :]
            carry_s[e] = jnp.where(partial, last, jnp.zeros_like(last))

        def start(c, carry, src0=src0, dst0=dst0):
            _chunk_copy(comp_s.at[slot], src0 + c * CHUNK, xs_hbm, dst0 + c * CHUNK, sem).start()
            return carry

        nchunks = lax.shift_right_logical(n_e, int(math.log2(CHUNK)))
        lax.fori_loop(0, nchunks, start, 0)
        total = total + nchunks
        tails.append(dst0 + n_e)
    pend_sm[0] = total

    @pl.when(i == nt - 1)
    def _():
        nz_total = 0
        for e in range(ne):
            end = tails[e]
            nz = lax.shift_right_logical(lax.rem(SUB - lax.rem(end, SUB), SUB), int(math.log2(CHUNK)))

            def zstart(c, carry, end=end):
                _chunk_copy(zero_s, 0, xs_hbm, end + c * CHUNK, sem).start()
                return carry

            lax.fori_loop(0, nz, zstart, 0)
            nz_total = nz_total + nz
        drain(total + nz_total)


def _moe_route(x2, ot, wo, gffn, wr_pad, *, tm, ne, cap, n_valid):
    rows, d = x2.shape
    nt = rows // tm
    w = _staging_rows(tm, ne)
    lst = jnp.asarray(np.tril(np.ones((tm, tm), np.float32), -1), BF16)
    ust = jnp.asarray(np.triu(np.ones((LANES, LANES), np.float32), 1), BF16)
    kern = functools.partial(_moe_route_kernel, ne=ne, cap=cap, n_valid=n_valid)
    return pl.pallas_call(
        kern,
        grid=(nt,),
        in_specs=[pl.BlockSpec((tm, d), lambda i: (i, 0)),
                  pl.BlockSpec((ot.shape[0], tm), lambda i: (0, i)),
                  _const_spec(wo.shape), _const_spec(gffn.shape), _const_spec(wr_pad.shape),
                  _const_spec(lst.shape), _const_spec(ust.shape)],
        out_specs=[pl.BlockSpec((tm, d), lambda i: (i, 0)),
                   pl.BlockSpec((tm, LANES), lambda i: (i, 0)),
                   pl.BlockSpec((8, LANES), lambda i: (i, 0)),
                   pl.BlockSpec(memory_space=pl.ANY)],
        out_shape=[jax.ShapeDtypeStruct((rows, d), F32),
                   jax.ShapeDtypeStruct((rows, LANES), F32),
                   jax.ShapeDtypeStruct((nt * 8, LANES), jnp.int32),
                   jax.ShapeDtypeStruct((ne * cap, d), BF16)],
        scratch_shapes=[pltpu.VMEM((2, w, d), BF16), pltpu.VMEM((ne, CHUNK, d), BF16), pltpu.VMEM((CHUNK, d), BF16),
                        pltpu.VMEM((1, LANES), F32), pltpu.SMEM((1,), jnp.int32),
                        pltpu.SemaphoreType.DMA(())],
        compiler_params=_params("arbitrary"),
        name="moe_route",
    )(x2, ot, wo, gffn, wr_pad, lst, ust)


def _staging_rows(tm, ne):
    return -(-(TOP_K * tm + 2 * ne * (CHUNK - 1)) // LANES) * LANES


def _max_blocks(rows, ne, row_block):
    return TOP_K * rows // row_block + ne


def _expert_steps(nrows, nf, max_blocks, row_block):
    ne = nrows.shape[0]
    nblk = (nrows + row_block - 1) // row_block
    cum = jnp.cumsum(nblk) * nf
    total = cum[-1]
    s = jnp.minimum(jnp.arange(max_blocks * nf, dtype=jnp.int32), total - 1)
    e = jnp.minimum(jnp.sum((s[:, None] >= cum[None, :]).astype(jnp.int32), axis=1), ne - 1)
    nb_e = nblk[e]
    within = s - (cum[e] - nb_e * nf)
    f = within // nb_e
    pos = within - f * nb_e
    r = (pos + nb_e - 1) % nb_e
    nsub = jnp.clip((nrows[e] - r * row_block + SUB - 1) // SUB, 0, row_block // SUB)
    first = (pos == 0).astype(jnp.int32)
    return (e, f.astype(jnp.int32), r.astype(jnp.int32), first, nsub.astype(jnp.int32),
            jnp.reshape(total, (1,)).astype(jnp.int32))


def _moe_up_kernel(e_ref, f_ref, r_ref, first_ref, nsub_ref, n_ref, xs_ref, wg_ref, wu_ref, act_ref, wgb_s, wub_s):
    s = pl.program_id(0)

    @pl.when(s < n_ref[0])
    def _():
        @pl.when(first_ref[s] == 1)
        def _():
            wgb_s[...] = wg_ref[...].astype(BF16)
            wub_s[...] = wu_ref[...].astype(BF16)

        def body(j, carry):
            rows = pl.ds(pl.multiple_of(j * SUB, SUB), SUB)
            xsb = xs_ref[rows, :]
            a = _dot(xsb, wgb_s[...])
            b = _dot(xsb, wub_s[...])
            act_ref[rows, :] = (jax.nn.silu(a) * b).astype(BF16)
            return carry

        lax.fori_loop(0, nsub_ref[s], body, 0)


def _moe_up(steps, xs, wgu, *, fc, cap, row_block):
    ne, d, dff2 = wgu.shape
    dff = dff2 // 2
    nf = dff // fc
    cb = cap // row_block
    grid_spec = pltpu.PrefetchScalarGridSpec(
        num_scalar_prefetch=6,
        grid=(steps[0].shape[0],),
        in_specs=[pl.BlockSpec((row_block, d), lambda s, e, f, r, *_: (e[s] * cb + r[s], 0)),
                  pl.BlockSpec((None, d, fc), lambda s, e, f, r, *_: (e[s], 0, f[s])),
                  pl.BlockSpec((None, d, fc), lambda s, e, f, r, *_: (e[s], 0, nf + f[s]))],
        out_specs=pl.BlockSpec((row_block, fc), lambda s, e, f, r, *_: (e[s] * cb + r[s], f[s])),
        scratch_shapes=[pltpu.VMEM((d, fc), BF16), pltpu.VMEM((d, fc), BF16)])
    return pl.pallas_call(
        _moe_up_kernel,
        grid_spec=grid_spec,
        out_shape=jax.ShapeDtypeStruct((ne * cap, dff), BF16),
        compiler_params=_params("arbitrary"),
        name="moe_up",
    )(*steps, xs, wgu, wgu)


def _moe_down_kernel(e_ref, f_ref, r_ref, first_ref, nsub_ref, n_ref, act_ref, wd_ref, ys_ref, wdb_s):
    s = pl.program_id(0)

    @pl.when(s < n_ref[0])
    def _():
        @pl.when(first_ref[s] == 1)
        def _():
            wdb_s[...] = wd_ref[...].astype(BF16)

        def body(j, carry):
            rows = pl.ds(pl.multiple_of(j * SUB, SUB), SUB)
            ys_ref[rows, :] = _dot(act_ref[rows, :], wdb_s[...]).astype(BF16)
            return carry

        lax.fori_loop(0, nsub_ref[s], body, 0)


def _moe_down(steps, act, wd, *, cap, row_block):
    ne, dff, d = wd.shape
    cb = cap // row_block
    grid_spec = pltpu.PrefetchScalarGridSpec(
        num_scalar_prefetch=6,
        grid=(steps[0].shape[0],),
        in_specs=[pl.BlockSpec((row_block, dff), lambda s, e, f, r, *_: (e[s] * cb + r[s], 0)),
                  pl.BlockSpec((None, dff, d), lambda s, e, f, r, *_: (e[s], 0, 0))],
        out_specs=pl.BlockSpec((row_block, d), lambda s, e, f, r, *_: (e[s] * cb + r[s], 0)),
        scratch_shapes=[pltpu.VMEM((dff, d), BF16)])
    return pl.pallas_call(
        _moe_down_kernel,
        grid_spec=grid_spec,
        out_shape=jax.ShapeDtypeStruct((ne * cap, d), BF16),
        compiler_params=_params("arbitrary"),
        name="moe_down",
    )(*steps, act, wd)


def _moe_combine_kernel(cpad_ref, seg_ref, x3_ref, cm_ref, gfin_ref, ys_hbm, yp_ref, ys_ref, yloc_s, sem,
                        *, ne, npt):
    i = pl.program_id(0)
    nt = pl.num_programs(0)
    tm = x3_ref.shape[0]
    w = yloc_s.shape[1]
    slot = lax.rem(i, 2)
    shift = int(math.log2(CHUNK))

    def issue(t, sl):
        off = 0
        for e in range(ne):
            n_e = cpad_ref[t * ne + e]
            src0 = seg_ref[t * ne + e]

            def start(c, carry, src0=src0, off=off):
                _chunk_copy(ys_hbm, src0 + c * CHUNK, yloc_s.at[sl], off + c * CHUNK, sem.at[sl]).start()
                return carry

            lax.fori_loop(0, lax.shift_right_logical(n_e, shift), start, 0)
            off = off + n_e

    @pl.when(i == 0)
    def _():
        yloc_s[...] = jnp.zeros_like(yloc_s)
        issue(0, 0)

    @pl.when(i + 1 < nt)
    def _():
        issue(i + 1, 1 - slot)

    total = 0
    for e in range(ne):
        total = total + lax.shift_right_logical(cpad_ref[i * ne + e], shift)

    def wait(c, carry):
        _chunk_copy(ys_hbm, 0, yloc_s.at[slot], 0, sem.at[slot]).wait()
        return carry

    lax.fori_loop(0, total, wait, 0)

    cm = cm_ref[...]
    col = lax.broadcasted_iota(jnp.int32, (tm, w), 1).astype(F32)
    yl = yloc_s[slot]
    y1 = _dot(jnp.where(col == cm[:, 0:1], 1.0, 0.0).astype(BF16), yl)
    y2 = _dot(jnp.where(col == cm[:, 1:2], 1.0, 0.0).astype(BF16), yl)
    y = _rms(x3_ref[...] + cm[:, 2:3] * y1 + cm[:, 3:4] * y2, gfin_ref[...])

    @pl.when(i < npt)
    def _():
        yp_ref[...] = y

    @pl.when(i >= npt)
    def _():
        ys_ref[...] = y


def _moe_combine(cpad, seg, x3, cm, gfin, ys, *, tm, ne, sample_rows):
    rows, d = x3.shape
    nt = rows // tm
    npt = (rows - sample_rows) // tm
    w = _staging_rows(tm, ne)
    kern = functools.partial(_moe_combine_kernel, ne=ne, npt=npt)
    grid_spec = pltpu.PrefetchScalarGridSpec(
        num_scalar_prefetch=2,
        grid=(nt,),
        in_specs=[pl.BlockSpec((tm, d), lambda i, *_: (i, 0)),
                  pl.BlockSpec((tm, LANES), lambda i, *_: (i, 0)),
                  pl.BlockSpec(gfin.shape, lambda i, *_: (0, 0)),
                  pl.BlockSpec(memory_space=pl.ANY)],
        out_specs=[pl.BlockSpec((tm, d), lambda i, *_: (jnp.minimum(i, npt - 1), 0)),
                   pl.BlockSpec((tm, d), lambda i, *_: (jnp.maximum(i - npt, 0), 0))],
        scratch_shapes=[pltpu.VMEM((2, w, d), BF16), pltpu.SemaphoreType.DMA((2,))])
    return pl.pallas_call(
        kern,
        grid_spec=grid_spec,
        out_shape=[jax.ShapeDtypeStruct((npt * tm, d), F32), jax.ShapeDtypeStruct((sample_rows, d), F32)],
        compiler_params=_params("arbitrary"),
        name="moe_combine",
    )(cpad, seg, x3, cm, gfin, ys)


def _moe(x2, ot, wo, gffn, wr, wgu, wd, gfin, *, tm, fc, n_valid, sample_rows):
    rows, d = x2.shape
    ne = wr.shape[1]
    nt = rows // tm
    nf = wgu.shape[2] // 2 // fc
    cap = -(-(rows + SUB) // UP_ROW_BLOCK) * UP_ROW_BLOCK
    wr_pad = jnp.pad(wr, ((0, 0), (0, LANES - ne))).astype(BF16)
    x3, cm, tmeta, xs = _moe_route(x2, ot, wo, gffn, wr_pad, tm=tm, ne=ne, cap=cap, n_valid=n_valid)
    tmeta = tmeta.reshape(nt, 8, LANES)[:, :, :ne]
    cpad = tmeta[:, 0, :].reshape(nt * ne)
    seg = (tmeta[:, 1, :] + jnp.arange(ne, dtype=jnp.int32)[None, :] * cap).reshape(nt * ne)
    nrows = tmeta[nt - 1, 2, :]
    up_steps = _expert_steps(nrows, nf, _max_blocks(rows, ne, UP_ROW_BLOCK), UP_ROW_BLOCK)
    act = _moe_up(up_steps, xs, wgu, fc=fc, cap=cap, row_block=UP_ROW_BLOCK)
    down_steps = _expert_steps(nrows, 1, _max_blocks(rows, ne, DOWN_ROW_BLOCK), DOWN_ROW_BLOCK)
    ys = _moe_down(down_steps, act, wd, cap=cap, row_block=DOWN_ROW_BLOCK)
    return _moe_combine(cpad, seg, x3, cm, gfin, ys, tm=tm, ne=ne, sample_rows=sample_rows)


def kernel(x_prompt, x_sample, state_ssm_re, state_ssm_im, cache_k_win, cache_v_win, g_mix, g_ffn, g_kv, g_final, ssm_a_re, ssm_a_im, ssm_log_dt, ssm_b_re, ssm_b_im, ssm_c_re, ssm_c_im, ssm_d, w_glu, w_kv, w_q, w_o, attn_sinks, rel_bias, w_ffn_gate_up, w_ffn_down, w_router, w_exp_gate_up, w_exp_down):
    bsz, seq, d = x_prompt.shape
    ns, dec_seq, _ = x_sample.shape
    assert dec_seq == 1 and g_mix.shape[0] == 2 and ssm_a_re.shape[0] == 1 and w_q.shape[0] == 1
    _, g, p = ssm_a_re.shape
    gp = g * p
    window, kvh, hd = cache_k_win.shape[1:]
    kvw = kvh * hd
    nh = attn_sinks.shape[1]
    rep = nh // kvh
    nq = nh * hd
    assert bsz == 8 and ns % bsz == 0 and seq % window == 0 and LANES % hd == 0

    lam_re, lam_im, wb_re, wb_im, wc_re, wc_imn = _zoh(ssm_a_re[0], ssm_a_im[0], ssm_log_dt[0],
                                                        ssm_b_re[0], ssm_b_im[0], ssm_c_re[0], ssm_c_im[0])
    wglu = w_glu[0].astype(BF16)
    d_skip = ssm_d[0].reshape(1, d)
    wgu = w_ffn_gate_up[0].astype(BF16)
    wd = w_ffn_down[0].astype(BF16)
    wkv = w_kv.astype(BF16)
    wq = w_q[0].reshape(d, kvh, rep, hd).transpose(0, 2, 1, 3).reshape(d, nq).astype(BF16)
    wo = w_o[0].reshape(kvh, rep, hd, d).transpose(1, 0, 2, 3).reshape(nq, d).astype(BF16)
    bias, bias_s = _bias_tables(rel_bias, window, kvh, rep)
    sinks = attn_sinks[0]
    sink_s = sinks.reshape(kvh, rep, 1)

    tm = TOKEN_TILE
    npr = seq * bsz
    assert npr % tm == 0 and ns <= tm

    x1_p, st_re, st_im = _ssm_prompt(x_prompt, g_mix[0:1], lam_re, lam_im, wb_re, wb_im, wc_re, wc_imn,
                                     d_skip, wglu, lc=32)
    x1_s, hs_re, hs_im = _ssm_sample(x_sample.reshape(ns, d), g_mix[0:1], lam_re, lam_im,
                                     state_ssm_re[0].reshape(ns, gp), state_ssm_im[0].reshape(ns, gp),
                                     wb_re, wb_im, wc_re, wc_imn, d_skip, wglu, pad_rows=tm)

    x2, kv, q = _ffn(x1_p.reshape(npr, d), x1_s, g_ffn[0:1], wgu, wd, g_kv.reshape(1, d), wkv, g_mix[1:2], wq)

    ot = _attn_prompt(q, kv, bias, sinks, bsz=bsz, nblk=seq // window, window=window, kvh=kvh, rep=rep, hd=hd)
    o_s, nk_s, nv_s = _attn_sample(q[npr:npr + ns].reshape(ns, rep, kvw), kv[npr:npr + ns].reshape(ns, 1, 2 * kvw),
                                   cache_k_win.reshape(ns, window, kvw), cache_v_win.reshape(ns, window, kvw),
                                   bias_s, sink_s, nb=16, kvh=kvh, rep=rep, hd=hd)
    ot = lax.dynamic_update_slice(ot, jnp.pad(o_s.reshape(ns, nq).T, ((0, 0), (0, tm - ns))), (0, npr))

    y_p, y_s = _moe(x2, ot, wo, g_ffn[1:2], w_router[0], w_exp_gate_up[0], w_exp_down[0], g_final.reshape(1, d),
                    tm=MOE_TILE, fc=w_exp_down.shape[2] // 2, n_valid=npr + ns, sample_rows=tm)

    y_prompt = y_p.reshape(bsz, seq, d)
    y_sample = y_s[:ns].reshape(ns, 1, d)
    kv_tail = jnp.stack([kv[(b + 1) * seq - window:(b + 1) * seq] for b in range(bsz)])
    kv_tail = kv_tail.reshape(bsz, window, 2, kvh, hd).transpose(2, 0, 1, 3, 4)
    return (y_prompt, y_sample,
            st_re.reshape(1, bsz, g, p), st_im.reshape(1, bsz, g, p), kv_tail[0], kv_tail[1],
            hs_re.reshape(1, ns, g, p), hs_im.reshape(1, ns, g, p),
            nk_s.reshape(ns, window, kvh, hd), nv_s.reshape(ns, window, kvh, hd))
```

```python
import functools
import math

import numpy as np
import jax
import jax.numpy as jnp
from jax import lax
from jax.experimental import pallas as pl
from jax.experimental.pallas import tpu as pltpu

F32 = jnp.float32
BF16 = jnp.bfloat16

EPS = 1e-6
NEG_INF = -1e30
TOP_K = 2
MAX_DISTANCE = 128
MXU_DIM = 256
LANES = 128
VMEM_LIMIT_BYTES = 56 * 1024 * 1024


def _dot(a, b):
    return jnp.dot(a, b, preferred_element_type=F32)


def _rms(x, g):
    return x * lax.rsqrt(jnp.mean(x * x, axis=-1, keepdims=True) + EPS) * g


def _const_spec(shape):
    nd = len(shape)
    return pl.BlockSpec(shape, lambda *_: (0,) * nd)


def _params(*sem, vmem=VMEM_LIMIT_BYTES):
    return pltpu.CompilerParams(dimension_semantics=sem, vmem_limit_bytes=vmem)


def _zoh_kernel(a_re_ref, a_im_ref, log_dt_ref, b_re_ref, b_im_ref, ct_re_ref, ct_im_ref,
                lam_re_ref, lam_im_ref, wb_re_ref, wb_im_ref, wc_re_ref, wc_imn_ref, *, hch, p):
    a_re = a_re_ref[...]
    a_im = a_im_ref[...]
    dt = jnp.exp(log_dt_ref[...])
    mag = jnp.exp(a_re * dt)
    lr = mag * jnp.cos(a_im * dt)
    li = mag * jnp.sin(a_im * dt)
    lam_re_ref[...] = lr
    lam_im_ref[...] = li
    nr = lr - 1.0
    den = a_re * a_re + a_im * a_im
    qr = (nr * a_re + li * a_im) / den
    qi = (li * a_re - nr * a_im) / den
    b_re = b_re_ref[...]
    b_im = b_im_ref[...]
    bb_re = qr * b_re - qi * b_im
    bb_im = qr * b_im + qi * b_re

    nkb, ublk, sblk = wb_re_ref.shape
    gpb = ublk // hch
    sh_h, sh_p = int(math.log2(hch)), int(math.log2(p))
    row = lax.broadcasted_iota(jnp.int32, (ublk, sblk), 0)
    col = lax.broadcasted_iota(jnp.int32, (ublk, sblk), 1)
    diag_in = lax.shift_right_logical(row, sh_h) == lax.shift_right_logical(col, sh_p)
    row = lax.broadcasted_iota(jnp.int32, (sblk, ublk), 0)
    col = lax.broadcasted_iota(jnp.int32, (sblk, ublk), 1)
    diag_out = lax.shift_right_logical(row, sh_p) == lax.shift_right_logical(col, sh_h)
    spread = jnp.where(lax.broadcasted_iota(jnp.int32, (hch, ublk), 0)
                       == jnp.bitwise_and(lax.broadcasted_iota(jnp.int32, (hch, ublk), 1), hch - 1),
                       1.0, 0.0).astype(BF16)
    for kb in range(nkb):
        cs = slice(kb * sblk, (kb + 1) * sblk)
        wb_re_ref[kb] = jnp.where(diag_in, jnp.concatenate([bb_re[:, cs]] * gpb, axis=0), 0.0).astype(BF16)
        wb_im_ref[kb] = jnp.where(diag_in, jnp.concatenate([bb_im[:, cs]] * gpb, axis=0), 0.0).astype(BF16)
        c_re = _dot(ct_re_ref[cs, :].astype(BF16), spread)
        c_im = _dot(ct_im_ref[cs, :].astype(BF16), spread)
        wc_re_ref[kb] = jnp.where(diag_out, c_re, 0.0).astype(BF16)
        wc_imn_ref[kb] = jnp.where(diag_out, -c_im, 0.0).astype(BF16)


def _zoh(a_re, a_im, log_dt, b_re, b_im, c_re, c_im):
    g, p = a_re.shape
    h = b_re.shape[-1]
    gp = g * p
    gpb = MXU_DIM // h
    nkb = g // gpb
    assert h & (h - 1) == 0 and p & (p - 1) == 0 and g % gpb == 0
    row = jax.ShapeDtypeStruct((1, gp), F32)
    wb = jax.ShapeDtypeStruct((nkb, gpb * h, gpb * p), BF16)
    wc = jax.ShapeDtypeStruct((nkb, gpb * p, gpb * h), BF16)
    return pl.pallas_call(
        functools.partial(_zoh_kernel, hch=h, p=p),
        out_shape=(row, row, wb, wb, wc, wc),
        name="s5_zoh",
    )(a_re.reshape(1, gp), a_im.reshape(1, gp), jnp.repeat(log_dt, p).reshape(1, gp),
      b_re.transpose(2, 0, 1).reshape(h, gp), b_im.transpose(2, 0, 1).reshape(h, gp),
      c_re.transpose(0, 2, 1).reshape(gp, h), c_im.transpose(0, 2, 1).reshape(gp, h))


def _ssm_in_proj(ub, wb_re_ref, wb_im_ref, bu_re, bu_im):
    nkb, ublk, sblk = wb_re_ref.shape
    for kb in range(nkb):
        ukb = ub[:, kb * ublk:(kb + 1) * ublk]
        bu_re[:, kb * sblk:(kb + 1) * sblk] = _dot(ukb, wb_re_ref[kb])
        bu_im[:, kb * sblk:(kb + 1) * sblk] = _dot(ukb, wb_im_ref[kb])


def _ssm_out_proj(h_re, h_im, wc_re_ref, wc_imn_ref):
    nkb, sblk, _ = wc_re_ref.shape
    ys = []
    for kb in range(nkb):
        hr = h_re[:, kb * sblk:(kb + 1) * sblk].astype(BF16)
        hi = h_im[:, kb * sblk:(kb + 1) * sblk].astype(BF16)
        ys.append(_dot(hr, wc_re_ref[kb]) + _dot(hi, wc_imn_ref[kb]))
    return jnp.concatenate(ys, axis=1)


def _ssm_glu(x, u, y, d_ref, wglu_ref):
    z = jax.nn.gelu(y + d_ref[...] * u).astype(BF16)
    gl = _dot(z, wglu_ref[...])
    d = x.shape[1]
    return x + gl[:, :d] * jax.nn.sigmoid(gl[:, d:])


def _split3(v):
    hi = v.astype(BF16)
    r1 = v - hi.astype(F32)
    mid = r1.astype(BF16)
    lo = (r1 - mid.astype(F32)).astype(BF16)
    return hi, mid, lo


def _ssm_prompt_kernel(x_ref, g_ref, lam_re_ref, lam_im_ref, wb_re_ref, wb_im_ref,
                       wc_re_ref, wc_imn_ref, d_ref, wglu_ref, perm_ref, permt_ref,
                       out_ref, st_re_ref, st_im_ref, bu_re, bu_im, *, lc, bsz, lane_chunk):
    @pl.when(pl.program_id(0) == 0)
    def _():
        st_re_ref[...] = jnp.zeros_like(st_re_ref)
        st_im_ref[...] = jnp.zeros_like(st_im_ref)

    d = x_ref.shape[2]
    x = x_ref[...].reshape(bsz * lc, d)
    u = _rms(x, g_ref[...])
    ub_tb = _dot(perm_ref[...], u.astype(BF16)).astype(BF16)
    _ssm_in_proj(ub_tb, wb_re_ref, wb_im_ref, bu_re, bu_im)

    gp = bu_re.shape[1]
    for c0 in range(0, gp, lane_chunk):
        sl = slice(c0, c0 + lane_chunk)
        lre = jnp.broadcast_to(lam_re_ref[:, sl], (bsz, lane_chunk))
        lim = jnp.broadcast_to(lam_im_ref[:, sl], (bsz, lane_chunk))
        hr = st_re_ref[:, sl]
        hi = st_im_ref[:, sl]
        for t in range(lc):
            rows = slice(t * bsz, (t + 1) * bsz)
            hr, hi = (lre * hr - lim * hi + bu_re[rows, sl], lre * hi + lim * hr + bu_im[rows, sl])
            bu_re[rows, sl] = hr
            bu_im[rows, sl] = hi
        st_re_ref[:, sl] = hr
        st_im_ref[:, sl] = hi

    permt = permt_ref[...]
    hi, mid, lo = _split3(_ssm_out_proj(bu_re, bu_im, wc_re_ref, wc_imn_ref))
    y = (_dot(permt, hi) + _dot(permt, mid)) + _dot(permt, lo)
    out_ref[...] = _ssm_glu(x, u, y, d_ref, wglu_ref).reshape(bsz, lc, d)


def _ssm_sample_kernel(x_ref, g_ref, lam_re_ref, lam_im_ref, h0_re_ref, h0_im_ref, wb_re_ref, wb_im_ref,
                       wc_re_ref, wc_imn_ref, d_ref, wglu_ref,
                       out_ref, h_re_ref, h_im_ref):
    x = x_ref[...]
    u = _rms(x, g_ref[...])
    _ssm_in_proj(u.astype(BF16), wb_re_ref, wb_im_ref, h_re_ref, h_im_ref)
    lre = lam_re_ref[...]
    lim = lam_im_ref[...]
    h0r = h0_re_ref[...]
    h0i = h0_im_ref[...]
    h_re_ref[...] = lre * h0r - lim * h0i + h_re_ref[...]
    h_im_ref[...] = lre * h0i + lim * h0r + h_im_ref[...]
    y = _ssm_out_proj(h_re_ref, h_im_ref, wc_re_ref, wc_imn_ref)
    n = x.shape[0]
    out_ref[0:n, :] = _ssm_glu(x, u, y, d_ref, wglu_ref)
    out_ref[n:, :] = jnp.zeros((out_ref.shape[0] - n, out_ref.shape[1]), F32)


def _ssm_prompt(x, g, lam_re, lam_im, wb_re, wb_im, wc_re, wc_imn, d_skip, wglu, *, lc):
    bsz, seq, d = x.shape
    gp = lam_re.shape[1]
    r = lc * bsz
    perm = np.zeros((r, r), np.float32)
    for b in range(bsz):
        for t in range(lc):
            perm[t * bsz + b, b * lc + t] = 1.0
    permt = jnp.asarray(perm.T, BF16)
    perm = jnp.asarray(perm, BF16)
    kern = functools.partial(_ssm_prompt_kernel, lc=lc, bsz=bsz, lane_chunk=8 * LANES)
    return pl.pallas_call(
        kern,
        grid=(seq // lc,),
        in_specs=[pl.BlockSpec((bsz, lc, d), lambda c: (0, c, 0)),
                  _const_spec(g.shape), _const_spec(lam_re.shape), _const_spec(lam_im.shape),
                  _const_spec(wb_re.shape), _const_spec(wb_im.shape),
                  _const_spec(wc_re.shape), _const_spec(wc_imn.shape),
                  _const_spec(d_skip.shape), _const_spec(wglu.shape),
                  _const_spec(perm.shape), _const_spec(permt.shape)],
        out_specs=[pl.BlockSpec((bsz, lc, d), lambda c: (0, c, 0)),
                   _const_spec((bsz, gp)), _const_spec((bsz, gp))],
        out_shape=[jax.ShapeDtypeStruct((bsz, seq, d), F32),
                   jax.ShapeDtypeStruct((bsz, gp), F32), jax.ShapeDtypeStruct((bsz, gp), F32)],
        scratch_shapes=[pltpu.VMEM((r, gp), F32), pltpu.VMEM((r, gp), F32)],
        compiler_params=_params("arbitrary"),
        name="s5_prompt",
    )(x, g, lam_re, lam_im, wb_re, wb_im, wc_re, wc_imn, d_skip, wglu, perm, permt)


def _ssm_sample(x, g, lam_re, lam_im, h0_re, h0_im, wb_re, wb_im, wc_re, wc_imn, d_skip, wglu, *, pad_rows):
    n, d = x.shape
    gp = lam_re.shape[1]
    return pl.pallas_call(
        _ssm_sample_kernel,
        out_shape=[jax.ShapeDtypeStruct((pad_rows, d), F32),
                   jax.ShapeDtypeStruct((n, gp), F32), jax.ShapeDtypeStruct((n, gp), F32)],
        compiler_params=_params(),
        name="s5_sample",
    )(x, g, lam_re, lam_im, h0_re, h0_im, wb_re, wb_im, wc_re, wc_imn, d_skip, wglu)


def _ffn_chunks(d_ff):
    step = 3 * MXU_DIM
    return [(c, min(c + step, d_ff)) for c in range(0, d_ff, step)]


def _ffn_kernel(xp_ref, xs_ref, gffn_ref, wgu_ref, wd_ref, gkv_ref, wkv_ref, gq_ref, wq_ref,
                x2_ref, kv_ref, q_ref):
    x = jnp.where(pl.program_id(0) < pl.num_programs(0) - 1, xp_ref[...], xs_ref[...])
    hb = _rms(x, gffn_ref[...]).astype(BF16)
    d_ff = wd_ref.shape[0]
    acc = None
    for c0, c1 in _ffn_chunks(d_ff):
        a = _dot(hb, wgu_ref[:, c0:c1])
        b = _dot(hb, wgu_ref[:, d_ff + c0:d_ff + c1])
        part = _dot((jax.nn.silu(a) * b).astype(BF16), wd_ref[c0:c1, :])
        acc = part if acc is None else acc + part
    x2 = x + acc
    x2_ref[...] = x2
    kv_ref[...] = _dot(_rms(x2, gkv_ref[...]).astype(BF16), wkv_ref[...])
    q_ref[...] = _dot(_rms(x2, gq_ref[...]).astype(BF16), wq_ref[...]).astype(BF16)


def _ffn(xp, xs, gffn, wgu, wd, gkv, wkv, gq, wq):
    tm, d = xs.shape
    npt = xp.shape[0] // tm
    rows = xp.shape[0] + tm
    kvw = wkv.shape[1]
    nq = wq.shape[1]
    return pl.pallas_call(
        _ffn_kernel,
        grid=(npt + 1,),
        in_specs=[pl.BlockSpec((tm, d), lambda i: (jnp.minimum(i, npt - 1), 0)),
                  _const_spec(xs.shape),
                  _const_spec(gffn.shape), _const_spec(wgu.shape), _const_spec(wd.shape),
                  _const_spec(gkv.shape), _const_spec(wkv.shape),
                  _const_spec(gq.shape), _const_spec(wq.shape)],
        out_specs=[pl.BlockSpec((tm, d), lambda i: (i, 0)),
                   pl.BlockSpec((tm, kvw), lambda i: (i, 0)),
                   pl.BlockSpec((tm, nq), lambda i: (i, 0))],
        out_shape=[jax.ShapeDtypeStruct((rows, d), F32),
                   jax.ShapeDtypeStruct((rows, kvw), F32),
                   jax.ShapeDtypeStruct((rows, nq), BF16)],
        compiler_params=_params("arbitrary"),
        name="ffn_kv_q",
    )(xp, xs, gffn, wgu, wd, gkv, wkv, gq, wq)


def _t5_bucket(dist, num_buckets):
    max_exact = num_buckets // 2
    d = jnp.maximum(dist, 0)
    large = max_exact + (jnp.log(jnp.maximum(d, 1).astype(F32) / max_exact)
                         / math.log(MAX_DISTANCE / max_exact) * (num_buckets - max_exact)).astype(jnp.int32)
    large = jnp.minimum(large, num_buckets - 1)
    return jnp.where(d < max_exact, d, large)


def _bias_kernel(bm_ref, bs_ref, rb_ref, bias_ref, bias_s_ref, *, kvh, rep):
    nb, nh = rb_ref.shape
    bm = bm_ref[...]
    bs = bs_ref[...]

    def lookup(buckets, h):
        def body(k, acc):
            return jnp.where(buckets == k, rb_ref[k, h], acc)
        return lax.fori_loop(0, nb, body, jnp.zeros(buckets.shape, F32))

    w = bm.shape[0]
    for g in range(kvh):
        for r in range(rep):
            bias_ref[g, :, r * w:(r + 1) * w] = lookup(bm, g * rep + r) * math.log2(math.e)
            bias_s_ref[g, r:r + 1, :] = lookup(bs, g * rep + r)


def _bias_tables(rel_bias, window, kvh, rep):
    nb, nh = rel_bias.shape
    key = jnp.arange(window)[:, None]
    qry = jnp.arange(window)[None, :]
    bm = _t5_bucket((qry - key) % window, nb).astype(jnp.int32)
    bs = _t5_bucket(window - 1 - qry, nb).astype(jnp.int32)
    kern = functools.partial(_bias_kernel, kvh=kvh, rep=rep)
    return pl.pallas_call(
        kern,
        in_specs=[pl.BlockSpec(memory_space=pltpu.VMEM), pl.BlockSpec(memory_space=pltpu.VMEM),
                  pl.BlockSpec(memory_space=pltpu.SMEM)],
        out_shape=[jax.ShapeDtypeStruct((kvh, window, rep * window), F32),
                   jax.ShapeDtypeStruct((kvh, rep, window), F32)],
        name="t5_bias",
    )(bm, bs, rel_bias)


def _attn_prompt_kernel(q_ref, kvc_ref, kvp_ref, bias_ref, sink_ref, ot_ref, *, kvh, rep, hd, scale):
    w = kvp_ref.shape[0]
    kv_all = kvc_ref[...]
    for sub in range(q_ref.shape[0] // w):
        rows = slice(sub * w, (sub + 1) * w)
        kvp = kvp_ref[...] if sub == 0 else kv_all[(sub - 1) * w:sub * w]
        first = pl.program_id(1) == 0 if sub == 0 else None
        _attn_prompt_block(q_ref[rows, :], kv_all[rows], kvp, first, bias_ref, sink_ref, ot_ref, sub * w,
                           kvh=kvh, rep=rep, hd=hd, scale=scale)


def _attn_prompt_block(q, kvc, kvp, first, bias_ref, sink_ref, ot_ref, col0, *, kvh, rep, hd, scale):
    w = q.shape[0]
    kvw = kvh * hd
    lane = lax.broadcasted_iota(jnp.int32, (1, LANES), 1)
    key = lax.broadcasted_iota(jnp.int32, (w, rep * w), 0)
    qry = lax.broadcasted_iota(jnp.int32, (w, rep * w), 1) % w
    upper = key > qry
    mask_add = None if first is None else jnp.where(jnp.logical_and(upper, first), NEG_INF, 0.0)
    log2e = math.log2(math.e)
    heads_per_blk = LANES // hd
    for p in range(kvw // LANES):
        cs = slice(p * LANES, (p + 1) * LANES)
        kc, kp = kvc[:, cs], kvp[:, cs]
        vs = slice(kvw + p * LANES, kvw + (p + 1) * LANES)
        vb = jnp.concatenate([kvp[:, vs], kvc[:, vs]], axis=0).astype(BF16)
        qg = jnp.concatenate([q[:, r * kvw + p * LANES:r * kvw + (p + 1) * LANES] for r in range(rep)], axis=0)
        halves = []
        for half in range(heads_per_blk):
            g = p * heads_per_blk + half
            lmask = jnp.logical_and(lane >= half * hd, lane < (half + 1) * hd)
            kb = jnp.concatenate([jnp.where(lmask, kp, 0.0), jnp.where(lmask, kc, 0.0)], axis=0).astype(BF16)
            s = lax.dot_general(kb, qg, (((1,), (1,)), ((), ())), preferred_element_type=F32)
            bias = bias_ref[g] if first is None else bias_ref[g] + mask_add
            sc = jnp.where(upper, s[:w], s[w:]) * (scale * log2e) + bias
            sink = jnp.concatenate([jnp.full((1, w), sink_ref[g * rep + r] * log2e, F32) for r in range(rep)],
                                   axis=1)
            m = jnp.maximum(jnp.max(sc, axis=0, keepdims=True), sink)
            pe = jnp.exp2(sc - m)
            denom = jnp.sum(pe, axis=0, keepdims=True) + jnp.exp2(sink - m)
            pcat = jnp.concatenate([jnp.where(upper, pe, 0.0), jnp.where(upper, 0.0, pe)], axis=0).astype(BF16)
            og = lax.dot_general(vb, pcat, (((0,), (0,)), ((), ())), preferred_element_type=F32)
            halves.append((og / denom)[half * hd:(half + 1) * hd])
        o_blk = jnp.concatenate(halves, axis=0).astype(BF16)
        for r in range(rep):
            ot_ref[r * kvw + p * LANES:r * kvw + (p + 1) * LANES, col0:col0 + w] = o_blk[:, r * w:(r + 1) * w]


def _attn_prompt(q, kv, bias, sinks, *, bsz, nblk, window, kvh, rep, hd, blocks_per_step=2):
    rows, nq = q.shape
    kvw2 = kv.shape[1]
    nsub = blocks_per_step
    nstep = nblk // nsub
    kern = functools.partial(_attn_prompt_kernel, kvh=kvh, rep=rep, hd=hd, scale=1.0 / math.sqrt(hd))
    return pl.pallas_call(
        kern,
        grid=(bsz, nstep),
        in_specs=[pl.BlockSpec((nsub * window, nq), lambda b, i: (b * nstep + i, 0)),
                  pl.BlockSpec((nsub * window, kvw2), lambda b, i: (b * nstep + i, 0)),
                  pl.BlockSpec((window, kvw2), lambda b, i: (b * nblk + jnp.maximum(nsub * i - 1, 0), 0)),
                  _const_spec(bias.shape),
                  pl.BlockSpec(memory_space=pltpu.SMEM)],
        out_specs=pl.BlockSpec((nq, nsub * window), lambda b, i: (0, b * nstep + i)),
        out_shape=jax.ShapeDtypeStruct((nq, rows), BF16),
        compiler_params=_params("arbitrary", "arbitrary"),
        name="swa_prompt",
    )(q, kv, kv, bias, sinks)


def _attn_sample_kernel(q_ref, kv_ref, ck_ref, cv_ref, bias_ref, sink_ref, o_ref, nk_ref, nv_ref,
                        *, kvh, rep, hd, scale):
    nb, w, kvw = ck_ref.shape
    kv = kv_ref[...]
    lane = lax.broadcasted_iota(jnp.int32, (nb, rep, kvw), 2)

    def shifted(c_ref, n_ref, new):
        flat = c_ref[...].reshape(nb * w, kvw)
        n_ref[...] = pltpu.roll(flat, nb * w - 1, axis=0).reshape(nb, w, kvw)
        n_ref[:, w - 1:w, :] = new
        return n_ref[...].astype(BF16)

    nkb = shifted(ck_ref, nk_ref, kv[:, :, :kvw])
    nvb = shifted(cv_ref, nv_ref, kv[:, :, kvw:])
    q = q_ref[...].astype(F32)
    o = jnp.zeros((nb, rep, kvw), F32)
    for g in range(kvh):
        lmask = jnp.logical_and(lane >= g * hd, lane < (g + 1) * hd)
        qg = jnp.where(lmask, q, 0.0).astype(BF16)
        s = jnp.einsum("nrc,njc->nrj", qg, nkb, preferred_element_type=F32)
        sc = s * scale + bias_ref[g][None]
        sink = sink_ref[g][None]
        m = jnp.maximum(jnp.max(sc, axis=-1, keepdims=True), sink)
        pe = jnp.exp(sc - m)
        probs = pe / (jnp.sum(pe, axis=-1, keepdims=True) + jnp.exp(sink - m))
        og = jnp.einsum("nrj,njc->nrc", probs.astype(BF16), nvb, preferred_element_type=F32)
        o = jnp.where(lmask, og, o)
    o_ref[...] = o.astype(BF16)


def _attn_sample(q3, kv3, ck, cv, bias_s, sink_s, *, nb, kvh, rep, hd):
    n, w, kvw = ck.shape
    kern = functools.partial(_attn_sample_kernel, kvh=kvh, rep=rep, hd=hd, scale=1.0 / math.sqrt(hd))
    cache_spec = pl.BlockSpec((nb, w, kvw), lambda i: (i, 0, 0))
    return pl.pallas_call(
        kern,
        grid=(n // nb,),
        in_specs=[pl.BlockSpec((nb, rep, kvw), lambda i: (i, 0, 0)),
                  pl.BlockSpec((nb, 1, 2 * kvw), lambda i: (i, 0, 0)),
                  cache_spec, cache_spec,
                  _const_spec(bias_s.shape), _const_spec(sink_s.shape)],
        out_specs=[pl.BlockSpec((nb, rep, kvw), lambda i: (i, 0, 0)), cache_spec, cache_spec],
        out_shape=[jax.ShapeDtypeStruct((n, rep, kvw), BF16),
                   jax.ShapeDtypeStruct((n, w, kvw), F32), jax.ShapeDtypeStruct((n, w, kvw), F32)],
        compiler_params=_params("arbitrary"),
        name="swa_sample",
    )(q3, kv3, ck, cv, bias_s, sink_s)


CHUNK = 16
SUB = 256
UP_ROW_BLOCK = 1024
DOWN_ROW_BLOCK = 1024
TOKEN_TILE = 512
MOE_TILE = 512


def _chunk_copy(src, src_row, dst, dst_row, sem):
    return pltpu.make_async_copy(src.at[pl.ds(pl.multiple_of(src_row, CHUNK), CHUNK), :],
                                 dst.at[pl.ds(pl.multiple_of(dst_row, CHUNK), CHUNK), :], sem)


def _moe_route_kernel(x2_ref, ot_ref, wo_ref, gffn_ref, wr_ref, lst_ref, ust_ref,
                      x3_ref, cm_ref, tmeta_ref, xs_hbm,
                      comp_s, carry_s, zero_s, base_v, pend_sm, sem, *, ne, cap, n_valid):
    i = pl.program_id(0)
    nt = pl.num_programs(0)
    tm = x2_ref.shape[0]
    w = comp_s.shape[1]
    slot = lax.rem(i, 2)

    @pl.when(i == 0)
    def _():
        base_v[...] = jnp.zeros_like(base_v)
        zero_s[...] = jnp.zeros_like(zero_s)
        carry_s[...] = jnp.zeros_like(carry_s)

    def drain(n):
        def body(c, carry):
            _chunk_copy(zero_s, 0, xs_hbm, 0, sem).wait()
            return carry
        lax.fori_loop(0, n, body, 0)

    @pl.when(i > 0)
    def _():
        drain(pend_sm[0])

    x3 = x2_ref[...] + lax.dot_general(ot_ref[...], wo_ref[...], (((0,), (0,)), ((), ())),
                                       preferred_element_type=F32)
    x3_ref[...] = x3
    hb = _rms(x3, gffn_ref[...]).astype(BF16)

    lane = lax.broadcasted_iota(jnp.int32, (tm, LANES), 1)
    logits = jnp.where(lane < ne, _dot(hb, wr_ref[...]), -jnp.inf)
    m1 = jnp.max(logits, axis=-1, keepdims=True)
    i1 = jnp.min(jnp.where(logits == m1, lane, LANES), axis=-1, keepdims=True)
    rest = jnp.where(lane == i1, -jnp.inf, logits)
    m2 = jnp.max(rest, axis=-1, keepdims=True)
    i2 = jnp.min(jnp.where(rest == m2, lane, LANES), axis=-1, keepdims=True)
    e2 = jnp.exp(m2 - m1)
    g1 = 1.0 / (1.0 + e2)
    g2 = e2 / (1.0 + e2)

    live = i * tm + lax.broadcasted_iota(jnp.int32, (tm, 1), 0) < n_valid
    sel = jnp.where(jnp.logical_and(live, jnp.logical_or(lane == i1, lane == i2)), 1.0, 0.0)
    rank = _dot(lst_ref[...], sel.astype(BF16))
    cnt = jnp.sum(sel, axis=0, keepdims=True)
    fill = base_v[...]
    rem = fill - jnp.floor(fill / CHUNK) * CHUNK
    cpad = jnp.floor((rem + cnt + (CHUNK - 1)) / CHUNK) * CHUNK
    loff = _dot(jnp.broadcast_to(cpad, (8, LANES)).astype(BF16), ust_ref[...])[0:1]
    dest = loff + rem + rank
    ld1 = jnp.where(live, jnp.sum(jnp.where(lane == i1, dest, 0.0), axis=-1, keepdims=True), -1.0)
    ld2 = jnp.where(live, jnp.sum(jnp.where(lane == i2, dest, 0.0), axis=-1, keepdims=True), -1.0)
    cm = jnp.where(lane == 0, ld1, jnp.where(lane == 1, ld2, jnp.where(lane == 2, g1, jnp.where(lane == 3, g2, 0.0))))
    cm_ref[...] = cm

    rm = cm.T
    rowi = lax.broadcasted_iota(jnp.int32, (w, tm), 0).astype(F32)
    place = (jnp.where(rowi == rm[0:1], 1.0, 0.0) + jnp.where(rowi == rm[1:2], 1.0, 0.0)).astype(BF16)
    comp_s[slot] = _dot(place, hb).astype(BF16)

    base = fill - rem
    srow = lax.broadcasted_iota(jnp.int32, (8, LANES), 0)
    tmeta_ref[...] = jnp.where(srow == 0, cpad, jnp.where(srow == 1, base, jnp.where(srow == 2, fill + cnt, 0.0))
                               ).astype(jnp.int32)
    base_v[...] = fill + cnt

    cpad_i = cpad.astype(jnp.int32)
    loff_i = loff.astype(jnp.int32)
    base_i = base.astype(jnp.int32)
    tail_i = (rem + cnt - jnp.floor((rem + cnt) / CHUNK) * CHUNK).astype(jnp.int32)
    total = 0
    tails = []
    for e in range(ne):
        n_e = cpad_i[0, e]
        src0 = loff_i[0, e]
        dst0 = base_i[0, e] + e * cap

        @pl.when(n_e > 0)
        def _(e=e, n_e=n_e, src0=src0, partial=tail_i[0, e] > 0):
            head = pl.ds(pl.multiple_of(src0, CHUNK), CHUNK)
            comp_s[slot, head, :] = comp_s[slot, head, :] + carry_s[e]
            last = comp_s[slot, pl.ds(pl.multiple_of(src0 + n_e - CHUNK, CHUNK), CHUNK), :]
            carry_s[e] = jnp.where(partial, last, jnp.zeros_like(last))

        def start(c, carry, src0=src0, dst0=dst0):
            _chunk_copy(comp_s.at[slot], src0 + c * CHUNK, xs_hbm, dst0 + c * CHUNK, sem).start()
            return carry

        nchunks = lax.shift_right_logical(n_e, int(math.log2(CHUNK)))
        lax.fori_loop(0, nchunks, start, 0)
        total = total + nchunks
        tails.append(dst0 + n_e)
    pend_sm[0] = total

    @pl.when(i == nt - 1)
    def _():
        nz_total = 0
        for e in range(ne):
            end = tails[e]
            nz = lax.shift_right_logical(lax.rem(SUB - lax.rem(end, SUB), SUB), int(math.log2(CHUNK)))

            def zstart(c, carry, end=end):
                _chunk_copy(zero_s, 0, xs_hbm, end + c * CHUNK, sem).start()
                return carry

            lax.fori_loop(0, nz, zstart, 0)
            nz_total = nz_total + nz
        drain(total + nz_total)


def _moe_route(x2, ot, wo, gffn, wr_pad, *, tm, ne, cap, n_valid):
    rows, d = x2.shape
    nt = rows // tm
    w = _staging_rows(tm, ne)
    lst = jnp.asarray(np.tril(np.ones((tm, tm), np.float32), -1), BF16)
    ust = jnp.asarray(np.triu(np.ones((LANES, LANES), np.float32), 1), BF16)
    kern = functools.partial(_moe_route_kernel, ne=ne, cap=cap, n_valid=n_valid)
    return pl.pallas_call(
        kern,
        grid=(nt,),
        in_specs=[pl.BlockSpec((tm, d), lambda i: (i, 0)),
                  pl.BlockSpec((ot.shape[0], tm), lambda i: (0, i)),
                  _const_spec(wo.shape), _const_spec(gffn.shape), _const_spec(wr_pad.shape),
                  _const_spec(lst.shape), _const_spec(ust.shape)],
        out_specs=[pl.BlockSpec((tm, d), lambda i: (i, 0)),
                   pl.BlockSpec((tm, LANES), lambda i: (i, 0)),
                   pl.BlockSpec((8, LANES), lambda i: (i, 0)),
                   pl.BlockSpec(memory_space=pl.ANY)],
        out_shape=[jax.ShapeDtypeStruct((rows, d), F32),
                   jax.ShapeDtypeStruct((rows, LANES), F32),
                   jax.ShapeDtypeStruct((nt * 8, LANES), jnp.int32),
                   jax.ShapeDtypeStruct((ne * cap, d), BF16)],
        scratch_shapes=[pltpu.VMEM((2, w, d), BF16), pltpu.VMEM((ne, CHUNK, d), BF16), pltpu.VMEM((CHUNK, d), BF16),
                        pltpu.VMEM((1, LANES), F32), pltpu.SMEM((1,), jnp.int32),
                        pltpu.SemaphoreType.DMA(())],
        compiler_params=_params("arbitrary"),
        name="moe_route",
    )(x2, ot, wo, gffn, wr_pad, lst, ust)


def _staging_rows(tm, ne):
    return -(-(TOP_K * tm + 2 * ne * (CHUNK - 1)) // LANES) * LANES


def _max_blocks(rows, ne, row_block):
    return TOP_K * rows // row_block + ne


def _expert_steps(nrows, nf, max_blocks, row_block):
    ne = nrows.shape[0]
    nblk = (nrows + row_block - 1) // row_block
    cum = jnp.cumsum(nblk) * nf
    total = cum[-1]
    s = jnp.minimum(jnp.arange(max_blocks * nf, dtype=jnp.int32), total - 1)
    e = jnp.minimum(jnp.sum((s[:, None] >= cum[None, :]).astype(jnp.int32), axis=1), ne - 1)
    nb_e = nblk[e]
    within = s - (cum[e] - nb_e * nf)
    f = within // nb_e
    pos = within - f * nb_e
    r = (pos + nb_e - 1) % nb_e
    nsub = jnp.clip((nrows[e] - r * row_block + SUB - 1) // SUB, 0, row_block // SUB)
    first = (pos == 0).astype(jnp.int32)
    return (e, f.astype(jnp.int32), r.astype(jnp.int32), first, nsub.astype(jnp.int32),
            jnp.reshape(total, (1,)).astype(jnp.int32))


def _moe_up_kernel(e_ref, f_ref, r_ref, first_ref, nsub_ref, n_ref, xs_ref, wg_ref, wu_ref, act_ref, wgb_s, wub_s):
    s = pl.program_id(0)

    @pl.when(s < n_ref[0])
    def _():
        @pl.when(first_ref[s] == 1)
        def _():
            wgb_s[...] = wg_ref[...].astype(BF16)
            wub_s[...] = wu_ref[...].astype(BF16)

        def body(j, carry):
            rows = pl.ds(pl.multiple_of(j * SUB, SUB), SUB)
            xsb = xs_ref[rows, :]
            a = _dot(xsb, wgb_s[...])
            b = _dot(xsb, wub_s[...])
            act_ref[rows, :] = (jax.nn.silu(a) * b).astype(BF16)
            return carry

        lax.fori_loop(0, nsub_ref[s], body, 0)


def _moe_up(steps, xs, wgu, *, fc, cap, row_block):
    ne, d, dff2 = wgu.shape
    dff = dff2 // 2
    nf = dff // fc
    cb = cap // row_block
    grid_spec = pltpu.PrefetchScalarGridSpec(
        num_scalar_prefetch=6,
        grid=(steps[0].shape[0],),
        in_specs=[pl.BlockSpec((row_block, d), lambda s, e, f, r, *_: (e[s] * cb + r[s], 0)),
                  pl.BlockSpec((None, d, fc), lambda s, e, f, r, *_: (e[s], 0, f[s])),
                  pl.BlockSpec((None, d, fc), lambda s, e, f, r, *_: (e[s], 0, nf + f[s]))],
        out_specs=pl.BlockSpec((row_block, fc), lambda s, e, f, r, *_: (e[s] * cb + r[s], f[s])),
        scratch_shapes=[pltpu.VMEM((d, fc), BF16), pltpu.VMEM((d, fc), BF16)])
    return pl.pallas_call(
        _moe_up_kernel,
        grid_spec=grid_spec,
        out_shape=jax.ShapeDtypeStruct((ne * cap, dff), BF16),
        compiler_params=_params("arbitrary"),
        name="moe_up",
    )(*steps, xs, wgu, wgu)


def _moe_down_kernel(e_ref, f_ref, r_ref, first_ref, nsub_ref, n_ref, act_ref, wd_ref, ys_ref, wdb_s):
    s = pl.program_id(0)

    @pl.when(s < n_ref[0])
    def _():
        @pl.when(first_ref[s] == 1)
        def _():
            wdb_s[...] = wd_ref[...].astype(BF16)

        def body(j, carry):
            rows = pl.ds(pl.multiple_of(j * SUB, SUB), SUB)
            ys_ref[rows, :] = _dot(act_ref[rows, :], wdb_s[...]).astype(BF16)
            return carry

        lax.fori_loop(0, nsub_ref[s], body, 0)


def _moe_down(steps, act, wd, *, cap, row_block):
    ne, dff, d = wd.shape
    cb = cap // row_block
    grid_spec = pltpu.PrefetchScalarGridSpec(
        num_scalar_prefetch=6,
        grid=(steps[0].shape[0],),
        in_specs=[pl.BlockSpec((row_block, dff), lambda s, e, f, r, *_: (e[s] * cb + r[s], 0)),
                  pl.BlockSpec((None, dff, d), lambda s, e, f, r, *_: (e[s], 0, 0))],
        out_specs=pl.BlockSpec((row_block, d), lambda s, e, f, r, *_: (e[s] * cb + r[s], 0)),
        scratch_shapes=[pltpu.VMEM((dff, d), BF16)])
    return pl.pallas_call(
        _moe_down_kernel,
        grid_spec=grid_spec,
        out_shape=jax.ShapeDtypeStruct((ne * cap, d), BF16),
        compiler_params=_params("arbitrary", vmem=2 * (dff * d * 4 + row_block * (dff + d) * 2) + dff * d * 2
                                + 4 * SUB * d * 4),
        name="moe_down",
    )(*steps, act, wd)


def _moe_combine_kernel(cpad_ref, seg_ref, x3_ref, cm_ref, gfin_ref, ys_hbm, yp_ref, ys_ref, yloc_s, sem,
                        *, ne, npt):
    i = pl.program_id(0)
    nt = pl.num_programs(0)
    tm = x3_ref.shape[0]
    w = yloc_s.shape[1]
    slot = lax.rem(i, 2)
    shift = int(math.log2(CHUNK))

    def issue(t, sl):
        off = 0
        for e in range(ne):
            n_e = cpad_ref[t * ne + e]
            src0 = seg_ref[t * ne + e]

            def start(c, carry, src0=src0, off=off):
                _chunk_copy(ys_hbm, src0 + c * CHUNK, yloc_s.at[sl], off + c * CHUNK, sem.at[sl]).start()
                return carry

            lax.fori_loop(0, lax.shift_right_logical(n_e, shift), start, 0)
            off = off + n_e

    @pl.when(i == 0)
    def _():
        yloc_s[...] = jnp.zeros_like(yloc_s)
        issue(0, 0)

    @pl.when(i + 1 < nt)
    def _():
        issue(i + 1, 1 - slot)

    total = 0
    for e in range(ne):
        total = total + lax.shift_right_logical(cpad_ref[i * ne + e], shift)

    def wait(c, carry):
        _chunk_copy(ys_hbm, 0, yloc_s.at[slot], 0, sem.at[slot]).wait()
        return carry

    lax.fori_loop(0, total, wait, 0)

    cm = cm_ref[...]
    col = lax.broadcasted_iota(jnp.int32, (tm, w), 1).astype(F32)
    yl = yloc_s[slot]
    y1 = _dot(jnp.where(col == cm[:, 0:1], 1.0, 0.0).astype(BF16), yl)
    y2 = _dot(jnp.where(col == cm[:, 1:2], 1.0, 0.0).astype(BF16), yl)
    y = _rms(x3_ref[...] + cm[:, 2:3] * y1 + cm[:, 3:4] * y2, gfin_ref[...])

    @pl.when(i < npt)
    def _():
        yp_ref[...] = y

    @pl.when(i >= npt)
    def _():
        ys_ref[...] = y


def _moe_combine(cpad, seg, x3, cm, gfin, ys, *, tm, ne, sample_rows):
    rows, d = x3.shape
    nt = rows // tm
    npt = (rows - sample_rows) // tm
    w = _staging_rows(tm, ne)
    kern = functools.partial(_moe_combine_kernel, ne=ne, npt=npt)
    grid_spec = pltpu.PrefetchScalarGridSpec(
        num_scalar_prefetch=2,
        grid=(nt,),
        in_specs=[pl.BlockSpec((tm, d), lambda i, *_: (i, 0)),
                  pl.BlockSpec((tm, LANES), lambda i, *_: (i, 0)),
                  pl.BlockSpec(gfin.shape, lambda i, *_: (0, 0)),
                  pl.BlockSpec(memory_space=pl.ANY)],
        out_specs=[pl.BlockSpec((tm, d), lambda i, *_: (jnp.minimum(i, npt - 1), 0)),
                   pl.BlockSpec((tm, d), lambda i, *_: (jnp.maximum(i - npt, 0), 0))],
        scratch_shapes=[pltpu.VMEM((2, w, d), BF16), pltpu.SemaphoreType.DMA((2,))])
    return pl.pallas_call(
        kern,
        grid_spec=grid_spec,
        out_shape=[jax.ShapeDtypeStruct((npt * tm, d), F32), jax.ShapeDtypeStruct((sample_rows, d), F32)],
        compiler_params=_params("arbitrary"),
        name="moe_combine",
    )(cpad, seg, x3, cm, gfin, ys)


def _moe(x2, ot, wo, gffn, wr, wgu, wd, gfin, *, tm, fc, n_valid, sample_rows):
    rows, d = x2.shape
    ne = wr.shape[1]
    nt = rows // tm
    nf = wgu.shape[2] // 2 // fc
    cap = -(-(rows + SUB) // UP_ROW_BLOCK) * UP_ROW_BLOCK
    wr_pad = jnp.pad(wr, ((0, 0), (0, LANES - ne))).astype(BF16)
    x3, cm, tmeta, xs = _moe_route(x2, ot, wo, gffn, wr_pad, tm=tm, ne=ne, cap=cap, n_valid=n_valid)
    tmeta = tmeta.reshape(nt, 8, LANES)[:, :, :ne]
    cpad = tmeta[:, 0, :].reshape(nt * ne)
    seg = (tmeta[:, 1, :] + jnp.arange(ne, dtype=jnp.int32)[None, :] * cap).reshape(nt * ne)
    nrows = tmeta[nt - 1, 2, :]
    up_steps = _expert_steps(nrows, nf, _max_blocks(rows, ne, UP_ROW_BLOCK), UP_ROW_BLOCK)
    act = _moe_up(up_steps, xs, wgu, fc=fc, cap=cap, row_block=UP_ROW_BLOCK)
    down_steps = _expert_steps(nrows, 1, _max_blocks(rows, ne, DOWN_ROW_BLOCK), DOWN_ROW_BLOCK)
    ys = _moe_down(down_steps, act, wd, cap=cap, row_block=DOWN_ROW_BLOCK)
    return _moe_combine(cpad, seg, x3, cm, gfin, ys, tm=tm, ne=ne, sample_rows=sample_rows)


def kernel(x_prompt, x_sample, state_ssm_re, state_ssm_im, cache_k_win, cache_v_win, g_mix, g_ffn, g_kv, g_final, ssm_a_re, ssm_a_im, ssm_log_dt, ssm_b_re, ssm_b_im, ssm_c_re, ssm_c_im, ssm_d, w_glu, w_kv, w_q, w_o, attn_sinks, rel_bias, w_ffn_gate_up, w_ffn_down, w_router, w_exp_gate_up, w_exp_down):
    bsz, seq, d = x_prompt.shape
    ns, dec_seq, _ = x_sample.shape
    assert dec_seq == 1 and g_mix.shape[0] == 2 and ssm_a_re.shape[0] == 1 and w_q.shape[0] == 1
    _, g, p = ssm_a_re.shape
    gp = g * p
    window, kvh, hd = cache_k_win.shape[1:]
    kvw = kvh * hd
    nh = attn_sinks.shape[1]
    rep = nh // kvh
    nq = nh * hd
    assert bsz == 8 and ns % bsz == 0 and seq % window == 0 and LANES % hd == 0

    lam_re, lam_im, wb_re, wb_im, wc_re, wc_imn = _zoh(ssm_a_re[0], ssm_a_im[0], ssm_log_dt[0],
                                                        ssm_b_re[0], ssm_b_im[0], ssm_c_re[0], ssm_c_im[0])
    wglu = w_glu[0].astype(BF16)
    d_skip = ssm_d[0].reshape(1, d)
    wgu = w_ffn_gate_up[0].astype(BF16)
    wd = w_ffn_down[0].astype(BF16)
    wkv = w_kv.astype(BF16)
    wq = w_q[0].reshape(d, kvh, rep, hd).transpose(0, 2, 1, 3).reshape(d, nq).astype(BF16)
    wo = w_o[0].reshape(kvh, rep, hd, d).transpose(1, 0, 2, 3).reshape(nq, d).astype(BF16)
    bias, bias_s = _bias_tables(rel_bias, window, kvh, rep)
    sinks = attn_sinks[0]
    sink_s = sinks.reshape(kvh, rep, 1)

    tm = TOKEN_TILE
    npr = seq * bsz
    assert npr % tm == 0 and ns <= tm

    x1_p, st_re, st_im = _ssm_prompt(x_prompt, g_mix[0:1], lam_re, lam_im, wb_re, wb_im, wc_re, wc_imn,
                                     d_skip, wglu, lc=32)
    x1_s, hs_re, hs_im = _ssm_sample(x_sample.reshape(ns, d), g_mix[0:1], lam_re, lam_im,
                                     state_ssm_re[0].reshape(ns, gp), state_ssm_im[0].reshape(ns, gp),
                                     wb_re, wb_im, wc_re, wc_imn, d_skip, wglu, pad_rows=tm)

    x2, kv, q = _ffn(x1_p.reshape(npr, d), x1_s, g_ffn[0:1], wgu, wd, g_kv.reshape(1, d), wkv, g_mix[1:2], wq)

    ot = _attn_prompt(q, kv, bias, sinks, bsz=bsz, nblk=seq // window, window=window, kvh=kvh, rep=rep, hd=hd)
    o_s, nk_s, nv_s = _attn_sample(q[npr:npr + ns].reshape(ns, rep, kvw), kv[npr:npr + ns].reshape(ns, 1, 2 * kvw),
                                   cache_k_win.reshape(ns, window, kvw), cache_v_win.reshape(ns, window, kvw),
                                   bias_s, sink_s, nb=16, kvh=kvh, rep=rep, hd=hd)
    ot = lax.dynamic_update_slice(ot, jnp.pad(o_s.reshape(ns, nq).T, ((0, 0), (0, tm - ns))), (0, npr))

    y_p, y_s = _moe(x2, ot, wo, g_ffn[1:2], w_router[0], w_exp_gate_up[0], w_exp_down[0], g_final.reshape(1, d),
                    tm=MOE_TILE, fc=w_exp_down.shape[2] // 2, n_valid=npr + ns, sample_rows=tm)

    y_prompt = y_p.reshape(bsz, seq, d)
    y_sample = y_s[:ns].reshape(ns, 1, d)
    kv_tail = jnp.stack([kv[(b + 1) * seq - window:(b + 1) * seq] for b in range(bsz)])
    kv_tail = kv_tail.reshape(bsz, window, 2, kvh, hd).transpose(2, 0, 1, 3, 4)
    return (y_prompt, y_sample,
            st_re.reshape(1, bsz, g, p), st_im.reshape(1, bsz, g, p), kv_tail[0], kv_tail[1],
            hs_re.reshape(1, ns, g, p), hs_im.reshape(1, ns, g, p),
            nk_s.reshape(ns, window, kvh, hd), nv_s.reshape(ns, window, kvh, hd))
```

```python
import functools
import math

import numpy as np
import jax
import jax.numpy as jnp
from jax import lax
from jax.experimental import pallas as pl
from jax.experimental.pallas import tpu as pltpu

F32 = jnp.float32
BF16 = jnp.bfloat16

EPS = 1e-6
NEG_INF = -1e30
TOP_K = 2
MAX_DISTANCE = 128
MXU_DIM = 256
LANES = 128
VMEM_LIMIT_BYTES = 56 * 1024 * 1024


def _dot(a, b):
    return jnp.dot(a, b, preferred_element_type=F32)


def _rms(x, g):
    return x * lax.rsqrt(jnp.mean(x * x, axis=-1, keepdims=True) + EPS) * g


def _const_spec(shape):
    nd = len(shape)
    return pl.BlockSpec(shape, lambda *_: (0,) * nd)


def _params(*sem, vmem=VMEM_LIMIT_BYTES):
    return pltpu.CompilerParams(dimension_semantics=sem, vmem_limit_bytes=vmem)


def _zoh_kernel(a_re_ref, a_im_ref, log_dt_ref, b_re_ref, b_im_ref, ct_re_ref, ct_im_ref,
                lam_re_ref, lam_im_ref, wb_re_ref, wb_im_ref, wc_re_ref, wc_imn_ref, *, hch, p):
    a_re = a_re_ref[...]
    a_im = a_im_ref[...]
    dt = jnp.exp(log_dt_ref[...])
    mag = jnp.exp(a_re * dt)
    lr = mag * jnp.cos(a_im * dt)
    li = mag * jnp.sin(a_im * dt)
    lam_re_ref[...] = lr
    lam_im_ref[...] = li
    nr = lr - 1.0
    den = a_re * a_re + a_im * a_im
    qr = (nr * a_re + li * a_im) / den
    qi = (li * a_re - nr * a_im) / den
    b_re = b_re_ref[...]
    b_im = b_im_ref[...]
    bb_re = qr * b_re - qi * b_im
    bb_im = qr * b_im + qi * b_re

    nkb, ublk, sblk = wb_re_ref.shape
    gpb = ublk // hch
    sh_h, sh_p = int(math.log2(hch)), int(math.log2(p))
    row = lax.broadcasted_iota(jnp.int32, (ublk, sblk), 0)
    col = lax.broadcasted_iota(jnp.int32, (ublk, sblk), 1)
    diag_in = lax.shift_right_logical(row, sh_h) == lax.shift_right_logical(col, sh_p)
    row = lax.broadcasted_iota(jnp.int32, (sblk, ublk), 0)
    col = lax.broadcasted_iota(jnp.int32, (sblk, ublk), 1)
    diag_out = lax.shift_right_logical(row, sh_p) == lax.shift_right_logical(col, sh_h)
    spread = jnp.where(lax.broadcasted_iota(jnp.int32, (hch, ublk), 0)
                       == jnp.bitwise_and(lax.broadcasted_iota(jnp.int32, (hch, ublk), 1), hch - 1),
                       1.0, 0.0).astype(BF16)
    for kb in range(nkb):
        cs = slice(kb * sblk, (kb + 1) * sblk)
        wb_re_ref[kb] = jnp.where(diag_in, jnp.concatenate([bb_re[:, cs]] * gpb, axis=0), 0.0).astype(BF16)
        wb_im_ref[kb] = jnp.where(diag_in, jnp.concatenate([bb_im[:, cs]] * gpb, axis=0), 0.0).astype(BF16)
        c_re = _dot(ct_re_ref[cs, :].astype(BF16), spread)
        c_im = _dot(ct_im_ref[cs, :].astype(BF16), spread)
        wc_re_ref[kb] = jnp.where(diag_out, c_re, 0.0).astype(BF16)
        wc_imn_ref[kb] = jnp.where(diag_out, -c_im, 0.0).astype(BF16)


def _zoh(a_re, a_im, log_dt, b_re, b_im, c_re, c_im):
    g, p = a_re.shape
    h = b_re.shape[-1]
    gp = g * p
    gpb = MXU_DIM // h
    nkb = g // gpb
    assert h & (h - 1) == 0 and p & (p - 1) == 0 and g % gpb == 0
    row = jax.ShapeDtypeStruct((1, gp), F32)
    wb = jax.ShapeDtypeStruct((nkb, gpb * h, gpb * p), BF16)
    wc = jax.ShapeDtypeStruct((nkb, gpb * p, gpb * h), BF16)
    return pl.pallas_call(
        functools.partial(_zoh_kernel, hch=h, p=p),
        out_shape=(row, row, wb, wb, wc, wc),
        name="s5_zoh",
    )(a_re.reshape(1, gp), a_im.reshape(1, gp), jnp.repeat(log_dt, p).reshape(1, gp),
      b_re.transpose(2, 0, 1).reshape(h, gp), b_im.transpose(2, 0, 1).reshape(h, gp),
      c_re.transpose(0, 2, 1).reshape(gp, h), c_im.transpose(0, 2, 1).reshape(gp, h))


def _ssm_in_proj(ub, wb_re_ref, wb_im_ref, bu_re, bu_im):
    nkb, ublk, sblk = wb_re_ref.shape
    for kb in range(nkb):
        ukb = ub[:, kb * ublk:(kb + 1) * ublk]
        bu_re[:, kb * sblk:(kb + 1) * sblk] = _dot(ukb, wb_re_ref[kb])
        bu_im[:, kb * sblk:(kb + 1) * sblk] = _dot(ukb, wb_im_ref[kb])


def _ssm_out_proj(h_re, h_im, wc_re_ref, wc_imn_ref):
    nkb, sblk, _ = wc_re_ref.shape
    ys = []
    for kb in range(nkb):
        hr = h_re[:, kb * sblk:(kb + 1) * sblk].astype(BF16)
        hi = h_im[:, kb * sblk:(kb + 1) * sblk].astype(BF16)
        ys.append(_dot(hr, wc_re_ref[kb]) + _dot(hi, wc_imn_ref[kb]))
    return jnp.concatenate(ys, axis=1)


def _ssm_glu(x, u, y, d_ref, wglu_ref):
    z = jax.nn.gelu(y + d_ref[...] * u).astype(BF16)
    gl = _dot(z, wglu_ref[...])
    d = x.shape[1]
    return x + gl[:, :d] * jax.nn.sigmoid(gl[:, d:])


def _split3(v):
    hi = v.astype(BF16)
    r1 = v - hi.astype(F32)
    mid = r1.astype(BF16)
    lo = (r1 - mid.astype(F32)).astype(BF16)
    return hi, mid, lo


def _ssm_prompt_kernel(x_ref, g_ref, lam_re_ref, lam_im_ref, wb_re_ref, wb_im_ref,
                       wc_re_ref, wc_imn_ref, d_ref, wglu_ref, perm_ref, permt_ref,
                       out_ref, st_re_ref, st_im_ref, bu_re_all, bu_im_all, *, lc, bsz, lane_chunk):
    @pl.when(pl.program_id(0) == 0)
    def _():
        st_re_ref[...] = jnp.zeros_like(st_re_ref)
        st_im_ref[...] = jnp.zeros_like(st_im_ref)

    for sub in range(bu_re_all.shape[0]):
        ts = slice(sub * lc, (sub + 1) * lc)
        _ssm_prompt_chunk(x_ref[:, ts, :], g_ref, lam_re_ref, lam_im_ref, wb_re_ref, wb_im_ref,
                          wc_re_ref, wc_imn_ref, d_ref, wglu_ref, perm_ref, permt_ref,
                          out_ref.at[:, ts, :], st_re_ref, st_im_ref, bu_re_all.at[sub], bu_im_all.at[sub],
                          lc=lc, bsz=bsz, lane_chunk=lane_chunk)


def _ssm_prompt_chunk(x3, g_ref, lam_re_ref, lam_im_ref, wb_re_ref, wb_im_ref,
                      wc_re_ref, wc_imn_ref, d_ref, wglu_ref, perm_ref, permt_ref,
                      out_ref, st_re_ref, st_im_ref, bu_re, bu_im, *, lc, bsz, lane_chunk):
    d = x3.shape[2]
    x = x3.reshape(bsz * lc, d)
    u = _rms(x, g_ref[...])
    ub_tb = _dot(perm_ref[...], u.astype(BF16)).astype(BF16)
    _ssm_in_proj(ub_tb, wb_re_ref, wb_im_ref, bu_re, bu_im)

    gp = bu_re.shape[1]
    for c0 in range(0, gp, lane_chunk):
        sl = slice(c0, c0 + lane_chunk)
        lre = jnp.broadcast_to(lam_re_ref[:, sl], (bsz, lane_chunk))
        lim = jnp.broadcast_to(lam_im_ref[:, sl], (bsz, lane_chunk))
        hr = st_re_ref[:, sl]
        hi = st_im_ref[:, sl]
        for t in range(lc):
            rows = slice(t * bsz, (t + 1) * bsz)
            hr, hi = (lre * hr - lim * hi + bu_re[rows, sl], lre * hi + lim * hr + bu_im[rows, sl])
            bu_re[rows, sl] = hr
            bu_im[rows, sl] = hi
        st_re_ref[:, sl] = hr
        st_im_ref[:, sl] = hi

    permt = permt_ref[...]
    hi, mid, lo = _split3(_ssm_out_proj(bu_re, bu_im, wc_re_ref, wc_imn_ref))
    y = (_dot(permt, hi) + _dot(permt, mid)) + _dot(permt, lo)
    out_ref[...] = _ssm_glu(x, u, y, d_ref, wglu_ref).reshape(bsz, lc, d)


def _ssm_sample_kernel(x_ref, g_ref, lam_re_ref, lam_im_ref, h0_re_ref, h0_im_ref, wb_re_ref, wb_im_ref,
                       wc_re_ref, wc_imn_ref, d_ref, wglu_ref,
                       out_ref, h_re_ref, h_im_ref):
    x = x_ref[...]
    u = _rms(x, g_ref[...])
    _ssm_in_proj(u.astype(BF16), wb_re_ref, wb_im_ref, h_re_ref, h_im_ref)
    lre = lam_re_ref[...]
    lim = lam_im_ref[...]
    h0r = h0_re_ref[...]
    h0i = h0_im_ref[...]
    h_re_ref[...] = lre * h0r - lim * h0i + h_re_ref[...]
    h_im_ref[...] = lre * h0i + lim * h0r + h_im_ref[...]
    y = _ssm_out_proj(h_re_ref, h_im_ref, wc_re_ref, wc_imn_ref)
    n = x.shape[0]
    out_ref[0:n, :] = _ssm_glu(x, u, y, d_ref, wglu_ref)
    out_ref[n:, :] = jnp.zeros((out_ref.shape[0] - n, out_ref.shape[1]), F32)


def _ssm_prompt(x, g, lam_re, lam_im, wb_re, wb_im, wc_re, wc_imn, d_skip, wglu, *, lc, nsub):
    bsz, seq, d = x.shape
    assert seq % (lc * nsub) == 0
    gp = lam_re.shape[1]
    r = lc * bsz
    perm = np.zeros((r, r), np.float32)
    for b in range(bsz):
        for t in range(lc):
            perm[t * bsz + b, b * lc + t] = 1.0
    permt = jnp.asarray(perm.T, BF16)
    perm = jnp.asarray(perm, BF16)
    kern = functools.partial(_ssm_prompt_kernel, lc=lc, bsz=bsz, lane_chunk=8 * LANES)
    return pl.pallas_call(
        kern,
        grid=(seq // (lc * nsub),),
        in_specs=[pl.BlockSpec((bsz, lc * nsub, d), lambda c: (0, c, 0)),
                  _const_spec(g.shape), _const_spec(lam_re.shape), _const_spec(lam_im.shape),
                  _const_spec(wb_re.shape), _const_spec(wb_im.shape),
                  _const_spec(wc_re.shape), _const_spec(wc_imn.shape),
                  _const_spec(d_skip.shape), _const_spec(wglu.shape),
                  _const_spec(perm.shape), _const_spec(permt.shape)],
        out_specs=[pl.BlockSpec((bsz, lc * nsub, d), lambda c: (0, c, 0)),
                   _const_spec((bsz, gp)), _const_spec((bsz, gp))],
        out_shape=[jax.ShapeDtypeStruct((bsz, seq, d), F32),
                   jax.ShapeDtypeStruct((bsz, gp), F32), jax.ShapeDtypeStruct((bsz, gp), F32)],
        scratch_shapes=[pltpu.VMEM((nsub, r, gp), F32), pltpu.VMEM((nsub, r, gp), F32)],
        compiler_params=_params("arbitrary"),
        name="s5_prompt",
    )(x, g, lam_re, lam_im, wb_re, wb_im, wc_re, wc_imn, d_skip, wglu, perm, permt)


def _ssm_sample(x, g, lam_re, lam_im, h0_re, h0_im, wb_re, wb_im, wc_re, wc_imn, d_skip, wglu, *, pad_rows):
    n, d = x.shape
    gp = lam_re.shape[1]
    return pl.pallas_call(
        _ssm_sample_kernel,
        out_shape=[jax.ShapeDtypeStruct((pad_rows, d), F32),
                   jax.ShapeDtypeStruct((n, gp), F32), jax.ShapeDtypeStruct((n, gp), F32)],
        compiler_params=_params(),
        name="s5_sample",
    )(x, g, lam_re, lam_im, h0_re, h0_im, wb_re, wb_im, wc_re, wc_imn, d_skip, wglu)


def _ffn_chunks(d_ff):
    step = 3 * MXU_DIM
    return [(c, min(c + step, d_ff)) for c in range(0, d_ff, step)]


def _ffn_kernel(xp_ref, xs_ref, gffn_ref, wgu_ref, wd_ref, gkv_ref, wkv_ref, gq_ref, wq_ref,
                x2_ref, kv_ref, q_ref):
    x = jnp.where(pl.program_id(0) < pl.num_programs(0) - 1, xp_ref[...], xs_ref[...])
    hb = _rms(x, gffn_ref[...]).astype(BF16)
    d_ff = wd_ref.shape[0]
    acc = None
    for c0, c1 in _ffn_chunks(d_ff):
        a = _dot(hb, wgu_ref[:, c0:c1])
        b = _dot(hb, wgu_ref[:, d_ff + c0:d_ff + c1])
        part = _dot((jax.nn.silu(a) * b).astype(BF16), wd_ref[c0:c1, :])
        acc = part if acc is None else acc + part
    x2 = x + acc
    x2_ref[...] = x2
    kv_ref[...] = _dot(_rms(x2, gkv_ref[...]).astype(BF16), wkv_ref[...])
    q_ref[...] = _dot(_rms(x2, gq_ref[...]).astype(BF16), wq_ref[...]).astype(BF16)


def _ffn(xp, xs, gffn, wgu, wd, gkv, wkv, gq, wq):
    tm, d = xs.shape
    npt = xp.shape[0] // tm
    rows = xp.shape[0] + tm
    kvw = wkv.shape[1]
    nq = wq.shape[1]
    return pl.pallas_call(
        _ffn_kernel,
        grid=(npt + 1,),
        in_specs=[pl.BlockSpec((tm, d), lambda i: (jnp.minimum(i, npt - 1), 0)),
                  _const_spec(xs.shape),
                  _const_spec(gffn.shape), _const_spec(wgu.shape), _const_spec(wd.shape),
                  _const_spec(gkv.shape), _const_spec(wkv.shape),
                  _const_spec(gq.shape), _const_spec(wq.shape)],
        out_specs=[pl.BlockSpec((tm, d), lambda i: (i, 0)),
                   pl.BlockSpec((tm, kvw), lambda i: (i, 0)),
                   pl.BlockSpec((tm, nq), lambda i: (i, 0))],
        out_shape=[jax.ShapeDtypeStruct((rows, d), F32),
                   jax.ShapeDtypeStruct((rows, kvw), F32),
                   jax.ShapeDtypeStruct((rows, nq), BF16)],
        compiler_params=_params("arbitrary"),
        name="ffn_kv_q",
    )(xp, xs, gffn, wgu, wd, gkv, wkv, gq, wq)


def _t5_bucket(dist, num_buckets):
    max_exact = num_buckets // 2
    d = jnp.maximum(dist, 0)
    large = max_exact + (jnp.log(jnp.maximum(d, 1).astype(F32) / max_exact)
                         / math.log(MAX_DISTANCE / max_exact) * (num_buckets - max_exact)).astype(jnp.int32)
    large = jnp.minimum(large, num_buckets - 1)
    return jnp.where(d < max_exact, d, large)


def _bias_kernel(bm_ref, bs_ref, rb_ref, bias_ref, bias_s_ref, *, kvh, rep):
    nb, nh = rb_ref.shape
    bm = bm_ref[...]
    bs = bs_ref[...]

    def lookup(buckets, h):
        def body(k, acc):
            return jnp.where(buckets == k, rb_ref[k, h], acc)
        return lax.fori_loop(0, nb, body, jnp.zeros(buckets.shape, F32))

    w = bm.shape[0]
    for g in range(kvh):
        for r in range(rep):
            bias_ref[g, :, r * w:(r + 1) * w] = lookup(bm, g * rep + r) * math.log2(math.e)
            bias_s_ref[g, r:r + 1, :] = lookup(bs, g * rep + r)


def _bias_tables(rel_bias, window, kvh, rep):
    nb, nh = rel_bias.shape
    key = jnp.arange(window)[:, None]
    qry = jnp.arange(window)[None, :]
    bm = _t5_bucket((qry - key) % window, nb).astype(jnp.int32)
    bs = _t5_bucket(window - 1 - qry, nb).astype(jnp.int32)
    kern = functools.partial(_bias_kernel, kvh=kvh, rep=rep)
    return pl.pallas_call(
        kern,
        in_specs=[pl.BlockSpec(memory_space=pltpu.VMEM), pl.BlockSpec(memory_space=pltpu.VMEM),
                  pl.BlockSpec(memory_space=pltpu.SMEM)],
        out_shape=[jax.ShapeDtypeStruct((kvh, window, rep * window), F32),
                   jax.ShapeDtypeStruct((kvh, rep, window), F32)],
        name="t5_bias",
    )(bm, bs, rel_bias)


def _attn_prompt_kernel(q_ref, kvc_ref, kvp_ref, bias_ref, sink_ref, ot_ref, *, kvh, rep, hd, scale):
    w = kvp_ref.shape[0]
    kv_all = kvc_ref[...]
    for sub in range(q_ref.shape[0] // w):
        rows = slice(sub * w, (sub + 1) * w)
        kvp = kvp_ref[...] if sub == 0 else kv_all[(sub - 1) * w:sub * w]
        first = pl.program_id(1) == 0 if sub == 0 else None
        _attn_prompt_block(q_ref[rows, :], kv_all[rows], kvp, first, bias_ref, sink_ref, ot_ref, sub * w,
                           kvh=kvh, rep=rep, hd=hd, scale=scale)


def _attn_prompt_block(q, kvc, kvp, first, bias_ref, sink_ref, ot_ref, col0, *, kvh, rep, hd, scale):
    w = q.shape[0]
    kvw = kvh * hd
    lane = lax.broadcasted_iota(jnp.int32, (1, LANES), 1)
    key = lax.broadcasted_iota(jnp.int32, (w, rep * w), 0)
    qry = lax.broadcasted_iota(jnp.int32, (w, rep * w), 1) % w
    upper = key > qry
    mask_add = None if first is None else jnp.where(jnp.logical_and(upper, first), NEG_INF, 0.0)
    log2e = math.log2(math.e)
    heads_per_blk = LANES // hd
    for p in range(kvw // LANES):
        cs = slice(p * LANES, (p + 1) * LANES)
        kc, kp = kvc[:, cs], kvp[:, cs]
        vs = slice(kvw + p * LANES, kvw + (p + 1) * LANES)
        vb = jnp.concatenate([kvp[:, vs], kvc[:, vs]], axis=0).astype(BF16)
        qg = jnp.concatenate([q[:, r * kvw + p * LANES:r * kvw + (p + 1) * LANES] for r in range(rep)], axis=0)
        halves = []
        for half in range(heads_per_blk):
            g = p * heads_per_blk + half
            lmask = jnp.logical_and(lane >= half * hd, lane < (half + 1) * hd)
            kb = jnp.concatenate([jnp.where(lmask, kp, 0.0), jnp.where(lmask, kc, 0.0)], axis=0).astype(BF16)
            s = lax.dot_general(kb, qg, (((1,), (1,)), ((), ())), preferred_element_type=F32)
            bias = bias_ref[g] if first is None else bias_ref[g] + mask_add
            sc = jnp.where(upper, s[:w], s[w:]) * (scale * log2e) + bias
            sink = jnp.concatenate([jnp.full((1, w), sink_ref[g * rep + r] * log2e, F32) for r in range(rep)],
                                   axis=1)
            m = jnp.maximum(jnp.max(sc, axis=0, keepdims=True), sink)
            pe = jnp.exp2(sc - m)
            denom = jnp.sum(pe, axis=0, keepdims=True) + jnp.exp2(sink - m)
            pcat = jnp.concatenate([jnp.where(upper, pe, 0.0), jnp.where(upper, 0.0, pe)], axis=0).astype(BF16)
            og = lax.dot_general(vb, pcat, (((0,), (0,)), ((), ())), preferred_element_type=F32)
            halves.append((og / denom)[half * hd:(half + 1) * hd])
        o_blk = jnp.concatenate(halves, axis=0).astype(BF16)
        for r in range(rep):
            ot_ref[r * kvw + p * LANES:r * kvw + (p + 1) * LANES, col0:col0 + w] = o_blk[:, r * w:(r + 1) * w]


def _attn_prompt(q, kv, bias, sinks, *, bsz, nblk, window, kvh, rep, hd):
    rows, nq = q.shape
    kvw2 = kv.shape[1]
    nsub = next(n for n in (4, 2, 1) if nblk % n == 0 and rows % (n * window) == 0)
    nstep = nblk // nsub
    kern = functools.partial(_attn_prompt_kernel, kvh=kvh, rep=rep, hd=hd, scale=1.0 / math.sqrt(hd))
    return pl.pallas_call(
        kern,
        grid=(bsz, nstep),
        in_specs=[pl.BlockSpec((nsub * window, nq), lambda b, i: (b * nstep + i, 0)),
                  pl.BlockSpec((nsub * window, kvw2), lambda b, i: (b * nstep + i, 0)),
                  pl.BlockSpec((window, kvw2), lambda b, i: (b * nblk + jnp.maximum(nsub * i - 1, 0), 0)),
                  _const_spec(bias.shape),
                  pl.BlockSpec(memory_space=pltpu.SMEM)],
        out_specs=pl.BlockSpec((nq, nsub * window), lambda b, i: (0, b * nstep + i)),
        out_shape=jax.ShapeDtypeStruct((nq, rows), BF16),
        compiler_params=_params("arbitrary", "arbitrary"),
        name="swa_prompt",
    )(q, kv, kv, bias, sinks)


def _attn_sample_kernel(q_ref, kv_ref, ck_ref, cv_ref, bias_ref, sink_ref, o_ref, nk_ref, nv_ref,
                        *, kvh, rep, hd, scale):
    nb, w, kvw = ck_ref.shape
    kv = kv_ref[...]
    lane = lax.broadcasted_iota(jnp.int32, (nb, rep, kvw), 2)

    def shifted(c_ref, n_ref, new):
        flat = c_ref[...].reshape(nb * w, kvw)
        n_ref[...] = pltpu.roll(flat, nb * w - 1, axis=0).reshape(nb, w, kvw)
        n_ref[:, w - 1:w, :] = new
        return n_ref[...].astype(BF16)

    nkb = shifted(ck_ref, nk_ref, kv[:, :, :kvw])
    nvb = shifted(cv_ref, nv_ref, kv[:, :, kvw:])
    q = q_ref[...].astype(F32)
    o = jnp.zeros((nb, rep, kvw), F32)
    for g in range(kvh):
        lmask = jnp.logical_and(lane >= g * hd, lane < (g + 1) * hd)
        qg = jnp.where(lmask, q, 0.0).astype(BF16)
        s = jnp.einsum("nrc,njc->nrj", qg, nkb, preferred_element_type=F32)
        sc = s * scale + bias_ref[g][None]
        sink = sink_ref[g][None]
        m = jnp.maximum(jnp.max(sc, axis=-1, keepdims=True), sink)
        pe = jnp.exp(sc - m)
        probs = pe / (jnp.sum(pe, axis=-1, keepdims=True) + jnp.exp(sink - m))
        og = jnp.einsum("nrj,njc->nrc", probs.astype(BF16), nvb, preferred_element_type=F32)
        o = jnp.where(lmask, og, o)
    o_ref[...] = o.astype(BF16)


def _attn_sample(q3, kv3, ck, cv, bias_s, sink_s, *, nb, kvh, rep, hd):
    n, w, kvw = ck.shape
    kern = functools.partial(_attn_sample_kernel, kvh=kvh, rep=rep, hd=hd, scale=1.0 / math.sqrt(hd))
    cache_spec = pl.BlockSpec((nb, w, kvw), lambda i: (i, 0, 0))
    return pl.pallas_call(
        kern,
        grid=(n // nb,),
        in_specs=[pl.BlockSpec((nb, rep, kvw), lambda i: (i, 0, 0)),
                  pl.BlockSpec((nb, 1, 2 * kvw), lambda i: (i, 0, 0)),
                  cache_spec, cache_spec,
                  _const_spec(bias_s.shape), _const_spec(sink_s.shape)],
        out_specs=[pl.BlockSpec((nb, rep, kvw), lambda i: (i, 0, 0)), cache_spec, cache_spec],
        out_shape=[jax.ShapeDtypeStruct((n, rep, kvw), BF16),
                   jax.ShapeDtypeStruct((n, w, kvw), F32), jax.ShapeDtypeStruct((n, w, kvw), F32)],
        compiler_params=_params("arbitrary"),
        name="swa_sample",
    )(q3, kv3, ck, cv, bias_s, sink_s)


CHUNK = 16
SUB = 256
UP_ROW_BLOCK = 1024
DOWN_ROW_BLOCK = 1024
TOKEN_TILE = 512
MOE_TILE = 512


def _chunk_copy(src, src_row, dst, dst_row, sem):
    return pltpu.make_async_copy(src.at[pl.ds(pl.multiple_of(src_row, CHUNK), CHUNK), :],
                                 dst.at[pl.ds(pl.multiple_of(dst_row, CHUNK), CHUNK), :], sem)


def _moe_route_kernel(x2_ref, ot_ref, wo_ref, gffn_ref, wr_ref, lst_ref, ust_ref,
                      x3_ref, cm_ref, tmeta_ref, xs_hbm,
                      comp_s, carry_s, zero_s, base_v, pend_sm, sem, *, ne, cap, n_valid):
    i = pl.program_id(0)
    nt = pl.num_programs(0)
    tm = x2_ref.shape[0]
    w = comp_s.shape[1]
    slot = lax.rem(i, 2)

    @pl.when(i == 0)
    def _():
        base_v[...] = jnp.zeros_like(base_v)
        zero_s[...] = jnp.zeros_like(zero_s)
        carry_s[...] = jnp.zeros_like(carry_s)

    def drain(n):
        def body(c, carry):
            _chunk_copy(zero_s, 0, xs_hbm, 0, sem).wait()
            return carry
        lax.fori_loop(0, n, body, 0)

    @pl.when(i > 0)
    def _():
        drain(pend_sm[0])

    x3 = x2_ref[...] + lax.dot_general(ot_ref[...], wo_ref[...], (((0,), (0,)), ((), ())),
                                       preferred_element_type=F32)
    x3_ref[...] = x3
    hb = _rms(x3, gffn_ref[...]).astype(BF16)

    lane = lax.broadcasted_iota(jnp.int32, (tm, LANES), 1)
    logits = jnp.where(lane < ne, _dot(hb, wr_ref[...]), -jnp.inf)
    m1 = jnp.max(logits, axis=-1, keepdims=True)
    i1 = jnp.min(jnp.where(logits == m1, lane, LANES), axis=-1, keepdims=True)
    rest = jnp.where(lane == i1, -jnp.inf, logits)
    m2 = jnp.max(rest, axis=-1, keepdims=True)
    i2 = jnp.min(jnp.where(rest == m2, lane, LANES), axis=-1, keepdims=True)
    e2 = jnp.exp(m2 - m1)
    g1 = 1.0 / (1.0 + e2)
    g2 = e2 / (1.0 + e2)

    live = i * tm + lax.broadcasted_iota(jnp.int32, (tm, 1), 0) < n_valid
    sel = jnp.where(jnp.logical_and(live, jnp.logical_or(lane == i1, lane == i2)), 1.0, 0.0)
    rank = _dot(lst_ref[...], sel.astype(BF16))
    cnt = jnp.sum(sel, axis=0, keepdims=True)
    fill = base_v[...]
    rem = fill - jnp.floor(fill / CHUNK) * CHUNK
    cpad = jnp.floor((rem + cnt + (CHUNK - 1)) / CHUNK) * CHUNK
    loff = _dot(jnp.broadcast_to(cpad, (8, LANES)).astype(BF16), ust_ref[...])[0:1]
    dest = loff + rem + rank
    ld1 = jnp.where(live, jnp.sum(jnp.where(lane == i1, dest, 0.0), axis=-1, keepdims=True), -1.0)
    ld2 = jnp.where(live, jnp.sum(jnp.where(lane == i2, dest, 0.0), axis=-1, keepdims=True), -1.0)
    cm = jnp.where(lane == 0, ld1, jnp.where(lane == 1, ld2, jnp.where(lane == 2, g1, jnp.where(lane == 3, g2, 0.0))))
    cm_ref[...] = cm

    rm = cm.T
    rowi = lax.broadcasted_iota(jnp.int32, (w, tm), 0).astype(F32)
    place = (jnp.where(rowi == rm[0:1], 1.0, 0.0) + jnp.where(rowi == rm[1:2], 1.0, 0.0)).astype(BF16)
    comp_s[slot] = _dot(place, hb).astype(BF16)

    base = fill - rem
    srow = lax.broadcasted_iota(jnp.int32, (8, LANES), 0)
    tmeta_ref[...] = jnp.where(srow == 0, cpad, jnp.where(srow == 1, base, jnp.where(srow == 2, fill + cnt, 0.0))
                               ).astype(jnp.int32)
    base_v[...] = fill + cnt

    cpad_i = cpad.astype(jnp.int32)
    loff_i = loff.astype(jnp.int32)
    base_i = base.astype(jnp.int32)
    tail_i = (rem + cnt - jnp.floor((rem + cnt) / CHUNK) * CHUNK).astype(jnp.int32)
    total = 0
    tails = []
    for e in range(ne):
        n_e = cpad_i[0, e]
        src0 = loff_i[0, e]
        dst0 = base_i[0, e] + e * cap

        @pl.when(n_e > 0)
        def _(e=e, n_e=n_e, src0=src0, partial=tail_i[0, e] > 0):
            head = pl.ds(pl.multiple_of(src0, CHUNK), CHUNK)
            comp_s[slot, head, :] = comp_s[slot, head, :] + carry_s[e]
            last = comp_s[slot, pl.ds(pl.multiple_of(src0 + n_e - CHUNK, CHUNK), CHUNK), :]
            carry_s[e] = jnp.where(partial, last, jnp.zeros_like(last))

        def start(c, carry, src0=src0, dst0=dst0):
            _chunk_copy(comp_s.at[slot], src0 + c * CHUNK, xs_hbm, dst0 + c * CHUNK, sem).start()
            return carry

        nchunks = lax.shift_right_logical(n_e, int(math.log2(CHUNK)))
        lax.fori_loop(0, nchunks, start, 0)
        total = total + nchunks
        tails.append(dst0 + n_e)
    pend_sm[0] = total

    @pl.when(i == nt - 1)
    def _():
        nz_total = 0
        for e in range(ne):
            end = tails[e]
            nz = lax.shift_right_logical(lax.rem(SUB - lax.rem(end, SUB), SUB), int(math.log2(CHUNK)))

            def zstart(c, carry, end=end):
                _chunk_copy(zero_s, 0, xs_hbm, end + c * CHUNK, sem).start()
                return carry

            lax.fori_loop(0, nz, zstart, 0)
            nz_total = nz_total + nz
        drain(total + nz_total)


def _moe_route(x2, ot, wo, gffn, wr_pad, *, tm, ne, cap, n_valid):
    rows, d = x2.shape
    nt = rows // tm
    w = _staging_rows(tm, ne)
    lst = jnp.asarray(np.tril(np.ones((tm, tm), np.float32), -1), BF16)
    ust = jnp.asarray(np.triu(np.ones((LANES, LANES), np.float32), 1), BF16)
    kern = functools.partial(_moe_route_kernel, ne=ne, cap=cap, n_valid=n_valid)
    return pl.pallas_call(
        kern,
        grid=(nt,),
        in_specs=[pl.BlockSpec((tm, d), lambda i: (i, 0)),
                  pl.BlockSpec((ot.shape[0], tm), lambda i: (0, i)),
                  _const_spec(wo.shape), _const_spec(gffn.shape), _const_spec(wr_pad.shape),
                  _const_spec(lst.shape), _const_spec(ust.shape)],
        out_specs=[pl.BlockSpec((tm, d), lambda i: (i, 0)),
                   pl.BlockSpec((tm, LANES), lambda i: (i, 0)),
                   pl.BlockSpec((8, LANES), lambda i: (i, 0)),
                   pl.BlockSpec(memory_space=pl.ANY)],
        out_shape=[jax.ShapeDtypeStruct((rows, d), F32),
                   jax.ShapeDtypeStruct((rows, LANES), F32),
                   jax.ShapeDtypeStruct((nt * 8, LANES), jnp.int32),
                   jax.ShapeDtypeStruct((ne * cap, d), BF16)],
        scratch_shapes=[pltpu.VMEM((2, w, d), BF16), pltpu.VMEM((ne, CHUNK, d), BF16), pltpu.VMEM((CHUNK, d), BF16),
                        pltpu.VMEM((1, LANES), F32), pltpu.SMEM((1,), jnp.int32),
                        pltpu.SemaphoreType.DMA(())],
        compiler_params=_params("arbitrary"),
        name="moe_route",
    )(x2, ot, wo, gffn, wr_pad, lst, ust)


def _staging_rows(tm, ne):
    return -(-(TOP_K * tm + 2 * ne * (CHUNK - 1)) // LANES) * LANES


def _max_blocks(rows, ne, row_block):
    return TOP_K * rows // row_block + ne


def _expert_steps(nrows, nf, max_blocks, row_block):
    ne = nrows.shape[0]
    nblk = (nrows + row_block - 1) // row_block
    cum = jnp.cumsum(nblk) * nf
    total = cum[-1]
    s = jnp.minimum(jnp.arange(max_blocks * nf, dtype=jnp.int32), total - 1)
    e = jnp.minimum(jnp.sum((s[:, None] >= cum[None, :]).astype(jnp.int32), axis=1), ne - 1)
    nb_e = nblk[e]
    within = s - (cum[e] - nb_e * nf)
    f = within // nb_e
    pos = within - f * nb_e
    r = (pos + nb_e - 1) % nb_e
    nsub = jnp.clip((nrows[e] - r * row_block + SUB - 1) // SUB, 0, row_block // SUB)
    first = (pos == 0).astype(jnp.int32)
    return (e, f.astype(jnp.int32), r.astype(jnp.int32), first, nsub.astype(jnp.int32),
            jnp.reshape(total, (1,)).astype(jnp.int32))


def _moe_up_kernel(e_ref, f_ref, r_ref, first_ref, nsub_ref, n_ref, xs_ref, wg_ref, wu_ref, act_ref, wgb_s, wub_s):
    s = pl.program_id(0)

    @pl.when(s < n_ref[0])
    def _():
        @pl.when(first_ref[s] == 1)
        def _():
            wgb_s[...] = wg_ref[...].astype(BF16)
            wub_s[...] = wu_ref[...].astype(BF16)

        def body(j, carry):
            rows = pl.ds(pl.multiple_of(j * SUB, SUB), SUB)
            xsb = xs_ref[rows, :]
            a = _dot(xsb, wgb_s[...])
            b = _dot(xsb, wub_s[...])
            act_ref[rows, :] = (jax.nn.silu(a) * b).astype(BF16)
            return carry

        lax.fori_loop(0, nsub_ref[s], body, 0)


def _moe_up(steps, xs, wgu, *, fc, cap, row_block):
    ne, d, dff2 = wgu.shape
    dff = dff2 // 2
    nf = dff // fc
    cb = cap // row_block
    grid_spec = pltpu.PrefetchScalarGridSpec(
        num_scalar_prefetch=6,
        grid=(steps[0].shape[0],),
        in_specs=[pl.BlockSpec((row_block, d), lambda s, e, f, r, *_: (e[s] * cb + r[s], 0)),
                  pl.BlockSpec((None, d, fc), lambda s, e, f, r, *_: (e[s], 0, f[s])),
                  pl.BlockSpec((None, d, fc), lambda s, e, f, r, *_: (e[s], 0, nf + f[s]))],
        out_specs=pl.BlockSpec((row_block, fc), lambda s, e, f, r, *_: (e[s] * cb + r[s], f[s])),
        scratch_shapes=[pltpu.VMEM((d, fc), BF16), pltpu.VMEM((d, fc), BF16)])
    return pl.pallas_call(
        _moe_up_kernel,
        grid_spec=grid_spec,
        out_shape=jax.ShapeDtypeStruct((ne * cap, dff), BF16),
        compiler_params=_params("arbitrary"),
        name="moe_up",
    )(*steps, xs, wgu, wgu)


def _moe_down_kernel(e_ref, f_ref, r_ref, first_ref, nsub_ref, n_ref, act_ref, wd_ref, ys_ref, wdb_s):
    s = pl.program_id(0)

    @pl.when(s < n_ref[0])
    def _():
        @pl.when(first_ref[s] == 1)
        def _():
            wdb_s[...] = wd_ref[...].astype(BF16)

        def body(j, carry):
            rows = pl.ds(pl.multiple_of(j * SUB, SUB), SUB)
            ys_ref[rows, :] = _dot(act_ref[rows, :], wdb_s[...]).astype(BF16)
            return carry

        lax.fori_loop(0, nsub_ref[s], body, 0)


def _moe_down(steps, act, wd, *, cap, row_block):
    ne, dff, d = wd.shape
    cb = cap // row_block
    grid_spec = pltpu.PrefetchScalarGridSpec(
        num_scalar_prefetch=6,
        grid=(steps[0].shape[0],),
        in_specs=[pl.BlockSpec((row_block, dff), lambda s, e, f, r, *_: (e[s] * cb + r[s], 0)),
                  pl.BlockSpec((None, dff, d), lambda s, e, f, r, *_: (e[s], 0, 0))],
        out_specs=pl.BlockSpec((row_block, d), lambda s, e, f, r, *_: (e[s] * cb + r[s], 0)),
        scratch_shapes=[pltpu.VMEM((dff, d), BF16)])
    return pl.pallas_call(
        _moe_down_kernel,
        grid_spec=grid_spec,
        out_shape=jax.ShapeDtypeStruct((ne * cap, d), BF16),
        compiler_params=_params("arbitrary", vmem=2 * (dff * d * 4 + row_block * (dff + d) * 2) + dff * d * 2
                                + 4 * SUB * d * 4),
        name="moe_down",
    )(*steps, act, wd)


def _moe_combine_kernel(cpad_ref, seg_ref, x3_ref, cm_ref, gfin_ref, ys_hbm, yp_ref, ys_ref, yloc_s, sem,
                        *, ne, npt):
    i = pl.program_id(0)
    nt = pl.num_programs(0)
    tm = x3_ref.shape[0]
    w = yloc_s.shape[1]
    slot = lax.rem(i, 2)
    shift = int(math.log2(CHUNK))

    def issue(t, sl):
        off = 0
        for e in range(ne):
            n_e = cpad_ref[t * ne + e]
            src0 = seg_ref[t * ne + e]

            def start(c, carry, src0=src0, off=off):
                _chunk_copy(ys_hbm, src0 + c * CHUNK, yloc_s.at[sl], off + c * CHUNK, sem.at[sl]).start()
                return carry

            lax.fori_loop(0, lax.shift_right_logical(n_e, shift), start, 0)
            off = off + n_e

    @pl.when(i == 0)
    def _():
        yloc_s[...] = jnp.zeros_like(yloc_s)
        issue(0, 0)

    @pl.when(i + 1 < nt)
    def _():
        issue(i + 1, 1 - slot)

    total = 0
    for e in range(ne):
        total = total + lax.shift_right_logical(cpad_ref[i * ne + e], shift)

    def wait(c, carry):
        _chunk_copy(ys_hbm, 0, yloc_s.at[slot], 0, sem.at[slot]).wait()
        return carry

    lax.fori_loop(0, total, wait, 0)

    cm = cm_ref[...]
    col = lax.broadcasted_iota(jnp.int32, (tm, w), 1).astype(F32)
    yl = yloc_s[slot]
    y1 = _dot(jnp.where(col == cm[:, 0:1], 1.0, 0.0).astype(BF16), yl)
    y2 = _dot(jnp.where(col == cm[:, 1:2], 1.0, 0.0).astype(BF16), yl)
    y = _rms(x3_ref[...] + cm[:, 2:3] * y1 + cm[:, 3:4] * y2, gfin_ref[...])

    @pl.when(i < npt)
    def _():
        yp_ref[...] = y

    @pl.when(i >= npt)
    def _():
        ys_ref[...] = y


def _moe_combine(cpad, seg, x3, cm, gfin, ys, *, tm, ne, sample_rows):
    rows, d = x3.shape
    nt = rows // tm
    npt = (rows - sample_rows) // tm
    w = _staging_rows(tm, ne)
    kern = functools.partial(_moe_combine_kernel, ne=ne, npt=npt)
    grid_spec = pltpu.PrefetchScalarGridSpec(
        num_scalar_prefetch=2,
        grid=(nt,),
        in_specs=[pl.BlockSpec((tm, d), lambda i, *_: (i, 0)),
                  pl.BlockSpec((tm, LANES), lambda i, *_: (i, 0)),
                  pl.BlockSpec(gfin.shape, lambda i, *_: (0, 0)),
                  pl.BlockSpec(memory_space=pl.ANY)],
        out_specs=[pl.BlockSpec((tm, d), lambda i, *_: (jnp.minimum(i, npt - 1), 0)),
                   pl.BlockSpec((tm, d), lambda i, *_: (jnp.maximum(i - npt, 0), 0))],
        scratch_shapes=[pltpu.VMEM((2, w, d), BF16), pltpu.SemaphoreType.DMA((2,))])
    return pl.pallas_call(
        kern,
        grid_spec=grid_spec,
        out_shape=[jax.ShapeDtypeStruct((npt * tm, d), F32), jax.ShapeDtypeStruct((sample_rows, d), F32)],
        compiler_params=_params("arbitrary"),
        name="moe_combine",
    )(cpad, seg, x3, cm, gfin, ys)


def _moe(x2, ot, wo, gffn, wr, wgu, wd, gfin, *, tm, fc, n_valid, sample_rows):
    rows, d = x2.shape
    ne = wr.shape[1]
    nt = rows // tm
    nf = wgu.shape[2] // 2 // fc
    cap = -(-(rows + SUB) // UP_ROW_BLOCK) * UP_ROW_BLOCK
    wr_pad = jnp.pad(wr, ((0, 0), (0, LANES - ne))).astype(BF16)
    x3, cm, tmeta, xs = _moe_route(x2, ot, wo, gffn, wr_pad, tm=tm, ne=ne, cap=cap, n_valid=n_valid)
    tmeta = tmeta.reshape(nt, 8, LANES)[:, :, :ne]
    cpad = tmeta[:, 0, :].reshape(nt * ne)
    seg = (tmeta[:, 1, :] + jnp.arange(ne, dtype=jnp.int32)[None, :] * cap).reshape(nt * ne)
    nrows = tmeta[nt - 1, 2, :]
    up_steps = _expert_steps(nrows, nf, _max_blocks(rows, ne, UP_ROW_BLOCK), UP_ROW_BLOCK)
    act = _moe_up(up_steps, xs, wgu, fc=fc, cap=cap, row_block=UP_ROW_BLOCK)
    down_steps = _expert_steps(nrows, 1, _max_blocks(rows, ne, DOWN_ROW_BLOCK), DOWN_ROW_BLOCK)
    ys = _moe_down(down_steps, act, wd, cap=cap, row_block=DOWN_ROW_BLOCK)
    return _moe_combine(cpad, seg, x3, cm, gfin, ys, tm=tm, ne=ne, sample_rows=sample_rows)


def kernel(x_prompt, x_sample, state_ssm_re, state_ssm_im, cache_k_win, cache_v_win, g_mix, g_ffn, g_kv, g_final, ssm_a_re, ssm_a_im, ssm_log_dt, ssm_b_re, ssm_b_im, ssm_c_re, ssm_c_im, ssm_d, w_glu, w_kv, w_q, w_o, attn_sinks, rel_bias, w_ffn_gate_up, w_ffn_down, w_router, w_exp_gate_up, w_exp_down):
    bsz, seq, d = x_prompt.shape
    ns, dec_seq, _ = x_sample.shape
    assert dec_seq == 1 and g_mix.shape[0] == 2 and ssm_a_re.shape[0] == 1 and w_q.shape[0] == 1
    _, g, p = ssm_a_re.shape
    gp = g * p
    window, kvh, hd = cache_k_win.shape[1:]
    kvw = kvh * hd
    nh = attn_sinks.shape[1]
    rep = nh // kvh
    nq = nh * hd
    assert bsz == 8 and ns % bsz == 0 and seq % window == 0 and LANES % hd == 0

    lam_re, lam_im, wb_re, wb_im, wc_re, wc_imn = _zoh(ssm_a_re[0], ssm_a_im[0], ssm_log_dt[0],
                                                        ssm_b_re[0], ssm_b_im[0], ssm_c_re[0], ssm_c_im[0])
    wglu = w_glu[0].astype(BF16)
    d_skip = ssm_d[0].reshape(1, d)
    wgu = w_ffn_gate_up[0].astype(BF16)
    wd = w_ffn_down[0].astype(BF16)
    wkv = w_kv.astype(BF16)
    wq = w_q[0].reshape(d, kvh, rep, hd).transpose(0, 2, 1, 3).reshape(d, nq).astype(BF16)
    wo = w_o[0].reshape(kvh, rep, hd, d).transpose(1, 0, 2, 3).reshape(nq, d).astype(BF16)
    bias, bias_s = _bias_tables(rel_bias, window, kvh, rep)
    sinks = attn_sinks[0]
    sink_s = sinks.reshape(kvh, rep, 1)

    tm = TOKEN_TILE
    npr = seq * bsz
    assert npr % tm == 0 and ns <= tm

    x1_p, st_re, st_im = _ssm_prompt(x_prompt, g_mix[0:1], lam_re, lam_im, wb_re, wb_im, wc_re, wc_imn,
                                     d_skip, wglu, lc=32, nsub=2)
    x1_s, hs_re, hs_im = _ssm_sample(x_sample.reshape(ns, d), g_mix[0:1], lam_re, lam_im,
                                     state_ssm_re[0].reshape(ns, gp), state_ssm_im[0].reshape(ns, gp),
                                     wb_re, wb_im, wc_re, wc_imn, d_skip, wglu, pad_rows=tm)

    x2, kv, q = _ffn(x1_p.reshape(npr, d), x1_s, g_ffn[0:1], wgu, wd, g_kv.reshape(1, d), wkv, g_mix[1:2], wq)

    ot = _attn_prompt(q, kv, bias, sinks, bsz=bsz, nblk=seq // window, window=window, kvh=kvh, rep=rep, hd=hd)
    o_s, nk_s, nv_s = _attn_sample(q[npr:npr + ns].reshape(ns, rep, kvw), kv[npr:npr + ns].reshape(ns, 1, 2 * kvw),
                                   cache_k_win.reshape(ns, window, kvw), cache_v_win.reshape(ns, window, kvw),
                                   bias_s, sink_s, nb=16, kvh=kvh, rep=rep, hd=hd)
    ot = lax.dynamic_update_slice(ot, jnp.pad(o_s.reshape(ns, nq).T, ((0, 0), (0, tm - ns))), (0, npr))

    y_p, y_s = _moe(x2, ot, wo, g_ffn[1:2], w_router[0], w_exp_gate_up[0], w_exp_down[0], g_final.reshape(1, d),
                    tm=MOE_TILE, fc=w_exp_down.shape[2] // 2, n_valid=npr + ns, sample_rows=tm)

    y_prompt = y_p.reshape(bsz, seq, d)
    y_sample = y_s[:ns].reshape(ns, 1, d)
    kv_tail = jnp.stack([kv[(b + 1) * seq - window:(b + 1) * seq] for b in range(bsz)])
    kv_tail = kv_tail.reshape(bsz, window, 2, kvh, hd).transpose(2, 0, 1, 3, 4)
    return (y_prompt, y_sample,
            st_re.reshape(1, bsz, g, p), st_im.reshape(1, bsz, g, p), kv_tail[0], kv_tail[1],
            hs_re.reshape(1, ns, g, p), hs_im.reshape(1, ns, g, p),
            nk_s.reshape(ns, window, kvh, hd), nv_s.reshape(ns, window, kvh, hd))
```

```python
import functools
import math

import numpy as np
import jax
import jax.numpy as jnp
from jax import lax
from jax.experimental import pallas as pl
from jax.experimental.pallas import tpu as pltpu

F32 = jnp.float32
BF16 = jnp.bfloat16

EPS = 1e-6
NEG_INF = -1e30
TOP_K = 2
MAX_DISTANCE = 128
MXU_DIM = 256
LANES = 128
VMEM_LIMIT_BYTES = 56 * 1024 * 1024


def _dot(a, b):
    return jnp.dot(a, b, preferred_element_type=F32)


def _rms(x, g):
    return x * lax.rsqrt(jnp.mean(x * x, axis=-1, keepdims=True) + EPS) * g


def _const_spec(shape):
    nd = len(shape)
    return pl.BlockSpec(shape, lambda *_: (0,) * nd)


def _params(*sem, vmem=VMEM_LIMIT_BYTES):
    return pltpu.CompilerParams(dimension_semantics=sem, vmem_limit_bytes=vmem)


def _zoh_kernel(a_re_ref, a_im_ref, log_dt_ref, b_re_ref, b_im_ref, ct_re_ref, ct_im_ref,
                lam_re_ref, lam_im_ref, wb_re_ref, wb_im_ref, wc_re_ref, wc_imn_ref, *, hch, p):
    a_re = a_re_ref[...]
    a_im = a_im_ref[...]
    dt = jnp.exp(log_dt_ref[...])
    mag = jnp.exp(a_re * dt)
    lr = mag * jnp.cos(a_im * dt)
    li = mag * jnp.sin(a_im * dt)
    lam_re_ref[...] = lr
    lam_im_ref[...] = li
    nr = lr - 1.0
    den = a_re * a_re + a_im * a_im
    qr = (nr * a_re + li * a_im) / den
    qi = (li * a_re - nr * a_im) / den
    b_re = b_re_ref[...]
    b_im = b_im_ref[...]
    bb_re = qr * b_re - qi * b_im
    bb_im = qr * b_im + qi * b_re

    nkb, ublk, sblk = wb_re_ref.shape
    gpb = ublk // hch
    sh_h, sh_p = int(math.log2(hch)), int(math.log2(p))
    row = lax.broadcasted_iota(jnp.int32, (ublk, sblk), 0)
    col = lax.broadcasted_iota(jnp.int32, (ublk, sblk), 1)
    diag_in = lax.shift_right_logical(row, sh_h) == lax.shift_right_logical(col, sh_p)
    row = lax.broadcasted_iota(jnp.int32, (sblk, ublk), 0)
    col = lax.broadcasted_iota(jnp.int32, (sblk, ublk), 1)
    diag_out = lax.shift_right_logical(row, sh_p) == lax.shift_right_logical(col, sh_h)
    spread = jnp.where(lax.broadcasted_iota(jnp.int32, (hch, ublk), 0)
                       == jnp.bitwise_and(lax.broadcasted_iota(jnp.int32, (hch, ublk), 1), hch - 1),
                       1.0, 0.0).astype(BF16)
    for kb in range(nkb):
        cs = slice(kb * sblk, (kb + 1) * sblk)
        wb_re_ref[kb] = jnp.where(diag_in, jnp.concatenate([bb_re[:, cs]] * gpb, axis=0), 0.0).astype(BF16)
        wb_im_ref[kb] = jnp.where(diag_in, jnp.concatenate([bb_im[:, cs]] * gpb, axis=0), 0.0).astype(BF16)
        c_re = _dot(ct_re_ref[cs, :].astype(BF16), spread)
        c_im = _dot(ct_im_ref[cs, :].astype(BF16), spread)
        wc_re_ref[kb] = jnp.where(diag_out, c_re, 0.0).astype(BF16)
        wc_imn_ref[kb] = jnp.where(diag_out, -c_im, 0.0).astype(BF16)


def _zoh(a_re, a_im, log_dt, b_re, b_im, c_re, c_im):
    g, p = a_re.shape
    h = b_re.shape[-1]
    gp = g * p
    gpb = MXU_DIM // h
    nkb = g // gpb
    assert h & (h - 1) == 0 and p & (p - 1) == 0 and g % gpb == 0
    row = jax.ShapeDtypeStruct((1, gp), F32)
    wb = jax.ShapeDtypeStruct((nkb, gpb * h, gpb * p), BF16)
    wc = jax.ShapeDtypeStruct((nkb, gpb * p, gpb * h), BF16)
    return pl.pallas_call(
        functools.partial(_zoh_kernel, hch=h, p=p),
        out_shape=(row, row, wb, wb, wc, wc),
        name="s5_zoh",
    )(a_re.reshape(1, gp), a_im.reshape(1, gp), jnp.repeat(log_dt, p).reshape(1, gp),
      b_re.transpose(2, 0, 1).reshape(h, gp), b_im.transpose(2, 0, 1).reshape(h, gp),
      c_re.transpose(0, 2, 1).reshape(gp, h), c_im.transpose(0, 2, 1).reshape(gp, h))


def _ssm_in_proj(ub, wb_re_ref, wb_im_ref, bu_re, bu_im):
    nkb, ublk, sblk = wb_re_ref.shape
    for kb in range(nkb):
        ukb = ub[:, kb * ublk:(kb + 1) * ublk]
        bu_re[:, kb * sblk:(kb + 1) * sblk] = _dot(ukb, wb_re_ref[kb])
        bu_im[:, kb * sblk:(kb + 1) * sblk] = _dot(ukb, wb_im_ref[kb])


def _ssm_out_proj(h_re, h_im, wc_re_ref, wc_imn_ref):
    nkb, sblk, _ = wc_re_ref.shape
    ys = []
    for kb in range(nkb):
        hr = h_re[:, kb * sblk:(kb + 1) * sblk].astype(BF16)
        hi = h_im[:, kb * sblk:(kb + 1) * sblk].astype(BF16)
        ys.append(_dot(hr, wc_re_ref[kb]) + _dot(hi, wc_imn_ref[kb]))
    return jnp.concatenate(ys, axis=1)


def _ssm_glu(x, u, y, d_ref, wglu_ref):
    z = jax.nn.gelu(y + d_ref[...] * u).astype(BF16)
    gl = _dot(z, wglu_ref[...])
    d = x.shape[1]
    return x + gl[:, :d] * jax.nn.sigmoid(gl[:, d:])


def _ssm_sample_kernel(x_ref, g_ref, lam_re_ref, lam_im_ref, h0_re_ref, h0_im_ref, wb_re_ref, wb_im_ref,
                       wc_re_ref, wc_imn_ref, d_ref, wglu_ref,
                       out_ref, h_re_ref, h_im_ref):
    x = x_ref[...]
    u = _rms(x, g_ref[...])
    _ssm_in_proj(u.astype(BF16), wb_re_ref, wb_im_ref, h_re_ref, h_im_ref)
    lre = lam_re_ref[...]
    lim = lam_im_ref[...]
    h0r = h0_re_ref[...]
    h0i = h0_im_ref[...]
    h_re_ref[...] = lre * h0r - lim * h0i + h_re_ref[...]
    h_im_ref[...] = lre * h0i + lim * h0r + h_im_ref[...]
    y = _ssm_out_proj(h_re_ref, h_im_ref, wc_re_ref, wc_imn_ref)
    n = x.shape[0]
    out_ref[0:n, :] = _ssm_glu(x, u, y, d_ref, wglu_ref)
    out_ref[n:, :] = jnp.zeros((out_ref.shape[0] - n, out_ref.shape[1]), F32)


def _ssm_sample(x, g, lam_re, lam_im, h0_re, h0_im, wb_re, wb_im, wc_re, wc_imn, d_skip, wglu, *, pad_rows):
    n, d = x.shape
    gp = lam_re.shape[1]
    return pl.pallas_call(
        _ssm_sample_kernel,
        out_shape=[jax.ShapeDtypeStruct((pad_rows, d), F32),
                   jax.ShapeDtypeStruct((n, gp), F32), jax.ShapeDtypeStruct((n, gp), F32)],
        compiler_params=_params(),
        name="s5_sample",
    )(x, g, lam_re, lam_im, h0_re, h0_im, wb_re, wb_im, wc_re, wc_imn, d_skip, wglu)


CHUNK_LEN = 16


def _chunk_params_kernel(a_re_ref, a_im_ref, log_dt_ref, bt_re_ref, bt_im_ref, c_re_ref, c_im_ref,
                         m_ref, s_re_ref, s_im_ref, e1_re_ref, e1_im_ref, laml_re_ref, laml_im_ref, *, l):
    a_re = a_re_ref[...]
    a_im = a_im_ref[...]
    dt = jnp.exp(log_dt_ref[...])
    gb, _, p = a_re.shape
    h = c_re_ref.shape[1]
    k = lax.broadcasted_iota(jnp.int32, (gb, 2 * l, p), 1).astype(F32)
    mag = jnp.exp(a_re * dt * k)
    pw_re = mag * jnp.cos(a_im * dt * k)
    pw_im = mag * jnp.sin(a_im * dt * k)
    nr = pw_re[:, 1:2] - 1.0
    li = pw_im[:, 1:2]
    den = a_re * a_re + a_im * a_im
    qr = (nr * a_re + li * a_im) / den
    qi = (li * a_re - nr * a_im) / den
    bb_re = qr * bt_re_ref[...] - qi * bt_im_ref[...]
    bb_im = qr * bt_im_ref[...] + qi * bt_re_ref[...]
    c_re = c_re_ref[...]
    c_im = c_im_ref[...]
    laml_re_ref[...] = pw_re[:, l:l + 1]
    laml_im_ref[...] = pw_im[:, l:l + 1]
    nt = (((2,), (2,)), ((0,), (0,)))

    def contract(x, y):
        return lax.dot_general(x, y, nt, preferred_element_type=F32, precision=lax.Precision.HIGHEST)

    for t in range(l):
        rows = slice(t * h, (t + 1) * h)
        pr, pi = pw_re[:, t:t + 1], pw_im[:, t:t + 1]
        e_re = c_re * pr - c_im * pi
        e_im = c_re * pi + c_im * pr
        m_ref[:, rows, :] = contract(e_re, bb_re) - contract(e_im, bb_im)
        pr, pi = pw_re[:, t + 1:t + 2], pw_im[:, t + 1:t + 2]
        e1_re_ref[:, rows, :] = c_re * pr - c_im * pi
        e1_im_ref[:, rows, :] = c_re * pi + c_im * pr
        pr, pi = pw_re[:, l - 1 - t:l - t], pw_im[:, l - 1 - t:l - t]
        s_re_ref[:, rows, :] = pr * bb_re - pi * bb_im
        s_im_ref[:, rows, :] = pr * bb_im + pi * bb_re


def _chunk_params(a_re, a_im, log_dt, b_re, b_im, c_re, c_im, *, l, gb=8):
    g, p = a_re.shape
    h = b_re.shape[-1]
    lh = l * h
    vec = lambda v: v.reshape(g, 1, p)
    blk3 = lambda r, c: pl.BlockSpec((gb, r, c), lambda i: (i, 0, 0))
    sds = lambda r, c: jax.ShapeDtypeStruct((g, r, c), F32)
    m, s_re, s_im, e1_re, e1_im, laml_re, laml_im = pl.pallas_call(
        functools.partial(_chunk_params_kernel, l=l),
        grid=(g // gb,),
        in_specs=[blk3(1, p)] * 3 + [blk3(h, p)] * 4,
        out_specs=[blk3(lh, h), blk3(lh, p), blk3(lh, p), blk3(lh, p), blk3(lh, p), blk3(1, p), blk3(1, p)],
        out_shape=[sds(lh, h), sds(lh, p), sds(lh, p), sds(lh, p), sds(lh, p), sds(1, p), sds(1, p)],
        name="s5_chunk_params",
    )(vec(a_re), vec(a_im), jnp.broadcast_to(log_dt[:, None, None], (g, 1, p)),
      b_re.transpose(0, 2, 1), b_im.transpose(0, 2, 1), c_re, c_im)
    s_idx = np.arange(l)[:, None]
    t_idx = np.arange(l)[None, :]
    tz = m.reshape(g, l, h, h)[:, np.clip(t_idx - s_idx, 0, l - 1)]
    tz = jnp.where((t_idx >= s_idx)[None, :, :, None, None], tz, 0.0)
    tz = tz.transpose(0, 1, 4, 2, 3).reshape(g, lh, lh).astype(BF16)
    even = (jnp.arange(g) % 2 == 0)[:, None, None]
    halves = lambda v, axis: jnp.concatenate([jnp.where(even, v, 0.0), jnp.where(even, 0.0, v)], axis=axis)
    s_re = halves(s_re, 2).astype(BF16)
    s_im = halves(s_im, 2).astype(BF16)
    r_re = halves(e1_re.transpose(0, 2, 1), 1).astype(BF16)
    r_imn = halves(-e1_im.transpose(0, 2, 1), 1).astype(BF16)
    return tz, s_re, s_im, r_re, r_imn, laml_re.reshape(g // 2, 1, 2 * p), laml_im.reshape(g // 2, 1, 2 * p)


def _s5_norm_kernel(x_ref, g_ref, u_ref):
    u_ref[...] = _rms(x_ref[...], g_ref[...]).astype(BF16)


def _s5_norm(x, g, *, tm):
    rows, d = x.shape
    return pl.pallas_call(
        _s5_norm_kernel,
        grid=(rows // tm,),
        in_specs=[pl.BlockSpec((tm, d), lambda i: (i, 0)), _const_spec(g.shape)],
        out_specs=pl.BlockSpec((tm, d), lambda i: (i, 0)),
        out_shape=jax.ShapeDtypeStruct((rows, d), BF16),
        compiler_params=_params("arbitrary"),
        name="s5_norm",
    )(x, g)


def _s5_chunks_kernel(u_ref, tz_ref, s_re_ref, s_im_ref, r_re_ref, r_imn_ref, laml_re_ref, laml_im_ref,
                      y_ref, hf_re_ref, hf_im_ref, hp_re, hp_im, *, bsz):
    npair = u_ref.shape[0]
    rows = u_ref.shape[1]
    us = [u_ref[j] for j in range(npair)]
    v_re = sum(_dot(us[j], s_re_ref[j]) for j in range(npair))
    v_im = sum(_dot(us[j], s_im_ref[j]) for j in range(npair))
    lanes = v_re.shape[1]
    lr = jnp.broadcast_to(laml_re_ref[0], (bsz, lanes))
    li = jnp.broadcast_to(laml_im_ref[0], (bsz, lanes))
    h_re = jnp.zeros((bsz, lanes), F32)
    h_im = jnp.zeros((bsz, lanes), F32)
    for c in range(rows // bsz):
        rs = slice(c * bsz, (c + 1) * bsz)
        hp_re[rs, :] = h_re
        hp_im[rs, :] = h_im
        h_re, h_im = lr * h_re - li * h_im + v_re[rs], lr * h_im + li * h_re + v_im[rs]
    hf_re_ref[0] = h_re
    hf_im_ref[0] = h_im
    hpb_re = hp_re[...].astype(BF16)
    hpb_im = hp_im[...].astype(BF16)
    for j in range(npair):
        y_ref[j] = _dot(us[j], tz_ref[j]) + _dot(hpb_re, r_re_ref[j]) + _dot(hpb_im, r_imn_ref[j])


def _s5_chunks(u5, tz, s_re, s_im, r_re, r_imn, laml_re, laml_im, *, bsz):
    g, rows, lh = u5.shape
    lanes = s_re.shape[2]
    pair = lambda r, c: pl.BlockSpec((2, r, c), lambda i: (i, 0, 0))
    one = lambda r, c: pl.BlockSpec((1, r, c), lambda i: (i, 0, 0))
    return pl.pallas_call(
        functools.partial(_s5_chunks_kernel, bsz=bsz),
        grid=(g // 2,),
        in_specs=[pair(rows, lh), pair(lh, lh), pair(lh, lanes), pair(lh, lanes), pair(lanes, lh), pair(lanes, lh),
                  one(1, lanes), one(1, lanes)],
        out_specs=[pair(rows, lh), one(bsz, lanes), one(bsz, lanes)],
        out_shape=[jax.ShapeDtypeStruct((g, rows, lh), F32),
                   jax.ShapeDtypeStruct((g // 2, bsz, lanes), F32), jax.ShapeDtypeStruct((g // 2, bsz, lanes), F32)],
        scratch_shapes=[pltpu.VMEM((rows, lanes), F32), pltpu.VMEM((rows, lanes), F32)],
        compiler_params=_params("arbitrary"),
        name="s5_chunks",
    )(u5, tz, s_re, s_im, r_re, r_imn, laml_re, laml_im)


def _ssm_prompt_chunked(x, g_norm, a_re, a_im, log_dt, b_re, b_im, c_re, c_im):
    bsz, seq, d = x.shape
    grp, p = a_re.shape
    h = d // grp
    l = CHUNK_LEN
    nc = seq // l
    tz, s_re, s_im, r_re, r_imn, laml_re, laml_im = _chunk_params(a_re, a_im, log_dt, b_re, b_im, c_re, c_im, l=l)
    u = _s5_norm(x.reshape(bsz * seq, d), g_norm, tm=TOKEN_TILE)
    u5 = u.reshape(bsz, nc, l, grp, h).transpose(3, 1, 0, 2, 4).reshape(grp, nc * bsz, l * h)
    y5, hf_re, hf_im = _s5_chunks(u5, tz, s_re, s_im, r_re, r_imn, laml_re, laml_im, bsz=bsz)
    y = y5.reshape(grp, nc, bsz, l, h).transpose(2, 1, 3, 0, 4).reshape(bsz * seq, d)
    unpair = lambda v: v.reshape(grp // 2, bsz, 2, p).transpose(1, 0, 2, 3).reshape(bsz, grp * p)
    return y, unpair(hf_re), unpair(hf_im)


def _ffn_chunks(d_ff):
    step = 3 * MXU_DIM
    return [(c, min(c + step, d_ff)) for c in range(0, d_ff, step)]


def _ffn_kernel(xp_ref, yp_ref, xs_ref, gmix_ref, dskip_ref, wglu_ref,
                gffn_ref, wgu_ref, wd_ref, gkv_ref, wkv_ref, gq_ref, wq_ref,
                x2_ref, kv_ref, q_ref):
    xp = xp_ref[...]
    x1p = _ssm_glu(xp, _rms(xp, gmix_ref[...]), yp_ref[...], dskip_ref, wglu_ref)
    x = jnp.where(pl.program_id(0) < pl.num_programs(0) - 1, x1p, xs_ref[...])
    hb = _rms(x, gffn_ref[...]).astype(BF16)
    d_ff = wd_ref.shape[0]
    acc = None
    for c0, c1 in _ffn_chunks(d_ff):
        a = _dot(hb, wgu_ref[:, c0:c1])
        b = _dot(hb, wgu_ref[:, d_ff + c0:d_ff + c1])
        part = _dot((jax.nn.silu(a) * b).astype(BF16), wd_ref[c0:c1, :])
        acc = part if acc is None else acc + part
    x2 = x + acc
    x2_ref[...] = x2
    kv_ref[...] = _dot(_rms(x2, gkv_ref[...]).astype(BF16), wkv_ref[...])
    q_ref[...] = _dot(_rms(x2, gq_ref[...]).astype(BF16), wq_ref[...]).astype(BF16)


def _ffn(xp, yp, xs, gmix, dskip, wglu, gffn, wgu, wd, gkv, wkv, gq, wq):
    tm, d = xs.shape
    npt = xp.shape[0] // tm
    rows = xp.shape[0] + tm
    kvw = wkv.shape[1]
    nq = wq.shape[1]
    prompt_tile = pl.BlockSpec((tm, d), lambda i: (jnp.minimum(i, npt - 1), 0))
    single = lambda a: pl.BlockSpec(a.shape, lambda i: (0,) * a.ndim, pipeline_mode=pl.Buffered(1))
    return pl.pallas_call(
        _ffn_kernel,
        grid=(npt + 1,),
        in_specs=[prompt_tile, prompt_tile, _const_spec(xs.shape),
                  _const_spec(gmix.shape), _const_spec(dskip.shape), single(wglu),
                  _const_spec(gffn.shape), single(wgu), single(wd),
                  _const_spec(gkv.shape), single(wkv),
                  _const_spec(gq.shape), single(wq)],
        out_specs=[pl.BlockSpec((tm, d), lambda i: (i, 0)),
                   pl.BlockSpec((tm, kvw), lambda i: (i, 0)),
                   pl.BlockSpec((tm, nq), lambda i: (i, 0))],
        out_shape=[jax.ShapeDtypeStruct((rows, d), F32),
                   jax.ShapeDtypeStruct((rows, kvw), F32),
                   jax.ShapeDtypeStruct((rows, nq), BF16)],
        compiler_params=_params("arbitrary"),
        name="ffn_kv_q",
    )(xp, yp, xs, gmix, dskip, wglu, gffn, wgu, wd, gkv, wkv, gq, wq)


def _t5_bucket(dist, num_buckets):
    max_exact = num_buckets // 2
    d = jnp.maximum(dist, 0)
    large = max_exact + (jnp.log(jnp.maximum(d, 1).astype(F32) / max_exact)
                         / math.log(MAX_DISTANCE / max_exact) * (num_buckets - max_exact)).astype(jnp.int32)
    large = jnp.minimum(large, num_buckets - 1)
    return jnp.where(d < max_exact, d, large)


def _bias_kernel(bm_ref, bs_ref, rb_ref, bias_ref, bias_s_ref, *, kvh, rep):
    nb, nh = rb_ref.shape
    bm = bm_ref[...]
    bs = bs_ref[...]

    def lookup(buckets, h):
        def body(k, acc):
            return jnp.where(buckets == k, rb_ref[k, h], acc)
        return lax.fori_loop(0, nb, body, jnp.zeros(buckets.shape, F32))

    w = bm.shape[0]
    for g in range(kvh):
        for r in range(rep):
            bias_ref[g, :, r * w:(r + 1) * w] = lookup(bm, g * rep + r) * math.log2(math.e)
            bias_s_ref[g, r:r + 1, :] = lookup(bs, g * rep + r)


def _bias_tables(rel_bias, window, kvh, rep):
    nb, nh = rel_bias.shape
    key = jnp.arange(window)[:, None]
    qry = jnp.arange(window)[None, :]
    bm = _t5_bucket((qry - key) % window, nb).astype(jnp.int32)
    bs = _t5_bucket(window - 1 - qry, nb).astype(jnp.int32)
    kern = functools.partial(_bias_kernel, kvh=kvh, rep=rep)
    return pl.pallas_call(
        kern,
        in_specs=[pl.BlockSpec(memory_space=pltpu.VMEM), pl.BlockSpec(memory_space=pltpu.VMEM),
                  pl.BlockSpec(memory_space=pltpu.SMEM)],
        out_shape=[jax.ShapeDtypeStruct((kvh, window, rep * window), F32),
                   jax.ShapeDtypeStruct((kvh, rep, window), F32)],
        name="t5_bias",
    )(bm, bs, rel_bias)


def _attn_prompt_kernel(q_ref, kvc_ref, kvp_ref, bias_ref, sink_ref, ot_ref, *, kvh, rep, hd, scale):
    w = kvp_ref.shape[0]
    kv_all = kvc_ref[...]
    for sub in range(q_ref.shape[0] // w):
        rows = slice(sub * w, (sub + 1) * w)
        kvp = kvp_ref[...] if sub == 0 else kv_all[(sub - 1) * w:sub * w]
        first = pl.program_id(1) == 0 if sub == 0 else None
        _attn_prompt_block(q_ref[rows, :], kv_all[rows], kvp, first, bias_ref, sink_ref, ot_ref, sub * w,
                           kvh=kvh, rep=rep, hd=hd, scale=scale)


def _attn_prompt_block(q, kvc, kvp, first, bias_ref, sink_ref, ot_ref, col0, *, kvh, rep, hd, scale):
    w = q.shape[0]
    kvw = kvh * hd
    lane = lax.broadcasted_iota(jnp.int32, (1, LANES), 1)
    key = lax.broadcasted_iota(jnp.int32, (w, rep * w), 0)
    qry = lax.broadcasted_iota(jnp.int32, (w, rep * w), 1) % w
    upper = key > qry
    mask_add = None if first is None else jnp.where(jnp.logical_and(upper, first), NEG_INF, 0.0)
    log2e = math.log2(math.e)
    heads_per_blk = LANES // hd
    for p in range(kvw // LANES):
        cs = slice(p * LANES, (p + 1) * LANES)
        kc, kp = kvc[:, cs], kvp[:, cs]
        vs = slice(kvw + p * LANES, kvw + (p + 1) * LANES)
        vb = jnp.concatenate([kvp[:, vs], kvc[:, vs]], axis=0).astype(BF16)
        qg = jnp.concatenate([q[:, r * kvw + p * LANES:r * kvw + (p + 1) * LANES] for r in range(rep)], axis=0)
        halves = []
        for half in range(heads_per_blk):
            g = p * heads_per_blk + half
            lmask = jnp.logical_and(lane >= half * hd, lane < (half + 1) * hd)
            kb = jnp.concatenate([jnp.where(lmask, kp, 0.0), jnp.where(lmask, kc, 0.0)], axis=0).astype(BF16)
            s = lax.dot_general(kb, qg, (((1,), (1,)), ((), ())), preferred_element_type=F32)
            bias = bias_ref[g] if first is None else bias_ref[g] + mask_add
            sc = jnp.where(upper, s[:w], s[w:]) * (scale * log2e) + bias
            sink = jnp.concatenate([jnp.full((1, w), sink_ref[g * rep + r] * log2e, F32) for r in range(rep)],
                                   axis=1)
            m = jnp.maximum(jnp.max(sc, axis=0, keepdims=True), sink)
            pe = jnp.exp2(sc - m)
            denom = jnp.sum(pe, axis=0, keepdims=True) + jnp.exp2(sink - m)
            pcat = jnp.concatenate([jnp.where(upper, pe, 0.0), jnp.where(upper, 0.0, pe)], axis=0).astype(BF16)
            og = lax.dot_general(vb, pcat, (((0,), (0,)), ((), ())), preferred_element_type=F32)
            halves.append((og / denom)[half * hd:(half + 1) * hd])
        o_blk = jnp.concatenate(halves, axis=0).astype(BF16)
        for r in range(rep):
            ot_ref[r * kvw + p * LANES:r * kvw + (p + 1) * LANES, col0:col0 + w] = o_blk[:, r * w:(r + 1) * w]


def _attn_prompt(q, kv, bias, sinks, *, bsz, nblk, window, kvh, rep, hd):
    rows, nq = q.shape
    kvw2 = kv.shape[1]
    nsub = next(n for n in (4, 2, 1) if nblk % n == 0 and rows % (n * window) == 0)
    nstep = nblk // nsub
    kern = functools.partial(_attn_prompt_kernel, kvh=kvh, rep=rep, hd=hd, scale=1.0 / math.sqrt(hd))
    return pl.pallas_call(
        kern,
        grid=(bsz, nstep),
        in_specs=[pl.BlockSpec((nsub * window, nq), lambda b, i: (b * nstep + i, 0)),
                  pl.BlockSpec((nsub * window, kvw2), lambda b, i: (b * nstep + i, 0)),
                  pl.BlockSpec((window, kvw2), lambda b, i: (b * nblk + jnp.maximum(nsub * i - 1, 0), 0)),
                  _const_spec(bias.shape),
                  pl.BlockSpec(memory_space=pltpu.SMEM)],
        out_specs=pl.BlockSpec((nq, nsub * window), lambda b, i: (0, b * nstep + i)),
        out_shape=jax.ShapeDtypeStruct((nq, rows), BF16),
        compiler_params=_params("arbitrary", "arbitrary"),
        name="swa_prompt",
    )(q, kv, kv, bias, sinks)


def _attn_sample_kernel(q_ref, kv_ref, ck_ref, cv_ref, bias_ref, sink_ref, o_ref, nk_ref, nv_ref,
                        *, kvh, rep, hd, scale):
    nb, w, kvw = ck_ref.shape
    kv = kv_ref[...]
    lane = lax.broadcasted_iota(jnp.int32, (nb, rep, kvw), 2)

    def shifted(c_ref, n_ref, new):
        flat = c_ref[...].reshape(nb * w, kvw)
        n_ref[...] = pltpu.roll(flat, nb * w - 1, axis=0).reshape(nb, w, kvw)
        n_ref[:, w - 1:w, :] = new
        return n_ref[...].astype(BF16)

    nkb = shifted(ck_ref, nk_ref, kv[:, :, :kvw])
    nvb = shifted(cv_ref, nv_ref, kv[:, :, kvw:])
    q = q_ref[...].astype(F32)
    o = jnp.zeros((nb, rep, kvw), F32)
    for g in range(kvh):
        lmask = jnp.logical_and(lane >= g * hd, lane < (g + 1) * hd)
        qg = jnp.where(lmask, q, 0.0).astype(BF16)
        s = jnp.einsum("nrc,njc->nrj", qg, nkb, preferred_element_type=F32)
        sc = s * scale + bias_ref[g][None]
        sink = sink_ref[g][None]
        m = jnp.maximum(jnp.max(sc, axis=-1, keepdims=True), sink)
        pe = jnp.exp(sc - m)
        probs = pe / (jnp.sum(pe, axis=-1, keepdims=True) + jnp.exp(sink - m))
        og = jnp.einsum("nrj,njc->nrc", probs.astype(BF16), nvb, preferred_element_type=F32)
        o = jnp.where(lmask, og, o)
    o_ref[...] = o.astype(BF16)


def _attn_sample(q3, kv3, ck, cv, bias_s, sink_s, *, nb, kvh, rep, hd):
    n, w, kvw = ck.shape
    kern = functools.partial(_attn_sample_kernel, kvh=kvh, rep=rep, hd=hd, scale=1.0 / math.sqrt(hd))
    cache_spec = pl.BlockSpec((nb, w, kvw), lambda i: (i, 0, 0))
    return pl.pallas_call(
        kern,
        grid=(n // nb,),
        in_specs=[pl.BlockSpec((nb, rep, kvw), lambda i: (i, 0, 0)),
                  pl.BlockSpec((nb, 1, 2 * kvw), lambda i: (i, 0, 0)),
                  cache_spec, cache_spec,
                  _const_spec(bias_s.shape), _const_spec(sink_s.shape)],
        out_specs=[pl.BlockSpec((nb, rep, kvw), lambda i: (i, 0, 0)), cache_spec, cache_spec],
        out_shape=[jax.ShapeDtypeStruct((n, rep, kvw), BF16),
                   jax.ShapeDtypeStruct((n, w, kvw), F32), jax.ShapeDtypeStruct((n, w, kvw), F32)],
        compiler_params=_params("arbitrary"),
        name="swa_sample",
    )(q3, kv3, ck, cv, bias_s, sink_s)


CHUNK = 16
SUB = 256
UP_ROW_BLOCK = 1024
DOWN_ROW_BLOCK = 1024
TOKEN_TILE = 512
MOE_TILE = 512


def _chunk_copy(src, src_row, dst, dst_row, sem):
    return pltpu.make_async_copy(src.at[pl.ds(pl.multiple_of(src_row, CHUNK), CHUNK), :],
                                 dst.at[pl.ds(pl.multiple_of(dst_row, CHUNK), CHUNK), :], sem)


def _moe_route_kernel(x2_ref, ot_ref, wo_ref, gffn_ref, wr_ref, lst_ref, ust_ref,
                      x3_ref, cm_ref, tmeta_ref, xs_hbm,
                      comp_s, carry_s, zero_s, base_v, pend_sm, sem, *, ne, cap, n_valid):
    i = pl.program_id(0)
    nt = pl.num_programs(0)
    tm = x2_ref.shape[0]
    w = comp_s.shape[1]
    slot = lax.rem(i, 2)

    @pl.when(i == 0)
    def _():
        base_v[...] = jnp.zeros_like(base_v)
        zero_s[...] = jnp.zeros_like(zero_s)
        carry_s[...] = jnp.zeros_like(carry_s)

    def drain(n):
        def body(c, carry):
            _chunk_copy(zero_s, 0, xs_hbm, 0, sem).wait()
            return carry
        lax.fori_loop(0, n, body, 0)

    @pl.when(i > 0)
    def _():
        drain(pend_sm[0])

    x3 = x2_ref[...] + lax.dot_general(ot_ref[...], wo_ref[...], (((0,), (0,)), ((), ())),
                                       preferred_element_type=F32)
    x3_ref[...] = x3
    hb = _rms(x3, gffn_ref[...]).astype(BF16)

    lane = lax.broadcasted_iota(jnp.int32, (tm, LANES), 1)
    logits = jnp.where(lane < ne, _dot(hb, wr_ref[...]), -jnp.inf)
    m1 = jnp.max(logits, axis=-1, keepdims=True)
    i1 = jnp.min(jnp.where(logits == m1, lane, LANES), axis=-1, keepdims=True)
    rest = jnp.where(lane == i1, -jnp.inf, logits)
    m2 = jnp.max(rest, axis=-1, keepdims=True)
    i2 = jnp.min(jnp.where(rest == m2, lane, LANES), axis=-1, keepdims=True)
    e2 = jnp.exp(m2 - m1)
    g1 = 1.0 / (1.0 + e2)
    g2 = e2 / (1.0 + e2)

    live = i * tm + lax.broadcasted_iota(jnp.int32, (tm, 1), 0) < n_valid
    sel = jnp.where(jnp.logical_and(live, jnp.logical_or(lane == i1, lane == i2)), 1.0, 0.0)
    rank = _dot(lst_ref[...], sel.astype(BF16))
    cnt = jnp.sum(sel, axis=0, keepdims=True)
    fill = base_v[...]
    rem = fill - jnp.floor(fill / CHUNK) * CHUNK
    cpad = jnp.floor((rem + cnt + (CHUNK - 1)) / CHUNK) * CHUNK
    loff = _dot(jnp.broadcast_to(cpad, (8, LANES)).astype(BF16), ust_ref[...])[0:1]
    dest = loff + rem + rank
    ld1 = jnp.where(live, jnp.sum(jnp.where(lane == i1, dest, 0.0), axis=-1, keepdims=True), -1.0)
    ld2 = jnp.where(live, jnp.sum(jnp.where(lane == i2, dest, 0.0), axis=-1, keepdims=True), -1.0)
    cm = jnp.where(lane == 0, ld1, jnp.where(lane == 1, ld2, jnp.where(lane == 2, g1, jnp.where(lane == 3, g2, 0.0))))
    cm_ref[...] = cm

    rm = cm.T
    rowi = lax.broadcasted_iota(jnp.int32, (w, tm), 0).astype(F32)
    place = (jnp.where(rowi == rm[0:1], 1.0, 0.0) + jnp.where(rowi == rm[1:2], 1.0, 0.0)).astype(BF16)
    comp_s[slot] = _dot(place, hb).astype(BF16)

    base = fill - rem
    srow = lax.broadcasted_iota(jnp.int32, (8, LANES), 0)
    tmeta_ref[...] = jnp.where(srow == 0, cpad, jnp.where(srow == 1, base, jnp.where(srow == 2, fill + cnt, 0.0))
                               ).astype(jnp.int32)
    base_v[...] = fill + cnt

    cpad_i = cpad.astype(jnp.int32)
    loff_i = loff.astype(jnp.int32)
    base_i = base.astype(jnp.int32)
    tail_i = (rem + cnt - jnp.floor((rem + cnt) / CHUNK) * CHUNK).astype(jnp.int32)
    total = 0
    tails = []
    for e in range(ne):
        n_e = cpad_i[0, e]
        src0 = loff_i[0, e]
        dst0 = base_i[0, e] + e * cap

        @pl.when(n_e > 0)
        def _(e=e, n_e=n_e, src0=src0, partial=tail_i[0, e] > 0):
            head = pl.ds(pl.multiple_of(src0, CHUNK), CHUNK)
            comp_s[slot, head, :] = comp_s[slot, head, :] + carry_s[e]
            last = comp_s[slot, pl.ds(pl.multiple_of(src0 + n_e - CHUNK, CHUNK), CHUNK), :]
            carry_s[e] = jnp.where(partial, last, jnp.zeros_like(last))

        def start(c, carry, src0=src0, dst0=dst0):
            _chunk_copy(comp_s.at[slot], src0 + c * CHUNK, xs_hbm, dst0 + c * CHUNK, sem).start()
            return carry

        nchunks = lax.shift_right_logical(n_e, int(math.log2(CHUNK)))
        lax.fori_loop(0, nchunks, start, 0)
        total = total + nchunks
        tails.append(dst0 + n_e)
    pend_sm[0] = total

    @pl.when(i == nt - 1)
    def _():
        nz_total = 0
        for e in range(ne):
            end = tails[e]
            nz = lax.shift_right_logical(lax.rem(SUB - lax.rem(end, SUB), SUB), int(math.log2(CHUNK)))

            def zstart(c, carry, end=end):
                _chunk_copy(zero_s, 0, xs_hbm, end + c * CHUNK, sem).start()
                return carry

            lax.fori_loop(0, nz, zstart, 0)
            nz_total = nz_total + nz
        drain(total + nz_total)


def _moe_route(x2, ot, wo, gffn, wr_pad, *, tm, ne, cap, n_valid):
    rows, d = x2.shape
    nt = rows // tm
    w = _staging_rows(tm, ne)
    lst = jnp.asarray(np.tril(np.ones((tm, tm), np.float32), -1), BF16)
    ust = jnp.asarray(np.triu(np.ones((LANES, LANES), np.float32), 1), BF16)
    kern = functools.partial(_moe_route_kernel, ne=ne, cap=cap, n_valid=n_valid)
    return pl.pallas_call(
        kern,
        grid=(nt,),
        in_specs=[pl.BlockSpec((tm, d), lambda i: (i, 0)),
                  pl.BlockSpec((ot.shape[0], tm), lambda i: (0, i)),
                  _const_spec(wo.shape), _const_spec(gffn.shape), _const_spec(wr_pad.shape),
                  _const_spec(lst.shape), _const_spec(ust.shape)],
        out_specs=[pl.BlockSpec((tm, d), lambda i: (i, 0)),
                   pl.BlockSpec((tm, LANES), lambda i: (i, 0)),
                   pl.BlockSpec((8, LANES), lambda i: (i, 0)),
                   pl.BlockSpec(memory_space=pl.ANY)],
        out_shape=[jax.ShapeDtypeStruct((rows, d), F32),
                   jax.ShapeDtypeStruct((rows, LANES), F32),
                   jax.ShapeDtypeStruct((nt * 8, LANES), jnp.int32),
                   jax.ShapeDtypeStruct((ne * cap, d), BF16)],
        scratch_shapes=[pltpu.VMEM((2, w, d), BF16), pltpu.VMEM((ne, CHUNK, d), BF16), pltpu.VMEM((CHUNK, d), BF16),
                        pltpu.VMEM((1, LANES), F32), pltpu.SMEM((1,), jnp.int32),
                        pltpu.SemaphoreType.DMA(())],
        compiler_params=_params("arbitrary"),
        name="moe_route",
    )(x2, ot, wo, gffn, wr_pad, lst, ust)


def _staging_rows(tm, ne):
    return -(-(TOP_K * tm + 2 * ne * (CHUNK - 1)) // LANES) * LANES


def _max_blocks(rows, ne, row_block):
    return TOP_K * rows // row_block + ne


def _expert_steps(nrows, nf, max_blocks, row_block):
    ne = nrows.shape[0]
    nblk = (nrows + row_block - 1) // row_block
    cum = jnp.cumsum(nblk) * nf
    total = cum[-1]
    s = jnp.minimum(jnp.arange(max_blocks * nf, dtype=jnp.int32), total - 1)
    e = jnp.minimum(jnp.sum((s[:, None] >= cum[None, :]).astype(jnp.int32), axis=1), ne - 1)
    nb_e = nblk[e]
    within = s - (cum[e] - nb_e * nf)
    f = within // nb_e
    pos = within - f * nb_e
    r = (pos + nb_e - 1) % nb_e
    nsub = jnp.clip((nrows[e] - r * row_block + SUB - 1) // SUB, 0, row_block // SUB)
    first = (pos == 0).astype(jnp.int32)
    return (e, f.astype(jnp.int32), r.astype(jnp.int32), first, nsub.astype(jnp.int32),
            jnp.reshape(total, (1,)).astype(jnp.int32))


def _moe_up_kernel(e_ref, f_ref, r_ref, first_ref, nsub_ref, n_ref, xs_ref, wg_ref, wu_ref, act_ref, wgb_s, wub_s):
    s = pl.program_id(0)

    @pl.when(s < n_ref[0])
    def _():
        @pl.when(first_ref[s] == 1)
        def _():
            wgb_s[...] = wg_ref[...].astype(BF16)
            wub_s[...] = wu_ref[...].astype(BF16)

        def body(j, carry):
            rows = pl.ds(pl.multiple_of(j * SUB, SUB), SUB)
            xsb = xs_ref[rows, :]
            a = _dot(xsb, wgb_s[...])
            b = _dot(xsb, wub_s[...])
            act_ref[rows, :] = (jax.nn.silu(a) * b).astype(BF16)
            return carry

        lax.fori_loop(0, nsub_ref[s], body, 0)


def _moe_up(steps, xs, wgu, *, fc, cap, row_block):
    ne, d, dff2 = wgu.shape
    dff = dff2 // 2
    nf = dff // fc
    cb = cap // row_block
    grid_spec = pltpu.PrefetchScalarGridSpec(
        num_scalar_prefetch=6,
        grid=(steps[0].shape[0],),
        in_specs=[pl.BlockSpec((row_block, d), lambda s, e, f, r, *_: (e[s] * cb + r[s], 0)),
                  pl.BlockSpec((None, d, fc), lambda s, e, f, r, *_: (e[s], 0, f[s])),
                  pl.BlockSpec((None, d, fc), lambda s, e, f, r, *_: (e[s], 0, nf + f[s]))],
        out_specs=pl.BlockSpec((row_block, fc), lambda s, e, f, r, *_: (e[s] * cb + r[s], f[s])),
        scratch_shapes=[pltpu.VMEM((d, fc), BF16), pltpu.VMEM((d, fc), BF16)])
    return pl.pallas_call(
        _moe_up_kernel,
        grid_spec=grid_spec,
        out_shape=jax.ShapeDtypeStruct((ne * cap, dff), BF16),
        compiler_params=_params("arbitrary"),
        name="moe_up",
    )(*steps, xs, wgu, wgu)


def _moe_down_kernel(e_ref, f_ref, r_ref, first_ref, nsub_ref, n_ref, act_ref, wd_ref, ys_ref, wdb_s):
    s = pl.program_id(0)

    @pl.when(s < n_ref[0])
    def _():
        @pl.when(first_ref[s] == 1)
        def _():
            wdb_s[...] = wd_ref[...].astype(BF16)

        def body(j, carry):
            rows = pl.ds(pl.multiple_of(j * SUB, SUB), SUB)
            ys_ref[rows, :] = _dot(act_ref[rows, :], wdb_s[...]).astype(BF16)
            return carry

        lax.fori_loop(0, nsub_ref[s], body, 0)


def _moe_down(steps, act, wd, *, cap, row_block):
    ne, dff, d = wd.shape
    cb = cap // row_block
    grid_spec = pltpu.PrefetchScalarGridSpec(
        num_scalar_prefetch=6,
        grid=(steps[0].shape[0],),
        in_specs=[pl.BlockSpec((row_block, dff), lambda s, e, f, r, *_: (e[s] * cb + r[s], 0)),
                  pl.BlockSpec((None, dff, d), lambda s, e, f, r, *_: (e[s], 0, 0))],
        out_specs=pl.BlockSpec((row_block, d), lambda s, e, f, r, *_: (e[s] * cb + r[s], 0)),
        scratch_shapes=[pltpu.VMEM((dff, d), BF16)])
    return pl.pallas_call(
        _moe_down_kernel,
        grid_spec=grid_spec,
        out_shape=jax.ShapeDtypeStruct((ne * cap, d), BF16),
        compiler_params=_params("arbitrary", vmem=2 * (dff * d * 4 + row_block * (dff + d) * 2) + dff * d * 2
                                + 4 * SUB * d * 4),
        name="moe_down",
    )(*steps, act, wd)


def _moe_combine_kernel(cpad_ref, seg_ref, x3_ref, cm_ref, gfin_ref, ys_hbm, yp_ref, ys_ref, yloc_s, sem,
                        *, ne, npt):
    i = pl.program_id(0)
    nt = pl.num_programs(0)
    tm = x3_ref.shape[0]
    w = yloc_s.shape[1]
    slot = lax.rem(i, 2)
    shift = int(math.log2(CHUNK))

    def issue(t, sl):
        off = 0
        for e in range(ne):
            n_e = cpad_ref[t * ne + e]
            src0 = seg_ref[t * ne + e]

            def start(c, carry, src0=src0, off=off):
                _chunk_copy(ys_hbm, src0 + c * CHUNK, yloc_s.at[sl], off + c * CHUNK, sem.at[sl]).start()
                return carry

            lax.fori_loop(0, lax.shift_right_logical(n_e, shift), start, 0)
            off = off + n_e

    @pl.when(i == 0)
    def _():
        yloc_s[...] = jnp.zeros_like(yloc_s)
        issue(0, 0)

    @pl.when(i + 1 < nt)
    def _():
        issue(i + 1, 1 - slot)

    total = 0
    for e in range(ne):
        total = total + lax.shift_right_logical(cpad_ref[i * ne + e], shift)

    def wait(c, carry):
        _chunk_copy(ys_hbm, 0, yloc_s.at[slot], 0, sem.at[slot]).wait()
        return carry

    lax.fori_loop(0, total, wait, 0)

    cm = cm_ref[...]
    col = lax.broadcasted_iota(jnp.int32, (tm, w), 1).astype(F32)
    yl = yloc_s[slot]
    y1 = _dot(jnp.where(col == cm[:, 0:1], 1.0, 0.0).astype(BF16), yl)
    y2 = _dot(jnp.where(col == cm[:, 1:2], 1.0, 0.0).astype(BF16), yl)
    y = _rms(x3_ref[...] + cm[:, 2:3] * y1 + cm[:, 3:4] * y2, gfin_ref[...])

    @pl.when(i < npt)
    def _():
        yp_ref[...] = y

    @pl.when(i >= npt)
    def _():
        ys_ref[...] = y


def _moe_combine(cpad, seg, x3, cm, gfin, ys, *, tm, ne, sample_rows):
    rows, d = x3.shape
    nt = rows // tm
    npt = (rows - sample_rows) // tm
    w = _staging_rows(tm, ne)
    kern = functools.partial(_moe_combine_kernel, ne=ne, npt=npt)
    grid_spec = pltpu.PrefetchScalarGridSpec(
        num_scalar_prefetch=2,
        grid=(nt,),
        in_specs=[pl.BlockSpec((tm, d), lambda i, *_: (i, 0)),
                  pl.BlockSpec((tm, LANES), lambda i, *_: (i, 0)),
                  pl.BlockSpec(gfin.shape, lambda i, *_: (0, 0)),
                  pl.BlockSpec(memory_space=pl.ANY)],
        out_specs=[pl.BlockSpec((tm, d), lambda i, *_: (jnp.minimum(i, npt - 1), 0)),
                   pl.BlockSpec((tm, d), lambda i, *_: (jnp.maximum(i - npt, 0), 0))],
        scratch_shapes=[pltpu.VMEM((2, w, d), BF16), pltpu.SemaphoreType.DMA((2,))])
    return pl.pallas_call(
        kern,
        grid_spec=grid_spec,
        out_shape=[jax.ShapeDtypeStruct((npt * tm, d), F32), jax.ShapeDtypeStruct((sample_rows, d), F32)],
        compiler_params=_params("arbitrary"),
        name="moe_combine",
    )(cpad, seg, x3, cm, gfin, ys)


def _moe(x2, ot, wo, gffn, wr, wgu, wd, gfin, *, tm, fc, n_valid, sample_rows):
    rows, d = x2.shape
    ne = wr.shape[1]
    nt = rows // tm
    nf = wgu.shape[2] // 2 // fc
    cap = -(-(rows + SUB) // UP_ROW_BLOCK) * UP_ROW_BLOCK
    wr_pad = jnp.pad(wr, ((0, 0), (0, LANES - ne))).astype(BF16)
    x3, cm, tmeta, xs = _moe_route(x2, ot, wo, gffn, wr_pad, tm=tm, ne=ne, cap=cap, n_valid=n_valid)
    tmeta = tmeta.reshape(nt, 8, LANES)[:, :, :ne]
    cpad = tmeta[:, 0, :].reshape(nt * ne)
    seg = (tmeta[:, 1, :] + jnp.arange(ne, dtype=jnp.int32)[None, :] * cap).reshape(nt * ne)
    nrows = tmeta[nt - 1, 2, :]
    up_steps = _expert_steps(nrows, nf, _max_blocks(rows, ne, UP_ROW_BLOCK), UP_ROW_BLOCK)
    act = _moe_up(up_steps, xs, wgu, fc=fc, cap=cap, row_block=UP_ROW_BLOCK)
    down_steps = _expert_steps(nrows, 1, _max_blocks(rows, ne, DOWN_ROW_BLOCK), DOWN_ROW_BLOCK)
    ys = _moe_down(down_steps, act, wd, cap=cap, row_block=DOWN_ROW_BLOCK)
    return _moe_combine(cpad, seg, x3, cm, gfin, ys, tm=tm, ne=ne, sample_rows=sample_rows)


def kernel(x_prompt, x_sample, state_ssm_re, state_ssm_im, cache_k_win, cache_v_win, g_mix, g_ffn, g_kv, g_final, ssm_a_re, ssm_a_im, ssm_log_dt, ssm_b_re, ssm_b_im, ssm_c_re, ssm_c_im, ssm_d, w_glu, w_kv, w_q, w_o, attn_sinks, rel_bias, w_ffn_gate_up, w_ffn_down, w_router, w_exp_gate_up, w_exp_down):
    bsz, seq, d = x_prompt.shape
    ns, dec_seq, _ = x_sample.shape
    assert dec_seq == 1 and g_mix.shape[0] == 2 and ssm_a_re.shape[0] == 1 and w_q.shape[0] == 1
    _, g, p = ssm_a_re.shape
    gp = g * p
    window, kvh, hd = cache_k_win.shape[1:]
    kvw = kvh * hd
    nh = attn_sinks.shape[1]
    rep = nh // kvh
    nq = nh * hd
    assert bsz == 8 and ns % bsz == 0 and seq % window == 0 and LANES % hd == 0

    lam_re, lam_im, wb_re, wb_im, wc_re, wc_imn = _zoh(ssm_a_re[0], ssm_a_im[0], ssm_log_dt[0],
                                                        ssm_b_re[0], ssm_b_im[0], ssm_c_re[0], ssm_c_im[0])
    wglu = w_glu[0].astype(BF16)
    d_skip = ssm_d[0].reshape(1, d)
    wgu = w_ffn_gate_up[0].astype(BF16)
    wd = w_ffn_down[0].astype(BF16)
    wkv = w_kv.astype(BF16)
    wq = w_q[0].reshape(d, kvh, rep, hd).transpose(0, 2, 1, 3).reshape(d, nq).astype(BF16)
    wo = w_o[0].reshape(kvh, rep, hd, d).transpose(1, 0, 2, 3).reshape(nq, d).astype(BF16)
    bias, bias_s = _bias_tables(rel_bias, window, kvh, rep)
    sinks = attn_sinks[0]
    sink_s = sinks.reshape(kvh, rep, 1)

    tm = TOKEN_TILE
    npr = seq * bsz
    assert npr % tm == 0 and ns <= tm

    y_ssm, st_re, st_im = _ssm_prompt_chunked(x_prompt, g_mix[0:1], ssm_a_re[0], ssm_a_im[0], ssm_log_dt[0],
                                              ssm_b_re[0], ssm_b_im[0], ssm_c_re[0], ssm_c_im[0])
    x1_s, hs_re, hs_im = _ssm_sample(x_sample.reshape(ns, d), g_mix[0:1], lam_re, lam_im,
                                     state_ssm_re[0].reshape(ns, gp), state_ssm_im[0].reshape(ns, gp),
                                     wb_re, wb_im, wc_re, wc_imn, d_skip, wglu, pad_rows=tm)

    x2, kv, q = _ffn(x_prompt.reshape(npr, d), y_ssm, x1_s, g_mix[0:1], d_skip, wglu,
                     g_ffn[0:1], wgu, wd, g_kv.reshape(1, d), wkv, g_mix[1:2], wq)

    ot = _attn_prompt(q, kv, bias, sinks, bsz=bsz, nblk=seq // window, window=window, kvh=kvh, rep=rep, hd=hd)
    o_s, nk_s, nv_s = _attn_sample(q[npr:npr + ns].reshape(ns, rep, kvw), kv[npr:npr + ns].reshape(ns, 1, 2 * kvw),
                                   cache_k_win.reshape(ns, window, kvw), cache_v_win.reshape(ns, window, kvw),
                                   bias_s, sink_s, nb=16, kvh=kvh, rep=rep, hd=hd)
    ot = lax.dynamic_update_slice(ot, jnp.pad(o_s.reshape(ns, nq).T, ((0, 0), (0, tm - ns))), (0, npr))

    y_p, y_s = _moe(x2, ot, wo, g_ffn[1:2], w_router[0], w_exp_gate_up[0], w_exp_down[0], g_final.reshape(1, d),
                    tm=MOE_TILE, fc=w_exp_down.shape[2] // 2, n_valid=npr + ns, sample_rows=tm)

    y_prompt = y_p.reshape(bsz, seq, d)
    y_sample = y_s[:ns].reshape(ns, 1, d)
    kv_tail = jnp.stack([kv[(b + 1) * seq - window:(b + 1) * seq] for b in range(bsz)])
    kv_tail = kv_tail.reshape(bsz, window, 2, kvh, hd).transpose(2, 0, 1, 3, 4)
    return (y_prompt, y_sample,
            st_re.reshape(1, bsz, g, p), st_im.reshape(1, bsz, g, p), kv_tail[0], kv_tail[1],
            hs_re.reshape(1, ns, g, p), hs_im.reshape(1, ns, g, p),
            nk_s.reshape(ns, window, kvh, hd), nv_s.reshape(ns, window, kvh, hd))
```

```python
import functools
import math

import numpy as np
import jax
import jax.numpy as jnp
from jax import lax
from jax.experimental import pallas as pl
from jax.experimental.pallas import tpu as pltpu

F32 = jnp.float32
BF16 = jnp.bfloat16

EPS = 1e-6
NEG_INF = -1e30
TOP_K = 2
MAX_DISTANCE = 128
MXU_DIM = 256
LANES = 128
VMEM_LIMIT_BYTES = 56 * 1024 * 1024


def _dot(a, b):
    return jnp.dot(a, b, preferred_element_type=F32)


def _rms(x, g):
    return x * lax.rsqrt(jnp.mean(x * x, axis=-1, keepdims=True) + EPS) * g


def _const_spec(shape):
    nd = len(shape)
    return pl.BlockSpec(shape, lambda *_: (0,) * nd)


def _params(*sem, vmem=VMEM_LIMIT_BYTES):
    return pltpu.CompilerParams(dimension_semantics=sem, vmem_limit_bytes=vmem)


def _zoh_kernel(a_re_ref, a_im_ref, log_dt_ref, b_re_ref, b_im_ref, ct_re_ref, ct_im_ref,
                lam_re_ref, lam_im_ref, wb_re_ref, wb_im_ref, wc_re_ref, wc_imn_ref, *, hch, p):
    a_re = a_re_ref[...]
    a_im = a_im_ref[...]
    dt = jnp.exp(log_dt_ref[...])
    mag = jnp.exp(a_re * dt)
    lr = mag * jnp.cos(a_im * dt)
    li = mag * jnp.sin(a_im * dt)
    lam_re_ref[...] = lr
    lam_im_ref[...] = li
    nr = lr - 1.0
    den = a_re * a_re + a_im * a_im
    qr = (nr * a_re + li * a_im) / den
    qi = (li * a_re - nr * a_im) / den
    b_re = b_re_ref[...]
    b_im = b_im_ref[...]
    bb_re = qr * b_re - qi * b_im
    bb_im = qr * b_im + qi * b_re

    nkb, ublk, sblk = wb_re_ref.shape
    gpb = ublk // hch
    sh_h, sh_p = int(math.log2(hch)), int(math.log2(p))
    row = lax.broadcasted_iota(jnp.int32, (ublk, sblk), 0)
    col = lax.broadcasted_iota(jnp.int32, (ublk, sblk), 1)
    diag_in = lax.shift_right_logical(row, sh_h) == lax.shift_right_logical(col, sh_p)
    row = lax.broadcasted_iota(jnp.int32, (sblk, ublk), 0)
    col = lax.broadcasted_iota(jnp.int32, (sblk, ublk), 1)
    diag_out = lax.shift_right_logical(row, sh_p) == lax.shift_right_logical(col, sh_h)
    spread = jnp.where(lax.broadcasted_iota(jnp.int32, (hch, ublk), 0)
                       == jnp.bitwise_and(lax.broadcasted_iota(jnp.int32, (hch, ublk), 1), hch - 1),
                       1.0, 0.0).astype(BF16)
    for kb in range(nkb):
        cs = slice(kb * sblk, (kb + 1) * sblk)
        wb_re_ref[kb] = jnp.where(diag_in, jnp.concatenate([bb_re[:, cs]] * gpb, axis=0), 0.0).astype(BF16)
        wb_im_ref[kb] = jnp.where(diag_in, jnp.concatenate([bb_im[:, cs]] * gpb, axis=0), 0.0).astype(BF16)
        c_re = _dot(ct_re_ref[cs, :].astype(BF16), spread)
        c_im = _dot(ct_im_ref[cs, :].astype(BF16), spread)
        wc_re_ref[kb] = jnp.where(diag_out, c_re, 0.0).astype(BF16)
        wc_imn_ref[kb] = jnp.where(diag_out, -c_im, 0.0).astype(BF16)


def _zoh(a_re, a_im, log_dt, b_re, b_im, c_re, c_im):
    g, p = a_re.shape
    h = b_re.shape[-1]
    gp = g * p
    gpb = MXU_DIM // h
    nkb = g // gpb
    assert h & (h - 1) == 0 and p & (p - 1) == 0 and g % gpb == 0
    row = jax.ShapeDtypeStruct((1, gp), F32)
    wb = jax.ShapeDtypeStruct((nkb, gpb * h, gpb * p), BF16)
    wc = jax.ShapeDtypeStruct((nkb, gpb * p, gpb * h), BF16)
    return pl.pallas_call(
        functools.partial(_zoh_kernel, hch=h, p=p),
        out_shape=(row, row, wb, wb, wc, wc),
        name="s5_zoh",
    )(a_re.reshape(1, gp), a_im.reshape(1, gp), jnp.repeat(log_dt, p).reshape(1, gp),
      b_re.transpose(2, 0, 1).reshape(h, gp), b_im.transpose(2, 0, 1).reshape(h, gp),
      c_re.transpose(0, 2, 1).reshape(gp, h), c_im.transpose(0, 2, 1).reshape(gp, h))


def _ssm_in_proj(ub, wb_re_ref, wb_im_ref, bu_re, bu_im):
    nkb, ublk, sblk = wb_re_ref.shape
    for kb in range(nkb):
        ukb = ub[:, kb * ublk:(kb + 1) * ublk]
        bu_re[:, kb * sblk:(kb + 1) * sblk] = _dot(ukb, wb_re_ref[kb])
        bu_im[:, kb * sblk:(kb + 1) * sblk] = _dot(ukb, wb_im_ref[kb])


def _ssm_out_proj(h_re, h_im, wc_re_ref, wc_imn_ref):
    nkb, sblk, _ = wc_re_ref.shape
    ys = []
    for kb in range(nkb):
        hr = h_re[:, kb * sblk:(kb + 1) * sblk].astype(BF16)
        hi = h_im[:, kb * sblk:(kb + 1) * sblk].astype(BF16)
        ys.append(_dot(hr, wc_re_ref[kb]) + _dot(hi, wc_imn_ref[kb]))
    return jnp.concatenate(ys, axis=1)


def _ssm_glu(x, u, y, d_ref, wglu_ref):
    z = jax.nn.gelu(y + d_ref[...] * u).astype(BF16)
    gl = _dot(z, wglu_ref[...])
    d = x.shape[1]
    return x + gl[:, :d] * jax.nn.sigmoid(gl[:, d:])


def _split3(v):
    hi = v.astype(BF16)
    r1 = v - hi.astype(F32)
    mid = r1.astype(BF16)
    lo = (r1 - mid.astype(F32)).astype(BF16)
    return hi, mid, lo


def _ssm_prompt_kernel(x_ref, g_ref, lam_re_ref, lam_im_ref, wb_re_ref, wb_im_ref,
                       wc_re_ref, wc_imn_ref, d_ref, wglu_ref, perm_ref, permt_ref,
                       out_ref, st_re_ref, st_im_ref, bu_re_all, bu_im_all, *, lc, bsz, lane_chunk):
    @pl.when(pl.program_id(0) == 0)
    def _():
        st_re_ref[...] = jnp.zeros_like(st_re_ref)
        st_im_ref[...] = jnp.zeros_like(st_im_ref)

    for sub in range(bu_re_all.shape[0]):
        ts = slice(sub * lc, (sub + 1) * lc)
        _ssm_prompt_chunk(x_ref[:, ts, :], g_ref, lam_re_ref, lam_im_ref, wb_re_ref, wb_im_ref,
                          wc_re_ref, wc_imn_ref, d_ref, wglu_ref, perm_ref, permt_ref,
                          out_ref.at[:, ts, :], st_re_ref, st_im_ref, bu_re_all.at[sub], bu_im_all.at[sub],
                          lc=lc, bsz=bsz, lane_chunk=lane_chunk)


def _ssm_prompt_chunk(x3, g_ref, lam_re_ref, lam_im_ref, wb_re_ref, wb_im_ref,
                      wc_re_ref, wc_imn_ref, d_ref, wglu_ref, perm_ref, permt_ref,
                      out_ref, st_re_ref, st_im_ref, bu_re, bu_im, *, lc, bsz, lane_chunk):
    d = x3.shape[2]
    x = x3.reshape(bsz * lc, d)
    u = _rms(x, g_ref[...])
    ub_tb = _dot(perm_ref[...], u.astype(BF16)).astype(BF16)
    _ssm_in_proj(ub_tb, wb_re_ref, wb_im_ref, bu_re, bu_im)

    gp = bu_re.shape[1]
    for c0 in range(0, gp, lane_chunk):
        sl = slice(c0, c0 + lane_chunk)
        lre = jnp.broadcast_to(lam_re_ref[:, sl], (bsz, lane_chunk))
        lim = jnp.broadcast_to(lam_im_ref[:, sl], (bsz, lane_chunk))
        hr = st_re_ref[:, sl]
        hi = st_im_ref[:, sl]
        for t in range(lc):
            rows = slice(t * bsz, (t + 1) * bsz)
            hr, hi = (lre * hr - lim * hi + bu_re[rows, sl], lre * hi + lim * hr + bu_im[rows, sl])
            bu_re[rows, sl] = hr
            bu_im[rows, sl] = hi
        st_re_ref[:, sl] = hr
        st_im_ref[:, sl] = hi

    permt = permt_ref[...]
    hi, mid, lo = _split3(_ssm_out_proj(bu_re, bu_im, wc_re_ref, wc_imn_ref))
    y = (_dot(permt, hi) + _dot(permt, mid)) + _dot(permt, lo)
    out_ref[...] = _ssm_glu(x, u, y, d_ref, wglu_ref).reshape(bsz, lc, d)


def _ssm_sample_kernel(x_ref, g_ref, lam_re_ref, lam_im_ref, h0_re_ref, h0_im_ref, wb_re_ref, wb_im_ref,
                       wc_re_ref, wc_imn_ref, d_ref, wglu_ref,
                       out_ref, h_re_ref, h_im_ref):
    x = x_ref[...]
    u = _rms(x, g_ref[...])
    _ssm_in_proj(u.astype(BF16), wb_re_ref, wb_im_ref, h_re_ref, h_im_ref)
    lre = lam_re_ref[...]
    lim = lam_im_ref[...]
    h0r = h0_re_ref[...]
    h0i = h0_im_ref[...]
    h_re_ref[...] = lre * h0r - lim * h0i + h_re_ref[...]
    h_im_ref[...] = lre * h0i + lim * h0r + h_im_ref[...]
    y = _ssm_out_proj(h_re_ref, h_im_ref, wc_re_ref, wc_imn_ref)
    n = x.shape[0]
    out_ref[0:n, :] = _ssm_glu(x, u, y, d_ref, wglu_ref)
    out_ref[n:, :] = jnp.zeros((out_ref.shape[0] - n, out_ref.shape[1]), F32)


def _ssm_prompt(x, g, lam_re, lam_im, wb_re, wb_im, wc_re, wc_imn, d_skip, wglu, *, lc, nsub):
    bsz, seq, d = x.shape
    assert seq % (lc * nsub) == 0
    gp = lam_re.shape[1]
    r = lc * bsz
    perm = np.zeros((r, r), np.float32)
    for b in range(bsz):
        for t in range(lc):
            perm[t * bsz + b, b * lc + t] = 1.0
    permt = jnp.asarray(perm.T, BF16)
    perm = jnp.asarray(perm, BF16)
    kern = functools.partial(_ssm_prompt_kernel, lc=lc, bsz=bsz, lane_chunk=8 * LANES)
    return pl.pallas_call(
        kern,
        grid=(seq // (lc * nsub),),
        in_specs=[pl.BlockSpec((bsz, lc * nsub, d), lambda c: (0, c, 0)),
                  _const_spec(g.shape), _const_spec(lam_re.shape), _const_spec(lam_im.shape),
                  _const_spec(wb_re.shape), _const_spec(wb_im.shape),
                  _const_spec(wc_re.shape), _const_spec(wc_imn.shape),
                  _const_spec(d_skip.shape), _const_spec(wglu.shape),
                  _const_spec(perm.shape), _const_spec(permt.shape)],
        out_specs=[pl.BlockSpec((bsz, lc * nsub, d), lambda c: (0, c, 0)),
                   _const_spec((bsz, gp)), _const_spec((bsz, gp))],
        out_shape=[jax.ShapeDtypeStruct((bsz, seq, d), F32),
                   jax.ShapeDtypeStruct((bsz, gp), F32), jax.ShapeDtypeStruct((bsz, gp), F32)],
        scratch_shapes=[pltpu.VMEM((nsub, r, gp), F32), pltpu.VMEM((nsub, r, gp), F32)],
        compiler_params=_params("arbitrary"),
        name="s5_prompt",
    )(x, g, lam_re, lam_im, wb_re, wb_im, wc_re, wc_imn, d_skip, wglu, perm, permt)


def _ssm_sample(x, g, lam_re, lam_im, h0_re, h0_im, wb_re, wb_im, wc_re, wc_imn, d_skip, wglu, *, pad_rows):
    n, d = x.shape
    gp = lam_re.shape[1]
    return pl.pallas_call(
        _ssm_sample_kernel,
        out_shape=[jax.ShapeDtypeStruct((pad_rows, d), F32),
                   jax.ShapeDtypeStruct((n, gp), F32), jax.ShapeDtypeStruct((n, gp), F32)],
        compiler_params=_params(),
        name="s5_sample",
    )(x, g, lam_re, lam_im, h0_re, h0_im, wb_re, wb_im, wc_re, wc_imn, d_skip, wglu)


def _ffn_chunks(d_ff):
    step = 3 * MXU_DIM
    return [(c, min(c + step, d_ff)) for c in range(0, d_ff, step)]


def _ffn_kernel(xp_ref, xs_ref, gffn_ref, wgu_ref, wd_ref, gkv_ref, wkv_ref, gq_ref, wq_ref,
                x2_ref, kv_ref, q_ref):
    x = jnp.where(pl.program_id(0) < pl.num_programs(0) - 1, xp_ref[...], xs_ref[...])
    hb = _rms(x, gffn_ref[...]).astype(BF16)
    d_ff = wd_ref.shape[0]
    acc = None
    for c0, c1 in _ffn_chunks(d_ff):
        a = _dot(hb, wgu_ref[:, c0:c1])
        b = _dot(hb, wgu_ref[:, d_ff + c0:d_ff + c1])
        part = _dot((jax.nn.silu(a) * b).astype(BF16), wd_ref[c0:c1, :])
        acc = part if acc is None else acc + part
    x2 = x + acc
    x2_ref[...] = x2
    kv_ref[...] = _dot(_rms(x2, gkv_ref[...]).astype(BF16), wkv_ref[...])
    q_ref[...] = _dot(_rms(x2, gq_ref[...]).astype(BF16), wq_ref[...]).astype(BF16)


def _ffn(xp, xs, gffn, wgu, wd, gkv, wkv, gq, wq):
    tm, d = xs.shape
    npt = xp.shape[0] // tm
    rows = xp.shape[0] + tm
    kvw = wkv.shape[1]
    nq = wq.shape[1]
    return pl.pallas_call(
        _ffn_kernel,
        grid=(npt + 1,),
        in_specs=[pl.BlockSpec((tm, d), lambda i: (jnp.minimum(i, npt - 1), 0)),
                  _const_spec(xs.shape),
                  _const_spec(gffn.shape), _const_spec(wgu.shape), _const_spec(wd.shape),
                  _const_spec(gkv.shape), _const_spec(wkv.shape),
                  _const_spec(gq.shape), _const_spec(wq.shape)],
        out_specs=[pl.BlockSpec((tm, d), lambda i: (i, 0)),
                   pl.BlockSpec((tm, kvw), lambda i: (i, 0)),
                   pl.BlockSpec((tm, nq), lambda i: (i, 0))],
        out_shape=[jax.ShapeDtypeStruct((rows, d), F32),
                   jax.ShapeDtypeStruct((rows, kvw), F32),
                   jax.ShapeDtypeStruct((rows, nq), BF16)],
        compiler_params=_params("arbitrary"),
        name="ffn_kv_q",
    )(xp, xs, gffn, wgu, wd, gkv, wkv, gq, wq)


def _t5_bucket(dist, num_buckets):
    max_exact = num_buckets // 2
    d = jnp.maximum(dist, 0)
    large = max_exact + (jnp.log(jnp.maximum(d, 1).astype(F32) / max_exact)
                         / math.log(MAX_DISTANCE / max_exact) * (num_buckets - max_exact)).astype(jnp.int32)
    large = jnp.minimum(large, num_buckets - 1)
    return jnp.where(d < max_exact, d, large)


def _bias_kernel(bm_ref, bs_ref, rb_ref, bias_ref, bias_s_ref, *, kvh, rep):
    nb, nh = rb_ref.shape
    bm = bm_ref[...]
    bs = bs_ref[...]

    def lookup(buckets, h):
        def body(k, acc):
            return jnp.where(buckets == k, rb_ref[k, h], acc)
        return lax.fori_loop(0, nb, body, jnp.zeros(buckets.shape, F32))

    w = bm.shape[0]
    for g in range(kvh):
        for r in range(rep):
            bias_ref[g, :, r * w:(r + 1) * w] = lookup(bm, g * rep + r) * math.log2(math.e)
            bias_s_ref[g, r:r + 1, :] = lookup(bs, g * rep + r)


def _bias_tables(rel_bias, window, kvh, rep):
    nb, nh = rel_bias.shape
    key = jnp.arange(window)[:, None]
    qry = jnp.arange(window)[None, :]
    bm = _t5_bucket((qry - key) % window, nb).astype(jnp.int32)
    bs = _t5_bucket(window - 1 - qry, nb).astype(jnp.int32)
    kern = functools.partial(_bias_kernel, kvh=kvh, rep=rep)
    return pl.pallas_call(
        kern,
        in_specs=[pl.BlockSpec(memory_space=pltpu.VMEM), pl.BlockSpec(memory_space=pltpu.VMEM),
                  pl.BlockSpec(memory_space=pltpu.SMEM)],
        out_shape=[jax.ShapeDtypeStruct((kvh, window, rep * window), F32),
                   jax.ShapeDtypeStruct((kvh, rep, window), F32)],
        name="t5_bias",
    )(bm, bs, rel_bias)


def _attn_prompt_kernel(q_ref, kvc_ref, kvp_ref, bias_ref, sink_ref, ot_ref, *, kvh, rep, hd, scale):
    w = kvp_ref.shape[0]
    kv_all = kvc_ref[...]
    for sub in range(q_ref.shape[0] // w):
        rows = slice(sub * w, (sub + 1) * w)
        kvp = kvp_ref[...] if sub == 0 else kv_all[(sub - 1) * w:sub * w]
        first = pl.program_id(1) == 0 if sub == 0 else None
        _attn_prompt_block(q_ref[rows, :], kv_all[rows], kvp, first, bias_ref, sink_ref, ot_ref, sub * w,
                           kvh=kvh, rep=rep, hd=hd, scale=scale)


def _attn_prompt_block(q, kvc, kvp, first, bias_ref, sink_ref, ot_ref, col0, *, kvh, rep, hd, scale):
    w = q.shape[0]
    kvw = kvh * hd
    lane = lax.broadcasted_iota(jnp.int32, (1, LANES), 1)
    key = lax.broadcasted_iota(jnp.int32, (w, rep * w), 0)
    qry = lax.broadcasted_iota(jnp.int32, (w, rep * w), 1) % w
    upper = key > qry
    mask_add = None if first is None else jnp.where(jnp.logical_and(upper, first), NEG_INF, 0.0)
    log2e = math.log2(math.e)
    heads_per_blk = LANES // hd
    for p in range(kvw // LANES):
        cs = slice(p * LANES, (p + 1) * LANES)
        kc, kp = kvc[:, cs], kvp[:, cs]
        vs = slice(kvw + p * LANES, kvw + (p + 1) * LANES)
        vb = jnp.concatenate([kvp[:, vs], kvc[:, vs]], axis=0).astype(BF16)
        qg = jnp.concatenate([q[:, r * kvw + p * LANES:r * kvw + (p + 1) * LANES] for r in range(rep)], axis=0)
        halves = []
        for half in range(heads_per_blk):
            g = p * heads_per_blk + half
            lmask = jnp.logical_and(lane >= half * hd, lane < (half + 1) * hd)
            kb = jnp.concatenate([jnp.where(lmask, kp, 0.0), jnp.where(lmask, kc, 0.0)], axis=0).astype(BF16)
            s = lax.dot_general(kb, qg, (((1,), (1,)), ((), ())), preferred_element_type=F32)
            bias = bias_ref[g] if first is None else bias_ref[g] + mask_add
            sc = jnp.where(upper, s[:w], s[w:]) * (scale * log2e) + bias
            sink = jnp.concatenate([jnp.full((1, w), sink_ref[g * rep + r] * log2e, F32) for r in range(rep)],
                                   axis=1)
            m = jnp.maximum(jnp.max(sc, axis=0, keepdims=True), sink)
            pe = jnp.exp2(sc - m)
            denom = jnp.sum(pe, axis=0, keepdims=True) + jnp.exp2(sink - m)
            pcat = jnp.concatenate([jnp.where(upper, pe, 0.0), jnp.where(upper, 0.0, pe)], axis=0).astype(BF16)
            og = lax.dot_general(vb, pcat, (((0,), (0,)), ((), ())), preferred_element_type=F32)
            halves.append((og / denom)[half * hd:(half + 1) * hd])
        o_blk = jnp.concatenate(halves, axis=0).astype(BF16)
        for r in range(rep):
            ot_ref[r * kvw + p * LANES:r * kvw + (p + 1) * LANES, col0:col0 + w] = o_blk[:, r * w:(r + 1) * w]


def _attn_prompt(q, kv, bias, sinks, *, bsz, nblk, window, kvh, rep, hd):
    rows, nq = q.shape
    kvw2 = kv.shape[1]
    nsub = next(n for n in (4, 2, 1) if nblk % n == 0 and rows % (n * window) == 0)
    nstep = nblk // nsub
    kern = functools.partial(_attn_prompt_kernel, kvh=kvh, rep=rep, hd=hd, scale=1.0 / math.sqrt(hd))
    return pl.pallas_call(
        kern,
        grid=(bsz, nstep),
        in_specs=[pl.BlockSpec((nsub * window, nq), lambda b, i: (b * nstep + i, 0)),
                  pl.BlockSpec((nsub * window, kvw2), lambda b, i: (b * nstep + i, 0)),
                  pl.BlockSpec((window, kvw2), lambda b, i: (b * nblk + jnp.maximum(nsub * i - 1, 0), 0)),
                  _const_spec(bias.shape),
                  pl.BlockSpec(memory_space=pltpu.SMEM)],
        out_specs=pl.BlockSpec((nq, nsub * window), lambda b, i: (0, b * nstep + i)),
        out_shape=jax.ShapeDtypeStruct((nq, rows), BF16),
        compiler_params=_params("arbitrary", "arbitrary"),
        name="swa_prompt",
    )(q, kv, kv, bias, sinks)


def _attn_sample_kernel(q_ref, kv_ref, ck_ref, cv_ref, bias_ref, sink_ref, o_ref, nk_ref, nv_ref,
                        *, kvh, rep, hd, scale):
    nb, w, kvw = ck_ref.shape
    kv = kv_ref[...]
    lane = lax.broadcasted_iota(jnp.int32, (nb, rep, kvw), 2)

    def shifted(c_ref, n_ref, new):
        flat = c_ref[...].reshape(nb * w, kvw)
        n_ref[...] = pltpu.roll(flat, nb * w - 1, axis=0).reshape(nb, w, kvw)
        n_ref[:, w - 1:w, :] = new
        return n_ref[...].astype(BF16)

    nkb = shifted(ck_ref, nk_ref, kv[:, :, :kvw])
    nvb = shifted(cv_ref, nv_ref, kv[:, :, kvw:])
    q = q_ref[...].astype(F32)
    o = jnp.zeros((nb, rep, kvw), F32)
    for g in range(kvh):
        lmask = jnp.logical_and(lane >= g * hd, lane < (g + 1) * hd)
        qg = jnp.where(lmask, q, 0.0).astype(BF16)
        s = jnp.einsum("nrc,njc->nrj", qg, nkb, preferred_element_type=F32)
        sc = s * scale + bias_ref[g][None]
        sink = sink_ref[g][None]
        m = jnp.maximum(jnp.max(sc, axis=-1, keepdims=True), sink)
        pe = jnp.exp(sc - m)
        probs = pe / (jnp.sum(pe, axis=-1, keepdims=True) + jnp.exp(sink - m))
        og = jnp.einsum("nrj,njc->nrc", probs.astype(BF16), nvb, preferred_element_type=F32)
        o = jnp.where(lmask, og, o)
    o_ref[...] = o.astype(BF16)


def _attn_sample(q3, kv3, ck, cv, bias_s, sink_s, *, nb, kvh, rep, hd):
    n, w, kvw = ck.shape
    kern = functools.partial(_attn_sample_kernel, kvh=kvh, rep=rep, hd=hd, scale=1.0 / math.sqrt(hd))
    cache_spec = pl.BlockSpec((nb, w, kvw), lambda i: (i, 0, 0))
    return pl.pallas_call(
        kern,
        grid=(n // nb,),
        in_specs=[pl.BlockSpec((nb, rep, kvw), lambda i: (i, 0, 0)),
                  pl.BlockSpec((nb, 1, 2 * kvw), lambda i: (i, 0, 0)),
                  cache_spec, cache_spec,
                  _const_spec(bias_s.shape), _const_spec(sink_s.shape)],
        out_specs=[pl.BlockSpec((nb, rep, kvw), lambda i: (i, 0, 0)), cache_spec, cache_spec],
        out_shape=[jax.ShapeDtypeStruct((n, rep, kvw), BF16),
                   jax.ShapeDtypeStruct((n, w, kvw), F32), jax.ShapeDtypeStruct((n, w, kvw), F32)],
        compiler_params=_params("arbitrary"),
        name="swa_sample",
    )(q3, kv3, ck, cv, bias_s, sink_s)


CHUNK = 16
BIG_COPY = 128
SUB = 256
UP_ROW_BLOCK = 1024
DOWN_ROW_BLOCK = 1024
TOKEN_TILE = 512
MOE_TILE = 512


def _chunk_copy(src, src_row, dst, dst_row, sem, rows=CHUNK):
    return pltpu.make_async_copy(src.at[pl.ds(pl.multiple_of(src_row, CHUNK), rows), :],
                                 dst.at[pl.ds(pl.multiple_of(dst_row, CHUNK), rows), :], sem)


def _start_copies(src, src_row, dst, dst_row, sem, nrows):
    nbig = lax.shift_right_logical(nrows, int(math.log2(BIG_COPY)))

    def big(c, carry):
        _chunk_copy(src, src_row + c * BIG_COPY, dst, dst_row + c * BIG_COPY, sem, BIG_COPY).start()
        return carry

    lax.fori_loop(0, nbig, big, 0)
    done = nbig * BIG_COPY

    def small(c, carry):
        _chunk_copy(src, src_row + done + c * CHUNK, dst, dst_row + done + c * CHUNK, sem).start()
        return carry

    lax.fori_loop(0, lax.shift_right_logical(nrows - done, int(math.log2(CHUNK))), small, 0)


def _moe_route_kernel(x2_ref, ot_ref, wo_ref, gffn_ref, wr_ref, lst_ref, ust_ref,
                      x3_ref, cm_ref, tmeta_ref, xs_hbm,
                      comp_s, carry_s, zero_s, base_v, pend_sm, sem, *, ne, cap, n_valid):
    i = pl.program_id(0)
    nt = pl.num_programs(0)
    tm = x2_ref.shape[0]
    w = comp_s.shape[1]
    slot = lax.rem(i, 2)

    @pl.when(i == 0)
    def _():
        base_v[...] = jnp.zeros_like(base_v)
        zero_s[...] = jnp.zeros_like(zero_s)
        carry_s[...] = jnp.zeros_like(carry_s)

    def drain(n):
        def body(c, carry):
            _chunk_copy(zero_s, 0, xs_hbm, 0, sem).wait()
            return carry
        lax.fori_loop(0, n, body, 0)

    @pl.when(i > 0)
    def _():
        drain(pend_sm[0])

    x3 = x2_ref[...] + lax.dot_general(ot_ref[...], wo_ref[...], (((0,), (0,)), ((), ())),
                                       preferred_element_type=F32)
    x3_ref[...] = x3
    hb = _rms(x3, gffn_ref[...]).astype(BF16)

    lane = lax.broadcasted_iota(jnp.int32, (tm, LANES), 1)
    logits = jnp.where(lane < ne, _dot(hb, wr_ref[...]), -jnp.inf)
    m1 = jnp.max(logits, axis=-1, keepdims=True)
    i1 = jnp.min(jnp.where(logits == m1, lane, LANES), axis=-1, keepdims=True)
    rest = jnp.where(lane == i1, -jnp.inf, logits)
    m2 = jnp.max(rest, axis=-1, keepdims=True)
    i2 = jnp.min(jnp.where(rest == m2, lane, LANES), axis=-1, keepdims=True)
    e2 = jnp.exp(m2 - m1)
    g1 = 1.0 / (1.0 + e2)
    g2 = e2 / (1.0 + e2)

    live = i * tm + lax.broadcasted_iota(jnp.int32, (tm, 1), 0) < n_valid
    sel = jnp.where(jnp.logical_and(live, jnp.logical_or(lane == i1, lane == i2)), 1.0, 0.0)
    rank = _dot(lst_ref[...], sel.astype(BF16))
    cnt = jnp.sum(sel, axis=0, keepdims=True)
    fill = base_v[...]
    rem = fill - jnp.floor(fill / CHUNK) * CHUNK
    cpad = jnp.floor((rem + cnt + (CHUNK - 1)) / CHUNK) * CHUNK
    loff = _dot(jnp.broadcast_to(cpad, (8, LANES)).astype(BF16), ust_ref[...])[0:1]
    dest = loff + rem + rank
    ld1 = jnp.where(live, jnp.sum(jnp.where(lane == i1, dest, 0.0), axis=-1, keepdims=True), -1.0)
    ld2 = jnp.where(live, jnp.sum(jnp.where(lane == i2, dest, 0.0), axis=-1, keepdims=True), -1.0)
    cm = jnp.where(lane == 0, ld1, jnp.where(lane == 1, ld2, jnp.where(lane == 2, g1, jnp.where(lane == 3, g2, 0.0))))
    cm_ref[...] = cm

    rm = cm.T
    rowi = lax.broadcasted_iota(jnp.int32, (w, tm), 0).astype(F32)
    place = (jnp.where(rowi == rm[0:1], 1.0, 0.0) + jnp.where(rowi == rm[1:2], 1.0, 0.0)).astype(BF16)
    comp_s[slot] = _dot(place, hb).astype(BF16)

    base = fill - rem
    srow = lax.broadcasted_iota(jnp.int32, (8, LANES), 0)
    tmeta_ref[...] = jnp.where(srow == 0, cpad, jnp.where(srow == 1, base, jnp.where(srow == 2, fill + cnt, 0.0))
                               ).astype(jnp.int32)
    base_v[...] = fill + cnt

    cpad_i = cpad.astype(jnp.int32)
    loff_i = loff.astype(jnp.int32)
    base_i = base.astype(jnp.int32)
    tail_i = (rem + cnt - jnp.floor((rem + cnt) / CHUNK) * CHUNK).astype(jnp.int32)
    total = 0
    tails = []
    for e in range(ne):
        n_e = cpad_i[0, e]
        src0 = loff_i[0, e]
        dst0 = base_i[0, e] + e * cap

        @pl.when(n_e > 0)
        def _(e=e, n_e=n_e, src0=src0, partial=tail_i[0, e] > 0):
            head = pl.ds(pl.multiple_of(src0, CHUNK), CHUNK)
            comp_s[slot, head, :] = comp_s[slot, head, :] + carry_s[e]
            last = comp_s[slot, pl.ds(pl.multiple_of(src0 + n_e - CHUNK, CHUNK), CHUNK), :]
            carry_s[e] = jnp.where(partial, last, jnp.zeros_like(last))

        _start_copies(comp_s.at[slot], src0, xs_hbm, dst0, sem, n_e)
        total = total + lax.shift_right_logical(n_e, int(math.log2(CHUNK)))
        tails.append(dst0 + n_e)
    pend_sm[0] = total

    @pl.when(i == nt - 1)
    def _():
        nz_total = 0
        for e in range(ne):
            end = tails[e]
            nz = lax.shift_right_logical(lax.rem(SUB - lax.rem(end, SUB), SUB), int(math.log2(CHUNK)))

            def zstart(c, carry, end=end):
                _chunk_copy(zero_s, 0, xs_hbm, end + c * CHUNK, sem).start()
                return carry

            lax.fori_loop(0, nz, zstart, 0)
            nz_total = nz_total + nz
        drain(total + nz_total)


def _moe_route(x2, ot, wo, gffn, wr_pad, *, tm, ne, cap, n_valid):
    rows, d = x2.shape
    nt = rows // tm
    w = _staging_rows(tm, ne)
    lst = jnp.asarray(np.tril(np.ones((tm, tm), np.float32), -1), BF16)
    ust = jnp.asarray(np.triu(np.ones((LANES, LANES), np.float32), 1), BF16)
    kern = functools.partial(_moe_route_kernel, ne=ne, cap=cap, n_valid=n_valid)
    return pl.pallas_call(
        kern,
        grid=(nt,),
        in_specs=[pl.BlockSpec((tm, d), lambda i: (i, 0)),
                  pl.BlockSpec((ot.shape[0], tm), lambda i: (0, i)),
                  _const_spec(wo.shape), _const_spec(gffn.shape), _const_spec(wr_pad.shape),
                  _const_spec(lst.shape), _const_spec(ust.shape)],
        out_specs=[pl.BlockSpec((tm, d), lambda i: (i, 0)),
                   pl.BlockSpec((tm, LANES), lambda i: (i, 0)),
                   pl.BlockSpec((8, LANES), lambda i: (i, 0)),
                   pl.BlockSpec(memory_space=pl.ANY)],
        out_shape=[jax.ShapeDtypeStruct((rows, d), F32),
                   jax.ShapeDtypeStruct((rows, LANES), F32),
                   jax.ShapeDtypeStruct((nt * 8, LANES), jnp.int32),
                   jax.ShapeDtypeStruct((ne * cap, d), BF16)],
        scratch_shapes=[pltpu.VMEM((2, w, d), BF16), pltpu.VMEM((ne, CHUNK, d), BF16), pltpu.VMEM((CHUNK, d), BF16),
                        pltpu.VMEM((1, LANES), F32), pltpu.SMEM((1,), jnp.int32),
                        pltpu.SemaphoreType.DMA(())],
        compiler_params=_params("arbitrary"),
        name="moe_route",
    )(x2, ot, wo, gffn, wr_pad, lst, ust)


def _staging_rows(tm, ne):
    return -(-(TOP_K * tm + 2 * ne * (CHUNK - 1)) // LANES) * LANES


def _max_blocks(rows, ne, row_block):
    return TOP_K * rows // row_block + ne


def _expert_steps(nrows, nf, max_blocks, row_block):
    ne = nrows.shape[0]
    nblk = (nrows + row_block - 1) // row_block
    cum = jnp.cumsum(nblk) * nf
    total = cum[-1]
    s = jnp.minimum(jnp.arange(max_blocks * nf, dtype=jnp.int32), total - 1)
    e = jnp.minimum(jnp.sum((s[:, None] >= cum[None, :]).astype(jnp.int32), axis=1), ne - 1)
    nb_e = nblk[e]
    within = s - (cum[e] - nb_e * nf)
    f = within // nb_e
    pos = within - f * nb_e
    r = (pos + nb_e - 1) % nb_e
    nsub = jnp.clip((nrows[e] - r * row_block + SUB - 1) // SUB, 0, row_block // SUB)
    first = (pos == 0).astype(jnp.int32)
    return (e, f.astype(jnp.int32), r.astype(jnp.int32), first, nsub.astype(jnp.int32),
            jnp.reshape(total, (1,)).astype(jnp.int32))


def _moe_up_kernel(e_ref, f_ref, r_ref, first_ref, nsub_ref, n_ref, xs_ref, wg_ref, wu_ref, act_ref, wgb_s, wub_s):
    s = pl.program_id(0)

    @pl.when(s < n_ref[0])
    def _():
        @pl.when(first_ref[s] == 1)
        def _():
            wgb_s[...] = wg_ref[...].astype(BF16)
            wub_s[...] = wu_ref[...].astype(BF16)

        def body(j, carry):
            rows = pl.ds(pl.multiple_of(j * SUB, SUB), SUB)
            xsb = xs_ref[rows, :]
            a = _dot(xsb, wgb_s[...])
            b = _dot(xsb, wub_s[...])
            act_ref[rows, :] = (jax.nn.silu(a) * b).astype(BF16)
            return carry

        lax.fori_loop(0, nsub_ref[s], body, 0)


def _moe_up(steps, xs, wgu, *, fc, cap, row_block):
    ne, d, dff2 = wgu.shape
    dff = dff2 // 2
    nf = dff // fc
    cb = cap // row_block
    grid_spec = pltpu.PrefetchScalarGridSpec(
        num_scalar_prefetch=6,
        grid=(steps[0].shape[0],),
        in_specs=[pl.BlockSpec((row_block, d), lambda s, e, f, r, *_: (e[s] * cb + r[s], 0)),
                  pl.BlockSpec((None, d, fc), lambda s, e, f, r, *_: (e[s], 0, f[s])),
                  pl.BlockSpec((None, d, fc), lambda s, e, f, r, *_: (e[s], 0, nf + f[s]))],
        out_specs=pl.BlockSpec((row_block, fc), lambda s, e, f, r, *_: (e[s] * cb + r[s], f[s])),
        scratch_shapes=[pltpu.VMEM((d, fc), BF16), pltpu.VMEM((d, fc), BF16)])
    return pl.pallas_call(
        _moe_up_kernel,
        grid_spec=grid_spec,
        out_shape=jax.ShapeDtypeStruct((ne * cap, dff), BF16),
        compiler_params=_params("arbitrary"),
        name="moe_up",
    )(*steps, xs, wgu, wgu)


def _moe_down_kernel(e_ref, f_ref, r_ref, first_ref, nsub_ref, n_ref, act_ref, wd_ref, ys_ref, wdb_s):
    s = pl.program_id(0)

    @pl.when(s < n_ref[0])
    def _():
        @pl.when(first_ref[s] == 1)
        def _():
            wdb_s[...] = wd_ref[...].astype(BF16)

        def body(j, carry):
            rows = pl.ds(pl.multiple_of(j * SUB, SUB), SUB)
            ys_ref[rows, :] = _dot(act_ref[rows, :], wdb_s[...]).astype(BF16)
            return carry

        lax.fori_loop(0, nsub_ref[s], body, 0)


def _moe_down(steps, act, wd, *, cap, row_block):
    ne, dff, d = wd.shape
    cb = cap // row_block
    grid_spec = pltpu.PrefetchScalarGridSpec(
        num_scalar_prefetch=6,
        grid=(steps[0].shape[0],),
        in_specs=[pl.BlockSpec((row_block, dff), lambda s, e, f, r, *_: (e[s] * cb + r[s], 0)),
                  pl.BlockSpec((None, dff, d), lambda s, e, f, r, *_: (e[s], 0, 0))],
        out_specs=pl.BlockSpec((row_block, d), lambda s, e, f, r, *_: (e[s] * cb + r[s], 0)),
        scratch_shapes=[pltpu.VMEM((dff, d), BF16)])
    return pl.pallas_call(
        _moe_down_kernel,
        grid_spec=grid_spec,
        out_shape=jax.ShapeDtypeStruct((ne * cap, d), BF16),
        compiler_params=_params("arbitrary", vmem=2 * (dff * d * 4 + row_block * (dff + d) * 2) + dff * d * 2
                                + 4 * SUB * d * 4),
        name="moe_down",
    )(*steps, act, wd)


def _moe_combine_kernel(cpad_ref, seg_ref, x3_ref, cm_ref, gfin_ref, ys_hbm, yp_ref, ys_ref, yloc_s, sem,
                        *, ne, npt):
    i = pl.program_id(0)
    nt = pl.num_programs(0)
    tm = x3_ref.shape[0]
    w = yloc_s.shape[1]
    slot = lax.rem(i, 2)
    shift = int(math.log2(CHUNK))

    def issue(t, sl):
        off = 0
        for e in range(ne):
            n_e = cpad_ref[t * ne + e]
            _start_copies(ys_hbm, seg_ref[t * ne + e], yloc_s.at[sl], off, sem.at[sl], n_e)
            off = off + n_e

    @pl.when(i == 0)
    def _():
        yloc_s[...] = jnp.zeros_like(yloc_s)
        issue(0, 0)

    @pl.when(i + 1 < nt)
    def _():
        issue(i + 1, 1 - slot)

    total = 0
    for e in range(ne):
        total = total + lax.shift_right_logical(cpad_ref[i * ne + e], shift)

    def wait(c, carry):
        _chunk_copy(ys_hbm, 0, yloc_s.at[slot], 0, sem.at[slot]).wait()
        return carry

    lax.fori_loop(0, total, wait, 0)

    cm = cm_ref[...]
    col = lax.broadcasted_iota(jnp.int32, (tm, w), 1).astype(F32)
    yl = yloc_s[slot]
    y1 = _dot(jnp.where(col == cm[:, 0:1], 1.0, 0.0).astype(BF16), yl)
    y2 = _dot(jnp.where(col == cm[:, 1:2], 1.0, 0.0).astype(BF16), yl)
    y = _rms(x3_ref[...] + cm[:, 2:3] * y1 + cm[:, 3:4] * y2, gfin_ref[...])

    @pl.when(i < npt)
    def _():
        yp_ref[...] = y

    @pl.when(i >= npt)
    def _():
        ys_ref[...] = y


def _moe_combine(cpad, seg, x3, cm, gfin, ys, *, tm, ne, sample_rows):
    rows, d = x3.shape
    nt = rows // tm
    npt = (rows - sample_rows) // tm
    w = _staging_rows(tm, ne)
    kern = functools.partial(_moe_combine_kernel, ne=ne, npt=npt)
    grid_spec = pltpu.PrefetchScalarGridSpec(
        num_scalar_prefetch=2,
        grid=(nt,),
        in_specs=[pl.BlockSpec((tm, d), lambda i, *_: (i, 0)),
                  pl.BlockSpec((tm, LANES), lambda i, *_: (i, 0)),
                  pl.BlockSpec(gfin.shape, lambda i, *_: (0, 0)),
                  pl.BlockSpec(memory_space=pl.ANY)],
        out_specs=[pl.BlockSpec((tm, d), lambda i, *_: (jnp.minimum(i, npt - 1), 0)),
                   pl.BlockSpec((tm, d), lambda i, *_: (jnp.maximum(i - npt, 0), 0))],
        scratch_shapes=[pltpu.VMEM((2, w, d), BF16), pltpu.SemaphoreType.DMA((2,))])
    return pl.pallas_call(
        kern,
        grid_spec=grid_spec,
        out_shape=[jax.ShapeDtypeStruct((npt * tm, d), F32), jax.ShapeDtypeStruct((sample_rows, d), F32)],
        compiler_params=_params("arbitrary"),
        name="moe_combine",
    )(cpad, seg, x3, cm, gfin, ys)


def _moe(x2, ot, wo, gffn, wr, wgu, wd, gfin, *, tm, fc, n_valid, sample_rows):
    rows, d = x2.shape
    ne = wr.shape[1]
    nt = rows // tm
    nf = wgu.shape[2] // 2 // fc
    cap = -(-(rows + SUB) // UP_ROW_BLOCK) * UP_ROW_BLOCK
    wr_pad = jnp.pad(wr, ((0, 0), (0, LANES - ne))).astype(BF16)
    x3, cm, tmeta, xs = _moe_route(x2, ot, wo, gffn, wr_pad, tm=tm, ne=ne, cap=cap, n_valid=n_valid)
    tmeta = tmeta.reshape(nt, 8, LANES)[:, :, :ne]
    cpad = tmeta[:, 0, :].reshape(nt * ne)
    seg = (tmeta[:, 1, :] + jnp.arange(ne, dtype=jnp.int32)[None, :] * cap).reshape(nt * ne)
    nrows = tmeta[nt - 1, 2, :]
    up_steps = _expert_steps(nrows, nf, _max_blocks(rows, ne, UP_ROW_BLOCK), UP_ROW_BLOCK)
    act = _moe_up(up_steps, xs, wgu, fc=fc, cap=cap, row_block=UP_ROW_BLOCK)
    down_steps = _expert_steps(nrows, 1, _max_blocks(rows, ne, DOWN_ROW_BLOCK), DOWN_ROW_BLOCK)
    ys = _moe_down(down_steps, act, wd, cap=cap, row_block=DOWN_ROW_BLOCK)
    return _moe_combine(cpad, seg, x3, cm, gfin, ys, tm=tm, ne=ne, sample_rows=sample_rows)


def kernel(x_prompt, x_sample, state_ssm_re, state_ssm_im, cache_k_win, cache_v_win, g_mix, g_ffn, g_kv, g_final, ssm_a_re, ssm_a_im, ssm_log_dt, ssm_b_re, ssm_b_im, ssm_c_re, ssm_c_im, ssm_d, w_glu, w_kv, w_q, w_o, attn_sinks, rel_bias, w_ffn_gate_up, w_ffn_down, w_router, w_exp_gate_up, w_exp_down):
    bsz, seq, d = x_prompt.shape
    ns, dec_seq, _ = x_sample.shape
    assert dec_seq == 1 and g_mix.shape[0] == 2 and ssm_a_re.shape[0] == 1 and w_q.shape[0] == 1
    _, g, p = ssm_a_re.shape
    gp = g * p
    window, kvh, hd = cache_k_win.shape[1:]
    kvw = kvh * hd
    nh = attn_sinks.shape[1]
    rep = nh // kvh
    nq = nh * hd
    assert bsz == 8 and ns % bsz == 0 and seq % window == 0 and LANES % hd == 0

    lam_re, lam_im, wb_re, wb_im, wc_re, wc_imn = _zoh(ssm_a_re[0], ssm_a_im[0], ssm_log_dt[0],
                                                        ssm_b_re[0], ssm_b_im[0], ssm_c_re[0], ssm_c_im[0])
    wglu = w_glu[0].astype(BF16)
    d_skip = ssm_d[0].reshape(1, d)
    wgu = w_ffn_gate_up[0].astype(BF16)
    wd = w_ffn_down[0].astype(BF16)
    wkv = w_kv.astype(BF16)
    wq = w_q[0].reshape(d, kvh, rep, hd).transpose(0, 2, 1, 3).reshape(d, nq).astype(BF16)
    wo = w_o[0].reshape(kvh, rep, hd, d).transpose(1, 0, 2, 3).reshape(nq, d).astype(BF16)
    bias, bias_s = _bias_tables(rel_bias, window, kvh, rep)
    sinks = attn_sinks[0]
    sink_s = sinks.reshape(kvh, rep, 1)

    tm = TOKEN_TILE
    npr = seq * bsz
    assert npr % tm == 0 and ns <= tm

    x1_p, st_re, st_im = _ssm_prompt(x_prompt, g_mix[0:1], lam_re, lam_im, wb_re, wb_im, wc_re, wc_imn,
                                     d_skip, wglu, lc=32, nsub=2)
    x1_s, hs_re, hs_im = _ssm_sample(x_sample.reshape(ns, d), g_mix[0:1], lam_re, lam_im,
                                     state_ssm_re[0].reshape(ns, gp), state_ssm_im[0].reshape(ns, gp),
                                     wb_re, wb_im, wc_re, wc_imn, d_skip, wglu, pad_rows=tm)

    x2, kv, q = _ffn(x1_p.reshape(npr, d), x1_s, g_ffn[0:1], wgu, wd, g_kv.reshape(1, d), wkv, g_mix[1:2], wq)

    ot = _attn_prompt(q, kv, bias, sinks, bsz=bsz, nblk=seq // window, window=window, kvh=kvh, rep=rep, hd=hd)
    o_s, nk_s, nv_s = _attn_sample(q[npr:npr + ns].reshape(ns, rep, kvw), kv[npr:npr + ns].reshape(ns, 1, 2 * kvw),
                                   cache_k_win.reshape(ns, window, kvw), cache_v_win.reshape(ns, window, kvw),
                                   bias_s, sink_s, nb=16, kvh=kvh, rep=rep, hd=hd)
    ot = lax.dynamic_update_slice(ot, jnp.pad(o_s.reshape(ns, nq).T, ((0, 0), (0, tm - ns))), (0, npr))

    y_p, y_s = _moe(x2, ot, wo, g_ffn[1:2], w_router[0], w_exp_gate_up[0], w_exp_down[0], g_final.reshape(1, d),
                    tm=MOE_TILE, fc=w_exp_down.shape[2] // 2, n_valid=npr + ns, sample_rows=tm)

    y_prompt = y_p.reshape(bsz, seq, d)
    y_sample = y_s[:ns].reshape(ns, 1, d)
    kv_tail = jnp.stack([kv[(b + 1) * seq - window:(b + 1) * seq] for b in range(bsz)])
    kv_tail = kv_tail.reshape(bsz, window, 2, kvh, hd).transpose(2, 0, 1, 3, 4)
    return (y_prompt, y_sample,
            st_re.reshape(1, bsz, g, p), st_im.reshape(1, bsz, g, p), kv_tail[0], kv_tail[1],
            hs_re.reshape(1, ns, g, p), hs_im.reshape(1, ns, g, p),
            nk_s.reshape(ns, window, kvh, hd), nv_s.reshape(ns, window, kvh, hd))
```

```python
import functools
import math

import numpy as np
import jax
import jax.numpy as jnp
from jax import lax
from jax.experimental import pallas as pl
from jax.experimental.pallas import tpu as pltpu

F32 = jnp.float32
BF16 = jnp.bfloat16

EPS = 1e-6
NEG_INF = -1e30
TOP_K = 2
MAX_DISTANCE = 128
MXU_DIM = 256
LANES = 128
VMEM_LIMIT_BYTES = 56 * 1024 * 1024


def _dot(a, b):
    return jnp.dot(a, b, preferred_element_type=F32)


def _rms(x, g):
    return x * lax.rsqrt(jnp.mean(x * x, axis=-1, keepdims=True) + EPS) * g


def _const_spec(shape):
    nd = len(shape)
    return pl.BlockSpec(shape, lambda *_: (0,) * nd)


def _params(*sem, vmem=VMEM_LIMIT_BYTES):
    return pltpu.CompilerParams(dimension_semantics=sem, vmem_limit_bytes=vmem)


def _zoh_kernel(a_re_ref, a_im_ref, log_dt_ref, b_re_ref, b_im_ref, ct_re_ref, ct_im_ref,
                lam_re_ref, lam_im_ref, wb_re_ref, wb_im_ref, wc_re_ref, wc_imn_ref, *, hch, p):
    a_re = a_re_ref[...]
    a_im = a_im_ref[...]
    dt = jnp.exp(log_dt_ref[...])
    mag = jnp.exp(a_re * dt)
    lr = mag * jnp.cos(a_im * dt)
    li = mag * jnp.sin(a_im * dt)
    lam_re_ref[...] = lr
    lam_im_ref[...] = li
    nr = lr - 1.0
    den = a_re * a_re + a_im * a_im
    qr = (nr * a_re + li * a_im) / den
    qi = (li * a_re - nr * a_im) / den
    b_re = b_re_ref[...]
    b_im = b_im_ref[...]
    bb_re = qr * b_re - qi * b_im
    bb_im = qr * b_im + qi * b_re

    nkb, ublk, sblk = wb_re_ref.shape
    gpb = ublk // hch
    sh_h, sh_p = int(math.log2(hch)), int(math.log2(p))
    row = lax.broadcasted_iota(jnp.int32, (ublk, sblk), 0)
    col = lax.broadcasted_iota(jnp.int32, (ublk, sblk), 1)
    diag_in = lax.shift_right_logical(row, sh_h) == lax.shift_right_logical(col, sh_p)
    row = lax.broadcasted_iota(jnp.int32, (sblk, ublk), 0)
    col = lax.broadcasted_iota(jnp.int32, (sblk, ublk), 1)
    diag_out = lax.shift_right_logical(row, sh_p) == lax.shift_right_logical(col, sh_h)
    spread = jnp.where(lax.broadcasted_iota(jnp.int32, (hch, ublk), 0)
                       == jnp.bitwise_and(lax.broadcasted_iota(jnp.int32, (hch, ublk), 1), hch - 1),
                       1.0, 0.0).astype(BF16)
    for kb in range(nkb):
        cs = slice(kb * sblk, (kb + 1) * sblk)
        wb_re_ref[kb] = jnp.where(diag_in, jnp.concatenate([bb_re[:, cs]] * gpb, axis=0), 0.0).astype(BF16)
        wb_im_ref[kb] = jnp.where(diag_in, jnp.concatenate([bb_im[:, cs]] * gpb, axis=0), 0.0).astype(BF16)
        c_re = _dot(ct_re_ref[cs, :].astype(BF16), spread)
        c_im = _dot(ct_im_ref[cs, :].astype(BF16), spread)
        wc_re_ref[kb] = jnp.where(diag_out, c_re, 0.0).astype(BF16)
        wc_imn_ref[kb] = jnp.where(diag_out, -c_im, 0.0).astype(BF16)


def _zoh(a_re, a_im, log_dt, b_re, b_im, c_re, c_im):
    g, p = a_re.shape
    h = b_re.shape[-1]
    gp = g * p
    gpb = MXU_DIM // h
    nkb = g // gpb
    assert h & (h - 1) == 0 and p & (p - 1) == 0 and g % gpb == 0
    row = jax.ShapeDtypeStruct((1, gp), F32)
    wb = jax.ShapeDtypeStruct((nkb, gpb * h, gpb * p), BF16)
    wc = jax.ShapeDtypeStruct((nkb, gpb * p, gpb * h), BF16)
    return pl.pallas_call(
        functools.partial(_zoh_kernel, hch=h, p=p),
        out_shape=(row, row, wb, wb, wc, wc),
        name="s5_zoh",
    )(a_re.reshape(1, gp), a_im.reshape(1, gp), jnp.repeat(log_dt, p).reshape(1, gp),
      b_re.transpose(2, 0, 1).reshape(h, gp), b_im.transpose(2, 0, 1).reshape(h, gp),
      c_re.transpose(0, 2, 1).reshape(gp, h), c_im.transpose(0, 2, 1).reshape(gp, h))


def _ssm_in_proj(ub, wb_re_ref, wb_im_ref, bu_re, bu_im):
    nkb, ublk, sblk = wb_re_ref.shape
    for kb in range(nkb):
        ukb = ub[:, kb * ublk:(kb + 1) * ublk]
        bu_re[:, kb * sblk:(kb + 1) * sblk] = _dot(ukb, wb_re_ref[kb])
        bu_im[:, kb * sblk:(kb + 1) * sblk] = _dot(ukb, wb_im_ref[kb])


def _ssm_out_proj(h_re, h_im, wc_re_ref, wc_imn_ref):
    nkb, sblk, _ = wc_re_ref.shape
    ys = []
    for kb in range(nkb):
        hr = h_re[:, kb * sblk:(kb + 1) * sblk].astype(BF16)
        hi = h_im[:, kb * sblk:(kb + 1) * sblk].astype(BF16)
        ys.append(_dot(hr, wc_re_ref[kb]) + _dot(hi, wc_imn_ref[kb]))
    return jnp.concatenate(ys, axis=1)


def _ssm_glu(x, u, y, d_ref, wglu_ref):
    z = jax.nn.gelu(y + d_ref[...] * u).astype(BF16)
    gl = _dot(z, wglu_ref[...])
    d = x.shape[1]
    return x + gl[:, :d] * jax.nn.sigmoid(gl[:, d:])


def _split3(v):
    hi = v.astype(BF16)
    r1 = v - hi.astype(F32)
    mid = r1.astype(BF16)
    lo = (r1 - mid.astype(F32)).astype(BF16)
    return hi, mid, lo


def _ssm_prompt_kernel(x_ref, g_ref, lam_re_ref, lam_im_ref, wb_re_ref, wb_im_ref,
                       wc_re_ref, wc_imn_ref, d_ref, wglu_ref, perm_ref, permt_ref,
                       out_ref, st_re_ref, st_im_ref, bu_re_all, bu_im_all, *, lc, bsz, lane_chunk):
    @pl.when(pl.program_id(0) == 0)
    def _():
        st_re_ref[...] = jnp.zeros_like(st_re_ref)
        st_im_ref[...] = jnp.zeros_like(st_im_ref)

    for sub in range(bu_re_all.shape[0]):
        ts = slice(sub * lc, (sub + 1) * lc)
        _ssm_prompt_chunk(x_ref[:, ts, :], g_ref, lam_re_ref, lam_im_ref, wb_re_ref, wb_im_ref,
                          wc_re_ref, wc_imn_ref, d_ref, wglu_ref, perm_ref, permt_ref,
                          out_ref.at[:, ts, :], st_re_ref, st_im_ref, bu_re_all.at[sub], bu_im_all.at[sub],
                          lc=lc, bsz=bsz, lane_chunk=lane_chunk)


def _ssm_prompt_chunk(x3, g_ref, lam_re_ref, lam_im_ref, wb_re_ref, wb_im_ref,
                      wc_re_ref, wc_imn_ref, d_ref, wglu_ref, perm_ref, permt_ref,
                      out_ref, st_re_ref, st_im_ref, bu_re, bu_im, *, lc, bsz, lane_chunk):
    d = x3.shape[2]
    x = x3.reshape(bsz * lc, d)
    u = _rms(x, g_ref[...])
    ub_tb = _dot(perm_ref[...], u.astype(BF16)).astype(BF16)
    _ssm_in_proj(ub_tb, wb_re_ref, wb_im_ref, bu_re, bu_im)

    gp = bu_re.shape[1]
    for c0 in range(0, gp, lane_chunk):
        sl = slice(c0, c0 + lane_chunk)
        lre = jnp.broadcast_to(lam_re_ref[:, sl], (bsz, lane_chunk))
        lim = jnp.broadcast_to(lam_im_ref[:, sl], (bsz, lane_chunk))
        hr = st_re_ref[:, sl]
        hi = st_im_ref[:, sl]
        for t in range(lc):
            rows = slice(t * bsz, (t + 1) * bsz)
            hr, hi = (lre * hr - lim * hi + bu_re[rows, sl], lre * hi + lim * hr + bu_im[rows, sl])
            bu_re[rows, sl] = hr
            bu_im[rows, sl] = hi
        st_re_ref[:, sl] = hr
        st_im_ref[:, sl] = hi

    permt = permt_ref[...]
    hi, mid, lo = _split3(_ssm_out_proj(bu_re, bu_im, wc_re_ref, wc_imn_ref))
    y = (_dot(permt, hi) + _dot(permt, mid)) + _dot(permt, lo)
    out_ref[...] = _ssm_glu(x, u, y, d_ref, wglu_ref).reshape(bsz, lc, d)


def _ssm_sample_kernel(x_ref, g_ref, lam_re_ref, lam_im_ref, h0_re_ref, h0_im_ref, wb_re_ref, wb_im_ref,
                       wc_re_ref, wc_imn_ref, d_ref, wglu_ref,
                       out_ref, h_re_ref, h_im_ref):
    x = x_ref[...]
    u = _rms(x, g_ref[...])
    _ssm_in_proj(u.astype(BF16), wb_re_ref, wb_im_ref, h_re_ref, h_im_ref)
    lre = lam_re_ref[...]
    lim = lam_im_ref[...]
    h0r = h0_re_ref[...]
    h0i = h0_im_ref[...]
    h_re_ref[...] = lre * h0r - lim * h0i + h_re_ref[...]
    h_im_ref[...] = lre * h0i + lim * h0r + h_im_ref[...]
    y = _ssm_out_proj(h_re_ref, h_im_ref, wc_re_ref, wc_imn_ref)
    n = x.shape[0]
    out_ref[0:n, :] = _ssm_glu(x, u, y, d_ref, wglu_ref)
    out_ref[n:, :] = jnp.zeros((out_ref.shape[0] - n, out_ref.shape[1]), F32)


def _ssm_prompt(x, g, lam_re, lam_im, wb_re, wb_im, wc_re, wc_imn, d_skip, wglu, *, lc, nsub):
    bsz, seq, d = x.shape
    assert seq % (lc * nsub) == 0
    gp = lam_re.shape[1]
    r = lc * bsz
    perm = np.zeros((r, r), np.float32)
    for b in range(bsz):
        for t in range(lc):
            perm[t * bsz + b, b * lc + t] = 1.0
    permt = jnp.asarray(perm.T, BF16)
    perm = jnp.asarray(perm, BF16)
    kern = functools.partial(_ssm_prompt_kernel, lc=lc, bsz=bsz, lane_chunk=8 * LANES)
    return pl.pallas_call(
        kern,
        grid=(seq // (lc * nsub),),
        in_specs=[pl.BlockSpec((bsz, lc * nsub, d), lambda c: (0, c, 0)),
                  _const_spec(g.shape), _const_spec(lam_re.shape), _const_spec(lam_im.shape),
                  _const_spec(wb_re.shape), _const_spec(wb_im.shape),
                  _const_spec(wc_re.shape), _const_spec(wc_imn.shape),
                  _const_spec(d_skip.shape), _const_spec(wglu.shape),
                  _const_spec(perm.shape), _const_spec(permt.shape)],
        out_specs=[pl.BlockSpec((bsz, lc * nsub, d), lambda c: (0, c, 0)),
                   _const_spec((bsz, gp)), _const_spec((bsz, gp))],
        out_shape=[jax.ShapeDtypeStruct((bsz, seq, d), F32),
                   jax.ShapeDtypeStruct((bsz, gp), F32), jax.ShapeDtypeStruct((bsz, gp), F32)],
        scratch_shapes=[pltpu.VMEM((nsub, r, gp), F32), pltpu.VMEM((nsub, r, gp), F32)],
        compiler_params=_params("arbitrary"),
        name="s5_prompt",
    )(x, g, lam_re, lam_im, wb_re, wb_im, wc_re, wc_imn, d_skip, wglu, perm, permt)


def _ssm_sample(x, g, lam_re, lam_im, h0_re, h0_im, wb_re, wb_im, wc_re, wc_imn, d_skip, wglu, *, pad_rows):
    n, d = x.shape
    gp = lam_re.shape[1]
    return pl.pallas_call(
        _ssm_sample_kernel,
        out_shape=[jax.ShapeDtypeStruct((pad_rows, d), F32),
                   jax.ShapeDtypeStruct((n, gp), F32), jax.ShapeDtypeStruct((n, gp), F32)],
        compiler_params=_params(),
        name="s5_sample",
    )(x, g, lam_re, lam_im, h0_re, h0_im, wb_re, wb_im, wc_re, wc_imn, d_skip, wglu)


def _ffn_chunks(d_ff):
    step = 3 * MXU_DIM
    return [(c, min(c + step, d_ff)) for c in range(0, d_ff, step)]


def _ffn_kernel(xp_ref, xs_ref, gffn_ref, wgu_ref, wd_ref, gkv_ref, wkv_ref, gq_ref, wq_ref,
                x2_ref, kv_ref, q_ref):
    x = jnp.where(pl.program_id(0) < pl.num_programs(0) - 1, xp_ref[...], xs_ref[...])
    hb = _rms(x, gffn_ref[...]).astype(BF16)
    d_ff = wd_ref.shape[0]
    acc = None
    for c0, c1 in _ffn_chunks(d_ff):
        a = _dot(hb, wgu_ref[:, c0:c1])
        b = _dot(hb, wgu_ref[:, d_ff + c0:d_ff + c1])
        part = _dot((jax.nn.silu(a) * b).astype(BF16), wd_ref[c0:c1, :])
        acc = part if acc is None else acc + part
    x2 = x + acc
    x2_ref[...] = x2
    kv_ref[...] = _dot(_rms(x2, gkv_ref[...]).astype(BF16), wkv_ref[...])
    q_ref[...] = _dot(_rms(x2, gq_ref[...]).astype(BF16), wq_ref[...]).astype(BF16)


def _ffn(xp, xs, gffn, wgu, wd, gkv, wkv, gq, wq):
    tm, d = xs.shape
    npt = xp.shape[0] // tm
    rows = xp.shape[0] + tm
    kvw = wkv.shape[1]
    nq = wq.shape[1]
    return pl.pallas_call(
        _ffn_kernel,
        grid=(npt + 1,),
        in_specs=[pl.BlockSpec((tm, d), lambda i: (jnp.minimum(i, npt - 1), 0)),
                  _const_spec(xs.shape),
                  _const_spec(gffn.shape), _const_spec(wgu.shape), _const_spec(wd.shape),
                  _const_spec(gkv.shape), _const_spec(wkv.shape),
                  _const_spec(gq.shape), _const_spec(wq.shape)],
        out_specs=[pl.BlockSpec((tm, d), lambda i: (i, 0)),
                   pl.BlockSpec((tm, kvw), lambda i: (i, 0)),
                   pl.BlockSpec((tm, nq), lambda i: (i, 0))],
        out_shape=[jax.ShapeDtypeStruct((rows, d), F32),
                   jax.ShapeDtypeStruct((rows, kvw), F32),
                   jax.ShapeDtypeStruct((rows, nq), BF16)],
        compiler_params=_params("arbitrary"),
        name="ffn_kv_q",
    )(xp, xs, gffn, wgu, wd, gkv, wkv, gq, wq)


def _t5_bucket(dist, num_buckets):
    max_exact = num_buckets // 2
    d = jnp.maximum(dist, 0)
    large = max_exact + (jnp.log(jnp.maximum(d, 1).astype(F32) / max_exact)
                         / math.log(MAX_DISTANCE / max_exact) * (num_buckets - max_exact)).astype(jnp.int32)
    large = jnp.minimum(large, num_buckets - 1)
    return jnp.where(d < max_exact, d, large)


def _bias_kernel(bm_ref, bs_ref, rb_ref, bias_ref, bias_s_ref, *, kvh, rep):
    nb, nh = rb_ref.shape
    bm = bm_ref[...]
    bs = bs_ref[...]

    def lookup(buckets, h):
        def body(k, acc):
            return jnp.where(buckets == k, rb_ref[k, h], acc)
        return lax.fori_loop(0, nb, body, jnp.zeros(buckets.shape, F32))

    w = bm.shape[0]
    for g in range(kvh):
        for r in range(rep):
            bias_ref[g, :, r * w:(r + 1) * w] = lookup(bm, g * rep + r) * math.log2(math.e)
            bias_s_ref[g, r:r + 1, :] = lookup(bs, g * rep + r)


def _bias_tables(rel_bias, window, kvh, rep):
    nb, nh = rel_bias.shape
    key = jnp.arange(window)[:, None]
    qry = jnp.arange(window)[None, :]
    bm = _t5_bucket((qry - key) % window, nb).astype(jnp.int32)
    bs = _t5_bucket(window - 1 - qry, nb).astype(jnp.int32)
    kern = functools.partial(_bias_kernel, kvh=kvh, rep=rep)
    return pl.pallas_call(
        kern,
        in_specs=[pl.BlockSpec(memory_space=pltpu.VMEM), pl.BlockSpec(memory_space=pltpu.VMEM),
                  pl.BlockSpec(memory_space=pltpu.SMEM)],
        out_shape=[jax.ShapeDtypeStruct((kvh, window, rep * window), F32),
                   jax.ShapeDtypeStruct((kvh, rep, window), F32)],
        name="t5_bias",
    )(bm, bs, rel_bias)


def _attn_prompt_kernel(q_ref, kvc_ref, kvp_ref, bias_ref, sink_ref, ot_ref, *, kvh, rep, hd, scale):
    w = kvp_ref.shape[0]
    kv_all = kvc_ref[...]
    for sub in range(q_ref.shape[0] // w):
        rows = slice(sub * w, (sub + 1) * w)
        kvp = kvp_ref[...] if sub == 0 else kv_all[(sub - 1) * w:sub * w]
        first = pl.program_id(1) == 0 if sub == 0 else None
        _attn_prompt_block(q_ref[rows, :], kv_all[rows], kvp, first, bias_ref, sink_ref, ot_ref, sub * w,
                           kvh=kvh, rep=rep, hd=hd, scale=scale)


def _attn_prompt_block(q, kvc, kvp, first, bias_ref, sink_ref, ot_ref, col0, *, kvh, rep, hd, scale):
    w = q.shape[0]
    kvw = kvh * hd
    lane = lax.broadcasted_iota(jnp.int32, (1, LANES), 1)
    key = lax.broadcasted_iota(jnp.int32, (w, rep * w), 0)
    qry = lax.broadcasted_iota(jnp.int32, (w, rep * w), 1) % w
    upper = key > qry
    mask_add = None if first is None else jnp.where(jnp.logical_and(upper, first), NEG_INF, 0.0)
    log2e = math.log2(math.e)
    heads_per_blk = LANES // hd
    for p in range(kvw // LANES):
        cs = slice(p * LANES, (p + 1) * LANES)
        kc, kp = kvc[:, cs], kvp[:, cs]
        vs = slice(kvw + p * LANES, kvw + (p + 1) * LANES)
        vb = jnp.concatenate([kvp[:, vs], kvc[:, vs]], axis=0).astype(BF16)
        qg = jnp.concatenate([q[:, r * kvw + p * LANES:r * kvw + (p + 1) * LANES] for r in range(rep)], axis=0)
        halves = []
        for half in range(heads_per_blk):
            g = p * heads_per_blk + half
            lmask = jnp.logical_and(lane >= half * hd, lane < (half + 1) * hd)
            kb = jnp.concatenate([jnp.where(lmask, kp, 0.0), jnp.where(lmask, kc, 0.0)], axis=0).astype(BF16)
            s = lax.dot_general(kb, qg, (((1,), (1,)), ((), ())), preferred_element_type=F32)
            bias = bias_ref[g] if first is None else bias_ref[g] + mask_add
            sc = jnp.where(upper, s[:w], s[w:]) * (scale * log2e) + bias
            sink = jnp.concatenate([jnp.full((1, w), sink_ref[g * rep + r] * log2e, F32) for r in range(rep)],
                                   axis=1)
            m = jnp.maximum(jnp.max(sc, axis=0, keepdims=True), sink)
            pe = jnp.exp2(sc - m)
            denom = jnp.sum(pe, axis=0, keepdims=True) + jnp.exp2(sink - m)
            pcat = jnp.concatenate([jnp.where(upper, pe, 0.0), jnp.where(upper, 0.0, pe)], axis=0).astype(BF16)
            og = lax.dot_general(vb, pcat, (((0,), (0,)), ((), ())), preferred_element_type=F32)
            halves.append((og / denom)[half * hd:(half + 1) * hd])
        o_blk = jnp.concatenate(halves, axis=0).astype(BF16)
        for r in range(rep):
            ot_ref[r * kvw + p * LANES:r * kvw + (p + 1) * LANES, col0:col0 + w] = o_blk[:, r * w:(r + 1) * w]


def _attn_prompt(q, kv, bias, sinks, *, bsz, nblk, window, kvh, rep, hd):
    rows, nq = q.shape
    kvw2 = kv.shape[1]
    nsub = next(n for n in (4, 2, 1) if nblk % n == 0 and rows % (n * window) == 0)
    nstep = nblk // nsub
    kern = functools.partial(_attn_prompt_kernel, kvh=kvh, rep=rep, hd=hd, scale=1.0 / math.sqrt(hd))
    return pl.pallas_call(
        kern,
        grid=(bsz, nstep),
        in_specs=[pl.BlockSpec((nsub * window, nq), lambda b, i: (b * nstep + i, 0)),
                  pl.BlockSpec((nsub * window, kvw2), lambda b, i: (b * nstep + i, 0)),
                  pl.BlockSpec((window, kvw2), lambda b, i: (b * nblk + jnp.maximum(nsub * i - 1, 0), 0)),
                  _const_spec(bias.shape),
                  pl.BlockSpec(memory_space=pltpu.SMEM)],
        out_specs=pl.BlockSpec((nq, nsub * window), lambda b, i: (0, b * nstep + i)),
        out_shape=jax.ShapeDtypeStruct((nq, rows), BF16),
        compiler_params=_params("arbitrary", "arbitrary"),
        name="swa_prompt",
    )(q, kv, kv, bias, sinks)


def _attn_sample_kernel(q_ref, kv_ref, ck_ref, cv_ref, bias_ref, sink_ref, o_ref, nk_ref, nv_ref,
                        *, kvh, rep, hd, scale):
    nb, w, kvw = ck_ref.shape
    kv = kv_ref[...]
    lane = lax.broadcasted_iota(jnp.int32, (nb, rep, kvw), 2)

    def shifted(c_ref, n_ref, new):
        flat = c_ref[...].reshape(nb * w, kvw)
        n_ref[...] = pltpu.roll(flat, nb * w - 1, axis=0).reshape(nb, w, kvw)
        n_ref[:, w - 1:w, :] = new
        return n_ref[...].astype(BF16)

    nkb = shifted(ck_ref, nk_ref, kv[:, :, :kvw])
    nvb = shifted(cv_ref, nv_ref, kv[:, :, kvw:])
    q = q_ref[...].astype(F32)
    o = jnp.zeros((nb, rep, kvw), F32)
    for g in range(kvh):
        lmask = jnp.logical_and(lane >= g * hd, lane < (g + 1) * hd)
        qg = jnp.where(lmask, q, 0.0).astype(BF16)
        s = jnp.einsum("nrc,njc->nrj", qg, nkb, preferred_element_type=F32)
        sc = s * scale + bias_ref[g][None]
        sink = sink_ref[g][None]
        m = jnp.maximum(jnp.max(sc, axis=-1, keepdims=True), sink)
        pe = jnp.exp(sc - m)
        probs = pe / (jnp.sum(pe, axis=-1, keepdims=True) + jnp.exp(sink - m))
        og = jnp.einsum("nrj,njc->nrc", probs.astype(BF16), nvb, preferred_element_type=F32)
        o = jnp.where(lmask, og, o)
    o_ref[...] = o.astype(BF16)


def _attn_sample(q3, kv3, ck, cv, bias_s, sink_s, *, nb, kvh, rep, hd):
    n, w, kvw = ck.shape
    kern = functools.partial(_attn_sample_kernel, kvh=kvh, rep=rep, hd=hd, scale=1.0 / math.sqrt(hd))
    cache_spec = pl.BlockSpec((nb, w, kvw), lambda i: (i, 0, 0))
    return pl.pallas_call(
        kern,
        grid=(n // nb,),
        in_specs=[pl.BlockSpec((nb, rep, kvw), lambda i: (i, 0, 0)),
                  pl.BlockSpec((nb, 1, 2 * kvw), lambda i: (i, 0, 0)),
                  cache_spec, cache_spec,
                  _const_spec(bias_s.shape), _const_spec(sink_s.shape)],
        out_specs=[pl.BlockSpec((nb, rep, kvw), lambda i: (i, 0, 0)), cache_spec, cache_spec],
        out_shape=[jax.ShapeDtypeStruct((n, rep, kvw), BF16),
                   jax.ShapeDtypeStruct((n, w, kvw), F32), jax.ShapeDtypeStruct((n, w, kvw), F32)],
        compiler_params=_params("arbitrary"),
        name="swa_sample",
    )(q3, kv3, ck, cv, bias_s, sink_s)


CHUNK = 16
BIG_COPY = 128
SUB = 256
UP_ROW_BLOCK = 1024
DOWN_ROW_BLOCK = 1024
TOKEN_TILE = 512
MOE_TILE = 512


def _chunk_copy(src, src_row, dst, dst_row, sem, rows=CHUNK):
    return pltpu.make_async_copy(src.at[pl.ds(pl.multiple_of(src_row, CHUNK), rows), :],
                                 dst.at[pl.ds(pl.multiple_of(dst_row, CHUNK), rows), :], sem)


def _start_copies(src, src_row, dst, dst_row, sem, nrows):
    nbig = lax.shift_right_logical(nrows, int(math.log2(BIG_COPY)))

    def big(c, carry):
        _chunk_copy(src, src_row + c * BIG_COPY, dst, dst_row + c * BIG_COPY, sem, BIG_COPY).start()
        return carry

    lax.fori_loop(0, nbig, big, 0)
    done = nbig * BIG_COPY

    def small(c, carry):
        _chunk_copy(src, src_row + done + c * CHUNK, dst, dst_row + done + c * CHUNK, sem).start()
        return carry

    lax.fori_loop(0, lax.shift_right_logical(nrows - done, int(math.log2(CHUNK))), small, 0)


def _moe_route_kernel(x2_ref, ot_ref, wo_ref, gffn_ref, wr_ref, lst_ref, ust_ref,
                      x3_ref, cm_ref, tmeta_ref, xs_hbm,
                      comp_s, carry_s, zero_s, base_v, pend_sm, sem, *, ne, cap, n_valid):
    i = pl.program_id(0)
    nt = pl.num_programs(0)
    tm = x2_ref.shape[0]
    w = comp_s.shape[1]
    slot = lax.rem(i, 2)

    @pl.when(i == 0)
    def _():
        base_v[...] = jnp.zeros_like(base_v)
        zero_s[...] = jnp.zeros_like(zero_s)
        carry_s[...] = jnp.zeros_like(carry_s)

    def drain(n):
        def body(c, carry):
            _chunk_copy(zero_s, 0, xs_hbm, 0, sem).wait()
            return carry
        lax.fori_loop(0, n, body, 0)

    @pl.when(i > 0)
    def _():
        drain(pend_sm[0])

    x3 = x2_ref[...] + lax.dot_general(ot_ref[...], wo_ref[...], (((0,), (0,)), ((), ())),
                                       preferred_element_type=F32)
    x3_ref[...] = x3
    hb = _rms(x3, gffn_ref[...]).astype(BF16)

    lane = lax.broadcasted_iota(jnp.int32, (tm, LANES), 1)
    logits = jnp.where(lane < ne, _dot(hb, wr_ref[...]), -jnp.inf)
    m1 = jnp.max(logits, axis=-1, keepdims=True)
    i1 = jnp.min(jnp.where(logits == m1, lane, LANES), axis=-1, keepdims=True)
    rest = jnp.where(lane == i1, -jnp.inf, logits)
    m2 = jnp.max(rest, axis=-1, keepdims=True)
    i2 = jnp.min(jnp.where(rest == m2, lane, LANES), axis=-1, keepdims=True)
    e2 = jnp.exp(m2 - m1)
    g1 = 1.0 / (1.0 + e2)
    g2 = e2 / (1.0 + e2)

    live = i * tm + lax.broadcasted_iota(jnp.int32, (tm, 1), 0) < n_valid
    sel = jnp.where(jnp.logical_and(live, jnp.logical_or(lane == i1, lane == i2)), 1.0, 0.0)
    rank = _dot(lst_ref[...], sel.astype(BF16))
    cnt = jnp.sum(sel, axis=0, keepdims=True)
    fill = base_v[...]
    rem = fill - jnp.floor(fill / CHUNK) * CHUNK
    cpad = jnp.floor((rem + cnt + (CHUNK - 1)) / CHUNK) * CHUNK
    loff = _dot(jnp.broadcast_to(cpad, (8, LANES)).astype(BF16), ust_ref[...])[0:1]
    dest = loff + rem + rank
    ld1 = jnp.where(live, jnp.sum(jnp.where(lane == i1, dest, 0.0), axis=-1, keepdims=True), -1.0)
    ld2 = jnp.where(live, jnp.sum(jnp.where(lane == i2, dest, 0.0), axis=-1, keepdims=True), -1.0)
    cm = jnp.where(lane == 0, ld1, jnp.where(lane == 1, ld2, jnp.where(lane == 2, g1, jnp.where(lane == 3, g2, 0.0))))
    cm_ref[...] = cm

    rm = cm.T
    rowi = lax.broadcasted_iota(jnp.int32, (w, tm), 0).astype(F32)
    place = (jnp.where(rowi == rm[0:1], 1.0, 0.0) + jnp.where(rowi == rm[1:2], 1.0, 0.0)).astype(BF16)
    comp_s[slot] = _dot(place, hb).astype(BF16)

    base = fill - rem
    srow = lax.broadcasted_iota(jnp.int32, (8, LANES), 0)
    tmeta_ref[...] = jnp.where(srow == 0, cpad, jnp.where(srow == 1, base, jnp.where(srow == 2, fill + cnt, 0.0))
                               ).astype(jnp.int32)
    base_v[...] = fill + cnt

    cpad_i = cpad.astype(jnp.int32)
    loff_i = loff.astype(jnp.int32)
    base_i = base.astype(jnp.int32)
    tail_i = (rem + cnt - jnp.floor((rem + cnt) / CHUNK) * CHUNK).astype(jnp.int32)
    total = 0
    tails = []
    for e in range(ne):
        n_e = cpad_i[0, e]
        src0 = loff_i[0, e]
        dst0 = base_i[0, e] + e * cap

        @pl.when(n_e > 0)
        def _(e=e, n_e=n_e, src0=src0, partial=tail_i[0, e] > 0):
            head = pl.ds(pl.multiple_of(src0, CHUNK), CHUNK)
            comp_s[slot, head, :] = comp_s[slot, head, :] + carry_s[e]
            last = comp_s[slot, pl.ds(pl.multiple_of(src0 + n_e - CHUNK, CHUNK), CHUNK), :]
            carry_s[e] = jnp.where(partial, last, jnp.zeros_like(last))

        _start_copies(comp_s.at[slot], src0, xs_hbm, dst0, sem, n_e)
        total = total + lax.shift_right_logical(n_e, int(math.log2(CHUNK)))
        tails.append(dst0 + n_e)
    pend_sm[0] = total

    @pl.when(i == nt - 1)
    def _():
        nz_total = 0
        for e in range(ne):
            end = tails[e]
            nz = lax.shift_right_logical(lax.rem(SUB - lax.rem(end, SUB), SUB), int(math.log2(CHUNK)))

            def zstart(c, carry, end=end):
                _chunk_copy(zero_s, 0, xs_hbm, end + c * CHUNK, sem).start()
                return carry

            lax.fori_loop(0, nz, zstart, 0)
            nz_total = nz_total + nz
        drain(total + nz_total)


def _moe_route(x2, ot, wo, gffn, wr_pad, *, tm, ne, cap, n_valid):
    rows, d = x2.shape
    nt = rows // tm
    w = _staging_rows(tm, ne)
    lst = jnp.asarray(np.tril(np.ones((tm, tm), np.float32), -1), BF16)
    ust = jnp.asarray(np.triu(np.ones((LANES, LANES), np.float32), 1), BF16)
    kern = functools.partial(_moe_route_kernel, ne=ne, cap=cap, n_valid=n_valid)
    return pl.pallas_call(
        kern,
        grid=(nt,),
        in_specs=[pl.BlockSpec((tm, d), lambda i: (i, 0)),
                  pl.BlockSpec((ot.shape[0], tm), lambda i: (0, i)),
                  _const_spec(wo.shape), _const_spec(gffn.shape), _const_spec(wr_pad.shape),
                  _const_spec(lst.shape), _const_spec(ust.shape)],
        out_specs=[pl.BlockSpec((tm, d), lambda i: (i, 0)),
                   pl.BlockSpec((tm, LANES), lambda i: (i, 0)),
                   pl.BlockSpec((8, LANES), lambda i: (i, 0)),
                   pl.BlockSpec(memory_space=pl.ANY)],
        out_shape=[jax.ShapeDtypeStruct((rows, d), F32),
                   jax.ShapeDtypeStruct((rows, LANES), F32),
                   jax.ShapeDtypeStruct((nt * 8, LANES), jnp.int32),
                   jax.ShapeDtypeStruct((ne * cap, d), BF16)],
        scratch_shapes=[pltpu.VMEM((2, w, d), BF16), pltpu.VMEM((ne, CHUNK, d), BF16), pltpu.VMEM((CHUNK, d), BF16),
                        pltpu.VMEM((1, LANES), F32), pltpu.SMEM((1,), jnp.int32),
                        pltpu.SemaphoreType.DMA(())],
        compiler_params=_params("arbitrary"),
        name="moe_route",
    )(x2, ot, wo, gffn, wr_pad, lst, ust)


def _staging_rows(tm, ne):
    return -(-(TOP_K * tm + 2 * ne * (CHUNK - 1)) // LANES) * LANES


def _max_blocks(rows, ne, row_block):
    return TOP_K * rows // row_block + ne


def _expert_steps(nrows, nf, max_blocks, row_block):
    ne = nrows.shape[0]
    nblk = (nrows + row_block - 1) // row_block
    cum = jnp.cumsum(nblk) * nf
    total = cum[-1]
    s = jnp.minimum(jnp.arange(max_blocks * nf, dtype=jnp.int32), total - 1)
    e = jnp.minimum(jnp.sum((s[:, None] >= cum[None, :]).astype(jnp.int32), axis=1), ne - 1)
    nb_e = nblk[e]
    within = s - (cum[e] - nb_e * nf)
    f = within // nb_e
    pos = within - f * nb_e
    r = (pos + nb_e - 1) % nb_e
    nsub = jnp.clip((nrows[e] - r * row_block + SUB - 1) // SUB, 0, row_block // SUB)
    first = (pos == 0).astype(jnp.int32)
    return (e, f.astype(jnp.int32), r.astype(jnp.int32), first, nsub.astype(jnp.int32),
            jnp.reshape(total, (1,)).astype(jnp.int32))


def _moe_up_kernel(e_ref, f_ref, r_ref, first_ref, nsub_ref, n_ref, xs_ref, wg_ref, wu_ref, act_ref, wgb_s, wub_s):
    s = pl.program_id(0)

    @pl.when(s < n_ref[0])
    def _():
        @pl.when(first_ref[s] == 1)
        def _():
            wgb_s[...] = wg_ref[...].astype(BF16)
            wub_s[...] = wu_ref[...].astype(BF16)

        def gate_up(row0, nrows):
            rows = pl.ds(pl.multiple_of(row0, SUB), nrows)
            xsb = xs_ref[rows, :]
            a = _dot(xsb, wgb_s[...])
            b = _dot(xsb, wub_s[...])
            act_ref[rows, :] = (jax.nn.silu(a) * b).astype(BF16)

        nsub = nsub_ref[s]
        npair = lax.shift_right_logical(nsub, 1)

        def body(j, carry):
            gate_up(j * (2 * SUB), 2 * SUB)
            return carry

        lax.fori_loop(0, npair, body, 0)

        @pl.when(nsub > 2 * npair)
        def _():
            gate_up(npair * (2 * SUB), SUB)


def _moe_up(steps, xs, wgu, *, fc, cap, row_block):
    ne, d, dff2 = wgu.shape
    dff = dff2 // 2
    nf = dff // fc
    cb = cap // row_block
    grid_spec = pltpu.PrefetchScalarGridSpec(
        num_scalar_prefetch=6,
        grid=(steps[0].shape[0],),
        in_specs=[pl.BlockSpec((row_block, d), lambda s, e, f, r, *_: (e[s] * cb + r[s], 0)),
                  pl.BlockSpec((None, d, fc), lambda s, e, f, r, *_: (e[s], 0, f[s])),
                  pl.BlockSpec((None, d, fc), lambda s, e, f, r, *_: (e[s], 0, nf + f[s]))],
        out_specs=pl.BlockSpec((row_block, fc), lambda s, e, f, r, *_: (e[s] * cb + r[s], f[s])),
        scratch_shapes=[pltpu.VMEM((d, fc), BF16), pltpu.VMEM((d, fc), BF16)])
    return pl.pallas_call(
        _moe_up_kernel,
        grid_spec=grid_spec,
        out_shape=jax.ShapeDtypeStruct((ne * cap, dff), BF16),
        compiler_params=_params("arbitrary"),
        name="moe_up",
    )(*steps, xs, wgu, wgu)


def _moe_down_kernel(e_ref, f_ref, r_ref, first_ref, nsub_ref, n_ref, act_ref, wd_ref, ys_ref, wdb_s):
    s = pl.program_id(0)

    @pl.when(s < n_ref[0])
    def _():
        @pl.when(first_ref[s] == 1)
        def _():
            wdb_s[...] = wd_ref[...].astype(BF16)

        def project(row0, nrows):
            rows = pl.ds(pl.multiple_of(row0, SUB), nrows)
            ys_ref[rows, :] = _dot(act_ref[rows, :], wdb_s[...]).astype(BF16)

        nsub = nsub_ref[s]
        npair = lax.shift_right_logical(nsub, 1)

        def body(j, carry):
            project(j * (2 * SUB), 2 * SUB)
            return carry

        lax.fori_loop(0, npair, body, 0)

        @pl.when(nsub > 2 * npair)
        def _():
            project(npair * (2 * SUB), SUB)


def _moe_down(steps, act, wd, *, cap, row_block):
    ne, dff, d = wd.shape
    cb = cap // row_block
    grid_spec = pltpu.PrefetchScalarGridSpec(
        num_scalar_prefetch=6,
        grid=(steps[0].shape[0],),
        in_specs=[pl.BlockSpec((row_block, dff), lambda s, e, f, r, *_: (e[s] * cb + r[s], 0)),
                  pl.BlockSpec((None, dff, d), lambda s, e, f, r, *_: (e[s], 0, 0))],
        out_specs=pl.BlockSpec((row_block, d), lambda s, e, f, r, *_: (e[s] * cb + r[s], 0)),
        scratch_shapes=[pltpu.VMEM((dff, d), BF16)])
    return pl.pallas_call(
        _moe_down_kernel,
        grid_spec=grid_spec,
        out_shape=jax.ShapeDtypeStruct((ne * cap, d), BF16),
        compiler_params=_params("arbitrary", vmem=2 * (dff * d * 4 + row_block * (dff + d) * 2) + dff * d * 2
                                + 4 * SUB * d * 4),
        name="moe_down",
    )(*steps, act, wd)


def _moe_combine_kernel(cpad_ref, seg_ref, x3_ref, cm_ref, gfin_ref, ys_hbm, yp_ref, ys_ref, yloc_s, sem,
                        *, ne, npt):
    i = pl.program_id(0)
    nt = pl.num_programs(0)
    tm = x3_ref.shape[0]
    w = yloc_s.shape[1]
    slot = lax.rem(i, 2)
    shift = int(math.log2(CHUNK))

    def issue(t, sl):
        off = 0
        for e in range(ne):
            n_e = cpad_ref[t * ne + e]
            _start_copies(ys_hbm, seg_ref[t * ne + e], yloc_s.at[sl], off, sem.at[sl], n_e)
            off = off + n_e

    @pl.when(i == 0)
    def _():
        yloc_s[...] = jnp.zeros_like(yloc_s)
        issue(0, 0)

    @pl.when(i + 1 < nt)
    def _():
        issue(i + 1, 1 - slot)

    total = 0
    for e in range(ne):
        total = total + lax.shift_right_logical(cpad_ref[i * ne + e], shift)

    def wait(c, carry):
        _chunk_copy(ys_hbm, 0, yloc_s.at[slot], 0, sem.at[slot]).wait()
        return carry

    lax.fori_loop(0, total, wait, 0)

    cm = cm_ref[...]
    col = lax.broadcasted_iota(jnp.int32, (tm, w), 1).astype(F32)
    yl = yloc_s[slot]
    y1 = _dot(jnp.where(col == cm[:, 0:1], 1.0, 0.0).astype(BF16), yl)
    y2 = _dot(jnp.where(col == cm[:, 1:2], 1.0, 0.0).astype(BF16), yl)
    y = _rms(x3_ref[...] + cm[:, 2:3] * y1 + cm[:, 3:4] * y2, gfin_ref[...])

    @pl.when(i < npt)
    def _():
        yp_ref[...] = y

    @pl.when(i >= npt)
    def _():
        ys_ref[...] = y


def _moe_combine(cpad, seg, x3, cm, gfin, ys, *, tm, ne, sample_rows):
    rows, d = x3.shape
    nt = rows // tm
    npt = (rows - sample_rows) // tm
    w = _staging_rows(tm, ne)
    kern = functools.partial(_moe_combine_kernel, ne=ne, npt=npt)
    grid_spec = pltpu.PrefetchScalarGridSpec(
        num_scalar_prefetch=2,
        grid=(nt,),
        in_specs=[pl.BlockSpec((tm, d), lambda i, *_: (i, 0)),
                  pl.BlockSpec((tm, LANES), lambda i, *_: (i, 0)),
                  pl.BlockSpec(gfin.shape, lambda i, *_: (0, 0)),
                  pl.BlockSpec(memory_space=pl.ANY)],
        out_specs=[pl.BlockSpec((tm, d), lambda i, *_: (jnp.minimum(i, npt - 1), 0)),
                   pl.BlockSpec((tm, d), lambda i, *_: (jnp.maximum(i - npt, 0), 0))],
        scratch_shapes=[pltpu.VMEM((2, w, d), BF16), pltpu.SemaphoreType.DMA((2,))])
    return pl.pallas_call(
        kern,
        grid_spec=grid_spec,
        out_shape=[jax.ShapeDtypeStruct((npt * tm, d), F32), jax.ShapeDtypeStruct((sample_rows, d), F32)],
        compiler_params=_params("arbitrary"),
        name="moe_combine",
    )(cpad, seg, x3, cm, gfin, ys)


def _moe(x2, ot, wo, gffn, wr, wgu, wd, gfin, *, tm, fc, n_valid, sample_rows):
    rows, d = x2.shape
    ne = wr.shape[1]
    nt = rows // tm
    nf = wgu.shape[2] // 2 // fc
    cap = -(-(rows + SUB) // UP_ROW_BLOCK) * UP_ROW_BLOCK
    wr_pad = jnp.pad(wr, ((0, 0), (0, LANES - ne))).astype(BF16)
    x3, cm, tmeta, xs = _moe_route(x2, ot, wo, gffn, wr_pad, tm=tm, ne=ne, cap=cap, n_valid=n_valid)
    tmeta = tmeta.reshape(nt, 8, LANES)[:, :, :ne]
    cpad = tmeta[:, 0, :].reshape(nt * ne)
    seg = (tmeta[:, 1, :] + jnp.arange(ne, dtype=jnp.int32)[None, :] * cap).reshape(nt * ne)
    nrows = tmeta[nt - 1, 2, :]
    up_steps = _expert_steps(nrows, nf, _max_blocks(rows, ne, UP_ROW_BLOCK), UP_ROW_BLOCK)
    act = _moe_up(up_steps, xs, wgu, fc=fc, cap=cap, row_block=UP_ROW_BLOCK)
    down_steps = _expert_steps(nrows, 1, _max_blocks(rows, ne, DOWN_ROW_BLOCK), DOWN_ROW_BLOCK)
    ys = _moe_down(down_steps, act, wd, cap=cap, row_block=DOWN_ROW_BLOCK)
    return _moe_combine(cpad, seg, x3, cm, gfin, ys, tm=tm, ne=ne, sample_rows=sample_rows)


def kernel(x_prompt, x_sample, state_ssm_re, state_ssm_im, cache_k_win, cache_v_win, g_mix, g_ffn, g_kv, g_final, ssm_a_re, ssm_a_im, ssm_log_dt, ssm_b_re, ssm_b_im, ssm_c_re, ssm_c_im, ssm_d, w_glu, w_kv, w_q, w_o, attn_sinks, rel_bias, w_ffn_gate_up, w_ffn_down, w_router, w_exp_gate_up, w_exp_down):
    bsz, seq, d = x_prompt.shape
    ns, dec_seq, _ = x_sample.shape
    assert dec_seq == 1 and g_mix.shape[0] == 2 and ssm_a_re.shape[0] == 1 and w_q.shape[0] == 1
    _, g, p = ssm_a_re.shape
    gp = g * p
    window, kvh, hd = cache_k_win.shape[1:]
    kvw = kvh * hd
    nh = attn_sinks.shape[1]
    rep = nh // kvh
    nq = nh * hd
    assert bsz == 8 and ns % bsz == 0 and seq % window == 0 and LANES % hd == 0

    lam_re, lam_im, wb_re, wb_im, wc_re, wc_imn = _zoh(ssm_a_re[0], ssm_a_im[0], ssm_log_dt[0],
                                                        ssm_b_re[0], ssm_b_im[0], ssm_c_re[0], ssm_c_im[0])
    wglu = w_glu[0].astype(BF16)
    d_skip = ssm_d[0].reshape(1, d)
    wgu = w_ffn_gate_up[0].astype(BF16)
    wd = w_ffn_down[0].astype(BF16)
    wkv = w_kv.astype(BF16)
    wq = w_q[0].reshape(d, kvh, rep, hd).transpose(0, 2, 1, 3).reshape(d, nq).astype(BF16)
    wo = w_o[0].reshape(kvh, rep, hd, d).transpose(1, 0, 2, 3).reshape(nq, d).astype(BF16)
    bias, bias_s = _bias_tables(rel_bias, window, kvh, rep)
    sinks = attn_sinks[0]
    sink_s = sinks.reshape(kvh, rep, 1)

    tm = TOKEN_TILE
    npr = seq * bsz
    assert npr % tm == 0 and ns <= tm

    x1_p, st_re, st_im = _ssm_prompt(x_prompt, g_mix[0:1], lam_re, lam_im, wb_re, wb_im, wc_re, wc_imn,
                                     d_skip, wglu, lc=32, nsub=2)
    x1_s, hs_re, hs_im = _ssm_sample(x_sample.reshape(ns, d), g_mix[0:1], lam_re, lam_im,
                                     state_ssm_re[0].reshape(ns, gp), state_ssm_im[0].reshape(ns, gp),
                                     wb_re, wb_im, wc_re, wc_imn, d_skip, wglu, pad_rows=tm)

    x2, kv, q = _ffn(x1_p.reshape(npr, d), x1_s, g_ffn[0:1], wgu, wd, g_kv.reshape(1, d), wkv, g_mix[1:2], wq)

    ot = _attn_prompt(q, kv, bias, sinks, bsz=bsz, nblk=seq // window, window=window, kvh=kvh, rep=rep, hd=hd)
    o_s, nk_s, nv_s = _attn_sample(q[npr:npr + ns].reshape(ns, rep, kvw), kv[npr:npr + ns].reshape(ns, 1, 2 * kvw),
                                   cache_k_win.reshape(ns, window, kvw), cache_v_win.reshape(ns, window, kvw),
                                   bias_s, sink_s, nb=16, kvh=kvh, rep=rep, hd=hd)
    ot = lax.dynamic_update_slice(ot, jnp.pad(o_s.reshape(ns, nq).T, ((0, 0), (0, tm - ns))), (0, npr))

    y_p, y_s = _moe(x2, ot, wo, g_ffn[1:2], w_router[0], w_exp_gate_up[0], w_exp_down[0], g_final.reshape(1, d),
                    tm=MOE_TILE, fc=w_exp_down.shape[2] // 2, n_valid=npr + ns, sample_rows=tm)

    y_prompt = y_p.reshape(bsz, seq, d)
    y_sample = y_s[:ns].reshape(ns, 1, d)
    kv_tail = jnp.stack([kv[(b + 1) * seq - window:(b + 1) * seq] for b in range(bsz)])
    kv_tail = kv_tail.reshape(bsz, window, 2, kvh, hd).transpose(2, 0, 1, 3, 4)
    return (y_prompt, y_sample,
            st_re.reshape(1, bsz, g, p), st_im.reshape(1, bsz, g, p), kv_tail[0], kv_tail[1],
            hs_re.reshape(1, ns, g, p), hs_im.reshape(1, ns, g, p),
            nk_s.reshape(ns, window, kvh, hd), nv_s.reshape(ns, window, kvh, hd))
```

```python
import functools
import math

import numpy as np
import jax
import jax.numpy as jnp
from jax import lax
from jax.experimental import pallas as pl
from jax.experimental.pallas import tpu as pltpu

F32 = jnp.float32
BF16 = jnp.bfloat16

EPS = 1e-6
NEG_INF = -1e30
TOP_K = 2
MAX_DISTANCE = 128
MXU_DIM = 256
LANES = 128
VMEM_LIMIT_BYTES = 56 * 1024 * 1024


def _dot(a, b):
    return jnp.dot(a, b, preferred_element_type=F32)


def _rms(x, g):
    return x * lax.rsqrt(jnp.mean(x * x, axis=-1, keepdims=True) + EPS) * g


def _const_spec(shape):
    nd = len(shape)
    return pl.BlockSpec(shape, lambda *_: (0,) * nd)


def _params(*sem, vmem=VMEM_LIMIT_BYTES):
    return pltpu.CompilerParams(dimension_semantics=sem, vmem_limit_bytes=vmem)


def _zoh_kernel(a_re_ref, a_im_ref, log_dt_ref, b_re_ref, b_im_ref, ct_re_ref, ct_im_ref,
                lam_re_ref, lam_im_ref, wb_re_ref, wb_im_ref, wc_re_ref, wc_imn_ref, *, hch, p):
    a_re = a_re_ref[...]
    a_im = a_im_ref[...]
    dt = jnp.exp(log_dt_ref[...])
    mag = jnp.exp(a_re * dt)
    lr = mag * jnp.cos(a_im * dt)
    li = mag * jnp.sin(a_im * dt)
    lam_re_ref[...] = lr
    lam_im_ref[...] = li
    nr = lr - 1.0
    den = a_re * a_re + a_im * a_im
    qr = (nr * a_re + li * a_im) / den
    qi = (li * a_re - nr * a_im) / den
    b_re = b_re_ref[...]
    b_im = b_im_ref[...]
    bb_re = qr * b_re - qi * b_im
    bb_im = qr * b_im + qi * b_re

    nkb, ublk, sblk = wb_re_ref.shape
    gpb = ublk // hch
    sh_h, sh_p = int(math.log2(hch)), int(math.log2(p))
    row = lax.broadcasted_iota(jnp.int32, (ublk, sblk), 0)
    col = lax.broadcasted_iota(jnp.int32, (ublk, sblk), 1)
    diag_in = lax.shift_right_logical(row, sh_h) == lax.shift_right_logical(col, sh_p)
    row = lax.broadcasted_iota(jnp.int32, (sblk, ublk), 0)
    col = lax.broadcasted_iota(jnp.int32, (sblk, ublk), 1)
    diag_out = lax.shift_right_logical(row, sh_p) == lax.shift_right_logical(col, sh_h)
    spread = jnp.where(lax.broadcasted_iota(jnp.int32, (hch, ublk), 0)
                       == jnp.bitwise_and(lax.broadcasted_iota(jnp.int32, (hch, ublk), 1), hch - 1),
                       1.0, 0.0).astype(BF16)
    for kb in range(nkb):
        cs = slice(kb * sblk, (kb + 1) * sblk)
        wb_re_ref[kb] = jnp.where(diag_in, jnp.concatenate([bb_re[:, cs]] * gpb, axis=0), 0.0).astype(BF16)
        wb_im_ref[kb] = jnp.where(diag_in, jnp.concatenate([bb_im[:, cs]] * gpb, axis=0), 0.0).astype(BF16)
        c_re = _dot(ct_re_ref[cs, :].astype(BF16), spread)
        c_im = _dot(ct_im_ref[cs, :].astype(BF16), spread)
        wc_re_ref[kb] = jnp.where(diag_out, c_re, 0.0).astype(BF16)
        wc_imn_ref[kb] = jnp.where(diag_out, -c_im, 0.0).astype(BF16)


def _zoh(a_re, a_im, log_dt, b_re, b_im, c_re, c_im):
    g, p = a_re.shape
    h = b_re.shape[-1]
    gp = g * p
    gpb = MXU_DIM // h
    nkb = g // gpb
    assert h & (h - 1) == 0 and p & (p - 1) == 0 and g % gpb == 0
    row = jax.ShapeDtypeStruct((1, gp), F32)
    wb = jax.ShapeDtypeStruct((nkb, gpb * h, gpb * p), BF16)
    wc = jax.ShapeDtypeStruct((nkb, gpb * p, gpb * h), BF16)
    return pl.pallas_call(
        functools.partial(_zoh_kernel, hch=h, p=p),
        out_shape=(row, row, wb, wb, wc, wc),
        name="s5_zoh",
    )(a_re.reshape(1, gp), a_im.reshape(1, gp), jnp.repeat(log_dt, p).reshape(1, gp),
      b_re.transpose(2, 0, 1).reshape(h, gp), b_im.transpose(2, 0, 1).reshape(h, gp),
      c_re.transpose(0, 2, 1).reshape(gp, h), c_im.transpose(0, 2, 1).reshape(gp, h))


def _ssm_in_proj(ub, wb_re_ref, wb_im_ref, bu_re, bu_im):
    nkb, ublk, sblk = wb_re_ref.shape
    for kb in range(nkb):
        ukb = ub[:, kb * ublk:(kb + 1) * ublk]
        bu_re[:, kb * sblk:(kb + 1) * sblk] = _dot(ukb, wb_re_ref[kb])
        bu_im[:, kb * sblk:(kb + 1) * sblk] = _dot(ukb, wb_im_ref[kb])


def _ssm_out_proj(h_re, h_im, wc_re_ref, wc_imn_ref):
    nkb, sblk, _ = wc_re_ref.shape
    ys = []
    for kb in range(nkb):
        hr = h_re[:, kb * sblk:(kb + 1) * sblk].astype(BF16)
        hi = h_im[:, kb * sblk:(kb + 1) * sblk].astype(BF16)
        ys.append(_dot(hr, wc_re_ref[kb]) + _dot(hi, wc_imn_ref[kb]))
    return jnp.concatenate(ys, axis=1)


def _ssm_glu(x, u, y, d_ref, wglu_ref):
    z = jax.nn.gelu(y + d_ref[...] * u).astype(BF16)
    gl = _dot(z, wglu_ref[...])
    d = x.shape[1]
    return x + gl[:, :d] * jax.nn.sigmoid(gl[:, d:])


def _split3(v):
    hi = v.astype(BF16)
    r1 = v - hi.astype(F32)
    mid = r1.astype(BF16)
    lo = (r1 - mid.astype(F32)).astype(BF16)
    return hi, mid, lo


def _ssm_prompt_kernel(x_ref, g_ref, lam_re_ref, lam_im_ref, wb_re_ref, wb_im_ref,
                       wc_re_ref, wc_imn_ref, d_ref, wglu_ref, perm_ref, permt_ref,
                       out_ref, st_re_ref, st_im_ref, bu_re_all, bu_im_all, *, lc, bsz, lane_chunk):
    @pl.when(pl.program_id(0) == 0)
    def _():
        st_re_ref[...] = jnp.zeros_like(st_re_ref)
        st_im_ref[...] = jnp.zeros_like(st_im_ref)

    for sub in range(bu_re_all.shape[0]):
        ts = slice(sub * lc, (sub + 1) * lc)
        _ssm_prompt_chunk(x_ref[:, ts, :], g_ref, lam_re_ref, lam_im_ref, wb_re_ref, wb_im_ref,
                          wc_re_ref, wc_imn_ref, d_ref, wglu_ref, perm_ref, permt_ref,
                          out_ref.at[:, ts, :], st_re_ref, st_im_ref, bu_re_all.at[sub], bu_im_all.at[sub],
                          lc=lc, bsz=bsz, lane_chunk=lane_chunk)


def _ssm_prompt_chunk(x3, g_ref, lam_re_ref, lam_im_ref, wb_re_ref, wb_im_ref,
                      wc_re_ref, wc_imn_ref, d_ref, wglu_ref, perm_ref, permt_ref,
                      out_ref, st_re_ref, st_im_ref, bu_re, bu_im, *, lc, bsz, lane_chunk):
    d = x3.shape[2]
    x = x3.reshape(bsz * lc, d)
    u = _rms(x, g_ref[...])
    ub_tb = _dot(perm_ref[...], u.astype(BF16)).astype(BF16)
    _ssm_in_proj(ub_tb, wb_re_ref, wb_im_ref, bu_re, bu_im)

    gp = bu_re.shape[1]
    for c0 in range(0, gp, lane_chunk):
        sl = slice(c0, c0 + lane_chunk)
        lre = jnp.broadcast_to(lam_re_ref[:, sl], (bsz, lane_chunk))
        lim = jnp.broadcast_to(lam_im_ref[:, sl], (bsz, lane_chunk))
        hr = st_re_ref[:, sl]
        hi = st_im_ref[:, sl]
        for t in range(lc):
            rows = slice(t * bsz, (t + 1) * bsz)
            hr, hi = (lre * hr - lim * hi + bu_re[rows, sl], lre * hi + lim * hr + bu_im[rows, sl])
            bu_re[rows, sl] = hr
            bu_im[rows, sl] = hi
        st_re_ref[:, sl] = hr
        st_im_ref[:, sl] = hi

    permt = permt_ref[...]
    hi, mid, lo = _split3(_ssm_out_proj(bu_re, bu_im, wc_re_ref, wc_imn_ref))
    y = (_dot(permt, hi) + _dot(permt, mid)) + _dot(permt, lo)
    out_ref[...] = _ssm_glu(x, u, y, d_ref, wglu_ref).reshape(bsz, lc, d)


def _ssm_sample_kernel(x_ref, g_ref, lam_re_ref, lam_im_ref, h0_re_ref, h0_im_ref, wb_re_ref, wb_im_ref,
                       wc_re_ref, wc_imn_ref, d_ref, wglu_ref,
                       out_ref, h_re_ref, h_im_ref):
    x = x_ref[...]
    u = _rms(x, g_ref[...])
    _ssm_in_proj(u.astype(BF16), wb_re_ref, wb_im_ref, h_re_ref, h_im_ref)
    lre = lam_re_ref[...]
    lim = lam_im_ref[...]
    h0r = h0_re_ref[...]
    h0i = h0_im_ref[...]
    h_re_ref[...] = lre * h0r - lim * h0i + h_re_ref[...]
    h_im_ref[...] = lre * h0i + lim * h0r + h_im_ref[...]
    y = _ssm_out_proj(h_re_ref, h_im_ref, wc_re_ref, wc_imn_ref)
    n = x.shape[0]
    out_ref[0:n, :] = _ssm_glu(x, u, y, d_ref, wglu_ref)
    out_ref[n:, :] = jnp.zeros((out_ref.shape[0] - n, out_ref.shape[1]), F32)


def _ssm_prompt(x, g, lam_re, lam_im, wb_re, wb_im, wc_re, wc_imn, d_skip, wglu, *, lc, nsub):
    bsz, seq, d = x.shape
    assert seq % (lc * nsub) == 0
    gp = lam_re.shape[1]
    r = lc * bsz
    perm = np.zeros((r, r), np.float32)
    for b in range(bsz):
        for t in range(lc):
            perm[t * bsz + b, b * lc + t] = 1.0
    permt = jnp.asarray(perm.T, BF16)
    perm = jnp.asarray(perm, BF16)
    kern = functools.partial(_ssm_prompt_kernel, lc=lc, bsz=bsz, lane_chunk=8 * LANES)
    return pl.pallas_call(
        kern,
        grid=(seq // (lc * nsub),),
        in_specs=[pl.BlockSpec((bsz, lc * nsub, d), lambda c: (0, c, 0)),
                  _const_spec(g.shape), _const_spec(lam_re.shape), _const_spec(lam_im.shape),
                  _const_spec(wb_re.shape), _const_spec(wb_im.shape),
                  _const_spec(wc_re.shape), _const_spec(wc_imn.shape),
                  _const_spec(d_skip.shape), _const_spec(wglu.shape),
                  _const_spec(perm.shape), _const_spec(permt.shape)],
        out_specs=[pl.BlockSpec((bsz, lc * nsub, d), lambda c: (0, c, 0)),
                   _const_spec((bsz, gp)), _const_spec((bsz, gp))],
        out_shape=[jax.ShapeDtypeStruct((bsz, seq, d), F32),
                   jax.ShapeDtypeStruct((bsz, gp), F32), jax.ShapeDtypeStruct((bsz, gp), F32)],
        scratch_shapes=[pltpu.VMEM((nsub, r, gp), F32), pltpu.VMEM((nsub, r, gp), F32)],
        compiler_params=_params("arbitrary"),
        name="s5_prompt",
    )(x, g, lam_re, lam_im, wb_re, wb_im, wc_re, wc_imn, d_skip, wglu, perm, permt)


def _ssm_sample(x, g, lam_re, lam_im, h0_re, h0_im, wb_re, wb_im, wc_re, wc_imn, d_skip, wglu, *, pad_rows):
    n, d = x.shape
    gp = lam_re.shape[1]
    return pl.pallas_call(
        _ssm_sample_kernel,
        out_shape=[jax.ShapeDtypeStruct((pad_rows, d), F32),
                   jax.ShapeDtypeStruct((n, gp), F32), jax.ShapeDtypeStruct((n, gp), F32)],
        compiler_params=_params(),
        name="s5_sample",
    )(x, g, lam_re, lam_im, h0_re, h0_im, wb_re, wb_im, wc_re, wc_imn, d_skip, wglu)


def _ffn_chunks(d_ff):
    step = 3 * MXU_DIM
    return [(c, min(c + step, d_ff)) for c in range(0, d_ff, step)]


def _ffn_kernel(xp_ref, xs_ref, gffn_ref, wgu_ref, wd_ref, gkv_ref, wkv_ref, gq_ref, wq_ref,
                x2_ref, kv_ref, q_ref):
    x = jnp.where(pl.program_id(0) < pl.num_programs(0) - 1, xp_ref[...], xs_ref[...])
    hb = _rms(x, gffn_ref[...]).astype(BF16)
    d_ff = wd_ref.shape[0]
    acc = None
    for c0, c1 in _ffn_chunks(d_ff):
        a = _dot(hb, wgu_ref[:, c0:c1])
        b = _dot(hb, wgu_ref[:, d_ff + c0:d_ff + c1])
        part = _dot((jax.nn.silu(a) * b).astype(BF16), wd_ref[c0:c1, :])
        acc = part if acc is None else acc + part
    x2 = x + acc
    x2_ref[...] = x2
    kv_ref[...] = _dot(_rms(x2, gkv_ref[...]).astype(BF16), wkv_ref[...])
    q_ref[...] = _dot(_rms(x2, gq_ref[...]).astype(BF16), wq_ref[...]).astype(BF16)


def _ffn(xp, xs, gffn, wgu, wd, gkv, wkv, gq, wq):
    tm, d = xs.shape
    npt = xp.shape[0] // tm
    rows = xp.shape[0] + tm
    kvw = wkv.shape[1]
    nq = wq.shape[1]
    return pl.pallas_call(
        _ffn_kernel,
        grid=(npt + 1,),
        in_specs=[pl.BlockSpec((tm, d), lambda i: (jnp.minimum(i, npt - 1), 0)),
                  _const_spec(xs.shape),
                  _const_spec(gffn.shape), _const_spec(wgu.shape), _const_spec(wd.shape),
                  _const_spec(gkv.shape), _const_spec(wkv.shape),
                  _const_spec(gq.shape), _const_spec(wq.shape)],
        out_specs=[pl.BlockSpec((tm, d), lambda i: (i, 0)),
                   pl.BlockSpec((tm, kvw), lambda i: (i, 0)),
                   pl.BlockSpec((tm, nq), lambda i: (i, 0))],
        out_shape=[jax.ShapeDtypeStruct((rows, d), F32),
                   jax.ShapeDtypeStruct((rows, kvw), F32),
                   jax.ShapeDtypeStruct((rows, nq), BF16)],
        compiler_params=_params("arbitrary"),
        name="ffn_kv_q",
    )(xp, xs, gffn, wgu, wd, gkv, wkv, gq, wq)


def _t5_bucket(dist, num_buckets):
    max_exact = num_buckets // 2
    d = jnp.maximum(dist, 0)
    large = max_exact + (jnp.log(jnp.maximum(d, 1).astype(F32) / max_exact)
                         / math.log(MAX_DISTANCE / max_exact) * (num_buckets - max_exact)).astype(jnp.int32)
    large = jnp.minimum(large, num_buckets - 1)
    return jnp.where(d < max_exact, d, large)


def _bias_kernel(bm_ref, bs_ref, rb_ref, bias_ref, bias_s_ref, *, kvh, rep):
    nb, nh = rb_ref.shape
    bm = bm_ref[...]
    bs = bs_ref[...]

    def lookup(buckets, h):
        terms = [jnp.where(buckets == k, rb_ref[k, h], 0.0) for k in range(nb)]
        while len(terms) > 1:
            terms = [a + b for a, b in zip(terms[0::2], terms[1::2])] + (terms[-1:] if len(terms) % 2 else [])
        return terms[0]

    w = bm.shape[0]
    for g in range(kvh):
        for r in range(rep):
            bias_ref[g, :, r * w:(r + 1) * w] = lookup(bm, g * rep + r) * math.log2(math.e)
            bias_s_ref[g, r:r + 1, :] = lookup(bs, g * rep + r)


def _bias_tables(rel_bias, window, kvh, rep):
    nb, nh = rel_bias.shape
    key = jnp.arange(window)[:, None]
    qry = jnp.arange(window)[None, :]
    bm = _t5_bucket((qry - key) % window, nb).astype(jnp.int32)
    bs = _t5_bucket(window - 1 - qry, nb).astype(jnp.int32)
    kern = functools.partial(_bias_kernel, kvh=kvh, rep=rep)
    return pl.pallas_call(
        kern,
        in_specs=[pl.BlockSpec(memory_space=pltpu.VMEM), pl.BlockSpec(memory_space=pltpu.VMEM),
                  pl.BlockSpec(memory_space=pltpu.SMEM)],
        out_shape=[jax.ShapeDtypeStruct((kvh, window, rep * window), F32),
                   jax.ShapeDtypeStruct((kvh, rep, window), F32)],
        name="t5_bias",
    )(bm, bs, rel_bias)


def _attn_prompt_kernel(q_ref, kvc_ref, kvp_ref, bias_ref, sink_ref, ot_ref, *, kvh, rep, hd, scale):
    w = kvp_ref.shape[0]
    kv_all = kvc_ref[...]
    for sub in range(q_ref.shape[0] // w):
        rows = slice(sub * w, (sub + 1) * w)
        kvp = kvp_ref[...] if sub == 0 else kv_all[(sub - 1) * w:sub * w]
        first = pl.program_id(1) == 0 if sub == 0 else None
        _attn_prompt_block(q_ref[rows, :], kv_all[rows], kvp, first, bias_ref, sink_ref, ot_ref, sub * w,
                           kvh=kvh, rep=rep, hd=hd, scale=scale)


def _attn_prompt_block(q, kvc, kvp, first, bias_ref, sink_ref, ot_ref, col0, *, kvh, rep, hd, scale):
    w = q.shape[0]
    kvw = kvh * hd
    lane = lax.broadcasted_iota(jnp.int32, (1, LANES), 1)
    key = lax.broadcasted_iota(jnp.int32, (w, rep * w), 0)
    qry = lax.broadcasted_iota(jnp.int32, (w, rep * w), 1) % w
    upper = key > qry
    mask_add = None if first is None else jnp.where(jnp.logical_and(upper, first), NEG_INF, 0.0)
    log2e = math.log2(math.e)
    heads_per_blk = LANES // hd
    for p in range(kvw // LANES):
        cs = slice(p * LANES, (p + 1) * LANES)
        kc, kp = kvc[:, cs], kvp[:, cs]
        vs = slice(kvw + p * LANES, kvw + (p + 1) * LANES)
        vb = jnp.concatenate([kvp[:, vs], kvc[:, vs]], axis=0).astype(BF16)
        qg = jnp.concatenate([q[:, r * kvw + p * LANES:r * kvw + (p + 1) * LANES] for r in range(rep)], axis=0)
        halves = []
        for half in range(heads_per_blk):
            g = p * heads_per_blk + half
            lmask = jnp.logical_and(lane >= half * hd, lane < (half + 1) * hd)
            kb = jnp.concatenate([jnp.where(lmask, kp, 0.0), jnp.where(lmask, kc, 0.0)], axis=0).astype(BF16)
            s = lax.dot_general(kb, qg, (((1,), (1,)), ((), ())), preferred_element_type=F32)
            bias = bias_ref[g] if first is None else bias_ref[g] + mask_add
            sc = jnp.where(upper, s[:w], s[w:]) * (scale * log2e) + bias
            sink = jnp.concatenate([jnp.full((1, w), sink_ref[g * rep + r] * log2e, F32) for r in range(rep)],
                                   axis=1)
            m = jnp.maximum(jnp.max(sc, axis=0, keepdims=True), sink)
            pe = jnp.exp2(sc - m)
            denom = jnp.sum(pe, axis=0, keepdims=True) + jnp.exp2(sink - m)
            pcat = jnp.concatenate([jnp.where(upper, pe, 0.0), jnp.where(upper, 0.0, pe)], axis=0).astype(BF16)
            og = lax.dot_general(vb, pcat, (((0,), (0,)), ((), ())), preferred_element_type=F32)
            halves.append((og / denom)[half * hd:(half + 1) * hd])
        o_blk = jnp.concatenate(halves, axis=0).astype(BF16)
        for r in range(rep):
            ot_ref[r * kvw + p * LANES:r * kvw + (p + 1) * LANES, col0:col0 + w] = o_blk[:, r * w:(r + 1) * w]


def _attn_prompt(q, kv, bias, sinks, *, bsz, nblk, window, kvh, rep, hd):
    rows, nq = q.shape
    kvw2 = kv.shape[1]
    nsub = next(n for n in (4, 2, 1) if nblk % n == 0 and rows % (n * window) == 0)
    nstep = nblk // nsub
    kern = functools.partial(_attn_prompt_kernel, kvh=kvh, rep=rep, hd=hd, scale=1.0 / math.sqrt(hd))
    return pl.pallas_call(
        kern,
        grid=(bsz, nstep),
        in_specs=[pl.BlockSpec((nsub * window, nq), lambda b, i: (b * nstep + i, 0)),
                  pl.BlockSpec((nsub * window, kvw2), lambda b, i: (b * nstep + i, 0)),
                  pl.BlockSpec((window, kvw2), lambda b, i: (b * nblk + jnp.maximum(nsub * i - 1, 0), 0)),
                  _const_spec(bias.shape),
                  pl.BlockSpec(memory_space=pltpu.SMEM)],
        out_specs=pl.BlockSpec((nq, nsub * window), lambda b, i: (0, b * nstep + i)),
        out_shape=jax.ShapeDtypeStruct((nq, rows), BF16),
        compiler_params=_params("arbitrary", "arbitrary"),
        name="swa_prompt",
    )(q, kv, kv, bias, sinks)


def _attn_sample_kernel(q_ref, kv_ref, ck_ref, cv_ref, bias_ref, sink_ref, o_ref, nk_ref, nv_ref,
                        *, kvh, rep, hd, scale):
    nb, w, kvw = ck_ref.shape
    kv = kv_ref[...]
    lane = lax.broadcasted_iota(jnp.int32, (nb, rep, kvw), 2)

    def shifted(c_ref, n_ref, new):
        flat = c_ref[...].reshape(nb * w, kvw)
        n_ref[...] = pltpu.roll(flat, nb * w - 1, axis=0).reshape(nb, w, kvw)
        n_ref[:, w - 1:w, :] = new
        return n_ref[...].astype(BF16)

    nkb = shifted(ck_ref, nk_ref, kv[:, :, :kvw])
    nvb = shifted(cv_ref, nv_ref, kv[:, :, kvw:])
    q = q_ref[...].astype(F32)
    o = jnp.zeros((nb, rep, kvw), F32)
    for g in range(kvh):
        lmask = jnp.logical_and(lane >= g * hd, lane < (g + 1) * hd)
        qg = jnp.where(lmask, q, 0.0).astype(BF16)
        s = jnp.einsum("nrc,njc->nrj", qg, nkb, preferred_element_type=F32)
        sc = s * scale + bias_ref[g][None]
        sink = sink_ref[g][None]
        m = jnp.maximum(jnp.max(sc, axis=-1, keepdims=True), sink)
        pe = jnp.exp(sc - m)
        probs = pe / (jnp.sum(pe, axis=-1, keepdims=True) + jnp.exp(sink - m))
        og = jnp.einsum("nrj,njc->nrc", probs.astype(BF16), nvb, preferred_element_type=F32)
        o = jnp.where(lmask, og, o)
    o_ref[...] = o.astype(BF16)


def _attn_sample(q3, kv3, ck, cv, bias_s, sink_s, *, nb, kvh, rep, hd):
    n, w, kvw = ck.shape
    kern = functools.partial(_attn_sample_kernel, kvh=kvh, rep=rep, hd=hd, scale=1.0 / math.sqrt(hd))
    cache_spec = pl.BlockSpec((nb, w, kvw), lambda i: (i, 0, 0))
    return pl.pallas_call(
        kern,
        grid=(n // nb,),
        in_specs=[pl.BlockSpec((nb, rep, kvw), lambda i: (i, 0, 0)),
                  pl.BlockSpec((nb, 1, 2 * kvw), lambda i: (i, 0, 0)),
                  cache_spec, cache_spec,
                  _const_spec(bias_s.shape), _const_spec(sink_s.shape)],
        out_specs=[pl.BlockSpec((nb, rep, kvw), lambda i: (i, 0, 0)), cache_spec, cache_spec],
        out_shape=[jax.ShapeDtypeStruct((n, rep, kvw), BF16),
                   jax.ShapeDtypeStruct((n, w, kvw), F32), jax.ShapeDtypeStruct((n, w, kvw), F32)],
        compiler_params=_params("arbitrary"),
        name="swa_sample",
    )(q3, kv3, ck, cv, bias_s, sink_s)


CHUNK = 16
BIG_COPY = 128
SUB = 256
UP_ROW_BLOCK = 1024
DOWN_ROW_BLOCK = 1024
TOKEN_TILE = 512
MOE_TILE = 512


def _chunk_copy(src, src_row, dst, dst_row, sem, rows=CHUNK):
    return pltpu.make_async_copy(src.at[pl.ds(pl.multiple_of(src_row, CHUNK), rows), :],
                                 dst.at[pl.ds(pl.multiple_of(dst_row, CHUNK), rows), :], sem)


def _start_copies(src, src_row, dst, dst_row, sem, nrows):
    nbig = lax.shift_right_logical(nrows, int(math.log2(BIG_COPY)))

    def big(c, carry):
        _chunk_copy(src, src_row + c * BIG_COPY, dst, dst_row + c * BIG_COPY, sem, BIG_COPY).start()
        return carry

    lax.fori_loop(0, nbig, big, 0)
    done = nbig * BIG_COPY

    def small(c, carry):
        _chunk_copy(src, src_row + done + c * CHUNK, dst, dst_row + done + c * CHUNK, sem).start()
        return carry

    lax.fori_loop(0, lax.shift_right_logical(nrows - done, int(math.log2(CHUNK))), small, 0)


def _moe_route_kernel(x2_ref, ot_ref, wo_ref, gffn_ref, wr_ref, lst_ref, ust_ref,
                      x3_ref, cm_ref, tmeta_ref, xs_hbm,
                      comp_s, carry_s, zero_s, base_v, pend_sm, sem, *, ne, cap, n_valid):
    i = pl.program_id(0)
    nt = pl.num_programs(0)
    tm = x2_ref.shape[0]
    w = comp_s.shape[1]
    slot = lax.rem(i, 2)

    @pl.when(i == 0)
    def _():
        base_v[...] = jnp.zeros_like(base_v)
        zero_s[...] = jnp.zeros_like(zero_s)
        carry_s[...] = jnp.zeros_like(carry_s)

    def drain(n):
        def body(c, carry):
            _chunk_copy(zero_s, 0, xs_hbm, 0, sem).wait()
            return carry
        lax.fori_loop(0, n, body, 0)

    @pl.when(i > 0)
    def _():
        drain(pend_sm[0])

    x3 = x2_ref[...] + lax.dot_general(ot_ref[...], wo_ref[...], (((0,), (0,)), ((), ())),
                                       preferred_element_type=F32)
    x3_ref[...] = x3
    hb = _rms(x3, gffn_ref[...]).astype(BF16)

    lane = lax.broadcasted_iota(jnp.int32, (tm, LANES), 1)
    logits = jnp.where(lane < ne, _dot(hb, wr_ref[...]), -jnp.inf)
    m1 = jnp.max(logits, axis=-1, keepdims=True)
    i1 = jnp.min(jnp.where(logits == m1, lane, LANES), axis=-1, keepdims=True)
    rest = jnp.where(lane == i1, -jnp.inf, logits)
    m2 = jnp.max(rest, axis=-1, keepdims=True)
    i2 = jnp.min(jnp.where(rest == m2, lane, LANES), axis=-1, keepdims=True)
    e2 = jnp.exp(m2 - m1)
    g1 = 1.0 / (1.0 + e2)
    g2 = e2 / (1.0 + e2)

    live = i * tm + lax.broadcasted_iota(jnp.int32, (tm, 1), 0) < n_valid
    sel = jnp.where(jnp.logical_and(live, jnp.logical_or(lane == i1, lane == i2)), 1.0, 0.0)
    rank = _dot(lst_ref[...], sel.astype(BF16))
    cnt = jnp.sum(sel, axis=0, keepdims=True)
    fill = base_v[...]
    rem = fill - jnp.floor(fill / CHUNK) * CHUNK
    cpad = jnp.floor((rem + cnt + (CHUNK - 1)) / CHUNK) * CHUNK
    loff = _dot(jnp.broadcast_to(cpad, (8, LANES)).astype(BF16), ust_ref[...])[0:1]
    dest = loff + rem + rank
    ld1 = jnp.where(live, jnp.sum(jnp.where(lane == i1, dest, 0.0), axis=-1, keepdims=True), -1.0)
    ld2 = jnp.where(live, jnp.sum(jnp.where(lane == i2, dest, 0.0), axis=-1, keepdims=True), -1.0)
    cm = jnp.where(lane == 0, ld1, jnp.where(lane == 1, ld2, jnp.where(lane == 2, g1, jnp.where(lane == 3, g2, 0.0))))
    cm_ref[...] = cm

    rm = cm.T
    rowi = lax.broadcasted_iota(jnp.int32, (w, tm), 0).astype(F32)
    place = (jnp.where(rowi == rm[0:1], 1.0, 0.0) + jnp.where(rowi == rm[1:2], 1.0, 0.0)).astype(BF16)
    comp_s[slot] = _dot(place, hb).astype(BF16)

    base = fill - rem
    srow = lax.broadcasted_iota(jnp.int32, (8, LANES), 0)
    tmeta_ref[...] = jnp.where(srow == 0, cpad, jnp.where(srow == 1, base, jnp.where(srow == 2, fill + cnt, 0.0))
                               ).astype(jnp.int32)
    base_v[...] = fill + cnt

    cpad_i = cpad.astype(jnp.int32)
    loff_i = loff.astype(jnp.int32)
    base_i = base.astype(jnp.int32)
    tail_i = (rem + cnt - jnp.floor((rem + cnt) / CHUNK) * CHUNK).astype(jnp.int32)
    total = 0
    tails = []
    for e in range(ne):
        n_e = cpad_i[0, e]
        src0 = loff_i[0, e]
        dst0 = base_i[0, e] + e * cap

        @pl.when(n_e > 0)
        def _(e=e, n_e=n_e, src0=src0, partial=tail_i[0, e] > 0):
            head = pl.ds(pl.multiple_of(src0, CHUNK), CHUNK)
            comp_s[slot, head, :] = comp_s[slot, head, :] + carry_s[e]
            last = comp_s[slot, pl.ds(pl.multiple_of(src0 + n_e - CHUNK, CHUNK), CHUNK), :]
            carry_s[e] = jnp.where(partial, last, jnp.zeros_like(last))

        _start_copies(comp_s.at[slot], src0, xs_hbm, dst0, sem, n_e)
        total = total + lax.shift_right_logical(n_e, int(math.log2(CHUNK)))
        tails.append(dst0 + n_e)
    pend_sm[0] = total

    @pl.when(i == nt - 1)
    def _():
        nz_total = 0
        for e in range(ne):
            end = tails[e]
            nz = lax.shift_right_logical(lax.rem(SUB - lax.rem(end, SUB), SUB), int(math.log2(CHUNK)))

            def zstart(c, carry, end=end):
                _chunk_copy(zero_s, 0, xs_hbm, end + c * CHUNK, sem).start()
                return carry

            lax.fori_loop(0, nz, zstart, 0)
            nz_total = nz_total + nz
        drain(total + nz_total)


def _moe_route(x2, ot, wo, gffn, wr_pad, *, tm, ne, cap, n_valid):
    rows, d = x2.shape
    nt = rows // tm
    w = _staging_rows(tm, ne)
    lst = jnp.asarray(np.tril(np.ones((tm, tm), np.float32), -1), BF16)
    ust = jnp.asarray(np.triu(np.ones((LANES, LANES), np.float32), 1), BF16)
    kern = functools.partial(_moe_route_kernel, ne=ne, cap=cap, n_valid=n_valid)
    return pl.pallas_call(
        kern,
        grid=(nt,),
        in_specs=[pl.BlockSpec((tm, d), lambda i: (i, 0)),
                  pl.BlockSpec((ot.shape[0], tm), lambda i: (0, i)),
                  _const_spec(wo.shape), _const_spec(gffn.shape), _const_spec(wr_pad.shape),
                  _const_spec(lst.shape), _const_spec(ust.shape)],
        out_specs=[pl.BlockSpec((tm, d), lambda i: (i, 0)),
                   pl.BlockSpec((tm, LANES), lambda i: (i, 0)),
                   pl.BlockSpec((8, LANES), lambda i: (i, 0)),
                   pl.BlockSpec(memory_space=pl.ANY)],
        out_shape=[jax.ShapeDtypeStruct((rows, d), F32),
                   jax.ShapeDtypeStruct((rows, LANES), F32),
                   jax.ShapeDtypeStruct((nt * 8, LANES), jnp.int32),
                   jax.ShapeDtypeStruct((ne * cap, d), BF16)],
        scratch_shapes=[pltpu.VMEM((2, w, d), BF16), pltpu.VMEM((ne, CHUNK, d), BF16), pltpu.VMEM((CHUNK, d), BF16),
                        pltpu.VMEM((1, LANES), F32), pltpu.SMEM((1,), jnp.int32),
                        pltpu.SemaphoreType.DMA(())],
        compiler_params=_params("arbitrary"),
        name="moe_route",
    )(x2, ot, wo, gffn, wr_pad, lst, ust)


def _staging_rows(tm, ne):
    return -(-(TOP_K * tm + 2 * ne * (CHUNK - 1)) // LANES) * LANES


def _max_blocks(rows, ne, row_block):
    return TOP_K * rows // row_block + ne


def _expert_steps(nrows, nf, max_blocks, row_block):
    ne = nrows.shape[0]
    nblk = (nrows + row_block - 1) // row_block
    cum = jnp.cumsum(nblk) * nf
    total = cum[-1]
    s = jnp.minimum(jnp.arange(max_blocks * nf, dtype=jnp.int32), total - 1)
    e = jnp.minimum(jnp.sum((s[:, None] >= cum[None, :]).astype(jnp.int32), axis=1), ne - 1)
    nb_e = nblk[e]
    within = s - (cum[e] - nb_e * nf)
    f = within // nb_e
    pos = within - f * nb_e
    r = (pos + nb_e - 1) % nb_e
    nsub = jnp.clip((nrows[e] - r * row_block + SUB - 1) // SUB, 0, row_block // SUB)
    first = (pos == 0).astype(jnp.int32)
    return (e, f.astype(jnp.int32), r.astype(jnp.int32), first, nsub.astype(jnp.int32),
            jnp.reshape(total, (1,)).astype(jnp.int32))


def _moe_up_kernel(e_ref, f_ref, r_ref, first_ref, nsub_ref, n_ref, xs_ref, wg_ref, wu_ref, act_ref, wgb_s, wub_s):
    s = pl.program_id(0)

    @pl.when(s < n_ref[0])
    def _():
        @pl.when(first_ref[s] == 1)
        def _():
            wgb_s[...] = wg_ref[...].astype(BF16)
            wub_s[...] = wu_ref[...].astype(BF16)

        def gate_up(row0, nrows):
            rows = pl.ds(pl.multiple_of(row0, SUB), nrows)
            xsb = xs_ref[rows, :]
            a = _dot(xsb, wgb_s[...])
            b = _dot(xsb, wub_s[...])
            act_ref[rows, :] = (jax.nn.silu(a) * b).astype(BF16)

        nsub = nsub_ref[s]
        npair = lax.shift_right_logical(nsub, 1)

        def body(j, carry):
            gate_up(j * (2 * SUB), 2 * SUB)
            return carry

        lax.fori_loop(0, npair, body, 0)

        @pl.when(nsub > 2 * npair)
        def _():
            gate_up(npair * (2 * SUB), SUB)


def _moe_up(steps, xs, wgu, *, fc, cap, row_block):
    ne, d, dff2 = wgu.shape
    dff = dff2 // 2
    nf = dff // fc
    cb = cap // row_block
    grid_spec = pltpu.PrefetchScalarGridSpec(
        num_scalar_prefetch=6,
        grid=(steps[0].shape[0],),
        in_specs=[pl.BlockSpec((row_block, d), lambda s, e, f, r, *_: (e[s] * cb + r[s], 0)),
                  pl.BlockSpec((None, d, fc), lambda s, e, f, r, *_: (e[s], 0, f[s])),
                  pl.BlockSpec((None, d, fc), lambda s, e, f, r, *_: (e[s], 0, nf + f[s]))],
        out_specs=pl.BlockSpec((row_block, fc), lambda s, e, f, r, *_: (e[s] * cb + r[s], f[s])),
        scratch_shapes=[pltpu.VMEM((d, fc), BF16), pltpu.VMEM((d, fc), BF16)])
    return pl.pallas_call(
        _moe_up_kernel,
        grid_spec=grid_spec,
        out_shape=jax.ShapeDtypeStruct((ne * cap, dff), BF16),
        compiler_params=_params("arbitrary"),
        name="moe_up",
    )(*steps, xs, wgu, wgu)


def _moe_down_kernel(e_ref, f_ref, r_ref, first_ref, nsub_ref, n_ref, act_ref, wd_ref, ys_ref, wdb_s):
    s = pl.program_id(0)

    @pl.when(s < n_ref[0])
    def _():
        @pl.when(first_ref[s] == 1)
        def _():
            wdb_s[...] = wd_ref[...].astype(BF16)

        def project(row0, nrows):
            rows = pl.ds(pl.multiple_of(row0, SUB), nrows)
            ys_ref[rows, :] = _dot(act_ref[rows, :], wdb_s[...]).astype(BF16)

        nsub = nsub_ref[s]
        npair = lax.shift_right_logical(nsub, 1)

        def body(j, carry):
            project(j * (2 * SUB), 2 * SUB)
            return carry

        lax.fori_loop(0, npair, body, 0)

        @pl.when(nsub > 2 * npair)
        def _():
            project(npair * (2 * SUB), SUB)


def _moe_down(steps, act, wd, *, cap, row_block):
    ne, dff, d = wd.shape
    cb = cap // row_block
    grid_spec = pltpu.PrefetchScalarGridSpec(
        num_scalar_prefetch=6,
        grid=(steps[0].shape[0],),
        in_specs=[pl.BlockSpec((row_block, dff), lambda s, e, f, r, *_: (e[s] * cb + r[s], 0)),
                  pl.BlockSpec((None, dff, d), lambda s, e, f, r, *_: (e[s], 0, 0))],
        out_specs=pl.BlockSpec((row_block, d), lambda s, e, f, r, *_: (e[s] * cb + r[s], 0)),
        scratch_shapes=[pltpu.VMEM((dff, d), BF16)])
    return pl.pallas_call(
        _moe_down_kernel,
        grid_spec=grid_spec,
        out_shape=jax.ShapeDtypeStruct((ne * cap, d), BF16),
        compiler_params=_params("arbitrary", vmem=2 * (dff * d * 4 + row_block * (dff + d) * 2) + dff * d * 2
                                + 4 * SUB * d * 4),
        name="moe_down",
    )(*steps, act, wd)


def _moe_combine_kernel(cpad_ref, seg_ref, x3_ref, cm_ref, gfin_ref, ys_hbm, yp_ref, ys_ref, yloc_s, sem,
                        *, ne, npt):
    i = pl.program_id(0)
    nt = pl.num_programs(0)
    tm = x3_ref.shape[0]
    w = yloc_s.shape[1]
    slot = lax.rem(i, 2)
    shift = int(math.log2(CHUNK))

    def issue(t, sl):
        off = 0
        for e in range(ne):
            n_e = cpad_ref[t * ne + e]
            _start_copies(ys_hbm, seg_ref[t * ne + e], yloc_s.at[sl], off, sem.at[sl], n_e)
            off = off + n_e

    @pl.when(i == 0)
    def _():
        yloc_s[...] = jnp.zeros_like(yloc_s)
        issue(0, 0)

    @pl.when(i + 1 < nt)
    def _():
        issue(i + 1, 1 - slot)

    total = 0
    for e in range(ne):
        total = total + lax.shift_right_logical(cpad_ref[i * ne + e], shift)

    def wait(c, carry):
        _chunk_copy(ys_hbm, 0, yloc_s.at[slot], 0, sem.at[slot]).wait()
        return carry

    lax.fori_loop(0, total, wait, 0)

    cm = cm_ref[...]
    col = lax.broadcasted_iota(jnp.int32, (tm, w), 1).astype(F32)
    yl = yloc_s[slot]
    y1 = _dot(jnp.where(col == cm[:, 0:1], 1.0, 0.0).astype(BF16), yl)
    y2 = _dot(jnp.where(col == cm[:, 1:2], 1.0, 0.0).astype(BF16), yl)
    y = _rms(x3_ref[...] + cm[:, 2:3] * y1 + cm[:, 3:4] * y2, gfin_ref[...])

    @pl.when(i < npt)
    def _():
        yp_ref[...] = y

    @pl.when(i >= npt)
    def _():
        ys_ref[...] = y


def _moe_combine(cpad, seg, x3, cm, gfin, ys, *, tm, ne, sample_rows):
    rows, d = x3.shape
    nt = rows // tm
    npt = (rows - sample_rows) // tm
    w = _staging_rows(tm, ne)
    kern = functools.partial(_moe_combine_kernel, ne=ne, npt=npt)
    grid_spec = pltpu.PrefetchScalarGridSpec(
        num_scalar_prefetch=2,
        grid=(nt,),
        in_specs=[pl.BlockSpec((tm, d), lambda i, *_: (i, 0)),
                  pl.BlockSpec((tm, LANES), lambda i, *_: (i, 0)),
                  pl.BlockSpec(gfin.shape, lambda i, *_: (0, 0)),
                  pl.BlockSpec(memory_space=pl.ANY)],
        out_specs=[pl.BlockSpec((tm, d), lambda i, *_: (jnp.minimum(i, npt - 1), 0)),
                   pl.BlockSpec((tm, d), lambda i, *_: (jnp.maximum(i - npt, 0), 0))],
        scratch_shapes=[pltpu.VMEM((2, w, d), BF16), pltpu.SemaphoreType.DMA((2,))])
    return pl.pallas_call(
        kern,
        grid_spec=grid_spec,
        out_shape=[jax.ShapeDtypeStruct((npt * tm, d), F32), jax.ShapeDtypeStruct((sample_rows, d), F32)],
        compiler_params=_params("arbitrary"),
        name="moe_combine",
    )(cpad, seg, x3, cm, gfin, ys)


def _moe(x2, ot, wo, gffn, wr, wgu, wd, gfin, *, tm, fc, n_valid, sample_rows):
    rows, d = x2.shape
    ne = wr.shape[1]
    nt = rows // tm
    nf = wgu.shape[2] // 2 // fc
    cap = -(-(rows + SUB) // UP_ROW_BLOCK) * UP_ROW_BLOCK
    wr_pad = jnp.pad(wr, ((0, 0), (0, LANES - ne))).astype(BF16)
    x3, cm, tmeta, xs = _moe_route(x2, ot, wo, gffn, wr_pad, tm=tm, ne=ne, cap=cap, n_valid=n_valid)
    tmeta = tmeta.reshape(nt, 8, LANES)[:, :, :ne]
    cpad = tmeta[:, 0, :].reshape(nt * ne)
    seg = (tmeta[:, 1, :] + jnp.arange(ne, dtype=jnp.int32)[None, :] * cap).reshape(nt * ne)
    nrows = tmeta[nt - 1, 2, :]
    up_steps = _expert_steps(nrows, nf, _max_blocks(rows, ne, UP_ROW_BLOCK), UP_ROW_BLOCK)
    act = _moe_up(up_steps, xs, wgu, fc=fc, cap=cap, row_block=UP_ROW_BLOCK)
    down_steps = _expert_steps(nrows, 1, _max_blocks(rows, ne, DOWN_ROW_BLOCK), DOWN_ROW_BLOCK)
    ys = _moe_down(down_steps, act, wd, cap=cap, row_block=DOWN_ROW_BLOCK)
    return _moe_combine(cpad, seg, x3, cm, gfin, ys, tm=tm, ne=ne, sample_rows=sample_rows)


def kernel(x_prompt, x_sample, state_ssm_re, state_ssm_im, cache_k_win, cache_v_win, g_mix, g_ffn, g_kv, g_final, ssm_a_re, ssm_a_im, ssm_log_dt, ssm_b_re, ssm_b_im, ssm_c_re, ssm_c_im, ssm_d, w_glu, w_kv, w_q, w_o, attn_sinks, rel_bias, w_ffn_gate_up, w_ffn_down, w_router, w_exp_gate_up, w_exp_down):
    bsz, seq, d = x_prompt.shape
    ns, dec_seq, _ = x_sample.shape
    assert dec_seq == 1 and g_mix.shape[0] == 2 and ssm_a_re.shape[0] == 1 and w_q.shape[0] == 1
    _, g, p = ssm_a_re.shape
    gp = g * p
    window, kvh, hd = cache_k_win.shape[1:]
    kvw = kvh * hd
    nh = attn_sinks.shape[1]
    rep = nh // kvh
    nq = nh * hd
    assert bsz == 8 and ns % bsz == 0 and seq % window == 0 and LANES % hd == 0

    lam_re, lam_im, wb_re, wb_im, wc_re, wc_imn = _zoh(ssm_a_re[0], ssm_a_im[0], ssm_log_dt[0],
                                                        ssm_b_re[0], ssm_b_im[0], ssm_c_re[0], ssm_c_im[0])
    wglu = w_glu[0].astype(BF16)
    d_skip = ssm_d[0].reshape(1, d)
    wgu = w_ffn_gate_up[0].astype(BF16)
    wd = w_ffn_down[0].astype(BF16)
    wkv = w_kv.astype(BF16)
    wq = w_q[0].reshape(d, kvh, rep, hd).transpose(0, 2, 1, 3).reshape(d, nq).astype(BF16)
    wo = w_o[0].reshape(kvh, rep, hd, d).transpose(1, 0, 2, 3).reshape(nq, d).astype(BF16)
    bias, bias_s = _bias_tables(rel_bias, window, kvh, rep)
    sinks = attn_sinks[0]
    sink_s = sinks.reshape(kvh, rep, 1)

    tm = TOKEN_TILE
    npr = seq * bsz
    assert npr % tm == 0 and ns <= tm

    x1_p, st_re, st_im = _ssm_prompt(x_prompt, g_mix[0:1], lam_re, lam_im, wb_re, wb_im, wc_re, wc_imn,
                                     d_skip, wglu, lc=32, nsub=2)
    x1_s, hs_re, hs_im = _ssm_sample(x_sample.reshape(ns, d), g_mix[0:1], lam_re, lam_im,
                                     state_ssm_re[0].reshape(ns, gp), state_ssm_im[0].reshape(ns, gp),
                                     wb_re, wb_im, wc_re, wc_imn, d_skip, wglu, pad_rows=tm)

    x2, kv, q = _ffn(x1_p.reshape(npr, d), x1_s, g_ffn[0:1], wgu, wd, g_kv.reshape(1, d), wkv, g_mix[1:2], wq)

    ot = _attn_prompt(q, kv, bias, sinks, bsz=bsz, nblk=seq // window, window=window, kvh=kvh, rep=rep, hd=hd)
    o_s, nk_s, nv_s = _attn_sample(q[npr:npr + ns].reshape(ns, rep, kvw), kv[npr:npr + ns].reshape(ns, 1, 2 * kvw),
                                   cache_k_win.reshape(ns, window, kvw), cache_v_win.reshape(ns, window, kvw),
                                   bias_s, sink_s, nb=32, kvh=kvh, rep=rep, hd=hd)
    ot = lax.dynamic_update_slice(ot, jnp.pad(o_s.reshape(ns, nq).T, ((0, 0), (0, tm - ns))), (0, npr))

    y_p, y_s = _moe(x2, ot, wo, g_ffn[1:2], w_router[0], w_exp_gate_up[0], w_exp_down[0], g_final.reshape(1, d),
                    tm=MOE_TILE, fc=w_exp_down.shape[2] // 2, n_valid=npr + ns, sample_rows=tm)

    y_prompt = y_p.reshape(bsz, seq, d)
    y_sample = y_s[:ns].reshape(ns, 1, d)
    kv_tail = jnp.stack([kv[(b + 1) * seq - window:(b + 1) * seq] for b in range(bsz)])
    kv_tail = kv_tail.reshape(bsz, window, 2, kvh, hd).transpose(2, 0, 1, 3, 4)
    return (y_prompt, y_sample,
            st_re.reshape(1, bsz, g, p), st_im.reshape(1, bsz, g, p), kv_tail[0], kv_tail[1],
            hs_re.reshape(1, ns, g, p), hs_im.reshape(1, ns, g, p),
            nk_s.reshape(ns, window, kvh, hd), nv_s.reshape(ns, window, kvh, hd))
```

```python
import functools
import math

import numpy as np
import jax
import jax.numpy as jnp
from jax import lax
from jax.experimental import pallas as pl
from jax.experimental.pallas import tpu as pltpu

F32 = jnp.float32
BF16 = jnp.bfloat16

EPS = 1e-6
NEG_INF = -1e30
TOP_K = 2
MAX_DISTANCE = 128
MXU_DIM = 256
LANES = 128
VMEM_LIMIT_BYTES = 56 * 1024 * 1024


def _dot(a, b):
    return jnp.dot(a, b, preferred_element_type=F32)


def _rms(x, g):
    return x * lax.rsqrt(jnp.mean(x * x, axis=-1, keepdims=True) + EPS) * g


def _const_spec(shape):
    nd = len(shape)
    return pl.BlockSpec(shape, lambda *_: (0,) * nd)


def _params(*sem, vmem=VMEM_LIMIT_BYTES):
    return pltpu.CompilerParams(dimension_semantics=sem, vmem_limit_bytes=vmem)


def _zoh_kernel(a_re_ref, a_im_ref, log_dt_ref, b_re_ref, b_im_ref, ct_re_ref, ct_im_ref,
                lam_re_ref, lam_im_ref, wb_re_ref, wb_im_ref, wc_re_ref, wc_imn_ref, *, hch, p):
    a_re = a_re_ref[...]
    a_im = a_im_ref[...]
    dt = jnp.exp(log_dt_ref[...])
    mag = jnp.exp(a_re * dt)
    lr = mag * jnp.cos(a_im * dt)
    li = mag * jnp.sin(a_im * dt)
    lam_re_ref[...] = lr
    lam_im_ref[...] = li
    nr = lr - 1.0
    den = a_re * a_re + a_im * a_im
    qr = (nr * a_re + li * a_im) / den
    qi = (li * a_re - nr * a_im) / den
    b_re = b_re_ref[...]
    b_im = b_im_ref[...]
    bb_re = qr * b_re - qi * b_im
    bb_im = qr * b_im + qi * b_re

    nkb, ublk, sblk = wb_re_ref.shape
    gpb = ublk // hch
    sh_h, sh_p = int(math.log2(hch)), int(math.log2(p))
    row = lax.broadcasted_iota(jnp.int32, (ublk, sblk), 0)
    col = lax.broadcasted_iota(jnp.int32, (ublk, sblk), 1)
    diag_in = lax.shift_right_logical(row, sh_h) == lax.shift_right_logical(col, sh_p)
    row = lax.broadcasted_iota(jnp.int32, (sblk, ublk), 0)
    col = lax.broadcasted_iota(jnp.int32, (sblk, ublk), 1)
    diag_out = lax.shift_right_logical(row, sh_p) == lax.shift_right_logical(col, sh_h)
    spread = jnp.where(lax.broadcasted_iota(jnp.int32, (hch, ublk), 0)
                       == jnp.bitwise_and(lax.broadcasted_iota(jnp.int32, (hch, ublk), 1), hch - 1),
                       1.0, 0.0).astype(BF16)
    for kb in range(nkb):
        cs = slice(kb * sblk, (kb + 1) * sblk)
        wb_re_ref[kb] = jnp.where(diag_in, jnp.concatenate([bb_re[:, cs]] * gpb, axis=0), 0.0).astype(BF16)
        wb_im_ref[kb] = jnp.where(diag_in, jnp.concatenate([bb_im[:, cs]] * gpb, axis=0), 0.0).astype(BF16)
        c_re = _dot(ct_re_ref[cs, :].astype(BF16), spread)
        c_im = _dot(ct_im_ref[cs, :].astype(BF16), spread)
        wc_re_ref[kb] = jnp.where(diag_out, c_re, 0.0).astype(BF16)
        wc_imn_ref[kb] = jnp.where(diag_out, -c_im, 0.0).astype(BF16)


def _zoh(a_re, a_im, log_dt, b_re, b_im, c_re, c_im):
    g, p = a_re.shape
    h = b_re.shape[-1]
    gp = g * p
    gpb = MXU_DIM // h
    nkb = g // gpb
    assert h & (h - 1) == 0 and p & (p - 1) == 0 and g % gpb == 0
    row = jax.ShapeDtypeStruct((1, gp), F32)
    wb = jax.ShapeDtypeStruct((nkb, gpb * h, gpb * p), BF16)
    wc = jax.ShapeDtypeStruct((nkb, gpb * p, gpb * h), BF16)
    return pl.pallas_call(
        functools.partial(_zoh_kernel, hch=h, p=p),
        out_shape=(row, row, wb, wb, wc, wc),
        name="s5_zoh",
    )(a_re.reshape(1, gp), a_im.reshape(1, gp), jnp.repeat(log_dt, p).reshape(1, gp),
      b_re.transpose(2, 0, 1).reshape(h, gp), b_im.transpose(2, 0, 1).reshape(h, gp),
      c_re.transpose(0, 2, 1).reshape(gp, h), c_im.transpose(0, 2, 1).reshape(gp, h))


def _ssm_in_proj(ub, wb_re_ref, wb_im_ref, bu_re, bu_im):
    nkb, ublk, sblk = wb_re_ref.shape
    for kb in range(nkb):
        ukb = ub[:, kb * ublk:(kb + 1) * ublk]
        bu_re[:, kb * sblk:(kb + 1) * sblk] = _dot(ukb, wb_re_ref[kb])
        bu_im[:, kb * sblk:(kb + 1) * sblk] = _dot(ukb, wb_im_ref[kb])


def _ssm_out_proj(h_re, h_im, wc_re_ref, wc_imn_ref):
    nkb, sblk, _ = wc_re_ref.shape
    ys = []
    for kb in range(nkb):
        hr = h_re[:, kb * sblk:(kb + 1) * sblk].astype(BF16)
        hi = h_im[:, kb * sblk:(kb + 1) * sblk].astype(BF16)
        ys.append(_dot(hr, wc_re_ref[kb]) + _dot(hi, wc_imn_ref[kb]))
    return jnp.concatenate(ys, axis=1)


def _ssm_glu(x, u, y, d_ref, wglu_ref):
    z = jax.nn.gelu(y + d_ref[...] * u).astype(BF16)
    gl = _dot(z, wglu_ref[...])
    d = x.shape[1]
    return x + gl[:, :d] * jax.nn.sigmoid(gl[:, d:])


def _split3(v):
    hi = v.astype(BF16)
    r1 = v - hi.astype(F32)
    mid = r1.astype(BF16)
    lo = (r1 - mid.astype(F32)).astype(BF16)
    return hi, mid, lo


def _ssm_prompt_kernel(x_ref, g_ref, lam_re_ref, lam_im_ref, wb_re_ref, wb_im_ref,
                       wc_re_ref, wc_imn_ref, d_ref, wglu_ref, perm_ref, permt_ref,
                       out_ref, st_re_ref, st_im_ref, bu_re_all, bu_im_all, *, lc, bsz, lane_chunk):
    @pl.when(pl.program_id(0) == 0)
    def _():
        st_re_ref[...] = jnp.zeros_like(st_re_ref)
        st_im_ref[...] = jnp.zeros_like(st_im_ref)

    for sub in range(bu_re_all.shape[0]):
        ts = slice(sub * lc, (sub + 1) * lc)
        _ssm_prompt_chunk(x_ref[:, ts, :], g_ref, lam_re_ref, lam_im_ref, wb_re_ref, wb_im_ref,
                          wc_re_ref, wc_imn_ref, d_ref, wglu_ref, perm_ref, permt_ref,
                          out_ref.at[:, ts, :], st_re_ref, st_im_ref, bu_re_all.at[sub], bu_im_all.at[sub],
                          lc=lc, bsz=bsz, lane_chunk=lane_chunk)


def _ssm_prompt_chunk(x3, g_ref, lam_re_ref, lam_im_ref, wb_re_ref, wb_im_ref,
                      wc_re_ref, wc_imn_ref, d_ref, wglu_ref, perm_ref, permt_ref,
                      out_ref, st_re_ref, st_im_ref, bu_re, bu_im, *, lc, bsz, lane_chunk):
    d = x3.shape[2]
    x = x3.reshape(bsz * lc, d)
    u = _rms(x, g_ref[...])
    ub_tb = _dot(perm_ref[...], u.astype(BF16)).astype(BF16)
    _ssm_in_proj(ub_tb, wb_re_ref, wb_im_ref, bu_re, bu_im)

    gp = bu_re.shape[1]
    for c0 in range(0, gp, lane_chunk):
        sl = slice(c0, c0 + lane_chunk)
        lre = jnp.broadcast_to(lam_re_ref[:, sl], (bsz, lane_chunk))
        lim = jnp.broadcast_to(lam_im_ref[:, sl], (bsz, lane_chunk))
        hr = st_re_ref[:, sl]
        hi = st_im_ref[:, sl]
        for t in range(lc):
            rows = slice(t * bsz, (t + 1) * bsz)
            hr, hi = (lre * hr - lim * hi + bu_re[rows, sl], lre * hi + lim * hr + bu_im[rows, sl])
            bu_re[rows, sl] = hr
            bu_im[rows, sl] = hi
        st_re_ref[:, sl] = hr
        st_im_ref[:, sl] = hi

    permt = permt_ref[...]
    hi, mid, lo = _split3(_ssm_out_proj(bu_re, bu_im, wc_re_ref, wc_imn_ref))
    y = (_dot(permt, hi) + _dot(permt, mid)) + _dot(permt, lo)
    out_ref[...] = _ssm_glu(x, u, y, d_ref, wglu_ref).reshape(bsz, lc, d)


def _ssm_sample_kernel(x_ref, g_ref, lam_re_ref, lam_im_ref, h0_re_ref, h0_im_ref, wb_re_ref, wb_im_ref,
                       wc_re_ref, wc_imn_ref, d_ref, wglu_ref,
                       out_ref, h_re_ref, h_im_ref):
    x = x_ref[...]
    u = _rms(x, g_ref[...])
    _ssm_in_proj(u.astype(BF16), wb_re_ref, wb_im_ref, h_re_ref, h_im_ref)
    lre = lam_re_ref[...]
    lim = lam_im_ref[...]
    h0r = h0_re_ref[...]
    h0i = h0_im_ref[...]
    h_re_ref[...] = lre * h0r - lim * h0i + h_re_ref[...]
    h_im_ref[...] = lre * h0i + lim * h0r + h_im_ref[...]
    y = _ssm_out_proj(h_re_ref, h_im_ref, wc_re_ref, wc_imn_ref)
    n = x.shape[0]
    out_ref[0:n, :] = _ssm_glu(x, u, y, d_ref, wglu_ref)
    out_ref[n:, :] = jnp.zeros((out_ref.shape[0] - n, out_ref.shape[1]), F32)


def _ssm_prompt(x, g, lam_re, lam_im, wb_re, wb_im, wc_re, wc_imn, d_skip, wglu, *, lc, nsub):
    bsz, seq, d = x.shape
    assert seq % (lc * nsub) == 0
    gp = lam_re.shape[1]
    r = lc * bsz
    perm = np.zeros((r, r), np.float32)
    for b in range(bsz):
        for t in range(lc):
            perm[t * bsz + b, b * lc + t] = 1.0
    permt = jnp.asarray(perm.T, BF16)
    perm = jnp.asarray(perm, BF16)
    kern = functools.partial(_ssm_prompt_kernel, lc=lc, bsz=bsz, lane_chunk=8 * LANES)
    return pl.pallas_call(
        kern,
        grid=(seq // (lc * nsub),),
        in_specs=[pl.BlockSpec((bsz, lc * nsub, d), lambda c: (0, c, 0)),
                  _const_spec(g.shape), _const_spec(lam_re.shape), _const_spec(lam_im.shape),
                  _const_spec(wb_re.shape), _const_spec(wb_im.shape),
                  _const_spec(wc_re.shape), _const_spec(wc_imn.shape),
                  _const_spec(d_skip.shape), _const_spec(wglu.shape),
                  _const_spec(perm.shape), _const_spec(permt.shape)],
        out_specs=[pl.BlockSpec((bsz, lc * nsub, d), lambda c: (0, c, 0)),
                   _const_spec((bsz, gp)), _const_spec((bsz, gp))],
        out_shape=[jax.ShapeDtypeStruct((bsz, seq, d), F32),
                   jax.ShapeDtypeStruct((bsz, gp), F32), jax.ShapeDtypeStruct((bsz, gp), F32)],
        scratch_shapes=[pltpu.VMEM((nsub, r, gp), F32), pltpu.VMEM((nsub, r, gp), F32)],
        compiler_params=_params("arbitrary"),
        name="s5_prompt",
    )(x, g, lam_re, lam_im, wb_re, wb_im, wc_re, wc_imn, d_skip, wglu, perm, permt)


def _ssm_sample(x, g, lam_re, lam_im, h0_re, h0_im, wb_re, wb_im, wc_re, wc_imn, d_skip, wglu, *, pad_rows):
    n, d = x.shape
    gp = lam_re.shape[1]
    return pl.pallas_call(
        _ssm_sample_kernel,
        out_shape=[jax.ShapeDtypeStruct((pad_rows, d), F32),
                   jax.ShapeDtypeStruct((n, gp), F32), jax.ShapeDtypeStruct((n, gp), F32)],
        compiler_params=_params(),
        name="s5_sample",
    )(x, g, lam_re, lam_im, h0_re, h0_im, wb_re, wb_im, wc_re, wc_imn, d_skip, wglu)


def _ffn_chunks(d_ff):
    step = 3 * MXU_DIM
    return [(c, min(c + step, d_ff)) for c in range(0, d_ff, step)]


def _ffn_kernel(xp_ref, xs_ref, gffn_ref, wgu_ref, wd_ref, gkv_ref, wkv_ref, gq_ref, wq_ref,
                x2_ref, kv_ref, q_ref):
    x = jnp.where(pl.program_id(0) < pl.num_programs(0) - 1, xp_ref[...], xs_ref[...])
    hb = _rms(x, gffn_ref[...]).astype(BF16)
    d_ff = wd_ref.shape[0]
    acc = None
    for c0, c1 in _ffn_chunks(d_ff):
        a = _dot(hb, wgu_ref[:, c0:c1])
        b = _dot(hb, wgu_ref[:, d_ff + c0:d_ff + c1])
        part = _dot((jax.nn.silu(a) * b).astype(BF16), wd_ref[c0:c1, :])
        acc = part if acc is None else acc + part
    x2 = x + acc
    x2_ref[...] = x2
    kv_ref[...] = _dot(_rms(x2, gkv_ref[...]).astype(BF16), wkv_ref[...])
    q_ref[...] = _dot(_rms(x2, gq_ref[...]).astype(BF16), wq_ref[...]).astype(BF16)


def _ffn(xp, xs, gffn, wgu, wd, gkv, wkv, gq, wq):
    tm, d = xs.shape
    npt = xp.shape[0] // tm
    rows = xp.shape[0] + tm
    kvw = wkv.shape[1]
    nq = wq.shape[1]
    return pl.pallas_call(
        _ffn_kernel,
        grid=(npt + 1,),
        in_specs=[pl.BlockSpec((tm, d), lambda i: (jnp.minimum(i, npt - 1), 0)),
                  _const_spec(xs.shape),
                  _const_spec(gffn.shape), _const_spec(wgu.shape), _const_spec(wd.shape),
                  _const_spec(gkv.shape), _const_spec(wkv.shape),
                  _const_spec(gq.shape), _const_spec(wq.shape)],
        out_specs=[pl.BlockSpec((tm, d), lambda i: (i, 0)),
                   pl.BlockSpec((tm, kvw), lambda i: (i, 0)),
                   pl.BlockSpec((tm, nq), lambda i: (i, 0))],
        out_shape=[jax.ShapeDtypeStruct((rows, d), F32),
                   jax.ShapeDtypeStruct((rows, kvw), F32),
                   jax.ShapeDtypeStruct((rows, nq), BF16)],
        compiler_params=_params("arbitrary"),
        name="ffn_kv_q",
    )(xp, xs, gffn, wgu, wd, gkv, wkv, gq, wq)


def _t5_bucket(dist, num_buckets):
    max_exact = num_buckets // 2
    d = jnp.maximum(dist, 0)
    large = max_exact + (jnp.log(jnp.maximum(d, 1).astype(F32) / max_exact)
                         / math.log(MAX_DISTANCE / max_exact) * (num_buckets - max_exact)).astype(jnp.int32)
    large = jnp.minimum(large, num_buckets - 1)
    return jnp.where(d < max_exact, d, large)


def _bias_kernel(bm_ref, bs_ref, rb_ref, bias_ref, bias_s_ref, *, kvh, rep):
    nb, nh = rb_ref.shape
    bm = bm_ref[...]
    bs = bs_ref[...]

    def lookup(buckets, h):
        terms = [jnp.where(buckets == k, rb_ref[k, h], 0.0) for k in range(nb)]
        while len(terms) > 1:
            terms = [a + b for a, b in zip(terms[0::2], terms[1::2])] + (terms[-1:] if len(terms) % 2 else [])
        return terms[0]

    w = bm.shape[0]
    for g in range(kvh):
        for r in range(rep):
            bias_ref[g, :, r * w:(r + 1) * w] = lookup(bm, g * rep + r) * math.log2(math.e)
            bias_s_ref[g, r:r + 1, :] = lookup(bs, g * rep + r)


def _bias_tables(rel_bias, window, kvh, rep):
    nb, nh = rel_bias.shape
    key = jnp.arange(window)[:, None]
    qry = jnp.arange(window)[None, :]
    bm = _t5_bucket((qry - key) % window, nb).astype(jnp.int32)
    bs = _t5_bucket(window - 1 - qry, nb).astype(jnp.int32)
    kern = functools.partial(_bias_kernel, kvh=kvh, rep=rep)
    return pl.pallas_call(
        kern,
        in_specs=[pl.BlockSpec(memory_space=pltpu.VMEM), pl.BlockSpec(memory_space=pltpu.VMEM),
                  pl.BlockSpec(memory_space=pltpu.SMEM)],
        out_shape=[jax.ShapeDtypeStruct((kvh, window, rep * window), F32),
                   jax.ShapeDtypeStruct((kvh, rep, window), F32)],
        name="t5_bias",
    )(bm, bs, rel_bias)


def _attn_prompt_kernel(q_ref, kvc_ref, kvp_ref, bias_ref, sink_ref, ot_ref, *, kvh, rep, hd, scale):
    w = kvp_ref.shape[0]
    kv_all = kvc_ref[...]
    for sub in range(q_ref.shape[0] // w):
        rows = slice(sub * w, (sub + 1) * w)
        kvp = kvp_ref[...] if sub == 0 else kv_all[(sub - 1) * w:sub * w]
        first = pl.program_id(1) == 0 if sub == 0 else None
        _attn_prompt_block(q_ref[rows, :], kv_all[rows], kvp, first, bias_ref, sink_ref, ot_ref, sub * w,
                           kvh=kvh, rep=rep, hd=hd, scale=scale)


def _attn_prompt_block(q, kvc, kvp, first, bias_ref, sink_ref, ot_ref, col0, *, kvh, rep, hd, scale):
    w = q.shape[0]
    kvw = kvh * hd
    lane = lax.broadcasted_iota(jnp.int32, (1, LANES), 1)
    key = lax.broadcasted_iota(jnp.int32, (w, rep * w), 0)
    qry = lax.broadcasted_iota(jnp.int32, (w, rep * w), 1) % w
    upper = key > qry
    mask_add = None if first is None else jnp.where(jnp.logical_and(upper, first), NEG_INF, 0.0)
    log2e = math.log2(math.e)
    heads_per_blk = LANES // hd
    for p in range(kvw // LANES):
        cs = slice(p * LANES, (p + 1) * LANES)
        kc, kp = kvc[:, cs], kvp[:, cs]
        vs = slice(kvw + p * LANES, kvw + (p + 1) * LANES)
        vb = jnp.concatenate([kvp[:, vs], kvc[:, vs]], axis=0).astype(BF16)
        qg = jnp.concatenate([q[:, r * kvw + p * LANES:r * kvw + (p + 1) * LANES] for r in range(rep)], axis=0)
        kbs = []
        for half in range(heads_per_blk):
            lmask = jnp.logical_and(lane >= half * hd, lane < (half + 1) * hd)
            kbs += [jnp.where(lmask, kp, 0.0), jnp.where(lmask, kc, 0.0)]
        s_all = lax.dot_general(jnp.concatenate(kbs, axis=0).astype(BF16), qg, (((1,), (1,)), ((), ())),
                                preferred_element_type=F32)
        pcats, denoms = [], []
        for half in range(heads_per_blk):
            g = p * heads_per_blk + half
            s = s_all[half * 2 * w:(half + 1) * 2 * w]
            bias = bias_ref[g] if first is None else bias_ref[g] + mask_add
            sc = jnp.where(upper, s[:w], s[w:]) * (scale * log2e) + bias
            sink = jnp.concatenate([jnp.full((1, w), sink_ref[g * rep + r] * log2e, F32) for r in range(rep)],
                                   axis=1)
            m = jnp.maximum(jnp.max(sc, axis=0, keepdims=True), sink)
            pe = jnp.exp2(sc - m)
            denoms.append(jnp.sum(pe, axis=0, keepdims=True) + jnp.exp2(sink - m))
            pcats.append(jnp.concatenate([jnp.where(upper, pe, 0.0), jnp.where(upper, 0.0, pe)], axis=0).astype(BF16))
        og_all = lax.dot_general(vb, jnp.concatenate(pcats, axis=1), (((0,), (0,)), ((), ())),
                                 preferred_element_type=F32)
        halves = [og_all[half * hd:(half + 1) * hd, half * rep * w:(half + 1) * rep * w] / denoms[half]
                  for half in range(heads_per_blk)]
        o_blk = jnp.concatenate(halves, axis=0).astype(BF16)
        for r in range(rep):
            ot_ref[r * kvw + p * LANES:r * kvw + (p + 1) * LANES, col0:col0 + w] = o_blk[:, r * w:(r + 1) * w]


def _attn_prompt(q, kv, bias, sinks, *, bsz, nblk, window, kvh, rep, hd):
    rows, nq = q.shape
    kvw2 = kv.shape[1]
    nsub = next(n for n in (4, 2, 1) if nblk % n == 0 and rows % (n * window) == 0)
    nstep = nblk // nsub
    kern = functools.partial(_attn_prompt_kernel, kvh=kvh, rep=rep, hd=hd, scale=1.0 / math.sqrt(hd))
    return pl.pallas_call(
        kern,
        grid=(bsz, nstep),
        in_specs=[pl.BlockSpec((nsub * window, nq), lambda b, i: (b * nstep + i, 0)),
                  pl.BlockSpec((nsub * window, kvw2), lambda b, i: (b * nstep + i, 0)),
                  pl.BlockSpec((window, kvw2), lambda b, i: (b * nblk + jnp.maximum(nsub * i - 1, 0), 0)),
                  _const_spec(bias.shape),
                  pl.BlockSpec(memory_space=pltpu.SMEM)],
        out_specs=pl.BlockSpec((nq, nsub * window), lambda b, i: (0, b * nstep + i)),
        out_shape=jax.ShapeDtypeStruct((nq, rows), BF16),
        compiler_params=_params("arbitrary", "arbitrary"),
        name="swa_prompt",
    )(q, kv, kv, bias, sinks)


def _attn_sample_kernel(q_ref, kv_ref, ck_ref, cv_ref, bias_ref, sink_ref, o_ref, nk_ref, nv_ref,
                        *, kvh, rep, hd, scale):
    nb, w, kvw = ck_ref.shape
    kv = kv_ref[...]
    lane = lax.broadcasted_iota(jnp.int32, (nb, rep, kvw), 2)

    def shifted(c_ref, n_ref, new):
        flat = c_ref[...].reshape(nb * w, kvw)
        n_ref[...] = pltpu.roll(flat, nb * w - 1, axis=0).reshape(nb, w, kvw)
        n_ref[:, w - 1:w, :] = new
        return n_ref[...].astype(BF16)

    nkb = shifted(ck_ref, nk_ref, kv[:, :, :kvw])
    nvb = shifted(cv_ref, nv_ref, kv[:, :, kvw:])
    q = q_ref[...].astype(F32)
    o = jnp.zeros((nb, rep, kvw), F32)
    for g in range(kvh):
        lmask = jnp.logical_and(lane >= g * hd, lane < (g + 1) * hd)
        qg = jnp.where(lmask, q, 0.0).astype(BF16)
        s = jnp.einsum("nrc,njc->nrj", qg, nkb, preferred_element_type=F32)
        sc = s * scale + bias_ref[g][None]
        sink = sink_ref[g][None]
        m = jnp.maximum(jnp.max(sc, axis=-1, keepdims=True), sink)
        pe = jnp.exp(sc - m)
        probs = pe / (jnp.sum(pe, axis=-1, keepdims=True) + jnp.exp(sink - m))
        og = jnp.einsum("nrj,njc->nrc", probs.astype(BF16), nvb, preferred_element_type=F32)
        o = jnp.where(lmask, og, o)
    o_ref[...] = o.astype(BF16)


def _attn_sample(q3, kv3, ck, cv, bias_s, sink_s, *, nb, kvh, rep, hd):
    n, w, kvw = ck.shape
    kern = functools.partial(_attn_sample_kernel, kvh=kvh, rep=rep, hd=hd, scale=1.0 / math.sqrt(hd))
    cache_spec = pl.BlockSpec((nb, w, kvw), lambda i: (i, 0, 0))
    return pl.pallas_call(
        kern,
        grid=(n // nb,),
        in_specs=[pl.BlockSpec((nb, rep, kvw), lambda i: (i, 0, 0)),
                  pl.BlockSpec((nb, 1, 2 * kvw), lambda i: (i, 0, 0)),
                  cache_spec, cache_spec,
                  _const_spec(bias_s.shape), _const_spec(sink_s.shape)],
        out_specs=[pl.BlockSpec((nb, rep, kvw), lambda i: (i, 0, 0)), cache_spec, cache_spec],
        out_shape=[jax.ShapeDtypeStruct((n, rep, kvw), BF16),
                   jax.ShapeDtypeStruct((n, w, kvw), F32), jax.ShapeDtypeStruct((n, w, kvw), F32)],
        compiler_params=_params("arbitrary"),
        name="swa_sample",
    )(q3, kv3, ck, cv, bias_s, sink_s)


CHUNK = 16
BIG_COPY = 128
SUB = 256
UP_ROW_BLOCK = 1024
DOWN_ROW_BLOCK = 1024
TOKEN_TILE = 512
MOE_TILE = 512


def _chunk_copy(src, src_row, dst, dst_row, sem, rows=CHUNK):
    return pltpu.make_async_copy(src.at[pl.ds(pl.multiple_of(src_row, CHUNK), rows), :],
                                 dst.at[pl.ds(pl.multiple_of(dst_row, CHUNK), rows), :], sem)


def _start_copies(src, src_row, dst, dst_row, sem, nrows):
    nbig = lax.shift_right_logical(nrows, int(math.log2(BIG_COPY)))

    def big(c, carry):
        _chunk_copy(src, src_row + c * BIG_COPY, dst, dst_row + c * BIG_COPY, sem, BIG_COPY).start()
        return carry

    lax.fori_loop(0, nbig, big, 0)
    done = nbig * BIG_COPY

    def small(c, carry):
        _chunk_copy(src, src_row + done + c * CHUNK, dst, dst_row + done + c * CHUNK, sem).start()
        return carry

    lax.fori_loop(0, lax.shift_right_logical(nrows - done, int(math.log2(CHUNK))), small, 0)


def _moe_route_kernel(x2_ref, ot_ref, wo_ref, gffn_ref, wr_ref, lst_ref, ust_ref,
                      x3_ref, cm_ref, tmeta_ref, xs_hbm,
                      comp_s, carry_s, zero_s, base_v, pend_sm, sem, *, ne, cap, n_valid):
    i = pl.program_id(0)
    nt = pl.num_programs(0)
    tm = x2_ref.shape[0]
    w = comp_s.shape[1]
    slot = lax.rem(i, 2)

    @pl.when(i == 0)
    def _():
        base_v[...] = jnp.zeros_like(base_v)
        zero_s[...] = jnp.zeros_like(zero_s)
        carry_s[...] = jnp.zeros_like(carry_s)

    def drain(n):
        def body(c, carry):
            _chunk_copy(zero_s, 0, xs_hbm, 0, sem).wait()
            return carry
        lax.fori_loop(0, n, body, 0)

    @pl.when(i > 0)
    def _():
        drain(pend_sm[0])

    x3 = x2_ref[...] + lax.dot_general(ot_ref[...], wo_ref[...], (((0,), (0,)), ((), ())),
                                       preferred_element_type=F32)
    x3_ref[...] = x3
    hb = _rms(x3, gffn_ref[...]).astype(BF16)

    lane = lax.broadcasted_iota(jnp.int32, (tm, LANES), 1)
    logits = jnp.where(lane < ne, _dot(hb, wr_ref[...]), -jnp.inf)
    m1 = jnp.max(logits, axis=-1, keepdims=True)
    i1 = jnp.min(jnp.where(logits == m1, lane, LANES), axis=-1, keepdims=True)
    rest = jnp.where(lane == i1, -jnp.inf, logits)
    m2 = jnp.max(rest, axis=-1, keepdims=True)
    i2 = jnp.min(jnp.where(rest == m2, lane, LANES), axis=-1, keepdims=True)
    e2 = jnp.exp(m2 - m1)
    g1 = 1.0 / (1.0 + e2)
    g2 = e2 / (1.0 + e2)

    live = i * tm + lax.broadcasted_iota(jnp.int32, (tm, 1), 0) < n_valid
    sel = jnp.where(jnp.logical_and(live, jnp.logical_or(lane == i1, lane == i2)), 1.0, 0.0)
    rank = _dot(lst_ref[...], sel.astype(BF16))
    cnt = jnp.sum(sel, axis=0, keepdims=True)
    fill = base_v[...]
    rem = fill - jnp.floor(fill / CHUNK) * CHUNK
    cpad = jnp.floor((rem + cnt + (CHUNK - 1)) / CHUNK) * CHUNK
    loff = _dot(jnp.broadcast_to(cpad, (8, LANES)).astype(BF16), ust_ref[...])[0:1]
    dest = loff + rem + rank
    ld1 = jnp.where(live, jnp.sum(jnp.where(lane == i1, dest, 0.0), axis=-1, keepdims=True), -1.0)
    ld2 = jnp.where(live, jnp.sum(jnp.where(lane == i2, dest, 0.0), axis=-1, keepdims=True), -1.0)
    cm = jnp.where(lane == 0, ld1, jnp.where(lane == 1, ld2, jnp.where(lane == 2, g1, jnp.where(lane == 3, g2, 0.0))))
    cm_ref[...] = cm

    rm = cm.T
    rowi = lax.broadcasted_iota(jnp.int32, (w, tm), 0).astype(F32)
    place = (jnp.where(rowi == rm[0:1], 1.0, 0.0) + jnp.where(rowi == rm[1:2], 1.0, 0.0)).astype(BF16)
    comp_s[slot] = _dot(place, hb).astype(BF16)

    base = fill - rem
    srow = lax.broadcasted_iota(jnp.int32, (8, LANES), 0)
    tmeta_ref[...] = jnp.where(srow == 0, cpad, jnp.where(srow == 1, base, jnp.where(srow == 2, fill + cnt, 0.0))
                               ).astype(jnp.int32)
    base_v[...] = fill + cnt

    cpad_i = cpad.astype(jnp.int32)
    loff_i = loff.astype(jnp.int32)
    base_i = base.astype(jnp.int32)
    tail_i = (rem + cnt - jnp.floor((rem + cnt) / CHUNK) * CHUNK).astype(jnp.int32)
    total = 0
    tails = []
    for e in range(ne):
        n_e = cpad_i[0, e]
        src0 = loff_i[0, e]
        dst0 = base_i[0, e] + e * cap

        @pl.when(n_e > 0)
        def _(e=e, n_e=n_e, src0=src0, partial=tail_i[0, e] > 0):
            head = pl.ds(pl.multiple_of(src0, CHUNK), CHUNK)
            comp_s[slot, head, :] = comp_s[slot, head, :] + carry_s[e]
            last = comp_s[slot, pl.ds(pl.multiple_of(src0 + n_e - CHUNK, CHUNK), CHUNK), :]
            carry_s[e] = jnp.where(partial, last, jnp.zeros_like(last))

        _start_copies(comp_s.at[slot], src0, xs_hbm, dst0, sem, n_e)
        total = total + lax.shift_right_logical(n_e, int(math.log2(CHUNK)))
        tails.append(dst0 + n_e)
    pend_sm[0] = total

    @pl.when(i == nt - 1)
    def _():
        nz_total = 0
        for e in range(ne):
            end = tails[e]
            nz = lax.shift_right_logical(lax.rem(SUB - lax.rem(end, SUB), SUB), int(math.log2(CHUNK)))

            def zstart(c, carry, end=end):
                _chunk_copy(zero_s, 0, xs_hbm, end + c * CHUNK, sem).start()
                return carry

            lax.fori_loop(0, nz, zstart, 0)
            nz_total = nz_total + nz
        drain(total + nz_total)


def _moe_route(x2, ot, wo, gffn, wr_pad, *, tm, ne, cap, n_valid):
    rows, d = x2.shape
    nt = rows // tm
    w = _staging_rows(tm, ne)
    lst = jnp.asarray(np.tril(np.ones((tm, tm), np.float32), -1), BF16)
    ust = jnp.asarray(np.triu(np.ones((LANES, LANES), np.float32), 1), BF16)
    kern = functools.partial(_moe_route_kernel, ne=ne, cap=cap, n_valid=n_valid)
    return pl.pallas_call(
        kern,
        grid=(nt,),
        in_specs=[pl.BlockSpec((tm, d), lambda i: (i, 0)),
                  pl.BlockSpec((ot.shape[0], tm), lambda i: (0, i)),
                  _const_spec(wo.shape), _const_spec(gffn.shape), _const_spec(wr_pad.shape),
                  _const_spec(lst.shape), _const_spec(ust.shape)],
        out_specs=[pl.BlockSpec((tm, d), lambda i: (i, 0)),
                   pl.BlockSpec((tm, LANES), lambda i: (i, 0)),
                   pl.BlockSpec((8, LANES), lambda i: (i, 0)),
                   pl.BlockSpec(memory_space=pl.ANY)],
        out_shape=[jax.ShapeDtypeStruct((rows, d), F32),
                   jax.ShapeDtypeStruct((rows, LANES), F32),
                   jax.ShapeDtypeStruct((nt * 8, LANES), jnp.int32),
                   jax.ShapeDtypeStruct((ne * cap, d), BF16)],
        scratch_shapes=[pltpu.VMEM((2, w, d), BF16), pltpu.VMEM((ne, CHUNK, d), BF16), pltpu.VMEM((CHUNK, d), BF16),
                        pltpu.VMEM((1, LANES), F32), pltpu.SMEM((1,), jnp.int32),
                        pltpu.SemaphoreType.DMA(())],
        compiler_params=_params("arbitrary"),
        name="moe_route",
    )(x2, ot, wo, gffn, wr_pad, lst, ust)


def _staging_rows(tm, ne):
    return -(-(TOP_K * tm + 2 * ne * (CHUNK - 1)) // LANES) * LANES


def _max_blocks(rows, ne, row_block):
    return TOP_K * rows // row_block + ne


def _expert_steps(nrows, nf, max_blocks, row_block):
    ne = nrows.shape[0]
    nblk = (nrows + row_block - 1) // row_block
    cum = jnp.cumsum(nblk) * nf
    total = cum[-1]
    s = jnp.minimum(jnp.arange(max_blocks * nf, dtype=jnp.int32), total - 1)
    e = jnp.minimum(jnp.sum((s[:, None] >= cum[None, :]).astype(jnp.int32), axis=1), ne - 1)
    nb_e = nblk[e]
    within = s - (cum[e] - nb_e * nf)
    f = within // nb_e
    pos = within - f * nb_e
    r = (pos + nb_e - 1) % nb_e
    nsub = jnp.clip((nrows[e] - r * row_block + SUB - 1) // SUB, 0, row_block // SUB)
    first = (pos == 0).astype(jnp.int32)
    return (e, f.astype(jnp.int32), r.astype(jnp.int32), first, nsub.astype(jnp.int32),
            jnp.reshape(total, (1,)).astype(jnp.int32))


def _moe_up_kernel(e_ref, f_ref, r_ref, first_ref, nsub_ref, n_ref, xs_ref, wg_ref, wu_ref, act_ref, wgb_s, wub_s):
    s = pl.program_id(0)

    @pl.when(s < n_ref[0])
    def _():
        @pl.when(first_ref[s] == 1)
        def _():
            wgb_s[...] = wg_ref[...].astype(BF16)
            wub_s[...] = wu_ref[...].astype(BF16)

        def gate_up(row0, nrows):
            rows = pl.ds(pl.multiple_of(row0, SUB), nrows)
            xsb = xs_ref[rows, :]
            a = _dot(xsb, wgb_s[...])
            b = _dot(xsb, wub_s[...])
            act_ref[rows, :] = (jax.nn.silu(a) * b).astype(BF16)

        nsub = nsub_ref[s]
        npair = lax.shift_right_logical(nsub, 1)

        def body(j, carry):
            gate_up(j * (2 * SUB), 2 * SUB)
            return carry

        lax.fori_loop(0, npair, body, 0)

        @pl.when(nsub > 2 * npair)
        def _():
            gate_up(npair * (2 * SUB), SUB)


def _moe_up(steps, xs, wgu, *, fc, cap, row_block):
    ne, d, dff2 = wgu.shape
    dff = dff2 // 2
    nf = dff // fc
    cb = cap // row_block
    grid_spec = pltpu.PrefetchScalarGridSpec(
        num_scalar_prefetch=6,
        grid=(steps[0].shape[0],),
        in_specs=[pl.BlockSpec((row_block, d), lambda s, e, f, r, *_: (e[s] * cb + r[s], 0)),
                  pl.BlockSpec((None, d, fc), lambda s, e, f, r, *_: (e[s], 0, f[s])),
                  pl.BlockSpec((None, d, fc), lambda s, e, f, r, *_: (e[s], 0, nf + f[s]))],
        out_specs=pl.BlockSpec((row_block, fc), lambda s, e, f, r, *_: (e[s] * cb + r[s], f[s])),
        scratch_shapes=[pltpu.VMEM((d, fc), BF16), pltpu.VMEM((d, fc), BF16)])
    return pl.pallas_call(
        _moe_up_kernel,
        grid_spec=grid_spec,
        out_shape=jax.ShapeDtypeStruct((ne * cap, dff), BF16),
        compiler_params=_params("arbitrary"),
        name="moe_up",
    )(*steps, xs, wgu, wgu)


def _moe_down_kernel(e_ref, f_ref, r_ref, first_ref, nsub_ref, n_ref, act_ref, wd_ref, ys_ref, wdb_s):
    s = pl.program_id(0)

    @pl.when(s < n_ref[0])
    def _():
        @pl.when(first_ref[s] == 1)
        def _():
            wdb_s[...] = wd_ref[...].astype(BF16)

        def project(row0, nrows):
            rows = pl.ds(pl.multiple_of(row0, SUB), nrows)
            ys_ref[rows, :] = _dot(act_ref[rows, :], wdb_s[...]).astype(BF16)

        nsub = nsub_ref[s]
        npair = lax.shift_right_logical(nsub, 1)

        def body(j, carry):
            project(j * (2 * SUB), 2 * SUB)
            return carry

        lax.fori_loop(0, npair, body, 0)

        @pl.when(nsub > 2 * npair)
        def _():
            project(npair * (2 * SUB), SUB)


def _moe_down(steps, act, wd, *, cap, row_block):
    ne, dff, d = wd.shape
    cb = cap // row_block
    grid_spec = pltpu.PrefetchScalarGridSpec(
        num_scalar_prefetch=6,
        grid=(steps[0].shape[0],),
        in_specs=[pl.BlockSpec((row_block, dff), lambda s, e, f, r, *_: (e[s] * cb + r[s], 0)),
                  pl.BlockSpec((None, dff, d), lambda s, e, f, r, *_: (e[s], 0, 0))],
        out_specs=pl.BlockSpec((row_block, d), lambda s, e, f, r, *_: (e[s] * cb + r[s], 0)),
        scratch_shapes=[pltpu.VMEM((dff, d), BF16)])
    return pl.pallas_call(
        _moe_down_kernel,
        grid_spec=grid_spec,
        out_shape=jax.ShapeDtypeStruct((ne * cap, d), BF16),
        compiler_params=_params("arbitrary", vmem=2 * (dff * d * 4 + row_block * (dff + d) * 2) + dff * d * 2
                                + 4 * SUB * d * 4),
        name="moe_down",
    )(*steps, act, wd)


def _moe_combine_kernel(cpad_ref, seg_ref, x3_ref, cm_ref, gfin_ref, ys_hbm, yp_ref, ys_ref, yloc_s, sem,
                        *, ne, npt):
    i = pl.program_id(0)
    nt = pl.num_programs(0)
    tm = x3_ref.shape[0]
    w = yloc_s.shape[1]
    slot = lax.rem(i, 2)
    shift = int(math.log2(CHUNK))

    def issue(t, sl):
        off = 0
        for e in range(ne):
            n_e = cpad_ref[t * ne + e]
            _start_copies(ys_hbm, seg_ref[t * ne + e], yloc_s.at[sl], off, sem.at[sl], n_e)
            off = off + n_e

    @pl.when(i == 0)
    def _():
        yloc_s[...] = jnp.zeros_like(yloc_s)
        issue(0, 0)

    @pl.when(i + 1 < nt)
    def _():
        issue(i + 1, 1 - slot)

    total = 0
    for e in range(ne):
        total = total + lax.shift_right_logical(cpad_ref[i * ne + e], shift)

    def wait(c, carry):
        _chunk_copy(ys_hbm, 0, yloc_s.at[slot], 0, sem.at[slot]).wait()
        return carry

    lax.fori_loop(0, total, wait, 0)

    cm = cm_ref[...]
    col = lax.broadcasted_iota(jnp.int32, (tm, w), 1).astype(F32)
    yl = yloc_s[slot]
    y1 = _dot(jnp.where(col == cm[:, 0:1], 1.0, 0.0).astype(BF16), yl)
    y2 = _dot(jnp.where(col == cm[:, 1:2], 1.0, 0.0).astype(BF16), yl)
    y = _rms(x3_ref[...] + cm[:, 2:3] * y1 + cm[:, 3:4] * y2, gfin_ref[...])

    @pl.when(i < npt)
    def _():
        yp_ref[...] = y

    @pl.when(i >= npt)
    def _():
        ys_ref[...] = y


def _moe_combine(cpad, seg, x3, cm, gfin, ys, *, tm, ne, sample_rows):
    rows, d = x3.shape
    nt = rows // tm
    npt = (rows - sample_rows) // tm
    w = _staging_rows(tm, ne)
    kern = functools.partial(_moe_combine_kernel, ne=ne, npt=npt)
    grid_spec = pltpu.PrefetchScalarGridSpec(
        num_scalar_prefetch=2,
        grid=(nt,),
        in_specs=[pl.BlockSpec((tm, d), lambda i, *_: (i, 0)),
                  pl.BlockSpec((tm, LANES), lambda i, *_: (i, 0)),
                  pl.BlockSpec(gfin.shape, lambda i, *_: (0, 0)),
                  pl.BlockSpec(memory_space=pl.ANY)],
        out_specs=[pl.BlockSpec((tm, d), lambda i, *_: (jnp.minimum(i, npt - 1), 0)),
                   pl.BlockSpec((tm, d), lambda i, *_: (jnp.maximum(i - npt, 0), 0))],
        scratch_shapes=[pltpu.VMEM((2, w, d), BF16), pltpu.SemaphoreType.DMA((2,))])
    return pl.pallas_call(
        kern,
        grid_spec=grid_spec,
        out_shape=[jax.ShapeDtypeStruct((npt * tm, d), F32), jax.ShapeDtypeStruct((sample_rows, d), F32)],
        compiler_params=_params("arbitrary"),
        name="moe_combine",
    )(cpad, seg, x3, cm, gfin, ys)


def _moe(x2, ot, wo, gffn, wr, wgu, wd, gfin, *, tm, fc, n_valid, sample_rows):
    rows, d = x2.shape
    ne = wr.shape[1]
    nt = rows // tm
    nf = wgu.shape[2] // 2 // fc
    cap = -(-(rows + SUB) // UP_ROW_BLOCK) * UP_ROW_BLOCK
    wr_pad = jnp.pad(wr, ((0, 0), (0, LANES - ne))).astype(BF16)
    x3, cm, tmeta, xs = _moe_route(x2, ot, wo, gffn, wr_pad, tm=tm, ne=ne, cap=cap, n_valid=n_valid)
    tmeta = tmeta.reshape(nt, 8, LANES)[:, :, :ne]
    cpad = tmeta[:, 0, :].reshape(nt * ne)
    seg = (tmeta[:, 1, :] + jnp.arange(ne, dtype=jnp.int32)[None, :] * cap).reshape(nt * ne)
    nrows = tmeta[nt - 1, 2, :]
    up_steps = _expert_steps(nrows, nf, _max_blocks(rows, ne, UP_ROW_BLOCK), UP_ROW_BLOCK)
    act = _moe_up(up_steps, xs, wgu, fc=fc, cap=cap, row_block=UP_ROW_BLOCK)
    down_steps = _expert_steps(nrows, 1, _max_blocks(rows, ne, DOWN_ROW_BLOCK), DOWN_ROW_BLOCK)
    ys = _moe_down(down_steps, act, wd, cap=cap, row_block=DOWN_ROW_BLOCK)
    return _moe_combine(cpad, seg, x3, cm, gfin, ys, tm=tm, ne=ne, sample_rows=sample_rows)


def kernel(x_prompt, x_sample, state_ssm_re, state_ssm_im, cache_k_win, cache_v_win, g_mix, g_ffn, g_kv, g_final, ssm_a_re, ssm_a_im, ssm_log_dt, ssm_b_re, ssm_b_im, ssm_c_re, ssm_c_im, ssm_d, w_glu, w_kv, w_q, w_o, attn_sinks, rel_bias, w_ffn_gate_up, w_ffn_down, w_router, w_exp_gate_up, w_exp_down):
    bsz, seq, d = x_prompt.shape
    ns, dec_seq, _ = x_sample.shape
    assert dec_seq == 1 and g_mix.shape[0] == 2 and ssm_a_re.shape[0] == 1 and w_q.shape[0] == 1
    _, g, p = ssm_a_re.shape
    gp = g * p
    window, kvh, hd = cache_k_win.shape[1:]
    kvw = kvh * hd
    nh = attn_sinks.shape[1]
    rep = nh // kvh
    nq = nh * hd
    assert bsz == 8 and ns % bsz == 0 and seq % window == 0 and LANES % hd == 0

    lam_re, lam_im, wb_re, wb_im, wc_re, wc_imn = _zoh(ssm_a_re[0], ssm_a_im[0], ssm_log_dt[0],
                                                        ssm_b_re[0], ssm_b_im[0], ssm_c_re[0], ssm_c_im[0])
    wglu = w_glu[0].astype(BF16)
    d_skip = ssm_d[0].reshape(1, d)
    wgu = w_ffn_gate_up[0].astype(BF16)
    wd = w_ffn_down[0].astype(BF16)
    wkv = w_kv.astype(BF16)
    wq = w_q[0].reshape(d, kvh, rep, hd).transpose(0, 2, 1, 3).reshape(d, nq).astype(BF16)
    wo = w_o[0].reshape(kvh, rep, hd, d).transpose(1, 0, 2, 3).reshape(nq, d).astype(BF16)
    bias, bias_s = _bias_tables(rel_bias, window, kvh, rep)
    sinks = attn_sinks[0]
    sink_s = sinks.reshape(kvh, rep, 1)

    tm = TOKEN_TILE
    npr = seq * bsz
    assert npr % tm == 0 and ns <= tm

    x1_p, st_re, st_im = _ssm_prompt(x_prompt, g_mix[0:1], lam_re, lam_im, wb_re, wb_im, wc_re, wc_imn,
                                     d_skip, wglu, lc=32, nsub=2)
    x1_s, hs_re, hs_im = _ssm_sample(x_sample.reshape(ns, d), g_mix[0:1], lam_re, lam_im,
                                     state_ssm_re[0].reshape(ns, gp), state_ssm_im[0].reshape(ns, gp),
                                     wb_re, wb_im, wc_re, wc_imn, d_skip, wglu, pad_rows=tm)

    x2, kv, q = _ffn(x1_p.reshape(npr, d), x1_s, g_ffn[0:1], wgu, wd, g_kv.reshape(1, d), wkv, g_mix[1:2], wq)

    ot = _attn_prompt(q, kv, bias, sinks, bsz=bsz, nblk=seq // window, window=window, kvh=kvh, rep=rep, hd=hd)
    o_s, nk_s, nv_s = _attn_sample(q[npr:npr + ns].reshape(ns, rep, kvw), kv[npr:npr + ns].reshape(ns, 1, 2 * kvw),
                                   cache_k_win.reshape(ns, window, kvw), cache_v_win.reshape(ns, window, kvw),
                                   bias_s, sink_s, nb=32, kvh=kvh, rep=rep, hd=hd)
    ot = lax.dynamic_update_slice(ot, jnp.pad(o_s.reshape(ns, nq).T, ((0, 0), (0, tm - ns))), (0, npr))

    y_p, y_s = _moe(x2, ot, wo, g_ffn[1:2], w_router[0], w_exp_gate_up[0], w_exp_down[0], g_final.reshape(1, d),
                    tm=MOE_TILE, fc=w_exp_down.shape[2] // 2, n_valid=npr + ns, sample_rows=tm)

    y_prompt = y_p.reshape(bsz, seq, d)
    y_sample = y_s[:ns].reshape(ns, 1, d)
    kv_tail = jnp.stack([kv[(b + 1) * seq - window:(b + 1) * seq] for b in range(bsz)])
    kv_tail = kv_tail.reshape(bsz, window, 2, kvh, hd).transpose(2, 0, 1, 3, 4)
    return (y_prompt, y_sample,
            st_re.reshape(1, bsz, g, p), st_im.reshape(1, bsz, g, p), kv_tail[0], kv_tail[1],
            hs_re.reshape(1, ns, g, p), hs_im.reshape(1, ns, g, p),
            nk_s.reshape(ns, window, kvh, hd), nv_s.reshape(ns, window, kvh, hd))
```

```python
import functools
import math

import numpy as np
import jax
import jax.numpy as jnp
from jax import lax
from jax.experimental import pallas as pl
from jax.experimental.pallas import tpu as pltpu

F32 = jnp.float32
BF16 = jnp.bfloat16

EPS = 1e-6
NEG_INF = -1e30
TOP_K = 2
MAX_DISTANCE = 128
MXU_DIM = 256
LANES = 128
VMEM_LIMIT_BYTES = 56 * 1024 * 1024


def _dot(a, b):
    return jnp.dot(a, b, preferred_element_type=F32)


def _rms(x, g):
    return x * lax.rsqrt(jnp.mean(x * x, axis=-1, keepdims=True) + EPS) * g


def _const_spec(shape):
    nd = len(shape)
    return pl.BlockSpec(shape, lambda *_: (0,) * nd)


def _params(*sem, vmem=VMEM_LIMIT_BYTES):
    return pltpu.CompilerParams(dimension_semantics=sem, vmem_limit_bytes=vmem)


def _zoh_kernel(a_re_ref, a_im_ref, log_dt_ref, b_re_ref, b_im_ref, ct_re_ref, ct_im_ref,
                lam_re_ref, lam_im_ref, wb_re_ref, wb_im_ref, wc_re_ref, wc_imn_ref, *, hch, p):
    a_re = a_re_ref[...]
    a_im = a_im_ref[...]
    dt = jnp.exp(log_dt_ref[...])
    mag = jnp.exp(a_re * dt)
    lr = mag * jnp.cos(a_im * dt)
    li = mag * jnp.sin(a_im * dt)
    lam_re_ref[...] = lr
    lam_im_ref[...] = li
    nr = lr - 1.0
    den = a_re * a_re + a_im * a_im
    qr = (nr * a_re + li * a_im) / den
    qi = (li * a_re - nr * a_im) / den
    b_re = b_re_ref[...]
    b_im = b_im_ref[...]
    bb_re = qr * b_re - qi * b_im
    bb_im = qr * b_im + qi * b_re

    nkb, ublk, sblk = wb_re_ref.shape
    gpb = ublk // hch
    sh_h, sh_p = int(math.log2(hch)), int(math.log2(p))
    row = lax.broadcasted_iota(jnp.int32, (ublk, sblk), 0)
    col = lax.broadcasted_iota(jnp.int32, (ublk, sblk), 1)
    diag_in = lax.shift_right_logical(row, sh_h) == lax.shift_right_logical(col, sh_p)
    row = lax.broadcasted_iota(jnp.int32, (sblk, ublk), 0)
    col = lax.broadcasted_iota(jnp.int32, (sblk, ublk), 1)
    diag_out = lax.shift_right_logical(row, sh_p) == lax.shift_right_logical(col, sh_h)
    spread = jnp.where(lax.broadcasted_iota(jnp.int32, (hch, ublk), 0)
                       == jnp.bitwise_and(lax.broadcasted_iota(jnp.int32, (hch, ublk), 1), hch - 1),
                       1.0, 0.0).astype(BF16)
    for kb in range(nkb):
        cs = slice(kb * sblk, (kb + 1) * sblk)
        wb_re_ref[kb] = jnp.where(diag_in, jnp.concatenate([bb_re[:, cs]] * gpb, axis=0), 0.0).astype(BF16)
        wb_im_ref[kb] = jnp.where(diag_in, jnp.concatenate([bb_im[:, cs]] * gpb, axis=0), 0.0).astype(BF16)
        c_re = _dot(ct_re_ref[cs, :].astype(BF16), spread)
        c_im = _dot(ct_im_ref[cs, :].astype(BF16), spread)
        wc_re_ref[kb] = jnp.where(diag_out, c_re, 0.0).astype(BF16)
        wc_imn_ref[kb] = jnp.where(diag_out, -c_im, 0.0).astype(BF16)


def _zoh(a_re, a_im, log_dt, b_re, b_im, c_re, c_im):
    g, p = a_re.shape
    h = b_re.shape[-1]
    gp = g * p
    gpb = MXU_DIM // h
    nkb = g // gpb
    assert h & (h - 1) == 0 and p & (p - 1) == 0 and g % gpb == 0
    row = jax.ShapeDtypeStruct((1, gp), F32)
    wb = jax.ShapeDtypeStruct((nkb, gpb * h, gpb * p), BF16)
    wc = jax.ShapeDtypeStruct((nkb, gpb * p, gpb * h), BF16)
    return pl.pallas_call(
        functools.partial(_zoh_kernel, hch=h, p=p),
        out_shape=(row, row, wb, wb, wc, wc),
        name="s5_zoh",
    )(a_re.reshape(1, gp), a_im.reshape(1, gp), jnp.repeat(log_dt, p).reshape(1, gp),
      b_re.transpose(2, 0, 1).reshape(h, gp), b_im.transpose(2, 0, 1).reshape(h, gp),
      c_re.transpose(0, 2, 1).reshape(gp, h), c_im.transpose(0, 2, 1).reshape(gp, h))


def _ssm_in_proj(ub, wb_re_ref, wb_im_ref, bu_re, bu_im):
    nkb, ublk, sblk = wb_re_ref.shape
    for kb in range(nkb):
        ukb = ub[:, kb * ublk:(kb + 1) * ublk]
        bu_re[:, kb * sblk:(kb + 1) * sblk] = _dot(ukb, wb_re_ref[kb])
        bu_im[:, kb * sblk:(kb + 1) * sblk] = _dot(ukb, wb_im_ref[kb])


def _ssm_out_proj(h_re, h_im, wc_re_ref, wc_imn_ref):
    nkb, sblk, _ = wc_re_ref.shape
    ys = []
    for kb in range(nkb):
        hr = h_re[:, kb * sblk:(kb + 1) * sblk].astype(BF16)
        hi = h_im[:, kb * sblk:(kb + 1) * sblk].astype(BF16)
        ys.append(_dot(hr, wc_re_ref[kb]) + _dot(hi, wc_imn_ref[kb]))
    return jnp.concatenate(ys, axis=1)


def _ssm_glu(x, u, y, d_ref, wglu_ref):
    z = jax.nn.gelu(y + d_ref[...] * u).astype(BF16)
    gl = _dot(z, wglu_ref[...])
    d = x.shape[1]
    return x + gl[:, :d] * jax.nn.sigmoid(gl[:, d:])


def _split3(v):
    hi = v.astype(BF16)
    r1 = v - hi.astype(F32)
    mid = r1.astype(BF16)
    lo = (r1 - mid.astype(F32)).astype(BF16)
    return hi, mid, lo


def _ssm_prompt_kernel(x_ref, g_ref, lam_re_ref, lam_im_ref, wb_re_ref, wb_im_ref,
                       wc_re_ref, wc_imn_ref, d_ref, wglu_ref, perm_ref, permt_ref,
                       out_ref, st_re_ref, st_im_ref, bu_re_all, bu_im_all, *, lc, bsz, lane_chunk):
    @pl.when(pl.program_id(0) == 0)
    def _():
        st_re_ref[...] = jnp.zeros_like(st_re_ref)
        st_im_ref[...] = jnp.zeros_like(st_im_ref)

    for sub in range(bu_re_all.shape[0]):
        ts = slice(sub * lc, (sub + 1) * lc)
        _ssm_prompt_chunk(x_ref[:, ts, :], g_ref, lam_re_ref, lam_im_ref, wb_re_ref, wb_im_ref,
                          wc_re_ref, wc_imn_ref, d_ref, wglu_ref, perm_ref, permt_ref,
                          out_ref.at[:, ts, :], st_re_ref, st_im_ref, bu_re_all.at[sub], bu_im_all.at[sub],
                          lc=lc, bsz=bsz, lane_chunk=lane_chunk)


def _ssm_prompt_chunk(x3, g_ref, lam_re_ref, lam_im_ref, wb_re_ref, wb_im_ref,
                      wc_re_ref, wc_imn_ref, d_ref, wglu_ref, perm_ref, permt_ref,
                      out_ref, st_re_ref, st_im_ref, bu_re, bu_im, *, lc, bsz, lane_chunk):
    d = x3.shape[2]
    x = x3.reshape(bsz * lc, d)
    u = _rms(x, g_ref[...])
    ub_tb = _dot(perm_ref[...], u.astype(BF16)).astype(BF16)
    _ssm_in_proj(ub_tb, wb_re_ref, wb_im_ref, bu_re, bu_im)

    gp = bu_re.shape[1]
    for c0 in range(0, gp, lane_chunk):
        sl = slice(c0, c0 + lane_chunk)
        lre = jnp.broadcast_to(lam_re_ref[:, sl], (bsz, lane_chunk))
        lim = jnp.broadcast_to(lam_im_ref[:, sl], (bsz, lane_chunk))
        hr = st_re_ref[:, sl]
        hi = st_im_ref[:, sl]
        for t in range(lc):
            rows = slice(t * bsz, (t + 1) * bsz)
            hr, hi = (lre * hr - lim * hi + bu_re[rows, sl], lre * hi + lim * hr + bu_im[rows, sl])
            bu_re[rows, sl] = hr
            bu_im[rows, sl] = hi
        st_re_ref[:, sl] = hr
        st_im_ref[:, sl] = hi

    permt = permt_ref[...]
    hi, mid, lo = _split3(_ssm_out_proj(bu_re, bu_im, wc_re_ref, wc_imn_ref))
    y = (_dot(permt, hi) + _dot(permt, mid)) + _dot(permt, lo)
    out_ref[...] = _ssm_glu(x, u, y, d_ref, wglu_ref).reshape(bsz, lc, d)


def _ssm_sample_kernel(x_ref, g_ref, lam_re_ref, lam_im_ref, h0_re_ref, h0_im_ref, wb_re_ref, wb_im_ref,
                       wc_re_ref, wc_imn_ref, d_ref, wglu_ref,
                       out_ref, h_re_ref, h_im_ref):
    x = x_ref[...]
    u = _rms(x, g_ref[...])
    _ssm_in_proj(u.astype(BF16), wb_re_ref, wb_im_ref, h_re_ref, h_im_ref)
    lre = lam_re_ref[...]
    lim = lam_im_ref[...]
    h0r = h0_re_ref[...]
    h0i = h0_im_ref[...]
    h_re_ref[...] = lre * h0r - lim * h0i + h_re_ref[...]
    h_im_ref[...] = lre * h0i + lim * h0r + h_im_ref[...]
    y = _ssm_out_proj(h_re_ref, h_im_ref, wc_re_ref, wc_imn_ref)
    n = x.shape[0]
    out_ref[0:n, :] = _ssm_glu(x, u, y, d_ref, wglu_ref)
    out_ref[n:, :] = jnp.zeros((out_ref.shape[0] - n, out_ref.shape[1]), F32)


def _ssm_prompt(x, g, lam_re, lam_im, wb_re, wb_im, wc_re, wc_imn, d_skip, wglu, *, lc, nsub):
    bsz, seq, d = x.shape
    assert seq % (lc * nsub) == 0
    gp = lam_re.shape[1]
    r = lc * bsz
    perm = np.zeros((r, r), np.float32)
    for b in range(bsz):
        for t in range(lc):
            perm[t * bsz + b, b * lc + t] = 1.0
    permt = jnp.asarray(perm.T, BF16)
    perm = jnp.asarray(perm, BF16)
    kern = functools.partial(_ssm_prompt_kernel, lc=lc, bsz=bsz, lane_chunk=8 * LANES)
    return pl.pallas_call(
        kern,
        grid=(seq // (lc * nsub),),
        in_specs=[pl.BlockSpec((bsz, lc * nsub, d), lambda c: (0, c, 0)),
                  _const_spec(g.shape), _const_spec(lam_re.shape), _const_spec(lam_im.shape),
                  _const_spec(wb_re.shape), _const_spec(wb_im.shape),
                  _const_spec(wc_re.shape), _const_spec(wc_imn.shape),
                  _const_spec(d_skip.shape), _const_spec(wglu.shape),
                  _const_spec(perm.shape), _const_spec(permt.shape)],
        out_specs=[pl.BlockSpec((bsz, lc * nsub, d), lambda c: (0, c, 0)),
                   _const_spec((bsz, gp)), _const_spec((bsz, gp))],
        out_shape=[jax.ShapeDtypeStruct((bsz, seq, d), F32),
                   jax.ShapeDtypeStruct((bsz, gp), F32), jax.ShapeDtypeStruct((bsz, gp), F32)],
        scratch_shapes=[pltpu.VMEM((nsub, r, gp), F32), pltpu.VMEM((nsub, r, gp), F32)],
        compiler_params=_params("arbitrary"),
        name="s5_prompt",
    )(x, g, lam_re, lam_im, wb_re, wb_im, wc_re, wc_imn, d_skip, wglu, perm, permt)


def _ssm_sample(x, g, lam_re, lam_im, h0_re, h0_im, wb_re, wb_im, wc_re, wc_imn, d_skip, wglu, *, pad_rows):
    n, d = x.shape
    gp = lam_re.shape[1]
    return pl.pallas_call(
        _ssm_sample_kernel,
        out_shape=[jax.ShapeDtypeStruct((pad_rows, d), F32),
                   jax.ShapeDtypeStruct((n, gp), F32), jax.ShapeDtypeStruct((n, gp), F32)],
        compiler_params=_params(),
        name="s5_sample",
    )(x, g, lam_re, lam_im, h0_re, h0_im, wb_re, wb_im, wc_re, wc_imn, d_skip, wglu)


def _ffn_chunks(d_ff):
    step = 3 * MXU_DIM
    return [(c, min(c + step, d_ff)) for c in range(0, d_ff, step)]


def _ffn_kernel(xp_ref, xs_ref, gffn_ref, wgu_ref, wd_ref, gkv_ref, wkv_ref, gq_ref, wq_ref,
                x2_ref, kv_ref, q_ref):
    x = jnp.where(pl.program_id(0) < pl.num_programs(0) - 1, xp_ref[...], xs_ref[...])
    hb = _rms(x, gffn_ref[...]).astype(BF16)
    d_ff = wd_ref.shape[0]
    acc = None
    for c0, c1 in _ffn_chunks(d_ff):
        a = _dot(hb, wgu_ref[:, c0:c1])
        b = _dot(hb, wgu_ref[:, d_ff + c0:d_ff + c1])
        part = _dot((jax.nn.silu(a) * b).astype(BF16), wd_ref[c0:c1, :])
        acc = part if acc is None else acc + part
    x2 = x + acc
    x2_ref[...] = x2
    kv_ref[...] = _dot(_rms(x2, gkv_ref[...]).astype(BF16), wkv_ref[...])
    q_ref[...] = _dot(_rms(x2, gq_ref[...]).astype(BF16), wq_ref[...]).astype(BF16)


def _ffn(xp, xs, gffn, wgu, wd, gkv, wkv, gq, wq):
    tm, d = xs.shape
    npt = xp.shape[0] // tm
    rows = xp.shape[0] + tm
    kvw = wkv.shape[1]
    nq = wq.shape[1]
    return pl.pallas_call(
        _ffn_kernel,
        grid=(npt + 1,),
        in_specs=[pl.BlockSpec((tm, d), lambda i: (jnp.minimum(i, npt - 1), 0)),
                  _const_spec(xs.shape),
                  _const_spec(gffn.shape), _const_spec(wgu.shape), _const_spec(wd.shape),
                  _const_spec(gkv.shape), _const_spec(wkv.shape),
                  _const_spec(gq.shape), _const_spec(wq.shape)],
        out_specs=[pl.BlockSpec((tm, d), lambda i: (i, 0)),
                   pl.BlockSpec((tm, kvw), lambda i: (i, 0)),
                   pl.BlockSpec((tm, nq), lambda i: (i, 0))],
        out_shape=[jax.ShapeDtypeStruct((rows, d), F32),
                   jax.ShapeDtypeStruct((rows, kvw), F32),
                   jax.ShapeDtypeStruct((rows, nq), BF16)],
        compiler_params=_params("arbitrary"),
        name="ffn_kv_q",
    )(xp, xs, gffn, wgu, wd, gkv, wkv, gq, wq)


def _t5_bucket(dist, num_buckets):
    max_exact = num_buckets // 2
    d = jnp.maximum(dist, 0)
    large = max_exact + (jnp.log(jnp.maximum(d, 1).astype(F32) / max_exact)
                         / math.log(MAX_DISTANCE / max_exact) * (num_buckets - max_exact)).astype(jnp.int32)
    large = jnp.minimum(large, num_buckets - 1)
    return jnp.where(d < max_exact, d, large)


def _bias_kernel(bm_ref, bs_ref, rb_ref, bias_ref, bias_s_ref, *, kvh, rep):
    nb, nh = rb_ref.shape
    bm = bm_ref[...]
    bs = bs_ref[...]

    def lookup(buckets, h):
        terms = [jnp.where(buckets == k, rb_ref[k, h], 0.0) for k in range(nb)]
        while len(terms) > 1:
            terms = [a + b for a, b in zip(terms[0::2], terms[1::2])] + (terms[-1:] if len(terms) % 2 else [])
        return terms[0]

    w = bm.shape[0]
    for g in range(kvh):
        for r in range(rep):
            bias_ref[g, :, r * w:(r + 1) * w] = lookup(bm, g * rep + r) * math.log2(math.e)
            bias_s_ref[g, r:r + 1, :] = lookup(bs, g * rep + r)


def _bias_tables(rel_bias, window, kvh, rep):
    nb, nh = rel_bias.shape
    key = jnp.arange(window)[:, None]
    qry = jnp.arange(window)[None, :]
    bm = _t5_bucket((qry - key) % window, nb).astype(jnp.int32)
    bs = _t5_bucket(window - 1 - qry, nb).astype(jnp.int32)
    kern = functools.partial(_bias_kernel, kvh=kvh, rep=rep)
    return pl.pallas_call(
        kern,
        in_specs=[pl.BlockSpec(memory_space=pltpu.VMEM), pl.BlockSpec(memory_space=pltpu.VMEM),
                  pl.BlockSpec(memory_space=pltpu.SMEM)],
        out_shape=[jax.ShapeDtypeStruct((kvh, window, rep * window), F32),
                   jax.ShapeDtypeStruct((kvh, rep, window), F32)],
        name="t5_bias",
    )(bm, bs, rel_bias)


def _attn_prompt_kernel(q_ref, kvc_ref, kvp_ref, bias_ref, sink_ref, ot_ref, *, kvh, rep, hd, scale):
    w = kvp_ref.shape[0]
    kv_all = kvc_ref[...]
    for sub in range(q_ref.shape[0] // w):
        rows = slice(sub * w, (sub + 1) * w)
        kvp = kvp_ref[...] if sub == 0 else kv_all[(sub - 1) * w:sub * w]
        first = pl.program_id(1) == 0 if sub == 0 else None
        _attn_prompt_block(q_ref[rows, :], kv_all[rows], kvp, first, bias_ref, sink_ref, ot_ref, sub * w,
                           kvh=kvh, rep=rep, hd=hd, scale=scale)


def _attn_prompt_block(q, kvc, kvp, first, bias_ref, sink_ref, ot_ref, col0, *, kvh, rep, hd, scale):
    w = q.shape[0]
    kvw = kvh * hd
    lane = lax.broadcasted_iota(jnp.int32, (1, LANES), 1)
    key = lax.broadcasted_iota(jnp.int32, (w, rep * w), 0)
    qry = lax.broadcasted_iota(jnp.int32, (w, rep * w), 1) % w
    upper = key > qry
    mask_add = None if first is None else jnp.where(jnp.logical_and(upper, first), NEG_INF, 0.0)
    log2e = math.log2(math.e)
    heads_per_blk = LANES // hd
    for p in range(kvw // LANES):
        cs = slice(p * LANES, (p + 1) * LANES)
        kc, kp = kvc[:, cs], kvp[:, cs]
        vs = slice(kvw + p * LANES, kvw + (p + 1) * LANES)
        vb = jnp.concatenate([kvp[:, vs], kvc[:, vs]], axis=0).astype(BF16)
        qg = jnp.concatenate([q[:, r * kvw + p * LANES:r * kvw + (p + 1) * LANES] for r in range(rep)], axis=0)
        kbs = []
        for half in range(heads_per_blk):
            lmask = jnp.logical_and(lane >= half * hd, lane < (half + 1) * hd)
            kbs += [jnp.where(lmask, kp, 0.0), jnp.where(lmask, kc, 0.0)]
        s_all = lax.dot_general(jnp.concatenate(kbs, axis=0).astype(BF16), qg, (((1,), (1,)), ((), ())),
                                preferred_element_type=F32)
        pcats, denoms = [], []
        for half in range(heads_per_blk):
            g = p * heads_per_blk + half
            s = s_all[half * 2 * w:(half + 1) * 2 * w]
            bias = bias_ref[g] if first is None else bias_ref[g] + mask_add
            sc = jnp.where(upper, s[:w], s[w:]) * (scale * log2e) + bias
            sink = jnp.concatenate([jnp.full((1, w), sink_ref[g * rep + r] * log2e, F32) for r in range(rep)],
                                   axis=1)
            m = jnp.maximum(jnp.max(sc, axis=0, keepdims=True), sink)
            pe = jnp.exp2(sc - m)
            denoms.append(jnp.sum(pe, axis=0, keepdims=True) + jnp.exp2(sink - m))
            pcats.append(jnp.concatenate([jnp.where(upper, pe, 0.0), jnp.where(upper, 0.0, pe)], axis=0).astype(BF16))
        og_all = lax.dot_general(vb, jnp.concatenate(pcats, axis=1), (((0,), (0,)), ((), ())),
                                 preferred_element_type=F32)
        halves = [og_all[half * hd:(half + 1) * hd, half * rep * w:(half + 1) * rep * w] / denoms[half]
                  for half in range(heads_per_blk)]
        o_blk = jnp.concatenate(halves, axis=0).astype(BF16)
        for r in range(rep):
            ot_ref[r * kvw + p * LANES:r * kvw + (p + 1) * LANES, col0:col0 + w] = o_blk[:, r * w:(r + 1) * w]


def _attn_prompt(q, kv, bias, sinks, *, bsz, nblk, window, kvh, rep, hd):
    rows, nq = q.shape
    kvw2 = kv.shape[1]
    nsub = next(n for n in (4, 2, 1) if nblk % n == 0 and rows % (n * window) == 0)
    nstep = nblk // nsub
    kern = functools.partial(_attn_prompt_kernel, kvh=kvh, rep=rep, hd=hd, scale=1.0 / math.sqrt(hd))
    return pl.pallas_call(
        kern,
        grid=(bsz, nstep),
        in_specs=[pl.BlockSpec((nsub * window, nq), lambda b, i: (b * nstep + i, 0)),
                  pl.BlockSpec((nsub * window, kvw2), lambda b, i: (b * nstep + i, 0)),
                  pl.BlockSpec((window, kvw2), lambda b, i: (b * nblk + jnp.maximum(nsub * i - 1, 0), 0)),
                  _const_spec(bias.shape),
                  pl.BlockSpec(memory_space=pltpu.SMEM)],
        out_specs=pl.BlockSpec((nq, nsub * window), lambda b, i: (0, b * nstep + i)),
        out_shape=jax.ShapeDtypeStruct((nq, rows), BF16),
        compiler_params=_params("arbitrary", "arbitrary"),
        name="swa_prompt",
    )(q, kv, kv, bias, sinks)


def _attn_sample_kernel(q_ref, kv_ref, ck_ref, cv_ref, bias_ref, sink_ref, o_ref, nk_ref, nv_ref,
                        *, kvh, rep, hd, scale):
    nb, w, kvw = ck_ref.shape
    kv = kv_ref[...]
    lane = lax.broadcasted_iota(jnp.int32, (nb, rep, kvw), 2)

    def shifted(c_ref, n_ref, new):
        flat = c_ref[...].reshape(nb * w, kvw)
        n_ref[...] = pltpu.roll(flat, nb * w - 1, axis=0).reshape(nb, w, kvw)
        n_ref[:, w - 1:w, :] = new
        return n_ref[...].astype(BF16)

    nkb = shifted(ck_ref, nk_ref, kv[:, :, :kvw])
    nvb = shifted(cv_ref, nv_ref, kv[:, :, kvw:])
    q = q_ref[...].astype(F32)
    masks = [jnp.logical_and(lane >= g * hd, lane < (g + 1) * hd) for g in range(kvh)]
    qall = jnp.concatenate([jnp.where(masks[g], q, 0.0) for g in range(kvh)], axis=1).astype(BF16)
    s = jnp.einsum("nmc,njc->nmj", qall, nkb, preferred_element_type=F32)
    sc = s * scale + bias_ref[...][None]
    sink = sink_ref[...][None]
    m = jnp.maximum(jnp.max(sc, axis=-1, keepdims=True), sink)
    pe = jnp.exp(sc - m)
    probs = pe / (jnp.sum(pe, axis=-1, keepdims=True) + jnp.exp(sink - m))
    og = jnp.einsum("nmj,njc->nmc", probs.astype(BF16), nvb, preferred_element_type=F32)
    o = jnp.zeros((nb, rep, kvw), F32)
    for g in range(kvh):
        o = jnp.where(masks[g], og[:, g * rep:(g + 1) * rep, :], o)
    o_ref[...] = o.astype(BF16)


def _attn_sample(q3, kv3, ck, cv, bias_s, sink_s, *, nb, kvh, rep, hd):
    n, w, kvw = ck.shape
    kern = functools.partial(_attn_sample_kernel, kvh=kvh, rep=rep, hd=hd, scale=1.0 / math.sqrt(hd))
    cache_spec = pl.BlockSpec((nb, w, kvw), lambda i: (i, 0, 0))
    return pl.pallas_call(
        kern,
        grid=(n // nb,),
        in_specs=[pl.BlockSpec((nb, rep, kvw), lambda i: (i, 0, 0)),
                  pl.BlockSpec((nb, 1, 2 * kvw), lambda i: (i, 0, 0)),
                  cache_spec, cache_spec,
                  _const_spec(bias_s.shape), _const_spec(sink_s.shape)],
        out_specs=[pl.BlockSpec((nb, rep, kvw), lambda i: (i, 0, 0)), cache_spec, cache_spec],
        out_shape=[jax.ShapeDtypeStruct((n, rep, kvw), BF16),
                   jax.ShapeDtypeStruct((n, w, kvw), F32), jax.ShapeDtypeStruct((n, w, kvw), F32)],
        compiler_params=_params("arbitrary"),
        name="swa_sample",
    )(q3, kv3, ck, cv, bias_s, sink_s)


CHUNK = 16
BIG_COPY = 128
SUB = 256
UP_ROW_BLOCK = 1024
DOWN_ROW_BLOCK = 1024
TOKEN_TILE = 512
MOE_TILE = 512


def _chunk_copy(src, src_row, dst, dst_row, sem, rows=CHUNK):
    return pltpu.make_async_copy(src.at[pl.ds(pl.multiple_of(src_row, CHUNK), rows), :],
                                 dst.at[pl.ds(pl.multiple_of(dst_row, CHUNK), rows), :], sem)


def _start_copies(src, src_row, dst, dst_row, sem, nrows):
    nbig = lax.shift_right_logical(nrows, int(math.log2(BIG_COPY)))

    def big(c, carry):
        _chunk_copy(src, src_row + c * BIG_COPY, dst, dst_row + c * BIG_COPY, sem, BIG_COPY).start()
        return carry

    lax.fori_loop(0, nbig, big, 0)
    done = nbig * BIG_COPY

    def small(c, carry):
        _chunk_copy(src, src_row + done + c * CHUNK, dst, dst_row + done + c * CHUNK, sem).start()
        return carry

    lax.fori_loop(0, lax.shift_right_logical(nrows - done, int(math.log2(CHUNK))), small, 0)


def _moe_route_kernel(x2_ref, ot_ref, wo_ref, gffn_ref, wr_ref, lst_ref, ust_ref,
                      x3_ref, cm_ref, tmeta_ref, xs_hbm,
                      comp_s, carry_s, zero_s, base_v, pend_sm, sem, *, ne, cap, n_valid):
    i = pl.program_id(0)
    nt = pl.num_programs(0)
    tm = x2_ref.shape[0]
    w = comp_s.shape[1]
    slot = lax.rem(i, 2)

    @pl.when(i == 0)
    def _():
        base_v[...] = jnp.zeros_like(base_v)
        zero_s[...] = jnp.zeros_like(zero_s)
        carry_s[...] = jnp.zeros_like(carry_s)

    def drain(n):
        def body(c, carry):
            _chunk_copy(zero_s, 0, xs_hbm, 0, sem).wait()
            return carry
        lax.fori_loop(0, n, body, 0)

    @pl.when(i > 0)
    def _():
        drain(pend_sm[0])

    x3 = x2_ref[...] + lax.dot_general(ot_ref[...], wo_ref[...], (((0,), (0,)), ((), ())),
                                       preferred_element_type=F32)
    x3_ref[...] = x3
    hb = _rms(x3, gffn_ref[...]).astype(BF16)

    lane = lax.broadcasted_iota(jnp.int32, (tm, LANES), 1)
    logits = jnp.where(lane < ne, _dot(hb, wr_ref[...]), -jnp.inf)
    m1 = jnp.max(logits, axis=-1, keepdims=True)
    i1 = jnp.min(jnp.where(logits == m1, lane, LANES), axis=-1, keepdims=True)
    rest = jnp.where(lane == i1, -jnp.inf, logits)
    m2 = jnp.max(rest, axis=-1, keepdims=True)
    i2 = jnp.min(jnp.where(rest == m2, lane, LANES), axis=-1, keepdims=True)
    e2 = jnp.exp(m2 - m1)
    g1 = 1.0 / (1.0 + e2)
    g2 = e2 / (1.0 + e2)

    live = i * tm + lax.broadcasted_iota(jnp.int32, (tm, 1), 0) < n_valid
    sel = jnp.where(jnp.logical_and(live, jnp.logical_or(lane == i1, lane == i2)), 1.0, 0.0)
    rank = _dot(lst_ref[...], sel.astype(BF16))
    cnt = jnp.sum(sel, axis=0, keepdims=True)
    fill = base_v[...]
    rem = fill - jnp.floor(fill / CHUNK) * CHUNK
    cpad = jnp.floor((rem + cnt + (CHUNK - 1)) / CHUNK) * CHUNK
    loff = _dot(jnp.broadcast_to(cpad, (8, LANES)).astype(BF16), ust_ref[...])[0:1]
    dest = loff + rem + rank
    ld1 = jnp.where(live, jnp.sum(jnp.where(lane == i1, dest, 0.0), axis=-1, keepdims=True), -1.0)
    ld2 = jnp.where(live, jnp.sum(jnp.where(lane == i2, dest, 0.0), axis=-1, keepdims=True), -1.0)
    cm = jnp.where(lane == 0, ld1, jnp.where(lane == 1, ld2, jnp.where(lane == 2, g1, jnp.where(lane == 3, g2, 0.0))))
    cm_ref[...] = cm

    rm = cm.T
    rowi = lax.broadcasted_iota(jnp.int32, (w, tm), 0).astype(F32)
    place = (jnp.where(rowi == rm[0:1], 1.0, 0.0) + jnp.where(rowi == rm[1:2], 1.0, 0.0)).astype(BF16)
    comp_s[slot] = _dot(place, hb).astype(BF16)

    base = fill - rem
    srow = lax.broadcasted_iota(jnp.int32, (8, LANES), 0)
    tmeta_ref[...] = jnp.where(srow == 0, cpad, jnp.where(srow == 1, base, jnp.where(srow == 2, fill + cnt, 0.0))
                               ).astype(jnp.int32)
    base_v[...] = fill + cnt

    cpad_i = cpad.astype(jnp.int32)
    loff_i = loff.astype(jnp.int32)
    base_i = base.astype(jnp.int32)
    tail_i = (rem + cnt - jnp.floor((rem + cnt) / CHUNK) * CHUNK).astype(jnp.int32)
    total = 0
    tails = []
    for e in range(ne):
        n_e = cpad_i[0, e]
        src0 = loff_i[0, e]
        dst0 = base_i[0, e] + e * cap

        @pl.when(n_e > 0)
        def _(e=e, n_e=n_e, src0=src0, partial=tail_i[0, e] > 0):
            head = pl.ds(pl.multiple_of(src0, CHUNK), CHUNK)
            comp_s[slot, head, :] = comp_s[slot, head, :] + carry_s[e]
            last = comp_s[slot, pl.ds(pl.multiple_of(src0 + n_e - CHUNK, CHUNK), CHUNK), :]
            carry_s[e] = jnp.where(partial, last, jnp.zeros_like(last))

        _start_copies(comp_s.at[slot], src0, xs_hbm, dst0, sem, n_e)
        total = total + lax.shift_right_logical(n_e, int(math.log2(CHUNK)))
        tails.append(dst0 + n_e)
    pend_sm[0] = total

    @pl.when(i == nt - 1)
    def _():
        nz_total = 0
        for e in range(ne):
            end = tails[e]
            nz = lax.shift_right_logical(lax.rem(SUB - lax.rem(end, SUB), SUB), int(math.log2(CHUNK)))

            def zstart(c, carry, end=end):
                _chunk_copy(zero_s, 0, xs_hbm, end + c * CHUNK, sem).start()
                return carry

            lax.fori_loop(0, nz, zstart, 0)
            nz_total = nz_total + nz
        drain(total + nz_total)


def _moe_route(x2, ot, wo, gffn, wr_pad, *, tm, ne, cap, n_valid):
    rows, d = x2.shape
    nt = rows // tm
    w = _staging_rows(tm, ne)
    lst = jnp.asarray(np.tril(np.ones((tm, tm), np.float32), -1), BF16)
    ust = jnp.asarray(np.triu(np.ones((LANES, LANES), np.float32), 1), BF16)
    kern = functools.partial(_moe_route_kernel, ne=ne, cap=cap, n_valid=n_valid)
    return pl.pallas_call(
        kern,
        grid=(nt,),
        in_specs=[pl.BlockSpec((tm, d), lambda i: (i, 0)),
                  pl.BlockSpec((ot.shape[0], tm), lambda i: (0, i)),
                  _const_spec(wo.shape), _const_spec(gffn.shape), _const_spec(wr_pad.shape),
                  _const_spec(lst.shape), _const_spec(ust.shape)],
        out_specs=[pl.BlockSpec((tm, d), lambda i: (i, 0)),
                   pl.BlockSpec((tm, LANES), lambda i: (i, 0)),
                   pl.BlockSpec((8, LANES), lambda i: (i, 0)),
                   pl.BlockSpec(memory_space=pl.ANY)],
        out_shape=[jax.ShapeDtypeStruct((rows, d), F32),
                   jax.ShapeDtypeStruct((rows, LANES), F32),
                   jax.ShapeDtypeStruct((nt * 8, LANES), jnp.int32),
                   jax.ShapeDtypeStruct((ne * cap, d), BF16)],
        scratch_shapes=[pltpu.VMEM((2, w, d), BF16), pltpu.VMEM((ne, CHUNK, d), BF16), pltpu.VMEM((CHUNK, d), BF16),
                        pltpu.VMEM((1, LANES), F32), pltpu.SMEM((1,), jnp.int32),
                        pltpu.SemaphoreType.DMA(())],
        compiler_params=_params("arbitrary"),
        name="moe_route",
    )(x2, ot, wo, gffn, wr_pad, lst, ust)


def _staging_rows(tm, ne):
    return -(-(TOP_K * tm + 2 * ne * (CHUNK - 1)) // LANES) * LANES


def _max_blocks(rows, ne, row_block):
    return TOP_K * rows // row_block + ne


def _expert_steps(nrows, nf, max_blocks, row_block):
    ne = nrows.shape[0]
    nblk = (nrows + row_block - 1) // row_block
    cum = jnp.cumsum(nblk) * nf
    total = cum[-1]
    s = jnp.minimum(jnp.arange(max_blocks * nf, dtype=jnp.int32), total - 1)
    e = jnp.minimum(jnp.sum((s[:, None] >= cum[None, :]).astype(jnp.int32), axis=1), ne - 1)
    nb_e = nblk[e]
    within = s - (cum[e] - nb_e * nf)
    f = within // nb_e
    pos = within - f * nb_e
    r = (pos + nb_e - 1) % nb_e
    nsub = jnp.clip((nrows[e] - r * row_block + SUB - 1) // SUB, 0, row_block // SUB)
    first = (pos == 0).astype(jnp.int32)
    return (e, f.astype(jnp.int32), r.astype(jnp.int32), first, nsub.astype(jnp.int32),
            jnp.reshape(total, (1,)).astype(jnp.int32))


def _moe_up_kernel(e_ref, f_ref, r_ref, first_ref, nsub_ref, n_ref, xs_ref, wg_ref, wu_ref, act_ref, wgb_s, wub_s):
    s = pl.program_id(0)

    @pl.when(s < n_ref[0])
    def _():
        @pl.when(first_ref[s] == 1)
        def _():
            wgb_s[...] = wg_ref[...].astype(BF16)
            wub_s[...] = wu_ref[...].astype(BF16)

        def gate_up(row0, nrows):
            rows = pl.ds(pl.multiple_of(row0, SUB), nrows)
            xsb = xs_ref[rows, :]
            a = _dot(xsb, wgb_s[...])
            b = _dot(xsb, wub_s[...])
            act_ref[rows, :] = (jax.nn.silu(a) * b).astype(BF16)

        nsub = nsub_ref[s]
        npair = lax.shift_right_logical(nsub, 1)

        def body(j, carry):
            gate_up(j * (2 * SUB), 2 * SUB)
            return carry

        lax.fori_loop(0, npair, body, 0)

        @pl.when(nsub > 2 * npair)
        def _():
            gate_up(npair * (2 * SUB), SUB)


def _moe_up(steps, xs, wgu, *, fc, cap, row_block):
    ne, d, dff2 = wgu.shape
    dff = dff2 // 2
    nf = dff // fc
    cb = cap // row_block
    grid_spec = pltpu.PrefetchScalarGridSpec(
        num_scalar_prefetch=6,
        grid=(steps[0].shape[0],),
        in_specs=[pl.BlockSpec((row_block, d), lambda s, e, f, r, *_: (e[s] * cb + r[s], 0)),
                  pl.BlockSpec((None, d, fc), lambda s, e, f, r, *_: (e[s], 0, f[s])),
                  pl.BlockSpec((None, d, fc), lambda s, e, f, r, *_: (e[s], 0, nf + f[s]))],
        out_specs=pl.BlockSpec((row_block, fc), lambda s, e, f, r, *_: (e[s] * cb + r[s], f[s])),
        scratch_shapes=[pltpu.VMEM((d, fc), BF16), pltpu.VMEM((d, fc), BF16)])
    return pl.pallas_call(
        _moe_up_kernel,
        grid_spec=grid_spec,
        out_shape=jax.ShapeDtypeStruct((ne * cap, dff), BF16),
        compiler_params=_params("arbitrary"),
        name="moe_up",
    )(*steps, xs, wgu, wgu)


def _moe_down_kernel(e_ref, f_ref, r_ref, first_ref, nsub_ref, n_ref, act_ref, wd_ref, ys_ref, wdb_s):
    s = pl.program_id(0)

    @pl.when(s < n_ref[0])
    def _():
        @pl.when(first_ref[s] == 1)
        def _():
            wdb_s[...] = wd_ref[...].astype(BF16)

        def project(row0, nrows):
            rows = pl.ds(pl.multiple_of(row0, SUB), nrows)
            ys_ref[rows, :] = _dot(act_ref[rows, :], wdb_s[...]).astype(BF16)

        nsub = nsub_ref[s]
        npair = lax.shift_right_logical(nsub, 1)

        def body(j, carry):
            project(j * (2 * SUB), 2 * SUB)
            return carry

        lax.fori_loop(0, npair, body, 0)

        @pl.when(nsub > 2 * npair)
        def _():
            project(npair * (2 * SUB), SUB)


def _moe_down(steps, act, wd, *, cap, row_block):
    ne, dff, d = wd.shape
    cb = cap // row_block
    grid_spec = pltpu.PrefetchScalarGridSpec(
        num_scalar_prefetch=6,
        grid=(steps[0].shape[0],),
        in_specs=[pl.BlockSpec((row_block, dff), lambda s, e, f, r, *_: (e[s] * cb + r[s], 0)),
                  pl.BlockSpec((None, dff, d), lambda s, e, f, r, *_: (e[s], 0, 0))],
        out_specs=pl.BlockSpec((row_block, d), lambda s, e, f, r, *_: (e[s] * cb + r[s], 0)),
        scratch_shapes=[pltpu.VMEM((dff, d), BF16)])
    return pl.pallas_call(
        _moe_down_kernel,
        grid_spec=grid_spec,
        out_shape=jax.ShapeDtypeStruct((ne * cap, d), BF16),
        compiler_params=_params("arbitrary", vmem=2 * (dff * d * 4 + row_block * (dff + d) * 2) + dff * d * 2
                                + 4 * SUB * d * 4),
        name="moe_down",
    )(*steps, act, wd)


def _moe_combine_kernel(cpad_ref, seg_ref, x3_ref, cm_ref, gfin_ref, ys_hbm, yp_ref, ys_ref, yloc_s, sem,
                        *, ne, npt):
    i = pl.program_id(0)
    nt = pl.num_programs(0)
    tm = x3_ref.shape[0]
    w = yloc_s.shape[1]
    slot = lax.rem(i, 2)
    shift = int(math.log2(CHUNK))

    def issue(t, sl):
        off = 0
        for e in range(ne):
            n_e = cpad_ref[t * ne + e]
            _start_copies(ys_hbm, seg_ref[t * ne + e], yloc_s.at[sl], off, sem.at[sl], n_e)
            off = off + n_e

    @pl.when(i == 0)
    def _():
        yloc_s[...] = jnp.zeros_like(yloc_s)
        issue(0, 0)

    @pl.when(i + 1 < nt)
    def _():
        issue(i + 1, 1 - slot)

    total = 0
    for e in range(ne):
        total = total + lax.shift_right_logical(cpad_ref[i * ne + e], shift)

    def wait(c, carry):
        _chunk_copy(ys_hbm, 0, yloc_s.at[slot], 0, sem.at[slot]).wait()
        return carry

    lax.fori_loop(0, total, wait, 0)

    cm = cm_ref[...]
    col = lax.broadcasted_iota(jnp.int32, (tm, w), 1).astype(F32)
    yl = yloc_s[slot]
    y1 = _dot(jnp.where(col == cm[:, 0:1], 1.0, 0.0).astype(BF16), yl)
    y2 = _dot(jnp.where(col == cm[:, 1:2], 1.0, 0.0).astype(BF16), yl)
    y = _rms(x3_ref[...] + cm[:, 2:3] * y1 + cm[:, 3:4] * y2, gfin_ref[...])

    @pl.when(i < npt)
    def _():
        yp_ref[...] = y

    @pl.when(i >= npt)
    def _():
        ys_ref[...] = y


def _moe_combine(cpad, seg, x3, cm, gfin, ys, *, tm, ne, sample_rows):
    rows, d = x3.shape
    nt = rows // tm
    npt = (rows - sample_rows) // tm
    w = _staging_rows(tm, ne)
    kern = functools.partial(_moe_combine_kernel, ne=ne, npt=npt)
    grid_spec = pltpu.PrefetchScalarGridSpec(
        num_scalar_prefetch=2,
        grid=(nt,),
        in_specs=[pl.BlockSpec((tm, d), lambda i, *_: (i, 0)),
                  pl.BlockSpec((tm, LANES), lambda i, *_: (i, 0)),
                  pl.BlockSpec(gfin.shape, lambda i, *_: (0, 0)),
                  pl.BlockSpec(memory_space=pl.ANY)],
        out_specs=[pl.BlockSpec((tm, d), lambda i, *_: (jnp.minimum(i, npt - 1), 0)),
                   pl.BlockSpec((tm, d), lambda i, *_: (jnp.maximum(i - npt, 0), 0))],
        scratch_shapes=[pltpu.VMEM((2, w, d), BF16), pltpu.SemaphoreType.DMA((2,))])
    return pl.pallas_call(
        kern,
        grid_spec=grid_spec,
        out_shape=[jax.ShapeDtypeStruct((npt * tm, d), F32), jax.ShapeDtypeStruct((sample_rows, d), F32)],
        compiler_params=_params("arbitrary"),
        name="moe_combine",
    )(cpad, seg, x3, cm, gfin, ys)


def _moe(x2, ot, wo, gffn, wr, wgu, wd, gfin, *, tm, fc, n_valid, sample_rows):
    rows, d = x2.shape
    ne = wr.shape[1]
    nt = rows // tm
    nf = wgu.shape[2] // 2 // fc
    cap = -(-(rows + SUB) // UP_ROW_BLOCK) * UP_ROW_BLOCK
    wr_pad = jnp.pad(wr, ((0, 0), (0, LANES - ne))).astype(BF16)
    x3, cm, tmeta, xs = _moe_route(x2, ot, wo, gffn, wr_pad, tm=tm, ne=ne, cap=cap, n_valid=n_valid)
    tmeta = tmeta.reshape(nt, 8, LANES)[:, :, :ne]
    cpad = tmeta[:, 0, :].reshape(nt * ne)
    seg = (tmeta[:, 1, :] + jnp.arange(ne, dtype=jnp.int32)[None, :] * cap).reshape(nt * ne)
    nrows = tmeta[nt - 1, 2, :]
    up_steps = _expert_steps(nrows, nf, _max_blocks(rows, ne, UP_ROW_BLOCK), UP_ROW_BLOCK)
    act = _moe_up(up_steps, xs, wgu, fc=fc, cap=cap, row_block=UP_ROW_BLOCK)
    down_steps = _expert_steps(nrows, 1, _max_blocks(rows, ne, DOWN_ROW_BLOCK), DOWN_ROW_BLOCK)
    ys = _moe_down(down_steps, act, wd, cap=cap, row_block=DOWN_ROW_BLOCK)
    return _moe_combine(cpad, seg, x3, cm, gfin, ys, tm=tm, ne=ne, sample_rows=sample_rows)


def kernel(x_prompt, x_sample, state_ssm_re, state_ssm_im, cache_k_win, cache_v_win, g_mix, g_ffn, g_kv, g_final, ssm_a_re, ssm_a_im, ssm_log_dt, ssm_b_re, ssm_b_im, ssm_c_re, ssm_c_im, ssm_d, w_glu, w_kv, w_q, w_o, attn_sinks, rel_bias, w_ffn_gate_up, w_ffn_down, w_router, w_exp_gate_up, w_exp_down):
    bsz, seq, d = x_prompt.shape
    ns, dec_seq, _ = x_sample.shape
    assert dec_seq == 1 and g_mix.shape[0] == 2 and ssm_a_re.shape[0] == 1 and w_q.shape[0] == 1
    _, g, p = ssm_a_re.shape
    gp = g * p
    window, kvh, hd = cache_k_win.shape[1:]
    kvw = kvh * hd
    nh = attn_sinks.shape[1]
    rep = nh // kvh
    nq = nh * hd
    assert bsz == 8 and ns % bsz == 0 and seq % window == 0 and LANES % hd == 0

    lam_re, lam_im, wb_re, wb_im, wc_re, wc_imn = _zoh(ssm_a_re[0], ssm_a_im[0], ssm_log_dt[0],
                                                        ssm_b_re[0], ssm_b_im[0], ssm_c_re[0], ssm_c_im[0])
    wglu = w_glu[0].astype(BF16)
    d_skip = ssm_d[0].reshape(1, d)
    wgu = w_ffn_gate_up[0].astype(BF16)
    wd = w_ffn_down[0].astype(BF16)
    wkv = w_kv.astype(BF16)
    wq = w_q[0].reshape(d, kvh, rep, hd).transpose(0, 2, 1, 3).reshape(d, nq).astype(BF16)
    wo = w_o[0].reshape(kvh, rep, hd, d).transpose(1, 0, 2, 3).reshape(nq, d).astype(BF16)
    bias, bias_s = _bias_tables(rel_bias, window, kvh, rep)
    bias_s = bias_s.reshape(nh, window)
    sinks = attn_sinks[0]
    sink_s = sinks.reshape(nh, 1)

    tm = TOKEN_TILE
    npr = seq * bsz
    assert npr % tm == 0 and ns <= tm

    x1_p, st_re, st_im = _ssm_prompt(x_prompt, g_mix[0:1], lam_re, lam_im, wb_re, wb_im, wc_re, wc_imn,
                                     d_skip, wglu, lc=32, nsub=2)
    x1_s, hs_re, hs_im = _ssm_sample(x_sample.reshape(ns, d), g_mix[0:1], lam_re, lam_im,
                                     state_ssm_re[0].reshape(ns, gp), state_ssm_im[0].reshape(ns, gp),
                                     wb_re, wb_im, wc_re, wc_imn, d_skip, wglu, pad_rows=tm)

    x2, kv, q = _ffn(x1_p.reshape(npr, d), x1_s, g_ffn[0:1], wgu, wd, g_kv.reshape(1, d), wkv, g_mix[1:2], wq)

    ot = _attn_prompt(q, kv, bias, sinks, bsz=bsz, nblk=seq // window, window=window, kvh=kvh, rep=rep, hd=hd)
    o_s, nk_s, nv_s = _attn_sample(q[npr:npr + ns].reshape(ns, rep, kvw), kv[npr:npr + ns].reshape(ns, 1, 2 * kvw),
                                   cache_k_win.reshape(ns, window, kvw), cache_v_win.reshape(ns, window, kvw),
                                   bias_s, sink_s, nb=32, kvh=kvh, rep=rep, hd=hd)
    ot = lax.dynamic_update_slice(ot, jnp.pad(o_s.reshape(ns, nq).T, ((0, 0), (0, tm - ns))), (0, npr))

    y_p, y_s = _moe(x2, ot, wo, g_ffn[1:2], w_router[0], w_exp_gate_up[0], w_exp_down[0], g_final.reshape(1, d),
                    tm=MOE_TILE, fc=w_exp_down.shape[2] // 2, n_valid=npr + ns, sample_rows=tm)

    y_prompt = y_p.reshape(bsz, seq, d)
    y_sample = y_s[:ns].reshape(ns, 1, d)
    kv_tail = jnp.stack([kv[(b + 1) * seq - window:(b + 1) * seq] for b in range(bsz)])
    kv_tail = kv_tail.reshape(bsz, window, 2, kvh, hd).transpose(2, 0, 1, 3, 4)
    return (y_prompt, y_sample,
            st_re.reshape(1, bsz, g, p), st_im.reshape(1, bsz, g, p), kv_tail[0], kv_tail[1],
            hs_re.reshape(1, ns, g, p), hs_im.reshape(1, ns, g, p),
            nk_s.reshape(ns, window, kvh, hd), nv_s.reshape(ns, window, kvh, hd))
```

```python
import functools
import math

import numpy as np
import jax
import jax.numpy as jnp
from jax import lax
from jax.experimental import pallas as pl
from jax.experimental.pallas import tpu as pltpu

F32 = jnp.float32
BF16 = jnp.bfloat16

EPS = 1e-6
NEG_INF = -1e30
TOP_K = 2
MAX_DISTANCE = 128
MXU_DIM = 256
LANES = 128
VMEM_LIMIT_BYTES = 56 * 1024 * 1024


def _dot(a, b):
    return jnp.dot(a, b, preferred_element_type=F32)


def _rms(x, g):
    return x * lax.rsqrt(jnp.mean(x * x, axis=-1, keepdims=True) + EPS) * g


def _const_spec(shape):
    nd = len(shape)
    return pl.BlockSpec(shape, lambda *_: (0,) * nd)


def _params(*sem, vmem=VMEM_LIMIT_BYTES):
    return pltpu.CompilerParams(dimension_semantics=sem, vmem_limit_bytes=vmem)


def _zoh_kernel(a_re_ref, a_im_ref, log_dt_ref, b_re_ref, b_im_ref, ct_re_ref, ct_im_ref,
                lam_re_ref, lam_im_ref, wb_re_ref, wb_im_ref, wc_re_ref, wc_imn_ref, *, hch, p):
    a_re = a_re_ref[...]
    a_im = a_im_ref[...]
    dt = jnp.exp(log_dt_ref[...])
    mag = jnp.exp(a_re * dt)
    lr = mag * jnp.cos(a_im * dt)
    li = mag * jnp.sin(a_im * dt)
    lam_re_ref[...] = lr
    lam_im_ref[...] = li
    nr = lr - 1.0
    den = a_re * a_re + a_im * a_im
    qr = (nr * a_re + li * a_im) / den
    qi = (li * a_re - nr * a_im) / den
    b_re = b_re_ref[...]
    b_im = b_im_ref[...]
    bb_re = qr * b_re - qi * b_im
    bb_im = qr * b_im + qi * b_re

    nkb, ublk, sblk = wb_re_ref.shape
    gpb = ublk // hch
    sh_h, sh_p = int(math.log2(hch)), int(math.log2(p))
    row = lax.broadcasted_iota(jnp.int32, (ublk, sblk), 0)
    col = lax.broadcasted_iota(jnp.int32, (ublk, sblk), 1)
    diag_in = lax.shift_right_logical(row, sh_h) == lax.shift_right_logical(col, sh_p)
    row = lax.broadcasted_iota(jnp.int32, (sblk, ublk), 0)
    col = lax.broadcasted_iota(jnp.int32, (sblk, ublk), 1)
    diag_out = lax.shift_right_logical(row, sh_p) == lax.shift_right_logical(col, sh_h)
    spread = jnp.where(lax.broadcasted_iota(jnp.int32, (hch, ublk), 0)
                       == jnp.bitwise_and(lax.broadcasted_iota(jnp.int32, (hch, ublk), 1), hch - 1),
                       1.0, 0.0).astype(BF16)
    for kb in range(nkb):
        cs = slice(kb * sblk, (kb + 1) * sblk)
        wb_re_ref[kb] = jnp.where(diag_in, jnp.concatenate([bb_re[:, cs]] * gpb, axis=0), 0.0).astype(BF16)
        wb_im_ref[kb] = jnp.where(diag_in, jnp.concatenate([bb_im[:, cs]] * gpb, axis=0), 0.0).astype(BF16)
        c_re = _dot(ct_re_ref[cs, :].astype(BF16), spread)
        c_im = _dot(ct_im_ref[cs, :].astype(BF16), spread)
        wc_re_ref[kb] = jnp.where(diag_out, c_re, 0.0).astype(BF16)
        wc_imn_ref[kb] = jnp.where(diag_out, -c_im, 0.0).astype(BF16)


def _zoh(a_re, a_im, log_dt, b_re, b_im, c_re, c_im):
    g, p = a_re.shape
    h = b_re.shape[-1]
    gp = g * p
    gpb = MXU_DIM // h
    nkb = g // gpb
    assert h & (h - 1) == 0 and p & (p - 1) == 0 and g % gpb == 0
    row = jax.ShapeDtypeStruct((1, gp), F32)
    wb = jax.ShapeDtypeStruct((nkb, gpb * h, gpb * p), BF16)
    wc = jax.ShapeDtypeStruct((nkb, gpb * p, gpb * h), BF16)
    return pl.pallas_call(
        functools.partial(_zoh_kernel, hch=h, p=p),
        out_shape=(row, row, wb, wb, wc, wc),
        name="s5_zoh",
    )(a_re.reshape(1, gp), a_im.reshape(1, gp), jnp.repeat(log_dt, p).reshape(1, gp),
      b_re.transpose(2, 0, 1).reshape(h, gp), b_im.transpose(2, 0, 1).reshape(h, gp),
      c_re.transpose(0, 2, 1).reshape(gp, h), c_im.transpose(0, 2, 1).reshape(gp, h))


def _ssm_in_proj(ub, wb_re_ref, wb_im_ref, bu_re, bu_im):
    nkb, ublk, sblk = wb_re_ref.shape
    for kb in range(nkb):
        ukb = ub[:, kb * ublk:(kb + 1) * ublk]
        bu_re[:, kb * sblk:(kb + 1) * sblk] = _dot(ukb, wb_re_ref[kb])
        bu_im[:, kb * sblk:(kb + 1) * sblk] = _dot(ukb, wb_im_ref[kb])


def _ssm_out_proj(h_re, h_im, wc_re_ref, wc_imn_ref):
    nkb, sblk, _ = wc_re_ref.shape
    ys = []
    for kb in range(nkb):
        hr = h_re[:, kb * sblk:(kb + 1) * sblk].astype(BF16)
        hi = h_im[:, kb * sblk:(kb + 1) * sblk].astype(BF16)
        ys.append(_dot(hr, wc_re_ref[kb]) + _dot(hi, wc_imn_ref[kb]))
    return jnp.concatenate(ys, axis=1)


def _ssm_glu(x, u, y, d_ref, wglu_ref):
    z = jax.nn.gelu(y + d_ref[...] * u).astype(BF16)
    gl = _dot(z, wglu_ref[...])
    d = x.shape[1]
    return x + gl[:, :d] * jax.nn.sigmoid(gl[:, d:])


def _split3(v):
    hi = v.astype(BF16)
    r1 = v - hi.astype(F32)
    mid = r1.astype(BF16)
    lo = (r1 - mid.astype(F32)).astype(BF16)
    return hi, mid, lo


def _ssm_prompt_kernel(x_ref, g_ref, lam_re_ref, lam_im_ref, wb_re_ref, wb_im_ref,
                       wc_re_ref, wc_imn_ref, d_ref, wglu_ref, perm_ref, permt_ref,
                       out_ref, st_re_ref, st_im_ref, bu_re_all, bu_im_all, *, lc, bsz, lane_chunk):
    @pl.when(pl.program_id(0) == 0)
    def _():
        st_re_ref[...] = jnp.zeros_like(st_re_ref)
        st_im_ref[...] = jnp.zeros_like(st_im_ref)

    for sub in range(bu_re_all.shape[0]):
        ts = slice(sub * lc, (sub + 1) * lc)
        _ssm_prompt_chunk(x_ref[:, ts, :], g_ref, lam_re_ref, lam_im_ref, wb_re_ref, wb_im_ref,
                          wc_re_ref, wc_imn_ref, d_ref, wglu_ref, perm_ref, permt_ref,
                          out_ref.at[:, ts, :], st_re_ref, st_im_ref, bu_re_all.at[sub], bu_im_all.at[sub],
                          lc=lc, bsz=bsz, lane_chunk=lane_chunk)


def _ssm_prompt_chunk(x3, g_ref, lam_re_ref, lam_im_ref, wb_re_ref, wb_im_ref,
                      wc_re_ref, wc_imn_ref, d_ref, wglu_ref, perm_ref, permt_ref,
                      out_ref, st_re_ref, st_im_ref, bu_re, bu_im, *, lc, bsz, lane_chunk):
    d = x3.shape[2]
    x = x3.reshape(bsz * lc, d)
    u = _rms(x, g_ref[...])
    ub_tb = _dot(perm_ref[...], u.astype(BF16)).astype(BF16)
    _ssm_in_proj(ub_tb, wb_re_ref, wb_im_ref, bu_re, bu_im)

    gp = bu_re.shape[1]
    for c0 in range(0, gp, lane_chunk):
        sl = slice(c0, c0 + lane_chunk)
        lre = jnp.broadcast_to(lam_re_ref[:, sl], (bsz, lane_chunk))
        lim = jnp.broadcast_to(lam_im_ref[:, sl], (bsz, lane_chunk))
        hr = st_re_ref[:, sl]
        hi = st_im_ref[:, sl]
        for t in range(lc):
            rows = slice(t * bsz, (t + 1) * bsz)
            hr, hi = (lre * hr - lim * hi + bu_re[rows, sl], lre * hi + lim * hr + bu_im[rows, sl])
            bu_re[rows, sl] = hr
            bu_im[rows, sl] = hi
        st_re_ref[:, sl] = hr
        st_im_ref[:, sl] = hi

    permt = permt_ref[...]
    hi, mid, lo = _split3(_ssm_out_proj(bu_re, bu_im, wc_re_ref, wc_imn_ref))
    y = (_dot(permt, hi) + _dot(permt, mid)) + _dot(permt, lo)
    out_ref[...] = _ssm_glu(x, u, y, d_ref, wglu_ref).reshape(bsz, lc, d)


def _ssm_sample_kernel(x_ref, g_ref, lam_re_ref, lam_im_ref, h0_re_ref, h0_im_ref, wb_re_ref, wb_im_ref,
                       wc_re_ref, wc_imn_ref, d_ref, wglu_ref,
                       out_ref, h_re_ref, h_im_ref):
    x = x_ref[...]
    u = _rms(x, g_ref[...])
    _ssm_in_proj(u.astype(BF16), wb_re_ref, wb_im_ref, h_re_ref, h_im_ref)
    lre = lam_re_ref[...]
    lim = lam_im_ref[...]
    h0r = h0_re_ref[...]
    h0i = h0_im_ref[...]
    h_re_ref[...] = lre * h0r - lim * h0i + h_re_ref[...]
    h_im_ref[...] = lre * h0i + lim * h0r + h_im_ref[...]
    y = _ssm_out_proj(h_re_ref, h_im_ref, wc_re_ref, wc_imn_ref)
    n = x.shape[0]
    out_ref[0:n, :] = _ssm_glu(x, u, y, d_ref, wglu_ref)
    out_ref[n:, :] = jnp.zeros((out_ref.shape[0] - n, out_ref.shape[1]), F32)


def _ssm_prompt(x, g, lam_re, lam_im, wb_re, wb_im, wc_re, wc_imn, d_skip, wglu, *, lc, nsub):
    bsz, seq, d = x.shape
    assert seq % (lc * nsub) == 0
    gp = lam_re.shape[1]
    r = lc * bsz
    perm = np.zeros((r, r), np.float32)
    for b in range(bsz):
        for t in range(lc):
            perm[t * bsz + b, b * lc + t] = 1.0
    permt = jnp.asarray(perm.T, BF16)
    perm = jnp.asarray(perm, BF16)
    kern = functools.partial(_ssm_prompt_kernel, lc=lc, bsz=bsz, lane_chunk=8 * LANES)
    return pl.pallas_call(
        kern,
        grid=(seq // (lc * nsub),),
        in_specs=[pl.BlockSpec((bsz, lc * nsub, d), lambda c: (0, c, 0)),
                  _const_spec(g.shape), _const_spec(lam_re.shape), _const_spec(lam_im.shape),
                  _const_spec(wb_re.shape), _const_spec(wb_im.shape),
                  _const_spec(wc_re.shape), _const_spec(wc_imn.shape),
                  _const_spec(d_skip.shape), _const_spec(wglu.shape),
                  _const_spec(perm.shape), _const_spec(permt.shape)],
        out_specs=[pl.BlockSpec((bsz, lc * nsub, d), lambda c: (0, c, 0)),
                   _const_spec((bsz, gp)), _const_spec((bsz, gp))],
        out_shape=[jax.ShapeDtypeStruct((bsz, seq, d), F32),
                   jax.ShapeDtypeStruct((bsz, gp), F32), jax.ShapeDtypeStruct((bsz, gp), F32)],
        scratch_shapes=[pltpu.VMEM((nsub, r, gp), F32), pltpu.VMEM((nsub, r, gp), F32)],
        compiler_params=_params("arbitrary"),
        name="s5_prompt",
    )(x, g, lam_re, lam_im, wb_re, wb_im, wc_re, wc_imn, d_skip, wglu, perm, permt)


def _ssm_sample(x, g, lam_re, lam_im, h0_re, h0_im, wb_re, wb_im, wc_re, wc_imn, d_skip, wglu, *, pad_rows):
    n, d = x.shape
    gp = lam_re.shape[1]
    return pl.pallas_call(
        _ssm_sample_kernel,
        out_shape=[jax.ShapeDtypeStruct((pad_rows, d), F32),
                   jax.ShapeDtypeStruct((n, gp), F32), jax.ShapeDtypeStruct((n, gp), F32)],
        compiler_params=_params(),
        name="s5_sample",
    )(x, g, lam_re, lam_im, h0_re, h0_im, wb_re, wb_im, wc_re, wc_imn, d_skip, wglu)


def _ffn_chunks(d_ff):
    step = 3 * MXU_DIM
    return [(c, min(c + step, d_ff)) for c in range(0, d_ff, step)]


def _ffn_kernel(xp_ref, xs_ref, gffn_ref, wgu_ref, wd_ref, gkv_ref, wkv_ref, gq_ref, wq_ref,
                x2_ref, kv_ref, q_ref):
    x = jnp.where(pl.program_id(0) < pl.num_programs(0) - 1, xp_ref[...], xs_ref[...])
    hb = _rms(x, gffn_ref[...]).astype(BF16)
    d_ff = wd_ref.shape[0]
    acc = None
    for c0, c1 in _ffn_chunks(d_ff):
        a = _dot(hb, wgu_ref[:, c0:c1])
        b = _dot(hb, wgu_ref[:, d_ff + c0:d_ff + c1])
        part = _dot((jax.nn.silu(a) * b).astype(BF16), wd_ref[c0:c1, :])
        acc = part if acc is None else acc + part
    x2 = x + acc
    x2_ref[...] = x2
    kv_ref[...] = _dot(_rms(x2, gkv_ref[...]).astype(BF16), wkv_ref[...])
    q_ref[...] = _dot(_rms(x2, gq_ref[...]).astype(BF16), wq_ref[...]).astype(BF16)


def _ffn(xp, xs, gffn, wgu, wd, gkv, wkv, gq, wq):
    tm, d = xs.shape
    npt = xp.shape[0] // tm
    rows = xp.shape[0] + tm
    kvw = wkv.shape[1]
    nq = wq.shape[1]
    return pl.pallas_call(
        _ffn_kernel,
        grid=(npt + 1,),
        in_specs=[pl.BlockSpec((tm, d), lambda i: (jnp.minimum(i, npt - 1), 0)),
                  _const_spec(xs.shape),
                  _const_spec(gffn.shape), _const_spec(wgu.shape), _const_spec(wd.shape),
                  _const_spec(gkv.shape), _const_spec(wkv.shape),
                  _const_spec(gq.shape), _const_spec(wq.shape)],
        out_specs=[pl.BlockSpec((tm, d), lambda i: (i, 0)),
                   pl.BlockSpec((tm, kvw), lambda i: (i, 0)),
                   pl.BlockSpec((tm, nq), lambda i: (i, 0))],
        out_shape=[jax.ShapeDtypeStruct((rows, d), F32),
                   jax.ShapeDtypeStruct((rows, kvw), F32),
                   jax.ShapeDtypeStruct((rows, nq), BF16)],
        compiler_params=_params("arbitrary"),
        name="ffn_kv_q",
    )(xp, xs, gffn, wgu, wd, gkv, wkv, gq, wq)


def _t5_bucket(dist, num_buckets):
    max_exact = num_buckets // 2
    d = jnp.maximum(dist, 0)
    large = max_exact + (jnp.log(jnp.maximum(d, 1).astype(F32) / max_exact)
                         / math.log(MAX_DISTANCE / max_exact) * (num_buckets - max_exact)).astype(jnp.int32)
    large = jnp.minimum(large, num_buckets - 1)
    return jnp.where(d < max_exact, d, large)


def _bias_kernel(bm_ref, bs_ref, rb_ref, bias_ref, bias_s_ref, *, kvh, rep):
    nb, nh = rb_ref.shape
    bm = bm_ref[...]
    bs = bs_ref[...]

    def lookup(buckets, h):
        terms = [jnp.where(buckets == k, rb_ref[k, h], 0.0) for k in range(nb)]
        while len(terms) > 1:
            terms = [a + b for a, b in zip(terms[0::2], terms[1::2])] + (terms[-1:] if len(terms) % 2 else [])
        return terms[0]

    w = bm.shape[0]
    for g in range(kvh):
        for r in range(rep):
            bias_ref[g, :, r * w:(r + 1) * w] = lookup(bm, g * rep + r) * math.log2(math.e)
            bias_s_ref[g, r:r + 1, :] = lookup(bs, g * rep + r)


def _bias_tables(rel_bias, window, kvh, rep):
    nb, nh = rel_bias.shape
    key = jnp.arange(window)[:, None]
    qry = jnp.arange(window)[None, :]
    bm = _t5_bucket((qry - key) % window, nb).astype(jnp.int32)
    bs = _t5_bucket(window - 1 - qry, nb).astype(jnp.int32)
    kern = functools.partial(_bias_kernel, kvh=kvh, rep=rep)
    return pl.pallas_call(
        kern,
        in_specs=[pl.BlockSpec(memory_space=pltpu.VMEM), pl.BlockSpec(memory_space=pltpu.VMEM),
                  pl.BlockSpec(memory_space=pltpu.SMEM)],
        out_shape=[jax.ShapeDtypeStruct((kvh, window, rep * window), F32),
                   jax.ShapeDtypeStruct((kvh, rep, window), F32)],
        name="t5_bias",
    )(bm, bs, rel_bias)


def _attn_prompt_kernel(q_ref, kvc_ref, kvp_ref, bias_ref, sink_ref, ot_ref, *, kvh, rep, hd, scale):
    w = kvp_ref.shape[0]
    kv_all = kvc_ref[...]
    for sub in range(q_ref.shape[0] // w):
        rows = slice(sub * w, (sub + 1) * w)
        kvp = kvp_ref[...] if sub == 0 else kv_all[(sub - 1) * w:sub * w]
        first = pl.program_id(1) == 0 if sub == 0 else None
        _attn_prompt_block(q_ref[rows, :], kv_all[rows], kvp, first, bias_ref, sink_ref, ot_ref, sub * w,
                           kvh=kvh, rep=rep, hd=hd, scale=scale)


def _attn_prompt_block(q, kvc, kvp, first, bias_ref, sink_ref, ot_ref, col0, *, kvh, rep, hd, scale):
    w = q.shape[0]
    kvw = kvh * hd
    lane = lax.broadcasted_iota(jnp.int32, (1, kvw), 1)
    key = lax.broadcasted_iota(jnp.int32, (w, rep * w), 0)
    qry = lax.broadcasted_iota(jnp.int32, (w, rep * w), 1) % w
    upper = key > qry
    mask_add = None if first is None else jnp.where(jnp.logical_and(upper, first), NEG_INF, 0.0)
    log2e = math.log2(math.e)
    heads_per_blk = LANES // hd
    lane_blocks = kvw // LANES
    kbs = []
    for half in range(heads_per_blk):
        lmask = (lane % LANES) // hd == half
        kbs += [jnp.where(lmask, kvp[:, :kvw], 0.0), jnp.where(lmask, kvc[:, :kvw], 0.0)]
    qrows = []
    for p in range(lane_blocks):
        qg = jnp.concatenate([q[:, r * kvw + p * LANES:r * kvw + (p + 1) * LANES] for r in range(rep)], axis=0)
        zero = jnp.zeros_like(qg)
        qrows.append(jnp.concatenate([qg if other == p else zero for other in range(lane_blocks)], axis=1))
    s_full = lax.dot_general(jnp.concatenate(kbs, axis=0).astype(BF16), jnp.concatenate(qrows, axis=0),
                             (((1,), (1,)), ((), ())), preferred_element_type=F32)
    for p in range(lane_blocks):
        vs = slice(kvw + p * LANES, kvw + (p + 1) * LANES)
        vb = jnp.concatenate([kvp[:, vs], kvc[:, vs]], axis=0).astype(BF16)
        s_all = s_full[:, p * rep * w:(p + 1) * rep * w]
        pcats, denoms = [], []
        for half in range(heads_per_blk):
            g = p * heads_per_blk + half
            s = s_all[half * 2 * w:(half + 1) * 2 * w]
            bias = bias_ref[g] if first is None else bias_ref[g] + mask_add
            sc = jnp.where(upper, s[:w], s[w:]) * (scale * log2e) + bias
            sink = jnp.concatenate([jnp.full((1, w), sink_ref[g * rep + r] * log2e, F32) for r in range(rep)],
                                   axis=1)
            m = jnp.maximum(jnp.max(sc, axis=0, keepdims=True), sink)
            pe = jnp.exp2(sc - m)
            denoms.append(jnp.sum(pe, axis=0, keepdims=True) + jnp.exp2(sink - m))
            pcats.append(jnp.concatenate([jnp.where(upper, pe, 0.0), jnp.where(upper, 0.0, pe)], axis=0).astype(BF16))
        og_all = lax.dot_general(vb, jnp.concatenate(pcats, axis=1), (((0,), (0,)), ((), ())),
                                 preferred_element_type=F32)
        halves = [og_all[half * hd:(half + 1) * hd, half * rep * w:(half + 1) * rep * w] / denoms[half]
                  for half in range(heads_per_blk)]
        o_blk = jnp.concatenate(halves, axis=0).astype(BF16)
        for r in range(rep):
            ot_ref[r * kvw + p * LANES:r * kvw + (p + 1) * LANES, col0:col0 + w] = o_blk[:, r * w:(r + 1) * w]


def _attn_prompt(q, kv, bias, sinks, *, bsz, nblk, window, kvh, rep, hd):
    rows, nq = q.shape
    kvw2 = kv.shape[1]
    nsub = next(n for n in (4, 2, 1) if nblk % n == 0 and rows % (n * window) == 0)
    nstep = nblk // nsub
    kern = functools.partial(_attn_prompt_kernel, kvh=kvh, rep=rep, hd=hd, scale=1.0 / math.sqrt(hd))
    return pl.pallas_call(
        kern,
        grid=(bsz, nstep),
        in_specs=[pl.BlockSpec((nsub * window, nq), lambda b, i: (b * nstep + i, 0)),
                  pl.BlockSpec((nsub * window, kvw2), lambda b, i: (b * nstep + i, 0)),
                  pl.BlockSpec((window, kvw2), lambda b, i: (b * nblk + jnp.maximum(nsub * i - 1, 0), 0)),
                  _const_spec(bias.shape),
                  pl.BlockSpec(memory_space=pltpu.SMEM)],
        out_specs=pl.BlockSpec((nq, nsub * window), lambda b, i: (0, b * nstep + i)),
        out_shape=jax.ShapeDtypeStruct((nq, rows), BF16),
        compiler_params=_params("arbitrary", "arbitrary"),
        name="swa_prompt",
    )(q, kv, kv, bias, sinks)


def _attn_sample_kernel(q_ref, kv_ref, ck_ref, cv_ref, bias_ref, sink_ref, o_ref, nk_ref, nv_ref,
                        *, kvh, rep, hd, scale):
    nb, w, kvw = ck_ref.shape
    kv = kv_ref[...]
    lane = lax.broadcasted_iota(jnp.int32, (nb, rep, kvw), 2)

    def shifted(c_ref, n_ref, new):
        flat = c_ref[...].reshape(nb * w, kvw)
        n_ref[...] = pltpu.roll(flat, nb * w - 1, axis=0).reshape(nb, w, kvw)
        n_ref[:, w - 1:w, :] = new
        return n_ref[...].astype(BF16)

    nkb = shifted(ck_ref, nk_ref, kv[:, :, :kvw])
    nvb = shifted(cv_ref, nv_ref, kv[:, :, kvw:])
    q = q_ref[...].astype(F32)
    masks = [jnp.logical_and(lane >= g * hd, lane < (g + 1) * hd) for g in range(kvh)]
    qall = jnp.concatenate([jnp.where(masks[g], q, 0.0) for g in range(kvh)], axis=1).astype(BF16)
    s = jnp.einsum("nmc,njc->nmj", qall, nkb, preferred_element_type=F32)
    sc = s * scale + bias_ref[...][None]
    sink = sink_ref[...][None]
    m = jnp.maximum(jnp.max(sc, axis=-1, keepdims=True), sink)
    pe = jnp.exp(sc - m)
    probs = pe / (jnp.sum(pe, axis=-1, keepdims=True) + jnp.exp(sink - m))
    og = jnp.einsum("nmj,njc->nmc", probs.astype(BF16), nvb, preferred_element_type=F32)
    o = jnp.zeros((nb, rep, kvw), F32)
    for g in range(kvh):
        o = jnp.where(masks[g], og[:, g * rep:(g + 1) * rep, :], o)
    o_ref[...] = o.astype(BF16)


def _attn_sample(q3, kv3, ck, cv, bias_s, sink_s, *, nb, kvh, rep, hd):
    n, w, kvw = ck.shape
    kern = functools.partial(_attn_sample_kernel, kvh=kvh, rep=rep, hd=hd, scale=1.0 / math.sqrt(hd))
    cache_spec = pl.BlockSpec((nb, w, kvw), lambda i: (i, 0, 0))
    return pl.pallas_call(
        kern,
        grid=(n // nb,),
        in_specs=[pl.BlockSpec((nb, rep, kvw), lambda i: (i, 0, 0)),
                  pl.BlockSpec((nb, 1, 2 * kvw), lambda i: (i, 0, 0)),
                  cache_spec, cache_spec,
                  _const_spec(bias_s.shape), _const_spec(sink_s.shape)],
        out_specs=[pl.BlockSpec((nb, rep, kvw), lambda i: (i, 0, 0)), cache_spec, cache_spec],
        out_shape=[jax.ShapeDtypeStruct((n, rep, kvw), BF16),
                   jax.ShapeDtypeStruct((n, w, kvw), F32), jax.ShapeDtypeStruct((n, w, kvw), F32)],
        compiler_params=_params("arbitrary"),
        name="swa_sample",
    )(q3, kv3, ck, cv, bias_s, sink_s)


CHUNK = 16
BIG_COPY = 128
SUB = 256
UP_ROW_BLOCK = 1024
DOWN_ROW_BLOCK = 1024
TOKEN_TILE = 512
MOE_TILE = 512


def _chunk_copy(src, src_row, dst, dst_row, sem, rows=CHUNK):
    return pltpu.make_async_copy(src.at[pl.ds(pl.multiple_of(src_row, CHUNK), rows), :],
                                 dst.at[pl.ds(pl.multiple_of(dst_row, CHUNK), rows), :], sem)


def _start_copies(src, src_row, dst, dst_row, sem, nrows):
    nbig = lax.shift_right_logical(nrows, int(math.log2(BIG_COPY)))

    def big(c, carry):
        _chunk_copy(src, src_row + c * BIG_COPY, dst, dst_row + c * BIG_COPY, sem, BIG_COPY).start()
        return carry

    lax.fori_loop(0, nbig, big, 0)
    done = nbig * BIG_COPY

    def small(c, carry):
        _chunk_copy(src, src_row + done + c * CHUNK, dst, dst_row + done + c * CHUNK, sem).start()
        return carry

    lax.fori_loop(0, lax.shift_right_logical(nrows - done, int(math.log2(CHUNK))), small, 0)


def _moe_route_kernel(x2_ref, ot_ref, wo_ref, gffn_ref, wr_ref, lst_ref, ust_ref,
                      x3_ref, cm_ref, tmeta_ref, xs_hbm,
                      comp_s, carry_s, zero_s, base_v, pend_sm, sem, *, ne, cap, n_valid):
    i = pl.program_id(0)
    nt = pl.num_programs(0)
    tm = x2_ref.shape[0]
    w = comp_s.shape[1]
    slot = lax.rem(i, 2)

    @pl.when(i == 0)
    def _():
        base_v[...] = jnp.zeros_like(base_v)
        zero_s[...] = jnp.zeros_like(zero_s)
        carry_s[...] = jnp.zeros_like(carry_s)

    def drain(n):
        def body(c, carry):
            _chunk_copy(zero_s, 0, xs_hbm, 0, sem).wait()
            return carry
        lax.fori_loop(0, n, body, 0)

    @pl.when(i > 0)
    def _():
        drain(pend_sm[0])

    x3 = x2_ref[...] + lax.dot_general(ot_ref[...], wo_ref[...], (((0,), (0,)), ((), ())),
                                       preferred_element_type=F32)
    x3_ref[...] = x3
    hb = _rms(x3, gffn_ref[...]).astype(BF16)

    lane = lax.broadcasted_iota(jnp.int32, (tm, LANES), 1)
    logits = jnp.where(lane < ne, _dot(hb, wr_ref[...]), -jnp.inf)
    m1 = jnp.max(logits, axis=-1, keepdims=True)
    i1 = jnp.min(jnp.where(logits == m1, lane, LANES), axis=-1, keepdims=True)
    rest = jnp.where(lane == i1, -jnp.inf, logits)
    m2 = jnp.max(rest, axis=-1, keepdims=True)
    i2 = jnp.min(jnp.where(rest == m2, lane, LANES), axis=-1, keepdims=True)
    e2 = jnp.exp(m2 - m1)
    g1 = 1.0 / (1.0 + e2)
    g2 = e2 / (1.0 + e2)

    live = i * tm + lax.broadcasted_iota(jnp.int32, (tm, 1), 0) < n_valid
    sel = jnp.where(jnp.logical_and(live, jnp.logical_or(lane == i1, lane == i2)), 1.0, 0.0)
    rank = _dot(lst_ref[...], sel.astype(BF16))
    cnt = jnp.sum(sel, axis=0, keepdims=True)
    fill = base_v[...]
    rem = fill - jnp.floor(fill / CHUNK) * CHUNK
    cpad = jnp.floor((rem + cnt + (CHUNK - 1)) / CHUNK) * CHUNK
    loff = _dot(jnp.broadcast_to(cpad, (8, LANES)).astype(BF16), ust_ref[...])[0:1]
    dest = loff + rem + rank
    ld1 = jnp.where(live, jnp.sum(jnp.where(lane == i1, dest, 0.0), axis=-1, keepdims=True), -1.0)
    ld2 = jnp.where(live, jnp.sum(jnp.where(lane == i2, dest, 0.0), axis=-1, keepdims=True), -1.0)
    cm = jnp.where(lane == 0, ld1, jnp.where(lane == 1, ld2, jnp.where(lane == 2, g1, jnp.where(lane == 3, g2, 0.0))))
    cm_ref[...] = cm

    rm = cm.T
    rowi = lax.broadcasted_iota(jnp.int32, (w, tm), 0).astype(F32)
    place = (jnp.where(rowi == rm[0:1], 1.0, 0.0) + jnp.where(rowi == rm[1:2], 1.0, 0.0)).astype(BF16)
    comp_s[slot] = _dot(place, hb).astype(BF16)

    base = fill - rem
    srow = lax.broadcasted_iota(jnp.int32, (8, LANES), 0)
    tmeta_ref[...] = jnp.where(srow == 0, cpad, jnp.where(srow == 1, base, jnp.where(srow == 2, fill + cnt, 0.0))
                               ).astype(jnp.int32)
    base_v[...] = fill + cnt

    cpad_i = cpad.astype(jnp.int32)
    loff_i = loff.astype(jnp.int32)
    base_i = base.astype(jnp.int32)
    tail_i = (rem + cnt - jnp.floor((rem + cnt) / CHUNK) * CHUNK).astype(jnp.int32)
    total = 0
    tails = []
    for e in range(ne):
        n_e = cpad_i[0, e]
        src0 = loff_i[0, e]
        dst0 = base_i[0, e] + e * cap

        @pl.when(n_e > 0)
        def _(e=e, n_e=n_e, src0=src0, partial=tail_i[0, e] > 0):
            head = pl.ds(pl.multiple_of(src0, CHUNK), CHUNK)
            comp_s[slot, head, :] = comp_s[slot, head, :] + carry_s[e]
            last = comp_s[slot, pl.ds(pl.multiple_of(src0 + n_e - CHUNK, CHUNK), CHUNK), :]
            carry_s[e] = jnp.where(partial, last, jnp.zeros_like(last))

        _start_copies(comp_s.at[slot], src0, xs_hbm, dst0, sem, n_e)
        total = total + lax.shift_right_logical(n_e, int(math.log2(CHUNK)))
        tails.append(dst0 + n_e)
    pend_sm[0] = total

    @pl.when(i == nt - 1)
    def _():
        nz_total = 0
        for e in range(ne):
            end = tails[e]
            nz = lax.shift_right_logical(lax.rem(SUB - lax.rem(end, SUB), SUB), int(math.log2(CHUNK)))

            def zstart(c, carry, end=end):
                _chunk_copy(zero_s, 0, xs_hbm, end + c * CHUNK, sem).start()
                return carry

            lax.fori_loop(0, nz, zstart, 0)
            nz_total = nz_total + nz
        drain(total + nz_total)


def _moe_route(x2, ot, wo, gffn, wr_pad, *, tm, ne, cap, n_valid):
    rows, d = x2.shape
    nt = rows // tm
    w = _staging_rows(tm, ne)
    lst = jnp.asarray(np.tril(np.ones((tm, tm), np.float32), -1), BF16)
    ust = jnp.asarray(np.triu(np.ones((LANES, LANES), np.float32), 1), BF16)
    kern = functools.partial(_moe_route_kernel, ne=ne, cap=cap, n_valid=n_valid)
    return pl.pallas_call(
        kern,
        grid=(nt,),
        in_specs=[pl.BlockSpec((tm, d), lambda i: (i, 0)),
                  pl.BlockSpec((ot.shape[0], tm), lambda i: (0, i)),
                  _const_spec(wo.shape), _const_spec(gffn.shape), _const_spec(wr_pad.shape),
                  _const_spec(lst.shape), _const_spec(ust.shape)],
        out_specs=[pl.BlockSpec((tm, d), lambda i: (i, 0)),
                   pl.BlockSpec((tm, LANES), lambda i: (i, 0)),
                   pl.BlockSpec((8, LANES), lambda i: (i, 0)),
                   pl.BlockSpec(memory_space=pl.ANY)],
        out_shape=[jax.ShapeDtypeStruct((rows, d), F32),
                   jax.ShapeDtypeStruct((rows, LANES), F32),
                   jax.ShapeDtypeStruct((nt * 8, LANES), jnp.int32),
                   jax.ShapeDtypeStruct((ne * cap, d), BF16)],
        scratch_shapes=[pltpu.VMEM((2, w, d), BF16), pltpu.VMEM((ne, CHUNK, d), BF16), pltpu.VMEM((CHUNK, d), BF16),
                        pltpu.VMEM((1, LANES), F32), pltpu.SMEM((1,), jnp.int32),
                        pltpu.SemaphoreType.DMA(())],
        compiler_params=_params("arbitrary"),
        name="moe_route",
    )(x2, ot, wo, gffn, wr_pad, lst, ust)


def _staging_rows(tm, ne):
    return -(-(TOP_K * tm + 2 * ne * (CHUNK - 1)) // LANES) * LANES


def _max_blocks(rows, ne, row_block):
    return TOP_K * rows // row_block + ne


def _expert_steps(nrows, nf, max_blocks, row_block):
    ne = nrows.shape[0]
    nblk = (nrows + row_block - 1) // row_block
    cum = jnp.cumsum(nblk) * nf
    total = cum[-1]
    s = jnp.minimum(jnp.arange(max_blocks * nf, dtype=jnp.int32), total - 1)
    e = jnp.minimum(jnp.sum((s[:, None] >= cum[None, :]).astype(jnp.int32), axis=1), ne - 1)
    nb_e = nblk[e]
    within = s - (cum[e] - nb_e * nf)
    f = within // nb_e
    pos = within - f * nb_e
    r = (pos + nb_e - 1) % nb_e
    nsub = jnp.clip((nrows[e] - r * row_block + SUB - 1) // SUB, 0, row_block // SUB)
    first = (pos == 0).astype(jnp.int32)
    return (e, f.astype(jnp.int32), r.astype(jnp.int32), first, nsub.astype(jnp.int32),
            jnp.reshape(total, (1,)).astype(jnp.int32))


def _moe_up_kernel(e_ref, f_ref, r_ref, first_ref, nsub_ref, n_ref, xs_ref, wg_ref, wu_ref, act_ref, wgb_s, wub_s):
    s = pl.program_id(0)

    @pl.when(s < n_ref[0])
    def _():
        @pl.when(first_ref[s] == 1)
        def _():
            wgb_s[...] = wg_ref[...].astype(BF16)
            wub_s[...] = wu_ref[...].astype(BF16)

        def gate_up(row0, nrows):
            rows = pl.ds(pl.multiple_of(row0, SUB), nrows)
            xsb = xs_ref[rows, :]
            a = _dot(xsb, wgb_s[...])
            b = _dot(xsb, wub_s[...])
            act_ref[rows, :] = (jax.nn.silu(a) * b).astype(BF16)

        nsub = nsub_ref[s]
        npair = lax.shift_right_logical(nsub, 1)

        def body(j, carry):
            gate_up(j * (2 * SUB), 2 * SUB)
            return carry

        lax.fori_loop(0, npair, body, 0)

        @pl.when(nsub > 2 * npair)
        def _():
            gate_up(npair * (2 * SUB), SUB)


def _moe_up(steps, xs, wgu, *, fc, cap, row_block):
    ne, d, dff2 = wgu.shape
    dff = dff2 // 2
    nf = dff // fc
    cb = cap // row_block
    grid_spec = pltpu.PrefetchScalarGridSpec(
        num_scalar_prefetch=6,
        grid=(steps[0].shape[0],),
        in_specs=[pl.BlockSpec((row_block, d), lambda s, e, f, r, *_: (e[s] * cb + r[s], 0)),
                  pl.BlockSpec((None, d, fc), lambda s, e, f, r, *_: (e[s], 0, f[s])),
                  pl.BlockSpec((None, d, fc), lambda s, e, f, r, *_: (e[s], 0, nf + f[s]))],
        out_specs=pl.BlockSpec((row_block, fc), lambda s, e, f, r, *_: (e[s] * cb + r[s], f[s])),
        scratch_shapes=[pltpu.VMEM((d, fc), BF16), pltpu.VMEM((d, fc), BF16)])
    return pl.pallas_call(
        _moe_up_kernel,
        grid_spec=grid_spec,
        out_shape=jax.ShapeDtypeStruct((ne * cap, dff), BF16),
        compiler_params=_params("arbitrary"),
        name="moe_up",
    )(*steps, xs, wgu, wgu)


def _moe_down_kernel(e_ref, f_ref, r_ref, first_ref, nsub_ref, n_ref, act_ref, wd_ref, ys_ref, wdb_s):
    s = pl.program_id(0)

    @pl.when(s < n_ref[0])
    def _():
        @pl.when(first_ref[s] == 1)
        def _():
            wdb_s[...] = wd_ref[...].astype(BF16)

        def project(row0, nrows):
            rows = pl.ds(pl.multiple_of(row0, SUB), nrows)
            ys_ref[rows, :] = _dot(act_ref[rows, :], wdb_s[...]).astype(BF16)

        nsub = nsub_ref[s]
        npair = lax.shift_right_logical(nsub, 1)

        def body(j, carry):
            project(j * (2 * SUB), 2 * SUB)
            return carry

        lax.fori_loop(0, npair, body, 0)

        @pl.when(nsub > 2 * npair)
        def _():
            project(npair * (2 * SUB), SUB)


def _moe_down(steps, act, wd, *, cap, row_block):
    ne, dff, d = wd.shape
    cb = cap // row_block
    grid_spec = pltpu.PrefetchScalarGridSpec(
        num_scalar_prefetch=6,
        grid=(steps[0].shape[0],),
        in_specs=[pl.BlockSpec((row_block, dff), lambda s, e, f, r, *_: (e[s] * cb + r[s], 0)),
                  pl.BlockSpec((None, dff, d), lambda s, e, f, r, *_: (e[s], 0, 0))],
        out_specs=pl.BlockSpec((row_block, d), lambda s, e, f, r, *_: (e[s] * cb + r[s], 0)),
        scratch_shapes=[pltpu.VMEM((dff, d), BF16)])
    return pl.pallas_call(
        _moe_down_kernel,
        grid_spec=grid_spec,
        out_shape=jax.ShapeDtypeStruct((ne * cap, d), BF16),
        compiler_params=_params("arbitrary", vmem=2 * (dff * d * 4 + row_block * (dff + d) * 2) + dff * d * 2
                                + 4 * SUB * d * 4),
        name="moe_down",
    )(*steps, act, wd)


def _moe_combine_kernel(cpad_ref, seg_ref, x3_ref, cm_ref, gfin_ref, ys_hbm, yp_ref, ys_ref, yloc_s, sem,
                        *, ne, npt):
    i = pl.program_id(0)
    nt = pl.num_programs(0)
    tm = x3_ref.shape[0]
    w = yloc_s.shape[1]
    slot = lax.rem(i, 2)
    shift = int(math.log2(CHUNK))

    def issue(t, sl):
        off = 0
        for e in range(ne):
            n_e = cpad_ref[t * ne + e]
            _start_copies(ys_hbm, seg_ref[t * ne + e], yloc_s.at[sl], off, sem.at[sl], n_e)
            off = off + n_e

    @pl.when(i == 0)
    def _():
        yloc_s[...] = jnp.zeros_like(yloc_s)
        issue(0, 0)

    @pl.when(i + 1 < nt)
    def _():
        issue(i + 1, 1 - slot)

    total = 0
    for e in range(ne):
        total = total + lax.shift_right_logical(cpad_ref[i * ne + e], shift)

    def wait(c, carry):
        _chunk_copy(ys_hbm, 0, yloc_s.at[slot], 0, sem.at[slot]).wait()
        return carry

    lax.fori_loop(0, total, wait, 0)

    cm = cm_ref[...]
    col = lax.broadcasted_iota(jnp.int32, (tm, w), 1).astype(F32)
    yl = yloc_s[slot]
    y1 = _dot(jnp.where(col == cm[:, 0:1], 1.0, 0.0).astype(BF16), yl)
    y2 = _dot(jnp.where(col == cm[:, 1:2], 1.0, 0.0).astype(BF16), yl)
    y = _rms(x3_ref[...] + cm[:, 2:3] * y1 + cm[:, 3:4] * y2, gfin_ref[...])

    @pl.when(i < npt)
    def _():
        yp_ref[...] = y

    @pl.when(i >= npt)
    def _():
        ys_ref[...] = y


def _moe_combine(cpad, seg, x3, cm, gfin, ys, *, tm, ne, sample_rows):
    rows, d = x3.shape
    nt = rows // tm
    npt = (rows - sample_rows) // tm
    w = _staging_rows(tm, ne)
    kern = functools.partial(_moe_combine_kernel, ne=ne, npt=npt)
    grid_spec = pltpu.PrefetchScalarGridSpec(
        num_scalar_prefetch=2,
        grid=(nt,),
        in_specs=[pl.BlockSpec((tm, d), lambda i, *_: (i, 0)),
                  pl.BlockSpec((tm, LANES), lambda i, *_: (i, 0)),
                  pl.BlockSpec(gfin.shape, lambda i, *_: (0, 0)),
                  pl.BlockSpec(memory_space=pl.ANY)],
        out_specs=[pl.BlockSpec((tm, d), lambda i, *_: (jnp.minimum(i, npt - 1), 0)),
                   pl.BlockSpec((tm, d), lambda i, *_: (jnp.maximum(i - npt, 0), 0))],
        scratch_shapes=[pltpu.VMEM((2, w, d), BF16), pltpu.SemaphoreType.DMA((2,))])
    return pl.pallas_call(
        kern,
        grid_spec=grid_spec,
        out_shape=[jax.ShapeDtypeStruct((npt * tm, d), F32), jax.ShapeDtypeStruct((sample_rows, d), F32)],
        compiler_params=_params("arbitrary"),
        name="moe_combine",
    )(cpad, seg, x3, cm, gfin, ys)


def _moe(x2, ot, wo, gffn, wr, wgu, wd, gfin, *, tm, fc, n_valid, sample_rows):
    rows, d = x2.shape
    ne = wr.shape[1]
    nt = rows // tm
    nf = wgu.shape[2] // 2 // fc
    cap = -(-(rows + SUB) // UP_ROW_BLOCK) * UP_ROW_BLOCK
    wr_pad = jnp.pad(wr, ((0, 0), (0, LANES - ne))).astype(BF16)
    x3, cm, tmeta, xs = _moe_route(x2, ot, wo, gffn, wr_pad, tm=tm, ne=ne, cap=cap, n_valid=n_valid)
    tmeta = tmeta.reshape(nt, 8, LANES)[:, :, :ne]
    cpad = tmeta[:, 0, :].reshape(nt * ne)
    seg = (tmeta[:, 1, :] + jnp.arange(ne, dtype=jnp.int32)[None, :] * cap).reshape(nt * ne)
    nrows = tmeta[nt - 1, 2, :]
    up_steps = _expert_steps(nrows, nf, _max_blocks(rows, ne, UP_ROW_BLOCK), UP_ROW_BLOCK)
    act = _moe_up(up_steps, xs, wgu, fc=fc, cap=cap, row_block=UP_ROW_BLOCK)
    down_steps = _expert_steps(nrows, 1, _max_blocks(rows, ne, DOWN_ROW_BLOCK), DOWN_ROW_BLOCK)
    ys = _moe_down(down_steps, act, wd, cap=cap, row_block=DOWN_ROW_BLOCK)
    return _moe_combine(cpad, seg, x3, cm, gfin, ys, tm=tm, ne=ne, sample_rows=sample_rows)


def kernel(x_prompt, x_sample, state_ssm_re, state_ssm_im, cache_k_win, cache_v_win, g_mix, g_ffn, g_kv, g_final, ssm_a_re, ssm_a_im, ssm_log_dt, ssm_b_re, ssm_b_im, ssm_c_re, ssm_c_im, ssm_d, w_glu, w_kv, w_q, w_o, attn_sinks, rel_bias, w_ffn_gate_up, w_ffn_down, w_router, w_exp_gate_up, w_exp_down):
    bsz, seq, d = x_prompt.shape
    ns, dec_seq, _ = x_sample.shape
    assert dec_seq == 1 and g_mix.shape[0] == 2 and ssm_a_re.shape[0] == 1 and w_q.shape[0] == 1
    _, g, p = ssm_a_re.shape
    gp = g * p
    window, kvh, hd = cache_k_win.shape[1:]
    kvw = kvh * hd
    nh = attn_sinks.shape[1]
    rep = nh // kvh
    nq = nh * hd
    assert bsz == 8 and ns % bsz == 0 and seq % window == 0 and LANES % hd == 0

    lam_re, lam_im, wb_re, wb_im, wc_re, wc_imn = _zoh(ssm_a_re[0], ssm_a_im[0], ssm_log_dt[0],
                                                        ssm_b_re[0], ssm_b_im[0], ssm_c_re[0], ssm_c_im[0])
    wglu = w_glu[0].astype(BF16)
    d_skip = ssm_d[0].reshape(1, d)
    wgu = w_ffn_gate_up[0].astype(BF16)
    wd = w_ffn_down[0].astype(BF16)
    wkv = w_kv.astype(BF16)
    wq = w_q[0].reshape(d, kvh, rep, hd).transpose(0, 2, 1, 3).reshape(d, nq).astype(BF16)
    wo = w_o[0].reshape(kvh, rep, hd, d).transpose(1, 0, 2, 3).reshape(nq, d).astype(BF16)
    bias, bias_s = _bias_tables(rel_bias, window, kvh, rep)
    bias_s = bias_s.reshape(nh, window)
    sinks = attn_sinks[0]
    sink_s = sinks.reshape(nh, 1)

    tm = TOKEN_TILE
    npr = seq * bsz
    assert npr % tm == 0 and ns <= tm

    x1_p, st_re, st_im = _ssm_prompt(x_prompt, g_mix[0:1], lam_re, lam_im, wb_re, wb_im, wc_re, wc_imn,
                                     d_skip, wglu, lc=32, nsub=2)
    x1_s, hs_re, hs_im = _ssm_sample(x_sample.reshape(ns, d), g_mix[0:1], lam_re, lam_im,
                                     state_ssm_re[0].reshape(ns, gp), state_ssm_im[0].reshape(ns, gp),
                                     wb_re, wb_im, wc_re, wc_imn, d_skip, wglu, pad_rows=tm)

    x2, kv, q = _ffn(x1_p.reshape(npr, d), x1_s, g_ffn[0:1], wgu, wd, g_kv.reshape(1, d), wkv, g_mix[1:2], wq)

    ot = _attn_prompt(q, kv, bias, sinks, bsz=bsz, nblk=seq // window, window=window, kvh=kvh, rep=rep, hd=hd)
    o_s, nk_s, nv_s = _attn_sample(q[npr:npr + ns].reshape(ns, rep, kvw), kv[npr:npr + ns].reshape(ns, 1, 2 * kvw),
                                   cache_k_win.reshape(ns, window, kvw), cache_v_win.reshape(ns, window, kvw),
                                   bias_s, sink_s, nb=32, kvh=kvh, rep=rep, hd=hd)
    ot = lax.dynamic_update_slice(ot, jnp.pad(o_s.reshape(ns, nq).T, ((0, 0), (0, tm - ns))), (0, npr))

    y_p, y_s = _moe(x2, ot, wo, g_ffn[1:2], w_router[0], w_exp_gate_up[0], w_exp_down[0], g_final.reshape(1, d),
                    tm=MOE_TILE, fc=w_exp_down.shape[2] // 2, n_valid=npr + ns, sample_rows=tm)

    y_prompt = y_p.reshape(bsz, seq, d)
    y_sample = y_s[:ns].reshape(ns, 1, d)
    kv_tail = jnp.stack([kv[(b + 1) * seq - window:(b + 1) * seq] for b in range(bsz)])
    kv_tail = kv_tail.reshape(bsz, window, 2, kvh, hd).transpose(2, 0, 1, 3, 4)
    return (y_prompt, y_sample,
            st_re.reshape(1, bsz, g, p), st_im.reshape(1, bsz, g, p), kv_tail[0], kv_tail[1],
            hs_re.reshape(1, ns, g, p), hs_im.reshape(1, ns, g, p),
            nk_s.reshape(ns, window, kvh, hd), nv_s.reshape(ns, window, kvh, hd))
```

```python
import functools
import math

import numpy as np
import jax
import jax.numpy as jnp
from jax import lax
from jax.experimental import pallas as pl
from jax.experimental.pallas import tpu as pltpu

F32 = jnp.float32
BF16 = jnp.bfloat16

EPS = 1e-6
NEG_INF = -1e30
TOP_K = 2
MAX_DISTANCE = 128
MXU_DIM = 256
LANES = 128
VMEM_LIMIT_BYTES = 56 * 1024 * 1024


def _dot(a, b):
    return jnp.dot(a, b, preferred_element_type=F32)


def _rms(x, g):
    return x * lax.rsqrt(jnp.mean(x * x, axis=-1, keepdims=True) + EPS) * g


def _const_spec(shape):
    nd = len(shape)
    return pl.BlockSpec(shape, lambda *_: (0,) * nd)


def _params(*sem, vmem=VMEM_LIMIT_BYTES):
    return pltpu.CompilerParams(dimension_semantics=sem, vmem_limit_bytes=vmem)


def _zoh_kernel(a_re_ref, a_im_ref, log_dt_ref, b_re_ref, b_im_ref, ct_re_ref, ct_im_ref,
                lam_re_ref, lam_im_ref, wb_re_ref, wb_im_ref, wc_re_ref, wc_imn_ref, *, hch, p):
    a_re = a_re_ref[...]
    a_im = a_im_ref[...]
    dt = jnp.exp(log_dt_ref[...])
    mag = jnp.exp(a_re * dt)
    lr = mag * jnp.cos(a_im * dt)
    li = mag * jnp.sin(a_im * dt)
    lam_re_ref[...] = lr
    lam_im_ref[...] = li
    nr = lr - 1.0
    den = a_re * a_re + a_im * a_im
    qr = (nr * a_re + li * a_im) / den
    qi = (li * a_re - nr * a_im) / den
    b_re = b_re_ref[...]
    b_im = b_im_ref[...]
    bb_re = qr * b_re - qi * b_im
    bb_im = qr * b_im + qi * b_re

    nkb, ublk, sblk = wb_re_ref.shape
    gpb = ublk // hch
    sh_h, sh_p = int(math.log2(hch)), int(math.log2(p))
    row = lax.broadcasted_iota(jnp.int32, (ublk, sblk), 0)
    col = lax.broadcasted_iota(jnp.int32, (ublk, sblk), 1)
    diag_in = lax.shift_right_logical(row, sh_h) == lax.shift_right_logical(col, sh_p)
    row = lax.broadcasted_iota(jnp.int32, (sblk, ublk), 0)
    col = lax.broadcasted_iota(jnp.int32, (sblk, ublk), 1)
    diag_out = lax.shift_right_logical(row, sh_p) == lax.shift_right_logical(col, sh_h)
    spread = jnp.where(lax.broadcasted_iota(jnp.int32, (hch, ublk), 0)
                       == jnp.bitwise_and(lax.broadcasted_iota(jnp.int32, (hch, ublk), 1), hch - 1),
                       1.0, 0.0).astype(BF16)
    for kb in range(nkb):
        cs = slice(kb * sblk, (kb + 1) * sblk)
        wb_re_ref[kb] = jnp.where(diag_in, jnp.concatenate([bb_re[:, cs]] * gpb, axis=0), 0.0).astype(BF16)
        wb_im_ref[kb] = jnp.where(diag_in, jnp.concatenate([bb_im[:, cs]] * gpb, axis=0), 0.0).astype(BF16)
        c_re = _dot(ct_re_ref[cs, :].astype(BF16), spread)
        c_im = _dot(ct_im_ref[cs, :].astype(BF16), spread)
        wc_re_ref[kb] = jnp.where(diag_out, c_re, 0.0).astype(BF16)
        wc_imn_ref[kb] = jnp.where(diag_out, -c_im, 0.0).astype(BF16)


def _zoh(a_re, a_im, log_dt, b_re, b_im, c_re, c_im):
    g, p = a_re.shape
    h = b_re.shape[-1]
    gp = g * p
    gpb = MXU_DIM // h
    nkb = g // gpb
    assert h & (h - 1) == 0 and p & (p - 1) == 0 and g % gpb == 0
    row = jax.ShapeDtypeStruct((1, gp), F32)
    wb = jax.ShapeDtypeStruct((nkb, gpb * h, gpb * p), BF16)
    wc = jax.ShapeDtypeStruct((nkb, gpb * p, gpb * h), BF16)
    return pl.pallas_call(
        functools.partial(_zoh_kernel, hch=h, p=p),
        out_shape=(row, row, wb, wb, wc, wc),
        name="s5_zoh",
    )(a_re.reshape(1, gp), a_im.reshape(1, gp), jnp.repeat(log_dt, p).reshape(1, gp),
      b_re.transpose(2, 0, 1).reshape(h, gp), b_im.transpose(2, 0, 1).reshape(h, gp),
      c_re.transpose(0, 2, 1).reshape(gp, h), c_im.transpose(0, 2, 1).reshape(gp, h))


def _ssm_in_proj(ub, wb_re_ref, wb_im_ref, bu_re, bu_im):
    nkb, ublk, sblk = wb_re_ref.shape
    for kb in range(nkb):
        ukb = ub[:, kb * ublk:(kb + 1) * ublk]
        bu_re[:, kb * sblk:(kb + 1) * sblk] = _dot(ukb, wb_re_ref[kb])
        bu_im[:, kb * sblk:(kb + 1) * sblk] = _dot(ukb, wb_im_ref[kb])


def _ssm_out_proj(h_re, h_im, wc_re_ref, wc_imn_ref):
    nkb, sblk, _ = wc_re_ref.shape
    ys = []
    for kb in range(nkb):
        hr = h_re[:, kb * sblk:(kb + 1) * sblk].astype(BF16)
        hi = h_im[:, kb * sblk:(kb + 1) * sblk].astype(BF16)
        ys.append(_dot(hr, wc_re_ref[kb]) + _dot(hi, wc_imn_ref[kb]))
    return jnp.concatenate(ys, axis=1)


def _ssm_glu(x, u, y, d_ref, wglu_ref):
    z = jax.nn.gelu(y + d_ref[...] * u).astype(BF16)
    gl = _dot(z, wglu_ref[...])
    d = x.shape[1]
    return x + gl[:, :d] * jax.nn.sigmoid(gl[:, d:])


def _split3(v):
    hi = v.astype(BF16)
    r1 = v - hi.astype(F32)
    mid = r1.astype(BF16)
    lo = (r1 - mid.astype(F32)).astype(BF16)
    return hi, mid, lo


def _ssm_prompt_kernel(x_ref, g_ref, lam_re_ref, lam_im_ref, wb_re_ref, wb_im_ref,
                       wc_re_ref, wc_imn_ref, d_ref, wglu_ref, perm_ref, permt_ref,
                       out_ref, st_re_ref, st_im_ref, bu_re_all, bu_im_all, *, lc, bsz, lane_chunk):
    @pl.when(pl.program_id(0) == 0)
    def _():
        st_re_ref[...] = jnp.zeros_like(st_re_ref)
        st_im_ref[...] = jnp.zeros_like(st_im_ref)

    for sub in range(bu_re_all.shape[0]):
        ts = slice(sub * lc, (sub + 1) * lc)
        _ssm_prompt_chunk(x_ref[:, ts, :], g_ref, lam_re_ref, lam_im_ref, wb_re_ref, wb_im_ref,
                          wc_re_ref, wc_imn_ref, d_ref, wglu_ref, perm_ref, permt_ref,
                          out_ref.at[:, ts, :], st_re_ref, st_im_ref, bu_re_all.at[sub], bu_im_all.at[sub],
                          lc=lc, bsz=bsz, lane_chunk=lane_chunk)


def _ssm_prompt_chunk(x3, g_ref, lam_re_ref, lam_im_ref, wb_re_ref, wb_im_ref,
                      wc_re_ref, wc_imn_ref, d_ref, wglu_ref, perm_ref, permt_ref,
                      out_ref, st_re_ref, st_im_ref, bu_re, bu_im, *, lc, bsz, lane_chunk):
    d = x3.shape[2]
    x = x3.reshape(bsz * lc, d)
    u = _rms(x, g_ref[...])
    ub_tb = _dot(perm_ref[...], u.astype(BF16)).astype(BF16)
    _ssm_in_proj(ub_tb, wb_re_ref, wb_im_ref, bu_re, bu_im)

    gp = bu_re.shape[1]
    for c0 in range(0, gp, lane_chunk):
        sl = slice(c0, c0 + lane_chunk)
        lre = jnp.broadcast_to(lam_re_ref[:, sl], (bsz, lane_chunk))
        lim = jnp.broadcast_to(lam_im_ref[:, sl], (bsz, lane_chunk))
        hr = st_re_ref[:, sl]
        hi = st_im_ref[:, sl]
        for t in range(lc):
            rows = slice(t * bsz, (t + 1) * bsz)
            hr, hi = (lre * hr - lim * hi + bu_re[rows, sl], lre * hi + lim * hr + bu_im[rows, sl])
            bu_re[rows, sl] = hr
            bu_im[rows, sl] = hi
        st_re_ref[:, sl] = hr
        st_im_ref[:, sl] = hi

    permt = permt_ref[...]
    hi, mid, lo = _split3(_ssm_out_proj(bu_re, bu_im, wc_re_ref, wc_imn_ref))
    y = (_dot(permt, hi) + _dot(permt, mid)) + _dot(permt, lo)
    out_ref[...] = _ssm_glu(x, u, y, d_ref, wglu_ref).reshape(bsz, lc, d)


def _ssm_sample_kernel(x_ref, g_ref, lam_re_ref, lam_im_ref, h0_re_ref, h0_im_ref, wb_re_ref, wb_im_ref,
                       wc_re_ref, wc_imn_ref, d_ref, wglu_ref,
                       out_ref, h_re_ref, h_im_ref):
    x = x_ref[...]
    u = _rms(x, g_ref[...])
    _ssm_in_proj(u.astype(BF16), wb_re_ref, wb_im_ref, h_re_ref, h_im_ref)
    lre = lam_re_ref[...]
    lim = lam_im_ref[...]
    h0r = h0_re_ref[...]
    h0i = h0_im_ref[...]
    h_re_ref[...] = lre * h0r - lim * h0i + h_re_ref[...]
    h_im_ref[...] = lre * h0i + lim * h0r + h_im_ref[...]
    y = _ssm_out_proj(h_re_ref, h_im_ref, wc_re_ref, wc_imn_ref)
    n = x.shape[0]
    out_ref[0:n, :] = _ssm_glu(x, u, y, d_ref, wglu_ref)
    out_ref[n:, :] = jnp.zeros((out_ref.shape[0] - n, out_ref.shape[1]), F32)


def _ssm_prompt(x, g, lam_re, lam_im, wb_re, wb_im, wc_re, wc_imn, d_skip, wglu, *, lc, nsub):
    bsz, seq, d = x.shape
    assert seq % (lc * nsub) == 0
    gp = lam_re.shape[1]
    r = lc * bsz
    perm = np.zeros((r, r), np.float32)
    for b in range(bsz):
        for t in range(lc):
            perm[t * bsz + b, b * lc + t] = 1.0
    permt = jnp.asarray(perm.T, BF16)
    perm = jnp.asarray(perm, BF16)
    kern = functools.partial(_ssm_prompt_kernel, lc=lc, bsz=bsz, lane_chunk=8 * LANES)
    return pl.pallas_call(
        kern,
        grid=(seq // (lc * nsub),),
        in_specs=[pl.BlockSpec((bsz, lc * nsub, d), lambda c: (0, c, 0)),
                  _const_spec(g.shape), _const_spec(lam_re.shape), _const_spec(lam_im.shape),
                  _const_spec(wb_re.shape), _const_spec(wb_im.shape),
                  _const_spec(wc_re.shape), _const_spec(wc_imn.shape),
                  _const_spec(d_skip.shape), _const_spec(wglu.shape),
                  _const_spec(perm.shape), _const_spec(permt.shape)],
        out_specs=[pl.BlockSpec((bsz, lc * nsub, d), lambda c: (0, c, 0)),
                   _const_spec((bsz, gp)), _const_spec((bsz, gp))],
        out_shape=[jax.ShapeDtypeStruct((bsz, seq, d), F32),
                   jax.ShapeDtypeStruct((bsz, gp), F32), jax.ShapeDtypeStruct((bsz, gp), F32)],
        scratch_shapes=[pltpu.VMEM((nsub, r, gp), F32), pltpu.VMEM((nsub, r, gp), F32)],
        compiler_params=_params("arbitrary"),
        name="s5_prompt",
    )(x, g, lam_re, lam_im, wb_re, wb_im, wc_re, wc_imn, d_skip, wglu, perm, permt)


def _ssm_sample(x, g, lam_re, lam_im, h0_re, h0_im, wb_re, wb_im, wc_re, wc_imn, d_skip, wglu, *, pad_rows):
    n, d = x.shape
    gp = lam_re.shape[1]
    return pl.pallas_call(
        _ssm_sample_kernel,
        out_shape=[jax.ShapeDtypeStruct((pad_rows, d), F32),
                   jax.ShapeDtypeStruct((n, gp), F32), jax.ShapeDtypeStruct((n, gp), F32)],
        compiler_params=_params(),
        name="s5_sample",
    )(x, g, lam_re, lam_im, h0_re, h0_im, wb_re, wb_im, wc_re, wc_imn, d_skip, wglu)


def _ffn_chunks(d_ff):
    step = 3 * MXU_DIM
    return [(c, min(c + step, d_ff)) for c in range(0, d_ff, step)]


def _ffn_kernel(xp_ref, xs_ref, gffn_ref, wgu_ref, wd_ref, gkv_ref, wkv_ref, gq_ref, wq_ref,
                x2_ref, kv_ref, q_ref):
    x = jnp.where(pl.program_id(0) < pl.num_programs(0) - 1, xp_ref[...], xs_ref[...])
    hb = _rms(x, gffn_ref[...]).astype(BF16)
    d_ff = wd_ref.shape[0]
    acc = None
    for c0, c1 in _ffn_chunks(d_ff):
        a = _dot(hb, wgu_ref[:, c0:c1])
        b = _dot(hb, wgu_ref[:, d_ff + c0:d_ff + c1])
        part = _dot((jax.nn.silu(a) * b).astype(BF16), wd_ref[c0:c1, :])
        acc = part if acc is None else acc + part
    x2 = x + acc
    x2_ref[...] = x2
    kv_ref[...] = _dot(_rms(x2, gkv_ref[...]).astype(BF16), wkv_ref[...])
    q_ref[...] = _dot(_rms(x2, gq_ref[...]).astype(BF16), wq_ref[...]).astype(BF16)


def _ffn(xp, xs, gffn, wgu, wd, gkv, wkv, gq, wq):
    tm, d = xs.shape
    npt = xp.shape[0] // tm
    rows = xp.shape[0] + tm
    kvw = wkv.shape[1]
    nq = wq.shape[1]
    return pl.pallas_call(
        _ffn_kernel,
        grid=(npt + 1,),
        in_specs=[pl.BlockSpec((tm, d), lambda i: (jnp.minimum(i, npt - 1), 0)),
                  _const_spec(xs.shape),
                  _const_spec(gffn.shape), _const_spec(wgu.shape), _const_spec(wd.shape),
                  _const_spec(gkv.shape), _const_spec(wkv.shape),
                  _const_spec(gq.shape), _const_spec(wq.shape)],
        out_specs=[pl.BlockSpec((tm, d), lambda i: (i, 0)),
                   pl.BlockSpec((tm, kvw), lambda i: (i, 0)),
                   pl.BlockSpec((tm, nq), lambda i: (i, 0))],
        out_shape=[jax.ShapeDtypeStruct((rows, d), F32),
                   jax.ShapeDtypeStruct((rows, kvw), F32),
                   jax.ShapeDtypeStruct((rows, nq), BF16)],
        compiler_params=_params("arbitrary"),
        name="ffn_kv_q",
    )(xp, xs, gffn, wgu, wd, gkv, wkv, gq, wq)


def _t5_bucket(dist, num_buckets):
    max_exact = num_buckets // 2
    d = jnp.maximum(dist, 0)
    large = max_exact + (jnp.log(jnp.maximum(d, 1).astype(F32) / max_exact)
                         / math.log(MAX_DISTANCE / max_exact) * (num_buckets - max_exact)).astype(jnp.int32)
    large = jnp.minimum(large, num_buckets - 1)
    return jnp.where(d < max_exact, d, large)


def _bias_kernel(bm_ref, bs_ref, rb_ref, bias_ref, bias_s_ref, *, kvh, rep):
    nb, nh = rb_ref.shape
    bm = bm_ref[...]
    bs = bs_ref[...]

    def lookup(buckets, h):
        terms = [jnp.where(buckets == k, rb_ref[k, h], 0.0) for k in range(nb)]
        while len(terms) > 1:
            terms = [a + b for a, b in zip(terms[0::2], terms[1::2])] + (terms[-1:] if len(terms) % 2 else [])
        return terms[0]

    w = bm.shape[0]
    for g in range(kvh):
        for r in range(rep):
            bias_ref[g, :, r * w:(r + 1) * w] = lookup(bm, g * rep + r) * math.log2(math.e)
            bias_s_ref[g, r:r + 1, :] = lookup(bs, g * rep + r)


def _bias_tables(rel_bias, window, kvh, rep):
    nb, nh = rel_bias.shape
    key = jnp.arange(window)[:, None]
    qry = jnp.arange(window)[None, :]
    bm = _t5_bucket((qry - key) % window, nb).astype(jnp.int32)
    bs = _t5_bucket(window - 1 - qry, nb).astype(jnp.int32)
    kern = functools.partial(_bias_kernel, kvh=kvh, rep=rep)
    return pl.pallas_call(
        kern,
        in_specs=[pl.BlockSpec(memory_space=pltpu.VMEM), pl.BlockSpec(memory_space=pltpu.VMEM),
                  pl.BlockSpec(memory_space=pltpu.SMEM)],
        out_shape=[jax.ShapeDtypeStruct((kvh, window, rep * window), F32),
                   jax.ShapeDtypeStruct((kvh, rep, window), F32)],
        name="t5_bias",
    )(bm, bs, rel_bias)


def _attn_prompt_kernel(q_ref, kvc_ref, kvp_ref, bias_ref, sink_ref, ot_ref, *, kvh, rep, hd, scale):
    w = kvp_ref.shape[0]
    kv_all = kvc_ref[...]
    for sub in range(q_ref.shape[0] // w):
        rows = slice(sub * w, (sub + 1) * w)
        kvp = kvp_ref[...] if sub == 0 else kv_all[(sub - 1) * w:sub * w]
        first = pl.program_id(1) == 0 if sub == 0 else None
        _attn_prompt_block(q_ref[rows, :], kv_all[rows], kvp, first, bias_ref, sink_ref, ot_ref, sub * w,
                           kvh=kvh, rep=rep, hd=hd, scale=scale)


def _attn_prompt_block(q, kvc, kvp, first, bias_ref, sink_ref, ot_ref, col0, *, kvh, rep, hd, scale):
    w = q.shape[0]
    kvw = kvh * hd
    lane = lax.broadcasted_iota(jnp.int32, (1, kvw), 1)
    key = lax.broadcasted_iota(jnp.int32, (w, rep * w), 0)
    qry = lax.broadcasted_iota(jnp.int32, (w, rep * w), 1) % w
    upper = key > qry
    mask_add = None if first is None else jnp.where(jnp.logical_and(upper, first), NEG_INF, 0.0)
    log2e = math.log2(math.e)
    heads_per_blk = LANES // hd
    lane_blocks = kvw // LANES
    kbs = []
    for half in range(heads_per_blk):
        lmask = (lane % LANES) // hd == half
        kbs += [jnp.where(lmask, kvp[:, :kvw], 0.0), jnp.where(lmask, kvc[:, :kvw], 0.0)]
    qrows = []
    for p in range(lane_blocks):
        qg = jnp.concatenate([q[:, r * kvw + p * LANES:r * kvw + (p + 1) * LANES] for r in range(rep)], axis=0)
        zero = jnp.zeros_like(qg)
        qrows.append(jnp.concatenate([qg if other == p else zero for other in range(lane_blocks)], axis=1))
    s_full = lax.dot_general(jnp.concatenate(kbs, axis=0).astype(BF16), jnp.concatenate(qrows, axis=0),
                             (((1,), (1,)), ((), ())), preferred_element_type=F32)
    for p in range(lane_blocks):
        vs = slice(kvw + p * LANES, kvw + (p + 1) * LANES)
        vb = jnp.concatenate([kvp[:, vs], kvc[:, vs]], axis=0).astype(BF16)
        s_all = s_full[:, p * rep * w:(p + 1) * rep * w]
        pcats, denoms = [], []
        for half in range(heads_per_blk):
            g = p * heads_per_blk + half
            s = s_all[half * 2 * w:(half + 1) * 2 * w]
            bias = bias_ref[g] if first is None else bias_ref[g] + mask_add
            sc = jnp.where(upper, s[:w], s[w:]) * (scale * log2e) + bias
            sink = jnp.concatenate([jnp.full((1, w), sink_ref[g * rep + r] * log2e, F32) for r in range(rep)],
                                   axis=1)
            m = jnp.maximum(jnp.max(sc, axis=0, keepdims=True), sink)
            pe = jnp.exp2(sc - m)
            denoms.append(jnp.sum(pe, axis=0, keepdims=True) + jnp.exp2(sink - m))
            pcats.append(jnp.concatenate([jnp.where(upper, pe, 0.0), jnp.where(upper, 0.0, pe)], axis=0).astype(BF16))
        og_all = lax.dot_general(vb, jnp.concatenate(pcats, axis=1), (((0,), (0,)), ((), ())),
                                 preferred_element_type=F32)
        halves = [og_all[half * hd:(half + 1) * hd, half * rep * w:(half + 1) * rep * w] / denoms[half]
                  for half in range(heads_per_blk)]
        o_blk = jnp.concatenate(halves, axis=0).astype(BF16)
        for r in range(rep):
            ot_ref[r * kvw + p * LANES:r * kvw + (p + 1) * LANES, col0:col0 + w] = o_blk[:, r * w:(r + 1) * w]


def _attn_prompt(q, kv, bias, sinks, *, bsz, nblk, window, kvh, rep, hd):
    rows, nq = q.shape
    kvw2 = kv.shape[1]
    nsub = next(n for n in (4, 2, 1) if nblk % n == 0 and rows % (n * window) == 0)
    nstep = nblk // nsub
    kern = functools.partial(_attn_prompt_kernel, kvh=kvh, rep=rep, hd=hd, scale=1.0 / math.sqrt(hd))
    return pl.pallas_call(
        kern,
        grid=(bsz, nstep),
        in_specs=[pl.BlockSpec((nsub * window, nq), lambda b, i: (b * nstep + i, 0)),
                  pl.BlockSpec((nsub * window, kvw2), lambda b, i: (b * nstep + i, 0)),
                  pl.BlockSpec((window, kvw2), lambda b, i: (b * nblk + jnp.maximum(nsub * i - 1, 0), 0)),
                  _const_spec(bias.shape),
                  pl.BlockSpec(memory_space=pltpu.SMEM)],
        out_specs=pl.BlockSpec((nq, nsub * window), lambda b, i: (0, b * nstep + i)),
        out_shape=jax.ShapeDtypeStruct((nq, rows), BF16),
        compiler_params=_params("arbitrary", "arbitrary"),
        name="swa_prompt",
    )(q, kv, kv, bias, sinks)


def _attn_sample_kernel(q_ref, kv_ref, ck_ref, cv_ref, bias_ref, sink_ref, o_ref, nk_ref, nv_ref,
                        *, kvh, rep, hd, scale):
    nb, w, kvw = ck_ref.shape
    kv = kv_ref[...]
    lane = lax.broadcasted_iota(jnp.int32, (nb, rep, kvw), 2)

    def shifted(c_ref, n_ref, new):
        flat = c_ref[...].reshape(nb * w, kvw)
        n_ref[...] = pltpu.roll(flat, nb * w - 1, axis=0).reshape(nb, w, kvw)
        n_ref[:, w - 1:w, :] = new
        return n_ref[...].astype(BF16)

    nkb = shifted(ck_ref, nk_ref, kv[:, :, :kvw])
    nvb = shifted(cv_ref, nv_ref, kv[:, :, kvw:])
    q = q_ref[...].astype(F32)
    masks = [jnp.logical_and(lane >= g * hd, lane < (g + 1) * hd) for g in range(kvh)]
    qall = jnp.concatenate([jnp.where(masks[g], q, 0.0) for g in range(kvh)], axis=1).astype(BF16)
    s = jnp.einsum("nmc,njc->nmj", qall, nkb, preferred_element_type=F32)
    sc = s * scale + bias_ref[...][None]
    sink = sink_ref[...][None]
    m = jnp.maximum(jnp.max(sc, axis=-1, keepdims=True), sink)
    pe = jnp.exp(sc - m)
    probs = pe / (jnp.sum(pe, axis=-1, keepdims=True) + jnp.exp(sink - m))
    og = jnp.einsum("nmj,njc->nmc", probs.astype(BF16), nvb, preferred_element_type=F32)
    o = jnp.zeros((nb, rep, kvw), F32)
    for g in range(kvh):
        o = jnp.where(masks[g], og[:, g * rep:(g + 1) * rep, :], o)
    o_ref[...] = o.astype(BF16)


def _attn_sample(q3, kv3, ck, cv, bias_s, sink_s, *, nb, kvh, rep, hd):
    n, w, kvw = ck.shape
    kern = functools.partial(_attn_sample_kernel, kvh=kvh, rep=rep, hd=hd, scale=1.0 / math.sqrt(hd))
    cache_spec = pl.BlockSpec((nb, w, kvw), lambda i: (i, 0, 0))
    return pl.pallas_call(
        kern,
        grid=(n // nb,),
        in_specs=[pl.BlockSpec((nb, rep, kvw), lambda i: (i, 0, 0)),
                  pl.BlockSpec((nb, 1, 2 * kvw), lambda i: (i, 0, 0)),
                  cache_spec, cache_spec,
                  _const_spec(bias_s.shape), _const_spec(sink_s.shape)],
        out_specs=[pl.BlockSpec((nb, rep, kvw), lambda i: (i, 0, 0)), cache_spec, cache_spec],
        out_shape=[jax.ShapeDtypeStruct((n, rep, kvw), BF16),
                   jax.ShapeDtypeStruct((n, w, kvw), F32), jax.ShapeDtypeStruct((n, w, kvw), F32)],
        compiler_params=_params("arbitrary"),
        name="swa_sample",
    )(q3, kv3, ck, cv, bias_s, sink_s)


CHUNK = 16
BIG_COPY = 128
SUB = 256
UP_ROW_BLOCK = 1024
DOWN_ROW_BLOCK = 1024
TOKEN_TILE = 512
MOE_TILE = 512


def _chunk_copy(src, src_row, dst, dst_row, sem, rows=CHUNK):
    return pltpu.make_async_copy(src.at[pl.ds(pl.multiple_of(src_row, CHUNK), rows), :],
                                 dst.at[pl.ds(pl.multiple_of(dst_row, CHUNK), rows), :], sem)


def _start_copies(src, src_row, dst, dst_row, sem, nrows):
    nbig = lax.shift_right_logical(nrows, int(math.log2(BIG_COPY)))

    def big(c, carry):
        _chunk_copy(src, src_row + c * BIG_COPY, dst, dst_row + c * BIG_COPY, sem, BIG_COPY).start()
        return carry

    lax.fori_loop(0, nbig, big, 0)
    done = nbig * BIG_COPY

    def small(c, carry):
        _chunk_copy(src, src_row + done + c * CHUNK, dst, dst_row + done + c * CHUNK, sem).start()
        return carry

    lax.fori_loop(0, lax.shift_right_logical(nrows - done, int(math.log2(CHUNK))), small, 0)


def _moe_route_kernel(x2_ref, ot_ref, wo_ref, gffn_ref, wr_ref, lst_ref, ust_ref,
                      x3_ref, cm_ref, tmeta_ref, xs_hbm,
                      comp_s, carry_s, zero_s, base_v, pend_sm, sem, *, ne, cap, n_valid):
    i = pl.program_id(0)
    nt = pl.num_programs(0)
    tm = x2_ref.shape[0]
    w = comp_s.shape[1]
    slot = lax.rem(i, 2)

    @pl.when(i == 0)
    def _():
        base_v[...] = jnp.zeros_like(base_v)
        zero_s[...] = jnp.zeros_like(zero_s)
        carry_s[...] = jnp.zeros_like(carry_s)

    def drain(n):
        def body(c, carry):
            _chunk_copy(zero_s, 0, xs_hbm, 0, sem).wait()
            return carry
        lax.fori_loop(0, n, body, 0)

    @pl.when(i > 0)
    def _():
        drain(pend_sm[0])

    x3 = x2_ref[...] + lax.dot_general(ot_ref[...], wo_ref[...], (((0,), (0,)), ((), ())),
                                       preferred_element_type=F32)
    x3_ref[...] = x3
    hb = _rms(x3, gffn_ref[...]).astype(BF16)

    lane = lax.broadcasted_iota(jnp.int32, (tm, LANES), 1)
    logits = jnp.where(lane < ne, _dot(hb, wr_ref[...]), -jnp.inf)
    m1 = jnp.max(logits, axis=-1, keepdims=True)
    i1 = jnp.min(jnp.where(logits == m1, lane, LANES), axis=-1, keepdims=True)
    rest = jnp.where(lane == i1, -jnp.inf, logits)
    m2 = jnp.max(rest, axis=-1, keepdims=True)
    i2 = jnp.min(jnp.where(rest == m2, lane, LANES), axis=-1, keepdims=True)
    e2 = jnp.exp(m2 - m1)
    g1 = 1.0 / (1.0 + e2)
    g2 = e2 / (1.0 + e2)

    live = i * tm + lax.broadcasted_iota(jnp.int32, (tm, 1), 0) < n_valid
    sel = jnp.where(jnp.logical_and(live, jnp.logical_or(lane == i1, lane == i2)), 1.0, 0.0)
    rank = _dot(lst_ref[...], sel.astype(BF16))
    cnt = jnp.sum(sel, axis=0, keepdims=True)
    fill = base_v[...]
    rem = fill - jnp.floor(fill / CHUNK) * CHUNK
    cpad = jnp.floor((rem + cnt + (CHUNK - 1)) / CHUNK) * CHUNK
    loff = _dot(jnp.broadcast_to(cpad, (8, LANES)).astype(BF16), ust_ref[...])[0:1]
    dest = loff + rem + rank
    ld1 = jnp.where(live, jnp.sum(jnp.where(lane == i1, dest, 0.0), axis=-1, keepdims=True), -1.0)
    ld2 = jnp.where(live, jnp.sum(jnp.where(lane == i2, dest, 0.0), axis=-1, keepdims=True), -1.0)
    cm = jnp.where(lane == 0, ld1, jnp.where(lane == 1, ld2, jnp.where(lane == 2, g1, jnp.where(lane == 3, g2, 0.0))))
    cm_ref[...] = cm

    rm = cm.T
    rowi = lax.broadcasted_iota(jnp.int32, (w, tm), 0).astype(F32)
    place = (jnp.where(rowi == rm[0:1], 1.0, 0.0) + jnp.where(rowi == rm[1:2], 1.0, 0.0)).astype(BF16)
    comp_s[slot] = _dot(place, hb).astype(BF16)

    base = fill - rem
    srow = lax.broadcasted_iota(jnp.int32, (8, LANES), 0)
    tmeta_ref[...] = jnp.where(srow == 0, cpad, jnp.where(srow == 1, base, jnp.where(srow == 2, fill + cnt, 0.0))
                               ).astype(jnp.int32)
    base_v[...] = fill + cnt

    cpad_i = cpad.astype(jnp.int32)
    loff_i = loff.astype(jnp.int32)
    base_i = base.astype(jnp.int32)
    tail_i = (rem + cnt - jnp.floor((rem + cnt) / CHUNK) * CHUNK).astype(jnp.int32)
    total = 0
    tails = []
    for e in range(ne):
        n_e = cpad_i[0, e]
        src0 = loff_i[0, e]
        dst0 = base_i[0, e] + e * cap

        @pl.when(n_e > 0)
        def _(e=e, n_e=n_e, src0=src0, partial=tail_i[0, e] > 0):
            head = pl.ds(pl.multiple_of(src0, CHUNK), CHUNK)
            comp_s[slot, head, :] = comp_s[slot, head, :] + carry_s[e]
            last = comp_s[slot, pl.ds(pl.multiple_of(src0 + n_e - CHUNK, CHUNK), CHUNK), :]
            carry_s[e] = jnp.where(partial, last, jnp.zeros_like(last))

        _start_copies(comp_s.at[slot], src0, xs_hbm, dst0, sem, n_e)
        total = total + lax.shift_right_logical(n_e, int(math.log2(CHUNK)))
        tails.append(dst0 + n_e)
    pend_sm[0] = total

    @pl.when(i == nt - 1)
    def _():
        nz_total = 0
        for e in range(ne):
            end = tails[e]
            nz = lax.shift_right_logical(lax.rem(SUB - lax.rem(end, SUB), SUB), int(math.log2(CHUNK)))

            def zstart(c, carry, end=end):
                _chunk_copy(zero_s, 0, xs_hbm, end + c * CHUNK, sem).start()
                return carry

            lax.fori_loop(0, nz, zstart, 0)
            nz_total = nz_total + nz
        drain(total + nz_total)


def _moe_route(x2, ot, wo, gffn, wr_pad, *, tm, ne, cap, n_valid):
    rows, d = x2.shape
    nt = rows // tm
    w = _staging_rows(tm, ne)
    lst = jnp.asarray(np.tril(np.ones((tm, tm), np.float32), -1), BF16)
    ust = jnp.asarray(np.triu(np.ones((LANES, LANES), np.float32), 1), BF16)
    kern = functools.partial(_moe_route_kernel, ne=ne, cap=cap, n_valid=n_valid)
    return pl.pallas_call(
        kern,
        grid=(nt,),
        in_specs=[pl.BlockSpec((tm, d), lambda i: (i, 0)),
                  pl.BlockSpec((ot.shape[0], tm), lambda i: (0, i)),
                  _const_spec(wo.shape), _const_spec(gffn.shape), _const_spec(wr_pad.shape),
                  _const_spec(lst.shape), _const_spec(ust.shape)],
        out_specs=[pl.BlockSpec((tm, d), lambda i: (i, 0)),
                   pl.BlockSpec((tm, LANES), lambda i: (i, 0)),
                   pl.BlockSpec((8, LANES), lambda i: (i, 0)),
                   pl.BlockSpec(memory_space=pl.ANY)],
        out_shape=[jax.ShapeDtypeStruct((rows, d), F32),
                   jax.ShapeDtypeStruct((rows, LANES), F32),
                   jax.ShapeDtypeStruct((nt * 8, LANES), jnp.int32),
                   jax.ShapeDtypeStruct((ne * cap, d), BF16)],
        scratch_shapes=[pltpu.VMEM((2, w, d), BF16), pltpu.VMEM((ne, CHUNK, d), BF16), pltpu.VMEM((CHUNK, d), BF16),
                        pltpu.VMEM((1, LANES), F32), pltpu.SMEM((1,), jnp.int32),
                        pltpu.SemaphoreType.DMA(())],
        compiler_params=_params("arbitrary"),
        name="moe_route",
    )(x2, ot, wo, gffn, wr_pad, lst, ust)


def _staging_rows(tm, ne):
    return -(-(TOP_K * tm + 2 * ne * (CHUNK - 1)) // LANES) * LANES


def _max_blocks(rows, ne, row_block):
    return TOP_K * rows // row_block + ne


def _expert_steps(nrows, nf, max_blocks, row_block):
    ne = nrows.shape[0]
    nblk = (nrows + row_block - 1) // row_block
    cum = jnp.cumsum(nblk) * nf
    total = cum[-1]
    s = jnp.minimum(jnp.arange(max_blocks * nf, dtype=jnp.int32), total - 1)
    e = jnp.minimum(jnp.sum((s[:, None] >= cum[None, :]).astype(jnp.int32), axis=1), ne - 1)
    nb_e = nblk[e]
    within = s - (cum[e] - nb_e * nf)
    f = within // nb_e
    pos = within - f * nb_e
    r = (pos + nb_e - 1) % nb_e
    nsub = jnp.clip((nrows[e] - r * row_block + SUB - 1) // SUB, 0, row_block // SUB)
    first = (pos == 0).astype(jnp.int32)
    return (e, f.astype(jnp.int32), r.astype(jnp.int32), first, nsub.astype(jnp.int32),
            jnp.reshape(total, (1,)).astype(jnp.int32))


def _moe_up_kernel(e_ref, f_ref, r_ref, first_ref, nsub_ref, n_ref, xs_ref, wg_ref, wu_ref, act_ref, wgb_s, wub_s):
    s = pl.program_id(0)

    @pl.when(s < n_ref[0])
    def _():
        @pl.when(first_ref[s] == 1)
        def _():
            wgb_s[...] = wg_ref[...].astype(BF16)
            wub_s[...] = wu_ref[...].astype(BF16)

        def gate_up(row0, nrows):
            rows = pl.ds(pl.multiple_of(row0, SUB), nrows)
            xsb = xs_ref[rows, :]
            a = _dot(xsb, wgb_s[...])
            b = _dot(xsb, wub_s[...])
            act_ref[rows, :] = (jax.nn.silu(a) * b).astype(BF16)

        nsub = nsub_ref[s]
        npair = lax.shift_right_logical(nsub, 1)

        def body(j, carry):
            gate_up(j * (2 * SUB), 2 * SUB)
            return carry

        lax.fori_loop(0, npair, body, 0)

        @pl.when(nsub > 2 * npair)
        def _():
            gate_up(npair * (2 * SUB), SUB)


def _moe_up(steps, xs, wgu, *, fc, cap, row_block):
    ne, d, dff2 = wgu.shape
    dff = dff2 // 2
    nf = dff // fc
    cb = cap // row_block
    grid_spec = pltpu.PrefetchScalarGridSpec(
        num_scalar_prefetch=6,
        grid=(steps[0].shape[0],),
        in_specs=[pl.BlockSpec((row_block, d), lambda s, e, f, r, *_: (e[s] * cb + r[s], 0)),
                  pl.BlockSpec((None, d, fc), lambda s, e, f, r, *_: (e[s], 0, f[s])),
                  pl.BlockSpec((None, d, fc), lambda s, e, f, r, *_: (e[s], 0, nf + f[s]))],
        out_specs=pl.BlockSpec((row_block, fc), lambda s, e, f, r, *_: (e[s] * cb + r[s], f[s])),
        scratch_shapes=[pltpu.VMEM((d, fc), BF16), pltpu.VMEM((d, fc), BF16)])
    return pl.pallas_call(
        _moe_up_kernel,
        grid_spec=grid_spec,
        out_shape=jax.ShapeDtypeStruct((ne * cap, dff), BF16),
        compiler_params=_params("arbitrary"),
        name="moe_up",
    )(*steps, xs, wgu, wgu)


def _moe_down_kernel(e_ref, f_ref, r_ref, first_ref, nsub_ref, n_ref, act_ref, wd_ref, ys_ref, wdb_s):
    s = pl.program_id(0)

    @pl.when(s < n_ref[0])
    def _():
        @pl.when(first_ref[s] == 1)
        def _():
            wdb_s[...] = wd_ref[...].astype(BF16)

        def project(row0, nrows):
            rows = pl.ds(pl.multiple_of(row0, SUB), nrows)
            ys_ref[rows, :] = _dot(act_ref[rows, :], wdb_s[...]).astype(BF16)

        nsub = nsub_ref[s]
        npair = lax.shift_right_logical(nsub, 1)

        def body(j, carry):
            project(j * (2 * SUB), 2 * SUB)
            return carry

        lax.fori_loop(0, npair, body, 0)

        @pl.when(nsub > 2 * npair)
        def _():
            project(npair * (2 * SUB), SUB)


def _moe_down(steps, act, wd, *, cap, row_block):
    ne, dff, d = wd.shape
    cb = cap // row_block
    grid_spec = pltpu.PrefetchScalarGridSpec(
        num_scalar_prefetch=6,
        grid=(steps[0].shape[0],),
        in_specs=[pl.BlockSpec((row_block, dff), lambda s, e, f, r, *_: (e[s] * cb + r[s], 0)),
                  pl.BlockSpec((None, dff, d), lambda s, e, f, r, *_: (e[s], 0, 0))],
        out_specs=pl.BlockSpec((row_block, d), lambda s, e, f, r, *_: (e[s] * cb + r[s], 0)),
        scratch_shapes=[pltpu.VMEM((dff, d), BF16)])
    return pl.pallas_call(
        _moe_down_kernel,
        grid_spec=grid_spec,
        out_shape=jax.ShapeDtypeStruct((ne * cap, d), BF16),
        compiler_params=_params("arbitrary", vmem=2 * (dff * d * 4 + row_block * (dff + d) * 2) + dff * d * 2
                                + 4 * SUB * d * 4),
        name="moe_down",
    )(*steps, act, wd)


def _moe_combine_kernel(cpad_ref, seg_ref, x3_ref, cm_ref, gfin_ref, ys_hbm, yp_ref, ys_ref, yloc_s, sem,
                        *, ne, npt):
    i = pl.program_id(0)
    nt = pl.num_programs(0)
    tm = x3_ref.shape[0]
    w = yloc_s.shape[1]
    slot = lax.rem(i, 2)
    shift = int(math.log2(CHUNK))

    def issue(t, sl):
        off = 0
        for e in range(ne):
            n_e = cpad_ref[t * ne + e]
            _start_copies(ys_hbm, seg_ref[t * ne + e], yloc_s.at[sl], off, sem.at[sl], n_e)
            off = off + n_e

    @pl.when(i == 0)
    def _():
        yloc_s[...] = jnp.zeros_like(yloc_s)
        issue(0, 0)

    @pl.when(i + 1 < nt)
    def _():
        issue(i + 1, 1 - slot)

    total = 0
    for e in range(ne):
        total = total + lax.shift_right_logical(cpad_ref[i * ne + e], shift)

    def wait(c, carry):
        _chunk_copy(ys_hbm, 0, yloc_s.at[slot], 0, sem.at[slot]).wait()
        return carry

    lax.fori_loop(0, total, wait, 0)

    cm = cm_ref[...]
    col = lax.broadcasted_iota(jnp.int32, (tm, w), 1).astype(F32)
    yl = yloc_s[slot]
    mix = jnp.where(col == cm[:, 0:1], cm[:, 2:3], jnp.where(col == cm[:, 1:2], cm[:, 3:4], 0.0))
    y = _rms(x3_ref[...] + _dot(mix.astype(BF16), yl), gfin_ref[...])

    @pl.when(i < npt)
    def _():
        yp_ref[...] = y

    @pl.when(i >= npt)
    def _():
        ys_ref[...] = y


def _moe_combine(cpad, seg, x3, cm, gfin, ys, *, tm, ne, sample_rows):
    rows, d = x3.shape
    nt = rows // tm
    npt = (rows - sample_rows) // tm
    w = _staging_rows(tm, ne)
    kern = functools.partial(_moe_combine_kernel, ne=ne, npt=npt)
    grid_spec = pltpu.PrefetchScalarGridSpec(
        num_scalar_prefetch=2,
        grid=(nt,),
        in_specs=[pl.BlockSpec((tm, d), lambda i, *_: (i, 0)),
                  pl.BlockSpec((tm, LANES), lambda i, *_: (i, 0)),
                  pl.BlockSpec(gfin.shape, lambda i, *_: (0, 0)),
                  pl.BlockSpec(memory_space=pl.ANY)],
        out_specs=[pl.BlockSpec((tm, d), lambda i, *_: (jnp.minimum(i, npt - 1), 0)),
                   pl.BlockSpec((tm, d), lambda i, *_: (jnp.maximum(i - npt, 0), 0))],
        scratch_shapes=[pltpu.VMEM((2, w, d), BF16), pltpu.SemaphoreType.DMA((2,))])
    return pl.pallas_call(
        kern,
        grid_spec=grid_spec,
        out_shape=[jax.ShapeDtypeStruct((npt * tm, d), F32), jax.ShapeDtypeStruct((sample_rows, d), F32)],
        compiler_params=_params("arbitrary"),
        name="moe_combine",
    )(cpad, seg, x3, cm, gfin, ys)


def _moe(x2, ot, wo, gffn, wr, wgu, wd, gfin, *, tm, fc, n_valid, sample_rows):
    rows, d = x2.shape
    ne = wr.shape[1]
    nt = rows // tm
    nf = wgu.shape[2] // 2 // fc
    cap = -(-(rows + SUB) // UP_ROW_BLOCK) * UP_ROW_BLOCK
    wr_pad = jnp.pad(wr, ((0, 0), (0, LANES - ne))).astype(BF16)
    x3, cm, tmeta, xs = _moe_route(x2, ot, wo, gffn, wr_pad, tm=tm, ne=ne, cap=cap, n_valid=n_valid)
    tmeta = tmeta.reshape(nt, 8, LANES)[:, :, :ne]
    cpad = tmeta[:, 0, :].reshape(nt * ne)
    seg = (tmeta[:, 1, :] + jnp.arange(ne, dtype=jnp.int32)[None, :] * cap).reshape(nt * ne)
    nrows = tmeta[nt - 1, 2, :]
    up_steps = _expert_steps(nrows, nf, _max_blocks(rows, ne, UP_ROW_BLOCK), UP_ROW_BLOCK)
    act = _moe_up(up_steps, xs, wgu, fc=fc, cap=cap, row_block=UP_ROW_BLOCK)
    down_steps = _expert_steps(nrows, 1, _max_blocks(rows, ne, DOWN_ROW_BLOCK), DOWN_ROW_BLOCK)
    ys = _moe_down(down_steps, act, wd, cap=cap, row_block=DOWN_ROW_BLOCK)
    return _moe_combine(cpad, seg, x3, cm, gfin, ys, tm=tm, ne=ne, sample_rows=sample_rows)


def kernel(x_prompt, x_sample, state_ssm_re, state_ssm_im, cache_k_win, cache_v_win, g_mix, g_ffn, g_kv, g_final, ssm_a_re, ssm_a_im, ssm_log_dt, ssm_b_re, ssm_b_im, ssm_c_re, ssm_c_im, ssm_d, w_glu, w_kv, w_q, w_o, attn_sinks, rel_bias, w_ffn_gate_up, w_ffn_down, w_router, w_exp_gate_up, w_exp_down):
    bsz, seq, d = x_prompt.shape
    ns, dec_seq, _ = x_sample.shape
    assert dec_seq == 1 and g_mix.shape[0] == 2 and ssm_a_re.shape[0] == 1 and w_q.shape[0] == 1
    _, g, p = ssm_a_re.shape
    gp = g * p
    window, kvh, hd = cache_k_win.shape[1:]
    kvw = kvh * hd
    nh = attn_sinks.shape[1]
    rep = nh // kvh
    nq = nh * hd
    assert bsz == 8 and ns % bsz == 0 and seq % window == 0 and LANES % hd == 0

    lam_re, lam_im, wb_re, wb_im, wc_re, wc_imn = _zoh(ssm_a_re[0], ssm_a_im[0], ssm_log_dt[0],
                                                        ssm_b_re[0], ssm_b_im[0], ssm_c_re[0], ssm_c_im[0])
    wglu = w_glu[0].astype(BF16)
    d_skip = ssm_d[0].reshape(1, d)
    wgu = w_ffn_gate_up[0].astype(BF16)
    wd = w_ffn_down[0].astype(BF16)
    wkv = w_kv.astype(BF16)
    wq = w_q[0].reshape(d, kvh, rep, hd).transpose(0, 2, 1, 3).reshape(d, nq).astype(BF16)
    wo = w_o[0].reshape(kvh, rep, hd, d).transpose(1, 0, 2, 3).reshape(nq, d).astype(BF16)
    bias, bias_s = _bias_tables(rel_bias, window, kvh, rep)
    bias_s = bias_s.reshape(nh, window)
    sinks = attn_sinks[0]
    sink_s = sinks.reshape(nh, 1)

    tm = TOKEN_TILE
    npr = seq * bsz
    assert npr % tm == 0 and ns <= tm

    x1_p, st_re, st_im = _ssm_prompt(x_prompt, g_mix[0:1], lam_re, lam_im, wb_re, wb_im, wc_re, wc_imn,
                                     d_skip, wglu, lc=32, nsub=2)
    x1_s, hs_re, hs_im = _ssm_sample(x_sample.reshape(ns, d), g_mix[0:1], lam_re, lam_im,
                                     state_ssm_re[0].reshape(ns, gp), state_ssm_im[0].reshape(ns, gp),
                                     wb_re, wb_im, wc_re, wc_imn, d_skip, wglu, pad_rows=tm)

    x2, kv, q = _ffn(x1_p.reshape(npr, d), x1_s, g_ffn[0:1], wgu, wd, g_kv.reshape(1, d), wkv, g_mix[1:2], wq)

    ot = _attn_prompt(q, kv, bias, sinks, bsz=bsz, nblk=seq // window, window=window, kvh=kvh, rep=rep, hd=hd)
    o_s, nk_s, nv_s = _attn_sample(q[npr:npr + ns].reshape(ns, rep, kvw), kv[npr:npr + ns].reshape(ns, 1, 2 * kvw),
                                   cache_k_win.reshape(ns, window, kvw), cache_v_win.reshape(ns, window, kvw),
                                   bias_s, sink_s, nb=32, kvh=kvh, rep=rep, hd=hd)
    ot = lax.dynamic_update_slice(ot, jnp.pad(o_s.reshape(ns, nq).T, ((0, 0), (0, tm - ns))), (0, npr))

    y_p, y_s = _moe(x2, ot, wo, g_ffn[1:2], w_router[0], w_exp_gate_up[0], w_exp_down[0], g_final.reshape(1, d),
                    tm=MOE_TILE, fc=w_exp_down.shape[2] // 2, n_valid=npr + ns, sample_rows=tm)

    y_prompt = y_p.reshape(bsz, seq, d)
    y_sample = y_s[:ns].reshape(ns, 1, d)
    kv_tail = jnp.stack([kv[(b + 1) * seq - window:(b + 1) * seq] for b in range(bsz)])
    kv_tail = kv_tail.reshape(bsz, window, 2, kvh, hd).transpose(2, 0, 1, 3, 4)
    return (y_prompt, y_sample,
            st_re.reshape(1, bsz, g, p), st_im.reshape(1, bsz, g, p), kv_tail[0], kv_tail[1],
            hs_re.reshape(1, ns, g, p), hs_im.reshape(1, ns, g, p),
            nk_s.reshape(ns, window, kvh, hd), nv_s.reshape(ns, window, kvh, hd))
```

```python
import functools
import math

import numpy as np
import jax
import jax.numpy as jnp
from jax import lax
from jax.experimental import pallas as pl
from jax.experimental.pallas import tpu as pltpu

F32 = jnp.float32
BF16 = jnp.bfloat16

EPS = 1e-6
NEG_INF = -1e30
TOP_K = 2
MAX_DISTANCE = 128
MXU_DIM = 256
LANES = 128
VMEM_LIMIT_BYTES = 56 * 1024 * 1024


def _dot(a, b):
    return jnp.dot(a, b, preferred_element_type=F32)


def _rms(x, g):
    return x * lax.rsqrt(jnp.mean(x * x, axis=-1, keepdims=True) + EPS) * g


def _const_spec(shape):
    nd = len(shape)
    return pl.BlockSpec(shape, lambda *_: (0,) * nd)


def _params(*sem, vmem=VMEM_LIMIT_BYTES):
    return pltpu.CompilerParams(dimension_semantics=sem, vmem_limit_bytes=vmem)


def _zoh_kernel(a_re_ref, a_im_ref, log_dt_ref, b_re_ref, b_im_ref, ct_re_ref, ct_im_ref,
                lam_re_ref, lam_im_ref, wb_re_ref, wb_im_ref, wc_re_ref, wc_imn_ref, *, hch, p):
    a_re = a_re_ref[...]
    a_im = a_im_ref[...]
    dt = jnp.exp(log_dt_ref[...])
    mag = jnp.exp(a_re * dt)
    lr = mag * jnp.cos(a_im * dt)
    li = mag * jnp.sin(a_im * dt)
    lam_re_ref[...] = lr
    lam_im_ref[...] = li
    nr = lr - 1.0
    den = a_re * a_re + a_im * a_im
    qr = (nr * a_re + li * a_im) / den
    qi = (li * a_re - nr * a_im) / den
    b_re = b_re_ref[...]
    b_im = b_im_ref[...]
    bb_re = qr * b_re - qi * b_im
    bb_im = qr * b_im + qi * b_re

    nkb, ublk, sblk = wb_re_ref.shape
    gpb = ublk // hch
    sh_h, sh_p = int(math.log2(hch)), int(math.log2(p))
    row = lax.broadcasted_iota(jnp.int32, (ublk, sblk), 0)
    col = lax.broadcasted_iota(jnp.int32, (ublk, sblk), 1)
    diag_in = lax.shift_right_logical(row, sh_h) == lax.shift_right_logical(col, sh_p)
    row = lax.broadcasted_iota(jnp.int32, (sblk, ublk), 0)
    col = lax.broadcasted_iota(jnp.int32, (sblk, ublk), 1)
    diag_out = lax.shift_right_logical(row, sh_p) == lax.shift_right_logical(col, sh_h)
    spread = jnp.where(lax.broadcasted_iota(jnp.int32, (hch, ublk), 0)
                       == jnp.bitwise_and(lax.broadcasted_iota(jnp.int32, (hch, ublk), 1), hch - 1),
                       1.0, 0.0).astype(BF16)
    for kb in range(nkb):
        cs = slice(kb * sblk, (kb + 1) * sblk)
        wb_re_ref[kb] = jnp.where(diag_in, jnp.concatenate([bb_re[:, cs]] * gpb, axis=0), 0.0).astype(BF16)
        wb_im_ref[kb] = jnp.where(diag_in, jnp.concatenate([bb_im[:, cs]] * gpb, axis=0), 0.0).astype(BF16)
        c_re = _dot(ct_re_ref[cs, :].astype(BF16), spread)
        c_im = _dot(ct_im_ref[cs, :].astype(BF16), spread)
        wc_re_ref[kb] = jnp.where(diag_out, c_re, 0.0).astype(BF16)
        wc_imn_ref[kb] = jnp.where(diag_out, -c_im, 0.0).astype(BF16)


def _zoh(a_re, a_im, log_dt, b_re, b_im, c_re, c_im):
    g, p = a_re.shape
    h = b_re.shape[-1]
    gp = g * p
    gpb = MXU_DIM // h
    nkb = g // gpb
    assert h & (h - 1) == 0 and p & (p - 1) == 0 and g % gpb == 0
    row = jax.ShapeDtypeStruct((1, gp), F32)
    wb = jax.ShapeDtypeStruct((nkb, gpb * h, gpb * p), BF16)
    wc = jax.ShapeDtypeStruct((nkb, gpb * p, gpb * h), BF16)
    return pl.pallas_call(
        functools.partial(_zoh_kernel, hch=h, p=p),
        out_shape=(row, row, wb, wb, wc, wc),
        name="s5_zoh",
    )(a_re.reshape(1, gp), a_im.reshape(1, gp), jnp.repeat(log_dt, p).reshape(1, gp),
      b_re.transpose(2, 0, 1).reshape(h, gp), b_im.transpose(2, 0, 1).reshape(h, gp),
      c_re.transpose(0, 2, 1).reshape(gp, h), c_im.transpose(0, 2, 1).reshape(gp, h))


def _ssm_in_proj(ub, wb_re_ref, wb_im_ref, bu_re, bu_im):
    nkb, ublk, sblk = wb_re_ref.shape
    for kb in range(nkb):
        ukb = ub[:, kb * ublk:(kb + 1) * ublk]
        bu_re[:, kb * sblk:(kb + 1) * sblk] = _dot(ukb, wb_re_ref[kb])
        bu_im[:, kb * sblk:(kb + 1) * sblk] = _dot(ukb, wb_im_ref[kb])


def _ssm_out_proj(h_re, h_im, wc_re_ref, wc_imn_ref):
    nkb, sblk, _ = wc_re_ref.shape
    ys = []
    for kb in range(nkb):
        hr = h_re[:, kb * sblk:(kb + 1) * sblk].astype(BF16)
        hi = h_im[:, kb * sblk:(kb + 1) * sblk].astype(BF16)
        ys.append(_dot(hr, wc_re_ref[kb]) + _dot(hi, wc_imn_ref[kb]))
    return jnp.concatenate(ys, axis=1)


def _ssm_glu(x, u, y, d_ref, wglu_ref):
    z = jax.nn.gelu(y + d_ref[...] * u).astype(BF16)
    gl = _dot(z, wglu_ref[...])
    d = x.shape[1]
    return x + gl[:, :d] * jax.nn.sigmoid(gl[:, d:])


def _split3(v):
    hi = v.astype(BF16)
    r1 = v - hi.astype(F32)
    mid = r1.astype(BF16)
    lo = (r1 - mid.astype(F32)).astype(BF16)
    return hi, mid, lo


def _ssm_prompt_kernel(x_ref, g_ref, lam_re_ref, lam_im_ref, wb_re_ref, wb_im_ref,
                       wc_re_ref, wc_imn_ref, d_ref, wglu_ref, perm_ref, permt_ref,
                       out_ref, st_re_ref, st_im_ref, bu_re_all, bu_im_all, *, lc, bsz, lane_chunk):
    @pl.when(pl.program_id(0) == 0)
    def _():
        st_re_ref[...] = jnp.zeros_like(st_re_ref)
        st_im_ref[...] = jnp.zeros_like(st_im_ref)

    for sub in range(bu_re_all.shape[0]):
        ts = slice(sub * lc, (sub + 1) * lc)
        _ssm_prompt_chunk(x_ref[:, ts, :], g_ref, lam_re_ref, lam_im_ref, wb_re_ref, wb_im_ref,
                          wc_re_ref, wc_imn_ref, d_ref, wglu_ref, perm_ref, permt_ref,
                          out_ref.at[:, ts, :], st_re_ref, st_im_ref, bu_re_all.at[sub], bu_im_all.at[sub],
                          lc=lc, bsz=bsz, lane_chunk=lane_chunk)


def _ssm_prompt_chunk(x3, g_ref, lam_re_ref, lam_im_ref, wb_re_ref, wb_im_ref,
                      wc_re_ref, wc_imn_ref, d_ref, wglu_ref, perm_ref, permt_ref,
                      out_ref, st_re_ref, st_im_ref, bu_re, bu_im, *, lc, bsz, lane_chunk):
    d = x3.shape[2]
    x = x3.reshape(bsz * lc, d)
    u = _rms(x, g_ref[...])
    ub_tb = _dot(perm_ref[...], u.astype(BF16)).astype(BF16)
    _ssm_in_proj(ub_tb, wb_re_ref, wb_im_ref, bu_re, bu_im)

    gp = bu_re.shape[1]
    for c0 in range(0, gp, lane_chunk):
        sl = slice(c0, c0 + lane_chunk)
        lre = jnp.broadcast_to(lam_re_ref[:, sl], (bsz, lane_chunk))
        lim = jnp.broadcast_to(lam_im_ref[:, sl], (bsz, lane_chunk))
        hr = st_re_ref[:, sl]
        hi = st_im_ref[:, sl]
        for t in range(lc):
            rows = slice(t * bsz, (t + 1) * bsz)
            hr, hi = (lre * hr - lim * hi + bu_re[rows, sl], lre * hi + lim * hr + bu_im[rows, sl])
            bu_re[rows, sl] = hr
            bu_im[rows, sl] = hi
        st_re_ref[:, sl] = hr
        st_im_ref[:, sl] = hi

    permt = permt_ref[...]
    hi, mid, lo = _split3(_ssm_out_proj(bu_re, bu_im, wc_re_ref, wc_imn_ref))
    y = (_dot(permt, hi) + _dot(permt, mid)) + _dot(permt, lo)
    out_ref[...] = _ssm_glu(x, u, y, d_ref, wglu_ref).reshape(bsz, lc, d)


def _ssm_sample_kernel(x_ref, g_ref, lam_re_ref, lam_im_ref, h0_re_ref, h0_im_ref, wb_re_ref, wb_im_ref,
                       wc_re_ref, wc_imn_ref, d_ref, wglu_ref,
                       out_ref, h_re_ref, h_im_ref):
    x = x_ref[...]
    u = _rms(x, g_ref[...])
    _ssm_in_proj(u.astype(BF16), wb_re_ref, wb_im_ref, h_re_ref, h_im_ref)
    lre = lam_re_ref[...]
    lim = lam_im_ref[...]
    h0r = h0_re_ref[...]
    h0i = h0_im_ref[...]
    h_re_ref[...] = lre * h0r - lim * h0i + h_re_ref[...]
    h_im_ref[...] = lre * h0i + lim * h0r + h_im_ref[...]
    y = _ssm_out_proj(h_re_ref, h_im_ref, wc_re_ref, wc_imn_ref)
    n = x.shape[0]
    out_ref[0:n, :] = _ssm_glu(x, u, y, d_ref, wglu_ref)
    out_ref[n:, :] = jnp.zeros((out_ref.shape[0] - n, out_ref.shape[1]), F32)


def _ssm_prompt(x, g, lam_re, lam_im, wb_re, wb_im, wc_re, wc_imn, d_skip, wglu, *, lc, nsub):
    bsz, seq, d = x.shape
    assert seq % (lc * nsub) == 0
    gp = lam_re.shape[1]
    r = lc * bsz
    perm = np.zeros((r, r), np.float32)
    for b in range(bsz):
        for t in range(lc):
            perm[t * bsz + b, b * lc + t] = 1.0
    permt = jnp.asarray(perm.T, BF16)
    perm = jnp.asarray(perm, BF16)
    kern = functools.partial(_ssm_prompt_kernel, lc=lc, bsz=bsz, lane_chunk=8 * LANES)
    return pl.pallas_call(
        kern,
        grid=(seq // (lc * nsub),),
        in_specs=[pl.BlockSpec((bsz, lc * nsub, d), lambda c: (0, c, 0)),
                  _const_spec(g.shape), _const_spec(lam_re.shape), _const_spec(lam_im.shape),
                  _const_spec(wb_re.shape), _const_spec(wb_im.shape),
                  _const_spec(wc_re.shape), _const_spec(wc_imn.shape),
                  _const_spec(d_skip.shape), _const_spec(wglu.shape),
                  _const_spec(perm.shape), _const_spec(permt.shape)],
        out_specs=[pl.BlockSpec((bsz, lc * nsub, d), lambda c: (0, c, 0)),
                   _const_spec((bsz, gp)), _const_spec((bsz, gp))],
        out_shape=[jax.ShapeDtypeStruct((bsz, seq, d), F32),
                   jax.ShapeDtypeStruct((bsz, gp), F32), jax.ShapeDtypeStruct((bsz, gp), F32)],
        scratch_shapes=[pltpu.VMEM((nsub, r, gp), F32), pltpu.VMEM((nsub, r, gp), F32)],
        compiler_params=_params("arbitrary"),
        name="s5_prompt",
    )(x, g, lam_re, lam_im, wb_re, wb_im, wc_re, wc_imn, d_skip, wglu, perm, permt)


def _ssm_sample(x, g, lam_re, lam_im, h0_re, h0_im, wb_re, wb_im, wc_re, wc_imn, d_skip, wglu, *, pad_rows):
    n, d = x.shape
    gp = lam_re.shape[1]
    return pl.pallas_call(
        _ssm_sample_kernel,
        out_shape=[jax.ShapeDtypeStruct((pad_rows, d), F32),
                   jax.ShapeDtypeStruct((n, gp), F32), jax.ShapeDtypeStruct((n, gp), F32)],
        compiler_params=_params(),
        name="s5_sample",
    )(x, g, lam_re, lam_im, h0_re, h0_im, wb_re, wb_im, wc_re, wc_imn, d_skip, wglu)


def _ffn_chunks(d_ff):
    step = 3 * MXU_DIM
    return [(c, min(c + step, d_ff)) for c in range(0, d_ff, step)]


def _ffn_kernel(xp_ref, xs_ref, gffn_ref, wgu_ref, wd_ref, gkv_ref, wkv_ref, gq_ref, wq_ref,
                x2_ref, kv_ref, q_ref):
    x = jnp.where(pl.program_id(0) < pl.num_programs(0) - 1, xp_ref[...], xs_ref[...])
    hb = _rms(x, gffn_ref[...]).astype(BF16)
    d_ff = wd_ref.shape[0]
    acc = None
    for c0, c1 in _ffn_chunks(d_ff):
        a = _dot(hb, wgu_ref[:, c0:c1])
        b = _dot(hb, wgu_ref[:, d_ff + c0:d_ff + c1])
        part = _dot((jax.nn.silu(a) * b).astype(BF16), wd_ref[c0:c1, :])
        acc = part if acc is None else acc + part
    x2 = x + acc
    x2_ref[...] = x2
    kv_ref[...] = _dot(_rms(x2, gkv_ref[...]).astype(BF16), wkv_ref[...])
    q_ref[...] = _dot(_rms(x2, gq_ref[...]).astype(BF16), wq_ref[...]).astype(BF16)


def _ffn(xp, xs, gffn, wgu, wd, gkv, wkv, gq, wq):
    tm, d = xs.shape
    npt = xp.shape[0] // tm
    rows = xp.shape[0] + tm
    kvw = wkv.shape[1]
    nq = wq.shape[1]
    return pl.pallas_call(
        _ffn_kernel,
        grid=(npt + 1,),
        in_specs=[pl.BlockSpec((tm, d), lambda i: (jnp.minimum(i, npt - 1), 0)),
                  _const_spec(xs.shape),
                  _const_spec(gffn.shape), _const_spec(wgu.shape), _const_spec(wd.shape),
                  _const_spec(gkv.shape), _const_spec(wkv.shape),
                  _const_spec(gq.shape), _const_spec(wq.shape)],
        out_specs=[pl.BlockSpec((tm, d), lambda i: (i, 0)),
                   pl.BlockSpec((tm, kvw), lambda i: (i, 0)),
                   pl.BlockSpec((tm, nq), lambda i: (i, 0))],
        out_shape=[jax.ShapeDtypeStruct((rows, d), F32),
                   jax.ShapeDtypeStruct((rows, kvw), F32),
                   jax.ShapeDtypeStruct((rows, nq), BF16)],
        compiler_params=_params("arbitrary"),
        name="ffn_kv_q",
    )(xp, xs, gffn, wgu, wd, gkv, wkv, gq, wq)


def _t5_bucket(dist, num_buckets):
    max_exact = num_buckets // 2
    d = jnp.maximum(dist, 0)
    large = max_exact + (jnp.log(jnp.maximum(d, 1).astype(F32) / max_exact)
                         / math.log(MAX_DISTANCE / max_exact) * (num_buckets - max_exact)).astype(jnp.int32)
    large = jnp.minimum(large, num_buckets - 1)
    return jnp.where(d < max_exact, d, large)


def _bias_kernel(bm_ref, bs_ref, rb_ref, bias_ref, bias_s_ref, *, kvh, rep):
    nb, nh = rb_ref.shape
    bm = bm_ref[...]
    bs = bs_ref[...]

    def lookup(buckets, h):
        terms = [jnp.where(buckets == k, rb_ref[k, h], 0.0) for k in range(nb)]
        while len(terms) > 1:
            terms = [a + b for a, b in zip(terms[0::2], terms[1::2])] + (terms[-1:] if len(terms) % 2 else [])
        return terms[0]

    w = bm.shape[0]
    for g in range(kvh):
        for r in range(rep):
            bias_ref[g, :, r * w:(r + 1) * w] = lookup(bm, g * rep + r) * math.log2(math.e)
            bias_s_ref[g, r:r + 1, :] = lookup(bs, g * rep + r)


def _bias_tables(rel_bias, window, kvh, rep):
    nb, nh = rel_bias.shape
    key = jnp.arange(window)[:, None]
    qry = jnp.arange(window)[None, :]
    bm = _t5_bucket((qry - key) % window, nb).astype(jnp.int32)
    bs = _t5_bucket(window - 1 - qry, nb).astype(jnp.int32)
    kern = functools.partial(_bias_kernel, kvh=kvh, rep=rep)
    return pl.pallas_call(
        kern,
        in_specs=[pl.BlockSpec(memory_space=pltpu.VMEM), pl.BlockSpec(memory_space=pltpu.VMEM),
                  pl.BlockSpec(memory_space=pltpu.SMEM)],
        out_shape=[jax.ShapeDtypeStruct((kvh, window, rep * window), F32),
                   jax.ShapeDtypeStruct((kvh, rep, window), F32)],
        name="t5_bias",
    )(bm, bs, rel_bias)


def _attn_prompt_kernel(q_ref, kvc_ref, kvp_ref, bias_ref, sink_ref, ot_ref, *, kvh, rep, hd, scale):
    w = kvp_ref.shape[0]
    kv_all = kvc_ref[...]
    for sub in range(q_ref.shape[0] // w):
        rows = slice(sub * w, (sub + 1) * w)
        kvp = kvp_ref[...] if sub == 0 else kv_all[(sub - 1) * w:sub * w]
        first = pl.program_id(1) == 0 if sub == 0 else None
        _attn_prompt_block(q_ref[rows, :], kv_all[rows], kvp, first, bias_ref, sink_ref, ot_ref, sub * w,
                           kvh=kvh, rep=rep, hd=hd, scale=scale)


def _attn_prompt_block(q, kvc, kvp, first, bias_ref, sink_ref, ot_ref, col0, *, kvh, rep, hd, scale):
    w = q.shape[0]
    kvw = kvh * hd
    lane = lax.broadcasted_iota(jnp.int32, (1, kvw), 1)
    key = lax.broadcasted_iota(jnp.int32, (w, rep * w), 0)
    qry = lax.broadcasted_iota(jnp.int32, (w, rep * w), 1) % w
    upper = key > qry
    mask_add = None if first is None else jnp.where(jnp.logical_and(upper, first), NEG_INF, 0.0)
    log2e = math.log2(math.e)
    heads_per_blk = LANES // hd
    lane_blocks = kvw // LANES
    kbs = []
    for half in range(heads_per_blk):
        lmask = (lane % LANES) // hd == half
        kbs += [jnp.where(lmask, kvp[:, :kvw], 0.0), jnp.where(lmask, kvc[:, :kvw], 0.0)]
    qrows = []
    for p in range(lane_blocks):
        qg = jnp.concatenate([q[:, r * kvw + p * LANES:r * kvw + (p + 1) * LANES] for r in range(rep)], axis=0)
        zero = jnp.zeros_like(qg)
        qrows.append(jnp.concatenate([qg if other == p else zero for other in range(lane_blocks)], axis=1))
    s_full = lax.dot_general(jnp.concatenate(kbs, axis=0).astype(BF16), jnp.concatenate(qrows, axis=0),
                             (((1,), (1,)), ((), ())), preferred_element_type=F32)
    for p in range(lane_blocks):
        vs = slice(kvw + p * LANES, kvw + (p + 1) * LANES)
        vb = jnp.concatenate([kvp[:, vs], kvc[:, vs]], axis=0).astype(BF16)
        s_all = s_full[:, p * rep * w:(p + 1) * rep * w]
        pcats, denoms = [], []
        for half in range(heads_per_blk):
            g = p * heads_per_blk + half
            s = s_all[half * 2 * w:(half + 1) * 2 * w]
            bias = bias_ref[g] if first is None else bias_ref[g] + mask_add
            sc = jnp.where(upper, s[:w], s[w:]) * (scale * log2e) + bias
            sink = jnp.concatenate([jnp.full((1, w), sink_ref[g * rep + r] * log2e, F32) for r in range(rep)],
                                   axis=1)
            m = jnp.maximum(jnp.max(sc, axis=0, keepdims=True), sink)
            pe = jnp.exp2(sc - m)
            denoms.append(jnp.sum(pe, axis=0, keepdims=True) + jnp.exp2(sink - m))
            pcats.append(jnp.concatenate([jnp.where(upper, pe, 0.0), jnp.where(upper, 0.0, pe)], axis=0).astype(BF16))
        og_all = lax.dot_general(vb, jnp.concatenate(pcats, axis=1), (((0,), (0,)), ((), ())),
                                 preferred_element_type=F32)
        halves = [og_all[half * hd:(half + 1) * hd, half * rep * w:(half + 1) * rep * w] / denoms[half]
                  for half in range(heads_per_blk)]
        o_blk = jnp.concatenate(halves, axis=0).astype(BF16)
        for r in range(rep):
            ot_ref[r * kvw + p * LANES:r * kvw + (p + 1) * LANES, col0:col0 + w] = o_blk[:, r * w:(r + 1) * w]


def _attn_prompt(q, kv, bias, sinks, *, bsz, nblk, window, kvh, rep, hd):
    rows, nq = q.shape
    kvw2 = kv.shape[1]
    nsub = next(n for n in (4, 2, 1) if nblk % n == 0 and rows % (n * window) == 0)
    nstep = nblk // nsub
    kern = functools.partial(_attn_prompt_kernel, kvh=kvh, rep=rep, hd=hd, scale=1.0 / math.sqrt(hd))
    return pl.pallas_call(
        kern,
        grid=(bsz, nstep),
        in_specs=[pl.BlockSpec((nsub * window, nq), lambda b, i: (b * nstep + i, 0)),
                  pl.BlockSpec((nsub * window, kvw2), lambda b, i: (b * nstep + i, 0)),
                  pl.BlockSpec((window, kvw2), lambda b, i: (b * nblk + jnp.maximum(nsub * i - 1, 0), 0)),
                  _const_spec(bias.shape),
                  pl.BlockSpec(memory_space=pltpu.SMEM)],
        out_specs=pl.BlockSpec((nq, nsub * window), lambda b, i: (0, b * nstep + i)),
        out_shape=jax.ShapeDtypeStruct((nq, rows), BF16),
        compiler_params=_params("arbitrary", "arbitrary"),
        name="swa_prompt",
    )(q, kv, kv, bias, sinks)


def _attn_sample_kernel(q_ref, kv_ref, ck_ref, cv_ref, bias_ref, sink_ref, o_ref, nk_ref, nv_ref,
                        *, kvh, rep, hd, scale):
    nb, w, kvw = ck_ref.shape
    kv = kv_ref[...]
    lane = lax.broadcasted_iota(jnp.int32, (nb, rep, kvw), 2)

    def shifted(c_ref, n_ref, new):
        flat = c_ref[...].reshape(nb * w, kvw)
        n_ref[...] = pltpu.roll(flat, nb * w - 1, axis=0).reshape(nb, w, kvw)
        n_ref[:, w - 1:w, :] = new
        return n_ref[...].astype(BF16)

    nkb = shifted(ck_ref, nk_ref, kv[:, :, :kvw])
    nvb = shifted(cv_ref, nv_ref, kv[:, :, kvw:])
    q = q_ref[...].astype(F32)
    masks = [jnp.logical_and(lane >= g * hd, lane < (g + 1) * hd) for g in range(kvh)]
    qall = jnp.concatenate([jnp.where(masks[g], q, 0.0) for g in range(kvh)], axis=1).astype(BF16)
    s = jnp.einsum("nmc,njc->nmj", qall, nkb, preferred_element_type=F32)
    sc = s * scale + bias_ref[...][None]
    sink = sink_ref[...][None]
    m = jnp.maximum(jnp.max(sc, axis=-1, keepdims=True), sink)
    pe = jnp.exp(sc - m)
    probs = pe / (jnp.sum(pe, axis=-1, keepdims=True) + jnp.exp(sink - m))
    og = jnp.einsum("nmj,njc->nmc", probs.astype(BF16), nvb, preferred_element_type=F32)
    o = jnp.zeros((nb, rep, kvw), F32)
    for g in range(kvh):
        o = jnp.where(masks[g], og[:, g * rep:(g + 1) * rep, :], o)
    o_ref[...] = o.astype(BF16)


def _attn_sample(q3, kv3, ck, cv, bias_s, sink_s, *, nb, kvh, rep, hd):
    n, w, kvw = ck.shape
    kern = functools.partial(_attn_sample_kernel, kvh=kvh, rep=rep, hd=hd, scale=1.0 / math.sqrt(hd))
    cache_spec = pl.BlockSpec((nb, w, kvw), lambda i: (i, 0, 0))
    return pl.pallas_call(
        kern,
        grid=(n // nb,),
        in_specs=[pl.BlockSpec((nb, rep, kvw), lambda i: (i, 0, 0)),
                  pl.BlockSpec((nb, 1, 2 * kvw), lambda i: (i, 0, 0)),
                  cache_spec, cache_spec,
                  _const_spec(bias_s.shape), _const_spec(sink_s.shape)],
        out_specs=[pl.BlockSpec((nb, rep, kvw), lambda i: (i, 0, 0)), cache_spec, cache_spec],
        out_shape=[jax.ShapeDtypeStruct((n, rep, kvw), BF16),
                   jax.ShapeDtypeStruct((n, w, kvw), F32), jax.ShapeDtypeStruct((n, w, kvw), F32)],
        compiler_params=_params("arbitrary"),
        name="swa_sample",
    )(q3, kv3, ck, cv, bias_s, sink_s)


CHUNK = 16
BIG_COPY = 128
SUB = 256
UP_ROW_BLOCK = 1024
DOWN_ROW_BLOCK = 1024
TOKEN_TILE = 512
MOE_TILE = 512


def _chunk_copy(src, src_row, dst, dst_row, sem, rows=CHUNK):
    return pltpu.make_async_copy(src.at[pl.ds(pl.multiple_of(src_row, CHUNK), rows), :],
                                 dst.at[pl.ds(pl.multiple_of(dst_row, CHUNK), rows), :], sem)


def _start_copies(src, src_row, dst, dst_row, sem, nrows):
    nbig = lax.shift_right_logical(nrows, int(math.log2(BIG_COPY)))

    def big(c, carry):
        _chunk_copy(src, src_row + c * BIG_COPY, dst, dst_row + c * BIG_COPY, sem, BIG_COPY).start()
        return carry

    lax.fori_loop(0, nbig, big, 0)
    done = nbig * BIG_COPY

    def small(c, carry):
        _chunk_copy(src, src_row + done + c * CHUNK, dst, dst_row + done + c * CHUNK, sem).start()
        return carry

    lax.fori_loop(0, lax.shift_right_logical(nrows - done, int(math.log2(CHUNK))), small, 0)


def _moe_route_kernel(x2_ref, ot_ref, wo_ref, gffn_ref, wr_ref, lst_ref, ust_ref,
                      x3_ref, cm_ref, tmeta_ref, xs_hbm,
                      comp_s, carry_s, zero_s, base_v, pend_sm, sem, *, ne, cap, n_valid):
    i = pl.program_id(0)
    nt = pl.num_programs(0)
    tm = x2_ref.shape[0]
    w = comp_s.shape[1]
    slot = lax.rem(i, 2)

    @pl.when(i == 0)
    def _():
        base_v[...] = jnp.zeros_like(base_v)
        zero_s[...] = jnp.zeros_like(zero_s)
        carry_s[...] = jnp.zeros_like(carry_s)

    def drain(n):
        def body(c, carry):
            _chunk_copy(zero_s, 0, xs_hbm, 0, sem).wait()
            return carry
        lax.fori_loop(0, n, body, 0)

    @pl.when(i > 0)
    def _():
        drain(pend_sm[0])

    x3 = x2_ref[...] + lax.dot_general(ot_ref[...], wo_ref[...], (((0,), (0,)), ((), ())),
                                       preferred_element_type=F32)
    x3_ref[...] = x3
    hb = _rms(x3, gffn_ref[...]).astype(BF16)

    lane = lax.broadcasted_iota(jnp.int32, (tm, LANES), 1)
    logits = jnp.where(lane < ne, _dot(hb, wr_ref[...]), -jnp.inf)
    m1 = jnp.max(logits, axis=-1, keepdims=True)
    i1 = jnp.min(jnp.where(logits == m1, lane, LANES), axis=-1, keepdims=True)
    rest = jnp.where(lane == i1, -jnp.inf, logits)
    m2 = jnp.max(rest, axis=-1, keepdims=True)
    i2 = jnp.min(jnp.where(rest == m2, lane, LANES), axis=-1, keepdims=True)
    e2 = jnp.exp(m2 - m1)
    g1 = 1.0 / (1.0 + e2)
    g2 = e2 / (1.0 + e2)

    live = i * tm + lax.broadcasted_iota(jnp.int32, (tm, 1), 0) < n_valid
    sel = jnp.where(jnp.logical_and(live, jnp.logical_or(lane == i1, lane == i2)), 1.0, 0.0)
    rank = _dot(lst_ref[...], sel.astype(BF16))
    cnt = jnp.sum(sel, axis=0, keepdims=True)
    fill = base_v[...]
    rem = fill - jnp.floor(fill / CHUNK) * CHUNK
    cpad = jnp.floor((rem + cnt + (CHUNK - 1)) / CHUNK) * CHUNK
    loff = _dot(jnp.broadcast_to(cpad, (8, LANES)).astype(BF16), ust_ref[...])[0:1]
    dest = loff + rem + rank
    ld1 = jnp.where(live, jnp.sum(jnp.where(lane == i1, dest, 0.0), axis=-1, keepdims=True), -1.0)
    ld2 = jnp.where(live, jnp.sum(jnp.where(lane == i2, dest, 0.0), axis=-1, keepdims=True), -1.0)
    cm = jnp.where(lane == 0, ld1, jnp.where(lane == 1, ld2, jnp.where(lane == 2, g1, jnp.where(lane == 3, g2, 0.0))))
    cm_ref[...] = cm

    rm = cm.T
    rowi = lax.broadcasted_iota(jnp.int32, (w, tm), 0).astype(F32)
    place = (jnp.where(rowi == rm[0:1], 1.0, 0.0) + jnp.where(rowi == rm[1:2], 1.0, 0.0)).astype(BF16)
    comp_s[slot] = _dot(place, hb).astype(BF16)

    base = fill - rem
    srow = lax.broadcasted_iota(jnp.int32, (8, LANES), 0)
    tmeta_ref[...] = jnp.where(srow == 0, cpad, jnp.where(srow == 1, base, jnp.where(srow == 2, fill + cnt, 0.0))
                               ).astype(jnp.int32)
    base_v[...] = fill + cnt

    cpad_i = cpad.astype(jnp.int32)
    loff_i = loff.astype(jnp.int32)
    base_i = base.astype(jnp.int32)
    tail_i = (rem + cnt - jnp.floor((rem + cnt) / CHUNK) * CHUNK).astype(jnp.int32)
    total = 0
    tails = []
    for e in range(ne):
        n_e = cpad_i[0, e]
        src0 = loff_i[0, e]
        dst0 = base_i[0, e] + e * cap

        @pl.when(n_e > 0)
        def _(e=e, n_e=n_e, src0=src0, partial=tail_i[0, e] > 0):
            head = pl.ds(pl.multiple_of(src0, CHUNK), CHUNK)
            comp_s[slot, head, :] = comp_s[slot, head, :] + carry_s[e]
            last = comp_s[slot, pl.ds(pl.multiple_of(src0 + n_e - CHUNK, CHUNK), CHUNK), :]
            carry_s[e] = jnp.where(partial, last, jnp.zeros_like(last))

        _start_copies(comp_s.at[slot], src0, xs_hbm, dst0, sem, n_e)
        total = total + lax.shift_right_logical(n_e, int(math.log2(CHUNK)))
        tails.append(dst0 + n_e)
    pend_sm[0] = total

    @pl.when(i == nt - 1)
    def _():
        nz_total = 0
        for e in range(ne):
            end = tails[e]
            nz = lax.shift_right_logical(lax.rem(SUB - lax.rem(end, SUB), SUB), int(math.log2(CHUNK)))

            def zstart(c, carry, end=end):
                _chunk_copy(zero_s, 0, xs_hbm, end + c * CHUNK, sem).start()
                return carry

            lax.fori_loop(0, nz, zstart, 0)
            nz_total = nz_total + nz
        drain(total + nz_total)


def _moe_route(x2, ot, wo, gffn, wr_pad, *, tm, ne, cap, n_valid):
    rows, d = x2.shape
    nt = rows // tm
    w = _staging_rows(tm, ne)
    lst = jnp.asarray(np.tril(np.ones((tm, tm), np.float32), -1), BF16)
    ust = jnp.asarray(np.triu(np.ones((LANES, LANES), np.float32), 1), BF16)
    kern = functools.partial(_moe_route_kernel, ne=ne, cap=cap, n_valid=n_valid)
    return pl.pallas_call(
        kern,
        grid=(nt,),
        in_specs=[pl.BlockSpec((tm, d), lambda i: (i, 0)),
                  pl.BlockSpec((ot.shape[0], tm), lambda i: (0, i)),
                  _const_spec(wo.shape), _const_spec(gffn.shape), _const_spec(wr_pad.shape),
                  _const_spec(lst.shape), _const_spec(ust.shape)],
        out_specs=[pl.BlockSpec((tm, d), lambda i: (i, 0)),
                   pl.BlockSpec((tm, LANES), lambda i: (i, 0)),
                   pl.BlockSpec((8, LANES), lambda i: (i, 0)),
                   pl.BlockSpec(memory_space=pl.ANY)],
        out_shape=[jax.ShapeDtypeStruct((rows, d), F32),
                   jax.ShapeDtypeStruct((rows, LANES), F32),
                   jax.ShapeDtypeStruct((nt * 8, LANES), jnp.int32),
                   jax.ShapeDtypeStruct((ne * cap, d), BF16)],
        scratch_shapes=[pltpu.VMEM((2, w, d), BF16), pltpu.VMEM((ne, CHUNK, d), BF16), pltpu.VMEM((CHUNK, d), BF16),
                        pltpu.VMEM((1, LANES), F32), pltpu.SMEM((1,), jnp.int32),
                        pltpu.SemaphoreType.DMA(())],
        compiler_params=_params("arbitrary"),
        name="moe_route",
    )(x2, ot, wo, gffn, wr_pad, lst, ust)


def _staging_rows(tm, ne):
    return -(-(TOP_K * tm + 2 * ne * (CHUNK - 1)) // LANES) * LANES


def _max_blocks(rows, ne, row_block):
    return TOP_K * rows // row_block + ne


def _expert_steps(nrows, nf, max_blocks, row_block):
    ne = nrows.shape[0]
    nblk = (nrows + row_block - 1) // row_block
    cum = jnp.cumsum(nblk) * nf
    total = cum[-1]
    s = jnp.minimum(jnp.arange(max_blocks * nf, dtype=jnp.int32), total - 1)
    e = jnp.minimum(jnp.sum((s[:, None] >= cum[None, :]).astype(jnp.int32), axis=1), ne - 1)
    nb_e = nblk[e]
    within = s - (cum[e] - nb_e * nf)
    f = within // nb_e
    pos = within - f * nb_e
    r = (pos + nb_e - 1) % nb_e
    nsub = jnp.clip((nrows[e] - r * row_block + SUB - 1) // SUB, 0, row_block // SUB)
    first = (pos == 0).astype(jnp.int32)
    return (e, f.astype(jnp.int32), r.astype(jnp.int32), first, nsub.astype(jnp.int32),
            jnp.reshape(total, (1,)).astype(jnp.int32))


def _moe_up_kernel(e_ref, f_ref, r_ref, first_ref, nsub_ref, n_ref, xs_ref, wg_ref, wu_ref, act_ref, wgb_s, wub_s):
    s = pl.program_id(0)

    @pl.when(s < n_ref[0])
    def _():
        @pl.when(first_ref[s] == 1)
        def _():
            wgb_s[...] = wg_ref[...].astype(BF16)
            wub_s[...] = wu_ref[...].astype(BF16)

        def gate_up(row0, nrows):
            rows = pl.ds(pl.multiple_of(row0, SUB), nrows)
            xsb = xs_ref[rows, :]
            a = _dot(xsb, wgb_s[...])
            b = _dot(xsb, wub_s[...])
            act_ref[rows, :] = (jax.nn.silu(a) * b).astype(BF16)

        nsub = nsub_ref[s]
        full = xs_ref.shape[0] // SUB

        @pl.when(nsub == full)
        def _():
            xsb = xs_ref[...]
            for c0 in range(0, act_ref.shape[1], MXU_DIM):
                cols = slice(c0, c0 + MXU_DIM)
                a = _dot(xsb, wgb_s[:, cols])
                b = _dot(xsb, wub_s[:, cols])
                act_ref[:, cols] = (jax.nn.silu(a) * b).astype(BF16)

        @pl.when(nsub < full)
        def _():
            npair = lax.shift_right_logical(nsub, 1)

            def body(j, carry):
                gate_up(j * (2 * SUB), 2 * SUB)
                return carry

            lax.fori_loop(0, npair, body, 0)

            @pl.when(nsub > 2 * npair)
            def _():
                gate_up(npair * (2 * SUB), SUB)


def _moe_up(steps, xs, wgu, *, fc, cap, row_block):
    ne, d, dff2 = wgu.shape
    dff = dff2 // 2
    nf = dff // fc
    cb = cap // row_block
    grid_spec = pltpu.PrefetchScalarGridSpec(
        num_scalar_prefetch=6,
        grid=(steps[0].shape[0],),
        in_specs=[pl.BlockSpec((row_block, d), lambda s, e, f, r, *_: (e[s] * cb + r[s], 0)),
                  pl.BlockSpec((None, d, fc), lambda s, e, f, r, *_: (e[s], 0, f[s])),
                  pl.BlockSpec((None, d, fc), lambda s, e, f, r, *_: (e[s], 0, nf + f[s]))],
        out_specs=pl.BlockSpec((row_block, fc), lambda s, e, f, r, *_: (e[s] * cb + r[s], f[s])),
        scratch_shapes=[pltpu.VMEM((d, fc), BF16), pltpu.VMEM((d, fc), BF16)])
    return pl.pallas_call(
        _moe_up_kernel,
        grid_spec=grid_spec,
        out_shape=jax.ShapeDtypeStruct((ne * cap, dff), BF16),
        compiler_params=_params("arbitrary"),
        name="moe_up",
    )(*steps, xs, wgu, wgu)


def _moe_down_kernel(e_ref, f_ref, r_ref, first_ref, nsub_ref, n_ref, act_ref, wd_ref, ys_ref, wdb_s):
    s = pl.program_id(0)

    @pl.when(s < n_ref[0])
    def _():
        @pl.when(first_ref[s] == 1)
        def _():
            wdb_s[...] = wd_ref[...].astype(BF16)

        def project(row0, nrows):
            rows = pl.ds(pl.multiple_of(row0, SUB), nrows)
            ys_ref[rows, :] = _dot(act_ref[rows, :], wdb_s[...]).astype(BF16)

        nsub = nsub_ref[s]
        full = act_ref.shape[0] // SUB

        @pl.when(nsub == full)
        def _():
            ab = act_ref[...]
            for c0 in range(0, ys_ref.shape[1], MXU_DIM):
                cols = slice(c0, c0 + MXU_DIM)
                ys_ref[:, cols] = _dot(ab, wdb_s[:, cols]).astype(BF16)

        @pl.when(nsub < full)
        def _():
            npair = lax.shift_right_logical(nsub, 1)

            def body(j, carry):
                project(j * (2 * SUB), 2 * SUB)
                return carry

            lax.fori_loop(0, npair, body, 0)

            @pl.when(nsub > 2 * npair)
            def _():
                project(npair * (2 * SUB), SUB)


def _moe_down(steps, act, wd, *, cap, row_block):
    ne, dff, d = wd.shape
    cb = cap // row_block
    grid_spec = pltpu.PrefetchScalarGridSpec(
        num_scalar_prefetch=6,
        grid=(steps[0].shape[0],),
        in_specs=[pl.BlockSpec((row_block, dff), lambda s, e, f, r, *_: (e[s] * cb + r[s], 0)),
                  pl.BlockSpec((None, dff, d), lambda s, e, f, r, *_: (e[s], 0, 0))],
        out_specs=pl.BlockSpec((row_block, d), lambda s, e, f, r, *_: (e[s] * cb + r[s], 0)),
        scratch_shapes=[pltpu.VMEM((dff, d), BF16)])
    return pl.pallas_call(
        _moe_down_kernel,
        grid_spec=grid_spec,
        out_shape=jax.ShapeDtypeStruct((ne * cap, d), BF16),
        compiler_params=_params("arbitrary", vmem=2 * (dff * d * 4 + row_block * (dff + d) * 2) + dff * d * 2
                                + 4 * SUB * d * 4),
        name="moe_down",
    )(*steps, act, wd)


def _moe_combine_kernel(cpad_ref, seg_ref, x3_ref, cm_ref, gfin_ref, ys_hbm, yp_ref, ys_ref, yloc_s, sem,
                        *, ne, npt):
    i = pl.program_id(0)
    nt = pl.num_programs(0)
    tm = x3_ref.shape[0]
    w = yloc_s.shape[1]
    slot = lax.rem(i, 2)
    shift = int(math.log2(CHUNK))

    def issue(t, sl):
        off = 0
        for e in range(ne):
            n_e = cpad_ref[t * ne + e]
            _start_copies(ys_hbm, seg_ref[t * ne + e], yloc_s.at[sl], off, sem.at[sl], n_e)
            off = off + n_e

    @pl.when(i == 0)
    def _():
        yloc_s[...] = jnp.zeros_like(yloc_s)
        issue(0, 0)

    @pl.when(i + 1 < nt)
    def _():
        issue(i + 1, 1 - slot)

    total = 0
    for e in range(ne):
        total = total + lax.shift_right_logical(cpad_ref[i * ne + e], shift)

    def wait(c, carry):
        _chunk_copy(ys_hbm, 0, yloc_s.at[slot], 0, sem.at[slot]).wait()
        return carry

    lax.fori_loop(0, total, wait, 0)

    cm = cm_ref[...]
    col = lax.broadcasted_iota(jnp.int32, (tm, w), 1).astype(F32)
    yl = yloc_s[slot]
    mix = jnp.where(col == cm[:, 0:1], cm[:, 2:3], jnp.where(col == cm[:, 1:2], cm[:, 3:4], 0.0))
    y = _rms(x3_ref[...] + _dot(mix.astype(BF16), yl), gfin_ref[...])

    @pl.when(i < npt)
    def _():
        yp_ref[...] = y

    @pl.when(i >= npt)
    def _():
        ys_ref[...] = y


def _moe_combine(cpad, seg, x3, cm, gfin, ys, *, tm, ne, sample_rows):
    rows, d = x3.shape
    nt = rows // tm
    npt = (rows - sample_rows) // tm
    w = _staging_rows(tm, ne)
    kern = functools.partial(_moe_combine_kernel, ne=ne, npt=npt)
    grid_spec = pltpu.PrefetchScalarGridSpec(
        num_scalar_prefetch=2,
        grid=(nt,),
        in_specs=[pl.BlockSpec((tm, d), lambda i, *_: (i, 0)),
                  pl.BlockSpec((tm, LANES), lambda i, *_: (i, 0)),
                  pl.BlockSpec(gfin.shape, lambda i, *_: (0, 0)),
                  pl.BlockSpec(memory_space=pl.ANY)],
        out_specs=[pl.BlockSpec((tm, d), lambda i, *_: (jnp.minimum(i, npt - 1), 0)),
                   pl.BlockSpec((tm, d), lambda i, *_: (jnp.maximum(i - npt, 0), 0))],
        scratch_shapes=[pltpu.VMEM((2, w, d), BF16), pltpu.SemaphoreType.DMA((2,))])
    return pl.pallas_call(
        kern,
        grid_spec=grid_spec,
        out_shape=[jax.ShapeDtypeStruct((npt * tm, d), F32), jax.ShapeDtypeStruct((sample_rows, d), F32)],
        compiler_params=_params("arbitrary"),
        name="moe_combine",
    )(cpad, seg, x3, cm, gfin, ys)


def _moe(x2, ot, wo, gffn, wr, wgu, wd, gfin, *, tm, fc, n_valid, sample_rows):
    rows, d = x2.shape
    ne = wr.shape[1]
    nt = rows // tm
    nf = wgu.shape[2] // 2 // fc
    cap = -(-(rows + SUB) // UP_ROW_BLOCK) * UP_ROW_BLOCK
    wr_pad = jnp.pad(wr, ((0, 0), (0, LANES - ne))).astype(BF16)
    x3, cm, tmeta, xs = _moe_route(x2, ot, wo, gffn, wr_pad, tm=tm, ne=ne, cap=cap, n_valid=n_valid)
    tmeta = tmeta.reshape(nt, 8, LANES)[:, :, :ne]
    cpad = tmeta[:, 0, :].reshape(nt * ne)
    seg = (tmeta[:, 1, :] + jnp.arange(ne, dtype=jnp.int32)[None, :] * cap).reshape(nt * ne)
    nrows = tmeta[nt - 1, 2, :]
    up_steps = _expert_steps(nrows, nf, _max_blocks(rows, ne, UP_ROW_BLOCK), UP_ROW_BLOCK)
    act = _moe_up(up_steps, xs, wgu, fc=fc, cap=cap, row_block=UP_ROW_BLOCK)
    down_steps = _expert_steps(nrows, 1, _max_blocks(rows, ne, DOWN_ROW_BLOCK), DOWN_ROW_BLOCK)
    ys = _moe_down(down_steps, act, wd, cap=cap, row_block=DOWN_ROW_BLOCK)
    return _moe_combine(cpad, seg, x3, cm, gfin, ys, tm=tm, ne=ne, sample_rows=sample_rows)


def kernel(x_prompt, x_sample, state_ssm_re, state_ssm_im, cache_k_win, cache_v_win, g_mix, g_ffn, g_kv, g_final, ssm_a_re, ssm_a_im, ssm_log_dt, ssm_b_re, ssm_b_im, ssm_c_re, ssm_c_im, ssm_d, w_glu, w_kv, w_q, w_o, attn_sinks, rel_bias, w_ffn_gate_up, w_ffn_down, w_router, w_exp_gate_up, w_exp_down):
    bsz, seq, d = x_prompt.shape
    ns, dec_seq, _ = x_sample.shape
    assert dec_seq == 1 and g_mix.shape[0] == 2 and ssm_a_re.shape[0] == 1 and w_q.shape[0] == 1
    _, g, p = ssm_a_re.shape
    gp = g * p
    window, kvh, hd = cache_k_win.shape[1:]
    kvw = kvh * hd
    nh = attn_sinks.shape[1]
    rep = nh // kvh
    nq = nh * hd
    assert bsz == 8 and ns % bsz == 0 and seq % window == 0 and LANES % hd == 0

    lam_re, lam_im, wb_re, wb_im, wc_re, wc_imn = _zoh(ssm_a_re[0], ssm_a_im[0], ssm_log_dt[0],
                                                        ssm_b_re[0], ssm_b_im[0], ssm_c_re[0], ssm_c_im[0])
    wglu = w_glu[0].astype(BF16)
    d_skip = ssm_d[0].reshape(1, d)
    wgu = w_ffn_gate_up[0].astype(BF16)
    wd = w_ffn_down[0].astype(BF16)
    wkv = w_kv.astype(BF16)
    wq = w_q[0].reshape(d, kvh, rep, hd).transpose(0, 2, 1, 3).reshape(d, nq).astype(BF16)
    wo = w_o[0].reshape(kvh, rep, hd, d).transpose(1, 0, 2, 3).reshape(nq, d).astype(BF16)
    bias, bias_s = _bias_tables(rel_bias, window, kvh, rep)
    bias_s = bias_s.reshape(nh, window)
    sinks = attn_sinks[0]
    sink_s = sinks.reshape(nh, 1)

    tm = TOKEN_TILE
    npr = seq * bsz
    assert npr % tm == 0 and ns <= tm

    x1_p, st_re, st_im = _ssm_prompt(x_prompt, g_mix[0:1], lam_re, lam_im, wb_re, wb_im, wc_re, wc_imn,
                                     d_skip, wglu, lc=32, nsub=2)
    x1_s, hs_re, hs_im = _ssm_sample(x_sample.reshape(ns, d), g_mix[0:1], lam_re, lam_im,
                                     state_ssm_re[0].reshape(ns, gp), state_ssm_im[0].reshape(ns, gp),
                                     wb_re, wb_im, wc_re, wc_imn, d_skip, wglu, pad_rows=tm)

    x2, kv, q = _ffn(x1_p.reshape(npr, d), x1_s, g_ffn[0:1], wgu, wd, g_kv.reshape(1, d), wkv, g_mix[1:2], wq)

    ot = _attn_prompt(q, kv, bias, sinks, bsz=bsz, nblk=seq // window, window=window, kvh=kvh, rep=rep, hd=hd)
    o_s, nk_s, nv_s = _attn_sample(q[npr:npr + ns].reshape(ns, rep, kvw), kv[npr:npr + ns].reshape(ns, 1, 2 * kvw),
                                   cache_k_win.reshape(ns, window, kvw), cache_v_win.reshape(ns, window, kvw),
                                   bias_s, sink_s, nb=32, kvh=kvh, rep=rep, hd=hd)
    ot = lax.dynamic_update_slice(ot, jnp.pad(o_s.reshape(ns, nq).T, ((0, 0), (0, tm - ns))), (0, npr))

    y_p, y_s = _moe(x2, ot, wo, g_ffn[1:2], w_router[0], w_exp_gate_up[0], w_exp_down[0], g_final.reshape(1, d),
                    tm=MOE_TILE, fc=w_exp_down.shape[2] // 2, n_valid=npr + ns, sample_rows=tm)

    y_prompt = y_p.reshape(bsz, seq, d)
    y_sample = y_s[:ns].reshape(ns, 1, d)
    kv_tail = jnp.stack([kv[(b + 1) * seq - window:(b + 1) * seq] for b in range(bsz)])
    kv_tail = kv_tail.reshape(bsz, window, 2, kvh, hd).transpose(2, 0, 1, 3, 4)
    return (y_prompt, y_sample,
            st_re.reshape(1, bsz, g, p), st_im.reshape(1, bsz, g, p), kv_tail[0], kv_tail[1],
            hs_re.reshape(1, ns, g, p), hs_im.reshape(1, ns, g, p),
            nk_s.reshape(ns, window, kvh, hd), nv_s.reshape(ns, window, kvh, hd))
```

```python
import functools
import math

import numpy as np
import jax
import jax.numpy as jnp
from jax import lax
from jax.experimental import pallas as pl
from jax.experimental.pallas import tpu as pltpu

F32 = jnp.float32
BF16 = jnp.bfloat16

EPS = 1e-6
NEG_INF = -1e30
TOP_K = 2
MAX_DISTANCE = 128
MXU_DIM = 256
LANES = 128
VMEM_LIMIT_BYTES = 56 * 1024 * 1024


def _dot(a, b):
    return jnp.dot(a, b, preferred_element_type=F32)


def _rms(x, g):
    return x * lax.rsqrt(jnp.mean(x * x, axis=-1, keepdims=True) + EPS) * g


def _const_spec(shape):
    nd = len(shape)
    return pl.BlockSpec(shape, lambda *_: (0,) * nd)


def _params(*sem, vmem=VMEM_LIMIT_BYTES):
    return pltpu.CompilerParams(dimension_semantics=sem, vmem_limit_bytes=vmem)


def _zoh_kernel(a_re_ref, a_im_ref, log_dt_ref, b_re_ref, b_im_ref, ct_re_ref, ct_im_ref,
                lam_re_ref, lam_im_ref, wb_re_ref, wb_im_ref, wc_re_ref, wc_imn_ref, *, hch, p):
    a_re = a_re_ref[...]
    a_im = a_im_ref[...]
    dt = jnp.exp(log_dt_ref[...])
    mag = jnp.exp(a_re * dt)
    lr = mag * jnp.cos(a_im * dt)
    li = mag * jnp.sin(a_im * dt)
    lam_re_ref[...] = lr
    lam_im_ref[...] = li
    nr = lr - 1.0
    den = a_re * a_re + a_im * a_im
    qr = (nr * a_re + li * a_im) / den
    qi = (li * a_re - nr * a_im) / den
    b_re = b_re_ref[...]
    b_im = b_im_ref[...]
    bb_re = qr * b_re - qi * b_im
    bb_im = qr * b_im + qi * b_re

    nkb, ublk, sblk = wb_re_ref.shape
    gpb = ublk // hch
    sh_h, sh_p = int(math.log2(hch)), int(math.log2(p))
    row = lax.broadcasted_iota(jnp.int32, (ublk, sblk), 0)
    col = lax.broadcasted_iota(jnp.int32, (ublk, sblk), 1)
    diag_in = lax.shift_right_logical(row, sh_h) == lax.shift_right_logical(col, sh_p)
    row = lax.broadcasted_iota(jnp.int32, (sblk, ublk), 0)
    col = lax.broadcasted_iota(jnp.int32, (sblk, ublk), 1)
    diag_out = lax.shift_right_logical(row, sh_p) == lax.shift_right_logical(col, sh_h)
    spread = jnp.where(lax.broadcasted_iota(jnp.int32, (hch, ublk), 0)
                       == jnp.bitwise_and(lax.broadcasted_iota(jnp.int32, (hch, ublk), 1), hch - 1),
                       1.0, 0.0).astype(BF16)
    for kb in range(nkb):
        cs = slice(kb * sblk, (kb + 1) * sblk)
        wb_re_ref[kb] = jnp.where(diag_in, jnp.concatenate([bb_re[:, cs]] * gpb, axis=0), 0.0).astype(BF16)
        wb_im_ref[kb] = jnp.where(diag_in, jnp.concatenate([bb_im[:, cs]] * gpb, axis=0), 0.0).astype(BF16)
        c_re = _dot(ct_re_ref[cs, :].astype(BF16), spread)
        c_im = _dot(ct_im_ref[cs, :].astype(BF16), spread)
        wc_re_ref[kb] = jnp.where(diag_out, c_re, 0.0).astype(BF16)
        wc_imn_ref[kb] = jnp.where(diag_out, -c_im, 0.0).astype(BF16)


def _zoh(a_re, a_im, log_dt, b_re, b_im, c_re, c_im):
    g, p = a_re.shape
    h = b_re.shape[-1]
    gp = g * p
    gpb = MXU_DIM // h
    nkb = g // gpb
    assert h & (h - 1) == 0 and p & (p - 1) == 0 and g % gpb == 0
    row = jax.ShapeDtypeStruct((1, gp), F32)
    wb = jax.ShapeDtypeStruct((nkb, gpb * h, gpb * p), BF16)
    wc = jax.ShapeDtypeStruct((nkb, gpb * p, gpb * h), BF16)
    return pl.pallas_call(
        functools.partial(_zoh_kernel, hch=h, p=p),
        out_shape=(row, row, wb, wb, wc, wc),
        name="s5_zoh",
    )(a_re.reshape(1, gp), a_im.reshape(1, gp), jnp.repeat(log_dt, p).reshape(1, gp),
      b_re.transpose(2, 0, 1).reshape(h, gp), b_im.transpose(2, 0, 1).reshape(h, gp),
      c_re.transpose(0, 2, 1).reshape(gp, h), c_im.transpose(0, 2, 1).reshape(gp, h))


def _ssm_in_proj(ub, wb_re_ref, wb_im_ref, bu_re, bu_im):
    nkb, ublk, sblk = wb_re_ref.shape
    for kb in range(nkb):
        ukb = ub[:, kb * ublk:(kb + 1) * ublk]
        bu_re[:, kb * sblk:(kb + 1) * sblk] = _dot(ukb, wb_re_ref[kb])
        bu_im[:, kb * sblk:(kb + 1) * sblk] = _dot(ukb, wb_im_ref[kb])


def _ssm_out_proj(h_re, h_im, wc_re_ref, wc_imn_ref):
    nkb, sblk, _ = wc_re_ref.shape
    ys = []
    for kb in range(nkb):
        hr = h_re[:, kb * sblk:(kb + 1) * sblk].astype(BF16)
        hi = h_im[:, kb * sblk:(kb + 1) * sblk].astype(BF16)
        ys.append(_dot(hr, wc_re_ref[kb]) + _dot(hi, wc_imn_ref[kb]))
    return jnp.concatenate(ys, axis=1)


def _ssm_glu(x, u, y, d_ref, wglu_ref):
    z = jax.nn.gelu(y + d_ref[...] * u).astype(BF16)
    gl = _dot(z, wglu_ref[...])
    d = x.shape[1]
    return x + gl[:, :d] * jax.nn.sigmoid(gl[:, d:])


def _split3(v):
    hi = v.astype(BF16)
    r1 = v - hi.astype(F32)
    mid = r1.astype(BF16)
    lo = (r1 - mid.astype(F32)).astype(BF16)
    return hi, mid, lo


def _ssm_prompt_kernel(x_ref, g_ref, lam_re_ref, lam_im_ref, wb_re_ref, wb_im_ref,
                       wc_re_ref, wc_imn_ref, d_ref, wglu_ref, perm_ref, permt_ref,
                       out_ref, st_re_ref, st_im_ref, bu_re_all, bu_im_all, *, lc, bsz, lane_chunk):
    @pl.when(pl.program_id(0) == 0)
    def _():
        st_re_ref[...] = jnp.zeros_like(st_re_ref)
        st_im_ref[...] = jnp.zeros_like(st_im_ref)

    for sub in range(bu_re_all.shape[0]):
        ts = slice(sub * lc, (sub + 1) * lc)
        _ssm_prompt_chunk(x_ref[:, ts, :], g_ref, lam_re_ref, lam_im_ref, wb_re_ref, wb_im_ref,
                          wc_re_ref, wc_imn_ref, d_ref, wglu_ref, perm_ref, permt_ref,
                          out_ref.at[:, ts, :], st_re_ref, st_im_ref, bu_re_all.at[sub], bu_im_all.at[sub],
                          lc=lc, bsz=bsz, lane_chunk=lane_chunk)


def _ssm_prompt_chunk(x3, g_ref, lam_re_ref, lam_im_ref, wb_re_ref, wb_im_ref,
                      wc_re_ref, wc_imn_ref, d_ref, wglu_ref, perm_ref, permt_ref,
                      out_ref, st_re_ref, st_im_ref, bu_re, bu_im, *, lc, bsz, lane_chunk):
    d = x3.shape[2]
    x = x3.reshape(bsz * lc, d)
    u = _rms(x, g_ref[...])
    ub_tb = _dot(perm_ref[...], u.astype(BF16)).astype(BF16)
    _ssm_in_proj(ub_tb, wb_re_ref, wb_im_ref, bu_re, bu_im)

    gp = bu_re.shape[1]
    for c0 in range(0, gp, lane_chunk):
        sl = slice(c0, c0 + lane_chunk)
        lre = jnp.broadcast_to(lam_re_ref[:, sl], (bsz, lane_chunk))
        lim = jnp.broadcast_to(lam_im_ref[:, sl], (bsz, lane_chunk))
        hr = st_re_ref[:, sl]
        hi = st_im_ref[:, sl]
        for t in range(lc):
            rows = slice(t * bsz, (t + 1) * bsz)
            hr, hi = (lre * hr - lim * hi + bu_re[rows, sl], lre * hi + lim * hr + bu_im[rows, sl])
            bu_re[rows, sl] = hr
            bu_im[rows, sl] = hi
        st_re_ref[:, sl] = hr
        st_im_ref[:, sl] = hi

    permt = permt_ref[...]
    hi, mid, lo = _split3(_ssm_out_proj(bu_re, bu_im, wc_re_ref, wc_imn_ref))
    y = (_dot(permt, hi) + _dot(permt, mid)) + _dot(permt, lo)
    out_ref[...] = _ssm_glu(x, u, y, d_ref, wglu_ref).reshape(bsz, lc, d)


def _ssm_sample_kernel(x_ref, g_ref, lam_re_ref, lam_im_ref, h0_re_ref, h0_im_ref, wb_re_ref, wb_im_ref,
                       wc_re_ref, wc_imn_ref, d_ref, wglu_ref,
                       out_ref, h_re_ref, h_im_ref):
    x = x_ref[...]
    u = _rms(x, g_ref[...])
    _ssm_in_proj(u.astype(BF16), wb_re_ref, wb_im_ref, h_re_ref, h_im_ref)
    lre = lam_re_ref[...]
    lim = lam_im_ref[...]
    h0r = h0_re_ref[...]
    h0i = h0_im_ref[...]
    h_re_ref[...] = lre * h0r - lim * h0i + h_re_ref[...]
    h_im_ref[...] = lre * h0i + lim * h0r + h_im_ref[...]
    y = _ssm_out_proj(h_re_ref, h_im_ref, wc_re_ref, wc_imn_ref)
    n = x.shape[0]
    out_ref[0:n, :] = _ssm_glu(x, u, y, d_ref, wglu_ref)
    out_ref[n:, :] = jnp.zeros((out_ref.shape[0] - n, out_ref.shape[1]), F32)


def _ssm_prompt(x, g, lam_re, lam_im, wb_re, wb_im, wc_re, wc_imn, d_skip, wglu, *, lc, nsub):
    bsz, seq, d = x.shape
    assert seq % (lc * nsub) == 0
    gp = lam_re.shape[1]
    r = lc * bsz
    perm = np.zeros((r, r), np.float32)
    for b in range(bsz):
        for t in range(lc):
            perm[t * bsz + b, b * lc + t] = 1.0
    permt = jnp.asarray(perm.T, BF16)
    perm = jnp.asarray(perm, BF16)
    kern = functools.partial(_ssm_prompt_kernel, lc=lc, bsz=bsz, lane_chunk=8 * LANES)
    return pl.pallas_call(
        kern,
        grid=(seq // (lc * nsub),),
        in_specs=[pl.BlockSpec((bsz, lc * nsub, d), lambda c: (0, c, 0)),
                  _const_spec(g.shape), _const_spec(lam_re.shape), _const_spec(lam_im.shape),
                  _const_spec(wb_re.shape), _const_spec(wb_im.shape),
                  _const_spec(wc_re.shape), _const_spec(wc_imn.shape),
                  _const_spec(d_skip.shape), _const_spec(wglu.shape),
                  _const_spec(perm.shape), _const_spec(permt.shape)],
        out_specs=[pl.BlockSpec((bsz, lc * nsub, d), lambda c: (0, c, 0)),
                   _const_spec((bsz, gp)), _const_spec((bsz, gp))],
        out_shape=[jax.ShapeDtypeStruct((bsz, seq, d), F32),
                   jax.ShapeDtypeStruct((bsz, gp), F32), jax.ShapeDtypeStruct((bsz, gp), F32)],
        scratch_shapes=[pltpu.VMEM((nsub, r, gp), F32), pltpu.VMEM((nsub, r, gp), F32)],
        compiler_params=_params("arbitrary"),
        name="s5_prompt",
    )(x, g, lam_re, lam_im, wb_re, wb_im, wc_re, wc_imn, d_skip, wglu, perm, permt)


def _ssm_sample(x, g, lam_re, lam_im, h0_re, h0_im, wb_re, wb_im, wc_re, wc_imn, d_skip, wglu, *, pad_rows):
    n, d = x.shape
    gp = lam_re.shape[1]
    return pl.pallas_call(
        _ssm_sample_kernel,
        out_shape=[jax.ShapeDtypeStruct((pad_rows, d), F32),
                   jax.ShapeDtypeStruct((n, gp), F32), jax.ShapeDtypeStruct((n, gp), F32)],
        compiler_params=_params(),
        name="s5_sample",
    )(x, g, lam_re, lam_im, h0_re, h0_im, wb_re, wb_im, wc_re, wc_imn, d_skip, wglu)


def _ffn_chunks(d_ff):
    step = 3 * MXU_DIM
    return [(c, min(c + step, d_ff)) for c in range(0, d_ff, step)]


def _ffn_kernel(xp_ref, xs_ref, gffn_ref, wgu_ref, wd_ref, gkv_ref, wkv_ref, gq_ref, wq_ref,
                x2_ref, kv_ref, q_ref):
    x = jnp.where(pl.program_id(0) < pl.num_programs(0) - 1, xp_ref[...], xs_ref[...])
    hb = _rms(x, gffn_ref[...]).astype(BF16)
    d_ff = wd_ref.shape[0]
    acc = None
    for c0, c1 in _ffn_chunks(d_ff):
        a = _dot(hb, wgu_ref[:, c0:c1])
        b = _dot(hb, wgu_ref[:, d_ff + c0:d_ff + c1])
        part = _dot((jax.nn.silu(a) * b).astype(BF16), wd_ref[c0:c1, :])
        acc = part if acc is None else acc + part
    x2 = x + acc
    x2_ref[...] = x2
    kv_ref[...] = _dot(_rms(x2, gkv_ref[...]).astype(BF16), wkv_ref[...])
    q_ref[...] = _dot(_rms(x2, gq_ref[...]).astype(BF16), wq_ref[...]).astype(BF16)


def _ffn(xp, xs, gffn, wgu, wd, gkv, wkv, gq, wq):
    tm, d = xs.shape
    npt = xp.shape[0] // tm
    rows = xp.shape[0] + tm
    kvw = wkv.shape[1]
    nq = wq.shape[1]
    return pl.pallas_call(
        _ffn_kernel,
        grid=(npt + 1,),
        in_specs=[pl.BlockSpec((tm, d), lambda i: (jnp.minimum(i, npt - 1), 0)),
                  _const_spec(xs.shape),
                  _const_spec(gffn.shape), _const_spec(wgu.shape), _const_spec(wd.shape),
                  _const_spec(gkv.shape), _const_spec(wkv.shape),
                  _const_spec(gq.shape), _const_spec(wq.shape)],
        out_specs=[pl.BlockSpec((tm, d), lambda i: (i, 0)),
                   pl.BlockSpec((tm, kvw), lambda i: (i, 0)),
                   pl.BlockSpec((tm, nq), lambda i: (i, 0))],
        out_shape=[jax.ShapeDtypeStruct((rows, d), F32),
                   jax.ShapeDtypeStruct((rows, kvw), F32),
                   jax.ShapeDtypeStruct((rows, nq), BF16)],
        compiler_params=_params("arbitrary"),
        name="ffn_kv_q",
    )(xp, xs, gffn, wgu, wd, gkv, wkv, gq, wq)


def _t5_bucket(dist, num_buckets):
    max_exact = num_buckets // 2
    d = jnp.maximum(dist, 0)
    large = max_exact + (jnp.log(jnp.maximum(d, 1).astype(F32) / max_exact)
                         / math.log(MAX_DISTANCE / max_exact) * (num_buckets - max_exact)).astype(jnp.int32)
    large = jnp.minimum(large, num_buckets - 1)
    return jnp.where(d < max_exact, d, large)


def _bias_kernel(bm_ref, bs_ref, rb_ref, bias_ref, bias_s_ref, *, kvh, rep):
    nb, nh = rb_ref.shape
    bm = bm_ref[...]
    bs = bs_ref[...]

    def lookup(buckets, h):
        terms = [jnp.where(buckets == k, rb_ref[k, h], 0.0) for k in range(nb)]
        while len(terms) > 1:
            terms = [a + b for a, b in zip(terms[0::2], terms[1::2])] + (terms[-1:] if len(terms) % 2 else [])
        return terms[0]

    w = bm.shape[0]
    for g in range(kvh):
        for r in range(rep):
            bias_ref[g, :, r * w:(r + 1) * w] = lookup(bm, g * rep + r) * math.log2(math.e)
            bias_s_ref[g, r:r + 1, :] = lookup(bs, g * rep + r)


def _bias_tables(rel_bias, window, kvh, rep):
    nb, nh = rel_bias.shape
    key = jnp.arange(window)[:, None]
    qry = jnp.arange(window)[None, :]
    bm = _t5_bucket((qry - key) % window, nb).astype(jnp.int32)
    bs = _t5_bucket(window - 1 - qry, nb).astype(jnp.int32)
    kern = functools.partial(_bias_kernel, kvh=kvh, rep=rep)
    return pl.pallas_call(
        kern,
        in_specs=[pl.BlockSpec(memory_space=pltpu.VMEM), pl.BlockSpec(memory_space=pltpu.VMEM),
                  pl.BlockSpec(memory_space=pltpu.SMEM)],
        out_shape=[jax.ShapeDtypeStruct((kvh, window, rep * window), F32),
                   jax.ShapeDtypeStruct((kvh, rep, window), F32)],
        name="t5_bias",
    )(bm, bs, rel_bias)


def _attn_prompt_kernel(q_ref, kvc_ref, kvp_ref, bias_ref, sink_ref, ot_ref, *, kvh, rep, hd, scale):
    w = kvp_ref.shape[0]
    kv_all = kvc_ref[...]
    for sub in range(q_ref.shape[0] // w):
        rows = slice(sub * w, (sub + 1) * w)
        kvp = kvp_ref[...] if sub == 0 else kv_all[(sub - 1) * w:sub * w]
        first = pl.program_id(1) == 0 if sub == 0 else None
        _attn_prompt_block(q_ref[rows, :], kv_all[rows], kvp, first, bias_ref, sink_ref, ot_ref, sub * w,
                           kvh=kvh, rep=rep, hd=hd, scale=scale)


def _attn_prompt_block(q, kvc, kvp, first, bias_ref, sink_ref, ot_ref, col0, *, kvh, rep, hd, scale):
    w = q.shape[0]
    kvw = kvh * hd
    lane = lax.broadcasted_iota(jnp.int32, (1, kvw), 1)
    key = lax.broadcasted_iota(jnp.int32, (w, rep * w), 0)
    qry = lax.broadcasted_iota(jnp.int32, (w, rep * w), 1) % w
    upper = key > qry
    mask_add = None if first is None else jnp.where(jnp.logical_and(upper, first), NEG_INF, 0.0)
    log2e = math.log2(math.e)
    heads_per_blk = LANES // hd
    lane_blocks = kvw // LANES
    kbs = []
    for half in range(heads_per_blk):
        lmask = (lane % LANES) // hd == half
        kbs += [jnp.where(lmask, kvp[:, :kvw], 0.0), jnp.where(lmask, kvc[:, :kvw], 0.0)]
    qrows = []
    for p in range(lane_blocks):
        qg = jnp.concatenate([q[:, r * kvw + p * LANES:r * kvw + (p + 1) * LANES] for r in range(rep)], axis=0)
        zero = jnp.zeros_like(qg)
        qrows.append(jnp.concatenate([qg if other == p else zero for other in range(lane_blocks)], axis=1))
    s_full = lax.dot_general(jnp.concatenate(kbs, axis=0).astype(BF16), jnp.concatenate(qrows, axis=0),
                             (((1,), (1,)), ((), ())), preferred_element_type=F32)
    for p in range(lane_blocks):
        vs = slice(kvw + p * LANES, kvw + (p + 1) * LANES)
        vb = jnp.concatenate([kvp[:, vs], kvc[:, vs]], axis=0).astype(BF16)
        s_all = s_full[:, p * rep * w:(p + 1) * rep * w]
        pcats, denoms = [], []
        for half in range(heads_per_blk):
            g = p * heads_per_blk + half
            s = s_all[half * 2 * w:(half + 1) * 2 * w]
            bias = bias_ref[g] if first is None else bias_ref[g] + mask_add
            sc = jnp.where(upper, s[:w], s[w:]) * (scale * log2e) + bias
            sink = jnp.concatenate([jnp.full((1, w), sink_ref[g * rep + r] * log2e, F32) for r in range(rep)],
                                   axis=1)
            m = jnp.maximum(jnp.max(sc, axis=0, keepdims=True), sink)
            pe = jnp.exp2(sc - m)
            denoms.append(jnp.sum(pe, axis=0, keepdims=True) + jnp.exp2(sink - m))
            pcats.append(jnp.concatenate([jnp.where(upper, pe, 0.0), jnp.where(upper, 0.0, pe)], axis=0).astype(BF16))
        og_all = lax.dot_general(vb, jnp.concatenate(pcats, axis=1), (((0,), (0,)), ((), ())),
                                 preferred_element_type=F32)
        halves = [og_all[half * hd:(half + 1) * hd, half * rep * w:(half + 1) * rep * w] / denoms[half]
                  for half in range(heads_per_blk)]
        o_blk = jnp.concatenate(halves, axis=0).astype(BF16)
        for r in range(rep):
            ot_ref[r * kvw + p * LANES:r * kvw + (p + 1) * LANES, col0:col0 + w] = o_blk[:, r * w:(r + 1) * w]


def _attn_prompt(q, kv, bias, sinks, *, bsz, nblk, window, kvh, rep, hd):
    rows, nq = q.shape
    kvw2 = kv.shape[1]
    nsub = next(n for n in (4, 2, 1) if nblk % n == 0 and rows % (n * window) == 0)
    nstep = nblk // nsub
    kern = functools.partial(_attn_prompt_kernel, kvh=kvh, rep=rep, hd=hd, scale=1.0 / math.sqrt(hd))
    return pl.pallas_call(
        kern,
        grid=(bsz, nstep),
        in_specs=[pl.BlockSpec((nsub * window, nq), lambda b, i: (b * nstep + i, 0)),
                  pl.BlockSpec((nsub * window, kvw2), lambda b, i: (b * nstep + i, 0)),
                  pl.BlockSpec((window, kvw2), lambda b, i: (b * nblk + jnp.maximum(nsub * i - 1, 0), 0)),
                  _const_spec(bias.shape),
                  pl.BlockSpec(memory_space=pltpu.SMEM)],
        out_specs=pl.BlockSpec((nq, nsub * window), lambda b, i: (0, b * nstep + i)),
        out_shape=jax.ShapeDtypeStruct((nq, rows), BF16),
        compiler_params=_params("arbitrary", "arbitrary"),
        name="swa_prompt",
    )(q, kv, kv, bias, sinks)


def _attn_sample_kernel(q_ref, kv_ref, ck_ref, cv_ref, bias_ref, sink_ref, o_ref, nk_ref, nv_ref,
                        *, kvh, rep, hd, scale):
    nb, w, kvw = ck_ref.shape
    kv = kv_ref[...]
    lane = lax.broadcasted_iota(jnp.int32, (nb, rep, kvw), 2)

    def shifted(c_ref, n_ref, new):
        flat = c_ref[...].reshape(nb * w, kvw)
        n_ref[...] = pltpu.roll(flat, nb * w - 1, axis=0).reshape(nb, w, kvw)
        n_ref[:, w - 1:w, :] = new
        return n_ref[...].astype(BF16)

    nkb = shifted(ck_ref, nk_ref, kv[:, :, :kvw])
    nvb = shifted(cv_ref, nv_ref, kv[:, :, kvw:])
    q = q_ref[...].astype(F32)
    masks = [jnp.logical_and(lane >= g * hd, lane < (g + 1) * hd) for g in range(kvh)]
    qall = jnp.concatenate([jnp.where(masks[g], q, 0.0) for g in range(kvh)], axis=1).astype(BF16)
    s = jnp.einsum("nmc,njc->nmj", qall, nkb, preferred_element_type=F32)
    sc = s * scale + bias_ref[...][None]
    sink = sink_ref[...][None]
    m = jnp.maximum(jnp.max(sc, axis=-1, keepdims=True), sink)
    pe = jnp.exp(sc - m)
    probs = pe / (jnp.sum(pe, axis=-1, keepdims=True) + jnp.exp(sink - m))
    og = jnp.einsum("nmj,njc->nmc", probs.astype(BF16), nvb, preferred_element_type=F32)
    o = jnp.zeros((nb, rep, kvw), F32)
    for g in range(kvh):
        o = jnp.where(masks[g], og[:, g * rep:(g + 1) * rep, :], o)
    o_ref[...] = o.astype(BF16)


def _attn_sample(q3, kv3, ck, cv, bias_s, sink_s, *, nb, kvh, rep, hd):
    n, w, kvw = ck.shape
    kern = functools.partial(_attn_sample_kernel, kvh=kvh, rep=rep, hd=hd, scale=1.0 / math.sqrt(hd))
    cache_spec = pl.BlockSpec((nb, w, kvw), lambda i: (i, 0, 0))
    return pl.pallas_call(
        kern,
        grid=(n // nb,),
        in_specs=[pl.BlockSpec((nb, rep, kvw), lambda i: (i, 0, 0)),
                  pl.BlockSpec((nb, 1, 2 * kvw), lambda i: (i, 0, 0)),
                  cache_spec, cache_spec,
                  _const_spec(bias_s.shape), _const_spec(sink_s.shape)],
        out_specs=[pl.BlockSpec((nb, rep, kvw), lambda i: (i, 0, 0)), cache_spec, cache_spec],
        out_shape=[jax.ShapeDtypeStruct((n, rep, kvw), BF16),
                   jax.ShapeDtypeStruct((n, w, kvw), F32), jax.ShapeDtypeStruct((n, w, kvw), F32)],
        compiler_params=_params("arbitrary"),
        name="swa_sample",
    )(q3, kv3, ck, cv, bias_s, sink_s)


CHUNK = 16
BIG_COPY = 128
SUB = 256
UP_ROW_BLOCK = 1024
DOWN_ROW_BLOCK = 1024
TOKEN_TILE = 512
MOE_TILE = 512


def _chunk_copy(src, src_row, dst, dst_row, sem, rows=CHUNK):
    return pltpu.make_async_copy(src.at[pl.ds(pl.multiple_of(src_row, CHUNK), rows), :],
                                 dst.at[pl.ds(pl.multiple_of(dst_row, CHUNK), rows), :], sem)


def _start_copies(src, src_row, dst, dst_row, sem, nrows):
    nbig = lax.shift_right_logical(nrows, int(math.log2(BIG_COPY)))

    def big(c, carry):
        _chunk_copy(src, src_row + c * BIG_COPY, dst, dst_row + c * BIG_COPY, sem, BIG_COPY).start()
        return carry

    lax.fori_loop(0, nbig, big, 0)
    done = nbig * BIG_COPY

    def small(c, carry):
        _chunk_copy(src, src_row + done + c * CHUNK, dst, dst_row + done + c * CHUNK, sem).start()
        return carry

    lax.fori_loop(0, lax.shift_right_logical(nrows - done, int(math.log2(CHUNK))), small, 0)


def _wait_chunks(src, dst, sem, nchunks):
    per_big = BIG_COPY // CHUNK
    nbig = lax.shift_right_logical(nchunks, int(math.log2(per_big)))

    def big(c, carry):
        _chunk_copy(src, 0, dst, 0, sem, BIG_COPY).wait()
        return carry

    lax.fori_loop(0, nbig, big, 0)

    def small(c, carry):
        _chunk_copy(src, 0, dst, 0, sem).wait()
        return carry

    lax.fori_loop(0, nchunks - nbig * per_big, small, 0)


def _moe_route_kernel(x2_ref, ot_ref, wo_ref, gffn_ref, wr_ref, lst_ref, ust_ref,
                      x3_ref, cm_ref, tmeta_ref, xs_hbm,
                      comp_s, carry_s, zero_s, base_v, pend_sm, sem, *, ne, cap, n_valid):
    i = pl.program_id(0)
    nt = pl.num_programs(0)
    tm = x2_ref.shape[0]
    w = comp_s.shape[1]
    slot = lax.rem(i, 2)

    @pl.when(i == 0)
    def _():
        base_v[...] = jnp.zeros_like(base_v)
        zero_s[...] = jnp.zeros_like(zero_s)
        carry_s[...] = jnp.zeros_like(carry_s)

    def drain(n):
        _wait_chunks(comp_s.at[0], xs_hbm, sem, n)

    @pl.when(i > 0)
    def _():
        drain(pend_sm[0])

    x3 = x2_ref[...] + lax.dot_general(ot_ref[...], wo_ref[...], (((0,), (0,)), ((), ())),
                                       preferred_element_type=F32)
    x3_ref[...] = x3
    hb = _rms(x3, gffn_ref[...]).astype(BF16)

    lane = lax.broadcasted_iota(jnp.int32, (tm, LANES), 1)
    logits = jnp.where(lane < ne, _dot(hb, wr_ref[...]), -jnp.inf)
    m1 = jnp.max(logits, axis=-1, keepdims=True)
    i1 = jnp.min(jnp.where(logits == m1, lane, LANES), axis=-1, keepdims=True)
    rest = jnp.where(lane == i1, -jnp.inf, logits)
    m2 = jnp.max(rest, axis=-1, keepdims=True)
    i2 = jnp.min(jnp.where(rest == m2, lane, LANES), axis=-1, keepdims=True)
    e2 = jnp.exp(m2 - m1)
    g1 = 1.0 / (1.0 + e2)
    g2 = e2 / (1.0 + e2)

    live = i * tm + lax.broadcasted_iota(jnp.int32, (tm, 1), 0) < n_valid
    sel = jnp.where(jnp.logical_and(live, jnp.logical_or(lane == i1, lane == i2)), 1.0, 0.0)
    rank = _dot(lst_ref[...], sel.astype(BF16))
    cnt = jnp.sum(sel, axis=0, keepdims=True)
    fill = base_v[...]
    rem = fill - jnp.floor(fill / CHUNK) * CHUNK
    cpad = jnp.floor((rem + cnt + (CHUNK - 1)) / CHUNK) * CHUNK
    loff = _dot(jnp.broadcast_to(cpad, (8, LANES)).astype(BF16), ust_ref[...])[0:1]
    dest = loff + rem + rank
    ld1 = jnp.where(live, jnp.sum(jnp.where(lane == i1, dest, 0.0), axis=-1, keepdims=True), -1.0)
    ld2 = jnp.where(live, jnp.sum(jnp.where(lane == i2, dest, 0.0), axis=-1, keepdims=True), -1.0)
    cm = jnp.where(lane == 0, ld1, jnp.where(lane == 1, ld2, jnp.where(lane == 2, g1, jnp.where(lane == 3, g2, 0.0))))
    cm_ref[...] = cm

    rm = cm.T
    rowi = lax.broadcasted_iota(jnp.int32, (w, tm), 0).astype(F32)
    place = (jnp.where(rowi == rm[0:1], 1.0, 0.0) + jnp.where(rowi == rm[1:2], 1.0, 0.0)).astype(BF16)
    comp_s[slot] = _dot(place, hb).astype(BF16)

    base = fill - rem
    srow = lax.broadcasted_iota(jnp.int32, (8, LANES), 0)
    tmeta_ref[...] = jnp.where(srow == 0, cpad, jnp.where(srow == 1, base, jnp.where(srow == 2, fill + cnt, 0.0))
                               ).astype(jnp.int32)
    base_v[...] = fill + cnt

    cpad_i = cpad.astype(jnp.int32)
    loff_i = loff.astype(jnp.int32)
    base_i = base.astype(jnp.int32)
    tail_i = (rem + cnt - jnp.floor((rem + cnt) / CHUNK) * CHUNK).astype(jnp.int32)
    total = 0
    tails = []
    for e in range(ne):
        n_e = cpad_i[0, e]
        src0 = loff_i[0, e]
        dst0 = base_i[0, e] + e * cap

        @pl.when(n_e > 0)
        def _(e=e, n_e=n_e, src0=src0, partial=tail_i[0, e] > 0):
            head = pl.ds(pl.multiple_of(src0, CHUNK), CHUNK)
            comp_s[slot, head, :] = comp_s[slot, head, :] + carry_s[e]
            last = comp_s[slot, pl.ds(pl.multiple_of(src0 + n_e - CHUNK, CHUNK), CHUNK), :]
            carry_s[e] = jnp.where(partial, last, jnp.zeros_like(last))

        _start_copies(comp_s.at[slot], src0, xs_hbm, dst0, sem, n_e)
        total = total + lax.shift_right_logical(n_e, int(math.log2(CHUNK)))
        tails.append(dst0 + n_e)
    pend_sm[0] = total

    @pl.when(i == nt - 1)
    def _():
        nz_total = 0
        for e in range(ne):
            end = tails[e]
            nz = lax.shift_right_logical(lax.rem(SUB - lax.rem(end, SUB), SUB), int(math.log2(CHUNK)))

            def zstart(c, carry, end=end):
                _chunk_copy(zero_s, 0, xs_hbm, end + c * CHUNK, sem).start()
                return carry

            lax.fori_loop(0, nz, zstart, 0)
            nz_total = nz_total + nz
        drain(total + nz_total)


def _moe_route(x2, ot, wo, gffn, wr_pad, *, tm, ne, cap, n_valid):
    rows, d = x2.shape
    nt = rows // tm
    w = _staging_rows(tm, ne)
    lst = jnp.asarray(np.tril(np.ones((tm, tm), np.float32), -1), BF16)
    ust = jnp.asarray(np.triu(np.ones((LANES, LANES), np.float32), 1), BF16)
    kern = functools.partial(_moe_route_kernel, ne=ne, cap=cap, n_valid=n_valid)
    return pl.pallas_call(
        kern,
        grid=(nt,),
        in_specs=[pl.BlockSpec((tm, d), lambda i: (i, 0)),
                  pl.BlockSpec((ot.shape[0], tm), lambda i: (0, i)),
                  _const_spec(wo.shape), _const_spec(gffn.shape), _const_spec(wr_pad.shape),
                  _const_spec(lst.shape), _const_spec(ust.shape)],
        out_specs=[pl.BlockSpec((tm, d), lambda i: (i, 0)),
                   pl.BlockSpec((tm, LANES), lambda i: (i, 0)),
                   pl.BlockSpec((8, LANES), lambda i: (i, 0)),
                   pl.BlockSpec(memory_space=pl.ANY)],
        out_shape=[jax.ShapeDtypeStruct((rows, d), F32),
                   jax.ShapeDtypeStruct((rows, LANES), F32),
                   jax.ShapeDtypeStruct((nt * 8, LANES), jnp.int32),
                   jax.ShapeDtypeStruct((ne * cap, d), BF16)],
        scratch_shapes=[pltpu.VMEM((2, w, d), BF16), pltpu.VMEM((ne, CHUNK, d), BF16), pltpu.VMEM((CHUNK, d), BF16),
                        pltpu.VMEM((1, LANES), F32), pltpu.SMEM((1,), jnp.int32),
                        pltpu.SemaphoreType.DMA(())],
        compiler_params=_params("arbitrary"),
        name="moe_route",
    )(x2, ot, wo, gffn, wr_pad, lst, ust)


def _staging_rows(tm, ne):
    return -(-(TOP_K * tm + 2 * ne * (CHUNK - 1)) // LANES) * LANES


def _max_blocks(rows, ne, row_block):
    return TOP_K * rows // row_block + ne


def _expert_steps(nrows, nf, max_blocks, row_block):
    ne = nrows.shape[0]
    nblk = (nrows + row_block - 1) // row_block
    cum = jnp.cumsum(nblk) * nf
    total = cum[-1]
    s = jnp.minimum(jnp.arange(max_blocks * nf, dtype=jnp.int32), total - 1)
    e = jnp.minimum(jnp.sum((s[:, None] >= cum[None, :]).astype(jnp.int32), axis=1), ne - 1)
    nb_e = nblk[e]
    within = s - (cum[e] - nb_e * nf)
    f = within // nb_e
    pos = within - f * nb_e
    r = (pos + nb_e - 1) % nb_e
    nsub = jnp.clip((nrows[e] - r * row_block + SUB - 1) // SUB, 0, row_block // SUB)
    first = (pos == 0).astype(jnp.int32)
    return (e, f.astype(jnp.int32), r.astype(jnp.int32), first, nsub.astype(jnp.int32),
            jnp.reshape(total, (1,)).astype(jnp.int32))


def _moe_up_kernel(e_ref, f_ref, r_ref, first_ref, nsub_ref, n_ref, xs_ref, wg_ref, wu_ref, act_ref, wgb_s, wub_s):
    s = pl.program_id(0)

    @pl.when(s < n_ref[0])
    def _():
        @pl.when(first_ref[s] == 1)
        def _():
            wgb_s[...] = wg_ref[...].astype(BF16)
            wub_s[...] = wu_ref[...].astype(BF16)

        def gate_up(row0, nrows):
            rows = pl.ds(pl.multiple_of(row0, SUB), nrows)
            xsb = xs_ref[rows, :]
            a = _dot(xsb, wgb_s[...])
            b = _dot(xsb, wub_s[...])
            act_ref[rows, :] = (jax.nn.silu(a) * b).astype(BF16)

        nsub = nsub_ref[s]
        full = xs_ref.shape[0] // SUB

        @pl.when(nsub == full)
        def _():
            xsb = xs_ref[...]
            for c0 in range(0, act_ref.shape[1], MXU_DIM):
                cols = slice(c0, c0 + MXU_DIM)
                a = _dot(xsb, wgb_s[:, cols])
                b = _dot(xsb, wub_s[:, cols])
                act_ref[:, cols] = (jax.nn.silu(a) * b).astype(BF16)

        @pl.when(nsub < full)
        def _():
            npair = lax.shift_right_logical(nsub, 1)

            def body(j, carry):
                gate_up(j * (2 * SUB), 2 * SUB)
                return carry

            lax.fori_loop(0, npair, body, 0)

            @pl.when(nsub > 2 * npair)
            def _():
                gate_up(npair * (2 * SUB), SUB)


def _moe_up(steps, xs, wgu, *, fc, cap, row_block):
    ne, d, dff2 = wgu.shape
    dff = dff2 // 2
    nf = dff // fc
    cb = cap // row_block
    grid_spec = pltpu.PrefetchScalarGridSpec(
        num_scalar_prefetch=6,
        grid=(steps[0].shape[0],),
        in_specs=[pl.BlockSpec((row_block, d), lambda s, e, f, r, *_: (e[s] * cb + r[s], 0)),
                  pl.BlockSpec((None, d, fc), lambda s, e, f, r, *_: (e[s], 0, f[s])),
                  pl.BlockSpec((None, d, fc), lambda s, e, f, r, *_: (e[s], 0, nf + f[s]))],
        out_specs=pl.BlockSpec((row_block, fc), lambda s, e, f, r, *_: (e[s] * cb + r[s], f[s])),
        scratch_shapes=[pltpu.VMEM((d, fc), BF16), pltpu.VMEM((d, fc), BF16)])
    return pl.pallas_call(
        _moe_up_kernel,
        grid_spec=grid_spec,
        out_shape=jax.ShapeDtypeStruct((ne * cap, dff), BF16),
        compiler_params=_params("arbitrary"),
        name="moe_up",
    )(*steps, xs, wgu, wgu)


def _moe_down_kernel(e_ref, f_ref, r_ref, first_ref, nsub_ref, n_ref, act_ref, wd_ref, ys_ref, wdb_s):
    s = pl.program_id(0)

    @pl.when(s < n_ref[0])
    def _():
        @pl.when(first_ref[s] == 1)
        def _():
            wdb_s[...] = wd_ref[...].astype(BF16)

        def project(row0, nrows):
            rows = pl.ds(pl.multiple_of(row0, SUB), nrows)
            ys_ref[rows, :] = _dot(act_ref[rows, :], wdb_s[...]).astype(BF16)

        nsub = nsub_ref[s]
        full = act_ref.shape[0] // SUB

        @pl.when(nsub == full)
        def _():
            ab = act_ref[...]
            for c0 in range(0, ys_ref.shape[1], MXU_DIM):
                cols = slice(c0, c0 + MXU_DIM)
                ys_ref[:, cols] = _dot(ab, wdb_s[:, cols]).astype(BF16)

        @pl.when(nsub < full)
        def _():
            npair = lax.shift_right_logical(nsub, 1)

            def body(j, carry):
                project(j * (2 * SUB), 2 * SUB)
                return carry

            lax.fori_loop(0, npair, body, 0)

            @pl.when(nsub > 2 * npair)
            def _():
                project(npair * (2 * SUB), SUB)


def _moe_down(steps, act, wd, *, cap, row_block):
    ne, dff, d = wd.shape
    cb = cap // row_block
    grid_spec = pltpu.PrefetchScalarGridSpec(
        num_scalar_prefetch=6,
        grid=(steps[0].shape[0],),
        in_specs=[pl.BlockSpec((row_block, dff), lambda s, e, f, r, *_: (e[s] * cb + r[s], 0)),
                  pl.BlockSpec((None, dff, d), lambda s, e, f, r, *_: (e[s], 0, 0))],
        out_specs=pl.BlockSpec((row_block, d), lambda s, e, f, r, *_: (e[s] * cb + r[s], 0)),
        scratch_shapes=[pltpu.VMEM((dff, d), BF16)])
    return pl.pallas_call(
        _moe_down_kernel,
        grid_spec=grid_spec,
        out_shape=jax.ShapeDtypeStruct((ne * cap, d), BF16),
        compiler_params=_params("arbitrary", vmem=2 * (dff * d * 4 + row_block * (dff + d) * 2) + dff * d * 2
                                + 4 * SUB * d * 4),
        name="moe_down",
    )(*steps, act, wd)


def _moe_combine_kernel(cpad_ref, seg_ref, x3_ref, cm_ref, gfin_ref, ys_hbm, yp_ref, ys_ref, yloc_s, sem,
                        *, ne, npt):
    i = pl.program_id(0)
    nt = pl.num_programs(0)
    tm = x3_ref.shape[0]
    w = yloc_s.shape[1]
    slot = lax.rem(i, 2)
    shift = int(math.log2(CHUNK))

    def issue(t, sl):
        off = 0
        for e in range(ne):
            n_e = cpad_ref[t * ne + e]
            _start_copies(ys_hbm, seg_ref[t * ne + e], yloc_s.at[sl], off, sem.at[sl], n_e)
            off = off + n_e

    @pl.when(i == 0)
    def _():
        yloc_s[...] = jnp.zeros_like(yloc_s)
        issue(0, 0)

    @pl.when(i + 1 < nt)
    def _():
        issue(i + 1, 1 - slot)

    total = 0
    for e in range(ne):
        total = total + lax.shift_right_logical(cpad_ref[i * ne + e], shift)

    _wait_chunks(ys_hbm, yloc_s.at[slot], sem.at[slot], total)

    cm = cm_ref[...]
    col = lax.broadcasted_iota(jnp.int32, (tm, w), 1).astype(F32)
    yl = yloc_s[slot]
    mix = jnp.where(col == cm[:, 0:1], cm[:, 2:3], jnp.where(col == cm[:, 1:2], cm[:, 3:4], 0.0))
    y = _rms(x3_ref[...] + _dot(mix.astype(BF16), yl), gfin_ref[...])

    @pl.when(i < npt)
    def _():
        yp_ref[...] = y

    @pl.when(i >= npt)
    def _():
        ys_ref[...] = y


def _moe_combine(cpad, seg, x3, cm, gfin, ys, *, tm, ne, sample_rows):
    rows, d = x3.shape
    nt = rows // tm
    npt = (rows - sample_rows) // tm
    w = _staging_rows(tm, ne)
    kern = functools.partial(_moe_combine_kernel, ne=ne, npt=npt)
    grid_spec = pltpu.PrefetchScalarGridSpec(
        num_scalar_prefetch=2,
        grid=(nt,),
        in_specs=[pl.BlockSpec((tm, d), lambda i, *_: (i, 0)),
                  pl.BlockSpec((tm, LANES), lambda i, *_: (i, 0)),
                  pl.BlockSpec(gfin.shape, lambda i, *_: (0, 0)),
                  pl.BlockSpec(memory_space=pl.ANY)],
        out_specs=[pl.BlockSpec((tm, d), lambda i, *_: (jnp.minimum(i, npt - 1), 0)),
                   pl.BlockSpec((tm, d), lambda i, *_: (jnp.maximum(i - npt, 0), 0))],
        scratch_shapes=[pltpu.VMEM((2, w, d), BF16), pltpu.SemaphoreType.DMA((2,))])
    return pl.pallas_call(
        kern,
        grid_spec=grid_spec,
        out_shape=[jax.ShapeDtypeStruct((npt * tm, d), F32), jax.ShapeDtypeStruct((sample_rows, d), F32)],
        compiler_params=_params("arbitrary"),
        name="moe_combine",
    )(cpad, seg, x3, cm, gfin, ys)


def _moe(x2, ot, wo, gffn, wr, wgu, wd, gfin, *, tm, fc, n_valid, sample_rows):
    rows, d = x2.shape
    ne = wr.shape[1]
    nt = rows // tm
    nf = wgu.shape[2] // 2 // fc
    cap = -(-(rows + SUB) // UP_ROW_BLOCK) * UP_ROW_BLOCK
    wr_pad = jnp.pad(wr, ((0, 0), (0, LANES - ne))).astype(BF16)
    x3, cm, tmeta, xs = _moe_route(x2, ot, wo, gffn, wr_pad, tm=tm, ne=ne, cap=cap, n_valid=n_valid)
    tmeta = tmeta.reshape(nt, 8, LANES)[:, :, :ne]
    cpad = tmeta[:, 0, :].reshape(nt * ne)
    seg = (tmeta[:, 1, :] + jnp.arange(ne, dtype=jnp.int32)[None, :] * cap).reshape(nt * ne)
    nrows = tmeta[nt - 1, 2, :]
    up_steps = _expert_steps(nrows, nf, _max_blocks(rows, ne, UP_ROW_BLOCK), UP_ROW_BLOCK)
    act = _moe_up(up_steps, xs, wgu, fc=fc, cap=cap, row_block=UP_ROW_BLOCK)
    down_steps = _expert_steps(nrows, 1, _max_blocks(rows, ne, DOWN_ROW_BLOCK), DOWN_ROW_BLOCK)
    ys = _moe_down(down_steps, act, wd, cap=cap, row_block=DOWN_ROW_BLOCK)
    return _moe_combine(cpad, seg, x3, cm, gfin, ys, tm=tm, ne=ne, sample_rows=sample_rows)


def kernel(x_prompt, x_sample, state_ssm_re, state_ssm_im, cache_k_win, cache_v_win, g_mix, g_ffn, g_kv, g_final, ssm_a_re, ssm_a_im, ssm_log_dt, ssm_b_re, ssm_b_im, ssm_c_re, ssm_c_im, ssm_d, w_glu, w_kv, w_q, w_o, attn_sinks, rel_bias, w_ffn_gate_up, w_ffn_down, w_router, w_exp_gate_up, w_exp_down):
    bsz, seq, d = x_prompt.shape
    ns, dec_seq, _ = x_sample.shape
    assert dec_seq == 1 and g_mix.shape[0] == 2 and ssm_a_re.shape[0] == 1 and w_q.shape[0] == 1
    _, g, p = ssm_a_re.shape
    gp = g * p
    window, kvh, hd = cache_k_win.shape[1:]
    kvw = kvh * hd
    nh = attn_sinks.shape[1]
    rep = nh // kvh
    nq = nh * hd
    assert bsz == 8 and ns % bsz == 0 and seq % window == 0 and LANES % hd == 0

    lam_re, lam_im, wb_re, wb_im, wc_re, wc_imn = _zoh(ssm_a_re[0], ssm_a_im[0], ssm_log_dt[0],
                                                        ssm_b_re[0], ssm_b_im[0], ssm_c_re[0], ssm_c_im[0])
    wglu = w_glu[0].astype(BF16)
    d_skip = ssm_d[0].reshape(1, d)
    wgu = w_ffn_gate_up[0].astype(BF16)
    wd = w_ffn_down[0].astype(BF16)
    wkv = w_kv.astype(BF16)
    wq = w_q[0].reshape(d, kvh, rep, hd).transpose(0, 2, 1, 3).reshape(d, nq).astype(BF16)
    wo = w_o[0].reshape(kvh, rep, hd, d).transpose(1, 0, 2, 3).reshape(nq, d).astype(BF16)
    bias, bias_s = _bias_tables(rel_bias, window, kvh, rep)
    bias_s = bias_s.reshape(nh, window)
    sinks = attn_sinks[0]
    sink_s = sinks.reshape(nh, 1)

    tm = TOKEN_TILE
    npr = seq * bsz
    assert npr % tm == 0 and ns <= tm

    x1_p, st_re, st_im = _ssm_prompt(x_prompt, g_mix[0:1], lam_re, lam_im, wb_re, wb_im, wc_re, wc_imn,
                                     d_skip, wglu, lc=32, nsub=2)
    x1_s, hs_re, hs_im = _ssm_sample(x_sample.reshape(ns, d), g_mix[0:1], lam_re, lam_im,
                                     state_ssm_re[0].reshape(ns, gp), state_ssm_im[0].reshape(ns, gp),
                                     wb_re, wb_im, wc_re, wc_imn, d_skip, wglu, pad_rows=tm)

    x2, kv, q = _ffn(x1_p.reshape(npr, d), x1_s, g_ffn[0:1], wgu, wd, g_kv.reshape(1, d), wkv, g_mix[1:2], wq)

    ot = _attn_prompt(q, kv, bias, sinks, bsz=bsz, nblk=seq // window, window=window, kvh=kvh, rep=rep, hd=hd)
    o_s, nk_s, nv_s = _attn_sample(q[npr:npr + ns].reshape(ns, rep, kvw), kv[npr:npr + ns].reshape(ns, 1, 2 * kvw),
                                   cache_k_win.reshape(ns, window, kvw), cache_v_win.reshape(ns, window, kvw),
                                   bias_s, sink_s, nb=32, kvh=kvh, rep=rep, hd=hd)
    ot = lax.dynamic_update_slice(ot, jnp.pad(o_s.reshape(ns, nq).T, ((0, 0), (0, tm - ns))), (0, npr))

    y_p, y_s = _moe(x2, ot, wo, g_ffn[1:2], w_router[0], w_exp_gate_up[0], w_exp_down[0], g_final.reshape(1, d),
                    tm=MOE_TILE, fc=w_exp_down.shape[2] // 2, n_valid=npr + ns, sample_rows=tm)

    y_prompt = y_p.reshape(bsz, seq, d)
    y_sample = y_s[:ns].reshape(ns, 1, d)
    kv_tail = jnp.stack([kv[(b + 1) * seq - window:(b + 1) * seq] for b in range(bsz)])
    kv_tail = kv_tail.reshape(bsz, window, 2, kvh, hd).transpose(2, 0, 1, 3, 4)
    return (y_prompt, y_sample,
            st_re.reshape(1, bsz, g, p), st_im.reshape(1, bsz, g, p), kv_tail[0], kv_tail[1],
            hs_re.reshape(1, ns, g, p), hs_im.reshape(1, ns, g, p),
            nk_s.reshape(ns, window, kvh, hd), nv_s.reshape(ns, window, kvh, hd))
```

```python
import functools
import math

import numpy as np
import jax
import jax.numpy as jnp
from jax import lax
from jax.experimental import pallas as pl
from jax.experimental.pallas import tpu as pltpu

F32 = jnp.float32
BF16 = jnp.bfloat16

EPS = 1e-6
NEG_INF = -1e30
TOP_K = 2
MAX_DISTANCE = 128
MXU_DIM = 256
LANES = 128
VMEM_LIMIT_BYTES = 56 * 1024 * 1024


def _dot(a, b):
    return jnp.dot(a, b, preferred_element_type=F32)


def _rms(x, g):
    return x * lax.rsqrt(jnp.mean(x * x, axis=-1, keepdims=True) + EPS) * g


def _const_spec(shape):
    nd = len(shape)
    return pl.BlockSpec(shape, lambda *_: (0,) * nd)


def _params(*sem, vmem=VMEM_LIMIT_BYTES):
    return pltpu.CompilerParams(dimension_semantics=sem, vmem_limit_bytes=vmem)


def _zoh_kernel(a_re_ref, a_im_ref, log_dt_ref, b_re_ref, b_im_ref, ct_re_ref, ct_im_ref,
                lam_re_ref, lam_im_ref, wb_re_ref, wb_im_ref, wc_re_ref, wc_imn_ref, *, hch, p):
    a_re = a_re_ref[...]
    a_im = a_im_ref[...]
    dt = jnp.exp(log_dt_ref[...])
    mag = jnp.exp(a_re * dt)
    lr = mag * jnp.cos(a_im * dt)
    li = mag * jnp.sin(a_im * dt)
    lam_re_ref[...] = lr
    lam_im_ref[...] = li
    nr = lr - 1.0
    den = a_re * a_re + a_im * a_im
    qr = (nr * a_re + li * a_im) / den
    qi = (li * a_re - nr * a_im) / den
    b_re = b_re_ref[...]
    b_im = b_im_ref[...]
    bb_re = qr * b_re - qi * b_im
    bb_im = qr * b_im + qi * b_re

    nkb, ublk, sblk = wb_re_ref.shape
    gpb = ublk // hch
    sh_h, sh_p = int(math.log2(hch)), int(math.log2(p))
    row = lax.broadcasted_iota(jnp.int32, (ublk, sblk), 0)
    col = lax.broadcasted_iota(jnp.int32, (ublk, sblk), 1)
    diag_in = lax.shift_right_logical(row, sh_h) == lax.shift_right_logical(col, sh_p)
    row = lax.broadcasted_iota(jnp.int32, (sblk, ublk), 0)
    col = lax.broadcasted_iota(jnp.int32, (sblk, ublk), 1)
    diag_out = lax.shift_right_logical(row, sh_p) == lax.shift_right_logical(col, sh_h)
    spread = jnp.where(lax.broadcasted_iota(jnp.int32, (hch, ublk), 0)
                       == jnp.bitwise_and(lax.broadcasted_iota(jnp.int32, (hch, ublk), 1), hch - 1),
                       1.0, 0.0).astype(BF16)
    for kb in range(nkb):
        cs = slice(kb * sblk, (kb + 1) * sblk)
        wb_re_ref[kb] = jnp.where(diag_in, jnp.concatenate([bb_re[:, cs]] * gpb, axis=0), 0.0).astype(BF16)
        wb_im_ref[kb] = jnp.where(diag_in, jnp.concatenate([bb_im[:, cs]] * gpb, axis=0), 0.0).astype(BF16)
        c_re = _dot(ct_re_ref[cs, :].astype(BF16), spread)
        c_im = _dot(ct_im_ref[cs, :].astype(BF16), spread)
        wc_re_ref[kb] = jnp.where(diag_out, c_re, 0.0).astype(BF16)
        wc_imn_ref[kb] = jnp.where(diag_out, -c_im, 0.0).astype(BF16)


def _zoh(a_re, a_im, log_dt, b_re, b_im, c_re, c_im):
    g, p = a_re.shape
    h = b_re.shape[-1]
    gp = g * p
    gpb = MXU_DIM // h
    nkb = g // gpb
    assert h & (h - 1) == 0 and p & (p - 1) == 0 and g % gpb == 0
    row = jax.ShapeDtypeStruct((1, gp), F32)
    wb = jax.ShapeDtypeStruct((nkb, gpb * h, gpb * p), BF16)
    wc = jax.ShapeDtypeStruct((nkb, gpb * p, gpb * h), BF16)
    return pl.pallas_call(
        functools.partial(_zoh_kernel, hch=h, p=p),
        out_shape=(row, row, wb, wb, wc, wc),
        name="s5_zoh",
    )(a_re.reshape(1, gp), a_im.reshape(1, gp), jnp.repeat(log_dt, p).reshape(1, gp),
      b_re.transpose(2, 0, 1).reshape(h, gp), b_im.transpose(2, 0, 1).reshape(h, gp),
      c_re.transpose(0, 2, 1).reshape(gp, h), c_im.transpose(0, 2, 1).reshape(gp, h))


def _ssm_in_proj(ub, wb_re_ref, wb_im_ref, bu_re, bu_im):
    nkb, ublk, sblk = wb_re_ref.shape
    for kb in range(nkb):
        ukb = ub[:, kb * ublk:(kb + 1) * ublk]
        bu_re[:, kb * sblk:(kb + 1) * sblk] = _dot(ukb, wb_re_ref[kb])
        bu_im[:, kb * sblk:(kb + 1) * sblk] = _dot(ukb, wb_im_ref[kb])


def _ssm_out_proj(h_re, h_im, wc_re_ref, wc_imn_ref):
    nkb, sblk, _ = wc_re_ref.shape
    ys = []
    for kb in range(nkb):
        hr = h_re[:, kb * sblk:(kb + 1) * sblk].astype(BF16)
        hi = h_im[:, kb * sblk:(kb + 1) * sblk].astype(BF16)
        ys.append(_dot(hr, wc_re_ref[kb]) + _dot(hi, wc_imn_ref[kb]))
    return jnp.concatenate(ys, axis=1)


def _ssm_glu(x, u, y, d_ref, wglu_ref):
    z = jax.nn.gelu(y + d_ref[...] * u).astype(BF16)
    gl = _dot(z, wglu_ref[...])
    d = x.shape[1]
    return x + gl[:, :d] * jax.nn.sigmoid(gl[:, d:])


def _split3(v):
    hi = v.astype(BF16)
    r1 = v - hi.astype(F32)
    mid = r1.astype(BF16)
    lo = (r1 - mid.astype(F32)).astype(BF16)
    return hi, mid, lo


def _ssm_prompt_kernel(x_ref, g_ref, lam_re_ref, lam_im_ref, wb_re_ref, wb_im_ref,
                       wc_re_ref, wc_imn_ref, d_ref, wglu_ref, perm_ref, permt_ref,
                       out_ref, st_re_ref, st_im_ref, bu_re_all, bu_im_all, *, lc, bsz, lane_chunk):
    @pl.when(pl.program_id(0) == 0)
    def _():
        st_re_ref[...] = jnp.zeros_like(st_re_ref)
        st_im_ref[...] = jnp.zeros_like(st_im_ref)

    for sub in range(bu_re_all.shape[0]):
        ts = slice(sub * lc, (sub + 1) * lc)
        _ssm_prompt_chunk(x_ref[:, ts, :], g_ref, lam_re_ref, lam_im_ref, wb_re_ref, wb_im_ref,
                          wc_re_ref, wc_imn_ref, d_ref, wglu_ref, perm_ref, permt_ref,
                          out_ref.at[:, ts, :], st_re_ref, st_im_ref, bu_re_all.at[sub], bu_im_all.at[sub],
                          lc=lc, bsz=bsz, lane_chunk=lane_chunk)


def _ssm_prompt_chunk(x3, g_ref, lam_re_ref, lam_im_ref, wb_re_ref, wb_im_ref,
                      wc_re_ref, wc_imn_ref, d_ref, wglu_ref, perm_ref, permt_ref,
                      out_ref, st_re_ref, st_im_ref, bu_re, bu_im, *, lc, bsz, lane_chunk):
    d = x3.shape[2]
    x = x3.reshape(bsz * lc, d)
    u = _rms(x, g_ref[...])
    ub_tb = _dot(perm_ref[...], u.astype(BF16)).astype(BF16)
    _ssm_in_proj(ub_tb, wb_re_ref, wb_im_ref, bu_re, bu_im)

    gp = bu_re.shape[1]
    for c0 in range(0, gp, lane_chunk):
        sl = slice(c0, c0 + lane_chunk)
        lre = jnp.broadcast_to(lam_re_ref[:, sl], (bsz, lane_chunk))
        lim = jnp.broadcast_to(lam_im_ref[:, sl], (bsz, lane_chunk))
        hr = st_re_ref[:, sl]
        hi = st_im_ref[:, sl]
        for t in range(lc):
            rows = slice(t * bsz, (t + 1) * bsz)
            hr, hi = (lre * hr - lim * hi + bu_re[rows, sl], lre * hi + lim * hr + bu_im[rows, sl])
            bu_re[rows, sl] = hr
            bu_im[rows, sl] = hi
        st_re_ref[:, sl] = hr
        st_im_ref[:, sl] = hi

    permt = permt_ref[...]
    hi, mid, lo = _split3(_ssm_out_proj(bu_re, bu_im, wc_re_ref, wc_imn_ref))
    y = (_dot(permt, hi) + _dot(permt, mid)) + _dot(permt, lo)
    out_ref[...] = _ssm_glu(x, u, y, d_ref, wglu_ref).reshape(bsz, lc, d)


def _ssm_sample_kernel(x_ref, g_ref, lam_re_ref, lam_im_ref, h0_re_ref, h0_im_ref, wb_re_ref, wb_im_ref,
                       wc_re_ref, wc_imn_ref, d_ref, wglu_ref,
                       out_ref, h_re_ref, h_im_ref):
    x = x_ref[...]
    u = _rms(x, g_ref[...])
    _ssm_in_proj(u.astype(BF16), wb_re_ref, wb_im_ref, h_re_ref, h_im_ref)
    lre = lam_re_ref[...]
    lim = lam_im_ref[...]
    h0r = h0_re_ref[...]
    h0i = h0_im_ref[...]
    h_re_ref[...] = lre * h0r - lim * h0i + h_re_ref[...]
    h_im_ref[...] = lre * h0i + lim * h0r + h_im_ref[...]
    y = _ssm_out_proj(h_re_ref, h_im_ref, wc_re_ref, wc_imn_ref)
    n = x.shape[0]
    out_ref[0:n, :] = _ssm_glu(x, u, y, d_ref, wglu_ref)
    out_ref[n:, :] = jnp.zeros((out_ref.shape[0] - n, out_ref.shape[1]), F32)


def _ssm_prompt(x, g, lam_re, lam_im, wb_re, wb_im, wc_re, wc_imn, d_skip, wglu, *, lc, nsub):
    bsz, seq, d = x.shape
    assert seq % (lc * nsub) == 0
    gp = lam_re.shape[1]
    r = lc * bsz
    perm = np.zeros((r, r), np.float32)
    for b in range(bsz):
        for t in range(lc):
            perm[t * bsz + b, b * lc + t] = 1.0
    permt = jnp.asarray(perm.T, BF16)
    perm = jnp.asarray(perm, BF16)
    kern = functools.partial(_ssm_prompt_kernel, lc=lc, bsz=bsz, lane_chunk=8 * LANES)
    return pl.pallas_call(
        kern,
        grid=(seq // (lc * nsub),),
        in_specs=[pl.BlockSpec((bsz, lc * nsub, d), lambda c: (0, c, 0)),
                  _const_spec(g.shape), _const_spec(lam_re.shape), _const_spec(lam_im.shape),
                  _const_spec(wb_re.shape), _const_spec(wb_im.shape),
                  _const_spec(wc_re.shape), _const_spec(wc_imn.shape),
                  _const_spec(d_skip.shape), _const_spec(wglu.shape),
                  _const_spec(perm.shape), _const_spec(permt.shape)],
        out_specs=[pl.BlockSpec((bsz, lc * nsub, d), lambda c: (0, c, 0)),
                   _const_spec((bsz, gp)), _const_spec((bsz, gp))],
        out_shape=[jax.ShapeDtypeStruct((bsz, seq, d), F32),
                   jax.ShapeDtypeStruct((bsz, gp), F32), jax.ShapeDtypeStruct((bsz, gp), F32)],
        scratch_shapes=[pltpu.VMEM((nsub, r, gp), F32), pltpu.VMEM((nsub, r, gp), F32)],
        compiler_params=_params("arbitrary"),
        name="s5_prompt",
    )(x, g, lam_re, lam_im, wb_re, wb_im, wc_re, wc_imn, d_skip, wglu, perm, permt)


def _ssm_sample(x, g, lam_re, lam_im, h0_re, h0_im, wb_re, wb_im, wc_re, wc_imn, d_skip, wglu, *, pad_rows):
    n, d = x.shape
    gp = lam_re.shape[1]
    return pl.pallas_call(
        _ssm_sample_kernel,
        out_shape=[jax.ShapeDtypeStruct((pad_rows, d), F32),
                   jax.ShapeDtypeStruct((n, gp), F32), jax.ShapeDtypeStruct((n, gp), F32)],
        compiler_params=_params(),
        name="s5_sample",
    )(x, g, lam_re, lam_im, h0_re, h0_im, wb_re, wb_im, wc_re, wc_imn, d_skip, wglu)


def _ffn_chunks(d_ff):
    step = 3 * MXU_DIM
    return [(c, min(c + step, d_ff)) for c in range(0, d_ff, step)]


def _ffn_kernel(xp_ref, xs_ref, gffn_ref, wgu_ref, wd_ref, gkv_ref, wkv_ref, gq_ref, wq_ref,
                x2_ref, kv_ref, q_ref):
    x = jnp.where(pl.program_id(0) < pl.num_programs(0) - 1, xp_ref[...], xs_ref[...])
    hb = _rms(x, gffn_ref[...]).astype(BF16)
    d_ff = wd_ref.shape[0]
    acc = None
    for c0, c1 in _ffn_chunks(d_ff):
        a = _dot(hb, wgu_ref[:, c0:c1])
        b = _dot(hb, wgu_ref[:, d_ff + c0:d_ff + c1])
        part = _dot((jax.nn.silu(a) * b).astype(BF16), wd_ref[c0:c1, :])
        acc = part if acc is None else acc + part
    x2 = x + acc
    x2_ref[...] = x2
    kv_ref[...] = _dot(_rms(x2, gkv_ref[...]).astype(BF16), wkv_ref[...])
    q_ref[...] = _dot(_rms(x2, gq_ref[...]).astype(BF16), wq_ref[...]).astype(BF16)


def _ffn(xp, xs, gffn, wgu, wd, gkv, wkv, gq, wq):
    tm, d = xs.shape
    npt = xp.shape[0] // tm
    rows = xp.shape[0] + tm
    kvw = wkv.shape[1]
    nq = wq.shape[1]
    return pl.pallas_call(
        _ffn_kernel,
        grid=(npt + 1,),
        in_specs=[pl.BlockSpec((tm, d), lambda i: (jnp.minimum(i, npt - 1), 0)),
                  _const_spec(xs.shape),
                  _const_spec(gffn.shape), _const_spec(wgu.shape), _const_spec(wd.shape),
                  _const_spec(gkv.shape), _const_spec(wkv.shape),
                  _const_spec(gq.shape), _const_spec(wq.shape)],
        out_specs=[pl.BlockSpec((tm, d), lambda i: (i, 0)),
                   pl.BlockSpec((tm, kvw), lambda i: (i, 0)),
                   pl.BlockSpec((tm, nq), lambda i: (i, 0))],
        out_shape=[jax.ShapeDtypeStruct((rows, d), F32),
                   jax.ShapeDtypeStruct((rows, kvw), F32),
                   jax.ShapeDtypeStruct((rows, nq), BF16)],
        compiler_params=_params("arbitrary"),
        name="ffn_kv_q",
    )(xp, xs, gffn, wgu, wd, gkv, wkv, gq, wq)


def _t5_bucket(dist, num_buckets):
    max_exact = num_buckets // 2
    d = jnp.maximum(dist, 0)
    large = max_exact + (jnp.log(jnp.maximum(d, 1).astype(F32) / max_exact)
                         / math.log(MAX_DISTANCE / max_exact) * (num_buckets - max_exact)).astype(jnp.int32)
    large = jnp.minimum(large, num_buckets - 1)
    return jnp.where(d < max_exact, d, large)


def _bias_kernel(bm_ref, bs_ref, rb_ref, bias_ref, bias_s_ref, *, kvh, rep):
    nb, nh = rb_ref.shape
    bm = bm_ref[...]
    bs = bs_ref[...]

    def lookup(buckets, h):
        terms = [jnp.where(buckets == k, rb_ref[k, h], 0.0) for k in range(nb)]
        while len(terms) > 1:
            terms = [a + b for a, b in zip(terms[0::2], terms[1::2])] + (terms[-1:] if len(terms) % 2 else [])
        return terms[0]

    w = bm.shape[0]
    for g in range(kvh):
        for r in range(rep):
            bias_ref[g, :, r * w:(r + 1) * w] = lookup(bm, g * rep + r) * math.log2(math.e)
            bias_s_ref[g, r:r + 1, :] = lookup(bs, g * rep + r)


def _bias_tables(rel_bias, window, kvh, rep):
    nb, nh = rel_bias.shape
    key = jnp.arange(window)[:, None]
    qry = jnp.arange(window)[None, :]
    bm = _t5_bucket((qry - key) % window, nb).astype(jnp.int32)
    bs = _t5_bucket(window - 1 - qry, nb).astype(jnp.int32)
    kern = functools.partial(_bias_kernel, kvh=kvh, rep=rep)
    return pl.pallas_call(
        kern,
        in_specs=[pl.BlockSpec(memory_space=pltpu.VMEM), pl.BlockSpec(memory_space=pltpu.VMEM),
                  pl.BlockSpec(memory_space=pltpu.SMEM)],
        out_shape=[jax.ShapeDtypeStruct((kvh, window, rep * window), F32),
                   jax.ShapeDtypeStruct((kvh, rep, window), F32)],
        name="t5_bias",
    )(bm, bs, rel_bias)


def _attn_prompt_kernel(q_ref, kvc_ref, kvp_ref, bias_ref, sink_ref, ot_ref, *, kvh, rep, hd, scale):
    w = kvp_ref.shape[0]
    kv_all = kvc_ref[...]
    for sub in range(q_ref.shape[0] // w):
        rows = slice(sub * w, (sub + 1) * w)
        kvp = kvp_ref[...] if sub == 0 else kv_all[(sub - 1) * w:sub * w]
        first = pl.program_id(1) == 0 if sub == 0 else None
        _attn_prompt_block(q_ref[rows, :], kv_all[rows], kvp, first, bias_ref, sink_ref, ot_ref, sub * w,
                           kvh=kvh, rep=rep, hd=hd, scale=scale)


def _attn_prompt_block(q, kvc, kvp, first, bias_ref, sink_ref, ot_ref, col0, *, kvh, rep, hd, scale):
    w = q.shape[0]
    kvw = kvh * hd
    lane = lax.broadcasted_iota(jnp.int32, (1, kvw), 1)
    key = lax.broadcasted_iota(jnp.int32, (w, w), 0)
    qry = lax.broadcasted_iota(jnp.int32, (w, w), 1)
    upper = key > qry
    mask_add = None if first is None else jnp.where(jnp.logical_and(upper, first), NEG_INF, 0.0)
    log2e = math.log2(math.e)
    heads_per_blk = LANES // hd
    lane_blocks = kvw // LANES
    kbs = []
    for half in range(heads_per_blk):
        lmask = (lane % LANES) // hd == half
        kbs += [jnp.where(lmask, kvp[:, :kvw], 0.0), jnp.where(lmask, kvc[:, :kvw], 0.0)]
    qrows = []
    for p in range(lane_blocks):
        qg = jnp.concatenate([q[:, r * kvw + p * LANES:r * kvw + (p + 1) * LANES] for r in range(rep)], axis=0)
        zero = jnp.zeros_like(qg)
        qrows.append(jnp.concatenate([qg if other == p else zero for other in range(lane_blocks)], axis=1))
    s_full = lax.dot_general(jnp.concatenate(kbs, axis=0).astype(BF16), jnp.concatenate(qrows, axis=0),
                             (((1,), (1,)), ((), ())), preferred_element_type=F32)
    for p in range(lane_blocks):
        vs = slice(kvw + p * LANES, kvw + (p + 1) * LANES)
        vb = jnp.concatenate([kvp[:, vs], kvc[:, vs]], axis=0).astype(BF16)
        s_all = s_full[:, p * rep * w:(p + 1) * rep * w]
        pcats, denoms = [], []
        for half in range(heads_per_blk):
            g = p * heads_per_blk + half
            pcols, dcols = [], []
            for r in range(rep):
                cs = slice(r * w, (r + 1) * w)
                s = s_all[half * 2 * w:(half + 1) * 2 * w, cs]
                bias = bias_ref[g, :, cs] if first is None else bias_ref[g, :, cs] + mask_add
                sc = jnp.where(upper, s[:w], s[w:]) * (scale * log2e) + bias
                sink = sink_ref[g * rep + r] * log2e
                m = jnp.maximum(jnp.max(sc, axis=0, keepdims=True), sink)
                pe = jnp.exp2(sc - m)
                dcols.append(jnp.sum(pe, axis=0, keepdims=True) + jnp.exp2(sink - m))
                pcols.append(jnp.concatenate([jnp.where(upper, pe, 0.0), jnp.where(upper, 0.0, pe)],
                                             axis=0).astype(BF16))
            denoms.append(jnp.concatenate(dcols, axis=1))
            pcats.append(jnp.concatenate(pcols, axis=1))
        og_all = lax.dot_general(vb, jnp.concatenate(pcats, axis=1), (((0,), (0,)), ((), ())),
                                 preferred_element_type=F32)
        halves = [og_all[half * hd:(half + 1) * hd, half * rep * w:(half + 1) * rep * w] / denoms[half]
                  for half in range(heads_per_blk)]
        o_blk = jnp.concatenate(halves, axis=0).astype(BF16)
        for r in range(rep):
            ot_ref[r * kvw + p * LANES:r * kvw + (p + 1) * LANES, col0:col0 + w] = o_blk[:, r * w:(r + 1) * w]


def _attn_prompt(q, kv, bias, sinks, *, bsz, nblk, window, kvh, rep, hd):
    rows, nq = q.shape
    kvw2 = kv.shape[1]
    nsub = next(n for n in (4, 2, 1) if nblk % n == 0 and rows % (n * window) == 0)
    nstep = nblk // nsub
    kern = functools.partial(_attn_prompt_kernel, kvh=kvh, rep=rep, hd=hd, scale=1.0 / math.sqrt(hd))
    return pl.pallas_call(
        kern,
        grid=(bsz, nstep),
        in_specs=[pl.BlockSpec((nsub * window, nq), lambda b, i: (b * nstep + i, 0)),
                  pl.BlockSpec((nsub * window, kvw2), lambda b, i: (b * nstep + i, 0)),
                  pl.BlockSpec((window, kvw2), lambda b, i: (b * nblk + jnp.maximum(nsub * i - 1, 0), 0)),
                  _const_spec(bias.shape),
                  pl.BlockSpec(memory_space=pltpu.SMEM)],
        out_specs=pl.BlockSpec((nq, nsub * window), lambda b, i: (0, b * nstep + i)),
        out_shape=jax.ShapeDtypeStruct((nq, rows), BF16),
        compiler_params=_params("arbitrary", "arbitrary"),
        name="swa_prompt",
    )(q, kv, kv, bias, sinks)


def _attn_sample_kernel(q_ref, kv_ref, ck_ref, cv_ref, bias_ref, sink_ref, o_ref, nk_ref, nv_ref,
                        *, kvh, rep, hd, scale):
    nb, w, kvw = ck_ref.shape
    kv = kv_ref[...]
    lane = lax.broadcasted_iota(jnp.int32, (nb, rep, kvw), 2)

    def shifted(c_ref, n_ref, new):
        flat = c_ref[...].reshape(nb * w, kvw)
        n_ref[...] = pltpu.roll(flat, nb * w - 1, axis=0).reshape(nb, w, kvw)
        n_ref[:, w - 1:w, :] = new
        return n_ref[...].astype(BF16)

    nkb = shifted(ck_ref, nk_ref, kv[:, :, :kvw])
    nvb = shifted(cv_ref, nv_ref, kv[:, :, kvw:])
    q = q_ref[...].astype(F32)
    masks = [jnp.logical_and(lane >= g * hd, lane < (g + 1) * hd) for g in range(kvh)]
    qall = jnp.concatenate([jnp.where(masks[g], q, 0.0) for g in range(kvh)], axis=1).astype(BF16)
    s = jnp.einsum("nmc,njc->nmj", qall, nkb, preferred_element_type=F32)
    sc = s * scale + bias_ref[...][None]
    sink = sink_ref[...][None]
    m = jnp.maximum(jnp.max(sc, axis=-1, keepdims=True), sink)
    pe = jnp.exp(sc - m)
    probs = pe / (jnp.sum(pe, axis=-1, keepdims=True) + jnp.exp(sink - m))
    og = jnp.einsum("nmj,njc->nmc", probs.astype(BF16), nvb, preferred_element_type=F32)
    o = jnp.zeros((nb, rep, kvw), F32)
    for g in range(kvh):
        o = jnp.where(masks[g], og[:, g * rep:(g + 1) * rep, :], o)
    o_ref[...] = o.astype(BF16)


def _attn_sample(q3, kv3, ck, cv, bias_s, sink_s, *, nb, kvh, rep, hd):
    n, w, kvw = ck.shape
    kern = functools.partial(_attn_sample_kernel, kvh=kvh, rep=rep, hd=hd, scale=1.0 / math.sqrt(hd))
    cache_spec = pl.BlockSpec((nb, w, kvw), lambda i: (i, 0, 0))
    return pl.pallas_call(
        kern,
        grid=(n // nb,),
        in_specs=[pl.BlockSpec((nb, rep, kvw), lambda i: (i, 0, 0)),
                  pl.BlockSpec((nb, 1, 2 * kvw), lambda i: (i, 0, 0)),
                  cache_spec, cache_spec,
                  _const_spec(bias_s.shape), _const_spec(sink_s.shape)],
        out_specs=[pl.BlockSpec((nb, rep, kvw), lambda i: (i, 0, 0)), cache_spec, cache_spec],
        out_shape=[jax.ShapeDtypeStruct((n, rep, kvw), BF16),
                   jax.ShapeDtypeStruct((n, w, kvw), F32), jax.ShapeDtypeStruct((n, w, kvw), F32)],
        compiler_params=_params("arbitrary"),
        name="swa_sample",
    )(q3, kv3, ck, cv, bias_s, sink_s)


CHUNK = 16
BIG_COPY = 128
SUB = 256
UP_ROW_BLOCK = 1024
DOWN_ROW_BLOCK = 1024
TOKEN_TILE = 512
MOE_TILE = 512


def _chunk_copy(src, src_row, dst, dst_row, sem, rows=CHUNK):
    return pltpu.make_async_copy(src.at[pl.ds(pl.multiple_of(src_row, CHUNK), rows), :],
                                 dst.at[pl.ds(pl.multiple_of(dst_row, CHUNK), rows), :], sem)


def _start_copies(src, src_row, dst, dst_row, sem, nrows):
    nbig = lax.shift_right_logical(nrows, int(math.log2(BIG_COPY)))

    def big(c, carry):
        _chunk_copy(src, src_row + c * BIG_COPY, dst, dst_row + c * BIG_COPY, sem, BIG_COPY).start()
        return carry

    lax.fori_loop(0, nbig, big, 0)
    done = nbig * BIG_COPY

    def small(c, carry):
        _chunk_copy(src, src_row + done + c * CHUNK, dst, dst_row + done + c * CHUNK, sem).start()
        return carry

    lax.fori_loop(0, lax.shift_right_logical(nrows - done, int(math.log2(CHUNK))), small, 0)


def _wait_chunks(src, dst, sem, nchunks):
    per_big = BIG_COPY // CHUNK
    nbig = lax.shift_right_logical(nchunks, int(math.log2(per_big)))

    def big(c, carry):
        _chunk_copy(src, 0, dst, 0, sem, BIG_COPY).wait()
        return carry

    lax.fori_loop(0, nbig, big, 0)

    def small(c, carry):
        _chunk_copy(src, 0, dst, 0, sem).wait()
        return carry

    lax.fori_loop(0, nchunks - nbig * per_big, small, 0)


def _moe_route_kernel(x2_ref, ot_ref, wo_ref, gffn_ref, wr_ref, lst_ref, ust_ref,
                      x3_ref, cm_ref, tmeta_ref, xs_hbm,
                      comp_s, carry_s, zero_s, base_v, pend_sm, sem, *, ne, cap, n_valid):
    i = pl.program_id(0)
    nt = pl.num_programs(0)
    tm = x2_ref.shape[0]
    w = comp_s.shape[1]
    slot = lax.rem(i, 2)

    @pl.when(i == 0)
    def _():
        base_v[...] = jnp.zeros_like(base_v)
        zero_s[...] = jnp.zeros_like(zero_s)
        carry_s[...] = jnp.zeros_like(carry_s)

    def drain(n):
        _wait_chunks(comp_s.at[0], xs_hbm, sem, n)

    @pl.when(i > 0)
    def _():
        drain(pend_sm[0])

    x3 = x2_ref[...] + lax.dot_general(ot_ref[...], wo_ref[...], (((0,), (0,)), ((), ())),
                                       preferred_element_type=F32)
    x3_ref[...] = x3
    hb = _rms(x3, gffn_ref[...]).astype(BF16)

    lane = lax.broadcasted_iota(jnp.int32, (tm, LANES), 1)
    logits = jnp.where(lane < ne, _dot(hb, wr_ref[...]), -jnp.inf)
    m1 = jnp.max(logits, axis=-1, keepdims=True)
    i1 = jnp.min(jnp.where(logits == m1, lane, LANES), axis=-1, keepdims=True)
    rest = jnp.where(lane == i1, -jnp.inf, logits)
    m2 = jnp.max(rest, axis=-1, keepdims=True)
    i2 = jnp.min(jnp.where(rest == m2, lane, LANES), axis=-1, keepdims=True)
    e2 = jnp.exp(m2 - m1)
    g1 = 1.0 / (1.0 + e2)
    g2 = e2 / (1.0 + e2)

    live = i * tm + lax.broadcasted_iota(jnp.int32, (tm, 1), 0) < n_valid
    sel = jnp.where(jnp.logical_and(live, jnp.logical_or(lane == i1, lane == i2)), 1.0, 0.0)
    rank = _dot(lst_ref[...], sel.astype(BF16))
    cnt = jnp.sum(sel, axis=0, keepdims=True)
    fill = base_v[...]
    rem = fill - jnp.floor(fill / CHUNK) * CHUNK
    cpad = jnp.floor((rem + cnt + (CHUNK - 1)) / CHUNK) * CHUNK
    loff = _dot(jnp.broadcast_to(cpad, (8, LANES)).astype(BF16), ust_ref[...])[0:1]
    dest = loff + rem + rank
    ld1 = jnp.where(live, jnp.sum(jnp.where(lane == i1, dest, 0.0), axis=-1, keepdims=True), -1.0)
    ld2 = jnp.where(live, jnp.sum(jnp.where(lane == i2, dest, 0.0), axis=-1, keepdims=True), -1.0)
    cm = jnp.where(lane == 0, ld1, jnp.where(lane == 1, ld2, jnp.where(lane == 2, g1, jnp.where(lane == 3, g2, 0.0))))
    cm_ref[...] = cm

    rm = cm.T
    rowi = lax.broadcasted_iota(jnp.int32, (w, tm), 0).astype(F32)
    place = (jnp.where(rowi == rm[0:1], 1.0, 0.0) + jnp.where(rowi == rm[1:2], 1.0, 0.0)).astype(BF16)
    comp_s[slot] = _dot(place, hb).astype(BF16)

    base = fill - rem
    srow = lax.broadcasted_iota(jnp.int32, (8, LANES), 0)
    tmeta_ref[...] = jnp.where(srow == 0, cpad, jnp.where(srow == 1, base, jnp.where(srow == 2, fill + cnt, 0.0))
                               ).astype(jnp.int32)
    base_v[...] = fill + cnt

    cpad_i = cpad.astype(jnp.int32)
    loff_i = loff.astype(jnp.int32)
    base_i = base.astype(jnp.int32)
    tail_i = (rem + cnt - jnp.floor((rem + cnt) / CHUNK) * CHUNK).astype(jnp.int32)
    total = 0
    tails = []
    for e in range(ne):
        n_e = cpad_i[0, e]
        src0 = loff_i[0, e]
        dst0 = base_i[0, e] + e * cap

        @pl.when(n_e > 0)
        def _(e=e, n_e=n_e, src0=src0, partial=tail_i[0, e] > 0):
            head = pl.ds(pl.multiple_of(src0, CHUNK), CHUNK)
            comp_s[slot, head, :] = comp_s[slot, head, :] + carry_s[e]
            last = comp_s[slot, pl.ds(pl.multiple_of(src0 + n_e - CHUNK, CHUNK), CHUNK), :]
            carry_s[e] = jnp.where(partial, last, jnp.zeros_like(last))

        _start_copies(comp_s.at[slot], src0, xs_hbm, dst0, sem, n_e)
        total = total + lax.shift_right_logical(n_e, int(math.log2(CHUNK)))
        tails.append(dst0 + n_e)
    pend_sm[0] = total

    @pl.when(i == nt - 1)
    def _():
        nz_total = 0
        for e in range(ne):
            end = tails[e]
            nz = lax.shift_right_logical(lax.rem(SUB - lax.rem(end, SUB), SUB), int(math.log2(CHUNK)))

            def zstart(c, carry, end=end):
                _chunk_copy(zero_s, 0, xs_hbm, end + c * CHUNK, sem).start()
                return carry

            lax.fori_loop(0, nz, zstart, 0)
            nz_total = nz_total + nz
        drain(total + nz_total)


def _moe_route(x2, ot, wo, gffn, wr_pad, *, tm, ne, cap, n_valid):
    rows, d = x2.shape
    nt = rows // tm
    w = _staging_rows(tm, ne)
    lst = jnp.asarray(np.tril(np.ones((tm, tm), np.float32), -1), BF16)
    ust = jnp.asarray(np.triu(np.ones((LANES, LANES), np.float32), 1), BF16)
    kern = functools.partial(_moe_route_kernel, ne=ne, cap=cap, n_valid=n_valid)
    return pl.pallas_call(
        kern,
        grid=(nt,),
        in_specs=[pl.BlockSpec((tm, d), lambda i: (i, 0)),
                  pl.BlockSpec((ot.shape[0], tm), lambda i: (0, i)),
                  _const_spec(wo.shape), _const_spec(gffn.shape), _const_spec(wr_pad.shape),
                  _const_spec(lst.shape), _const_spec(ust.shape)],
        out_specs=[pl.BlockSpec((tm, d), lambda i: (i, 0)),
                   pl.BlockSpec((tm, LANES), lambda i: (i, 0)),
                   pl.BlockSpec((8, LANES), lambda i: (i, 0)),
                   pl.BlockSpec(memory_space=pl.ANY)],
        out_shape=[jax.ShapeDtypeStruct((rows, d), F32),
                   jax.ShapeDtypeStruct((rows, LANES), F32),
                   jax.ShapeDtypeStruct((nt * 8, LANES), jnp.int32),
                   jax.ShapeDtypeStruct((ne * cap, d), BF16)],
        scratch_shapes=[pltpu.VMEM((2, w, d), BF16), pltpu.VMEM((ne, CHUNK, d), BF16), pltpu.VMEM((CHUNK, d), BF16),
                        pltpu.VMEM((1, LANES), F32), pltpu.SMEM((1,), jnp.int32),
                        pltpu.SemaphoreType.DMA(())],
        compiler_params=_params("arbitrary"),
        name="moe_route",
    )(x2, ot, wo, gffn, wr_pad, lst, ust)


def _staging_rows(tm, ne):
    return -(-(TOP_K * tm + 2 * ne * (CHUNK - 1)) // LANES) * LANES


def _max_blocks(rows, ne, row_block):
    return TOP_K * rows // row_block + ne


def _expert_steps(nrows, nf, max_blocks, row_block):
    ne = nrows.shape[0]
    nblk = (nrows + row_block - 1) // row_block
    cum = jnp.cumsum(nblk) * nf
    total = cum[-1]
    s = jnp.minimum(jnp.arange(max_blocks * nf, dtype=jnp.int32), total - 1)
    e = jnp.minimum(jnp.sum((s[:, None] >= cum[None, :]).astype(jnp.int32), axis=1), ne - 1)
    nb_e = nblk[e]
    within = s - (cum[e] - nb_e * nf)
    f = within // nb_e
    pos = within - f * nb_e
    r = (pos + nb_e - 1) % nb_e
    nsub = jnp.clip((nrows[e] - r * row_block + SUB - 1) // SUB, 0, row_block // SUB)
    first = (pos == 0).astype(jnp.int32)
    return (e, f.astype(jnp.int32), r.astype(jnp.int32), first, nsub.astype(jnp.int32),
            jnp.reshape(total, (1,)).astype(jnp.int32))


def _moe_up_kernel(e_ref, f_ref, r_ref, first_ref, nsub_ref, n_ref, xs_ref, wg_ref, wu_ref, act_ref, wgb_s, wub_s):
    s = pl.program_id(0)

    @pl.when(s < n_ref[0])
    def _():
        @pl.when(first_ref[s] == 1)
        def _():
            wgb_s[...] = wg_ref[...].astype(BF16)
            wub_s[...] = wu_ref[...].astype(BF16)

        def gate_up(row0, nrows):
            rows = pl.ds(pl.multiple_of(row0, SUB), nrows)
            xsb = xs_ref[rows, :]
            a = _dot(xsb, wgb_s[...])
            b = _dot(xsb, wub_s[...])
            act_ref[rows, :] = (jax.nn.silu(a) * b).astype(BF16)

        nsub = nsub_ref[s]
        full = xs_ref.shape[0] // SUB

        @pl.when(nsub == full)
        def _():
            xsb = xs_ref[...]
            for c0 in range(0, act_ref.shape[1], MXU_DIM):
                cols = slice(c0, c0 + MXU_DIM)
                a = _dot(xsb, wgb_s[:, cols])
                b = _dot(xsb, wub_s[:, cols])
                act_ref[:, cols] = (jax.nn.silu(a) * b).astype(BF16)

        @pl.when(nsub < full)
        def _():
            npair = lax.shift_right_logical(nsub, 1)

            def body(j, carry):
                gate_up(j * (2 * SUB), 2 * SUB)
                return carry

            lax.fori_loop(0, npair, body, 0)

            @pl.when(nsub > 2 * npair)
            def _():
                gate_up(npair * (2 * SUB), SUB)


def _moe_up(steps, xs, wgu, *, fc, cap, row_block):
    ne, d, dff2 = wgu.shape
    dff = dff2 // 2
    nf = dff // fc
    cb = cap // row_block
    grid_spec = pltpu.PrefetchScalarGridSpec(
        num_scalar_prefetch=6,
        grid=(steps[0].shape[0],),
        in_specs=[pl.BlockSpec((row_block, d), lambda s, e, f, r, *_: (e[s] * cb + r[s], 0)),
                  pl.BlockSpec((None, d, fc), lambda s, e, f, r, *_: (e[s], 0, f[s])),
                  pl.BlockSpec((None, d, fc), lambda s, e, f, r, *_: (e[s], 0, nf + f[s]))],
        out_specs=pl.BlockSpec((row_block, fc), lambda s, e, f, r, *_: (e[s] * cb + r[s], f[s])),
        scratch_shapes=[pltpu.VMEM((d, fc), BF16), pltpu.VMEM((d, fc), BF16)])
    return pl.pallas_call(
        _moe_up_kernel,
        grid_spec=grid_spec,
        out_shape=jax.ShapeDtypeStruct((ne * cap, dff), BF16),
        compiler_params=_params("arbitrary"),
        name="moe_up",
    )(*steps, xs, wgu, wgu)


def _moe_down_kernel(e_ref, f_ref, r_ref, first_ref, nsub_ref, n_ref, act_ref, wd_ref, ys_ref, wdb_s):
    s = pl.program_id(0)

    @pl.when(s < n_ref[0])
    def _():
        @pl.when(first_ref[s] == 1)
        def _():
            wdb_s[...] = wd_ref[...].astype(BF16)

        def project(row0, nrows):
            rows = pl.ds(pl.multiple_of(row0, SUB), nrows)
            ys_ref[rows, :] = _dot(act_ref[rows, :], wdb_s[...]).astype(BF16)

        nsub = nsub_ref[s]
        full = act_ref.shape[0] // SUB

        @pl.when(nsub == full)
        def _():
            ab = act_ref[...]
            for c0 in range(0, ys_ref.shape[1], MXU_DIM):
                cols = slice(c0, c0 + MXU_DIM)
                ys_ref[:, cols] = _dot(ab, wdb_s[:, cols]).astype(BF16)

        @pl.when(nsub < full)
        def _():
            npair = lax.shift_right_logical(nsub, 1)

            def body(j, carry):
                project(j * (2 * SUB), 2 * SUB)
                return carry

            lax.fori_loop(0, npair, body, 0)

            @pl.when(nsub > 2 * npair)
            def _():
                project(npair * (2 * SUB), SUB)


def _moe_down(steps, act, wd, *, cap, row_block):
    ne, dff, d = wd.shape
    cb = cap // row_block
    grid_spec = pltpu.PrefetchScalarGridSpec(
        num_scalar_prefetch=6,
        grid=(steps[0].shape[0],),
        in_specs=[pl.BlockSpec((row_block, dff), lambda s, e, f, r, *_: (e[s] * cb + r[s], 0)),
                  pl.BlockSpec((None, dff, d), lambda s, e, f, r, *_: (e[s], 0, 0))],
        out_specs=pl.BlockSpec((row_block, d), lambda s, e, f, r, *_: (e[s] * cb + r[s], 0)),
        scratch_shapes=[pltpu.VMEM((dff, d), BF16)])
    return pl.pallas_call(
        _moe_down_kernel,
        grid_spec=grid_spec,
        out_shape=jax.ShapeDtypeStruct((ne * cap, d), BF16),
        compiler_params=_params("arbitrary", vmem=2 * (dff * d * 4 + row_block * (dff + d) * 2) + dff * d * 2
                                + 4 * SUB * d * 4),
        name="moe_down",
    )(*steps, act, wd)


def _moe_combine_kernel(cpad_ref, seg_ref, x3_ref, cm_ref, gfin_ref, ys_hbm, yp_ref, ys_ref, yloc_s, sem,
                        *, ne, npt):
    i = pl.program_id(0)
    nt = pl.num_programs(0)
    tm = x3_ref.shape[0]
    w = yloc_s.shape[1]
    slot = lax.rem(i, 2)
    shift = int(math.log2(CHUNK))

    def issue(t, sl):
        off = 0
        for e in range(ne):
            n_e = cpad_ref[t * ne + e]
            _start_copies(ys_hbm, seg_ref[t * ne + e], yloc_s.at[sl], off, sem.at[sl], n_e)
            off = off + n_e

    @pl.when(i == 0)
    def _():
        yloc_s[...] = jnp.zeros_like(yloc_s)
        issue(0, 0)

    @pl.when(i + 1 < nt)
    def _():
        issue(i + 1, 1 - slot)

    total = 0
    for e in range(ne):
        total = total + lax.shift_right_logical(cpad_ref[i * ne + e], shift)

    _wait_chunks(ys_hbm, yloc_s.at[slot], sem.at[slot], total)

    cm = cm_ref[...]
    col = lax.broadcasted_iota(jnp.int32, (tm, w), 1).astype(F32)
    yl = yloc_s[slot]
    mix = jnp.where(col == cm[:, 0:1], cm[:, 2:3], jnp.where(col == cm[:, 1:2], cm[:, 3:4], 0.0))
    y = _rms(x3_ref[...] + _dot(mix.astype(BF16), yl), gfin_ref[...])

    @pl.when(i < npt)
    def _():
        yp_ref[...] = y

    @pl.when(i >= npt)
    def _():
        ys_ref[...] = y


def _moe_combine(cpad, seg, x3, cm, gfin, ys, *, tm, ne, sample_rows):
    rows, d = x3.shape
    nt = rows // tm
    npt = (rows - sample_rows) // tm
    w = _staging_rows(tm, ne)
    kern = functools.partial(_moe_combine_kernel, ne=ne, npt=npt)
    grid_spec = pltpu.PrefetchScalarGridSpec(
        num_scalar_prefetch=2,
        grid=(nt,),
        in_specs=[pl.BlockSpec((tm, d), lambda i, *_: (i, 0)),
                  pl.BlockSpec((tm, LANES), lambda i, *_: (i, 0)),
                  pl.BlockSpec(gfin.shape, lambda i, *_: (0, 0)),
                  pl.BlockSpec(memory_space=pl.ANY)],
        out_specs=[pl.BlockSpec((tm, d), lambda i, *_: (jnp.minimum(i, npt - 1), 0)),
                   pl.BlockSpec((tm, d), lambda i, *_: (jnp.maximum(i - npt, 0), 0))],
        scratch_shapes=[pltpu.VMEM((2, w, d), BF16), pltpu.SemaphoreType.DMA((2,))])
    return pl.pallas_call(
        kern,
        grid_spec=grid_spec,
        out_shape=[jax.ShapeDtypeStruct((npt * tm, d), F32), jax.ShapeDtypeStruct((sample_rows, d), F32)],
        compiler_params=_params("arbitrary"),
        name="moe_combine",
    )(cpad, seg, x3, cm, gfin, ys)


def _moe(x2, ot, wo, gffn, wr, wgu, wd, gfin, *, tm, fc, n_valid, sample_rows):
    rows, d = x2.shape
    ne = wr.shape[1]
    nt = rows // tm
    nf = wgu.shape[2] // 2 // fc
    cap = -(-(rows + SUB) // UP_ROW_BLOCK) * UP_ROW_BLOCK
    wr_pad = jnp.pad(wr, ((0, 0), (0, LANES - ne))).astype(BF16)
    x3, cm, tmeta, xs = _moe_route(x2, ot, wo, gffn, wr_pad, tm=tm, ne=ne, cap=cap, n_valid=n_valid)
    tmeta = tmeta.reshape(nt, 8, LANES)[:, :, :ne]
    cpad = tmeta[:, 0, :].reshape(nt * ne)
    seg = (tmeta[:, 1, :] + jnp.arange(ne, dtype=jnp.int32)[None, :] * cap).reshape(nt * ne)
    nrows = tmeta[nt - 1, 2, :]
    up_steps = _expert_steps(nrows, nf, _max_blocks(rows, ne, UP_ROW_BLOCK), UP_ROW_BLOCK)
    act = _moe_up(up_steps, xs, wgu, fc=fc, cap=cap, row_block=UP_ROW_BLOCK)
    down_steps = _expert_steps(nrows, 1, _max_blocks(rows, ne, DOWN_ROW_BLOCK), DOWN_ROW_BLOCK)
    ys = _moe_down(down_steps, act, wd, cap=cap, row_block=DOWN_ROW_BLOCK)
    return _moe_combine(cpad, seg, x3, cm, gfin, ys, tm=tm, ne=ne, sample_rows=sample_rows)


def kernel(x_prompt, x_sample, state_ssm_re, state_ssm_im, cache_k_win, cache_v_win, g_mix, g_ffn, g_kv, g_final, ssm_a_re, ssm_a_im, ssm_log_dt, ssm_b_re, ssm_b_im, ssm_c_re, ssm_c_im, ssm_d, w_glu, w_kv, w_q, w_o, attn_sinks, rel_bias, w_ffn_gate_up, w_ffn_down, w_router, w_exp_gate_up, w_exp_down):
    bsz, seq, d = x_prompt.shape
    ns, dec_seq, _ = x_sample.shape
    assert dec_seq == 1 and g_mix.shape[0] == 2 and ssm_a_re.shape[0] == 1 and w_q.shape[0] == 1
    _, g, p = ssm_a_re.shape
    gp = g * p
    window, kvh, hd = cache_k_win.shape[1:]
    kvw = kvh * hd
    nh = attn_sinks.shape[1]
    rep = nh // kvh
    nq = nh * hd
    assert bsz == 8 and ns % bsz == 0 and seq % window == 0 and LANES % hd == 0

    lam_re, lam_im, wb_re, wb_im, wc_re, wc_imn = _zoh(ssm_a_re[0], ssm_a_im[0], ssm_log_dt[0],
                                                        ssm_b_re[0], ssm_b_im[0], ssm_c_re[0], ssm_c_im[0])
    wglu = w_glu[0].astype(BF16)
    d_skip = ssm_d[0].reshape(1, d)
    wgu = w_ffn_gate_up[0].astype(BF16)
    wd = w_ffn_down[0].astype(BF16)
    wkv = w_kv.astype(BF16)
    wq = w_q[0].reshape(d, kvh, rep, hd).transpose(0, 2, 1, 3).reshape(d, nq).astype(BF16)
    wo = w_o[0].reshape(kvh, rep, hd, d).transpose(1, 0, 2, 3).reshape(nq, d).astype(BF16)
    bias, bias_s = _bias_tables(rel_bias, window, kvh, rep)
    bias_s = bias_s.reshape(nh, window)
    sinks = attn_sinks[0]
    sink_s = sinks.reshape(nh, 1)

    tm = TOKEN_TILE
    npr = seq * bsz
    assert npr % tm == 0 and ns <= tm

    x1_p, st_re, st_im = _ssm_prompt(x_prompt, g_mix[0:1], lam_re, lam_im, wb_re, wb_im, wc_re, wc_imn,
                                     d_skip, wglu, lc=32, nsub=2)
    x1_s, hs_re, hs_im = _ssm_sample(x_sample.reshape(ns, d), g_mix[0:1], lam_re, lam_im,
                                     state_ssm_re[0].reshape(ns, gp), state_ssm_im[0].reshape(ns, gp),
                                     wb_re, wb_im, wc_re, wc_imn, d_skip, wglu, pad_rows=tm)

    x2, kv, q = _ffn(x1_p.reshape(npr, d), x1_s, g_ffn[0:1], wgu, wd, g_kv.reshape(1, d), wkv, g_mix[1:2], wq)

    ot = _attn_prompt(q, kv, bias, sinks, bsz=bsz, nblk=seq // window, window=window, kvh=kvh, rep=rep, hd=hd)
    o_s, nk_s, nv_s = _attn_sample(q[npr:npr + ns].reshape(ns, rep, kvw), kv[npr:npr + ns].reshape(ns, 1, 2 * kvw),
                                   cache_k_win.reshape(ns, window, kvw), cache_v_win.reshape(ns, window, kvw),
                                   bias_s, sink_s, nb=32, kvh=kvh, rep=rep, hd=hd)
    ot = lax.dynamic_update_slice(ot, jnp.pad(o_s.reshape(ns, nq).T, ((0, 0), (0, tm - ns))), (0, npr))

    y_p, y_s = _moe(x2, ot, wo, g_ffn[1:2], w_router[0], w_exp_gate_up[0], w_exp_down[0], g_final.reshape(1, d),
                    tm=MOE_TILE, fc=w_exp_down.shape[2] // 2, n_valid=npr + ns, sample_rows=tm)

    y_prompt = y_p.reshape(bsz, seq, d)
    y_sample = y_s[:ns].reshape(ns, 1, d)
    kv_tail = jnp.stack([kv[(b + 1) * seq - window:(b + 1) * seq] for b in range(bsz)])
    kv_tail = kv_tail.reshape(bsz, window, 2, kvh, hd).transpose(2, 0, 1, 3, 4)
    return (y_prompt, y_sample,
            st_re.reshape(1, bsz, g, p), st_im.reshape(1, bsz, g, p), kv_tail[0], kv_tail[1],
            hs_re.reshape(1, ns, g, p), hs_im.reshape(1, ns, g, p),
            nk_s.reshape(ns, window, kvh, hd), nv_s.reshape(ns, window, kvh, hd))
```
